```python
import jax
import jax.numpy as jnp
from jax import lax
import numpy as np

D_MODEL = 1024
BATCH = 8
SEQ = 8192
DEPTH = 1

HEAD_DIM = 64
N_HEADS = D_MODEL // HEAD_DIM
N_HEADS_SB = N_HEADS // 2
N_HEADS_DIL = N_HEADS - N_HEADS_SB
D_SB = N_HEADS_SB * HEAD_DIM
D_DIL = N_HEADS_DIL * HEAD_DIM
D_IN = 3 * D_SB + 3 * D_DIL
D_FF = 2816
DILATED_PATTERNS = ((128, 1), (512, 4), (2048, 16))
BLOCK = 128
ROPE_THETA = 10000.0
RMS_EPS = 1e-6
HALF_STEP = 0.5

kernel_name = 'hybrid_stickbreak_dilated_macaron'


def rmsnorm(x, gain):
    xf = x.astype(jnp.float32)
    y = xf * lax.rsqrt(jnp.mean(xf * xf, axis=-1, keepdims=True) + RMS_EPS)
    return (y * gain.astype(jnp.float32)).astype(x.dtype)


def swiglu(x, w_gate, w_up, w_down):
    return (jax.nn.silu(x @ w_gate) * (x @ w_up)) @ w_down


def rotary(t, positions):
    half = t.shape[-1] // 2
    inv_freq = ROPE_THETA ** (-jnp.arange(half, dtype=jnp.float32) / half)
    ang = positions.astype(jnp.float32)[:, None] * inv_freq[None, :]
    cos, sin = jnp.cos(ang), jnp.sin(ang)
    tf = t.astype(jnp.float32)
    t1, t2 = tf[..., :half], tf[..., half:]
    return jnp.concatenate([t1 * cos - t2 * sin, t2 * cos + t1 * sin], axis=-1).astype(t.dtype)


def split_heads(t, n_heads):
    b, s, _ = t.shape
    return t.reshape(b, s, n_heads, HEAD_DIM).transpose(0, 2, 1, 3)


def merge_heads(t):
    b, h, s, d = t.shape
    return t.transpose(0, 2, 1, 3).reshape(b, s, h * d)


def stick_breaking_attention(q, k, v):
    b, h, s, d = q.shape
    nb = s // BLOCK
    scale = d ** -0.5
    qb = q.reshape(b, h, nb, BLOCK, d).transpose(2, 0, 1, 3, 4)
    key_pos = jnp.arange(s)

    def one_block(args):
        q_blk, blk = args
        q_pos = blk * BLOCK + jnp.arange(BLOCK)
        z = jnp.einsum('bhqd,bhkd->bhqk', q_blk, k, preferred_element_type=jnp.float32) * scale
        mask = key_pos[None, :] < q_pos[:, None]
        log_beta = jax.nn.log_sigmoid(z)
        log_stay = jnp.where(mask, jax.nn.log_sigmoid(-z), 0.0)
        later = lax.cumsum(log_stay, axis=3, reverse=True) - log_stay
        weights = jnp.where(mask, jnp.exp(log_beta + later), 0.0)
        return jnp.einsum('bhqk,bhkd->bhqd', weights.astype(v.dtype), v)

    out = lax.map(one_block, (qb, jnp.arange(nb)))
    return out.transpose(1, 2, 0, 3, 4).reshape(b, h, s, d)


def _dilated_pattern(q, k, v, window, dilation):
    b, h, s, d = q.shape
    span = window // dilation
    n_comp = s // dilation
    nb = n_comp // BLOCK

    def to_blocks(t):
        t = t.reshape(b, h, n_comp, dilation, d).transpose(0, 1, 3, 2, 4)
        return t.reshape(b, h, dilation, nb, BLOCK, d)

    def with_previous(t):
        prev = jnp.pad(t, ((0, 0), (0, 0), (0, 0), (1, 0), (0, 0), (0, 0)))[:, :, :, :-1]
        return jnp.concatenate([prev, t], axis=4)

    qb = to_blocks(q)
    kb = with_previous(to_blocks(k))
    vb = with_previous(to_blocks(v))
    z = jnp.einsum('bhcnqd,bhcnkd->bhcnqk', qb, kb, preferred_element_type=jnp.float32) * (d ** -0.5)
    q_idx = jnp.arange(BLOCK)[:, None] + BLOCK
    k_idx = jnp.arange(2 * BLOCK)[None, :]
    dist = q_idx - k_idx
    band = (dist >= 0) & (dist <= span)
    has_prev = (jnp.arange(nb) > 0)[:, None, None] | (k_idx >= BLOCK)[None]
    valid = band[None] & has_prev
    z = jnp.where(valid, z, -jnp.inf)
    m = jnp.max(z, axis=-1, keepdims=True)
    p = jnp.exp(z - m)
    denom = jnp.sum(p, axis=-1, keepdims=True)
    o = jnp.einsum('bhcnqk,bhcnkd->bhcnqd', p, vb.astype(jnp.float32)) / denom
    lse = m + jnp.log(denom)

    def from_blocks(t):
        e = t.shape[-1]
        t = t.reshape(b, h, dilation, n_comp, e).transpose(0, 1, 3, 2, 4)
        return t.reshape(b, h, s, e)

    return from_blocks(o), from_blocks(lse)


def dilated_mixture_attention(q, k, v):
    s = q.shape[2]
    outs, lses = [], []
    for window, dilation in DILATED_PATTERNS:
        unit = BLOCK * dilation
        s_pad = -(-s // unit) * unit
        pad = ((0, 0), (0, 0), (0, s_pad - s), (0, 0))
        o, lse = _dilated_pattern(jnp.pad(q, pad), jnp.pad(k, pad), jnp.pad(v, pad), window, dilation)
        outs.append(o[:, :, :s])
        lses.append(lse[:, :, :s])
    alpha = jax.nn.softmax(jnp.stack(lses), axis=0)
    return jnp.sum(alpha * jnp.stack(outs), axis=0).astype(q.dtype)


def token_mixer(h, w_in, sb_out_norm, dil_out_norm, w_out):
    seq = h.shape[1]
    proj = h @ w_in
    cuts = [D_SB, 2 * D_SB, 3 * D_SB, 3 * D_SB + D_DIL, 3 * D_SB + 2 * D_DIL]
    q_sb, k_sb, v_sb, q_dl, k_dl, v_dl = jnp.split(proj, cuts, axis=-1)
    positions = jnp.arange(seq)
    o_sb = stick_breaking_attention(split_heads(q_sb, N_HEADS_SB), split_heads(k_sb, N_HEADS_SB),
                                    split_heads(v_sb, N_HEADS_SB))
    o_dl = dilated_mixture_attention(rotary(split_heads(q_dl, N_HEADS_DIL), positions),
                                     rotary(split_heads(k_dl, N_HEADS_DIL), positions),
                                     split_heads(v_dl, N_HEADS_DIL))
    merged = jnp.concatenate([rmsnorm(merge_heads(o_sb), sb_out_norm),
                              rmsnorm(merge_heads(o_dl), dil_out_norm)], axis=-1)
    return merged @ w_out


def _fwd_setup_inputs(seed: int = 0) -> dict:
    key = jax.random.key(seed)
    ks = jax.random.split(key, 16)

    def normal(k, shape, scale):
        return jax.random.normal(k, shape, jnp.float32) * scale

    def gain(k, shape):
        return 1.0 + 0.02 * jax.random.normal(k, shape, jnp.float32)

    dm, df = D_MODEL ** -0.5, D_FF ** -0.5
    return {
        'x': normal(ks[0], (BATCH, SEQ, D_MODEL), 1.0),
        'ffn1_norm': gain(ks[1], (DEPTH, D_MODEL)),
        'ffn1_w_gate': normal(ks[2], (DEPTH, D_MODEL, D_FF), dm),
        'ffn1_w_up': normal(ks[3], (DEPTH, D_MODEL, D_FF), dm),
        'ffn1_w_down': normal(ks[4], (DEPTH, D_FF, D_MODEL), df),
        'mix_norm': gain(ks[5], (DEPTH, D_MODEL)),
        'w_in': normal(ks[6], (DEPTH, D_MODEL, D_IN), dm),
        'sb_out_norm': gain(ks[7], (DEPTH, D_SB)),
        'dil_out_norm': gain(ks[8], (DEPTH, D_DIL)),
        'w_out': normal(ks[9], (DEPTH, D_MODEL, D_MODEL), dm),
        'ffn2_norm': gain(ks[10], (DEPTH, D_MODEL)),
        'ffn2_w_gate': normal(ks[11], (DEPTH, D_MODEL, D_FF), dm),
        'ffn2_w_up': normal(ks[12], (DEPTH, D_MODEL, D_FF), dm),
        'ffn2_w_down': normal(ks[13], (DEPTH, D_FF, D_MODEL), df),
        'final_norm': gain(ks[14], (D_MODEL,)),
    }


def _fwd_reference(x, ffn1_norm, ffn1_w_gate, ffn1_w_up, ffn1_w_down, mix_norm, w_in, sb_out_norm,
              dil_out_norm, w_out, ffn2_norm, ffn2_w_gate, ffn2_w_up, ffn2_w_down, final_norm):
    for layer in range(DEPTH):
        x = x + HALF_STEP * swiglu(rmsnorm(x, ffn1_norm[layer]), ffn1_w_gate[layer],
                                   ffn1_w_up[layer], ffn1_w_down[layer])
        x = x + token_mixer(rmsnorm(x, mix_norm[layer]), w_in[layer], sb_out_norm[layer],
                            dil_out_norm[layer], w_out[layer])
        x = x + HALF_STEP * swiglu(rmsnorm(x, ffn2_norm[layer]), ffn2_w_gate[layer],
                                   ffn2_w_up[layer], ffn2_w_down[layer])
    return rmsnorm(x, final_norm)


import jax as _jax
import jax.numpy as _jnp

TWIN_FORMAT = 'train_step'
FWD_PARAMS = ['x', 'ffn1_norm', 'ffn1_w_gate', 'ffn1_w_up', 'ffn1_w_down', 'mix_norm', 'w_in', 'sb_out_norm', 'dil_out_norm', 'w_out', 'ffn2_norm', 'ffn2_w_gate', 'ffn2_w_up', 'ffn2_w_down', 'final_norm']
TWIN_WEIGHTS = ['ffn1_norm', 'ffn1_w_gate', 'ffn1_w_up', 'ffn1_w_down', 'mix_norm', 'w_in', 'sb_out_norm', 'dil_out_norm', 'w_out', 'ffn2_norm', 'ffn2_w_gate', 'ffn2_w_up', 'ffn2_w_down', 'final_norm']
TWIN_DIFF_INPUT = 'x'
TWIN_INPUTS = ['x', 'ffn1_norm', 'ffn1_w_gate', 'ffn1_w_up', 'ffn1_w_down', 'mix_norm', 'w_in', 'sb_out_norm', 'dil_out_norm', 'w_out', 'ffn2_norm', 'ffn2_w_gate', 'ffn2_w_up', 'ffn2_w_down', 'final_norm', 'loss_target', 'm_ffn1_norm', 'm_ffn1_w_gate', 'm_ffn1_w_up', 'm_ffn1_w_down', 'm_mix_norm', 'm_w_in', 'm_sb_out_norm', 'm_dil_out_norm', 'm_w_out', 'm_ffn2_norm', 'm_ffn2_w_gate', 'm_ffn2_w_up', 'm_ffn2_w_down', 'm_final_norm', 'v_ffn1_norm', 'v_ffn1_w_gate', 'v_ffn1_w_up', 'v_ffn1_w_down', 'v_mix_norm', 'v_w_in', 'v_sb_out_norm', 'v_dil_out_norm', 'v_w_out', 'v_ffn2_norm', 'v_ffn2_w_gate', 'v_ffn2_w_up', 'v_ffn2_w_down', 'v_final_norm']
TWIN_OUTPUTS = ['loss', 'grad_x', 'grad_ffn1_norm', 'grad_ffn1_w_gate', 'grad_ffn1_w_up', 'grad_ffn1_w_down', 'grad_mix_norm', 'grad_w_in', 'grad_sb_out_norm', 'grad_dil_out_norm', 'grad_w_out', 'grad_ffn2_norm', 'grad_ffn2_w_gate', 'grad_ffn2_w_up', 'grad_ffn2_w_down', 'grad_final_norm', 'delta_ffn1_norm', 'delta_ffn1_w_gate', 'delta_ffn1_w_up', 'delta_ffn1_w_down', 'delta_mix_norm', 'delta_w_in', 'delta_sb_out_norm', 'delta_dil_out_norm', 'delta_w_out', 'delta_ffn2_norm', 'delta_ffn2_w_gate', 'delta_ffn2_w_up', 'delta_ffn2_w_down', 'delta_final_norm', 'new_m_ffn1_norm', 'new_m_ffn1_w_gate', 'new_m_ffn1_w_up', 'new_m_ffn1_w_down', 'new_m_mix_norm', 'new_m_w_in', 'new_m_sb_out_norm', 'new_m_dil_out_norm', 'new_m_w_out', 'new_m_ffn2_norm', 'new_m_ffn2_w_gate', 'new_m_ffn2_w_up', 'new_m_ffn2_w_down', 'new_m_final_norm', 'new_v_ffn1_norm', 'new_v_ffn1_w_gate', 'new_v_ffn1_w_up', 'new_v_ffn1_w_down', 'new_v_mix_norm', 'new_v_w_in', 'new_v_sb_out_norm', 'new_v_dil_out_norm', 'new_v_w_out', 'new_v_ffn2_norm', 'new_v_ffn2_w_gate', 'new_v_ffn2_w_up', 'new_v_ffn2_w_down', 'new_v_final_norm']
TWIN_LEAF_KINDS = {'loss': 'loss', 'grad_x': 'grad_x', 'grad_ffn1_norm': 'grad_w', 'grad_ffn1_w_gate': 'grad_w', 'grad_ffn1_w_up': 'grad_w', 'grad_ffn1_w_down': 'grad_w', 'grad_mix_norm': 'grad_w', 'grad_w_in': 'grad_w', 'grad_sb_out_norm': 'grad_w', 'grad_dil_out_norm': 'grad_w', 'grad_w_out': 'grad_w', 'grad_ffn2_norm': 'grad_w', 'grad_ffn2_w_gate': 'grad_w', 'grad_ffn2_w_up': 'grad_w', 'grad_ffn2_w_down': 'grad_w', 'grad_final_norm': 'grad_w', 'delta_ffn1_norm': 'delta_w', 'delta_ffn1_w_gate': 'delta_w', 'delta_ffn1_w_up': 'delta_w', 'delta_ffn1_w_down': 'delta_w', 'delta_mix_norm': 'delta_w', 'delta_w_in': 'delta_w', 'delta_sb_out_norm': 'delta_w', 'delta_dil_out_norm': 'delta_w', 'delta_w_out': 'delta_w', 'delta_ffn2_norm': 'delta_w', 'delta_ffn2_w_gate': 'delta_w', 'delta_ffn2_w_up': 'delta_w', 'delta_ffn2_w_down': 'delta_w', 'delta_final_norm': 'delta_w', 'new_m_ffn1_norm': 'new_m', 'new_m_ffn1_w_gate': 'new_m', 'new_m_ffn1_w_up': 'new_m', 'new_m_ffn1_w_down': 'new_m', 'new_m_mix_norm': 'new_m', 'new_m_w_in': 'new_m', 'new_m_sb_out_norm': 'new_m', 'new_m_dil_out_norm': 'new_m', 'new_m_w_out': 'new_m', 'new_m_ffn2_norm': 'new_m', 'new_m_ffn2_w_gate': 'new_m', 'new_m_ffn2_w_up': 'new_m', 'new_m_ffn2_w_down': 'new_m', 'new_m_final_norm': 'new_m', 'new_v_ffn1_norm': 'new_v', 'new_v_ffn1_w_gate': 'new_v', 'new_v_ffn1_w_up': 'new_v', 'new_v_ffn1_w_down': 'new_v', 'new_v_mix_norm': 'new_v', 'new_v_w_in': 'new_v', 'new_v_sb_out_norm': 'new_v', 'new_v_dil_out_norm': 'new_v', 'new_v_w_out': 'new_v', 'new_v_ffn2_norm': 'new_v', 'new_v_ffn2_w_gate': 'new_v', 'new_v_ffn2_w_up': 'new_v', 'new_v_ffn2_w_down': 'new_v', 'new_v_final_norm': 'new_v'}


def _forward(args):
    return _fwd_reference(*[args[k] for k in FWD_PARAMS])


def _output_shape():
    def fwd():
        inp = _fwd_setup_inputs(0)
        return _fwd_reference(*[inp[k] for k in FWD_PARAMS])
    out = _jax.eval_shape(fwd)
    return out.shape, out.dtype

N_MICROBATCH = 1
ADAM_LR = 0.001
ADAM_B1 = 0.9
ADAM_B2 = 0.999
ADAM_EPS = 1e-08
ADAM_WD = 0.01
ADAM_STEP = 10
PER_EXAMPLE_BATCH_AXIS = {'x': 0, 'loss_target': 0}
SHARED_INPUTS = []
_WEIGHT_DTYPES = {'ffn1_norm': _jnp.float32, 'ffn1_w_gate': _jnp.float32, 'ffn1_w_up': _jnp.float32, 'ffn1_w_down': _jnp.float32, 'mix_norm': _jnp.float32, 'w_in': _jnp.float32, 'sb_out_norm': _jnp.float32, 'dil_out_norm': _jnp.float32, 'w_out': _jnp.float32, 'ffn2_norm': _jnp.float32, 'ffn2_w_gate': _jnp.float32, 'ffn2_w_up': _jnp.float32, 'ffn2_w_down': _jnp.float32, 'final_norm': _jnp.float32}
MOMENT_SCALE = {'ffn1_norm': 1.413570e-01, 'ffn1_w_gate': 5.720736e-02, 'ffn1_w_up': 5.545984e-02, 'ffn1_w_down': 9.213297e-02, 'mix_norm': 2.844176e-01, 'w_in': 1.531361e-01, 'sb_out_norm': 1.730077e-01, 'dil_out_norm': 1.754550e-01, 'w_out': 1.785885e-01, 'ffn2_norm': 7.776903e-02, 'ffn2_w_gate': 3.157171e-02, 'ffn2_w_up': 3.061786e-02, 'ffn2_w_down': 5.059697e-02, 'final_norm': 6.398119e+01}


def _to_microbatches(a, axis):
    t = _jnp.moveaxis(a, axis, 0)
    t = t.reshape((N_MICROBATCH, t.shape[0] // N_MICROBATCH) + t.shape[1:])
    return _jnp.moveaxis(t, 1, axis + 1)


def setup_inputs(seed: int = 0) -> dict:
    inp = _fwd_setup_inputs(seed)
    key = _jax.random.fold_in(_jax.random.key(seed), 7919)
    shape, _ = _output_shape()
    out = dict(inp)
    out["loss_target"] = _jax.random.normal(_jax.random.fold_in(key, 0), shape, _jnp.float32)
    for i, name in enumerate(TWIN_WEIGHTS):
        w = inp[name].astype(_jnp.float32)
        if MOMENT_SCALE is None:
            s = _jnp.sqrt(_jnp.mean(_jnp.square(w)) + 1e-30)
        else:
            s = MOMENT_SCALE[name]
        km, kv = _jax.random.split(_jax.random.fold_in(key, i + 1))
        out[name] = w
        out["m_" + name] = s * _jax.random.normal(km, w.shape, _jnp.float32)
        out["v_" + name] = (s * s) * _jax.random.uniform(kv, w.shape, _jnp.float32, 0.5, 1.5)
    if N_MICROBATCH > 1:
        for name, axis in PER_EXAMPLE_BATCH_AXIS.items():
            out[name] = _to_microbatches(out[name], axis)
    return {'x': out['x'], 'ffn1_norm': out['ffn1_norm'], 'ffn1_w_gate': out['ffn1_w_gate'], 'ffn1_w_up': out['ffn1_w_up'], 'ffn1_w_down': out['ffn1_w_down'], 'mix_norm': out['mix_norm'], 'w_in': out['w_in'], 'sb_out_norm': out['sb_out_norm'], 'dil_out_norm': out['dil_out_norm'], 'w_out': out['w_out'], 'ffn2_norm': out['ffn2_norm'], 'ffn2_w_gate': out['ffn2_w_gate'], 'ffn2_w_up': out['ffn2_w_up'], 'ffn2_w_down': out['ffn2_w_down'], 'final_norm': out['final_norm'], 'loss_target': out['loss_target'], 'm_ffn1_norm': out['m_ffn1_norm'], 'm_ffn1_w_gate': out['m_ffn1_w_gate'], 'm_ffn1_w_up': out['m_ffn1_w_up'], 'm_ffn1_w_down': out['m_ffn1_w_down'], 'm_mix_norm': out['m_mix_norm'], 'm_w_in': out['m_w_in'], 'm_sb_out_norm': out['m_sb_out_norm'], 'm_dil_out_norm': out['m_dil_out_norm'], 'm_w_out': out['m_w_out'], 'm_ffn2_norm': out['m_ffn2_norm'], 'm_ffn2_w_gate': out['m_ffn2_w_gate'], 'm_ffn2_w_up': out['m_ffn2_w_up'], 'm_ffn2_w_down': out['m_ffn2_w_down'], 'm_final_norm': out['m_final_norm'], 'v_ffn1_norm': out['v_ffn1_norm'], 'v_ffn1_w_gate': out['v_ffn1_w_gate'], 'v_ffn1_w_up': out['v_ffn1_w_up'], 'v_ffn1_w_down': out['v_ffn1_w_down'], 'v_mix_norm': out['v_mix_norm'], 'v_w_in': out['v_w_in'], 'v_sb_out_norm': out['v_sb_out_norm'], 'v_dil_out_norm': out['v_dil_out_norm'], 'v_w_out': out['v_w_out'], 'v_ffn2_norm': out['v_ffn2_norm'], 'v_ffn2_w_gate': out['v_ffn2_w_gate'], 'v_ffn2_w_up': out['v_ffn2_w_up'], 'v_ffn2_w_down': out['v_ffn2_w_down'], 'v_final_norm': out['v_final_norm']}


def _loss(weights, diff, rest, loss_target):
    with _jax.named_scope("forward"):
        args = {**rest, TWIN_DIFF_INPUT: diff, **{k: w.astype(_WEIGHT_DTYPES[k]) for k, w in weights.items()}}
        y = _forward(args)
    with _jax.named_scope("loss_head"):
        err = _jnp.square(y.astype(_jnp.float32) - loss_target)
        return 0.5 * _jnp.sum(_jnp.mean(err, axis=-1)) if err.ndim else 0.5 * err


def _adamw(w, g, m, v):
    m = ADAM_B1 * m + (1.0 - ADAM_B1) * g
    v = ADAM_B2 * v + (1.0 - ADAM_B2) * _jnp.square(g)
    m_hat = m / (1.0 - ADAM_B1 ** ADAM_STEP)
    v_hat = v / (1.0 - ADAM_B2 ** ADAM_STEP)
    delta = -ADAM_LR * (m_hat / (_jnp.sqrt(v_hat) + ADAM_EPS) + ADAM_WD * w)
    return delta, m, v


def reference(x, ffn1_norm, ffn1_w_gate, ffn1_w_up, ffn1_w_down, mix_norm, w_in, sb_out_norm, dil_out_norm, w_out, ffn2_norm, ffn2_w_gate, ffn2_w_up, ffn2_w_down, final_norm, loss_target, m_ffn1_norm, m_ffn1_w_gate, m_ffn1_w_up, m_ffn1_w_down, m_mix_norm, m_w_in, m_sb_out_norm, m_dil_out_norm, m_w_out, m_ffn2_norm, m_ffn2_w_gate, m_ffn2_w_up, m_ffn2_w_down, m_final_norm, v_ffn1_norm, v_ffn1_w_gate, v_ffn1_w_up, v_ffn1_w_down, v_mix_norm, v_w_in, v_sb_out_norm, v_dil_out_norm, v_w_out, v_ffn2_norm, v_ffn2_w_gate, v_ffn2_w_up, v_ffn2_w_down, v_final_norm):
    given = dict(x=x, ffn1_norm=ffn1_norm, ffn1_w_gate=ffn1_w_gate, ffn1_w_up=ffn1_w_up, ffn1_w_down=ffn1_w_down, mix_norm=mix_norm, w_in=w_in, sb_out_norm=sb_out_norm, dil_out_norm=dil_out_norm, w_out=w_out, ffn2_norm=ffn2_norm, ffn2_w_gate=ffn2_w_gate, ffn2_w_up=ffn2_w_up, ffn2_w_down=ffn2_w_down, final_norm=final_norm, loss_target=loss_target, m_ffn1_norm=m_ffn1_norm, m_ffn1_w_gate=m_ffn1_w_gate, m_ffn1_w_up=m_ffn1_w_up, m_ffn1_w_down=m_ffn1_w_down, m_mix_norm=m_mix_norm, m_w_in=m_w_in, m_sb_out_norm=m_sb_out_norm, m_dil_out_norm=m_dil_out_norm, m_w_out=m_w_out, m_ffn2_norm=m_ffn2_norm, m_ffn2_w_gate=m_ffn2_w_gate, m_ffn2_w_up=m_ffn2_w_up, m_ffn2_w_down=m_ffn2_w_down, m_final_norm=m_final_norm, v_ffn1_norm=v_ffn1_norm, v_ffn1_w_gate=v_ffn1_w_gate, v_ffn1_w_up=v_ffn1_w_up, v_ffn1_w_down=v_ffn1_w_down, v_mix_norm=v_mix_norm, v_w_in=v_w_in, v_sb_out_norm=v_sb_out_norm, v_dil_out_norm=v_dil_out_norm, v_w_out=v_w_out, v_ffn2_norm=v_ffn2_norm, v_ffn2_w_gate=v_ffn2_w_gate, v_ffn2_w_up=v_ffn2_w_up, v_ffn2_w_down=v_ffn2_w_down, v_final_norm=v_final_norm)
    weights = {n: given[n] for n in TWIN_WEIGHTS}
    shared = {n: given[n] for n in SHARED_INPUTS}
    per_example = {n: given[n] for n in ['x']}
    grad_fn = _jax.value_and_grad(_loss, argnums=(0, 1))

    def one_microbatch(ex, loss_target):
        ex = dict(ex)
        diff = ex.pop(TWIN_DIFF_INPUT)
        return grad_fn(weights, diff, {**shared, **ex}, loss_target)

    if N_MICROBATCH == 1:
        loss, (grad_w, grad_x) = one_microbatch(per_example, given["loss_target"])
    else:
        def body(carry, xs):
            loss_sum, grad_sum = carry
            l_k, (gw_k, gx_k) = one_microbatch(xs[0], xs[1])
            with _jax.named_scope("update"):
                return (loss_sum + l_k, _jax.tree.map(_jnp.add, grad_sum, gw_k)), gx_k

        init = (_jnp.zeros((), _jnp.float32), _jax.tree.map(_jnp.zeros_like, weights))
        (loss, grad_w), grad_x = _jax.lax.scan(body, init, (per_example, given["loss_target"]))
    with _jax.named_scope("update"):
        delta_w, new_m, new_v = {}, {}, {}
        for n in TWIN_WEIGHTS:
            delta_w[n], new_m[n], new_v[n] = _adamw(weights[n], grad_w[n], given["m_" + n], given["v_" + n])
    return (loss, grad_x, *[grad_w[n] for n in TWIN_WEIGHTS], *[delta_w[n] for n in TWIN_WEIGHTS],
            *[new_m[n] for n in TWIN_WEIGHTS], *[new_v[n] for n in TWIN_WEIGHTS])
```

```python
import functools

import jax
import jax.numpy as jnp
from jax import lax
from jax.experimental import pallas as pl
from jax.experimental.pallas import tpu as pltpu

D_MODEL = 1024
D_FF = 2816
HEAD_DIM = 64
D_SB = 512
D_DIL = 512
D_IN = 3072
N_CHIP = 4
FFB = D_FF // N_CHIP
INB = D_IN // N_CHIP
OUTB = D_MODEL // N_CHIP
BLK = 128
LANES = 128
DILATIONS = (1, 4, 16)
ROPE_THETA = 10000.0
RMS_EPS = 1e-6
SCALE = HEAD_DIM ** -0.5
NEG = -1e30
DEAD = -104.0
ADAM_LR = 0.001
ADAM_B1 = 0.9
ADAM_B2 = 0.999
ADAM_EPS = 1e-08
ADAM_WD = 0.01
ADAM_STEP = 10
MESH = pl.DeviceIdType.MESH
F32 = jnp.float32
BF16 = jnp.bfloat16
TM = 512


def _params(vmem_mb):
    return pltpu.CompilerParams(vmem_limit_bytes=vmem_mb << 20)


def _dot(a, b):
    return jnp.dot(a, b, preferred_element_type=F32)


def _dot_nt(a, b):
    return lax.dot_general(a, b, (((1,), (1,)), ((), ())), preferred_element_type=F32)


def _dot_tn(a, b):
    return lax.dot_general(a, b, (((0,), (0,)), ((), ())), preferred_element_type=F32)


def _rms_fwd(x, g):
    r = lax.rsqrt(jnp.mean(x * x, axis=-1, keepdims=True) + RMS_EPS)
    xh = x * r
    return xh * g, xh, r


def _rms_bwd(dy, xh, r, g):
    dyg = dy * g
    dx = r * (dyg - xh * jnp.mean(dyg * xh, axis=-1, keepdims=True))
    return dx, jnp.sum(dy * xh, axis=0, keepdims=True)


def _split_bf16(a):
    hi = a.astype(BF16)
    return hi, (a - hi.astype(F32)).astype(BF16)


def _dot_split(a, b):
    hi, lo = _split_bf16(a)
    return _dot(hi, b) + _dot(lo, b)


def _ffn_weight_specs(f):
    return [pl.BlockSpec((None, None, D_MODEL, FFB), lambda i, j: (j, 2 * f, 0, 0)),
            pl.BlockSpec((None, None, D_MODEL, FFB), lambda i, j: (j, 2 * f + 1, 0, 0)),
            pl.BlockSpec((None, None, FFB, D_MODEL), lambda i, j: (j, f, 0, 0))]


def _ffn_accumulate(h_scr, acc_scr, wg_ref, wu_ref, wd_ref):
    h = h_scr[...]
    a = _dot(h, wg_ref[...])
    b = _dot(h, wu_ref[...])
    act = (a * jax.nn.sigmoid(a)) * b
    acc_scr[...] += _dot(act.astype(BF16), wd_ref[...])


def _ffn1_fwd(x, g1, gmix, gu, wd):
    s = x.shape[0]
    row = pl.BlockSpec((TM, D_MODEL), lambda i, j: (i, 0))
    vec = pl.BlockSpec((1, D_MODEL), lambda i, j: (0, 0))

    def body(x_ref, g_ref, gm_ref, wg_ref, wu_ref, wd_ref, x1_ref, hm_ref, h_scr, acc_scr):
        j = pl.program_id(1)

        @pl.when(j == 0)
        def _():
            h, _, _ = _rms_fwd(x_ref[...], g_ref[...])
            h_scr[...] = h.astype(BF16)
            acc_scr[...] = jnp.zeros_like(acc_scr)

        _ffn_accumulate(h_scr, acc_scr, wg_ref, wu_ref, wd_ref)

        @pl.when(j == N_CHIP - 1)
        def _():
            x1 = x_ref[...] + 0.5 * acc_scr[...]
            x1_ref[...] = x1
            hm, _, _ = _rms_fwd(x1, gm_ref[...])
            hm_ref[...] = hm.astype(BF16)

    return pl.pallas_call(
        body, name="ffn1_fwd", grid=(s // TM, N_CHIP),
        in_specs=[row, vec, vec] + _ffn_weight_specs(0),
        out_specs=[row, row],
        out_shape=[jax.ShapeDtypeStruct((s, D_MODEL), F32), jax.ShapeDtypeStruct((s, D_MODEL), BF16)],
        scratch_shapes=[pltpu.VMEM((TM, D_MODEL), BF16), pltpu.VMEM((TM, D_MODEL), F32)],
        compiler_params=_params(48),
    )(x, g1, gmix, gu, gu, wd)


def _ffn2_fwd_loss(x2, g2, gf, target, gu, wd):
    s = x2.shape[0]
    row = pl.BlockSpec((TM, D_MODEL), lambda i, j: (i, 0))
    vec = pl.BlockSpec((1, D_MODEL), lambda i, j: (0, 0))
    stat = pl.BlockSpec((8, D_MODEL), lambda i, j: (0, 0))

    def body(x_ref, g_ref, gf_ref, t_ref, wg_ref, wu_ref, wd_ref, dx_ref, st_ref, h_scr, acc_scr):
        i, j = pl.program_id(0), pl.program_id(1)

        @pl.when((i == 0) & (j == 0))
        def _():
            st_ref[...] = jnp.zeros_like(st_ref)

        @pl.when(j == 0)
        def _():
            h, _, _ = _rms_fwd(x_ref[...], g_ref[...])
            h_scr[...] = h.astype(BF16)
            acc_scr[...] = jnp.zeros_like(acc_scr)

        _ffn_accumulate(h_scr, acc_scr, wg_ref, wu_ref, wd_ref)

        @pl.when(j == N_CHIP - 1)
        def _():
            x3 = x_ref[...] + 0.5 * acc_scr[...]
            y, xh, r = _rms_fwd(x3, gf_ref[...])
            err = y - t_ref[...]
            dx, dg = _rms_bwd(err * (1.0 / D_MODEL), xh, r, gf_ref[...])
            dx_ref[...] = dx
            st_ref[0:1, :] += dg
            st_ref[1:2, :] += jnp.sum(err * err, axis=0, keepdims=True)

    return pl.pallas_call(
        body, name="ffn2_fwd_loss", grid=(s // TM, N_CHIP),
        in_specs=[row, vec, vec, row] + _ffn_weight_specs(1),
        out_specs=[row, stat],
        out_shape=[jax.ShapeDtypeStruct((s, D_MODEL), F32), jax.ShapeDtypeStruct((8, D_MODEL), F32)],
        scratch_shapes=[pltpu.VMEM((TM, D_MODEL), BF16), pltpu.VMEM((TM, D_MODEL), F32)],
        compiler_params=_params(48),
    )(x2, g2, gf, target, gu, gu, wd)


def _ffn_bwd(xin, g, dy, gu, wd, f):
    s = xin.shape[0]
    row = pl.BlockSpec((TM, D_MODEL), lambda i, j: (i, 0))
    vec = pl.BlockSpec((1, D_MODEL), lambda i, j: (0, 0))
    stat = pl.BlockSpec((8, D_MODEL), lambda i, j: (0, 0))
    hid = pl.BlockSpec((None, TM, FFB), lambda i, j: (j, i, 0))

    def body(x_ref, g_ref, dy_ref, wg_ref, wu_ref, wd_ref, out_ref, h_ref, dyh_ref, da_ref, db_ref, act_ref, st_ref, dh_scr):
        i, j = pl.program_id(0), pl.program_id(1)

        @pl.when((i == 0) & (j == 0))
        def _():
            st_ref[...] = jnp.zeros_like(st_ref)

        @pl.when(j == 0)
        def _():
            h, _, _ = _rms_fwd(x_ref[...], g_ref[...])
            h_ref[...] = h.astype(BF16)
            dyh_ref[...] = (0.5 * dy_ref[...]).astype(BF16)
            dh_scr[...] = jnp.zeros_like(dh_scr)

        hb = h_ref[...]
        a = _dot(hb, wg_ref[...])
        b = _dot(hb, wu_ref[...])
        sg = jax.nn.sigmoid(a)
        sa = a * sg
        dact = _dot_nt(dyh_ref[...], wd_ref[...])
        dab = (dact * b * (sg * (1.0 + a * (1.0 - sg)))).astype(BF16)
        dbb = (dact * sa).astype(BF16)
        da_ref[...] = dab
        db_ref[...] = dbb
        act_ref[...] = (sa * b).astype(BF16)
        dh_scr[...] += _dot_nt(dab, wg_ref[...]) + _dot_nt(dbb, wu_ref[...])

        @pl.when(j == N_CHIP - 1)
        def _():
            _, xh, r = _rms_fwd(x_ref[...], g_ref[...])
            dx, dg = _rms_bwd(dh_scr[...], xh, r, g_ref[...])
            out_ref[...] = dy_ref[...] + dx
            st_ref[0:1, :] += dg

    hidden = jax.ShapeDtypeStruct((N_CHIP, s, FFB), BF16)
    dx, hb, dyh, da, db, act, st = pl.pallas_call(
        body, name=f"ffn{f + 1}_bwd_dx", grid=(s // TM, N_CHIP),
        in_specs=[row, vec, row] + _ffn_weight_specs(f),
        out_specs=[row, row, row, hid, hid, hid, stat],
        out_shape=[jax.ShapeDtypeStruct((s, D_MODEL), F32), jax.ShapeDtypeStruct((s, D_MODEL), BF16),
                   jax.ShapeDtypeStruct((s, D_MODEL), BF16), hidden, hidden, hidden,
                   jax.ShapeDtypeStruct((8, D_MODEL), F32)],
        scratch_shapes=[pltpu.VMEM((TM, D_MODEL), F32)],
        compiler_params=_params(56),
    )(xin, g, dy, gu, gu, wd)

    tok = pl.BlockSpec((TM, D_MODEL), lambda j, i: (i, 0))
    hid2 = pl.BlockSpec((None, TM, FFB), lambda j, i: (j, i, 0))
    gspecs = [pl.BlockSpec((None, D_MODEL, FFB), lambda j, i: (j, 0, 0)),
              pl.BlockSpec((None, D_MODEL, FFB), lambda j, i: (j, 0, 0)),
              pl.BlockSpec((None, FFB, D_MODEL), lambda j, i: (j, 0, 0))]

    def wbody(h_ref, dyh_ref, da_ref, db_ref, act_ref, dwg_ref, dwu_ref, dwd_ref):
        @pl.when(pl.program_id(1) == 0)
        def _():
            dwg_ref[...] = jnp.zeros_like(dwg_ref)
            dwu_ref[...] = jnp.zeros_like(dwu_ref)
            dwd_ref[...] = jnp.zeros_like(dwd_ref)

        hb = h_ref[...]
        dwg_ref[...] += _dot_tn(hb, da_ref[...])
        dwu_ref[...] += _dot_tn(hb, db_ref[...])
        dwd_ref[...] += _dot_tn(act_ref[...], dyh_ref[...])

    dwg, dwu, dwd = pl.pallas_call(
        wbody, name=f"ffn{f + 1}_bwd_dw", grid=(N_CHIP, s // TM),
        in_specs=[tok, tok, hid2, hid2, hid2], out_specs=gspecs,
        out_shape=[jax.ShapeDtypeStruct((N_CHIP, D_MODEL, FFB), F32),
                   jax.ShapeDtypeStruct((N_CHIP, D_MODEL, FFB), F32),
                   jax.ShapeDtypeStruct((N_CHIP, FFB, D_MODEL), F32)],
        compiler_params=_params(48),
    )(hb, dyh, da, db, act)
    return dx, dwg, dwu, dwd, st


def _rope_tables(s):
    half = HEAD_DIM // 2
    inv_freq = ROPE_THETA ** (-jnp.arange(half, dtype=F32) / half)
    ang = jnp.arange(s).astype(F32)[:, None] * inv_freq[None, :]
    cos, sin = jnp.cos(ang), jnp.sin(ang)
    cos2 = jnp.concatenate([cos, cos], axis=-1)
    sin2 = jnp.concatenate([-sin, sin], axis=-1)
    return jnp.tile(cos2, (1, LANES // HEAD_DIM)), jnp.tile(sin2, (1, LANES // HEAD_DIM))


def _rotate(t, cos, sin_signed):
    lane = lax.broadcasted_iota(jnp.int32, t.shape, 1)
    first = (lane % HEAD_DIM) < (HEAD_DIM // 2)
    partner = jnp.where(first, pltpu.roll(t, LANES - HEAD_DIM // 2, 1), pltpu.roll(t, HEAD_DIM // 2, 1))
    return t * cos + partner * sin_signed


def _proj_fwd(hm, win, cos, sin):
    s = hm.shape[0]
    n_sub = INB // LANES
    first_rot, last_rot = (3 * D_SB) // LANES, (3 * D_SB + 2 * D_DIL) // LANES

    def body(h_ref, w_ref, c_ref, s_ref, o_ref):
        j = pl.program_id(1)
        r = _dot(h_ref[...], w_ref[...])
        for c in range(n_sub):
            t = r[:, c * LANES:(c + 1) * LANES]
            col = j * n_sub + c
            rot = (col >= first_rot) & (col < last_rot)
            lanes = slice(c * LANES, (c + 1) * LANES)

            @pl.when(rot)
            def _():
                o_ref[:, lanes] = _rotate(t, c_ref[...], s_ref[...]).astype(BF16)

            @pl.when(jnp.logical_not(rot))
            def _():
                o_ref[:, lanes] = t.astype(BF16)

    return pl.pallas_call(
        body, name="proj_fwd", grid=(s // TM, N_CHIP),
        in_specs=[pl.BlockSpec((TM, D_MODEL), lambda i, j: (i, 0)),
                  pl.BlockSpec((None, D_MODEL, INB), lambda i, j: (j, 0, 0)),
                  pl.BlockSpec((TM, LANES), lambda i, j: (i, 0)),
                  pl.BlockSpec((TM, LANES), lambda i, j: (i, 0))],
        out_specs=pl.BlockSpec((TM, INB), lambda i, j: (i, j)),
        out_shape=jax.ShapeDtypeStruct((s, D_IN), BF16),
        compiler_params=_params(32),
    )(hm, win, cos, sin)


def _proj_bwd(x1, gmix, dqkv, win, dx2):
    s = x1.shape[0]
    row = pl.BlockSpec((TM, D_MODEL), lambda i, j: (i, 0))
    vec = pl.BlockSpec((1, D_MODEL), lambda i, j: (0, 0))

    def body(x_ref, g_ref, dq_ref, w_ref, dx2_ref, out_ref, dw_ref, st_ref, h_scr, dh_scr):
        i, j = pl.program_id(0), pl.program_id(1)

        @pl.when((i == 0) & (j == 0))
        def _():
            st_ref[...] = jnp.zeros_like(st_ref)
            dw_ref[...] = jnp.zeros_like(dw_ref)

        @pl.when(j == 0)
        def _():
            h, _, _ = _rms_fwd(x_ref[...], g_ref[...])
            h_scr[...] = h.astype(BF16)
            dh_scr[...] = jnp.zeros_like(dh_scr)

        dq = dq_ref[...]
        dw_ref[j] += _dot_tn(h_scr[...], dq)
        dh_scr[...] += _dot_nt(dq, w_ref[...])

        @pl.when(j == N_CHIP - 1)
        def _():
            _, xh, r = _rms_fwd(x_ref[...], g_ref[...])
            dx, dg = _rms_bwd(dh_scr[...], xh, r, g_ref[...])
            out_ref[...] = dx2_ref[...] + dx
            st_ref[0:1, :] += dg

    return pl.pallas_call(
        body, name="proj_bwd", grid=(s // TM, N_CHIP),
        in_specs=[row, vec, pl.BlockSpec((TM, INB), lambda i, j: (i, j)),
                  pl.BlockSpec((None, D_MODEL, INB), lambda i, j: (j, 0, 0)), row],
        out_specs=[row, pl.BlockSpec((N_CHIP, D_MODEL, INB), lambda i, j: (0, 0, 0)),
                   pl.BlockSpec((8, D_MODEL), lambda i, j: (0, 0))],
        out_shape=[jax.ShapeDtypeStruct((s, D_MODEL), F32),
                   jax.ShapeDtypeStruct((N_CHIP, D_MODEL, INB), F32),
                   jax.ShapeDtypeStruct((8, D_MODEL), F32)],
        scratch_shapes=[pltpu.VMEM((TM, D_MODEL), BF16), pltpu.VMEM((TM, D_MODEL), F32)],
        compiler_params=_params(56),
    )(x1, gmix, dqkv, win, dx2)


def _outproj_fwd(o_sb, o_dl, g_sb, g_dl, x1, wout):
    s = x1.shape[0]
    half = pl.BlockSpec((TM, D_SB), lambda i: (i, 0))
    row = pl.BlockSpec((TM, D_MODEL), lambda i: (i, 0))
    vec = pl.BlockSpec((1, D_SB), lambda i: (0, 0))

    def body(a_ref, b_ref, ga_ref, gb_ref, x_ref, w_ref, o_ref):
        ma, _, _ = _rms_fwd(a_ref[...], ga_ref[...])
        mb, _, _ = _rms_fwd(b_ref[...], gb_ref[...])
        o_ref[...] = (x_ref[...] + _dot(ma.astype(BF16), w_ref[0:D_SB, :])
                      + _dot(mb.astype(BF16), w_ref[D_SB:D_MODEL, :]))

    return pl.pallas_call(
        body, name="outproj_fwd", grid=(s // TM,),
        in_specs=[half, half, vec, vec, row, pl.BlockSpec((D_MODEL, D_MODEL), lambda i: (0, 0))],
        out_specs=row, out_shape=jax.ShapeDtypeStruct((s, D_MODEL), F32),
        compiler_params=_params(32),
    )(o_sb, o_dl, g_sb, g_dl, x1, wout)


def _outproj_bwd(dx2, o_sb, o_dl, g_sb, g_dl, wout):
    s = dx2.shape[0]
    half = pl.BlockSpec((TM, D_SB), lambda i: (i, 0))
    row = pl.BlockSpec((TM, D_MODEL), lambda i: (i, 0))
    vec = pl.BlockSpec((1, D_SB), lambda i: (0, 0))
    full = pl.BlockSpec((D_MODEL, D_MODEL), lambda i: (0, 0))

    def body(dy_ref, a_ref, b_ref, ga_ref, gb_ref, w_ref, da_ref, db_ref, dw_ref, st_ref):
        @pl.when(pl.program_id(0) == 0)
        def _():
            dw_ref[...] = jnp.zeros_like(dw_ref)
            st_ref[...] = jnp.zeros_like(st_ref)

        dy = dy_ref[...].astype(BF16)
        dm = _dot_nt(dy, w_ref[...])
        ma, xa, ra = _rms_fwd(a_ref[...], ga_ref[...])
        mb, xb, rb = _rms_fwd(b_ref[...], gb_ref[...])
        dw_ref[0:D_SB, :] += _dot_tn(ma.astype(BF16), dy)
        dw_ref[D_SB:D_MODEL, :] += _dot_tn(mb.astype(BF16), dy)
        da, dga = _rms_bwd(dm[:, 0:D_SB], xa, ra, ga_ref[...])
        db, dgb = _rms_bwd(dm[:, D_SB:D_MODEL], xb, rb, gb_ref[...])
        da_ref[...] = da
        db_ref[...] = db
        st_ref[0:1, :] += dga
        st_ref[1:2, :] += dgb

    return pl.pallas_call(
        body, name="outproj_bwd", grid=(s // TM,),
        in_specs=[row, half, half, vec, vec, full],
        out_specs=[half, half, full, pl.BlockSpec((8, D_SB), lambda i: (0, 0))],
        out_shape=[jax.ShapeDtypeStruct((s, D_SB), F32), jax.ShapeDtypeStruct((s, D_SB), F32),
                   jax.ShapeDtypeStruct((D_MODEL, D_MODEL), F32), jax.ShapeDtypeStruct((8, D_SB), F32)],
        compiler_params=_params(48),
    )(dx2, o_sb, o_dl, g_sb, g_dl, wout)


def _head_masks():
    lane = lax.broadcasted_iota(jnp.int32, (BLK, LANES), 1)
    return [lane < HEAD_DIM, lane >= HEAD_DIM]


def _keep(mask, a):
    return a * jnp.where(mask, 1.0, 0.0).astype(a.dtype)


def _suffix_matrices():
    r = lax.broadcasted_iota(jnp.int32, (BLK, BLK), 0)
    c = lax.broadcasted_iota(jnp.int32, (BLK, BLK), 1)
    ones = jnp.ones((BLK, BLK), BF16)
    excl = jnp.concatenate([(r > c).astype(BF16), ones], axis=1)
    incl = jnp.concatenate([(r >= c).astype(BF16), ones], axis=1)
    return excl, incl


def _blk(i):
    return pl.ds(pl.multiple_of(i * BLK, BLK), BLK)


def _alive(carry_m):
    return (jnp.max(carry_m) > DEAD).astype(jnp.int32)


def _more_keys(i, carry):
    return (carry[0] <= i) & (carry[1] > 0)


def _sb_scores(qh, k, i, j, carry_m, u_excl):
    row = lax.broadcasted_iota(jnp.int32, (BLK, BLK), 0)
    col = lax.broadcasted_iota(jnp.int32, (BLK, BLK), 1)
    valid = (j * BLK + col) < (i * BLK + row)
    z = _dot_nt(qh, k) * SCALE
    sp = jnp.maximum(z, 0.0) + jnp.log(1.0 + jnp.exp(-jnp.abs(z)))
    log_stay = jnp.where(valid, -sp, 0.0)
    log_beta = z - sp
    sums = _dot_split(log_stay, u_excl)
    later = carry_m + sums[:, :BLK]
    w = jnp.where(valid, jnp.exp(log_beta + later), 0.0)
    return valid, log_beta, w, carry_m + sums[:, BLK:]


def _sb_fwd(qkv):
    s = qkv.shape[0]
    nq = s // BLK
    pairs = D_SB // LANES
    col = lambda off: pl.BlockSpec((s, LANES), lambda p: (0, off + p))

    def body(q_ref, k_ref, v_ref, o_ref):
        masks = _head_masks()
        u_excl, _ = _suffix_matrices()
        zero = jnp.zeros((BLK, LANES), F32)

        def q_block(i, _):
            q = q_ref[_blk(i), :]
            out = zero
            for hm in masks:
                qh = _keep(hm, q)

                def k_block(carry):
                    jj, _, carry_m, acc = carry
                    j = i - jj
                    _, _, w, carry_m = _sb_scores(qh, k_ref[_blk(j), :], i, j, carry_m, u_excl)
                    return jj + 1, _alive(carry_m), carry_m, acc + _dot(w.astype(BF16), v_ref[_blk(j), :])

                _, _, _, acc = lax.while_loop(functools.partial(_more_keys, i), k_block,
                                              (jnp.int32(0), jnp.int32(1), zero, zero))
                out = jnp.where(hm, acc, out)
            o_ref[_blk(i), :] = out
            return 0

        lax.fori_loop(0, nq, q_block, 0)

    return pl.pallas_call(
        body, name="sb_fwd", grid=(pairs,),
        in_specs=[col(0), col(pairs), col(2 * pairs)],
        out_specs=pl.BlockSpec((s, LANES), lambda p: (0, p)),
        out_shape=jax.ShapeDtypeStruct((s, D_SB), F32),
        compiler_params=_params(48),
    )(qkv, qkv, qkv)


def _sb_bwd(qkv, o_sb, do_sb):
    s = qkv.shape[0]
    nq = s // BLK
    pairs = D_SB // LANES
    col = lambda off: pl.BlockSpec((s, LANES), lambda p: (0, off + p))
    own = pl.BlockSpec((s, LANES), lambda p: (0, p))

    def body(q_ref, k_ref, v_ref, o_ref, do_ref, dq_ref, dk_ref, dv_ref, dk_acc, dv_acc):
        masks = _head_masks()
        u_excl, u_incl = _suffix_matrices()
        zero = jnp.zeros((BLK, LANES), F32)
        dk_acc[...] = jnp.zeros_like(dk_acc)
        dv_acc[...] = jnp.zeros_like(dv_acc)

        def q_block(i, _):
            q = q_ref[_blk(i), :]
            do = do_ref[_blk(i), :].astype(BF16)
            prod = do.astype(F32) * o_ref[_blk(i), :]
            dq_out = zero
            for hm in masks:
                qh = _keep(hm, q)
                doh = _keep(hm, do)
                total = jnp.broadcast_to(jnp.sum(jnp.where(hm, prod, 0.0), axis=1, keepdims=True), (BLK, BLK))

                def k_block(carry):
                    jj, _, carry_m, carry_g, dq = carry
                    j = i - jj
                    k = k_ref[_blk(j), :]
                    valid, log_beta, w, carry_m = _sb_scores(qh, k, i, j, carry_m, u_excl)
                    wb = w.astype(BF16)
                    g = wb.astype(F32) * _dot_nt(doh, v_ref[_blk(j), :])
                    sums = _dot_split(g, u_incl)
                    before = total - (carry_g + sums[:, :BLK])
                    dz = jnp.where(valid, g - jnp.exp(log_beta) * (g + before), 0.0)
                    dzb = (dz * SCALE).astype(BF16)
                    dk_acc[_blk(j), :] += jnp.where(hm, _dot_tn(dzb, q), 0.0)
                    dv_acc[_blk(j), :] += jnp.where(hm, _dot_tn(wb, do), 0.0)
                    return jj + 1, _alive(carry_m), carry_m, carry_g + sums[:, BLK:], dq + _dot(dzb, k)

                _, _, _, _, dq = lax.while_loop(functools.partial(_more_keys, i), k_block,
                                                (jnp.int32(0), jnp.int32(1), zero, zero, zero))
                dq_out = jnp.where(hm, dq, dq_out)
            dq_ref[_blk(i), :] = dq_out.astype(BF16)
            return 0

        lax.fori_loop(0, nq, q_block, 0)
        dk_ref[...] = dk_acc[...].astype(BF16)
        dv_ref[...] = dv_acc[...].astype(BF16)

    return pl.pallas_call(
        body, name="sb_bwd", grid=(pairs,),
        in_specs=[col(0), col(pairs), col(2 * pairs), own, own],
        out_specs=[own, own, own],
        out_shape=[jax.ShapeDtypeStruct((s, D_SB), BF16)] * 3,
        scratch_shapes=[pltpu.VMEM((s, LANES), F32), pltpu.VMEM((s, LANES), F32)],
        compiler_params=_params(56),
    )(qkv, qkv, qkv, o_sb, do_sb)


def _class_view(a, d):
    return a.reshape(a.shape[0] // d, d * a.shape[1])


def _natural_view(a, d):
    return a.reshape(a.shape[0] * d, a.shape[1] // d)


def _band_masks(b):
    row = lax.broadcasted_iota(jnp.int32, (BLK, BLK), 0)
    col = lax.broadcasted_iota(jnp.int32, (BLK, BLK), 1)
    return col <= row, (col - row) >= jnp.where(b > 0, 0, BLK)


def _dil_specs(n, d, width_blocks, offsets):
    mode = {"pipeline_mode": pl.Buffered(1)} if d == 1 else {}
    return [pl.BlockSpec((n, LANES), functools.partial(lambda p, c, off: (0, c * width_blocks + off + p), off=off), **mode)
            for off in offsets]


def _dil_fwd(qkv, d):
    s = qkv.shape[0]
    n = s // d
    nb = n // BLK
    pairs = D_DIL // LANES
    base = (3 * D_SB) // LANES
    in_specs = _dil_specs(n, d, D_IN // LANES, (base, base + pairs, base + 2 * pairs))
    out_spec = _dil_specs(n, d, pairs, (0,))[0]
    view = _class_view(qkv, d)

    def body(q_ref, k_ref, v_ref, o_ref, l_ref):
        masks = _head_masks()

        def block(b, _):
            prev = jnp.maximum(b - 1, 0)
            in_cur, in_prev = _band_masks(b)
            q, kc, kp, vc, vp = q_ref[_blk(b), :], k_ref[_blk(b), :], k_ref[_blk(prev), :], v_ref[_blk(b), :], v_ref[_blk(prev), :]
            outs, lses = [], []
            for hm in masks:
                qh = _keep(hm, q)
                zc = jnp.where(in_cur, _dot_nt(qh, kc) * SCALE, NEG)
                zp = jnp.where(in_prev, _dot_nt(qh, kp) * SCALE, NEG)
                m = jnp.maximum(jnp.max(zc, axis=1, keepdims=True), jnp.max(zp, axis=1, keepdims=True))
                pc, pp = jnp.exp(zc - m), jnp.exp(zp - m)
                den = jnp.sum(pc, axis=1, keepdims=True) + jnp.sum(pp, axis=1, keepdims=True)
                outs.append((_dot(pc.astype(BF16), vc) + _dot(pp.astype(BF16), vp)) / den)
                lses.append(jnp.broadcast_to(m + jnp.log(den), (BLK, LANES)))
            o_ref[_blk(b), :] = jnp.where(masks[0], outs[0], outs[1])
            l_ref[_blk(b), :] = jnp.where(masks[0], lses[0], lses[1])
            return 0

        lax.fori_loop(0, nb, block, 0)

    o, lse = pl.pallas_call(
        body, name=f"dil_fwd_{d}", grid=(pairs, d),
        in_specs=in_specs, out_specs=[out_spec, out_spec],
        out_shape=[jax.ShapeDtypeStruct((n, d * D_DIL), F32)] * 2,
        compiler_params=_params(48),
    )(view, view, view)
    return _natural_view(o, d), _natural_view(lse, d)


def _dil_combine(outs, lses):
    s = outs[0].shape[0]
    spec = pl.BlockSpec((TM, D_DIL), lambda i: (i, 0))

    def body(o1, o2, o3, l1, l2, l3, out_ref, lse_ref):
        a, b, c = l1[...], l2[...], l3[...]
        m = jnp.maximum(jnp.maximum(a, b), c)
        ea, eb, ec = jnp.exp(a - m), jnp.exp(b - m), jnp.exp(c - m)
        tot = ea + eb + ec
        out_ref[...] = (ea / tot) * o1[...] + (eb / tot) * o2[...] + (ec / tot) * o3[...]
        lse_ref[...] = m + jnp.log(tot)

    return pl.pallas_call(
        body, name="dil_combine", grid=(s // TM,),
        in_specs=[spec] * 6, out_specs=[spec, spec],
        out_shape=[jax.ShapeDtypeStruct((s, D_DIL), F32)] * 2,
        compiler_params=_params(32),
    )(*outs, *lses)


def _dil_bwd(qkv, out, lse, dout, d):
    s = qkv.shape[0]
    n = s // d
    nb = n // BLK
    pairs = D_DIL // LANES
    base = (3 * D_SB) // LANES
    in_specs = (_dil_specs(n, d, D_IN // LANES, (base, base + pairs, base + 2 * pairs))
                + _dil_specs(n, d, pairs, (0, 0, 0)))
    out_spec = _dil_specs(n, d, pairs, (0,))[0]
    view = _class_view(qkv, d)

    def body(q_ref, k_ref, v_ref, o_ref, l_ref, do_ref, dq_ref, dk_ref, dv_ref):
        masks = _head_masks()
        dk_ref[...] = jnp.zeros_like(dk_ref)
        dv_ref[...] = jnp.zeros_like(dv_ref)

        def block(b, _):
            prev = jnp.maximum(b - 1, 0)
            in_cur, in_prev = _band_masks(b)
            q, kc, kp, vc, vp = q_ref[_blk(b), :], k_ref[_blk(b), :], k_ref[_blk(prev), :], v_ref[_blk(b), :], v_ref[_blk(prev), :]
            do32 = do_ref[_blk(b), :]
            do = do32.astype(BF16)
            prod = do32 * o_ref[_blk(b), :]
            lse_t = l_ref[_blk(b), :]
            dqs = []
            for hm in masks:
                qh = _keep(hm, q)
                doh = _keep(hm, do)
                delta = jnp.sum(jnp.where(hm, prod, 0.0), axis=1, keepdims=True)
                lse_h = jnp.max(jnp.where(hm, lse_t, NEG), axis=1, keepdims=True)
                wc = jnp.exp(jnp.where(in_cur, _dot_nt(qh, kc) * SCALE, NEG) - lse_h)
                wp = jnp.exp(jnp.where(in_prev, _dot_nt(qh, kp) * SCALE, NEG) - lse_h)
                dzc = (wc * (_dot_nt(doh, vc) - delta) * SCALE).astype(BF16)
                dzp = (wp * (_dot_nt(doh, vp) - delta) * SCALE).astype(BF16)
                dqs.append(_dot(dzc, kc) + _dot(dzp, kp))
                dk_ref[_blk(b), :] += jnp.where(hm, _dot_tn(dzc, q), 0.0)
                dk_ref[_blk(prev), :] += jnp.where(hm, _dot_tn(dzp, q), 0.0)
                dv_ref[_blk(b), :] += jnp.where(hm, _dot_tn(wc.astype(BF16), do), 0.0)
                dv_ref[_blk(prev), :] += jnp.where(hm, _dot_tn(wp.astype(BF16), do), 0.0)
            dq_ref[_blk(b), :] = jnp.where(masks[0], dqs[0], dqs[1])
            return 0

        lax.fori_loop(0, nb, block, 0)

    grads = pl.pallas_call(
        body, name=f"dil_bwd_{d}", grid=(pairs, d),
        in_specs=in_specs, out_specs=[out_spec] * 3,
        out_shape=[jax.ShapeDtypeStruct((n, d * D_DIL), F32)] * 3,
        compiler_params=_params(48),
    )(view, view, view, _class_view(out, d), _class_view(lse, d), _class_view(dout, d))
    return [_natural_view(g, d) for g in grads]


def _dil_finish(parts, cos, sin):
    s = parts[0][0].shape[0]
    spec = pl.BlockSpec((TM, D_DIL), lambda i: (i, 0))
    tab = pl.BlockSpec((TM, LANES), lambda i: (i, 0))

    def body(*refs):
        ins, c_ref, s_ref, outs = refs[:9], refs[9], refs[10], refs[11:]
        for t in range(3):
            tot = ins[t][...] + ins[3 + t][...] + ins[6 + t][...]
            for c in range(D_DIL // LANES):
                piece = tot[:, c * LANES:(c + 1) * LANES]
                if t < 2:
                    piece = _rotate(piece, c_ref[...], -s_ref[...])
                outs[t][:, c * LANES:(c + 1) * LANES] = piece.astype(BF16)

    return pl.pallas_call(
        body, name="dil_finish", grid=(s // TM,),
        in_specs=[spec] * 9 + [tab, tab], out_specs=[spec] * 3,
        out_shape=[jax.ShapeDtypeStruct((s, D_DIL), BF16)] * 3,
        compiler_params=_params(48),
    )(*parts[0], *parts[1], *parts[2], cos, sin)


def _place():
    x, y, c = lax.axis_index("x"), lax.axis_index("y"), lax.axis_index("c")
    return x, y, c, 2 * x + y


def _chip(k, c):
    return (k >> 1, k & 1, c)


def _half(ref, h):
    n = ref.shape[0] // 2
    return ref.at[pl.ds(h * n, n)]


def _all_gather(shards):
    na = len(shards)
    any_spec = pl.BlockSpec(memory_space=pl.ANY)

    def body(*refs):
        ins, outs = refs[:na], refs[na:2 * na]
        send_sem, recv_sem, local_sem = refs[2 * na:]
        x, y, c, k = _place()
        sibling = (x, y, 1 - c)
        started = []
        for a in range(na):
            cp = pltpu.make_async_copy(ins[a], outs[a].at[k], local_sem.at[a])
            cp.start()
            started.append(cp)

        def copy(a, slot, src, dst, to):
            return pltpu.make_async_remote_copy(src_ref=src, dst_ref=dst, send_sem=send_sem.at[a * 6 + slot],
                                                recv_sem=recv_sem.at[a * 6 + slot], device_id=to, device_id_type=MESH)

        sends = []
        for a in range(na):
            for j in range(1, N_CHIP):
                cp = copy(a, j - 1, _half(ins[a], c), _half(outs[a].at[k], c), _chip(k ^ j, c))
                cp.start()
                sends.append(cp)
        for j in range(1, N_CHIP):
            for a in range(na):
                landed = _half(outs[a].at[k ^ j], c)
                copy(a, j - 1, landed, landed, sibling).wait_recv()
                cp = copy(a, 2 + j, landed, landed, sibling)
                cp.start()
                sends.append(cp)
        for j in range(1, N_CHIP):
            for a in range(na):
                passed = _half(outs[a].at[k ^ j], 1 - c)
                copy(a, 2 + j, passed, passed, sibling).wait_recv()
        for cp in sends:
            cp.wait_send()
        for cp in started:
            cp.wait()

    return pl.pallas_call(
        body, name="weights_all_gather",
        in_specs=[any_spec] * na, out_specs=[any_spec] * na,
        out_shape=[jax.ShapeDtypeStruct((N_CHIP,) + a.shape, a.dtype) for a in shards],
        scratch_shapes=[pltpu.SemaphoreType.DMA((6 * na,)), pltpu.SemaphoreType.DMA((6 * na,)),
                        pltpu.SemaphoreType.DMA((na,))],
    )(*shards)


def _rows(ref, h):
    n = ref.shape[1] // 2
    return ref.at[:, pl.ds(h * n, n), :]


def _exchange_core_halves(grads):
    na = len(grads)
    any_spec = pl.BlockSpec(memory_space=pl.ANY)
    shapes = [jax.ShapeDtypeStruct((N_CHIP, g.shape[1] // 2, g.shape[2]), g.dtype) for g in grads]

    def body(*refs):
        ins, mine, theirs = refs[:na], refs[na:2 * na], refs[2 * na:3 * na]
        send_sem, recv_sem, local_sem = refs[3 * na:]
        x, y, c, _ = _place()
        started = []
        for a in range(na):
            loc = pltpu.make_async_copy(_rows(ins[a], c), mine[a], local_sem.at[a])
            rem = pltpu.make_async_remote_copy(src_ref=_rows(ins[a], 1 - c), dst_ref=theirs[a], send_sem=send_sem.at[a],
                                               recv_sem=recv_sem.at[a], device_id=(x, y, 1 - c), device_id_type=MESH)
            loc.start()
            rem.start()
            started.append((loc, rem))
        for loc, rem in started:
            rem.wait()
            loc.wait()

    outs = pl.pallas_call(
        body, name="grads_to_sibling",
        in_specs=[any_spec] * na, out_specs=[any_spec] * (2 * na), out_shape=shapes + shapes,
        scratch_shapes=[pltpu.SemaphoreType.DMA((na,)), pltpu.SemaphoreType.DMA((na,)), pltpu.SemaphoreType.DMA((na,))],
    )(*grads)
    return outs[:na], outs[na:]


def _exchange_chip_blocks(halves):
    na = len(halves)
    any_spec = pl.BlockSpec(memory_space=pl.ANY)

    def body(*refs):
        ins, outs = refs[:na], refs[na:2 * na]
        send_sem, recv_sem, local_sem = refs[2 * na:]
        _, _, c, k = _place()
        started = []
        for a in range(na):
            loc = pltpu.make_async_copy(ins[a].at[k], outs[a].at[k], local_sem.at[a])
            loc.start()
            started.append(loc)
        sends = []
        for a in range(na):
            for j in range(1, N_CHIP):
                cp = pltpu.make_async_remote_copy(
                    src_ref=ins[a].at[k ^ j], dst_ref=outs[a].at[k], send_sem=send_sem.at[a * 3 + j - 1],
                    recv_sem=recv_sem.at[a * 3 + j - 1], device_id=_chip(k ^ j, c), device_id_type=MESH)
                cp.start()
                sends.append(cp)
        for a in range(na):
            for j in range(1, N_CHIP):
                pltpu.make_async_remote_copy(
                    src_ref=ins[a].at[k], dst_ref=outs[a].at[k ^ j], send_sem=send_sem.at[a * 3 + j - 1],
                    recv_sem=recv_sem.at[a * 3 + j - 1], device_id=_chip(k ^ j, c), device_id_type=MESH).wait_recv()
        for cp in sends:
            cp.wait_send()
        for loc in started:
            loc.wait()

    return pl.pallas_call(
        body, name="grads_to_chips",
        in_specs=[any_spec] * na, out_specs=[any_spec] * na,
        out_shape=[jax.ShapeDtypeStruct(h.shape, h.dtype) for h in halves],
        scratch_shapes=[pltpu.SemaphoreType.DMA((3 * na,)), pltpu.SemaphoreType.DMA((3 * na,)),
                        pltpu.SemaphoreType.DMA((na,))],
    )(*halves)


def _join_core_halves(halves):
    na = len(halves)
    any_spec = pl.BlockSpec(memory_space=pl.ANY)

    def body(*refs):
        ins, outs = refs[:na], refs[na:2 * na]
        send_sem, recv_sem, local_sem = refs[2 * na:]
        x, y, c, _ = _place()
        started = []
        for a in range(na):
            loc = pltpu.make_async_copy(ins[a], _half(outs[a], c), local_sem.at[a])
            rem = pltpu.make_async_remote_copy(src_ref=ins[a], dst_ref=_half(outs[a], c), send_sem=send_sem.at[a],
                                               recv_sem=recv_sem.at[a], device_id=(x, y, 1 - c), device_id_type=MESH)
            loc.start()
            rem.start()
            started.append((loc, rem))
        for a, (loc, rem) in enumerate(started):
            rem.wait_send()
            pltpu.make_async_remote_copy(src_ref=ins[a], dst_ref=_half(outs[a], 1 - c), send_sem=send_sem.at[a],
                                         recv_sem=recv_sem.at[a], device_id=(x, y, 1 - c), device_id_type=MESH).wait_recv()
            loc.wait()

    return pl.pallas_call(
        body, name="grads_join",
        in_specs=[any_spec] * na, out_specs=[any_spec] * na,
        out_shape=[jax.ShapeDtypeStruct((2 * h.shape[0],) + h.shape[1:], h.dtype) for h in halves],
        scratch_shapes=[pltpu.SemaphoreType.DMA((na,)), pltpu.SemaphoreType.DMA((na,)), pltpu.SemaphoreType.DMA((na,))],
    )(*halves)


def _elementwise(fn, name, ins, n_out, rows):
    total, cols = ins[0].shape
    spec = pl.BlockSpec((rows, cols), lambda i: (i, 0))

    def body(*refs):
        res = fn(*[r[...] for r in refs[:len(ins)]])
        for o, v in zip(refs[len(ins):], res):
            o[...] = v

    return pl.pallas_call(
        body, name=name, grid=(total // rows,),
        in_specs=[spec] * len(ins), out_specs=[spec] * n_out,
        out_shape=[jax.ShapeDtypeStruct((total, cols), F32)] * n_out,
        compiler_params=_params(48),
    )(*ins)


def _sum_chips(t, name):
    n, r, c = t.shape

    def body(t_ref, o_ref):
        o_ref[...] = ((t_ref[0] + t_ref[1]) + t_ref[2]) + t_ref[3]

    return pl.pallas_call(
        body, name=name, grid=(2,),
        in_specs=[pl.BlockSpec((n, r // 2, c), lambda i: (0, i, 0))],
        out_specs=pl.BlockSpec((r // 2, c), lambda i: (i, 0)),
        out_shape=jax.ShapeDtypeStruct((r, c), F32),
        compiler_params=_params(48),
    )(t)


def _adamw(w, g, m, v):
    m = ADAM_B1 * m + (1.0 - ADAM_B1) * g
    v = ADAM_B2 * v + (1.0 - ADAM_B2) * (g * g)
    m_hat = m / (1.0 - ADAM_B1 ** ADAM_STEP)
    v_hat = v / (1.0 - ADAM_B2 ** ADAM_STEP)
    delta = -ADAM_LR * (m_hat / (jnp.sqrt(v_hat) + ADAM_EPS) + ADAM_WD * w)
    return delta, m, v


def _reduce_and_update(grads, weights, moms, vels):
    mine, theirs = _exchange_core_halves(grads)
    halves = []
    for a, (p, q) in enumerate(zip(mine, theirs)):
        n, r, c = p.shape
        (h,) = _elementwise(lambda u, v: (u + v,), f"grads_add_cores_{a}", [p.reshape(n * r, c), q.reshape(n * r, c)], 1, r)
        halves.append(h.reshape(n, r, c))
    landed = _exchange_chip_blocks(halves)
    reduced = [_sum_chips(t, f"grads_add_chips_{a}") for a, t in enumerate(landed)]
    full = _join_core_halves(reduced)
    out = []
    for a, (g, w, m, v) in enumerate(zip(full, weights, moms, vels)):
        rows = g.shape[0] // 2
        out.append((g,) + tuple(_elementwise(lambda gg, ww, mm, vv: _adamw(ww, gg, mm, vv), f"adamw_{a}", [g, w, m, v], 3, rows)))
    return out


def _reduce_vectors(part, w, m, v):
    n_dev = 8

    def body(p_ref, w_ref, m_ref, v_ref, g_ref, d_ref, nm_ref, nv_ref, buf, send_sem, recv_sem):
        x, y, c, _ = _place()
        me = 4 * x + 2 * y + c
        buf[me] = p_ref[...]
        sends = []
        for off in range(1, n_dev):
            peer = me ^ off
            cp = pltpu.make_async_remote_copy(src_ref=p_ref, dst_ref=buf.at[me], send_sem=send_sem.at[off - 1],
                                              recv_sem=recv_sem.at[off - 1], device_id=(peer >> 2, (peer >> 1) & 1, peer & 1),
                                              device_id_type=MESH)
            cp.start()
            sends.append(cp)
        for off in range(1, n_dev):
            peer = me ^ off
            pltpu.make_async_remote_copy(src_ref=p_ref, dst_ref=buf.at[peer], send_sem=send_sem.at[off - 1],
                                         recv_sem=recv_sem.at[off - 1], device_id=(peer >> 2, (peer >> 1) & 1, peer & 1),
                                         device_id_type=MESH).wait_recv()
        for cp in sends:
            cp.wait_send()
        g = buf[0]
        for d in range(1, n_dev):
            g = g + buf[d]
        g_ref[...] = g
        delta, nm, nv = _adamw(w_ref[...], g, m_ref[...], v_ref[...])
        d_ref[...] = delta
        nm_ref[...] = nm
        nv_ref[...] = nv

    vm = pl.BlockSpec(memory_space=pltpu.VMEM)
    return pl.pallas_call(
        body, name="gains_all_reduce",
        in_specs=[vm] * 4, out_specs=[vm] * 4,
        out_shape=[jax.ShapeDtypeStruct(part.shape, F32)] * 4,
        scratch_shapes=[pltpu.VMEM((n_dev,) + part.shape, F32), pltpu.SemaphoreType.DMA((n_dev - 1,)),
                        pltpu.SemaphoreType.DMA((n_dev - 1,))],
    )(part, w, m, v)


def _pad_row(a):
    a = a.reshape(1, -1)
    return jnp.pad(a, ((0, 0), (0, D_MODEL - a.shape[1])))


def kernel(x, ffn1_norm, ffn1_w_gate, ffn1_w_up, ffn1_w_down, mix_norm, w_in, sb_out_norm, dil_out_norm, w_out, ffn2_norm, ffn2_w_gate, ffn2_w_up, ffn2_w_down, final_norm, loss_target, m_ffn1_norm, m_ffn1_w_gate, m_ffn1_w_up, m_ffn1_w_down, m_mix_norm, m_w_in, m_sb_out_norm, m_dil_out_norm, m_w_out, m_ffn2_norm, m_ffn2_w_gate, m_ffn2_w_up, m_ffn2_w_down, m_final_norm, v_ffn1_norm, v_ffn1_w_gate, v_ffn1_w_up, v_ffn1_w_down, v_mix_norm, v_w_in, v_sb_out_norm, v_dil_out_norm, v_w_out, v_ffn2_norm, v_ffn2_w_gate, v_ffn2_w_up, v_ffn2_w_down, v_final_norm):
    x = x[0]
    target = loss_target[0]
    s = x.shape[0]
    gf = final_norm.reshape(1, D_MODEL)
    cos, sin = _rope_tables(s)

    gu_shard = jnp.stack([ffn1_w_gate[0], ffn1_w_up[0], ffn2_w_gate[0], ffn2_w_up[0]]).astype(BF16)
    wd_shard = jnp.stack([ffn1_w_down[0], ffn2_w_down[0]]).astype(BF16)
    gu, wd, win, wout = _all_gather([gu_shard, wd_shard, w_in[0].astype(BF16), w_out[0].astype(BF16)])
    wout = wout.reshape(D_MODEL, D_MODEL)

    x1, hm = _ffn1_fwd(x, ffn1_norm, mix_norm, gu, wd)
    qkv = _proj_fwd(hm, win, cos, sin)
    o_sb = _sb_fwd(qkv)
    pats = [_dil_fwd(qkv, d) for d in DILATIONS]
    o_dl, lse = _dil_combine([p[0] for p in pats], [p[1] for p in pats])
    x2 = _outproj_fwd(o_sb, o_dl, sb_out_norm, dil_out_norm, x1, wout)
    dx3, st_final = _ffn2_fwd_loss(x2, ffn2_norm, gf, target, gu, wd)

    dx2, dwg2, dwu2, dwd2, st_ffn2 = _ffn_bwd(x2, ffn2_norm, dx3, gu, wd, 1)
    do_sb, do_dl, dwout, st_out = _outproj_bwd(dx2, o_sb, o_dl, sb_out_norm, dil_out_norm, wout)
    dq_sb, dk_sb, dv_sb = _sb_bwd(qkv, o_sb, do_sb)
    dq_dl, dk_dl, dv_dl = _dil_finish([_dil_bwd(qkv, o_dl, lse, do_dl, d) for d in DILATIONS], cos, sin)
    dqkv = jnp.concatenate([dq_sb, dk_sb, dv_sb, dq_dl, dk_dl, dv_dl], axis=1)
    dx1, dwin, st_mix = _proj_bwd(x1, mix_norm, dqkv, win, dx2)
    grad_x, dwg1, dwu1, dwd1, st_ffn1 = _ffn_bwd(x, ffn1_norm, dx1, gu, wd, 0)

    names = ["ffn1_w_gate", "ffn1_w_up", "ffn1_w_down", "w_in", "w_out", "ffn2_w_gate", "ffn2_w_up", "ffn2_w_down"]
    grads = [dwg1, dwu1, dwd1, dwin, dwout.reshape(N_CHIP, OUTB, D_MODEL), dwg2, dwu2, dwd2]
    weights = [ffn1_w_gate[0], ffn1_w_up[0], ffn1_w_down[0], w_in[0], w_out[0], ffn2_w_gate[0], ffn2_w_up[0], ffn2_w_down[0]]
    moms = [m_ffn1_w_gate[0], m_ffn1_w_up[0], m_ffn1_w_down[0], m_w_in[0], m_w_out[0], m_ffn2_w_gate[0], m_ffn2_w_up[0], m_ffn2_w_down[0]]
    vels = [v_ffn1_w_gate[0], v_ffn1_w_up[0], v_ffn1_w_down[0], v_w_in[0], v_w_out[0], v_ffn2_w_gate[0], v_ffn2_w_up[0], v_ffn2_w_down[0]]
    mats = {n: tuple(t[None] for t in r) for n, r in zip(names, _reduce_and_update(grads, weights, moms, vels))}

    vec_names = ["ffn1_norm", "mix_norm", "sb_out_norm", "dil_out_norm", "ffn2_norm", "final_norm"]
    part = jnp.concatenate([st_ffn1[0:1], st_mix[0:1], _pad_row(st_out[0]), _pad_row(st_out[1]), st_ffn2[0:1],
                            st_final[0:1], st_final[1:2], jnp.zeros((1, D_MODEL), F32)], axis=0)
    pack = lambda arrs: jnp.concatenate([_pad_row(a) for a in arrs] + [jnp.zeros((2, D_MODEL), F32)], axis=0)
    g_vec, d_vec, m_vec, v_vec = _reduce_vectors(
        part,
        pack([ffn1_norm, mix_norm, sb_out_norm, dil_out_norm, ffn2_norm, final_norm]),
        pack([m_ffn1_norm, m_mix_norm, m_sb_out_norm, m_dil_out_norm, m_ffn2_norm, m_final_norm]),
        pack([v_ffn1_norm, v_mix_norm, v_sb_out_norm, v_dil_out_norm, v_ffn2_norm, v_final_norm]))
    like = {"ffn1_norm": ffn1_norm, "mix_norm": mix_norm, "sb_out_norm": sb_out_norm, "dil_out_norm": dil_out_norm,
            "ffn2_norm": ffn2_norm, "final_norm": final_norm}
    vecs = {n: tuple(t[i, :like[n].size].reshape(like[n].shape) for t in (g_vec, d_vec, m_vec, v_vec))
            for i, n in enumerate(vec_names)}
    loss = 0.5 * jnp.sum(g_vec[6]) / D_MODEL

    order = ["ffn1_norm", "ffn1_w_gate", "ffn1_w_up", "ffn1_w_down", "mix_norm", "w_in", "sb_out_norm", "dil_out_norm",
             "w_out", "ffn2_norm", "ffn2_w_gate", "ffn2_w_up", "ffn2_w_down", "final_norm"]
    both = {**mats, **vecs}
    return (loss, grad_x[None], *[both[n][0] for n in order], *[both[n][1] for n in order],
            *[both[n][2] for n in order], *[both[n][3] for n in order])
```

```python
import functools

import jax
import jax.numpy as jnp
from jax import lax
from jax.experimental import pallas as pl
from jax.experimental.pallas import tpu as pltpu

D_MODEL = 1024
D_FF = 2816
HEAD_DIM = 64
D_SB = 512
D_DIL = 512
D_IN = 3072
N_CHIP = 4
FFB = D_FF // N_CHIP
INB = D_IN // N_CHIP
OUTB = D_MODEL // N_CHIP
BLK = 128
LANES = 128
DILATIONS = (1, 4, 16)
ROPE_THETA = 10000.0
RMS_EPS = 1e-6
SCALE = HEAD_DIM ** -0.5
NEG = -1e30
DEAD = -104.0
ADAM_LR = 0.001
ADAM_B1 = 0.9
ADAM_B2 = 0.999
ADAM_EPS = 1e-08
ADAM_WD = 0.01
ADAM_STEP = 10
MESH = pl.DeviceIdType.MESH
F32 = jnp.float32
BF16 = jnp.bfloat16
TM = 512


def _params(vmem_mb):
    return pltpu.CompilerParams(vmem_limit_bytes=vmem_mb << 20)


def _dot(a, b):
    return jnp.dot(a, b, preferred_element_type=F32)


def _dot_nt(a, b):
    return lax.dot_general(a, b, (((1,), (1,)), ((), ())), preferred_element_type=F32)


def _dot_tn(a, b):
    return lax.dot_general(a, b, (((0,), (0,)), ((), ())), preferred_element_type=F32)


def _rms_fwd(x, g):
    r = lax.rsqrt(jnp.mean(x * x, axis=-1, keepdims=True) + RMS_EPS)
    xh = x * r
    return xh * g, xh, r


def _rms_bwd(dy, xh, r, g):
    dyg = dy * g
    dx = r * (dyg - xh * jnp.mean(dyg * xh, axis=-1, keepdims=True))
    return dx, jnp.sum(dy * xh, axis=0, keepdims=True)


def _split_bf16(a):
    hi = a.astype(BF16)
    return hi, (a - hi.astype(F32)).astype(BF16)


def _dot_split(a, b):
    hi, lo = _split_bf16(a)
    return _dot(hi, b) + _dot(lo, b)


def _ffn_weight_specs(f):
    return [pl.BlockSpec((None, None, D_MODEL, FFB), lambda i, j: (j, 2 * f, 0, 0)),
            pl.BlockSpec((None, None, D_MODEL, FFB), lambda i, j: (j, 2 * f + 1, 0, 0)),
            pl.BlockSpec((None, None, FFB, D_MODEL), lambda i, j: (j, f, 0, 0))]


def _ffn_accumulate(h_scr, acc_scr, wg_ref, wu_ref, wd_ref):
    h = h_scr[...]
    a = _dot(h, wg_ref[...])
    b = _dot(h, wu_ref[...])
    act = (a * jax.nn.sigmoid(a)) * b
    acc_scr[...] += _dot(act.astype(BF16), wd_ref[...])


def _ffn1_fwd(x, g1, gmix, gu, wd):
    s = x.shape[0]
    row = pl.BlockSpec((TM, D_MODEL), lambda i, j: (i, 0))
    vec = pl.BlockSpec((1, D_MODEL), lambda i, j: (0, 0))

    def body(x_ref, g_ref, gm_ref, wg_ref, wu_ref, wd_ref, x1_ref, hm_ref, h_scr, acc_scr):
        j = pl.program_id(1)

        @pl.when(j == 0)
        def _():
            h, _, _ = _rms_fwd(x_ref[...], g_ref[...])
            h_scr[...] = h.astype(BF16)
            acc_scr[...] = jnp.zeros_like(acc_scr)

        _ffn_accumulate(h_scr, acc_scr, wg_ref, wu_ref, wd_ref)

        @pl.when(j == N_CHIP - 1)
        def _():
            x1 = x_ref[...] + 0.5 * acc_scr[...]
            x1_ref[...] = x1
            hm, _, _ = _rms_fwd(x1, gm_ref[...])
            hm_ref[...] = hm.astype(BF16)

    return pl.pallas_call(
        body, name="ffn1_fwd", grid=(s // TM, N_CHIP),
        in_specs=[row, vec, vec] + _ffn_weight_specs(0),
        out_specs=[row, row],
        out_shape=[jax.ShapeDtypeStruct((s, D_MODEL), F32), jax.ShapeDtypeStruct((s, D_MODEL), BF16)],
        scratch_shapes=[pltpu.VMEM((TM, D_MODEL), BF16), pltpu.VMEM((TM, D_MODEL), F32)],
        compiler_params=_params(48),
    )(x, g1, gmix, gu, gu, wd)


def _ffn2_fwd_loss(x2, g2, gf, target, gu, wd):
    s = x2.shape[0]
    row = pl.BlockSpec((TM, D_MODEL), lambda i, j: (i, 0))
    vec = pl.BlockSpec((1, D_MODEL), lambda i, j: (0, 0))
    stat = pl.BlockSpec((8, D_MODEL), lambda i, j: (0, 0))

    def body(x_ref, g_ref, gf_ref, t_ref, wg_ref, wu_ref, wd_ref, dx_ref, st_ref, h_scr, acc_scr):
        i, j = pl.program_id(0), pl.program_id(1)

        @pl.when((i == 0) & (j == 0))
        def _():
            st_ref[...] = jnp.zeros_like(st_ref)

        @pl.when(j == 0)
        def _():
            h, _, _ = _rms_fwd(x_ref[...], g_ref[...])
            h_scr[...] = h.astype(BF16)
            acc_scr[...] = jnp.zeros_like(acc_scr)

        _ffn_accumulate(h_scr, acc_scr, wg_ref, wu_ref, wd_ref)

        @pl.when(j == N_CHIP - 1)
        def _():
            x3 = x_ref[...] + 0.5 * acc_scr[...]
            y, xh, r = _rms_fwd(x3, gf_ref[...])
            err = y - t_ref[...]
            dx, dg = _rms_bwd(err * (1.0 / D_MODEL), xh, r, gf_ref[...])
            dx_ref[...] = dx
            st_ref[0:1, :] += dg
            st_ref[1:2, :] += jnp.sum(err * err, axis=0, keepdims=True)

    return pl.pallas_call(
        body, name="ffn2_fwd_loss", grid=(s // TM, N_CHIP),
        in_specs=[row, vec, vec, row] + _ffn_weight_specs(1),
        out_specs=[row, stat],
        out_shape=[jax.ShapeDtypeStruct((s, D_MODEL), F32), jax.ShapeDtypeStruct((8, D_MODEL), F32)],
        scratch_shapes=[pltpu.VMEM((TM, D_MODEL), BF16), pltpu.VMEM((TM, D_MODEL), F32)],
        compiler_params=_params(48),
    )(x2, g2, gf, target, gu, gu, wd)


def _ffn_bwd(xin, g, dy, gu, wd, f):
    s = xin.shape[0]
    row = pl.BlockSpec((TM, D_MODEL), lambda i, j: (i, 0))
    vec = pl.BlockSpec((1, D_MODEL), lambda i, j: (0, 0))
    stat = pl.BlockSpec((8, D_MODEL), lambda i, j: (0, 0))
    hid = pl.BlockSpec((None, TM, FFB), lambda i, j: (j, i, 0))

    def body(x_ref, g_ref, dy_ref, wg_ref, wu_ref, wd_ref, out_ref, h_ref, dyh_ref, da_ref, db_ref, act_ref, st_ref, dh_scr):
        i, j = pl.program_id(0), pl.program_id(1)

        @pl.when((i == 0) & (j == 0))
        def _():
            st_ref[...] = jnp.zeros_like(st_ref)

        @pl.when(j == 0)
        def _():
            h, _, _ = _rms_fwd(x_ref[...], g_ref[...])
            h_ref[...] = h.astype(BF16)
            dyh_ref[...] = (0.5 * dy_ref[...]).astype(BF16)
            dh_scr[...] = jnp.zeros_like(dh_scr)

        hb = h_ref[...]
        a = _dot(hb, wg_ref[...])
        b = _dot(hb, wu_ref[...])
        sg = jax.nn.sigmoid(a)
        sa = a * sg
        dact = _dot_nt(dyh_ref[...], wd_ref[...])
        dab = (dact * b * (sg * (1.0 + a * (1.0 - sg)))).astype(BF16)
        dbb = (dact * sa).astype(BF16)
        da_ref[...] = dab
        db_ref[...] = dbb
        act_ref[...] = (sa * b).astype(BF16)
        dh_scr[...] += _dot_nt(dab, wg_ref[...]) + _dot_nt(dbb, wu_ref[...])

        @pl.when(j == N_CHIP - 1)
        def _():
            _, xh, r = _rms_fwd(x_ref[...], g_ref[...])
            dx, dg = _rms_bwd(dh_scr[...], xh, r, g_ref[...])
            out_ref[...] = dy_ref[...] + dx
            st_ref[0:1, :] += dg

    hidden = jax.ShapeDtypeStruct((N_CHIP, s, FFB), BF16)
    dx, hb, dyh, da, db, act, st = pl.pallas_call(
        body, name=f"ffn{f + 1}_bwd_dx", grid=(s // TM, N_CHIP),
        in_specs=[row, vec, row] + _ffn_weight_specs(f),
        out_specs=[row, row, row, hid, hid, hid, stat],
        out_shape=[jax.ShapeDtypeStruct((s, D_MODEL), F32), jax.ShapeDtypeStruct((s, D_MODEL), BF16),
                   jax.ShapeDtypeStruct((s, D_MODEL), BF16), hidden, hidden, hidden,
                   jax.ShapeDtypeStruct((8, D_MODEL), F32)],
        scratch_shapes=[pltpu.VMEM((TM, D_MODEL), F32)],
        compiler_params=_params(56),
    )(xin, g, dy, gu, gu, wd)

    tok = pl.BlockSpec((TM, D_MODEL), lambda j, i: (i, 0))
    hid2 = pl.BlockSpec((None, TM, FFB), lambda j, i: (j, i, 0))
    gspecs = [pl.BlockSpec((None, D_MODEL, FFB), lambda j, i: (j, 0, 0)),
              pl.BlockSpec((None, D_MODEL, FFB), lambda j, i: (j, 0, 0)),
              pl.BlockSpec((None, FFB, D_MODEL), lambda j, i: (j, 0, 0))]

    def wbody(h_ref, dyh_ref, da_ref, db_ref, act_ref, dwg_ref, dwu_ref, dwd_ref):
        @pl.when(pl.program_id(1) == 0)
        def _():
            dwg_ref[...] = jnp.zeros_like(dwg_ref)
            dwu_ref[...] = jnp.zeros_like(dwu_ref)
            dwd_ref[...] = jnp.zeros_like(dwd_ref)

        hb = h_ref[...]
        dwg_ref[...] += _dot_tn(hb, da_ref[...])
        dwu_ref[...] += _dot_tn(hb, db_ref[...])
        dwd_ref[...] += _dot_tn(act_ref[...], dyh_ref[...])

    dwg, dwu, dwd = pl.pallas_call(
        wbody, name=f"ffn{f + 1}_bwd_dw", grid=(N_CHIP, s // TM),
        in_specs=[tok, tok, hid2, hid2, hid2], out_specs=gspecs,
        out_shape=[jax.ShapeDtypeStruct((N_CHIP, D_MODEL, FFB), F32),
                   jax.ShapeDtypeStruct((N_CHIP, D_MODEL, FFB), F32),
                   jax.ShapeDtypeStruct((N_CHIP, FFB, D_MODEL), F32)],
        compiler_params=_params(48),
    )(hb, dyh, da, db, act)
    return dx, dwg, dwu, dwd, st


def _rope_tables(s):
    half = HEAD_DIM // 2
    inv_freq = ROPE_THETA ** (-jnp.arange(half, dtype=F32) / half)
    ang = jnp.arange(s).astype(F32)[:, None] * inv_freq[None, :]
    cos, sin = jnp.cos(ang), jnp.sin(ang)
    cos2 = jnp.concatenate([cos, cos], axis=-1)
    sin2 = jnp.concatenate([-sin, sin], axis=-1)
    return jnp.tile(cos2, (1, LANES // HEAD_DIM)), jnp.tile(sin2, (1, LANES // HEAD_DIM))


def _rotate(t, cos, sin_signed):
    lane = lax.broadcasted_iota(jnp.int32, t.shape, 1)
    first = (lane % HEAD_DIM) < (HEAD_DIM // 2)
    partner = jnp.where(first, pltpu.roll(t, LANES - HEAD_DIM // 2, 1), pltpu.roll(t, HEAD_DIM // 2, 1))
    return t * cos + partner * sin_signed


def _proj_fwd(hm, win, cos, sin):
    s = hm.shape[0]
    n_sub = INB // LANES
    first_rot, last_rot = (3 * D_SB) // LANES, (3 * D_SB + 2 * D_DIL) // LANES

    def body(h_ref, w_ref, c_ref, s_ref, o_ref):
        j = pl.program_id(1)
        r = _dot(h_ref[...], w_ref[...])
        for c in range(n_sub):
            t = r[:, c * LANES:(c + 1) * LANES]
            col = j * n_sub + c
            rot = (col >= first_rot) & (col < last_rot)
            lanes = slice(c * LANES, (c + 1) * LANES)

            @pl.when(rot)
            def _():
                o_ref[:, lanes] = _rotate(t, c_ref[...], s_ref[...]).astype(BF16)

            @pl.when(jnp.logical_not(rot))
            def _():
                o_ref[:, lanes] = t.astype(BF16)

    return pl.pallas_call(
        body, name="proj_fwd", grid=(s // TM, N_CHIP),
        in_specs=[pl.BlockSpec((TM, D_MODEL), lambda i, j: (i, 0)),
                  pl.BlockSpec((None, D_MODEL, INB), lambda i, j: (j, 0, 0)),
                  pl.BlockSpec((TM, LANES), lambda i, j: (i, 0)),
                  pl.BlockSpec((TM, LANES), lambda i, j: (i, 0))],
        out_specs=pl.BlockSpec((TM, INB), lambda i, j: (i, j)),
        out_shape=jax.ShapeDtypeStruct((s, D_IN), BF16),
        compiler_params=_params(32),
    )(hm, win, cos, sin)


def _proj_bwd(x1, gmix, dqkv, win, dx2):
    s = x1.shape[0]
    row = pl.BlockSpec((TM, D_MODEL), lambda i, j: (i, 0))
    vec = pl.BlockSpec((1, D_MODEL), lambda i, j: (0, 0))

    def body(x_ref, g_ref, dq_ref, w_ref, dx2_ref, out_ref, dw_ref, st_ref, h_scr, dh_scr):
        i, j = pl.program_id(0), pl.program_id(1)

        @pl.when((i == 0) & (j == 0))
        def _():
            st_ref[...] = jnp.zeros_like(st_ref)
            dw_ref[...] = jnp.zeros_like(dw_ref)

        @pl.when(j == 0)
        def _():
            h, _, _ = _rms_fwd(x_ref[...], g_ref[...])
            h_scr[...] = h.astype(BF16)
            dh_scr[...] = jnp.zeros_like(dh_scr)

        dq = dq_ref[...]
        dw_ref[j] += _dot_tn(h_scr[...], dq)
        dh_scr[...] += _dot_nt(dq, w_ref[...])

        @pl.when(j == N_CHIP - 1)
        def _():
            _, xh, r = _rms_fwd(x_ref[...], g_ref[...])
            dx, dg = _rms_bwd(dh_scr[...], xh, r, g_ref[...])
            out_ref[...] = dx2_ref[...] + dx
            st_ref[0:1, :] += dg

    return pl.pallas_call(
        body, name="proj_bwd", grid=(s // TM, N_CHIP),
        in_specs=[row, vec, pl.BlockSpec((TM, INB), lambda i, j: (i, j)),
                  pl.BlockSpec((None, D_MODEL, INB), lambda i, j: (j, 0, 0)), row],
        out_specs=[row, pl.BlockSpec((N_CHIP, D_MODEL, INB), lambda i, j: (0, 0, 0)),
                   pl.BlockSpec((8, D_MODEL), lambda i, j: (0, 0))],
        out_shape=[jax.ShapeDtypeStruct((s, D_MODEL), F32),
                   jax.ShapeDtypeStruct((N_CHIP, D_MODEL, INB), F32),
                   jax.ShapeDtypeStruct((8, D_MODEL), F32)],
        scratch_shapes=[pltpu.VMEM((TM, D_MODEL), BF16), pltpu.VMEM((TM, D_MODEL), F32)],
        compiler_params=_params(56),
    )(x1, gmix, dqkv, win, dx2)


def _outproj_fwd(o_sb, o_dl, g_sb, g_dl, x1, wout):
    s = x1.shape[0]
    half = pl.BlockSpec((TM, D_SB), lambda i: (i, 0))
    row = pl.BlockSpec((TM, D_MODEL), lambda i: (i, 0))
    vec = pl.BlockSpec((1, D_SB), lambda i: (0, 0))

    def body(a_ref, b_ref, ga_ref, gb_ref, x_ref, w_ref, o_ref):
        ma, _, _ = _rms_fwd(a_ref[...], ga_ref[...])
        mb, _, _ = _rms_fwd(b_ref[...], gb_ref[...])
        o_ref[...] = (x_ref[...] + _dot(ma.astype(BF16), w_ref[0:D_SB, :])
                      + _dot(mb.astype(BF16), w_ref[D_SB:D_MODEL, :]))

    return pl.pallas_call(
        body, name="outproj_fwd", grid=(s // TM,),
        in_specs=[half, half, vec, vec, row, pl.BlockSpec((D_MODEL, D_MODEL), lambda i: (0, 0))],
        out_specs=row, out_shape=jax.ShapeDtypeStruct((s, D_MODEL), F32),
        compiler_params=_params(32),
    )(o_sb, o_dl, g_sb, g_dl, x1, wout)


def _outproj_bwd(dx2, o_sb, o_dl, g_sb, g_dl, wout):
    s = dx2.shape[0]
    half = pl.BlockSpec((TM, D_SB), lambda i: (i, 0))
    row = pl.BlockSpec((TM, D_MODEL), lambda i: (i, 0))
    vec = pl.BlockSpec((1, D_SB), lambda i: (0, 0))
    full = pl.BlockSpec((D_MODEL, D_MODEL), lambda i: (0, 0))

    def body(dy_ref, a_ref, b_ref, ga_ref, gb_ref, w_ref, da_ref, db_ref, dw_ref, st_ref):
        @pl.when(pl.program_id(0) == 0)
        def _():
            dw_ref[...] = jnp.zeros_like(dw_ref)
            st_ref[...] = jnp.zeros_like(st_ref)

        dy = dy_ref[...].astype(BF16)
        dm = _dot_nt(dy, w_ref[...])
        ma, xa, ra = _rms_fwd(a_ref[...], ga_ref[...])
        mb, xb, rb = _rms_fwd(b_ref[...], gb_ref[...])
        dw_ref[0:D_SB, :] += _dot_tn(ma.astype(BF16), dy)
        dw_ref[D_SB:D_MODEL, :] += _dot_tn(mb.astype(BF16), dy)
        da, dga = _rms_bwd(dm[:, 0:D_SB], xa, ra, ga_ref[...])
        db, dgb = _rms_bwd(dm[:, D_SB:D_MODEL], xb, rb, gb_ref[...])
        da_ref[...] = da
        db_ref[...] = db
        st_ref[0:1, :] += dga
        st_ref[1:2, :] += dgb

    return pl.pallas_call(
        body, name="outproj_bwd", grid=(s // TM,),
        in_specs=[row, half, half, vec, vec, full],
        out_specs=[half, half, full, pl.BlockSpec((8, D_SB), lambda i: (0, 0))],
        out_shape=[jax.ShapeDtypeStruct((s, D_SB), F32), jax.ShapeDtypeStruct((s, D_SB), F32),
                   jax.ShapeDtypeStruct((D_MODEL, D_MODEL), F32), jax.ShapeDtypeStruct((8, D_SB), F32)],
        compiler_params=_params(48),
    )(dx2, o_sb, o_dl, g_sb, g_dl, wout)


def _head_masks():
    lane = lax.broadcasted_iota(jnp.int32, (BLK, LANES), 1)
    return [lane < HEAD_DIM, lane >= HEAD_DIM]


def _keep(mask, a):
    return a * jnp.where(mask, 1.0, 0.0).astype(a.dtype)


def _suffix_matrices():
    r = lax.broadcasted_iota(jnp.int32, (BLK, BLK), 0)
    c = lax.broadcasted_iota(jnp.int32, (BLK, BLK), 1)
    ones = jnp.ones((BLK, BLK), BF16)
    excl = jnp.concatenate([(r > c).astype(BF16), ones], axis=1)
    incl = jnp.concatenate([(r >= c).astype(BF16), ones], axis=1)
    return excl, incl


def _blk(i):
    return pl.ds(pl.multiple_of(i * BLK, BLK), BLK)


def _alive(carry_m):
    return (jnp.max(carry_m) > DEAD).astype(jnp.int32)


def _more_keys(i, carry):
    return (carry[0] <= i) & (carry[1] > 0)


def _stack_heads(a):
    masks = _head_masks()
    return jnp.concatenate([_keep(masks[0], a), _keep(masks[1], a)], axis=0)


def _unstack_heads(a2):
    return jnp.where(_head_masks()[0], a2[:BLK], a2[BLK:])


def _head_rowsum(a):
    masks = _head_masks()
    return jnp.concatenate([jnp.sum(jnp.where(m, a, 0.0), axis=1, keepdims=True) for m in masks], axis=0)


def _sb_scores(q2, k, i, j, carry_m, u_excl):
    row = lax.broadcasted_iota(jnp.int32, (2 * BLK, BLK), 0) & (BLK - 1)
    col = lax.broadcasted_iota(jnp.int32, (2 * BLK, BLK), 1)
    valid = (j * BLK + col) < (i * BLK + row)
    z = _dot_nt(q2, k) * SCALE
    sp = jnp.maximum(z, 0.0) + jnp.log(1.0 + jnp.exp(-jnp.abs(z)))
    log_stay = jnp.where(valid, -sp, 0.0)
    log_beta = z - sp
    sums = _dot_split(log_stay, u_excl)
    later = carry_m + sums[:, :BLK]
    w = jnp.where(valid, jnp.exp(log_beta + later), 0.0)
    return valid, log_beta, w, carry_m + sums[:, BLK:]


def _sb_fwd(qkv):
    s = qkv.shape[0]
    nq = s // BLK
    pairs = D_SB // LANES
    col = lambda off: pl.BlockSpec((s, LANES), lambda p: (0, off + p))

    def body(q_ref, k_ref, v_ref, o_ref):
        u_excl, _ = _suffix_matrices()
        zero = jnp.zeros((2 * BLK, LANES), F32)

        def q_block(i, _):
            q2 = _stack_heads(q_ref[_blk(i), :])

            def k_block(carry):
                jj, _, carry_m, acc = carry
                j = i - jj
                _, _, w, carry_m = _sb_scores(q2, k_ref[_blk(j), :], i, j, carry_m, u_excl)
                return jj + 1, _alive(carry_m), carry_m, acc + _dot(w.astype(BF16), v_ref[_blk(j), :])

            _, _, _, acc = lax.while_loop(functools.partial(_more_keys, i), k_block,
                                          (jnp.int32(0), jnp.int32(1), zero, zero))
            o_ref[_blk(i), :] = _unstack_heads(acc)
            return 0

        lax.fori_loop(0, nq, q_block, 0)

    return pl.pallas_call(
        body, name="sb_fwd", grid=(pairs,),
        in_specs=[col(0), col(pairs), col(2 * pairs)],
        out_specs=pl.BlockSpec((s, LANES), lambda p: (0, p)),
        out_shape=jax.ShapeDtypeStruct((s, D_SB), F32),
        compiler_params=_params(48),
    )(qkv, qkv, qkv)


def _sb_bwd(qkv, o_sb, do_sb):
    s = qkv.shape[0]
    nq = s // BLK
    pairs = D_SB // LANES
    col = lambda off: pl.BlockSpec((s, LANES), lambda p: (0, off + p))
    own = pl.BlockSpec((s, LANES), lambda p: (0, p))

    def body(q_ref, k_ref, v_ref, o_ref, do_ref, dq_ref, dk_ref, dv_ref, dk_acc, dv_acc):
        u_excl, u_incl = _suffix_matrices()
        zero = jnp.zeros((2 * BLK, LANES), F32)
        dk_acc[...] = jnp.zeros_like(dk_acc)
        dv_acc[...] = jnp.zeros_like(dv_acc)

        def q_block(i, _):
            q2 = _stack_heads(q_ref[_blk(i), :])
            do = do_ref[_blk(i), :].astype(BF16)
            do2 = _stack_heads(do)
            total = jnp.broadcast_to(_head_rowsum(do.astype(F32) * o_ref[_blk(i), :]), (2 * BLK, BLK))

            def k_block(carry):
                jj, _, carry_m, carry_g, dq = carry
                j = i - jj
                k = k_ref[_blk(j), :]
                valid, log_beta, w, carry_m = _sb_scores(q2, k, i, j, carry_m, u_excl)
                wb = w.astype(BF16)
                g = wb.astype(F32) * _dot_nt(do2, v_ref[_blk(j), :])
                sums = _dot_split(g, u_incl)
                before = total - (carry_g + sums[:, :BLK])
                dz = jnp.where(valid, g - jnp.exp(log_beta) * (g + before), 0.0)
                dzb = (dz * SCALE).astype(BF16)
                dk_acc[_blk(j), :] += _dot_tn(dzb, q2)
                dv_acc[_blk(j), :] += _dot_tn(wb, do2)
                return jj + 1, _alive(carry_m), carry_m, carry_g + sums[:, BLK:], dq + _dot(dzb, k)

            _, _, _, _, dq = lax.while_loop(functools.partial(_more_keys, i), k_block,
                                            (jnp.int32(0), jnp.int32(1), zero, zero, zero))
            dq_ref[_blk(i), :] = _unstack_heads(dq).astype(BF16)
            return 0

        lax.fori_loop(0, nq, q_block, 0)
        dk_ref[...] = dk_acc[...].astype(BF16)
        dv_ref[...] = dv_acc[...].astype(BF16)

    return pl.pallas_call(
        body, name="sb_bwd", grid=(pairs,),
        in_specs=[col(0), col(pairs), col(2 * pairs), own, own],
        out_specs=[own, own, own],
        out_shape=[jax.ShapeDtypeStruct((s, D_SB), BF16)] * 3,
        scratch_shapes=[pltpu.VMEM((s, LANES), F32), pltpu.VMEM((s, LANES), F32)],
        compiler_params=_params(56),
    )(qkv, qkv, qkv, o_sb, do_sb)


def _class_view(a, d):
    return a.reshape(a.shape[0] // d, d * a.shape[1])


def _natural_view(a, d):
    return a.reshape(a.shape[0] * d, a.shape[1] // d)


def _band_masks(b):
    row = lax.broadcasted_iota(jnp.int32, (2 * BLK, BLK), 0) & (BLK - 1)
    col = lax.broadcasted_iota(jnp.int32, (2 * BLK, BLK), 1)
    return col <= row, (col - row) >= jnp.where(b > 0, 0, BLK)


def _dil_specs(n, d, width_blocks, offsets):
    mode = {"pipeline_mode": pl.Buffered(1)} if d == 1 else {}
    return [pl.BlockSpec((n, LANES), functools.partial(lambda p, c, off: (0, c * width_blocks + off + p), off=off), **mode)
            for off in offsets]


def _dil_fwd(qkv, d):
    s = qkv.shape[0]
    n = s // d
    nb = n // BLK
    pairs = D_DIL // LANES
    base = (3 * D_SB) // LANES
    in_specs = _dil_specs(n, d, D_IN // LANES, (base, base + pairs, base + 2 * pairs))
    out_spec = _dil_specs(n, d, pairs, (0,))[0]
    view = _class_view(qkv, d)

    def body(q_ref, k_ref, v_ref, o_ref, l_ref):
        def block(b, _):
            prev = jnp.maximum(b - 1, 0)
            in_cur, in_prev = _band_masks(b)
            q2 = _stack_heads(q_ref[_blk(b), :])
            kc, kp, vc, vp = k_ref[_blk(b), :], k_ref[_blk(prev), :], v_ref[_blk(b), :], v_ref[_blk(prev), :]
            zc = jnp.where(in_cur, _dot_nt(q2, kc) * SCALE, NEG)
            zp = jnp.where(in_prev, _dot_nt(q2, kp) * SCALE, NEG)
            m = jnp.maximum(jnp.max(zc, axis=1, keepdims=True), jnp.max(zp, axis=1, keepdims=True))
            pc, pp = jnp.exp(zc - m), jnp.exp(zp - m)
            den = jnp.sum(pc, axis=1, keepdims=True) + jnp.sum(pp, axis=1, keepdims=True)
            o_ref[_blk(b), :] = _unstack_heads((_dot(pc.astype(BF16), vc) + _dot(pp.astype(BF16), vp)) / den)
            l_ref[_blk(b), :] = _unstack_heads(jnp.broadcast_to(m + jnp.log(den), (2 * BLK, LANES)))
            return 0

        lax.fori_loop(0, nb, block, 0, unroll=2)

    o, lse = pl.pallas_call(
        body, name=f"dil_fwd_{d}", grid=(pairs, d),
        in_specs=in_specs, out_specs=[out_spec, out_spec],
        out_shape=[jax.ShapeDtypeStruct((n, d * D_DIL), F32)] * 2,
        compiler_params=_params(48),
    )(view, view, view)
    return _natural_view(o, d), _natural_view(lse, d)


def _dil_combine(outs, lses):
    s = outs[0].shape[0]
    spec = pl.BlockSpec((TM, D_DIL), lambda i: (i, 0))

    def body(o1, o2, o3, l1, l2, l3, out_ref, lse_ref):
        a, b, c = l1[...], l2[...], l3[...]
        m = jnp.maximum(jnp.maximum(a, b), c)
        ea, eb, ec = jnp.exp(a - m), jnp.exp(b - m), jnp.exp(c - m)
        tot = ea + eb + ec
        out_ref[...] = (ea / tot) * o1[...] + (eb / tot) * o2[...] + (ec / tot) * o3[...]
        lse_ref[...] = m + jnp.log(tot)

    return pl.pallas_call(
        body, name="dil_combine", grid=(s // TM,),
        in_specs=[spec] * 6, out_specs=[spec, spec],
        out_shape=[jax.ShapeDtypeStruct((s, D_DIL), F32)] * 2,
        compiler_params=_params(32),
    )(*outs, *lses)


def _dil_bwd(qkv, out, lse, dout, d):
    s = qkv.shape[0]
    n = s // d
    nb = n // BLK
    pairs = D_DIL // LANES
    base = (3 * D_SB) // LANES
    in_specs = (_dil_specs(n, d, D_IN // LANES, (base, base + pairs, base + 2 * pairs))
                + _dil_specs(n, d, pairs, (0, 0, 0)))
    out_spec = _dil_specs(n, d, pairs, (0,))[0]
    view = _class_view(qkv, d)

    def body(q_ref, k_ref, v_ref, o_ref, l_ref, do_ref, dq_ref, dk_ref, dv_ref):
        masks = _head_masks()
        dk_ref[...] = jnp.zeros_like(dk_ref)
        dv_ref[...] = jnp.zeros_like(dv_ref)

        def block(b, _):
            prev = jnp.maximum(b - 1, 0)
            in_cur, in_prev = _band_masks(b)
            q2 = _stack_heads(q_ref[_blk(b), :])
            kc, kp, vc, vp = k_ref[_blk(b), :], k_ref[_blk(prev), :], v_ref[_blk(b), :], v_ref[_blk(prev), :]
            do32 = do_ref[_blk(b), :]
            do2 = _stack_heads(do32.astype(BF16))
            delta = _head_rowsum(do32 * o_ref[_blk(b), :])
            lse_t = l_ref[_blk(b), :]
            lse2 = jnp.concatenate([jnp.max(jnp.where(m, lse_t, NEG), axis=1, keepdims=True) for m in masks], axis=0)
            wc = jnp.exp(jnp.where(in_cur, _dot_nt(q2, kc) * SCALE, NEG) - lse2)
            wp = jnp.exp(jnp.where(in_prev, _dot_nt(q2, kp) * SCALE, NEG) - lse2)
            dzc = (wc * (_dot_nt(do2, vc) - delta) * SCALE).astype(BF16)
            dzp = (wp * (_dot_nt(do2, vp) - delta) * SCALE).astype(BF16)
            dq_ref[_blk(b), :] = _unstack_heads(_dot(dzc, kc) + _dot(dzp, kp))
            dk_ref[_blk(b), :] += _dot_tn(dzc, q2)
            dk_ref[_blk(prev), :] += _dot_tn(dzp, q2)
            dv_ref[_blk(b), :] += _dot_tn(wc.astype(BF16), do2)
            dv_ref[_blk(prev), :] += _dot_tn(wp.astype(BF16), do2)
            return 0

        lax.fori_loop(0, nb, block, 0)

    grads = pl.pallas_call(
        body, name=f"dil_bwd_{d}", grid=(pairs, d),
        in_specs=in_specs, out_specs=[out_spec] * 3,
        out_shape=[jax.ShapeDtypeStruct((n, d * D_DIL), F32)] * 3,
        compiler_params=_params(48),
    )(view, view, view, _class_view(out, d), _class_view(lse, d), _class_view(dout, d))
    return [_natural_view(g, d) for g in grads]


def _dil_finish(parts, cos, sin):
    s = parts[0][0].shape[0]
    spec = pl.BlockSpec((TM, D_DIL), lambda i: (i, 0))
    tab = pl.BlockSpec((TM, LANES), lambda i: (i, 0))

    def body(*refs):
        ins, c_ref, s_ref, outs = refs[:9], refs[9], refs[10], refs[11:]
        for t in range(3):
            tot = ins[t][...] + ins[3 + t][...] + ins[6 + t][...]
            for c in range(D_DIL // LANES):
                piece = tot[:, c * LANES:(c + 1) * LANES]
                if t < 2:
                    piece = _rotate(piece, c_ref[...], -s_ref[...])
                outs[t][:, c * LANES:(c + 1) * LANES] = piece.astype(BF16)

    return pl.pallas_call(
        body, name="dil_finish", grid=(s // TM,),
        in_specs=[spec] * 9 + [tab, tab], out_specs=[spec] * 3,
        out_shape=[jax.ShapeDtypeStruct((s, D_DIL), BF16)] * 3,
        compiler_params=_params(48),
    )(*parts[0], *parts[1], *parts[2], cos, sin)


def _place():
    x, y, c = lax.axis_index("x"), lax.axis_index("y"), lax.axis_index("c")
    return x, y, c, 2 * x + y


def _chip(k, c):
    return (k >> 1, k & 1, c)


def _half(ref, h):
    n = ref.shape[0] // 2
    return ref.at[pl.ds(h * n, n)]


def _all_gather(shards):
    na = len(shards)
    any_spec = pl.BlockSpec(memory_space=pl.ANY)

    def body(*refs):
        ins, outs = refs[:na], refs[na:2 * na]
        send_sem, recv_sem, local_sem = refs[2 * na:]
        x, y, c, k = _place()
        sibling = (x, y, 1 - c)
        started = []
        for a in range(na):
            cp = pltpu.make_async_copy(ins[a], outs[a].at[k], local_sem.at[a])
            cp.start()
            started.append(cp)

        def copy(a, slot, src, dst, to):
            return pltpu.make_async_remote_copy(src_ref=src, dst_ref=dst, send_sem=send_sem.at[a * 6 + slot],
                                                recv_sem=recv_sem.at[a * 6 + slot], device_id=to, device_id_type=MESH)

        sends = []
        for a in range(na):
            for j in range(1, N_CHIP):
                cp = copy(a, j - 1, _half(ins[a], c), _half(outs[a].at[k], c), _chip(k ^ j, c))
                cp.start()
                sends.append(cp)
        for j in range(1, N_CHIP):
            for a in range(na):
                landed = _half(outs[a].at[k ^ j], c)
                copy(a, j - 1, landed, landed, sibling).wait_recv()
                cp = copy(a, 2 + j, landed, landed, sibling)
                cp.start()
                sends.append(cp)
        for j in range(1, N_CHIP):
            for a in range(na):
                passed = _half(outs[a].at[k ^ j], 1 - c)
                copy(a, 2 + j, passed, passed, sibling).wait_recv()
        for cp in sends:
            cp.wait_send()
        for cp in started:
            cp.wait()

    return pl.pallas_call(
        body, name="weights_all_gather",
        in_specs=[any_spec] * na, out_specs=[any_spec] * na,
        out_shape=[jax.ShapeDtypeStruct((N_CHIP,) + a.shape, a.dtype) for a in shards],
        scratch_shapes=[pltpu.SemaphoreType.DMA((6 * na,)), pltpu.SemaphoreType.DMA((6 * na,)),
                        pltpu.SemaphoreType.DMA((na,))],
    )(*shards)


def _reduce_scatter(g, core, name):
    n, r, c = g.shape
    hr = r // 2
    once = pl.Buffered(1)
    in_specs = [pl.BlockSpec((n, hr, c), lambda i, core_ref: (0, core_ref[0], 0), pipeline_mode=once),
                pl.BlockSpec((n, hr, c), lambda i, core_ref: (0, 1 - core_ref[0], 0), pipeline_mode=once)]

    def body(core_ref, mine_ref, other_ref, out_ref, from_core, sums, sums_bf, from_chips, done, from_core2, send_sem, recv_sem):
        x, y, cc, k = _place()
        sibling = (x, y, 1 - cc)

        def copy(slot, src, dst, to):
            return pltpu.make_async_remote_copy(src_ref=src, dst_ref=dst, send_sem=send_sem.at[slot],
                                                recv_sem=recv_sem.at[slot], device_id=to, device_id_type=MESH)

        first = copy(0, other_ref, from_core, sibling)
        first.start()
        first.wait()
        total = mine_ref[...] + from_core[...]
        sums[...] = total
        sums_bf[...] = total.astype(BF16)
        sends = [copy(j, sums_bf.at[k ^ j], from_chips.at[j - 1], _chip(k ^ j, cc)) for j in range(1, N_CHIP)]
        for cp in sends:
            cp.start()
        for cp in sends:
            cp.wait()
        red = sums[k]
        for j in range(1, N_CHIP):
            red = red + from_chips[j - 1].astype(F32)
        done[...] = red
        last = copy(N_CHIP, done, from_core2, sibling)
        last.start()
        last.wait()
        row0 = pl.multiple_of(cc * hr, 8)
        row1 = pl.multiple_of((1 - cc) * hr, 8)
        out_ref[pl.ds(row0, hr), :] = red
        out_ref[pl.ds(row1, hr), :] = from_core2[...]

    grid_spec = pltpu.PrefetchScalarGridSpec(
        num_scalar_prefetch=1, grid=(1,), in_specs=in_specs,
        out_specs=pl.BlockSpec((r, c), lambda i, core_ref: (0, 0)),
        scratch_shapes=[pltpu.VMEM((n, hr, c), F32), pltpu.VMEM((n, hr, c), F32), pltpu.VMEM((n, hr, c), BF16),
                        pltpu.VMEM((N_CHIP - 1, hr, c), BF16), pltpu.VMEM((hr, c), F32), pltpu.VMEM((hr, c), F32),
                        pltpu.SemaphoreType.DMA((N_CHIP + 1,)), pltpu.SemaphoreType.DMA((N_CHIP + 1,))])
    return pl.pallas_call(
        body, name=name, grid_spec=grid_spec, out_shape=jax.ShapeDtypeStruct((r, c), F32),
        compiler_params=_params(56),
    )(core, g, g)


def _elementwise(fn, name, ins, n_out, rows):
    total, cols = ins[0].shape
    spec = pl.BlockSpec((rows, cols), lambda i: (i, 0))

    def body(*refs):
        res = fn(*[r[...] for r in refs[:len(ins)]])
        for o, v in zip(refs[len(ins):], res):
            o[...] = v

    return pl.pallas_call(
        body, name=name, grid=(total // rows,),
        in_specs=[spec] * len(ins), out_specs=[spec] * n_out,
        out_shape=[jax.ShapeDtypeStruct((total, cols), F32)] * n_out,
        compiler_params=_params(48),
    )(*ins)


def _adamw(w, g, m, v):
    m = ADAM_B1 * m + (1.0 - ADAM_B1) * g
    v = ADAM_B2 * v + (1.0 - ADAM_B2) * (g * g)
    m_hat = m / (1.0 - ADAM_B1 ** ADAM_STEP)
    v_hat = v / (1.0 - ADAM_B2 ** ADAM_STEP)
    delta = -ADAM_LR * (m_hat / (jnp.sqrt(v_hat) + ADAM_EPS) + ADAM_WD * w)
    return delta, m, v


def _reduce_and_update(grads, weights, moms, vels):
    core = lax.axis_index("c").astype(jnp.int32).reshape(1)
    full = [_reduce_scatter(g, core, f"grads_reduce_scatter_{a}") for a, g in enumerate(grads)]
    out = []
    for a, (g, w, m, v) in enumerate(zip(full, weights, moms, vels)):
        rows = g.shape[0] // 2
        out.append((g,) + tuple(_elementwise(lambda gg, ww, mm, vv: _adamw(ww, gg, mm, vv), f"adamw_{a}", [g, w, m, v], 3, rows)))
    return out


def _reduce_vectors(part, w, m, v):
    n_dev = 8

    def body(p_ref, w_ref, m_ref, v_ref, g_ref, d_ref, nm_ref, nv_ref, buf, send_sem, recv_sem):
        x, y, c, _ = _place()
        me = 4 * x + 2 * y + c
        buf[me] = p_ref[...]
        sends = []
        for off in range(1, n_dev):
            peer = me ^ off
            cp = pltpu.make_async_remote_copy(src_ref=p_ref, dst_ref=buf.at[me], send_sem=send_sem.at[off - 1],
                                              recv_sem=recv_sem.at[off - 1], device_id=(peer >> 2, (peer >> 1) & 1, peer & 1),
                                              device_id_type=MESH)
            cp.start()
            sends.append(cp)
        for off in range(1, n_dev):
            peer = me ^ off
            pltpu.make_async_remote_copy(src_ref=p_ref, dst_ref=buf.at[peer], send_sem=send_sem.at[off - 1],
                                         recv_sem=recv_sem.at[off - 1], device_id=(peer >> 2, (peer >> 1) & 1, peer & 1),
                                         device_id_type=MESH).wait_recv()
        for cp in sends:
            cp.wait_send()
        g = buf[0]
        for d in range(1, n_dev):
            g = g + buf[d]
        g_ref[...] = g
        delta, nm, nv = _adamw(w_ref[...], g, m_ref[...], v_ref[...])
        d_ref[...] = delta
        nm_ref[...] = nm
        nv_ref[...] = nv

    vm = pl.BlockSpec(memory_space=pltpu.VMEM)
    return pl.pallas_call(
        body, name="gains_all_reduce",
        in_specs=[vm] * 4, out_specs=[vm] * 4,
        out_shape=[jax.ShapeDtypeStruct(part.shape, F32)] * 4,
        scratch_shapes=[pltpu.VMEM((n_dev,) + part.shape, F32), pltpu.SemaphoreType.DMA((n_dev - 1,)),
                        pltpu.SemaphoreType.DMA((n_dev - 1,))],
    )(part, w, m, v)


def _pad_row(a):
    a = a.reshape(1, -1)
    return jnp.pad(a, ((0, 0), (0, D_MODEL - a.shape[1])))


def kernel(x, ffn1_norm, ffn1_w_gate, ffn1_w_up, ffn1_w_down, mix_norm, w_in, sb_out_norm, dil_out_norm, w_out, ffn2_norm, ffn2_w_gate, ffn2_w_up, ffn2_w_down, final_norm, loss_target, m_ffn1_norm, m_ffn1_w_gate, m_ffn1_w_up, m_ffn1_w_down, m_mix_norm, m_w_in, m_sb_out_norm, m_dil_out_norm, m_w_out, m_ffn2_norm, m_ffn2_w_gate, m_ffn2_w_up, m_ffn2_w_down, m_final_norm, v_ffn1_norm, v_ffn1_w_gate, v_ffn1_w_up, v_ffn1_w_down, v_mix_norm, v_w_in, v_sb_out_norm, v_dil_out_norm, v_w_out, v_ffn2_norm, v_ffn2_w_gate, v_ffn2_w_up, v_ffn2_w_down, v_final_norm):
    x = x[0]
    target = loss_target[0]
    s = x.shape[0]
    gf = final_norm.reshape(1, D_MODEL)
    cos, sin = _rope_tables(s)

    gu_shard = jnp.stack([ffn1_w_gate[0], ffn1_w_up[0], ffn2_w_gate[0], ffn2_w_up[0]]).astype(BF16)
    wd_shard = jnp.stack([ffn1_w_down[0], ffn2_w_down[0]]).astype(BF16)
    gu, wd, win, wout = _all_gather([gu_shard, wd_shard, w_in[0].astype(BF16), w_out[0].astype(BF16)])
    wout = wout.reshape(D_MODEL, D_MODEL)

    x1, hm = _ffn1_fwd(x, ffn1_norm, mix_norm, gu, wd)
    qkv = _proj_fwd(hm, win, cos, sin)
    o_sb = _sb_fwd(qkv)
    pats = [_dil_fwd(qkv, d) for d in DILATIONS]
    o_dl, lse = _dil_combine([p[0] for p in pats], [p[1] for p in pats])
    x2 = _outproj_fwd(o_sb, o_dl, sb_out_norm, dil_out_norm, x1, wout)
    dx3, st_final = _ffn2_fwd_loss(x2, ffn2_norm, gf, target, gu, wd)

    dx2, dwg2, dwu2, dwd2, st_ffn2 = _ffn_bwd(x2, ffn2_norm, dx3, gu, wd, 1)
    do_sb, do_dl, dwout, st_out = _outproj_bwd(dx2, o_sb, o_dl, sb_out_norm, dil_out_norm, wout)
    dq_sb, dk_sb, dv_sb = _sb_bwd(qkv, o_sb, do_sb)
    dq_dl, dk_dl, dv_dl = _dil_finish([_dil_bwd(qkv, o_dl, lse, do_dl, d) for d in DILATIONS], cos, sin)
    dqkv = jnp.concatenate([dq_sb, dk_sb, dv_sb, dq_dl, dk_dl, dv_dl], axis=1)
    dx1, dwin, st_mix = _proj_bwd(x1, mix_norm, dqkv, win, dx2)
    grad_x, dwg1, dwu1, dwd1, st_ffn1 = _ffn_bwd(x, ffn1_norm, dx1, gu, wd, 0)

    names = ["ffn1_w_gate", "ffn1_w_up", "ffn1_w_down", "w_in", "w_out", "ffn2_w_gate", "ffn2_w_up", "ffn2_w_down"]
    grads = [dwg1, dwu1, dwd1, dwin, dwout.reshape(N_CHIP, OUTB, D_MODEL), dwg2, dwu2, dwd2]
    weights = [ffn1_w_gate[0], ffn1_w_up[0], ffn1_w_down[0], w_in[0], w_out[0], ffn2_w_gate[0], ffn2_w_up[0], ffn2_w_down[0]]
    moms = [m_ffn1_w_gate[0], m_ffn1_w_up[0], m_ffn1_w_down[0], m_w_in[0], m_w_out[0], m_ffn2_w_gate[0], m_ffn2_w_up[0], m_ffn2_w_down[0]]
    vels = [v_ffn1_w_gate[0], v_ffn1_w_up[0], v_ffn1_w_down[0], v_w_in[0], v_w_out[0], v_ffn2_w_gate[0], v_ffn2_w_up[0], v_ffn2_w_down[0]]
    mats = {n: tuple(t[None] for t in r) for n, r in zip(names, _reduce_and_update(grads, weights, moms, vels))}

    vec_names = ["ffn1_norm", "mix_norm", "sb_out_norm", "dil_out_norm", "ffn2_norm", "final_norm"]
    part = jnp.concatenate([st_ffn1[0:1], st_mix[0:1], _pad_row(st_out[0]), _pad_row(st_out[1]), st_ffn2[0:1],
                            st_final[0:1], st_final[1:2], jnp.zeros((1, D_MODEL), F32)], axis=0)
    pack = lambda arrs: jnp.concatenate([_pad_row(a) for a in arrs] + [jnp.zeros((2, D_MODEL), F32)], axis=0)
    g_vec, d_vec, m_vec, v_vec = _reduce_vectors(
        part,
        pack([ffn1_norm, mix_norm, sb_out_norm, dil_out_norm, ffn2_norm, final_norm]),
        pack([m_ffn1_norm, m_mix_norm, m_sb_out_norm, m_dil_out_norm, m_ffn2_norm, m_final_norm]),
        pack([v_ffn1_norm, v_mix_norm, v_sb_out_norm, v_dil_out_norm, v_ffn2_norm, v_final_norm]))
    like = {"ffn1_norm": ffn1_norm, "mix_norm": mix_norm, "sb_out_norm": sb_out_norm, "dil_out_norm": dil_out_norm,
            "ffn2_norm": ffn2_norm, "final_norm": final_norm}
    vecs = {n: tuple(t[i, :like[n].size].reshape(like[n].shape) for t in (g_vec, d_vec, m_vec, v_vec))
            for i, n in enumerate(vec_names)}
    loss = 0.5 * jnp.sum(g_vec[6]) / D_MODEL

    order = ["ffn1_norm", "ffn1_w_gate", "ffn1_w_up", "ffn1_w_down", "mix_norm", "w_in", "sb_out_norm", "dil_out_norm",
             "w_out", "ffn2_norm", "ffn2_w_gate", "ffn2_w_up", "ffn2_w_down", "final_norm"]
    both = {**mats, **vecs}
    return (loss, grad_x[None], *[both[n][0] for n in order], *[both[n][1] for n in order],
            *[both[n][2] for n in order], *[both[n][3] for n in order])
```

```python
import functools

import jax
import jax.numpy as jnp
from jax import lax
from jax.experimental import pallas as pl
from jax.experimental.pallas import tpu as pltpu

D_MODEL = 1024
D_FF = 2816
HEAD_DIM = 64
D_SB = 512
D_DIL = 512
D_IN = 3072
N_CHIP = 4
FFB = D_FF // N_CHIP
INB = D_IN // N_CHIP
OUTB = D_MODEL // N_CHIP
BLK = 128
LANES = 128
DILATIONS = (1, 4, 16)
ROPE_THETA = 10000.0
RMS_EPS = 1e-6
SCALE = HEAD_DIM ** -0.5
NEG = -1e30
DEAD = -104.0
ADAM_LR = 0.001
ADAM_B1 = 0.9
ADAM_B2 = 0.999
ADAM_EPS = 1e-08
ADAM_WD = 0.01
ADAM_STEP = 10
MESH = pl.DeviceIdType.MESH
F32 = jnp.float32
BF16 = jnp.bfloat16
TM = 512


def _params(vmem_mb):
    return pltpu.CompilerParams(vmem_limit_bytes=vmem_mb << 20)


def _dot(a, b):
    return jnp.dot(a, b, preferred_element_type=F32)


def _dot_nt(a, b):
    return lax.dot_general(a, b, (((1,), (1,)), ((), ())), preferred_element_type=F32)


def _dot_tn(a, b):
    return lax.dot_general(a, b, (((0,), (0,)), ((), ())), preferred_element_type=F32)


def _rms_fwd(x, g):
    r = lax.rsqrt(jnp.mean(x * x, axis=-1, keepdims=True) + RMS_EPS)
    xh = x * r
    return xh * g, xh, r


def _rms_bwd(dy, xh, r, g):
    dyg = dy * g
    dx = r * (dyg - xh * jnp.mean(dyg * xh, axis=-1, keepdims=True))
    return dx, jnp.sum(dy * xh, axis=0, keepdims=True)


def _split_bf16(a):
    hi = a.astype(BF16)
    return hi, (a - hi.astype(F32)).astype(BF16)


def _dot_split(a, b):
    hi, lo = _split_bf16(a)
    return _dot(hi, b) + _dot(lo, b)


def _ffn_weight_specs(f):
    return [pl.BlockSpec((None, None, D_MODEL, FFB), lambda i, j: (j, 2 * f, 0, 0)),
            pl.BlockSpec((None, None, D_MODEL, FFB), lambda i, j: (j, 2 * f + 1, 0, 0)),
            pl.BlockSpec((None, None, FFB, D_MODEL), lambda i, j: (j, f, 0, 0))]


def _ffn_accumulate(h_scr, acc_scr, wg_ref, wu_ref, wd_ref):
    h = h_scr[...]
    a = _dot(h, wg_ref[...])
    b = _dot(h, wu_ref[...])
    act = (a * jax.nn.sigmoid(a)) * b
    acc_scr[...] += _dot(act.astype(BF16), wd_ref[...])


def _ffn1_fwd(x, g1, gmix, gu, wd):
    s = x.shape[0]
    row = pl.BlockSpec((TM, D_MODEL), lambda i, j: (i, 0))
    vec = pl.BlockSpec((1, D_MODEL), lambda i, j: (0, 0))

    def body(x_ref, g_ref, gm_ref, wg_ref, wu_ref, wd_ref, x1_ref, hm_ref, h_scr, acc_scr):
        j = pl.program_id(1)

        @pl.when(j == 0)
        def _():
            h, _, _ = _rms_fwd(x_ref[...], g_ref[...])
            h_scr[...] = h.astype(BF16)
            acc_scr[...] = jnp.zeros_like(acc_scr)

        _ffn_accumulate(h_scr, acc_scr, wg_ref, wu_ref, wd_ref)

        @pl.when(j == N_CHIP - 1)
        def _():
            x1 = x_ref[...] + 0.5 * acc_scr[...]
            x1_ref[...] = x1
            hm, _, _ = _rms_fwd(x1, gm_ref[...])
            hm_ref[...] = hm.astype(BF16)

    return pl.pallas_call(
        body, name="ffn1_fwd", grid=(s // TM, N_CHIP),
        in_specs=[row, vec, vec] + _ffn_weight_specs(0),
        out_specs=[row, row],
        out_shape=[jax.ShapeDtypeStruct((s, D_MODEL), F32), jax.ShapeDtypeStruct((s, D_MODEL), BF16)],
        scratch_shapes=[pltpu.VMEM((TM, D_MODEL), BF16), pltpu.VMEM((TM, D_MODEL), F32)],
        compiler_params=_params(48),
    )(x, g1, gmix, gu, gu, wd)


def _ffn2_fwd_loss(x2, g2, gf, target, gu, wd):
    s = x2.shape[0]
    row = pl.BlockSpec((TM, D_MODEL), lambda i, j: (i, 0))
    vec = pl.BlockSpec((1, D_MODEL), lambda i, j: (0, 0))
    stat = pl.BlockSpec((8, D_MODEL), lambda i, j: (0, 0))

    def body(x_ref, g_ref, gf_ref, t_ref, wg_ref, wu_ref, wd_ref, dx_ref, st_ref, h_scr, acc_scr):
        i, j = pl.program_id(0), pl.program_id(1)

        @pl.when((i == 0) & (j == 0))
        def _():
            st_ref[...] = jnp.zeros_like(st_ref)

        @pl.when(j == 0)
        def _():
            h, _, _ = _rms_fwd(x_ref[...], g_ref[...])
            h_scr[...] = h.astype(BF16)
            acc_scr[...] = jnp.zeros_like(acc_scr)

        _ffn_accumulate(h_scr, acc_scr, wg_ref, wu_ref, wd_ref)

        @pl.when(j == N_CHIP - 1)
        def _():
            x3 = x_ref[...] + 0.5 * acc_scr[...]
            y, xh, r = _rms_fwd(x3, gf_ref[...])
            err = y - t_ref[...]
            dx, dg = _rms_bwd(err * (1.0 / D_MODEL), xh, r, gf_ref[...])
            dx_ref[...] = dx
            st_ref[0:1, :] += dg
            st_ref[1:2, :] += jnp.sum(err * err, axis=0, keepdims=True)

    return pl.pallas_call(
        body, name="ffn2_fwd_loss", grid=(s // TM, N_CHIP),
        in_specs=[row, vec, vec, row] + _ffn_weight_specs(1),
        out_specs=[row, stat],
        out_shape=[jax.ShapeDtypeStruct((s, D_MODEL), F32), jax.ShapeDtypeStruct((8, D_MODEL), F32)],
        scratch_shapes=[pltpu.VMEM((TM, D_MODEL), BF16), pltpu.VMEM((TM, D_MODEL), F32)],
        compiler_params=_params(48),
    )(x2, g2, gf, target, gu, gu, wd)


def _ffn_bwd(xin, g, dy, gu, wd, f):
    s = xin.shape[0]
    row = pl.BlockSpec((TM, D_MODEL), lambda i, j: (i, 0))
    vec = pl.BlockSpec((1, D_MODEL), lambda i, j: (0, 0))
    stat = pl.BlockSpec((8, D_MODEL), lambda i, j: (0, 0))
    hid = pl.BlockSpec((None, TM, FFB), lambda i, j: (j, i, 0))

    def body(x_ref, g_ref, dy_ref, wg_ref, wu_ref, wd_ref, out_ref, h_ref, dyh_ref, da_ref, db_ref, act_ref, st_ref, dh_scr):
        i, j = pl.program_id(0), pl.program_id(1)

        @pl.when((i == 0) & (j == 0))
        def _():
            st_ref[...] = jnp.zeros_like(st_ref)

        @pl.when(j == 0)
        def _():
            h, _, _ = _rms_fwd(x_ref[...], g_ref[...])
            h_ref[...] = h.astype(BF16)
            dyh_ref[...] = (0.5 * dy_ref[...]).astype(BF16)
            dh_scr[...] = jnp.zeros_like(dh_scr)

        hb = h_ref[...]
        a = _dot(hb, wg_ref[...])
        b = _dot(hb, wu_ref[...])
        sg = jax.nn.sigmoid(a)
        sa = a * sg
        dact = _dot_nt(dyh_ref[...], wd_ref[...])
        dab = (dact * b * (sg * (1.0 + a * (1.0 - sg)))).astype(BF16)
        dbb = (dact * sa).astype(BF16)
        da_ref[...] = dab
        db_ref[...] = dbb
        act_ref[...] = (sa * b).astype(BF16)
        dh_scr[...] += _dot_nt(dab, wg_ref[...]) + _dot_nt(dbb, wu_ref[...])

        @pl.when(j == N_CHIP - 1)
        def _():
            _, xh, r = _rms_fwd(x_ref[...], g_ref[...])
            dx, dg = _rms_bwd(dh_scr[...], xh, r, g_ref[...])
            out_ref[...] = dy_ref[...] + dx
            st_ref[0:1, :] += dg

    hidden = jax.ShapeDtypeStruct((N_CHIP, s, FFB), BF16)
    dx, hb, dyh, da, db, act, st = pl.pallas_call(
        body, name=f"ffn{f + 1}_bwd_dx", grid=(s // TM, N_CHIP),
        in_specs=[row, vec, row] + _ffn_weight_specs(f),
        out_specs=[row, row, row, hid, hid, hid, stat],
        out_shape=[jax.ShapeDtypeStruct((s, D_MODEL), F32), jax.ShapeDtypeStruct((s, D_MODEL), BF16),
                   jax.ShapeDtypeStruct((s, D_MODEL), BF16), hidden, hidden, hidden,
                   jax.ShapeDtypeStruct((8, D_MODEL), F32)],
        scratch_shapes=[pltpu.VMEM((TM, D_MODEL), F32)],
        compiler_params=_params(56),
    )(xin, g, dy, gu, gu, wd)

    tok = pl.BlockSpec((TM, D_MODEL), lambda j, i: (i, 0))
    hid2 = pl.BlockSpec((None, TM, FFB), lambda j, i: (j, i, 0))
    gspecs = [pl.BlockSpec((None, D_MODEL, FFB), lambda j, i: (j, 0, 0)),
              pl.BlockSpec((None, D_MODEL, FFB), lambda j, i: (j, 0, 0)),
              pl.BlockSpec((None, FFB, D_MODEL), lambda j, i: (j, 0, 0))]

    def wbody(h_ref, dyh_ref, da_ref, db_ref, act_ref, dwg_ref, dwu_ref, dwd_ref):
        @pl.when(pl.program_id(1) == 0)
        def _():
            dwg_ref[...] = jnp.zeros_like(dwg_ref)
            dwu_ref[...] = jnp.zeros_like(dwu_ref)
            dwd_ref[...] = jnp.zeros_like(dwd_ref)

        hb = h_ref[...]
        dwg_ref[...] += _dot_tn(hb, da_ref[...])
        dwu_ref[...] += _dot_tn(hb, db_ref[...])
        dwd_ref[...] += _dot_tn(act_ref[...], dyh_ref[...])

    dwg, dwu, dwd = pl.pallas_call(
        wbody, name=f"ffn{f + 1}_bwd_dw", grid=(N_CHIP, s // TM),
        in_specs=[tok, tok, hid2, hid2, hid2], out_specs=gspecs,
        out_shape=[jax.ShapeDtypeStruct((N_CHIP, D_MODEL, FFB), F32),
                   jax.ShapeDtypeStruct((N_CHIP, D_MODEL, FFB), F32),
                   jax.ShapeDtypeStruct((N_CHIP, FFB, D_MODEL), F32)],
        compiler_params=_params(48),
    )(hb, dyh, da, db, act)
    return dx, dwg, dwu, dwd, st


def _rope_tables(s):
    half = HEAD_DIM // 2
    inv_freq = ROPE_THETA ** (-jnp.arange(half, dtype=F32) / half)
    ang = jnp.arange(s).astype(F32)[:, None] * inv_freq[None, :]
    cos, sin = jnp.cos(ang), jnp.sin(ang)
    cos2 = jnp.concatenate([cos, cos], axis=-1)
    sin2 = jnp.concatenate([-sin, sin], axis=-1)
    return jnp.tile(cos2, (1, LANES // HEAD_DIM)), jnp.tile(sin2, (1, LANES // HEAD_DIM))


def _rotate(t, cos, sin_signed):
    lane = lax.broadcasted_iota(jnp.int32, t.shape, 1)
    first = (lane % HEAD_DIM) < (HEAD_DIM // 2)
    partner = jnp.where(first, pltpu.roll(t, LANES - HEAD_DIM // 2, 1), pltpu.roll(t, HEAD_DIM // 2, 1))
    return t * cos + partner * sin_signed


def _proj_fwd(hm, win, cos, sin):
    s = hm.shape[0]
    n_sub = INB // LANES
    first_rot, last_rot = (3 * D_SB) // LANES, (3 * D_SB + 2 * D_DIL) // LANES

    def body(h_ref, w_ref, c_ref, s_ref, o_ref):
        j = pl.program_id(1)
        r = _dot(h_ref[...], w_ref[...])
        for c in range(n_sub):
            t = r[:, c * LANES:(c + 1) * LANES]
            col = j * n_sub + c
            rot = (col >= first_rot) & (col < last_rot)
            lanes = slice(c * LANES, (c + 1) * LANES)

            @pl.when(rot)
            def _():
                o_ref[:, lanes] = _rotate(t, c_ref[...], s_ref[...]).astype(BF16)

            @pl.when(jnp.logical_not(rot))
            def _():
                o_ref[:, lanes] = t.astype(BF16)

    return pl.pallas_call(
        body, name="proj_fwd", grid=(s // TM, N_CHIP),
        in_specs=[pl.BlockSpec((TM, D_MODEL), lambda i, j: (i, 0)),
                  pl.BlockSpec((None, D_MODEL, INB), lambda i, j: (j, 0, 0)),
                  pl.BlockSpec((TM, LANES), lambda i, j: (i, 0)),
                  pl.BlockSpec((TM, LANES), lambda i, j: (i, 0))],
        out_specs=pl.BlockSpec((TM, INB), lambda i, j: (i, j)),
        out_shape=jax.ShapeDtypeStruct((s, D_IN), BF16),
        compiler_params=_params(32),
    )(hm, win, cos, sin)


def _proj_bwd(x1, gmix, dqkv, win, dx2):
    s = x1.shape[0]
    row = pl.BlockSpec((TM, D_MODEL), lambda i, j: (i, 0))
    vec = pl.BlockSpec((1, D_MODEL), lambda i, j: (0, 0))

    def body(x_ref, g_ref, dq_ref, w_ref, dx2_ref, out_ref, dw_ref, st_ref, h_scr, dh_scr):
        i, j = pl.program_id(0), pl.program_id(1)

        @pl.when((i == 0) & (j == 0))
        def _():
            st_ref[...] = jnp.zeros_like(st_ref)
            dw_ref[...] = jnp.zeros_like(dw_ref)

        @pl.when(j == 0)
        def _():
            h, _, _ = _rms_fwd(x_ref[...], g_ref[...])
            h_scr[...] = h.astype(BF16)
            dh_scr[...] = jnp.zeros_like(dh_scr)

        dq = dq_ref[...]
        dw_ref[j] += _dot_tn(h_scr[...], dq)
        dh_scr[...] += _dot_nt(dq, w_ref[...])

        @pl.when(j == N_CHIP - 1)
        def _():
            _, xh, r = _rms_fwd(x_ref[...], g_ref[...])
            dx, dg = _rms_bwd(dh_scr[...], xh, r, g_ref[...])
            out_ref[...] = dx2_ref[...] + dx
            st_ref[0:1, :] += dg

    return pl.pallas_call(
        body, name="proj_bwd", grid=(s // TM, N_CHIP),
        in_specs=[row, vec, pl.BlockSpec((TM, INB), lambda i, j: (i, j)),
                  pl.BlockSpec((None, D_MODEL, INB), lambda i, j: (j, 0, 0)), row],
        out_specs=[row, pl.BlockSpec((N_CHIP, D_MODEL, INB), lambda i, j: (0, 0, 0)),
                   pl.BlockSpec((8, D_MODEL), lambda i, j: (0, 0))],
        out_shape=[jax.ShapeDtypeStruct((s, D_MODEL), F32),
                   jax.ShapeDtypeStruct((N_CHIP, D_MODEL, INB), F32),
                   jax.ShapeDtypeStruct((8, D_MODEL), F32)],
        scratch_shapes=[pltpu.VMEM((TM, D_MODEL), BF16), pltpu.VMEM((TM, D_MODEL), F32)],
        compiler_params=_params(56),
    )(x1, gmix, dqkv, win, dx2)


def _outproj_fwd(o_sb, o_dl, g_sb, g_dl, x1, wout):
    s = x1.shape[0]
    half = pl.BlockSpec((TM, D_SB), lambda i: (i, 0))
    row = pl.BlockSpec((TM, D_MODEL), lambda i: (i, 0))
    vec = pl.BlockSpec((1, D_SB), lambda i: (0, 0))

    def body(a_ref, b_ref, ga_ref, gb_ref, x_ref, w_ref, o_ref):
        ma, _, _ = _rms_fwd(a_ref[...], ga_ref[...])
        mb, _, _ = _rms_fwd(b_ref[...], gb_ref[...])
        o_ref[...] = (x_ref[...] + _dot(ma.astype(BF16), w_ref[0:D_SB, :])
                      + _dot(mb.astype(BF16), w_ref[D_SB:D_MODEL, :]))

    return pl.pallas_call(
        body, name="outproj_fwd", grid=(s // TM,),
        in_specs=[half, half, vec, vec, row, pl.BlockSpec((D_MODEL, D_MODEL), lambda i: (0, 0))],
        out_specs=row, out_shape=jax.ShapeDtypeStruct((s, D_MODEL), F32),
        compiler_params=_params(32),
    )(o_sb, o_dl, g_sb, g_dl, x1, wout)


def _outproj_bwd(dx2, o_sb, o_dl, g_sb, g_dl, wout):
    s = dx2.shape[0]
    half = pl.BlockSpec((TM, D_SB), lambda i: (i, 0))
    row = pl.BlockSpec((TM, D_MODEL), lambda i: (i, 0))
    vec = pl.BlockSpec((1, D_SB), lambda i: (0, 0))
    full = pl.BlockSpec((D_MODEL, D_MODEL), lambda i: (0, 0))

    def body(dy_ref, a_ref, b_ref, ga_ref, gb_ref, w_ref, da_ref, db_ref, dw_ref, st_ref):
        @pl.when(pl.program_id(0) == 0)
        def _():
            dw_ref[...] = jnp.zeros_like(dw_ref)
            st_ref[...] = jnp.zeros_like(st_ref)

        dy = dy_ref[...].astype(BF16)
        dm = _dot_nt(dy, w_ref[...])
        ma, xa, ra = _rms_fwd(a_ref[...], ga_ref[...])
        mb, xb, rb = _rms_fwd(b_ref[...], gb_ref[...])
        dw_ref[0:D_SB, :] += _dot_tn(ma.astype(BF16), dy)
        dw_ref[D_SB:D_MODEL, :] += _dot_tn(mb.astype(BF16), dy)
        da, dga = _rms_bwd(dm[:, 0:D_SB], xa, ra, ga_ref[...])
        db, dgb = _rms_bwd(dm[:, D_SB:D_MODEL], xb, rb, gb_ref[...])
        da_ref[...] = da
        db_ref[...] = db
        st_ref[0:1, :] += dga
        st_ref[1:2, :] += dgb

    return pl.pallas_call(
        body, name="outproj_bwd", grid=(s // TM,),
        in_specs=[row, half, half, vec, vec, full],
        out_specs=[half, half, full, pl.BlockSpec((8, D_SB), lambda i: (0, 0))],
        out_shape=[jax.ShapeDtypeStruct((s, D_SB), F32), jax.ShapeDtypeStruct((s, D_SB), F32),
                   jax.ShapeDtypeStruct((D_MODEL, D_MODEL), F32), jax.ShapeDtypeStruct((8, D_SB), F32)],
        compiler_params=_params(48),
    )(dx2, o_sb, o_dl, g_sb, g_dl, wout)


def _head_masks():
    lane = lax.broadcasted_iota(jnp.int32, (BLK, LANES), 1)
    return [lane < HEAD_DIM, lane >= HEAD_DIM]


def _keep(mask, a):
    return a * jnp.where(mask, 1.0, 0.0).astype(a.dtype)


def _suffix_matrices():
    r = lax.broadcasted_iota(jnp.int32, (BLK, BLK), 0)
    c = lax.broadcasted_iota(jnp.int32, (BLK, BLK), 1)
    ones = jnp.ones((BLK, BLK), BF16)
    excl = jnp.concatenate([(r > c).astype(BF16), ones], axis=1)
    incl = jnp.concatenate([(r >= c).astype(BF16), ones], axis=1)
    return excl, incl


def _blk(i):
    return pl.ds(pl.multiple_of(i * BLK, BLK), BLK)


def _alive(carry_m):
    return (jnp.max(carry_m) > DEAD).astype(jnp.int32)


def _more_keys(i, carry):
    return (carry[0] <= i) & (carry[1] > 0)


def _stack_heads(a):
    masks = _head_masks()
    return jnp.concatenate([_keep(masks[0], a), _keep(masks[1], a)], axis=0)


def _unstack_heads(a2):
    return jnp.where(_head_masks()[0], a2[:BLK], a2[BLK:])


def _head_rowsum(a):
    masks = _head_masks()
    return jnp.concatenate([jnp.sum(jnp.where(m, a, 0.0), axis=1, keepdims=True) for m in masks], axis=0)


def _sb_scores(q2, k, i, j, carry_m, u_excl):
    row = lax.broadcasted_iota(jnp.int32, (2 * BLK, BLK), 0) & (BLK - 1)
    col = lax.broadcasted_iota(jnp.int32, (2 * BLK, BLK), 1)
    valid = (j * BLK + col) < (i * BLK + row)
    z = _dot_nt(q2, k) * SCALE
    sp = jnp.maximum(z, 0.0) + jnp.log(1.0 + jnp.exp(-jnp.abs(z)))
    log_stay = jnp.where(valid, -sp, 0.0)
    log_beta = z - sp
    sums = _dot_split(log_stay, u_excl)
    later = carry_m + sums[:, :BLK]
    w = jnp.where(valid, jnp.exp(log_beta + later), 0.0)
    return valid, log_beta, w, carry_m + sums[:, BLK:]


def _sb_fwd(qkv):
    s = qkv.shape[0]
    nq = s // BLK
    pairs = D_SB // LANES
    col = lambda off: pl.BlockSpec((s, LANES), lambda p: (0, off + p))

    def body(q_ref, k_ref, v_ref, o_ref):
        u_excl, _ = _suffix_matrices()
        zero = jnp.zeros((2 * BLK, LANES), F32)

        def q_block(i, _):
            q2 = _stack_heads(q_ref[_blk(i), :])

            def k_block(carry):
                jj, _, carry_m, acc = carry
                j = i - jj
                _, _, w, carry_m = _sb_scores(q2, k_ref[_blk(j), :], i, j, carry_m, u_excl)
                return jj + 1, _alive(carry_m), carry_m, acc + _dot(w.astype(BF16), v_ref[_blk(j), :])

            _, _, _, acc = lax.while_loop(functools.partial(_more_keys, i), k_block,
                                          (jnp.int32(0), jnp.int32(1), zero, zero))
            o_ref[_blk(i), :] = _unstack_heads(acc)
            return 0

        lax.fori_loop(0, nq, q_block, 0)

    return pl.pallas_call(
        body, name="sb_fwd", grid=(pairs,),
        in_specs=[col(0), col(pairs), col(2 * pairs)],
        out_specs=pl.BlockSpec((s, LANES), lambda p: (0, p)),
        out_shape=jax.ShapeDtypeStruct((s, D_SB), F32),
        compiler_params=_params(48),
    )(qkv, qkv, qkv)


def _sb_bwd(qkv, o_sb, do_sb):
    s = qkv.shape[0]
    nq = s // BLK
    pairs = D_SB // LANES
    col = lambda off: pl.BlockSpec((s, LANES), lambda p: (0, off + p))
    own = pl.BlockSpec((s, LANES), lambda p: (0, p))

    def body(q_ref, k_ref, v_ref, o_ref, do_ref, dq_ref, dk_ref, dv_ref, dk_acc, dv_acc):
        u_excl, u_incl = _suffix_matrices()
        zero = jnp.zeros((2 * BLK, LANES), F32)
        dk_acc[...] = jnp.zeros_like(dk_acc)
        dv_acc[...] = jnp.zeros_like(dv_acc)

        def q_block(i, _):
            q2 = _stack_heads(q_ref[_blk(i), :])
            do = do_ref[_blk(i), :].astype(BF16)
            do2 = _stack_heads(do)
            total = jnp.broadcast_to(_head_rowsum(do.astype(F32) * o_ref[_blk(i), :]), (2 * BLK, BLK))

            def k_block(carry):
                jj, _, carry_m, carry_g, dq = carry
                j = i - jj
                k = k_ref[_blk(j), :]
                valid, log_beta, w, carry_m = _sb_scores(q2, k, i, j, carry_m, u_excl)
                wb = w.astype(BF16)
                g = wb.astype(F32) * _dot_nt(do2, v_ref[_blk(j), :])
                sums = _dot_split(g, u_incl)
                before = total - (carry_g + sums[:, :BLK])
                dz = jnp.where(valid, g - jnp.exp(log_beta) * (g + before), 0.0)
                dzb = (dz * SCALE).astype(BF16)
                dk_acc[_blk(j), :] += _dot_tn(dzb, q2)
                dv_acc[_blk(j), :] += _dot_tn(wb, do2)
                return jj + 1, _alive(carry_m), carry_m, carry_g + sums[:, BLK:], dq + _dot(dzb, k)

            _, _, _, _, dq = lax.while_loop(functools.partial(_more_keys, i), k_block,
                                            (jnp.int32(0), jnp.int32(1), zero, zero, zero))
            dq_ref[_blk(i), :] = _unstack_heads(dq).astype(BF16)
            return 0

        lax.fori_loop(0, nq, q_block, 0)
        dk_ref[...] = dk_acc[...].astype(BF16)
        dv_ref[...] = dv_acc[...].astype(BF16)

    return pl.pallas_call(
        body, name="sb_bwd", grid=(pairs,),
        in_specs=[col(0), col(pairs), col(2 * pairs), own, own],
        out_specs=[own, own, own],
        out_shape=[jax.ShapeDtypeStruct((s, D_SB), BF16)] * 3,
        scratch_shapes=[pltpu.VMEM((s, LANES), F32), pltpu.VMEM((s, LANES), F32)],
        compiler_params=_params(56),
    )(qkv, qkv, qkv, o_sb, do_sb)


def _band_masks(b):
    row = lax.broadcasted_iota(jnp.int32, (2 * BLK, BLK), 0) & (BLK - 1)
    col = lax.broadcasted_iota(jnp.int32, (2 * BLK, BLK), 1)
    return col <= row, (col - row) >= jnp.where(b > 0, 0, BLK)


def _dil_tiles(qf, kf, vf, d, t, nb):
    c, b = t // nb, t % nb
    start = c + d * BLK * b
    rows = pl.ds(start, BLK, stride=d)
    prev = pl.ds(jnp.where(b > 0, start - d * BLK, start), BLK, stride=d)
    bf = lambda ref, sl: ref[sl, :].astype(BF16)
    return b, rows, prev, _stack_heads(bf(qf, rows)), bf(kf, rows), bf(kf, prev), bf(vf, rows), bf(vf, prev)


def _lanes_of_heads(col2):
    return _unstack_heads(jnp.broadcast_to(col2, (2 * BLK, LANES)))


def _dilated_fwd(qkv):
    s = qkv.shape[0]
    pairs = D_DIL // LANES
    base = (3 * D_SB) // LANES
    col = lambda off: pl.BlockSpec((s, LANES), lambda p: (0, off + p))
    own = pl.BlockSpec((s, LANES), lambda p: (0, p))

    def body(q_ref, k_ref, v_ref, acc_ref, m_ref, qf, kf, vf, l_scr):
        qf[...] = q_ref[...].astype(F32)
        kf[...] = k_ref[...].astype(F32)
        vf[...] = v_ref[...].astype(F32)
        for d in DILATIONS:
            nb = s // (d * BLK)

            def block(t, _):
                b, rows, prev, q2, kc, kp, vc, vp = _dil_tiles(qf, kf, vf, d, t, nb)
                in_cur, in_prev = _band_masks(b)
                zc = jnp.where(in_cur, _dot_nt(q2, kc) * SCALE, NEG)
                zp = jnp.where(in_prev, _dot_nt(q2, kp) * SCALE, NEG)
                m = jnp.maximum(jnp.max(zc, axis=1, keepdims=True), jnp.max(zp, axis=1, keepdims=True))
                pc, pp = jnp.exp(zc - m), jnp.exp(zp - m)
                den = jnp.sum(pc, axis=1, keepdims=True) + jnp.sum(pp, axis=1, keepdims=True)
                acc = _unstack_heads(_dot(pc.astype(BF16), vc) + _dot(pp.astype(BF16), vp))
                m_t, l_t = _lanes_of_heads(m), _lanes_of_heads(den)
                if d == DILATIONS[0]:
                    m_ref[rows, :] = m_t
                    l_scr[rows, :] = l_t
                    acc_ref[rows, :] = acc
                else:
                    m_old = m_ref[rows, :]
                    m_new = jnp.maximum(m_old, m_t)
                    keep, add = jnp.exp(m_old - m_new), jnp.exp(m_t - m_new)
                    m_ref[rows, :] = m_new
                    l_scr[rows, :] = l_scr[rows, :] * keep + l_t * add
                    acc_ref[rows, :] = acc_ref[rows, :] * keep + acc * add
                return 0

            lax.fori_loop(0, s // BLK, block, 0, unroll=2)

        def finish(i, _):
            l = l_scr[_blk(i), :]
            acc_ref[_blk(i), :] = acc_ref[_blk(i), :] / l
            m_ref[_blk(i), :] = m_ref[_blk(i), :] + jnp.log(l)
            return 0

        lax.fori_loop(0, s // BLK, finish, 0)

    return pl.pallas_call(
        body, name="dilated_fwd", grid=(pairs,),
        in_specs=[col(base), col(base + pairs), col(base + 2 * pairs)],
        out_specs=[own, own],
        out_shape=[jax.ShapeDtypeStruct((s, D_DIL), F32)] * 2,
        scratch_shapes=[pltpu.VMEM((s, LANES), F32)] * 4,
        compiler_params=_params(56),
    )(qkv, qkv, qkv)


def _dilated_bwd(qkv, out, lse, dout):
    s = qkv.shape[0]
    pairs = D_DIL // LANES
    base = (3 * D_SB) // LANES
    once = pl.Buffered(1)
    col = lambda off: pl.BlockSpec((s, LANES), lambda p: (0, off + p), pipeline_mode=once)
    own = pl.BlockSpec((s, LANES), lambda p: (0, p), pipeline_mode=once)
    res = pl.BlockSpec((s, LANES), lambda p: (0, p))

    def body(q_ref, k_ref, v_ref, o_ref, l_ref, do_ref, dq_ref, dk_ref, dv_ref, qf, kf, vf):
        masks = _head_masks()
        qf[...] = q_ref[...].astype(F32)
        kf[...] = k_ref[...].astype(F32)
        vf[...] = v_ref[...].astype(F32)
        dq_ref[...] = jnp.zeros_like(dq_ref)
        dk_ref[...] = jnp.zeros_like(dk_ref)
        dv_ref[...] = jnp.zeros_like(dv_ref)
        for d in DILATIONS:
            nb = s // (d * BLK)

            def block(t, _):
                b, rows, prev, q2, kc, kp, vc, vp = _dil_tiles(qf, kf, vf, d, t, nb)
                in_cur, in_prev = _band_masks(b)
                do32 = do_ref[rows, :]
                do2 = _stack_heads(do32.astype(BF16))
                delta = _head_rowsum(do32 * o_ref[rows, :])
                lse_t = l_ref[rows, :]
                lse2 = jnp.concatenate([jnp.max(jnp.where(m, lse_t, NEG), axis=1, keepdims=True) for m in masks], axis=0)
                wc = jnp.exp(jnp.where(in_cur, _dot_nt(q2, kc) * SCALE, NEG) - lse2)
                wp = jnp.exp(jnp.where(in_prev, _dot_nt(q2, kp) * SCALE, NEG) - lse2)
                dzc = (wc * (_dot_nt(do2, vc) - delta) * SCALE).astype(BF16)
                dzp = (wp * (_dot_nt(do2, vp) - delta) * SCALE).astype(BF16)
                dq_ref[rows, :] += _unstack_heads(_dot(dzc, kc) + _dot(dzp, kp))
                dk_ref[rows, :] += _dot_tn(dzc, q2)
                dk_ref[prev, :] += _dot_tn(dzp, q2)
                dv_ref[rows, :] += _dot_tn(wc.astype(BF16), do2)
                dv_ref[prev, :] += _dot_tn(wp.astype(BF16), do2)
                return 0

            lax.fori_loop(0, s // BLK, block, 0)

    return pl.pallas_call(
        body, name="dilated_bwd", grid=(pairs,),
        in_specs=[col(base), col(base + pairs), col(base + 2 * pairs), own, own, own],
        out_specs=[res, res, res],
        out_shape=[jax.ShapeDtypeStruct((s, D_DIL), F32)] * 3,
        scratch_shapes=[pltpu.VMEM((s, LANES), F32)] * 3,
        compiler_params=_params(60),
    )(qkv, qkv, qkv, out, lse, dout)


def _dilated_finish(grads, cos, sin):
    s = grads[0].shape[0]
    spec = pl.BlockSpec((TM, D_DIL), lambda i: (i, 0))
    tab = pl.BlockSpec((TM, LANES), lambda i: (i, 0))

    def body(dq_ref, dk_ref, dv_ref, c_ref, s_ref, oq_ref, ok_ref, ov_ref):
        for src, dst, rotated in ((dq_ref, oq_ref, True), (dk_ref, ok_ref, True), (dv_ref, ov_ref, False)):
            for c in range(D_DIL // LANES):
                lanes = slice(c * LANES, (c + 1) * LANES)
                piece = src[:, lanes]
                dst[:, lanes] = (_rotate(piece, c_ref[...], -s_ref[...]) if rotated else piece).astype(BF16)

    return pl.pallas_call(
        body, name="dilated_finish", grid=(s // TM,),
        in_specs=[spec] * 3 + [tab, tab], out_specs=[spec] * 3,
        out_shape=[jax.ShapeDtypeStruct((s, D_DIL), BF16)] * 3,
        compiler_params=_params(32),
    )(*grads, cos, sin)


def _place():
    x, y, c = lax.axis_index("x"), lax.axis_index("y"), lax.axis_index("c")
    return x, y, c, 2 * x + y


def _chip(k, c):
    return (k >> 1, k & 1, c)


def _half(ref, h):
    n = ref.shape[0] // 2
    return ref.at[pl.ds(h * n, n)]


def _all_gather(shards):
    na = len(shards)
    any_spec = pl.BlockSpec(memory_space=pl.ANY)

    def body(*refs):
        ins, outs = refs[:na], refs[na:2 * na]
        send_sem, recv_sem, local_sem = refs[2 * na:]
        x, y, c, k = _place()
        sibling = (x, y, 1 - c)
        started = []
        for a in range(na):
            cp = pltpu.make_async_copy(ins[a], outs[a].at[k], local_sem.at[a])
            cp.start()
            started.append(cp)

        def copy(a, slot, src, dst, to):
            return pltpu.make_async_remote_copy(src_ref=src, dst_ref=dst, send_sem=send_sem.at[a * 6 + slot],
                                                recv_sem=recv_sem.at[a * 6 + slot], device_id=to, device_id_type=MESH)

        sends = []
        for a in range(na):
            for j in range(1, N_CHIP):
                cp = copy(a, j - 1, _half(ins[a], c), _half(outs[a].at[k], c), _chip(k ^ j, c))
                cp.start()
                sends.append(cp)
        for j in range(1, N_CHIP):
            for a in range(na):
                landed = _half(outs[a].at[k ^ j], c)
                copy(a, j - 1, landed, landed, sibling).wait_recv()
                cp = copy(a, 2 + j, landed, landed, sibling)
                cp.start()
                sends.append(cp)
        for j in range(1, N_CHIP):
            for a in range(na):
                passed = _half(outs[a].at[k ^ j], 1 - c)
                copy(a, 2 + j, passed, passed, sibling).wait_recv()
        for cp in sends:
            cp.wait_send()
        for cp in started:
            cp.wait()

    return pl.pallas_call(
        body, name="weights_all_gather",
        in_specs=[any_spec] * na, out_specs=[any_spec] * na,
        out_shape=[jax.ShapeDtypeStruct((N_CHIP,) + a.shape, a.dtype) for a in shards],
        scratch_shapes=[pltpu.SemaphoreType.DMA((6 * na,)), pltpu.SemaphoreType.DMA((6 * na,)),
                        pltpu.SemaphoreType.DMA((na,))],
    )(*shards)


def _reduce_scatter(g, core, name):
    n, r, c = g.shape
    hr = r // 2
    once = pl.Buffered(1)
    in_specs = [pl.BlockSpec((n, hr, c), lambda i, core_ref: (0, core_ref[0], 0), pipeline_mode=once),
                pl.BlockSpec((n, hr, c), lambda i, core_ref: (0, 1 - core_ref[0], 0), pipeline_mode=once)]

    def body(core_ref, mine_ref, other_ref, out_ref, from_core, sums, sums_bf, from_chips, done, from_core2, send_sem, recv_sem):
        x, y, cc, k = _place()
        sibling = (x, y, 1 - cc)

        def copy(slot, src, dst, to):
            return pltpu.make_async_remote_copy(src_ref=src, dst_ref=dst, send_sem=send_sem.at[slot],
                                                recv_sem=recv_sem.at[slot], device_id=to, device_id_type=MESH)

        first = copy(0, other_ref, from_core, sibling)
        first.start()
        first.wait()
        total = mine_ref[...] + from_core[...]
        sums[...] = total
        sums_bf[...] = total.astype(BF16)
        sends = [copy(j, sums_bf.at[k ^ j], from_chips.at[j - 1], _chip(k ^ j, cc)) for j in range(1, N_CHIP)]
        for cp in sends:
            cp.start()
        for cp in sends:
            cp.wait()
        red = sums[k]
        for j in range(1, N_CHIP):
            red = red + from_chips[j - 1].astype(F32)
        done[...] = red
        last = copy(N_CHIP, done, from_core2, sibling)
        last.start()
        last.wait()
        row0 = pl.multiple_of(cc * hr, 8)
        row1 = pl.multiple_of((1 - cc) * hr, 8)
        out_ref[pl.ds(row0, hr), :] = red
        out_ref[pl.ds(row1, hr), :] = from_core2[...]

    grid_spec = pltpu.PrefetchScalarGridSpec(
        num_scalar_prefetch=1, grid=(1,), in_specs=in_specs,
        out_specs=pl.BlockSpec((r, c), lambda i, core_ref: (0, 0)),
        scratch_shapes=[pltpu.VMEM((n, hr, c), F32), pltpu.VMEM((n, hr, c), F32), pltpu.VMEM((n, hr, c), BF16),
                        pltpu.VMEM((N_CHIP - 1, hr, c), BF16), pltpu.VMEM((hr, c), F32), pltpu.VMEM((hr, c), F32),
                        pltpu.SemaphoreType.DMA((N_CHIP + 1,)), pltpu.SemaphoreType.DMA((N_CHIP + 1,))])
    return pl.pallas_call(
        body, name=name, grid_spec=grid_spec, out_shape=jax.ShapeDtypeStruct((r, c), F32),
        compiler_params=_params(56),
    )(core, g, g)


def _elementwise(fn, name, ins, n_out, rows):
    total, cols = ins[0].shape
    spec = pl.BlockSpec((rows, cols), lambda i: (i, 0))

    def body(*refs):
        res = fn(*[r[...] for r in refs[:len(ins)]])
        for o, v in zip(refs[len(ins):], res):
            o[...] = v

    return pl.pallas_call(
        body, name=name, grid=(total // rows,),
        in_specs=[spec] * len(ins), out_specs=[spec] * n_out,
        out_shape=[jax.ShapeDtypeStruct((total, cols), F32)] * n_out,
        compiler_params=_params(48),
    )(*ins)


def _adamw(w, g, m, v):
    m = ADAM_B1 * m + (1.0 - ADAM_B1) * g
    v = ADAM_B2 * v + (1.0 - ADAM_B2) * (g * g)
    m_hat = m / (1.0 - ADAM_B1 ** ADAM_STEP)
    v_hat = v / (1.0 - ADAM_B2 ** ADAM_STEP)
    delta = -ADAM_LR * (m_hat / (jnp.sqrt(v_hat) + ADAM_EPS) + ADAM_WD * w)
    return delta, m, v


def _reduce_and_update(grads, weights, moms, vels):
    core = lax.axis_index("c").astype(jnp.int32).reshape(1)
    full = [_reduce_scatter(g, core, f"grads_reduce_scatter_{a}") for a, g in enumerate(grads)]
    out = []
    for a, (g, w, m, v) in enumerate(zip(full, weights, moms, vels)):
        rows = g.shape[0] // 2
        out.append((g,) + tuple(_elementwise(lambda gg, ww, mm, vv: _adamw(ww, gg, mm, vv), f"adamw_{a}", [g, w, m, v], 3, rows)))
    return out


def _reduce_vectors(part, w, m, v):
    n_dev = 8

    def body(p_ref, w_ref, m_ref, v_ref, g_ref, d_ref, nm_ref, nv_ref, buf, send_sem, recv_sem):
        x, y, c, _ = _place()
        me = 4 * x + 2 * y + c
        buf[me] = p_ref[...]
        sends = []
        for off in range(1, n_dev):
            peer = me ^ off
            cp = pltpu.make_async_remote_copy(src_ref=p_ref, dst_ref=buf.at[me], send_sem=send_sem.at[off - 1],
                                              recv_sem=recv_sem.at[off - 1], device_id=(peer >> 2, (peer >> 1) & 1, peer & 1),
                                              device_id_type=MESH)
            cp.start()
            sends.append(cp)
        for off in range(1, n_dev):
            peer = me ^ off
            pltpu.make_async_remote_copy(src_ref=p_ref, dst_ref=buf.at[peer], send_sem=send_sem.at[off - 1],
                                         recv_sem=recv_sem.at[off - 1], device_id=(peer >> 2, (peer >> 1) & 1, peer & 1),
                                         device_id_type=MESH).wait_recv()
        for cp in sends:
            cp.wait_send()
        g = buf[0]
        for d in range(1, n_dev):
            g = g + buf[d]
        g_ref[...] = g
        delta, nm, nv = _adamw(w_ref[...], g, m_ref[...], v_ref[...])
        d_ref[...] = delta
        nm_ref[...] = nm
        nv_ref[...] = nv

    vm = pl.BlockSpec(memory_space=pltpu.VMEM)
    return pl.pallas_call(
        body, name="gains_all_reduce",
        in_specs=[vm] * 4, out_specs=[vm] * 4,
        out_shape=[jax.ShapeDtypeStruct(part.shape, F32)] * 4,
        scratch_shapes=[pltpu.VMEM((n_dev,) + part.shape, F32), pltpu.SemaphoreType.DMA((n_dev - 1,)),
                        pltpu.SemaphoreType.DMA((n_dev - 1,))],
    )(part, w, m, v)


def _pad_row(a):
    a = a.reshape(1, -1)
    return jnp.pad(a, ((0, 0), (0, D_MODEL - a.shape[1])))


def kernel(x, ffn1_norm, ffn1_w_gate, ffn1_w_up, ffn1_w_down, mix_norm, w_in, sb_out_norm, dil_out_norm, w_out, ffn2_norm, ffn2_w_gate, ffn2_w_up, ffn2_w_down, final_norm, loss_target, m_ffn1_norm, m_ffn1_w_gate, m_ffn1_w_up, m_ffn1_w_down, m_mix_norm, m_w_in, m_sb_out_norm, m_dil_out_norm, m_w_out, m_ffn2_norm, m_ffn2_w_gate, m_ffn2_w_up, m_ffn2_w_down, m_final_norm, v_ffn1_norm, v_ffn1_w_gate, v_ffn1_w_up, v_ffn1_w_down, v_mix_norm, v_w_in, v_sb_out_norm, v_dil_out_norm, v_w_out, v_ffn2_norm, v_ffn2_w_gate, v_ffn2_w_up, v_ffn2_w_down, v_final_norm):
    x = x[0]
    target = loss_target[0]
    s = x.shape[0]
    gf = final_norm.reshape(1, D_MODEL)
    cos, sin = _rope_tables(s)

    gu_shard = jnp.stack([ffn1_w_gate[0], ffn1_w_up[0], ffn2_w_gate[0], ffn2_w_up[0]]).astype(BF16)
    wd_shard = jnp.stack([ffn1_w_down[0], ffn2_w_down[0]]).astype(BF16)
    gu, wd, win, wout = _all_gather([gu_shard, wd_shard, w_in[0].astype(BF16), w_out[0].astype(BF16)])
    wout = wout.reshape(D_MODEL, D_MODEL)

    x1, hm = _ffn1_fwd(x, ffn1_norm, mix_norm, gu, wd)
    qkv = _proj_fwd(hm, win, cos, sin)
    o_sb = _sb_fwd(qkv)
    o_dl, lse = _dilated_fwd(qkv)
    x2 = _outproj_fwd(o_sb, o_dl, sb_out_norm, dil_out_norm, x1, wout)
    dx3, st_final = _ffn2_fwd_loss(x2, ffn2_norm, gf, target, gu, wd)

    dx2, dwg2, dwu2, dwd2, st_ffn2 = _ffn_bwd(x2, ffn2_norm, dx3, gu, wd, 1)
    do_sb, do_dl, dwout, st_out = _outproj_bwd(dx2, o_sb, o_dl, sb_out_norm, dil_out_norm, wout)
    dq_sb, dk_sb, dv_sb = _sb_bwd(qkv, o_sb, do_sb)
    dq_dl, dk_dl, dv_dl = _dilated_finish(_dilated_bwd(qkv, o_dl, lse, do_dl), cos, sin)
    dqkv = jnp.concatenate([dq_sb, dk_sb, dv_sb, dq_dl, dk_dl, dv_dl], axis=1)
    dx1, dwin, st_mix = _proj_bwd(x1, mix_norm, dqkv, win, dx2)
    grad_x, dwg1, dwu1, dwd1, st_ffn1 = _ffn_bwd(x, ffn1_norm, dx1, gu, wd, 0)

    names = ["ffn1_w_gate", "ffn1_w_up", "ffn1_w_down", "w_in", "w_out", "ffn2_w_gate", "ffn2_w_up", "ffn2_w_down"]
    grads = [dwg1, dwu1, dwd1, dwin, dwout.reshape(N_CHIP, OUTB, D_MODEL), dwg2, dwu2, dwd2]
    weights = [ffn1_w_gate[0], ffn1_w_up[0], ffn1_w_down[0], w_in[0], w_out[0], ffn2_w_gate[0], ffn2_w_up[0], ffn2_w_down[0]]
    moms = [m_ffn1_w_gate[0], m_ffn1_w_up[0], m_ffn1_w_down[0], m_w_in[0], m_w_out[0], m_ffn2_w_gate[0], m_ffn2_w_up[0], m_ffn2_w_down[0]]
    vels = [v_ffn1_w_gate[0], v_ffn1_w_up[0], v_ffn1_w_down[0], v_w_in[0], v_w_out[0], v_ffn2_w_gate[0], v_ffn2_w_up[0], v_ffn2_w_down[0]]
    mats = {n: tuple(t[None] for t in r) for n, r in zip(names, _reduce_and_update(grads, weights, moms, vels))}

    vec_names = ["ffn1_norm", "mix_norm", "sb_out_norm", "dil_out_norm", "ffn2_norm", "final_norm"]
    part = jnp.concatenate([st_ffn1[0:1], st_mix[0:1], _pad_row(st_out[0]), _pad_row(st_out[1]), st_ffn2[0:1],
                            st_final[0:1], st_final[1:2], jnp.zeros((1, D_MODEL), F32)], axis=0)
    pack = lambda arrs: jnp.concatenate([_pad_row(a) for a in arrs] + [jnp.zeros((2, D_MODEL), F32)], axis=0)
    g_vec, d_vec, m_vec, v_vec = _reduce_vectors(
        part,
        pack([ffn1_norm, mix_norm, sb_out_norm, dil_out_norm, ffn2_norm, final_norm]),
        pack([m_ffn1_norm, m_mix_norm, m_sb_out_norm, m_dil_out_norm, m_ffn2_norm, m_final_norm]),
        pack([v_ffn1_norm, v_mix_norm, v_sb_out_norm, v_dil_out_norm, v_ffn2_norm, v_final_norm]))
    like = {"ffn1_norm": ffn1_norm, "mix_norm": mix_norm, "sb_out_norm": sb_out_norm, "dil_out_norm": dil_out_norm,
            "ffn2_norm": ffn2_norm, "final_norm": final_norm}
    vecs = {n: tuple(t[i, :like[n].size].reshape(like[n].shape) for t in (g_vec, d_vec, m_vec, v_vec))
            for i, n in enumerate(vec_names)}
    loss = 0.5 * jnp.sum(g_vec[6]) / D_MODEL

    order = ["ffn1_norm", "ffn1_w_gate", "ffn1_w_up", "ffn1_w_down", "mix_norm", "w_in", "sb_out_norm", "dil_out_norm",
             "w_out", "ffn2_norm", "ffn2_w_gate", "ffn2_w_up", "ffn2_w_down", "final_norm"]
    both = {**mats, **vecs}
    return (loss, grad_x[None], *[both[n][0] for n in order], *[both[n][1] for n in order],
            *[both[n][2] for n in order], *[both[n][3] for n in order])
```

```python
import functools

import jax
import jax.numpy as jnp
from jax import lax
from jax.experimental import pallas as pl
from jax.experimental.pallas import tpu as pltpu

D_MODEL = 1024
D_FF = 2816
HEAD_DIM = 64
D_SB = 512
D_DIL = 512
D_IN = 3072
N_CHIP = 4
FFB = D_FF // N_CHIP
INB = D_IN // N_CHIP
OUTB = D_MODEL // N_CHIP
BLK = 128
LANES = 128
DILATIONS = (1, 4, 16)
ROPE_THETA = 10000.0
RMS_EPS = 1e-6
SCALE = HEAD_DIM ** -0.5
NEG = -1e30
DEAD = -104.0
ADAM_LR = 0.001
ADAM_B1 = 0.9
ADAM_B2 = 0.999
ADAM_EPS = 1e-08
ADAM_WD = 0.01
ADAM_STEP = 10
MESH = pl.DeviceIdType.MESH
F32 = jnp.float32
BF16 = jnp.bfloat16
TM = 512


def _params(vmem_mb):
    return pltpu.CompilerParams(vmem_limit_bytes=vmem_mb << 20)


def _dot(a, b):
    return jnp.dot(a, b, preferred_element_type=F32)


def _dot_nt(a, b):
    return lax.dot_general(a, b, (((1,), (1,)), ((), ())), preferred_element_type=F32)


def _dot_tn(a, b):
    return lax.dot_general(a, b, (((0,), (0,)), ((), ())), preferred_element_type=F32)


def _rms_fwd(x, g):
    r = lax.rsqrt(jnp.mean(x * x, axis=-1, keepdims=True) + RMS_EPS)
    xh = x * r
    return xh * g, xh, r


def _rms_bwd(dy, xh, r, g):
    dyg = dy * g
    dx = r * (dyg - xh * jnp.mean(dyg * xh, axis=-1, keepdims=True))
    return dx, jnp.sum(dy * xh, axis=0, keepdims=True)


def _split_bf16(a):
    hi = a.astype(BF16)
    return hi, (a - hi.astype(F32)).astype(BF16)


def _dot_split(a, b):
    hi, lo = _split_bf16(a)
    return _dot(hi, b) + _dot(lo, b)


def _ffn_weight_specs(f):
    return [pl.BlockSpec((None, None, D_MODEL, FFB), lambda i, j: (j, 2 * f, 0, 0)),
            pl.BlockSpec((None, None, D_MODEL, FFB), lambda i, j: (j, 2 * f + 1, 0, 0)),
            pl.BlockSpec((None, None, FFB, D_MODEL), lambda i, j: (j, f, 0, 0))]


def _ffn_accumulate(h_scr, acc_scr, wg_ref, wu_ref, wd_ref):
    h = h_scr[...]
    a = _dot(h, wg_ref[...])
    b = _dot(h, wu_ref[...])
    act = (a * jax.nn.sigmoid(a)) * b
    acc_scr[...] += _dot(act.astype(BF16), wd_ref[...])


def _ffn1_fwd(x, g1, gmix, gu, wd):
    s = x.shape[0]
    row = pl.BlockSpec((TM, D_MODEL), lambda i, j: (i, 0))
    vec = pl.BlockSpec((1, D_MODEL), lambda i, j: (0, 0))

    def body(x_ref, g_ref, gm_ref, wg_ref, wu_ref, wd_ref, x1_ref, hm_ref, h_scr, acc_scr):
        j = pl.program_id(1)

        @pl.when(j == 0)
        def _():
            h, _, _ = _rms_fwd(x_ref[...], g_ref[...])
            h_scr[...] = h.astype(BF16)
            acc_scr[...] = jnp.zeros_like(acc_scr)

        _ffn_accumulate(h_scr, acc_scr, wg_ref, wu_ref, wd_ref)

        @pl.when(j == N_CHIP - 1)
        def _():
            x1 = x_ref[...] + 0.5 * acc_scr[...]
            x1_ref[...] = x1
            hm, _, _ = _rms_fwd(x1, gm_ref[...])
            hm_ref[...] = hm.astype(BF16)

    return pl.pallas_call(
        body, name="ffn1_fwd", grid=(s // TM, N_CHIP),
        in_specs=[row, vec, vec] + _ffn_weight_specs(0),
        out_specs=[row, row],
        out_shape=[jax.ShapeDtypeStruct((s, D_MODEL), F32), jax.ShapeDtypeStruct((s, D_MODEL), BF16)],
        scratch_shapes=[pltpu.VMEM((TM, D_MODEL), BF16), pltpu.VMEM((TM, D_MODEL), F32)],
        compiler_params=_params(48),
    )(x, g1, gmix, gu, gu, wd)


def _ffn2_fwd_loss(x2, g2, gf, target, gu, wd):
    s = x2.shape[0]
    row = pl.BlockSpec((TM, D_MODEL), lambda i, j: (i, 0))
    vec = pl.BlockSpec((1, D_MODEL), lambda i, j: (0, 0))
    stat = pl.BlockSpec((8, D_MODEL), lambda i, j: (0, 0))

    def body(x_ref, g_ref, gf_ref, t_ref, wg_ref, wu_ref, wd_ref, dx_ref, st_ref, h_scr, acc_scr):
        i, j = pl.program_id(0), pl.program_id(1)

        @pl.when((i == 0) & (j == 0))
        def _():
            st_ref[...] = jnp.zeros_like(st_ref)

        @pl.when(j == 0)
        def _():
            h, _, _ = _rms_fwd(x_ref[...], g_ref[...])
            h_scr[...] = h.astype(BF16)
            acc_scr[...] = jnp.zeros_like(acc_scr)

        _ffn_accumulate(h_scr, acc_scr, wg_ref, wu_ref, wd_ref)

        @pl.when(j == N_CHIP - 1)
        def _():
            x3 = x_ref[...] + 0.5 * acc_scr[...]
            y, xh, r = _rms_fwd(x3, gf_ref[...])
            err = y - t_ref[...]
            dx, dg = _rms_bwd(err * (1.0 / D_MODEL), xh, r, gf_ref[...])
            dx_ref[...] = dx
            st_ref[0:1, :] += dg
            st_ref[1:2, :] += jnp.sum(err * err, axis=0, keepdims=True)

    return pl.pallas_call(
        body, name="ffn2_fwd_loss", grid=(s // TM, N_CHIP),
        in_specs=[row, vec, vec, row] + _ffn_weight_specs(1),
        out_specs=[row, stat],
        out_shape=[jax.ShapeDtypeStruct((s, D_MODEL), F32), jax.ShapeDtypeStruct((8, D_MODEL), F32)],
        scratch_shapes=[pltpu.VMEM((TM, D_MODEL), BF16), pltpu.VMEM((TM, D_MODEL), F32)],
        compiler_params=_params(48),
    )(x2, g2, gf, target, gu, gu, wd)


def _ffn_bwd(xin, g, dy, gu, wd, f):
    s = xin.shape[0]
    row = pl.BlockSpec((TM, D_MODEL), lambda i, j: (i, 0))
    vec = pl.BlockSpec((1, D_MODEL), lambda i, j: (0, 0))
    stat = pl.BlockSpec((8, D_MODEL), lambda i, j: (0, 0))
    hid = pl.BlockSpec((None, TM, FFB), lambda i, j: (j, i, 0))

    def body(x_ref, g_ref, dy_ref, wg_ref, wu_ref, wd_ref, out_ref, h_ref, dyh_ref, da_ref, db_ref, act_ref, st_ref, dh_scr):
        i, j = pl.program_id(0), pl.program_id(1)

        @pl.when((i == 0) & (j == 0))
        def _():
            st_ref[...] = jnp.zeros_like(st_ref)

        @pl.when(j == 0)
        def _():
            h, _, _ = _rms_fwd(x_ref[...], g_ref[...])
            h_ref[...] = h.astype(BF16)
            dyh_ref[...] = (0.5 * dy_ref[...]).astype(BF16)
            dh_scr[...] = jnp.zeros_like(dh_scr)

        hb = h_ref[...]
        a = _dot(hb, wg_ref[...])
        b = _dot(hb, wu_ref[...])
        sg = jax.nn.sigmoid(a)
        sa = a * sg
        dact = _dot_nt(dyh_ref[...], wd_ref[...])
        dab = (dact * b * (sg * (1.0 + a * (1.0 - sg)))).astype(BF16)
        dbb = (dact * sa).astype(BF16)
        da_ref[...] = dab
        db_ref[...] = dbb
        act_ref[...] = (sa * b).astype(BF16)
        dh_scr[...] += _dot_nt(dab, wg_ref[...]) + _dot_nt(dbb, wu_ref[...])

        @pl.when(j == N_CHIP - 1)
        def _():
            _, xh, r = _rms_fwd(x_ref[...], g_ref[...])
            dx, dg = _rms_bwd(dh_scr[...], xh, r, g_ref[...])
            out_ref[...] = dy_ref[...] + dx
            st_ref[0:1, :] += dg

    hidden = jax.ShapeDtypeStruct((N_CHIP, s, FFB), BF16)
    dx, hb, dyh, da, db, act, st = pl.pallas_call(
        body, name=f"ffn{f + 1}_bwd_dx", grid=(s // TM, N_CHIP),
        in_specs=[row, vec, row] + _ffn_weight_specs(f),
        out_specs=[row, row, row, hid, hid, hid, stat],
        out_shape=[jax.ShapeDtypeStruct((s, D_MODEL), F32), jax.ShapeDtypeStruct((s, D_MODEL), BF16),
                   jax.ShapeDtypeStruct((s, D_MODEL), BF16), hidden, hidden, hidden,
                   jax.ShapeDtypeStruct((8, D_MODEL), F32)],
        scratch_shapes=[pltpu.VMEM((TM, D_MODEL), F32)],
        compiler_params=_params(56),
    )(xin, g, dy, gu, gu, wd)

    tok = pl.BlockSpec((TM, D_MODEL), lambda j, i: (i, 0))
    hid2 = pl.BlockSpec((None, TM, FFB), lambda j, i: (j, i, 0))
    gspecs = [pl.BlockSpec((None, D_MODEL, FFB), lambda j, i: (j, 0, 0)),
              pl.BlockSpec((None, D_MODEL, FFB), lambda j, i: (j, 0, 0)),
              pl.BlockSpec((None, FFB, D_MODEL), lambda j, i: (j, 0, 0))]

    def wbody(h_ref, dyh_ref, da_ref, db_ref, act_ref, dwg_ref, dwu_ref, dwd_ref):
        @pl.when(pl.program_id(1) == 0)
        def _():
            dwg_ref[...] = jnp.zeros_like(dwg_ref)
            dwu_ref[...] = jnp.zeros_like(dwu_ref)
            dwd_ref[...] = jnp.zeros_like(dwd_ref)

        hb = h_ref[...]
        dwg_ref[...] += _dot_tn(hb, da_ref[...])
        dwu_ref[...] += _dot_tn(hb, db_ref[...])
        dwd_ref[...] += _dot_tn(act_ref[...], dyh_ref[...])

    dwg, dwu, dwd = pl.pallas_call(
        wbody, name=f"ffn{f + 1}_bwd_dw", grid=(N_CHIP, s // TM),
        in_specs=[tok, tok, hid2, hid2, hid2], out_specs=gspecs,
        out_shape=[jax.ShapeDtypeStruct((N_CHIP, D_MODEL, FFB), F32),
                   jax.ShapeDtypeStruct((N_CHIP, D_MODEL, FFB), F32),
                   jax.ShapeDtypeStruct((N_CHIP, FFB, D_MODEL), F32)],
        compiler_params=_params(48),
    )(hb, dyh, da, db, act)
    return dx, dwg, dwu, dwd, st


def _rope_tables(s):
    half = HEAD_DIM // 2
    inv_freq = ROPE_THETA ** (-jnp.arange(half, dtype=F32) / half)
    ang = jnp.arange(s).astype(F32)[:, None] * inv_freq[None, :]
    cos, sin = jnp.cos(ang), jnp.sin(ang)
    cos2 = jnp.concatenate([cos, cos], axis=-1)
    sin2 = jnp.concatenate([-sin, sin], axis=-1)
    return jnp.tile(cos2, (1, LANES // HEAD_DIM)), jnp.tile(sin2, (1, LANES // HEAD_DIM))


def _rotate(t, cos, sin_signed):
    lane = lax.broadcasted_iota(jnp.int32, t.shape, 1)
    first = (lane % HEAD_DIM) < (HEAD_DIM // 2)
    partner = jnp.where(first, pltpu.roll(t, LANES - HEAD_DIM // 2, 1), pltpu.roll(t, HEAD_DIM // 2, 1))
    return t * cos + partner * sin_signed


def _proj_fwd(hm, win, cos, sin):
    s = hm.shape[0]
    n_sub = INB // LANES
    first_rot, last_rot = (3 * D_SB) // LANES, (3 * D_SB + 2 * D_DIL) // LANES

    def body(h_ref, w_ref, c_ref, s_ref, o_ref):
        j = pl.program_id(1)
        r = _dot(h_ref[...], w_ref[...])
        for c in range(n_sub):
            t = r[:, c * LANES:(c + 1) * LANES]
            col = j * n_sub + c
            rot = (col >= first_rot) & (col < last_rot)
            lanes = slice(c * LANES, (c + 1) * LANES)

            @pl.when(rot)
            def _():
                o_ref[:, lanes] = _rotate(t, c_ref[...], s_ref[...]).astype(BF16)

            @pl.when(jnp.logical_not(rot))
            def _():
                o_ref[:, lanes] = t.astype(BF16)

    return pl.pallas_call(
        body, name="proj_fwd", grid=(s // TM, N_CHIP),
        in_specs=[pl.BlockSpec((TM, D_MODEL), lambda i, j: (i, 0)),
                  pl.BlockSpec((None, D_MODEL, INB), lambda i, j: (j, 0, 0)),
                  pl.BlockSpec((TM, LANES), lambda i, j: (i, 0)),
                  pl.BlockSpec((TM, LANES), lambda i, j: (i, 0))],
        out_specs=pl.BlockSpec((TM, INB), lambda i, j: (i, j)),
        out_shape=jax.ShapeDtypeStruct((s, D_IN), BF16),
        compiler_params=_params(32),
    )(hm, win, cos, sin)


def _proj_bwd(x1, gmix, dqkv, win, dx2):
    s = x1.shape[0]
    row = pl.BlockSpec((TM, D_MODEL), lambda i, j: (i, 0))
    vec = pl.BlockSpec((1, D_MODEL), lambda i, j: (0, 0))

    def body(x_ref, g_ref, dq_ref, w_ref, dx2_ref, out_ref, dw_ref, st_ref, h_scr, dh_scr):
        i, j = pl.program_id(0), pl.program_id(1)

        @pl.when((i == 0) & (j == 0))
        def _():
            st_ref[...] = jnp.zeros_like(st_ref)
            dw_ref[...] = jnp.zeros_like(dw_ref)

        @pl.when(j == 0)
        def _():
            h, _, _ = _rms_fwd(x_ref[...], g_ref[...])
            h_scr[...] = h.astype(BF16)
            dh_scr[...] = jnp.zeros_like(dh_scr)

        dq = dq_ref[...]
        dw_ref[j] += _dot_tn(h_scr[...], dq)
        dh_scr[...] += _dot_nt(dq, w_ref[...])

        @pl.when(j == N_CHIP - 1)
        def _():
            _, xh, r = _rms_fwd(x_ref[...], g_ref[...])
            dx, dg = _rms_bwd(dh_scr[...], xh, r, g_ref[...])
            out_ref[...] = dx2_ref[...] + dx
            st_ref[0:1, :] += dg

    return pl.pallas_call(
        body, name="proj_bwd", grid=(s // TM, N_CHIP),
        in_specs=[row, vec, pl.BlockSpec((TM, INB), lambda i, j: (i, j)),
                  pl.BlockSpec((None, D_MODEL, INB), lambda i, j: (j, 0, 0)), row],
        out_specs=[row, pl.BlockSpec((N_CHIP, D_MODEL, INB), lambda i, j: (0, 0, 0)),
                   pl.BlockSpec((8, D_MODEL), lambda i, j: (0, 0))],
        out_shape=[jax.ShapeDtypeStruct((s, D_MODEL), F32),
                   jax.ShapeDtypeStruct((N_CHIP, D_MODEL, INB), F32),
                   jax.ShapeDtypeStruct((8, D_MODEL), F32)],
        scratch_shapes=[pltpu.VMEM((TM, D_MODEL), BF16), pltpu.VMEM((TM, D_MODEL), F32)],
        compiler_params=_params(56),
    )(x1, gmix, dqkv, win, dx2)


def _outproj_fwd(o_sb, o_dl, g_sb, g_dl, x1, wout):
    s = x1.shape[0]
    half = pl.BlockSpec((TM, D_SB), lambda i: (i, 0))
    row = pl.BlockSpec((TM, D_MODEL), lambda i: (i, 0))
    vec = pl.BlockSpec((1, D_SB), lambda i: (0, 0))

    def body(a_ref, b_ref, ga_ref, gb_ref, x_ref, w_ref, o_ref):
        ma, _, _ = _rms_fwd(a_ref[...], ga_ref[...])
        mb, _, _ = _rms_fwd(b_ref[...], gb_ref[...])
        o_ref[...] = (x_ref[...] + _dot(ma.astype(BF16), w_ref[0:D_SB, :])
                      + _dot(mb.astype(BF16), w_ref[D_SB:D_MODEL, :]))

    return pl.pallas_call(
        body, name="outproj_fwd", grid=(s // TM,),
        in_specs=[half, half, vec, vec, row, pl.BlockSpec((D_MODEL, D_MODEL), lambda i: (0, 0))],
        out_specs=row, out_shape=jax.ShapeDtypeStruct((s, D_MODEL), F32),
        compiler_params=_params(32),
    )(o_sb, o_dl, g_sb, g_dl, x1, wout)


def _outproj_bwd(dx2, o_sb, o_dl, g_sb, g_dl, wout):
    s = dx2.shape[0]
    half = pl.BlockSpec((TM, D_SB), lambda i: (i, 0))
    row = pl.BlockSpec((TM, D_MODEL), lambda i: (i, 0))
    vec = pl.BlockSpec((1, D_SB), lambda i: (0, 0))
    full = pl.BlockSpec((D_MODEL, D_MODEL), lambda i: (0, 0))

    def body(dy_ref, a_ref, b_ref, ga_ref, gb_ref, w_ref, da_ref, db_ref, dw_ref, st_ref):
        @pl.when(pl.program_id(0) == 0)
        def _():
            dw_ref[...] = jnp.zeros_like(dw_ref)
            st_ref[...] = jnp.zeros_like(st_ref)

        dy = dy_ref[...].astype(BF16)
        dm = _dot_nt(dy, w_ref[...])
        ma, xa, ra = _rms_fwd(a_ref[...], ga_ref[...])
        mb, xb, rb = _rms_fwd(b_ref[...], gb_ref[...])
        dw_ref[0:D_SB, :] += _dot_tn(ma.astype(BF16), dy)
        dw_ref[D_SB:D_MODEL, :] += _dot_tn(mb.astype(BF16), dy)
        da, dga = _rms_bwd(dm[:, 0:D_SB], xa, ra, ga_ref[...])
        db, dgb = _rms_bwd(dm[:, D_SB:D_MODEL], xb, rb, gb_ref[...])
        da_ref[...] = da
        db_ref[...] = db
        st_ref[0:1, :] += dga
        st_ref[1:2, :] += dgb

    return pl.pallas_call(
        body, name="outproj_bwd", grid=(s // TM,),
        in_specs=[row, half, half, vec, vec, full],
        out_specs=[half, half, full, pl.BlockSpec((8, D_SB), lambda i: (0, 0))],
        out_shape=[jax.ShapeDtypeStruct((s, D_SB), F32), jax.ShapeDtypeStruct((s, D_SB), F32),
                   jax.ShapeDtypeStruct((D_MODEL, D_MODEL), F32), jax.ShapeDtypeStruct((8, D_SB), F32)],
        compiler_params=_params(48),
    )(dx2, o_sb, o_dl, g_sb, g_dl, wout)


def _head_masks():
    lane = lax.broadcasted_iota(jnp.int32, (BLK, LANES), 1)
    return [lane < HEAD_DIM, lane >= HEAD_DIM]


def _keep(mask, a):
    return a * jnp.where(mask, 1.0, 0.0).astype(a.dtype)


def _suffix_matrices():
    r = lax.broadcasted_iota(jnp.int32, (BLK, BLK), 0)
    c = lax.broadcasted_iota(jnp.int32, (BLK, BLK), 1)
    ones = jnp.ones((BLK, BLK), BF16)
    excl = jnp.concatenate([(r > c).astype(BF16), ones], axis=1)
    incl = jnp.concatenate([(r >= c).astype(BF16), ones], axis=1)
    return excl, incl


def _blk(i):
    return pl.ds(pl.multiple_of(i * BLK, BLK), BLK)


def _alive(carry_m):
    return (jnp.max(carry_m) > DEAD).astype(jnp.int32)


def _more_keys(i, carry):
    return (carry[0] <= i) & (carry[1] > 0)


def _stack_heads(a):
    masks = _head_masks()
    return jnp.concatenate([_keep(masks[0], a), _keep(masks[1], a)], axis=0)


def _unstack_heads(a2):
    return jnp.where(_head_masks()[0], a2[:BLK], a2[BLK:])


def _head_rowsum(a):
    masks = _head_masks()
    return jnp.concatenate([jnp.sum(jnp.where(m, a, 0.0), axis=1, keepdims=True) for m in masks], axis=0)


SB_QB = 2
SB_ROWS = SB_QB * 2 * BLK


def _sb_rows(ref, i0, cast=None):
    tiles = [ref[_blk(i0 + t), :] for t in range(SB_QB)]
    return jnp.concatenate([_stack_heads(t if cast is None else t.astype(cast)) for t in tiles], axis=0)


def _sb_scores(q2, k, i, j, carry_m, u_excl):
    r = lax.broadcasted_iota(jnp.int32, (SB_ROWS, BLK), 0)
    row = (r & (BLK - 1)) + ((r >> 8) << 7)
    col = lax.broadcasted_iota(jnp.int32, (SB_ROWS, BLK), 1)
    valid = (j * BLK + col) < (i * BLK + row)
    z = _dot_nt(q2, k) * SCALE
    sp = jnp.maximum(z, 0.0) + jnp.log(1.0 + jnp.exp(-jnp.abs(z)))
    log_stay = jnp.where(valid, -sp, 0.0)
    log_beta = z - sp
    sums = _dot_split(log_stay, u_excl)
    later = carry_m + sums[:, :BLK]
    w = jnp.where(valid, jnp.exp(log_beta + later), 0.0)
    return valid, log_beta, w, carry_m + sums[:, BLK:]


def _sb_fwd(qkv):
    s = qkv.shape[0]
    nq = s // BLK
    pairs = D_SB // LANES
    col = lambda off: pl.BlockSpec((s, LANES), lambda p: (0, off + p))

    def body(q_ref, k_ref, v_ref, o_ref):
        u_excl, _ = _suffix_matrices()
        zero = jnp.zeros((SB_ROWS, LANES), F32)

        def q_block(ib, _):
            i = ib * SB_QB
            last = i + SB_QB - 1
            q2 = _sb_rows(q_ref, i)

            def k_block(carry):
                jj, _, carry_m, acc = carry
                j = last - jj
                _, _, w, carry_m = _sb_scores(q2, k_ref[_blk(j), :], i, j, carry_m, u_excl)
                return jj + 1, _alive(carry_m), carry_m, acc + _dot(w.astype(BF16), v_ref[_blk(j), :])

            _, _, _, acc = lax.while_loop(functools.partial(_more_keys, last), k_block,
                                          (jnp.int32(0), jnp.int32(1), zero, zero))
            for t in range(SB_QB):
                o_ref[_blk(i + t), :] = _unstack_heads(acc[2 * BLK * t:2 * BLK * (t + 1)])
            return 0

        lax.fori_loop(0, nq // SB_QB, q_block, 0)

    return pl.pallas_call(
        body, name="sb_fwd", grid=(pairs,),
        in_specs=[col(0), col(pairs), col(2 * pairs)],
        out_specs=pl.BlockSpec((s, LANES), lambda p: (0, p)),
        out_shape=jax.ShapeDtypeStruct((s, D_SB), F32),
        compiler_params=_params(48),
    )(qkv, qkv, qkv)


def _sb_bwd(qkv, o_sb, do_sb):
    s = qkv.shape[0]
    nq = s // BLK
    pairs = D_SB // LANES
    col = lambda off: pl.BlockSpec((s, LANES), lambda p: (0, off + p))
    own = pl.BlockSpec((s, LANES), lambda p: (0, p))

    def body(q_ref, k_ref, v_ref, o_ref, do_ref, dq_ref, dk_ref, dv_ref, dk_acc, dv_acc):
        u_excl, u_incl = _suffix_matrices()
        zero = jnp.zeros((SB_ROWS, LANES), F32)
        dk_acc[...] = jnp.zeros_like(dk_acc)
        dv_acc[...] = jnp.zeros_like(dv_acc)

        def q_block(ib, _):
            i = ib * SB_QB
            last = i + SB_QB - 1
            q2 = _sb_rows(q_ref, i)
            do2 = _sb_rows(do_ref, i, BF16)
            totals = [_head_rowsum(do_ref[_blk(i + t), :].astype(BF16).astype(F32) * o_ref[_blk(i + t), :])
                      for t in range(SB_QB)]
            total = jnp.broadcast_to(jnp.concatenate(totals, axis=0), (SB_ROWS, BLK))

            def k_block(carry):
                jj, _, carry_m, carry_g, dq = carry
                j = last - jj
                k = k_ref[_blk(j), :]
                valid, log_beta, w, carry_m = _sb_scores(q2, k, i, j, carry_m, u_excl)
                wb = w.astype(BF16)
                g = wb.astype(F32) * _dot_nt(do2, v_ref[_blk(j), :])
                sums = _dot_split(g, u_incl)
                before = total - (carry_g + sums[:, :BLK])
                dz = jnp.where(valid, g - jnp.exp(log_beta) * (g + before), 0.0)
                dzb = (dz * SCALE).astype(BF16)
                dk_acc[_blk(j), :] += _dot_tn(dzb, q2)
                dv_acc[_blk(j), :] += _dot_tn(wb, do2)
                return jj + 1, _alive(carry_m), carry_m, carry_g + sums[:, BLK:], dq + _dot(dzb, k)

            _, _, _, _, dq = lax.while_loop(functools.partial(_more_keys, last), k_block,
                                            (jnp.int32(0), jnp.int32(1), zero, zero, zero))
            for t in range(SB_QB):
                dq_ref[_blk(i + t), :] = _unstack_heads(dq[2 * BLK * t:2 * BLK * (t + 1)]).astype(BF16)
            return 0

        lax.fori_loop(0, nq // SB_QB, q_block, 0)
        dk_ref[...] = dk_acc[...].astype(BF16)
        dv_ref[...] = dv_acc[...].astype(BF16)

    return pl.pallas_call(
        body, name="sb_bwd", grid=(pairs,),
        in_specs=[col(0), col(pairs), col(2 * pairs), own, own],
        out_specs=[own, own, own],
        out_shape=[jax.ShapeDtypeStruct((s, D_SB), BF16)] * 3,
        scratch_shapes=[pltpu.VMEM((s, LANES), F32), pltpu.VMEM((s, LANES), F32)],
        compiler_params=_params(56),
    )(qkv, qkv, qkv, o_sb, do_sb)


def _band_masks(b):
    row = lax.broadcasted_iota(jnp.int32, (2 * BLK, BLK), 0) & (BLK - 1)
    col = lax.broadcasted_iota(jnp.int32, (2 * BLK, BLK), 1)
    return col <= row, (col - row) >= jnp.where(b > 0, 0, BLK)


def _dil_tiles(qf, kf, vf, d, t, nb):
    c, b = t // nb, t % nb
    start = c + d * BLK * b
    rows = pl.ds(start, BLK, stride=d)
    prev = pl.ds(jnp.where(b > 0, start - d * BLK, start), BLK, stride=d)
    bf = lambda ref, sl: ref[sl, :].astype(BF16)
    return b, rows, prev, _stack_heads(bf(qf, rows)), bf(kf, rows), bf(kf, prev), bf(vf, rows), bf(vf, prev)


def _lanes_of_heads(col2):
    return _unstack_heads(jnp.broadcast_to(col2, (2 * BLK, LANES)))


def _dilated_fwd(qkv):
    s = qkv.shape[0]
    pairs = D_DIL // LANES
    base = (3 * D_SB) // LANES
    col = lambda off: pl.BlockSpec((s, LANES), lambda p: (0, off + p))
    own = pl.BlockSpec((s, LANES), lambda p: (0, p))

    def body(q_ref, k_ref, v_ref, acc_ref, m_ref, qf, kf, vf, l_scr):
        qf[...] = q_ref[...].astype(F32)
        kf[...] = k_ref[...].astype(F32)
        vf[...] = v_ref[...].astype(F32)
        for d in DILATIONS:
            nb = s // (d * BLK)

            def block(t, _):
                b, rows, prev, q2, kc, kp, vc, vp = _dil_tiles(qf, kf, vf, d, t, nb)
                in_cur, in_prev = _band_masks(b)
                zc = jnp.where(in_cur, _dot_nt(q2, kc) * SCALE, NEG)
                zp = jnp.where(in_prev, _dot_nt(q2, kp) * SCALE, NEG)
                m = jnp.maximum(jnp.max(zc, axis=1, keepdims=True), jnp.max(zp, axis=1, keepdims=True))
                pc, pp = jnp.exp(zc - m), jnp.exp(zp - m)
                den = jnp.sum(pc, axis=1, keepdims=True) + jnp.sum(pp, axis=1, keepdims=True)
                acc = _unstack_heads(_dot(pc.astype(BF16), vc) + _dot(pp.astype(BF16), vp))
                m_t, l_t = _lanes_of_heads(m), _lanes_of_heads(den)
                if d == DILATIONS[0]:
                    m_ref[rows, :] = m_t
                    l_scr[rows, :] = l_t
                    acc_ref[rows, :] = acc
                else:
                    m_old = m_ref[rows, :]
                    m_new = jnp.maximum(m_old, m_t)
                    keep, add = jnp.exp(m_old - m_new), jnp.exp(m_t - m_new)
                    m_ref[rows, :] = m_new
                    l_scr[rows, :] = l_scr[rows, :] * keep + l_t * add
                    acc_ref[rows, :] = acc_ref[rows, :] * keep + acc * add
                return 0

            lax.fori_loop(0, s // BLK, block, 0, unroll=2)

        def finish(i, _):
            l = l_scr[_blk(i), :]
            acc_ref[_blk(i), :] = acc_ref[_blk(i), :] / l
            m_ref[_blk(i), :] = m_ref[_blk(i), :] + jnp.log(l)
            return 0

        lax.fori_loop(0, s // BLK, finish, 0)

    return pl.pallas_call(
        body, name="dilated_fwd", grid=(pairs,),
        in_specs=[col(base), col(base + pairs), col(base + 2 * pairs)],
        out_specs=[own, own],
        out_shape=[jax.ShapeDtypeStruct((s, D_DIL), F32)] * 2,
        scratch_shapes=[pltpu.VMEM((s, LANES), F32)] * 4,
        compiler_params=_params(56),
    )(qkv, qkv, qkv)


def _dilated_bwd(qkv, out, lse, dout):
    s = qkv.shape[0]
    pairs = D_DIL // LANES
    base = (3 * D_SB) // LANES
    once = pl.Buffered(1)
    col = lambda off: pl.BlockSpec((s, LANES), lambda p: (0, off + p), pipeline_mode=once)
    own = pl.BlockSpec((s, LANES), lambda p: (0, p), pipeline_mode=once)
    res = pl.BlockSpec((s, LANES), lambda p: (0, p))

    def body(q_ref, k_ref, v_ref, o_ref, l_ref, do_ref, dq_ref, dk_ref, dv_ref, qf, kf, vf):
        masks = _head_masks()
        qf[...] = q_ref[...].astype(F32)
        kf[...] = k_ref[...].astype(F32)
        vf[...] = v_ref[...].astype(F32)
        dq_ref[...] = jnp.zeros_like(dq_ref)
        dk_ref[...] = jnp.zeros_like(dk_ref)
        dv_ref[...] = jnp.zeros_like(dv_ref)
        for d in DILATIONS:
            nb = s // (d * BLK)

            def block(t, _):
                b, rows, prev, q2, kc, kp, vc, vp = _dil_tiles(qf, kf, vf, d, t, nb)
                in_cur, in_prev = _band_masks(b)
                do32 = do_ref[rows, :]
                do2 = _stack_heads(do32.astype(BF16))
                delta = _head_rowsum(do32 * o_ref[rows, :])
                lse_t = l_ref[rows, :]
                lse2 = jnp.concatenate([jnp.max(jnp.where(m, lse_t, NEG), axis=1, keepdims=True) for m in masks], axis=0)
                wc = jnp.exp(jnp.where(in_cur, _dot_nt(q2, kc) * SCALE, NEG) - lse2)
                wp = jnp.exp(jnp.where(in_prev, _dot_nt(q2, kp) * SCALE, NEG) - lse2)
                dzc = (wc * (_dot_nt(do2, vc) - delta) * SCALE).astype(BF16)
                dzp = (wp * (_dot_nt(do2, vp) - delta) * SCALE).astype(BF16)
                dq_ref[rows, :] += _unstack_heads(_dot(dzc, kc) + _dot(dzp, kp))
                dk_ref[rows, :] += _dot_tn(dzc, q2)
                dk_ref[prev, :] += _dot_tn(dzp, q2)
                dv_ref[rows, :] += _dot_tn(wc.astype(BF16), do2)
                dv_ref[prev, :] += _dot_tn(wp.astype(BF16), do2)
                return 0

            lax.fori_loop(0, s // BLK, block, 0, unroll=2)

    return pl.pallas_call(
        body, name="dilated_bwd", grid=(pairs,),
        in_specs=[col(base), col(base + pairs), col(base + 2 * pairs), own, own, own],
        out_specs=[res, res, res],
        out_shape=[jax.ShapeDtypeStruct((s, D_DIL), F32)] * 3,
        scratch_shapes=[pltpu.VMEM((s, LANES), F32)] * 3,
        compiler_params=_params(60),
    )(qkv, qkv, qkv, out, lse, dout)


def _dilated_finish(grads, cos, sin):
    s = grads[0].shape[0]
    spec = pl.BlockSpec((TM, D_DIL), lambda i: (i, 0))
    tab = pl.BlockSpec((TM, LANES), lambda i: (i, 0))

    def body(dq_ref, dk_ref, dv_ref, c_ref, s_ref, oq_ref, ok_ref, ov_ref):
        for src, dst, rotated in ((dq_ref, oq_ref, True), (dk_ref, ok_ref, True), (dv_ref, ov_ref, False)):
            for c in range(D_DIL // LANES):
                lanes = slice(c * LANES, (c + 1) * LANES)
                piece = src[:, lanes]
                dst[:, lanes] = (_rotate(piece, c_ref[...], -s_ref[...]) if rotated else piece).astype(BF16)

    return pl.pallas_call(
        body, name="dilated_finish", grid=(s // TM,),
        in_specs=[spec] * 3 + [tab, tab], out_specs=[spec] * 3,
        out_shape=[jax.ShapeDtypeStruct((s, D_DIL), BF16)] * 3,
        compiler_params=_params(32),
    )(*grads, cos, sin)


def _place():
    x, y, c = lax.axis_index("x"), lax.axis_index("y"), lax.axis_index("c")
    return x, y, c, 2 * x + y


def _chip(k, c):
    return (k >> 1, k & 1, c)


def _half(ref, h):
    n = ref.shape[0] // 2
    return ref.at[pl.ds(h * n, n)]


def _all_gather(shards):
    na = len(shards)
    any_spec = pl.BlockSpec(memory_space=pl.ANY)

    def body(*refs):
        ins, outs = refs[:na], refs[na:2 * na]
        send_sem, recv_sem, local_sem = refs[2 * na:]
        x, y, c, k = _place()
        sibling = (x, y, 1 - c)
        started = []
        for a in range(na):
            cp = pltpu.make_async_copy(ins[a], outs[a].at[k], local_sem.at[a])
            cp.start()
            started.append(cp)

        def copy(a, slot, src, dst, to):
            return pltpu.make_async_remote_copy(src_ref=src, dst_ref=dst, send_sem=send_sem.at[a * 6 + slot],
                                                recv_sem=recv_sem.at[a * 6 + slot], device_id=to, device_id_type=MESH)

        sends = []
        for a in range(na):
            for j in range(1, N_CHIP):
                cp = copy(a, j - 1, _half(ins[a], c), _half(outs[a].at[k], c), _chip(k ^ j, c))
                cp.start()
                sends.append(cp)
        for j in range(1, N_CHIP):
            for a in range(na):
                landed = _half(outs[a].at[k ^ j], c)
                copy(a, j - 1, landed, landed, sibling).wait_recv()
                cp = copy(a, 2 + j, landed, landed, sibling)
                cp.start()
                sends.append(cp)
        for j in range(1, N_CHIP):
            for a in range(na):
                passed = _half(outs[a].at[k ^ j], 1 - c)
                copy(a, 2 + j, passed, passed, sibling).wait_recv()
        for cp in sends:
            cp.wait_send()
        for cp in started:
            cp.wait()

    return pl.pallas_call(
        body, name="weights_all_gather",
        in_specs=[any_spec] * na, out_specs=[any_spec] * na,
        out_shape=[jax.ShapeDtypeStruct((N_CHIP,) + a.shape, a.dtype) for a in shards],
        scratch_shapes=[pltpu.SemaphoreType.DMA((6 * na,)), pltpu.SemaphoreType.DMA((6 * na,)),
                        pltpu.SemaphoreType.DMA((na,))],
    )(*shards)


def _reduce_scatter(g, core, name):
    n, r, c = g.shape
    hr = r // 2
    once = pl.Buffered(1)
    in_specs = [pl.BlockSpec((n, hr, c), lambda i, core_ref: (0, core_ref[0], 0), pipeline_mode=once),
                pl.BlockSpec((n, hr, c), lambda i, core_ref: (0, 1 - core_ref[0], 0), pipeline_mode=once)]

    def body(core_ref, mine_ref, other_ref, out_ref, from_core, sums, sums_bf, from_chips, done, from_core2, send_sem, recv_sem):
        x, y, cc, k = _place()
        sibling = (x, y, 1 - cc)

        def copy(slot, src, dst, to):
            return pltpu.make_async_remote_copy(src_ref=src, dst_ref=dst, send_sem=send_sem.at[slot],
                                                recv_sem=recv_sem.at[slot], device_id=to, device_id_type=MESH)

        first = copy(0, other_ref, from_core, sibling)
        first.start()
        first.wait()
        total = mine_ref[...] + from_core[...]
        sums[...] = total
        sums_bf[...] = total.astype(BF16)
        sends = [copy(j, sums_bf.at[k ^ j], from_chips.at[j - 1], _chip(k ^ j, cc)) for j in range(1, N_CHIP)]
        for cp in sends:
            cp.start()
        for cp in sends:
            cp.wait()
        red = sums[k]
        for j in range(1, N_CHIP):
            red = red + from_chips[j - 1].astype(F32)
        done[...] = red
        last = copy(N_CHIP, done, from_core2, sibling)
        last.start()
        last.wait()
        row0 = pl.multiple_of(cc * hr, 8)
        row1 = pl.multiple_of((1 - cc) * hr, 8)
        out_ref[pl.ds(row0, hr), :] = red
        out_ref[pl.ds(row1, hr), :] = from_core2[...]

    grid_spec = pltpu.PrefetchScalarGridSpec(
        num_scalar_prefetch=1, grid=(1,), in_specs=in_specs,
        out_specs=pl.BlockSpec((r, c), lambda i, core_ref: (0, 0)),
        scratch_shapes=[pltpu.VMEM((n, hr, c), F32), pltpu.VMEM((n, hr, c), F32), pltpu.VMEM((n, hr, c), BF16),
                        pltpu.VMEM((N_CHIP - 1, hr, c), BF16), pltpu.VMEM((hr, c), F32), pltpu.VMEM((hr, c), F32),
                        pltpu.SemaphoreType.DMA((N_CHIP + 1,)), pltpu.SemaphoreType.DMA((N_CHIP + 1,))])
    return pl.pallas_call(
        body, name=name, grid_spec=grid_spec, out_shape=jax.ShapeDtypeStruct((r, c), F32),
        compiler_params=_params(56),
    )(core, g, g)


def _elementwise(fn, name, ins, n_out, rows):
    total, cols = ins[0].shape
    spec = pl.BlockSpec((rows, cols), lambda i: (i, 0))

    def body(*refs):
        res = fn(*[r[...] for r in refs[:len(ins)]])
        for o, v in zip(refs[len(ins):], res):
            o[...] = v

    return pl.pallas_call(
        body, name=name, grid=(total // rows,),
        in_specs=[spec] * len(ins), out_specs=[spec] * n_out,
        out_shape=[jax.ShapeDtypeStruct((total, cols), F32)] * n_out,
        compiler_params=_params(48),
    )(*ins)


def _adamw(w, g, m, v):
    m = ADAM_B1 * m + (1.0 - ADAM_B1) * g
    v = ADAM_B2 * v + (1.0 - ADAM_B2) * (g * g)
    m_hat = m / (1.0 - ADAM_B1 ** ADAM_STEP)
    v_hat = v / (1.0 - ADAM_B2 ** ADAM_STEP)
    delta = -ADAM_LR * (m_hat / (jnp.sqrt(v_hat) + ADAM_EPS) + ADAM_WD * w)
    return delta, m, v


def _reduce_and_update(grads, weights, moms, vels):
    core = lax.axis_index("c").astype(jnp.int32).reshape(1)
    full = [_reduce_scatter(g, core, f"grads_reduce_scatter_{a}") for a, g in enumerate(grads)]
    out = []
    for a, (g, w, m, v) in enumerate(zip(full, weights, moms, vels)):
        rows = g.shape[0] // 2
        out.append((g,) + tuple(_elementwise(lambda gg, ww, mm, vv: _adamw(ww, gg, mm, vv), f"adamw_{a}", [g, w, m, v], 3, rows)))
    return out


def _reduce_vectors(part, w, m, v):
    n_dev = 8

    def body(p_ref, w_ref, m_ref, v_ref, g_ref, d_ref, nm_ref, nv_ref, buf, send_sem, recv_sem):
        x, y, c, _ = _place()
        me = 4 * x + 2 * y + c
        buf[me] = p_ref[...]
        sends = []
        for off in range(1, n_dev):
            peer = me ^ off
            cp = pltpu.make_async_remote_copy(src_ref=p_ref, dst_ref=buf.at[me], send_sem=send_sem.at[off - 1],
                                              recv_sem=recv_sem.at[off - 1], device_id=(peer >> 2, (peer >> 1) & 1, peer & 1),
                                              device_id_type=MESH)
            cp.start()
            sends.append(cp)
        for off in range(1, n_dev):
            peer = me ^ off
            pltpu.make_async_remote_copy(src_ref=p_ref, dst_ref=buf.at[peer], send_sem=send_sem.at[off - 1],
                                         recv_sem=recv_sem.at[off - 1], device_id=(peer >> 2, (peer >> 1) & 1, peer & 1),
                                         device_id_type=MESH).wait_recv()
        for cp in sends:
            cp.wait_send()
        g = buf[0]
        for d in range(1, n_dev):
            g = g + buf[d]
        g_ref[...] = g
        delta, nm, nv = _adamw(w_ref[...], g, m_ref[...], v_ref[...])
        d_ref[...] = delta
        nm_ref[...] = nm
        nv_ref[...] = nv

    vm = pl.BlockSpec(memory_space=pltpu.VMEM)
    return pl.pallas_call(
        body, name="gains_all_reduce",
        in_specs=[vm] * 4, out_specs=[vm] * 4,
        out_shape=[jax.ShapeDtypeStruct(part.shape, F32)] * 4,
        scratch_shapes=[pltpu.VMEM((n_dev,) + part.shape, F32), pltpu.SemaphoreType.DMA((n_dev - 1,)),
                        pltpu.SemaphoreType.DMA((n_dev - 1,))],
    )(part, w, m, v)


def _pad_row(a):
    a = a.reshape(1, -1)
    return jnp.pad(a, ((0, 0), (0, D_MODEL - a.shape[1])))


def kernel(x, ffn1_norm, ffn1_w_gate, ffn1_w_up, ffn1_w_down, mix_norm, w_in, sb_out_norm, dil_out_norm, w_out, ffn2_norm, ffn2_w_gate, ffn2_w_up, ffn2_w_down, final_norm, loss_target, m_ffn1_norm, m_ffn1_w_gate, m_ffn1_w_up, m_ffn1_w_down, m_mix_norm, m_w_in, m_sb_out_norm, m_dil_out_norm, m_w_out, m_ffn2_norm, m_ffn2_w_gate, m_ffn2_w_up, m_ffn2_w_down, m_final_norm, v_ffn1_norm, v_ffn1_w_gate, v_ffn1_w_up, v_ffn1_w_down, v_mix_norm, v_w_in, v_sb_out_norm, v_dil_out_norm, v_w_out, v_ffn2_norm, v_ffn2_w_gate, v_ffn2_w_up, v_ffn2_w_down, v_final_norm):
    x = x[0]
    target = loss_target[0]
    s = x.shape[0]
    gf = final_norm.reshape(1, D_MODEL)
    cos, sin = _rope_tables(s)

    gu_shard = jnp.stack([ffn1_w_gate[0], ffn1_w_up[0], ffn2_w_gate[0], ffn2_w_up[0]]).astype(BF16)
    wd_shard = jnp.stack([ffn1_w_down[0], ffn2_w_down[0]]).astype(BF16)
    gu, wd, win, wout = _all_gather([gu_shard, wd_shard, w_in[0].astype(BF16), w_out[0].astype(BF16)])
    wout = wout.reshape(D_MODEL, D_MODEL)

    x1, hm = _ffn1_fwd(x, ffn1_norm, mix_norm, gu, wd)
    qkv = _proj_fwd(hm, win, cos, sin)
    o_sb = _sb_fwd(qkv)
    o_dl, lse = _dilated_fwd(qkv)
    x2 = _outproj_fwd(o_sb, o_dl, sb_out_norm, dil_out_norm, x1, wout)
    dx3, st_final = _ffn2_fwd_loss(x2, ffn2_norm, gf, target, gu, wd)

    dx2, dwg2, dwu2, dwd2, st_ffn2 = _ffn_bwd(x2, ffn2_norm, dx3, gu, wd, 1)
    do_sb, do_dl, dwout, st_out = _outproj_bwd(dx2, o_sb, o_dl, sb_out_norm, dil_out_norm, wout)
    dq_sb, dk_sb, dv_sb = _sb_bwd(qkv, o_sb, do_sb)
    dq_dl, dk_dl, dv_dl = _dilated_finish(_dilated_bwd(qkv, o_dl, lse, do_dl), cos, sin)
    dqkv = jnp.concatenate([dq_sb, dk_sb, dv_sb, dq_dl, dk_dl, dv_dl], axis=1)
    dx1, dwin, st_mix = _proj_bwd(x1, mix_norm, dqkv, win, dx2)
    grad_x, dwg1, dwu1, dwd1, st_ffn1 = _ffn_bwd(x, ffn1_norm, dx1, gu, wd, 0)

    names = ["ffn1_w_gate", "ffn1_w_up", "ffn1_w_down", "w_in", "w_out", "ffn2_w_gate", "ffn2_w_up", "ffn2_w_down"]
    grads = [dwg1, dwu1, dwd1, dwin, dwout.reshape(N_CHIP, OUTB, D_MODEL), dwg2, dwu2, dwd2]
    weights = [ffn1_w_gate[0], ffn1_w_up[0], ffn1_w_down[0], w_in[0], w_out[0], ffn2_w_gate[0], ffn2_w_up[0], ffn2_w_down[0]]
    moms = [m_ffn1_w_gate[0], m_ffn1_w_up[0], m_ffn1_w_down[0], m_w_in[0], m_w_out[0], m_ffn2_w_gate[0], m_ffn2_w_up[0], m_ffn2_w_down[0]]
    vels = [v_ffn1_w_gate[0], v_ffn1_w_up[0], v_ffn1_w_down[0], v_w_in[0], v_w_out[0], v_ffn2_w_gate[0], v_ffn2_w_up[0], v_ffn2_w_down[0]]
    mats = {n: tuple(t[None] for t in r) for n, r in zip(names, _reduce_and_update(grads, weights, moms, vels))}

    vec_names = ["ffn1_norm", "mix_norm", "sb_out_norm", "dil_out_norm", "ffn2_norm", "final_norm"]
    part = jnp.concatenate([st_ffn1[0:1], st_mix[0:1], _pad_row(st_out[0]), _pad_row(st_out[1]), st_ffn2[0:1],
                            st_final[0:1], st_final[1:2], jnp.zeros((1, D_MODEL), F32)], axis=0)
    pack = lambda arrs: jnp.concatenate([_pad_row(a) for a in arrs] + [jnp.zeros((2, D_MODEL), F32)], axis=0)
    g_vec, d_vec, m_vec, v_vec = _reduce_vectors(
        part,
        pack([ffn1_norm, mix_norm, sb_out_norm, dil_out_norm, ffn2_norm, final_norm]),
        pack([m_ffn1_norm, m_mix_norm, m_sb_out_norm, m_dil_out_norm, m_ffn2_norm, m_final_norm]),
        pack([v_ffn1_norm, v_mix_norm, v_sb_out_norm, v_dil_out_norm, v_ffn2_norm, v_final_norm]))
    like = {"ffn1_norm": ffn1_norm, "mix_norm": mix_norm, "sb_out_norm": sb_out_norm, "dil_out_norm": dil_out_norm,
            "ffn2_norm": ffn2_norm, "final_norm": final_norm}
    vecs = {n: tuple(t[i, :like[n].size].reshape(like[n].shape) for t in (g_vec, d_vec, m_vec, v_vec))
            for i, n in enumerate(vec_names)}
    loss = 0.5 * jnp.sum(g_vec[6]) / D_MODEL

    order = ["ffn1_norm", "ffn1_w_gate", "ffn1_w_up", "ffn1_w_down", "mix_norm", "w_in", "sb_out_norm", "dil_out_norm",
             "w_out", "ffn2_norm", "ffn2_w_gate", "ffn2_w_up", "ffn2_w_down", "final_norm"]
    both = {**mats, **vecs}
    return (loss, grad_x[None], *[both[n][0] for n in order], *[both[n][1] for n in order],
            *[both[n][2] for n in order], *[both[n][3] for n in order])
```

```python
import functools

import jax
import jax.numpy as jnp
from jax import lax
from jax.experimental import pallas as pl
from jax.experimental.pallas import tpu as pltpu

D_MODEL = 1024
D_FF = 2816
HEAD_DIM = 64
D_SB = 512
D_DIL = 512
D_IN = 3072
N_CHIP = 4
FFB = D_FF // N_CHIP
INB = D_IN // N_CHIP
OUTB = D_MODEL // N_CHIP
BLK = 128
LANES = 128
DILATIONS = (1, 4, 16)
ROPE_THETA = 10000.0
RMS_EPS = 1e-6
SCALE = HEAD_DIM ** -0.5
NEG = -1e30
DEAD = -104.0
ADAM_LR = 0.001
ADAM_B1 = 0.9
ADAM_B2 = 0.999
ADAM_EPS = 1e-08
ADAM_WD = 0.01
ADAM_STEP = 10
MESH = pl.DeviceIdType.MESH
F32 = jnp.float32
BF16 = jnp.bfloat16
TM = 512


def _params(vmem_mb):
    return pltpu.CompilerParams(vmem_limit_bytes=vmem_mb << 20)


def _dot(a, b):
    return jnp.dot(a, b, preferred_element_type=F32)


def _dot_nt(a, b):
    return lax.dot_general(a, b, (((1,), (1,)), ((), ())), preferred_element_type=F32)


def _dot_tn(a, b):
    return lax.dot_general(a, b, (((0,), (0,)), ((), ())), preferred_element_type=F32)


def _rms_fwd(x, g):
    r = lax.rsqrt(jnp.mean(x * x, axis=-1, keepdims=True) + RMS_EPS)
    xh = x * r
    return xh * g, xh, r


def _rms_bwd(dy, xh, r, g):
    dyg = dy * g
    dx = r * (dyg - xh * jnp.mean(dyg * xh, axis=-1, keepdims=True))
    return dx, jnp.sum(dy * xh, axis=0, keepdims=True)


def _split_bf16(a):
    hi = a.astype(BF16)
    return hi, (a - hi.astype(F32)).astype(BF16)


def _dot_split(a, b):
    hi, lo = _split_bf16(a)
    return _dot(hi, b) + _dot(lo, b)


def _ffn_weight_specs(f):
    return [pl.BlockSpec((None, None, D_MODEL, FFB), lambda i, j: (j, 2 * f, 0, 0)),
            pl.BlockSpec((None, None, D_MODEL, FFB), lambda i, j: (j, 2 * f + 1, 0, 0)),
            pl.BlockSpec((None, None, FFB, D_MODEL), lambda i, j: (j, f, 0, 0))]


def _ffn_saved(s):
    hidden = jax.ShapeDtypeStruct((N_CHIP, s, FFB), BF16)
    hid = pl.BlockSpec((None, TM, FFB), lambda i, j: (j, i, 0))
    row = pl.BlockSpec((TM, D_MODEL), lambda i, j: (i, 0))
    return [row, hid, hid, hid], [jax.ShapeDtypeStruct((s, D_MODEL), BF16), hidden, hidden, hidden]


def _ffn_accumulate(h_ref, acc_scr, wg_ref, wu_ref, wd_ref, a_ref, b_ref, act_ref):
    h = h_ref[...]
    a = _dot(h, wg_ref[...])
    b = _dot(h, wu_ref[...])
    act = ((a * jax.nn.sigmoid(a)) * b).astype(BF16)
    a_ref[...] = a.astype(BF16)
    b_ref[...] = b.astype(BF16)
    act_ref[...] = act
    acc_scr[...] += _dot(act, wd_ref[...])


def _ffn1_fwd(x, g1, gmix, gu, wd):
    s = x.shape[0]
    row = pl.BlockSpec((TM, D_MODEL), lambda i, j: (i, 0))
    vec = pl.BlockSpec((1, D_MODEL), lambda i, j: (0, 0))
    saved_specs, saved_shapes = _ffn_saved(s)

    def body(x_ref, g_ref, gm_ref, wg_ref, wu_ref, wd_ref, x1_ref, hm_ref, h_ref, a_ref, b_ref, act_ref, acc_scr):
        j = pl.program_id(1)

        @pl.when(j == 0)
        def _():
            h, _, _ = _rms_fwd(x_ref[...], g_ref[...])
            h_ref[...] = h.astype(BF16)
            acc_scr[...] = jnp.zeros_like(acc_scr)

        _ffn_accumulate(h_ref, acc_scr, wg_ref, wu_ref, wd_ref, a_ref, b_ref, act_ref)

        @pl.when(j == N_CHIP - 1)
        def _():
            x1 = x_ref[...] + 0.5 * acc_scr[...]
            x1_ref[...] = x1
            hm, _, _ = _rms_fwd(x1, gm_ref[...])
            hm_ref[...] = hm.astype(BF16)

    x1, hm, *saved = pl.pallas_call(
        body, name="ffn1_fwd", grid=(s // TM, N_CHIP),
        in_specs=[row, vec, vec] + _ffn_weight_specs(0),
        out_specs=[row, row] + saved_specs,
        out_shape=[jax.ShapeDtypeStruct((s, D_MODEL), F32), jax.ShapeDtypeStruct((s, D_MODEL), BF16)] + saved_shapes,
        scratch_shapes=[pltpu.VMEM((TM, D_MODEL), F32)],
        compiler_params=_params(48),
    )(x, g1, gmix, gu, gu, wd)
    return x1, hm, saved


def _ffn2_fwd_loss(x2, g2, gf, target, gu, wd):
    s = x2.shape[0]
    row = pl.BlockSpec((TM, D_MODEL), lambda i, j: (i, 0))
    vec = pl.BlockSpec((1, D_MODEL), lambda i, j: (0, 0))
    stat = pl.BlockSpec((8, D_MODEL), lambda i, j: (0, 0))
    saved_specs, saved_shapes = _ffn_saved(s)

    def body(x_ref, g_ref, gf_ref, t_ref, wg_ref, wu_ref, wd_ref, dx_ref, st_ref, h_ref, a_ref, b_ref, act_ref, acc_scr):
        i, j = pl.program_id(0), pl.program_id(1)

        @pl.when((i == 0) & (j == 0))
        def _():
            st_ref[...] = jnp.zeros_like(st_ref)

        @pl.when(j == 0)
        def _():
            h, _, _ = _rms_fwd(x_ref[...], g_ref[...])
            h_ref[...] = h.astype(BF16)
            acc_scr[...] = jnp.zeros_like(acc_scr)

        _ffn_accumulate(h_ref, acc_scr, wg_ref, wu_ref, wd_ref, a_ref, b_ref, act_ref)

        @pl.when(j == N_CHIP - 1)
        def _():
            x3 = x_ref[...] + 0.5 * acc_scr[...]
            y, xh, r = _rms_fwd(x3, gf_ref[...])
            err = y - t_ref[...]
            dx, dg = _rms_bwd(err * (1.0 / D_MODEL), xh, r, gf_ref[...])
            dx_ref[...] = dx
            st_ref[0:1, :] += dg
            st_ref[1:2, :] += jnp.sum(err * err, axis=0, keepdims=True)

    dx3, st, *saved = pl.pallas_call(
        body, name="ffn2_fwd_loss", grid=(s // TM, N_CHIP),
        in_specs=[row, vec, vec, row] + _ffn_weight_specs(1),
        out_specs=[row, stat] + saved_specs,
        out_shape=[jax.ShapeDtypeStruct((s, D_MODEL), F32), jax.ShapeDtypeStruct((8, D_MODEL), F32)] + saved_shapes,
        scratch_shapes=[pltpu.VMEM((TM, D_MODEL), F32)],
        compiler_params=_params(48),
    )(x2, g2, gf, target, gu, gu, wd)
    return dx3, st, saved


def _ffn_bwd(xin, g, dy, saved, gu, wd, f):
    s = xin.shape[0]
    hb, gate, up, act = saved
    row = pl.BlockSpec((TM, D_MODEL), lambda i, j: (i, 0))
    vec = pl.BlockSpec((1, D_MODEL), lambda i, j: (0, 0))
    stat = pl.BlockSpec((8, D_MODEL), lambda i, j: (0, 0))
    hid = pl.BlockSpec((None, TM, FFB), lambda i, j: (j, i, 0))

    def body(x_ref, g_ref, dy_ref, a_ref, b_ref, wg_ref, wu_ref, wd_ref, out_ref, dyh_ref, da_ref, db_ref, st_ref, dh_scr):
        i, j = pl.program_id(0), pl.program_id(1)

        @pl.when((i == 0) & (j == 0))
        def _():
            st_ref[...] = jnp.zeros_like(st_ref)

        @pl.when(j == 0)
        def _():
            dyh_ref[...] = (0.5 * dy_ref[...]).astype(BF16)
            dh_scr[...] = jnp.zeros_like(dh_scr)

        a = a_ref[...].astype(F32)
        b = b_ref[...].astype(F32)
        sg = jax.nn.sigmoid(a)
        dact = _dot_nt(dyh_ref[...], wd_ref[...])
        dab = (dact * b * (sg * (1.0 + a * (1.0 - sg)))).astype(BF16)
        dbb = (dact * (a * sg)).astype(BF16)
        da_ref[...] = dab
        db_ref[...] = dbb
        dh_scr[...] += _dot_nt(dab, wg_ref[...]) + _dot_nt(dbb, wu_ref[...])

        @pl.when(j == N_CHIP - 1)
        def _():
            _, xh, r = _rms_fwd(x_ref[...], g_ref[...])
            dx, dg = _rms_bwd(dh_scr[...], xh, r, g_ref[...])
            out_ref[...] = dy_ref[...] + dx
            st_ref[0:1, :] += dg

    hidden = jax.ShapeDtypeStruct((N_CHIP, s, FFB), BF16)
    dx, dyh, da, db, st = pl.pallas_call(
        body, name=f"ffn{f + 1}_bwd_dx", grid=(s // TM, N_CHIP),
        in_specs=[row, vec, row, hid, hid] + _ffn_weight_specs(f),
        out_specs=[row, row, hid, hid, stat],
        out_shape=[jax.ShapeDtypeStruct((s, D_MODEL), F32), jax.ShapeDtypeStruct((s, D_MODEL), BF16),
                   hidden, hidden, jax.ShapeDtypeStruct((8, D_MODEL), F32)],
        scratch_shapes=[pltpu.VMEM((TM, D_MODEL), F32)],
        compiler_params=_params(56),
    )(xin, g, dy, gate, up, gu, gu, wd)

    tok = pl.BlockSpec((TM, D_MODEL), lambda j, i: (i, 0))
    hid2 = pl.BlockSpec((None, TM, FFB), lambda j, i: (j, i, 0))
    gspecs = [pl.BlockSpec((None, D_MODEL, FFB), lambda j, i: (j, 0, 0)),
              pl.BlockSpec((None, D_MODEL, FFB), lambda j, i: (j, 0, 0)),
              pl.BlockSpec((None, FFB, D_MODEL), lambda j, i: (j, 0, 0))]

    def wbody(h_ref, dyh_ref, da_ref, db_ref, act_ref, dwg_ref, dwu_ref, dwd_ref):
        @pl.when(pl.program_id(1) == 0)
        def _():
            dwg_ref[...] = jnp.zeros_like(dwg_ref)
            dwu_ref[...] = jnp.zeros_like(dwu_ref)
            dwd_ref[...] = jnp.zeros_like(dwd_ref)

        hb = h_ref[...]
        dwg_ref[...] += _dot_tn(hb, da_ref[...])
        dwu_ref[...] += _dot_tn(hb, db_ref[...])
        dwd_ref[...] += _dot_tn(act_ref[...], dyh_ref[...])

    dwg, dwu, dwd = pl.pallas_call(
        wbody, name=f"ffn{f + 1}_bwd_dw", grid=(N_CHIP, s // TM),
        in_specs=[tok, tok, hid2, hid2, hid2], out_specs=gspecs,
        out_shape=[jax.ShapeDtypeStruct((N_CHIP, D_MODEL, FFB), F32),
                   jax.ShapeDtypeStruct((N_CHIP, D_MODEL, FFB), F32),
                   jax.ShapeDtypeStruct((N_CHIP, FFB, D_MODEL), F32)],
        compiler_params=_params(48),
    )(hb, dyh, da, db, act)
    return dx, dwg, dwu, dwd, st


def _rope_tables(s):
    half = HEAD_DIM // 2
    inv_freq = ROPE_THETA ** (-jnp.arange(half, dtype=F32) / half)
    ang = jnp.arange(s).astype(F32)[:, None] * inv_freq[None, :]
    cos, sin = jnp.cos(ang), jnp.sin(ang)
    cos2 = jnp.concatenate([cos, cos], axis=-1)
    sin2 = jnp.concatenate([-sin, sin], axis=-1)
    return jnp.tile(cos2, (1, LANES // HEAD_DIM)), jnp.tile(sin2, (1, LANES // HEAD_DIM))


def _rotate(t, cos, sin_signed):
    lane = lax.broadcasted_iota(jnp.int32, t.shape, 1)
    first = (lane % HEAD_DIM) < (HEAD_DIM // 2)
    partner = jnp.where(first, pltpu.roll(t, LANES - HEAD_DIM // 2, 1), pltpu.roll(t, HEAD_DIM // 2, 1))
    return t * cos + partner * sin_signed


def _proj_fwd(hm, win, cos, sin):
    s = hm.shape[0]
    n_sub = INB // LANES
    first_rot, last_rot = (3 * D_SB) // LANES, (3 * D_SB + 2 * D_DIL) // LANES

    def body(h_ref, w_ref, c_ref, s_ref, o_ref):
        j = pl.program_id(1)
        r = _dot(h_ref[...], w_ref[...])
        for c in range(n_sub):
            t = r[:, c * LANES:(c + 1) * LANES]
            col = j * n_sub + c
            rot = (col >= first_rot) & (col < last_rot)
            lanes = slice(c * LANES, (c + 1) * LANES)

            @pl.when(rot)
            def _():
                o_ref[:, lanes] = _rotate(t, c_ref[...], s_ref[...]).astype(BF16)

            @pl.when(jnp.logical_not(rot))
            def _():
                o_ref[:, lanes] = t.astype(BF16)

    return pl.pallas_call(
        body, name="proj_fwd", grid=(s // TM, N_CHIP),
        in_specs=[pl.BlockSpec((TM, D_MODEL), lambda i, j: (i, 0)),
                  pl.BlockSpec((None, D_MODEL, INB), lambda i, j: (j, 0, 0)),
                  pl.BlockSpec((TM, LANES), lambda i, j: (i, 0)),
                  pl.BlockSpec((TM, LANES), lambda i, j: (i, 0))],
        out_specs=pl.BlockSpec((TM, INB), lambda i, j: (i, j)),
        out_shape=jax.ShapeDtypeStruct((s, D_IN), BF16),
        compiler_params=_params(32),
    )(hm, win, cos, sin)


def _proj_bwd(x1, gmix, dqkv, win, dx2):
    s = x1.shape[0]
    row = pl.BlockSpec((TM, D_MODEL), lambda i, j: (i, 0))
    vec = pl.BlockSpec((1, D_MODEL), lambda i, j: (0, 0))

    def body(x_ref, g_ref, dq_ref, w_ref, dx2_ref, out_ref, dw_ref, st_ref, h_scr, dh_scr):
        i, j = pl.program_id(0), pl.program_id(1)

        @pl.when((i == 0) & (j == 0))
        def _():
            st_ref[...] = jnp.zeros_like(st_ref)
            dw_ref[...] = jnp.zeros_like(dw_ref)

        @pl.when(j == 0)
        def _():
            h, _, _ = _rms_fwd(x_ref[...], g_ref[...])
            h_scr[...] = h.astype(BF16)
            dh_scr[...] = jnp.zeros_like(dh_scr)

        dq = dq_ref[...]
        dw_ref[j] += _dot_tn(h_scr[...], dq)
        dh_scr[...] += _dot_nt(dq, w_ref[...])

        @pl.when(j == N_CHIP - 1)
        def _():
            _, xh, r = _rms_fwd(x_ref[...], g_ref[...])
            dx, dg = _rms_bwd(dh_scr[...], xh, r, g_ref[...])
            out_ref[...] = dx2_ref[...] + dx
            st_ref[0:1, :] += dg

    return pl.pallas_call(
        body, name="proj_bwd", grid=(s // TM, N_CHIP),
        in_specs=[row, vec, pl.BlockSpec((TM, INB), lambda i, j: (i, j)),
                  pl.BlockSpec((None, D_MODEL, INB), lambda i, j: (j, 0, 0)), row],
        out_specs=[row, pl.BlockSpec((N_CHIP, D_MODEL, INB), lambda i, j: (0, 0, 0)),
                   pl.BlockSpec((8, D_MODEL), lambda i, j: (0, 0))],
        out_shape=[jax.ShapeDtypeStruct((s, D_MODEL), F32),
                   jax.ShapeDtypeStruct((N_CHIP, D_MODEL, INB), F32),
                   jax.ShapeDtypeStruct((8, D_MODEL), F32)],
        scratch_shapes=[pltpu.VMEM((TM, D_MODEL), BF16), pltpu.VMEM((TM, D_MODEL), F32)],
        compiler_params=_params(56),
    )(x1, gmix, dqkv, win, dx2)


def _outproj_fwd(o_sb, o_dl, g_sb, g_dl, x1, wout):
    s = x1.shape[0]
    half = pl.BlockSpec((TM, D_SB), lambda i: (i, 0))
    row = pl.BlockSpec((TM, D_MODEL), lambda i: (i, 0))
    vec = pl.BlockSpec((1, D_SB), lambda i: (0, 0))

    def body(a_ref, b_ref, ga_ref, gb_ref, x_ref, w_ref, o_ref):
        ma, _, _ = _rms_fwd(a_ref[...], ga_ref[...])
        mb, _, _ = _rms_fwd(b_ref[...], gb_ref[...])
        o_ref[...] = (x_ref[...] + _dot(ma.astype(BF16), w_ref[0:D_SB, :])
                      + _dot(mb.astype(BF16), w_ref[D_SB:D_MODEL, :]))

    return pl.pallas_call(
        body, name="outproj_fwd", grid=(s // TM,),
        in_specs=[half, half, vec, vec, row, pl.BlockSpec((D_MODEL, D_MODEL), lambda i: (0, 0))],
        out_specs=row, out_shape=jax.ShapeDtypeStruct((s, D_MODEL), F32),
        compiler_params=_params(32),
    )(o_sb, o_dl, g_sb, g_dl, x1, wout)


def _outproj_bwd(dx2, o_sb, o_dl, g_sb, g_dl, wout):
    s = dx2.shape[0]
    half = pl.BlockSpec((TM, D_SB), lambda i: (i, 0))
    row = pl.BlockSpec((TM, D_MODEL), lambda i: (i, 0))
    vec = pl.BlockSpec((1, D_SB), lambda i: (0, 0))
    full = pl.BlockSpec((D_MODEL, D_MODEL), lambda i: (0, 0))

    def body(dy_ref, a_ref, b_ref, ga_ref, gb_ref, w_ref, da_ref, db_ref, dw_ref, st_ref):
        @pl.when(pl.program_id(0) == 0)
        def _():
            dw_ref[...] = jnp.zeros_like(dw_ref)
            st_ref[...] = jnp.zeros_like(st_ref)

        dy = dy_ref[...].astype(BF16)
        dm = _dot_nt(dy, w_ref[...])
        ma, xa, ra = _rms_fwd(a_ref[...], ga_ref[...])
        mb, xb, rb = _rms_fwd(b_ref[...], gb_ref[...])
        dw_ref[0:D_SB, :] += _dot_tn(ma.astype(BF16), dy)
        dw_ref[D_SB:D_MODEL, :] += _dot_tn(mb.astype(BF16), dy)
        da, dga = _rms_bwd(dm[:, 0:D_SB], xa, ra, ga_ref[...])
        db, dgb = _rms_bwd(dm[:, D_SB:D_MODEL], xb, rb, gb_ref[...])
        da_ref[...] = da
        db_ref[...] = db
        st_ref[0:1, :] += dga
        st_ref[1:2, :] += dgb

    return pl.pallas_call(
        body, name="outproj_bwd", grid=(s // TM,),
        in_specs=[row, half, half, vec, vec, full],
        out_specs=[half, half, full, pl.BlockSpec((8, D_SB), lambda i: (0, 0))],
        out_shape=[jax.ShapeDtypeStruct((s, D_SB), F32), jax.ShapeDtypeStruct((s, D_SB), F32),
                   jax.ShapeDtypeStruct((D_MODEL, D_MODEL), F32), jax.ShapeDtypeStruct((8, D_SB), F32)],
        compiler_params=_params(48),
    )(dx2, o_sb, o_dl, g_sb, g_dl, wout)


def _head_masks():
    lane = lax.broadcasted_iota(jnp.int32, (BLK, LANES), 1)
    return [lane < HEAD_DIM, lane >= HEAD_DIM]


def _keep(mask, a):
    return a * jnp.where(mask, 1.0, 0.0).astype(a.dtype)


def _suffix_matrices():
    r = lax.broadcasted_iota(jnp.int32, (BLK, BLK), 0)
    c = lax.broadcasted_iota(jnp.int32, (BLK, BLK), 1)
    ones = jnp.ones((BLK, BLK), BF16)
    excl = jnp.concatenate([(r > c).astype(BF16), ones], axis=1)
    incl = jnp.concatenate([(r >= c).astype(BF16), ones], axis=1)
    return excl, incl


def _blk(i):
    return pl.ds(pl.multiple_of(i * BLK, BLK), BLK)


def _alive(carry_m):
    return (jnp.max(carry_m) > DEAD).astype(jnp.int32)


def _more_keys(i, carry):
    return (carry[0] <= i) & (carry[1] > 0)


def _stack_heads(a):
    masks = _head_masks()
    return jnp.concatenate([_keep(masks[0], a), _keep(masks[1], a)], axis=0)


def _unstack_heads(a2):
    return jnp.where(_head_masks()[0], a2[:BLK], a2[BLK:])


def _head_rowsum(a):
    masks = _head_masks()
    return jnp.concatenate([jnp.sum(jnp.where(m, a, 0.0), axis=1, keepdims=True) for m in masks], axis=0)


SB_QB = 2
SB_ROWS = SB_QB * 2 * BLK


def _sb_rows(ref, i0, cast=None):
    tiles = [ref[_blk(i0 + t), :] for t in range(SB_QB)]
    return jnp.concatenate([_stack_heads(t if cast is None else t.astype(cast)) for t in tiles], axis=0)


def _sb_scores(q2, k, i, j, carry_m, u_excl):
    r = lax.broadcasted_iota(jnp.int32, (SB_ROWS, BLK), 0)
    row = (r & (BLK - 1)) + ((r >> 8) << 7)
    col = lax.broadcasted_iota(jnp.int32, (SB_ROWS, BLK), 1)
    valid = (j * BLK + col) < (i * BLK + row)
    z = _dot_nt(q2, k) * SCALE
    sp = jnp.maximum(z, 0.0) + jnp.log(1.0 + jnp.exp(-jnp.abs(z)))
    log_stay = jnp.where(valid, -sp, 0.0)
    log_beta = z - sp
    sums = _dot_split(log_stay, u_excl)
    later = carry_m + sums[:, :BLK]
    w = jnp.where(valid, jnp.exp(log_beta + later), 0.0)
    return valid, log_beta, w, carry_m + sums[:, BLK:]


def _sb_fwd(qkv):
    s = qkv.shape[0]
    nq = s // BLK
    pairs = D_SB // LANES
    col = lambda off: pl.BlockSpec((s, LANES), lambda p: (0, off + p))

    def body(q_ref, k_ref, v_ref, o_ref):
        u_excl, _ = _suffix_matrices()
        zero = jnp.zeros((SB_ROWS, LANES), F32)

        def q_block(ib, _):
            i = ib * SB_QB
            last = i + SB_QB - 1
            q2 = _sb_rows(q_ref, i)

            def k_block(carry):
                jj, _, carry_m, acc = carry
                j = last - jj
                _, _, w, carry_m = _sb_scores(q2, k_ref[_blk(j), :], i, j, carry_m, u_excl)
                return jj + 1, _alive(carry_m), carry_m, acc + _dot(w.astype(BF16), v_ref[_blk(j), :])

            _, _, _, acc = lax.while_loop(functools.partial(_more_keys, last), k_block,
                                          (jnp.int32(0), jnp.int32(1), zero, zero))
            for t in range(SB_QB):
                o_ref[_blk(i + t), :] = _unstack_heads(acc[2 * BLK * t:2 * BLK * (t + 1)])
            return 0

        lax.fori_loop(0, nq // SB_QB, q_block, 0)

    return pl.pallas_call(
        body, name="sb_fwd", grid=(pairs,),
        in_specs=[col(0), col(pairs), col(2 * pairs)],
        out_specs=pl.BlockSpec((s, LANES), lambda p: (0, p)),
        out_shape=jax.ShapeDtypeStruct((s, D_SB), F32),
        compiler_params=_params(48),
    )(qkv, qkv, qkv)


def _sb_bwd(qkv, o_sb, do_sb):
    s = qkv.shape[0]
    nq = s // BLK
    pairs = D_SB // LANES
    col = lambda off: pl.BlockSpec((s, LANES), lambda p: (0, off + p))
    own = pl.BlockSpec((s, LANES), lambda p: (0, p))

    def body(q_ref, k_ref, v_ref, o_ref, do_ref, dq_ref, dk_ref, dv_ref, dk_acc, dv_acc):
        u_excl, u_incl = _suffix_matrices()
        zero = jnp.zeros((SB_ROWS, LANES), F32)
        dk_acc[...] = jnp.zeros_like(dk_acc)
        dv_acc[...] = jnp.zeros_like(dv_acc)

        def q_block(ib, _):
            i = ib * SB_QB
            last = i + SB_QB - 1
            q2 = _sb_rows(q_ref, i)
            do2 = _sb_rows(do_ref, i, BF16)
            totals = [_head_rowsum(do_ref[_blk(i + t), :].astype(BF16).astype(F32) * o_ref[_blk(i + t), :])
                      for t in range(SB_QB)]
            total = jnp.broadcast_to(jnp.concatenate(totals, axis=0), (SB_ROWS, BLK))

            def k_block(carry):
                jj, _, carry_m, carry_g, dq = carry
                j = last - jj
                k = k_ref[_blk(j), :]
                valid, log_beta, w, carry_m = _sb_scores(q2, k, i, j, carry_m, u_excl)
                wb = w.astype(BF16)
                g = wb.astype(F32) * _dot_nt(do2, v_ref[_blk(j), :])
                sums = _dot_split(g, u_incl)
                before = total - (carry_g + sums[:, :BLK])
                dz = jnp.where(valid, g - jnp.exp(log_beta) * (g + before), 0.0)
                dzb = (dz * SCALE).astype(BF16)
                dk_acc[_blk(j), :] += _dot_tn(dzb, q2)
                dv_acc[_blk(j), :] += _dot_tn(wb, do2)
                return jj + 1, _alive(carry_m), carry_m, carry_g + sums[:, BLK:], dq + _dot(dzb, k)

            _, _, _, _, dq = lax.while_loop(functools.partial(_more_keys, last), k_block,
                                            (jnp.int32(0), jnp.int32(1), zero, zero, zero))
            for t in range(SB_QB):
                dq_ref[_blk(i + t), :] = _unstack_heads(dq[2 * BLK * t:2 * BLK * (t + 1)]).astype(BF16)
            return 0

        lax.fori_loop(0, nq // SB_QB, q_block, 0)
        dk_ref[...] = dk_acc[...].astype(BF16)
        dv_ref[...] = dv_acc[...].astype(BF16)

    return pl.pallas_call(
        body, name="sb_bwd", grid=(pairs,),
        in_specs=[col(0), col(pairs), col(2 * pairs), own, own],
        out_specs=[own, own, own],
        out_shape=[jax.ShapeDtypeStruct((s, D_SB), BF16)] * 3,
        scratch_shapes=[pltpu.VMEM((s, LANES), F32), pltpu.VMEM((s, LANES), F32)],
        compiler_params=_params(56),
    )(qkv, qkv, qkv, o_sb, do_sb)


def _band_masks(b):
    row = lax.broadcasted_iota(jnp.int32, (2 * BLK, BLK), 0) & (BLK - 1)
    col = lax.broadcasted_iota(jnp.int32, (2 * BLK, BLK), 1)
    return col <= row, (col - row) >= jnp.where(b > 0, 0, BLK)


def _dil_tiles(qf, kf, vf, d, t, nb):
    c, b = t // nb, t % nb
    start = c + d * BLK * b
    rows = pl.ds(start, BLK, stride=d)
    prev = pl.ds(jnp.where(b > 0, start - d * BLK, start), BLK, stride=d)
    bf = lambda ref, sl: ref[sl, :].astype(BF16)
    return b, rows, prev, _stack_heads(bf(qf, rows)), bf(kf, rows), bf(kf, prev), bf(vf, rows), bf(vf, prev)


def _lanes_of_heads(col2):
    return _unstack_heads(jnp.broadcast_to(col2, (2 * BLK, LANES)))


def _dilated_fwd(qkv):
    s = qkv.shape[0]
    pairs = D_DIL // LANES
    base = (3 * D_SB) // LANES
    col = lambda off: pl.BlockSpec((s, LANES), lambda p: (0, off + p))
    own = pl.BlockSpec((s, LANES), lambda p: (0, p))

    def body(q_ref, k_ref, v_ref, acc_ref, m_ref, qf, kf, vf, l_scr):
        qf[...] = q_ref[...].astype(F32)
        kf[...] = k_ref[...].astype(F32)
        vf[...] = v_ref[...].astype(F32)
        for d in DILATIONS:
            nb = s // (d * BLK)

            def block(t, _):
                b, rows, prev, q2, kc, kp, vc, vp = _dil_tiles(qf, kf, vf, d, t, nb)
                in_cur, in_prev = _band_masks(b)
                zc = jnp.where(in_cur, _dot_nt(q2, kc) * SCALE, NEG)
                zp = jnp.where(in_prev, _dot_nt(q2, kp) * SCALE, NEG)
                m = jnp.maximum(jnp.max(zc, axis=1, keepdims=True), jnp.max(zp, axis=1, keepdims=True))
                pc, pp = jnp.exp(zc - m), jnp.exp(zp - m)
                den = jnp.sum(pc, axis=1, keepdims=True) + jnp.sum(pp, axis=1, keepdims=True)
                acc = _unstack_heads(_dot(pc.astype(BF16), vc) + _dot(pp.astype(BF16), vp))
                m_t, l_t = _lanes_of_heads(m), _lanes_of_heads(den)
                if d == DILATIONS[0]:
                    m_ref[rows, :] = m_t
                    l_scr[rows, :] = l_t
                    acc_ref[rows, :] = acc
                else:
                    m_old = m_ref[rows, :]
                    m_new = jnp.maximum(m_old, m_t)
                    keep, add = jnp.exp(m_old - m_new), jnp.exp(m_t - m_new)
                    m_ref[rows, :] = m_new
                    l_scr[rows, :] = l_scr[rows, :] * keep + l_t * add
                    acc_ref[rows, :] = acc_ref[rows, :] * keep + acc * add
                return 0

            lax.fori_loop(0, s // BLK, block, 0, unroll=2)

        def finish(i, _):
            l = l_scr[_blk(i), :]
            acc_ref[_blk(i), :] = acc_ref[_blk(i), :] / l
            m_ref[_blk(i), :] = m_ref[_blk(i), :] + jnp.log(l)
            return 0

        lax.fori_loop(0, s // BLK, finish, 0)

    return pl.pallas_call(
        body, name="dilated_fwd", grid=(pairs,),
        in_specs=[col(base), col(base + pairs), col(base + 2 * pairs)],
        out_specs=[own, own],
        out_shape=[jax.ShapeDtypeStruct((s, D_DIL), F32)] * 2,
        scratch_shapes=[pltpu.VMEM((s, LANES), F32)] * 4,
        compiler_params=_params(56),
    )(qkv, qkv, qkv)


def _dilated_bwd(qkv, out, lse, dout):
    s = qkv.shape[0]
    pairs = D_DIL // LANES
    base = (3 * D_SB) // LANES
    once = pl.Buffered(1)
    col = lambda off: pl.BlockSpec((s, LANES), lambda p: (0, off + p), pipeline_mode=once)
    own = pl.BlockSpec((s, LANES), lambda p: (0, p), pipeline_mode=once)
    res = pl.BlockSpec((s, LANES), lambda p: (0, p))

    def body(q_ref, k_ref, v_ref, o_ref, l_ref, do_ref, dq_ref, dk_ref, dv_ref, qf, kf, vf):
        masks = _head_masks()
        qf[...] = q_ref[...].astype(F32)
        kf[...] = k_ref[...].astype(F32)
        vf[...] = v_ref[...].astype(F32)
        dq_ref[...] = jnp.zeros_like(dq_ref)
        dk_ref[...] = jnp.zeros_like(dk_ref)
        dv_ref[...] = jnp.zeros_like(dv_ref)
        for d in DILATIONS:
            nb = s // (d * BLK)

            def block(t, _):
                b, rows, prev, q2, kc, kp, vc, vp = _dil_tiles(qf, kf, vf, d, t, nb)
                in_cur, in_prev = _band_masks(b)
                do32 = do_ref[rows, :]
                do2 = _stack_heads(do32.astype(BF16))
                delta = _head_rowsum(do32 * o_ref[rows, :])
                lse_t = l_ref[rows, :]
                lse2 = jnp.concatenate([jnp.max(jnp.where(m, lse_t, NEG), axis=1, keepdims=True) for m in masks], axis=0)
                wc = jnp.exp(jnp.where(in_cur, _dot_nt(q2, kc) * SCALE, NEG) - lse2)
                wp = jnp.exp(jnp.where(in_prev, _dot_nt(q2, kp) * SCALE, NEG) - lse2)
                dzc = (wc * (_dot_nt(do2, vc) - delta) * SCALE).astype(BF16)
                dzp = (wp * (_dot_nt(do2, vp) - delta) * SCALE).astype(BF16)
                dq_ref[rows, :] += _unstack_heads(_dot(dzc, kc) + _dot(dzp, kp))
                dk_ref[rows, :] += _dot_tn(dzc, q2)
                dk_ref[prev, :] += _dot_tn(dzp, q2)
                dv_ref[rows, :] += _dot_tn(wc.astype(BF16), do2)
                dv_ref[prev, :] += _dot_tn(wp.astype(BF16), do2)
                return 0

            lax.fori_loop(0, s // BLK, block, 0, unroll=2)

    return pl.pallas_call(
        body, name="dilated_bwd", grid=(pairs,),
        in_specs=[col(base), col(base + pairs), col(base + 2 * pairs), own, own, own],
        out_specs=[res, res, res],
        out_shape=[jax.ShapeDtypeStruct((s, D_DIL), F32)] * 3,
        scratch_shapes=[pltpu.VMEM((s, LANES), F32)] * 3,
        compiler_params=_params(60),
    )(qkv, qkv, qkv, out, lse, dout)


def _dilated_finish(grads, cos, sin):
    s = grads[0].shape[0]
    spec = pl.BlockSpec((TM, D_DIL), lambda i: (i, 0))
    tab = pl.BlockSpec((TM, LANES), lambda i: (i, 0))

    def body(dq_ref, dk_ref, dv_ref, c_ref, s_ref, oq_ref, ok_ref, ov_ref):
        for src, dst, rotated in ((dq_ref, oq_ref, True), (dk_ref, ok_ref, True), (dv_ref, ov_ref, False)):
            for c in range(D_DIL // LANES):
                lanes = slice(c * LANES, (c + 1) * LANES)
                piece = src[:, lanes]
                dst[:, lanes] = (_rotate(piece, c_ref[...], -s_ref[...]) if rotated else piece).astype(BF16)

    return pl.pallas_call(
        body, name="dilated_finish", grid=(s // TM,),
        in_specs=[spec] * 3 + [tab, tab], out_specs=[spec] * 3,
        out_shape=[jax.ShapeDtypeStruct((s, D_DIL), BF16)] * 3,
        compiler_params=_params(32),
    )(*grads, cos, sin)


def _place():
    x, y, c = lax.axis_index("x"), lax.axis_index("y"), lax.axis_index("c")
    return x, y, c, 2 * x + y


def _chip(k, c):
    return (k >> 1, k & 1, c)


def _half(ref, h):
    n = ref.shape[0] // 2
    return ref.at[pl.ds(h * n, n)]


def _all_gather(shards):
    na = len(shards)
    any_spec = pl.BlockSpec(memory_space=pl.ANY)

    def body(*refs):
        ins, outs = refs[:na], refs[na:2 * na]
        send_sem, recv_sem, local_sem = refs[2 * na:]
        x, y, c, k = _place()
        sibling = (x, y, 1 - c)
        started = []
        for a in range(na):
            cp = pltpu.make_async_copy(ins[a], outs[a].at[k], local_sem.at[a])
            cp.start()
            started.append(cp)

        def copy(a, slot, src, dst, to):
            return pltpu.make_async_remote_copy(src_ref=src, dst_ref=dst, send_sem=send_sem.at[a * 6 + slot],
                                                recv_sem=recv_sem.at[a * 6 + slot], device_id=to, device_id_type=MESH)

        sends = []
        for a in range(na):
            for j in range(1, N_CHIP):
                cp = copy(a, j - 1, _half(ins[a], c), _half(outs[a].at[k], c), _chip(k ^ j, c))
                cp.start()
                sends.append(cp)
        for j in range(1, N_CHIP):
            for a in range(na):
                landed = _half(outs[a].at[k ^ j], c)
                copy(a, j - 1, landed, landed, sibling).wait_recv()
                cp = copy(a, 2 + j, landed, landed, sibling)
                cp.start()
                sends.append(cp)
        for j in range(1, N_CHIP):
            for a in range(na):
                passed = _half(outs[a].at[k ^ j], 1 - c)
                copy(a, 2 + j, passed, passed, sibling).wait_recv()
        for cp in sends:
            cp.wait_send()
        for cp in started:
            cp.wait()

    return pl.pallas_call(
        body, name="weights_all_gather",
        in_specs=[any_spec] * na, out_specs=[any_spec] * na,
        out_shape=[jax.ShapeDtypeStruct((N_CHIP,) + a.shape, a.dtype) for a in shards],
        scratch_shapes=[pltpu.SemaphoreType.DMA((6 * na,)), pltpu.SemaphoreType.DMA((6 * na,)),
                        pltpu.SemaphoreType.DMA((na,))],
    )(*shards)


def _reduce_scatter(g, core, name):
    n, r, c = g.shape
    hr = r // 2
    once = pl.Buffered(1)
    in_specs = [pl.BlockSpec((n, hr, c), lambda i, core_ref: (0, core_ref[0], 0), pipeline_mode=once),
                pl.BlockSpec((n, hr, c), lambda i, core_ref: (0, 1 - core_ref[0], 0), pipeline_mode=once)]

    def body(core_ref, mine_ref, other_ref, out_ref, from_core, sums, sums_bf, from_chips, done, from_core2, send_sem, recv_sem):
        x, y, cc, k = _place()
        sibling = (x, y, 1 - cc)

        def copy(slot, src, dst, to):
            return pltpu.make_async_remote_copy(src_ref=src, dst_ref=dst, send_sem=send_sem.at[slot],
                                                recv_sem=recv_sem.at[slot], device_id=to, device_id_type=MESH)

        first = copy(0, other_ref, from_core, sibling)
        first.start()
        first.wait()
        total = mine_ref[...] + from_core[...]
        sums[...] = total
        sums_bf[...] = total.astype(BF16)
        sends = [copy(j, sums_bf.at[k ^ j], from_chips.at[j - 1], _chip(k ^ j, cc)) for j in range(1, N_CHIP)]
        for cp in sends:
            cp.start()
        for cp in sends:
            cp.wait()
        red = sums[k]
        for j in range(1, N_CHIP):
            red = red + from_chips[j - 1].astype(F32)
        done[...] = red
        last = copy(N_CHIP, done, from_core2, sibling)
        last.start()
        last.wait()
        row0 = pl.multiple_of(cc * hr, 8)
        row1 = pl.multiple_of((1 - cc) * hr, 8)
        out_ref[pl.ds(row0, hr), :] = red
        out_ref[pl.ds(row1, hr), :] = from_core2[...]

    grid_spec = pltpu.PrefetchScalarGridSpec(
        num_scalar_prefetch=1, grid=(1,), in_specs=in_specs,
        out_specs=pl.BlockSpec((r, c), lambda i, core_ref: (0, 0)),
        scratch_shapes=[pltpu.VMEM((n, hr, c), F32), pltpu.VMEM((n, hr, c), F32), pltpu.VMEM((n, hr, c), BF16),
                        pltpu.VMEM((N_CHIP - 1, hr, c), BF16), pltpu.VMEM((hr, c), F32), pltpu.VMEM((hr, c), F32),
                        pltpu.SemaphoreType.DMA((N_CHIP + 1,)), pltpu.SemaphoreType.DMA((N_CHIP + 1,))])
    return pl.pallas_call(
        body, name=name, grid_spec=grid_spec, out_shape=jax.ShapeDtypeStruct((r, c), F32),
        compiler_params=_params(56),
    )(core, g, g)


def _elementwise(fn, name, ins, n_out, rows):
    total, cols = ins[0].shape
    spec = pl.BlockSpec((rows, cols), lambda i: (i, 0))

    def body(*refs):
        res = fn(*[r[...] for r in refs[:len(ins)]])
        for o, v in zip(refs[len(ins):], res):
            o[...] = v

    return pl.pallas_call(
        body, name=name, grid=(total // rows,),
        in_specs=[spec] * len(ins), out_specs=[spec] * n_out,
        out_shape=[jax.ShapeDtypeStruct((total, cols), F32)] * n_out,
        compiler_params=_params(48),
    )(*ins)


def _adamw(w, g, m, v):
    m = ADAM_B1 * m + (1.0 - ADAM_B1) * g
    v = ADAM_B2 * v + (1.0 - ADAM_B2) * (g * g)
    m_hat = m / (1.0 - ADAM_B1 ** ADAM_STEP)
    v_hat = v / (1.0 - ADAM_B2 ** ADAM_STEP)
    delta = -ADAM_LR * (m_hat / (jnp.sqrt(v_hat) + ADAM_EPS) + ADAM_WD * w)
    return delta, m, v


def _reduce_and_update(grads, weights, moms, vels):
    core = lax.axis_index("c").astype(jnp.int32).reshape(1)
    full = [_reduce_scatter(g, core, f"grads_reduce_scatter_{a}") for a, g in enumerate(grads)]
    out = []
    for a, (g, w, m, v) in enumerate(zip(full, weights, moms, vels)):
        rows = g.shape[0] // 2
        out.append((g,) + tuple(_elementwise(lambda gg, ww, mm, vv: _adamw(ww, gg, mm, vv), f"adamw_{a}", [g, w, m, v], 3, rows)))
    return out


def _reduce_vectors(part, w, m, v):
    n_dev = 8

    def body(p_ref, w_ref, m_ref, v_ref, g_ref, d_ref, nm_ref, nv_ref, buf, send_sem, recv_sem):
        x, y, c, _ = _place()
        me = 4 * x + 2 * y + c
        buf[me] = p_ref[...]
        sends = []
        for off in range(1, n_dev):
            peer = me ^ off
            cp = pltpu.make_async_remote_copy(src_ref=p_ref, dst_ref=buf.at[me], send_sem=send_sem.at[off - 1],
                                              recv_sem=recv_sem.at[off - 1], device_id=(peer >> 2, (peer >> 1) & 1, peer & 1),
                                              device_id_type=MESH)
            cp.start()
            sends.append(cp)
        for off in range(1, n_dev):
            peer = me ^ off
            pltpu.make_async_remote_copy(src_ref=p_ref, dst_ref=buf.at[peer], send_sem=send_sem.at[off - 1],
                                         recv_sem=recv_sem.at[off - 1], device_id=(peer >> 2, (peer >> 1) & 1, peer & 1),
                                         device_id_type=MESH).wait_recv()
        for cp in sends:
            cp.wait_send()
        g = buf[0]
        for d in range(1, n_dev):
            g = g + buf[d]
        g_ref[...] = g
        delta, nm, nv = _adamw(w_ref[...], g, m_ref[...], v_ref[...])
        d_ref[...] = delta
        nm_ref[...] = nm
        nv_ref[...] = nv

    vm = pl.BlockSpec(memory_space=pltpu.VMEM)
    return pl.pallas_call(
        body, name="gains_all_reduce",
        in_specs=[vm] * 4, out_specs=[vm] * 4,
        out_shape=[jax.ShapeDtypeStruct(part.shape, F32)] * 4,
        scratch_shapes=[pltpu.VMEM((n_dev,) + part.shape, F32), pltpu.SemaphoreType.DMA((n_dev - 1,)),
                        pltpu.SemaphoreType.DMA((n_dev - 1,))],
    )(part, w, m, v)


def _pad_row(a):
    a = a.reshape(1, -1)
    return jnp.pad(a, ((0, 0), (0, D_MODEL - a.shape[1])))


def kernel(x, ffn1_norm, ffn1_w_gate, ffn1_w_up, ffn1_w_down, mix_norm, w_in, sb_out_norm, dil_out_norm, w_out, ffn2_norm, ffn2_w_gate, ffn2_w_up, ffn2_w_down, final_norm, loss_target, m_ffn1_norm, m_ffn1_w_gate, m_ffn1_w_up, m_ffn1_w_down, m_mix_norm, m_w_in, m_sb_out_norm, m_dil_out_norm, m_w_out, m_ffn2_norm, m_ffn2_w_gate, m_ffn2_w_up, m_ffn2_w_down, m_final_norm, v_ffn1_norm, v_ffn1_w_gate, v_ffn1_w_up, v_ffn1_w_down, v_mix_norm, v_w_in, v_sb_out_norm, v_dil_out_norm, v_w_out, v_ffn2_norm, v_ffn2_w_gate, v_ffn2_w_up, v_ffn2_w_down, v_final_norm):
    x = x[0]
    target = loss_target[0]
    s = x.shape[0]
    gf = final_norm.reshape(1, D_MODEL)
    cos, sin = _rope_tables(s)

    gu_shard = jnp.stack([ffn1_w_gate[0], ffn1_w_up[0], ffn2_w_gate[0], ffn2_w_up[0]]).astype(BF16)
    wd_shard = jnp.stack([ffn1_w_down[0], ffn2_w_down[0]]).astype(BF16)
    gu, wd, win, wout = _all_gather([gu_shard, wd_shard, w_in[0].astype(BF16), w_out[0].astype(BF16)])
    wout = wout.reshape(D_MODEL, D_MODEL)

    x1, hm, saved1 = _ffn1_fwd(x, ffn1_norm, mix_norm, gu, wd)
    qkv = _proj_fwd(hm, win, cos, sin)
    o_sb = _sb_fwd(qkv)
    o_dl, lse = _dilated_fwd(qkv)
    x2 = _outproj_fwd(o_sb, o_dl, sb_out_norm, dil_out_norm, x1, wout)
    dx3, st_final, saved2 = _ffn2_fwd_loss(x2, ffn2_norm, gf, target, gu, wd)

    dx2, dwg2, dwu2, dwd2, st_ffn2 = _ffn_bwd(x2, ffn2_norm, dx3, saved2, gu, wd, 1)
    do_sb, do_dl, dwout, st_out = _outproj_bwd(dx2, o_sb, o_dl, sb_out_norm, dil_out_norm, wout)
    dq_sb, dk_sb, dv_sb = _sb_bwd(qkv, o_sb, do_sb)
    dq_dl, dk_dl, dv_dl = _dilated_finish(_dilated_bwd(qkv, o_dl, lse, do_dl), cos, sin)
    dqkv = jnp.concatenate([dq_sb, dk_sb, dv_sb, dq_dl, dk_dl, dv_dl], axis=1)
    dx1, dwin, st_mix = _proj_bwd(x1, mix_norm, dqkv, win, dx2)
    grad_x, dwg1, dwu1, dwd1, st_ffn1 = _ffn_bwd(x, ffn1_norm, dx1, saved1, gu, wd, 0)

    names = ["ffn1_w_gate", "ffn1_w_up", "ffn1_w_down", "w_in", "w_out", "ffn2_w_gate", "ffn2_w_up", "ffn2_w_down"]
    grads = [dwg1, dwu1, dwd1, dwin, dwout.reshape(N_CHIP, OUTB, D_MODEL), dwg2, dwu2, dwd2]
    weights = [ffn1_w_gate[0], ffn1_w_up[0], ffn1_w_down[0], w_in[0], w_out[0], ffn2_w_gate[0], ffn2_w_up[0], ffn2_w_down[0]]
    moms = [m_ffn1_w_gate[0], m_ffn1_w_up[0], m_ffn1_w_down[0], m_w_in[0], m_w_out[0], m_ffn2_w_gate[0], m_ffn2_w_up[0], m_ffn2_w_down[0]]
    vels = [v_ffn1_w_gate[0], v_ffn1_w_up[0], v_ffn1_w_down[0], v_w_in[0], v_w_out[0], v_ffn2_w_gate[0], v_ffn2_w_up[0], v_ffn2_w_down[0]]
    mats = {n: tuple(t[None] for t in r) for n, r in zip(names, _reduce_and_update(grads, weights, moms, vels))}

    vec_names = ["ffn1_norm", "mix_norm", "sb_out_norm", "dil_out_norm", "ffn2_norm", "final_norm"]
    part = jnp.concatenate([st_ffn1[0:1], st_mix[0:1], _pad_row(st_out[0]), _pad_row(st_out[1]), st_ffn2[0:1],
                            st_final[0:1], st_final[1:2], jnp.zeros((1, D_MODEL), F32)], axis=0)
    pack = lambda arrs: jnp.concatenate([_pad_row(a) for a in arrs] + [jnp.zeros((2, D_MODEL), F32)], axis=0)
    g_vec, d_vec, m_vec, v_vec = _reduce_vectors(
        part,
        pack([ffn1_norm, mix_norm, sb_out_norm, dil_out_norm, ffn2_norm, final_norm]),
        pack([m_ffn1_norm, m_mix_norm, m_sb_out_norm, m_dil_out_norm, m_ffn2_norm, m_final_norm]),
        pack([v_ffn1_norm, v_mix_norm, v_sb_out_norm, v_dil_out_norm, v_ffn2_norm, v_final_norm]))
    like = {"ffn1_norm": ffn1_norm, "mix_norm": mix_norm, "sb_out_norm": sb_out_norm, "dil_out_norm": dil_out_norm,
            "ffn2_norm": ffn2_norm, "final_norm": final_norm}
    vecs = {n: tuple(t[i, :like[n].size].reshape(like[n].shape) for t in (g_vec, d_vec, m_vec, v_vec))
            for i, n in enumerate(vec_names)}
    loss = 0.5 * jnp.sum(g_vec[6]) / D_MODEL

    order = ["ffn1_norm", "ffn1_w_gate", "ffn1_w_up", "ffn1_w_down", "mix_norm", "w_in", "sb_out_norm", "dil_out_norm",
             "w_out", "ffn2_norm", "ffn2_w_gate", "ffn2_w_up", "ffn2_w_down", "final_norm"]
    both = {**mats, **vecs}
    return (loss, grad_x[None], *[both[n][0] for n in order], *[both[n][1] for n in order],
            *[both[n][2] for n in order], *[both[n][3] for n in order])
```

```python
import functools

import jax
import jax.numpy as jnp
from jax import lax
from jax.experimental import pallas as pl
from jax.experimental.pallas import tpu as pltpu

D_MODEL = 1024
D_FF = 2816
HEAD_DIM = 64
D_SB = 512
D_DIL = 512
D_IN = 3072
N_CHIP = 4
FFB = D_FF // N_CHIP
INB = D_IN // N_CHIP
OUTB = D_MODEL // N_CHIP
BLK = 128
LANES = 128
DILATIONS = (1, 4, 16)
ROPE_THETA = 10000.0
RMS_EPS = 1e-6
SCALE = HEAD_DIM ** -0.5
NEG = -1e30
DEAD = -104.0
ADAM_LR = 0.001
ADAM_B1 = 0.9
ADAM_B2 = 0.999
ADAM_EPS = 1e-08
ADAM_WD = 0.01
ADAM_STEP = 10
MESH = pl.DeviceIdType.MESH
F32 = jnp.float32
BF16 = jnp.bfloat16
TM = 512


def _params(vmem_mb):
    return pltpu.CompilerParams(vmem_limit_bytes=vmem_mb << 20)


def _dot(a, b):
    return jnp.dot(a, b, preferred_element_type=F32)


def _dot_nt(a, b):
    return lax.dot_general(a, b, (((1,), (1,)), ((), ())), preferred_element_type=F32)


def _dot_tn(a, b):
    return lax.dot_general(a, b, (((0,), (0,)), ((), ())), preferred_element_type=F32)


def _rms_fwd(x, g):
    r = lax.rsqrt(jnp.mean(x * x, axis=-1, keepdims=True) + RMS_EPS)
    xh = x * r
    return xh * g, xh, r


def _rms_bwd(dy, xh, r, g):
    dyg = dy * g
    dx = r * (dyg - xh * jnp.mean(dyg * xh, axis=-1, keepdims=True))
    return dx, jnp.sum(dy * xh, axis=0, keepdims=True)


def _split_bf16(a):
    hi = a.astype(BF16)
    return hi, (a - hi.astype(F32)).astype(BF16)


def _dot_split(a, b):
    hi, lo = _split_bf16(a)
    return _dot(hi, b) + _dot(lo, b)


def _ffn_weight_specs(f):
    return [pl.BlockSpec((None, None, D_MODEL, FFB), lambda i, j: (j, 2 * f, 0, 0)),
            pl.BlockSpec((None, None, D_MODEL, FFB), lambda i, j: (j, 2 * f + 1, 0, 0)),
            pl.BlockSpec((None, None, FFB, D_MODEL), lambda i, j: (j, f, 0, 0))]


def _ffn_saved(s):
    hidden = jax.ShapeDtypeStruct((N_CHIP, s, FFB), BF16)
    hid = pl.BlockSpec((None, TM, FFB), lambda i, j: (j, i, 0))
    row = pl.BlockSpec((TM, D_MODEL), lambda i, j: (i, 0))
    return [row, hid, hid, hid], [jax.ShapeDtypeStruct((s, D_MODEL), BF16), hidden, hidden, hidden]


def _ffn_accumulate(h_ref, acc_scr, wg_ref, wu_ref, wd_ref, a_ref, b_ref, act_ref):
    h = h_ref[...]
    a = _dot(h, wg_ref[...])
    b = _dot(h, wu_ref[...])
    act = ((a * jax.nn.sigmoid(a)) * b).astype(BF16)
    a_ref[...] = a.astype(BF16)
    b_ref[...] = b.astype(BF16)
    act_ref[...] = act
    acc_scr[...] += _dot(act, wd_ref[...])


def _ffn1_fwd(x, g1, gmix, gu, wd):
    s = x.shape[0]
    row = pl.BlockSpec((TM, D_MODEL), lambda i, j: (i, 0))
    vec = pl.BlockSpec((1, D_MODEL), lambda i, j: (0, 0))
    saved_specs, saved_shapes = _ffn_saved(s)

    def body(x_ref, g_ref, gm_ref, wg_ref, wu_ref, wd_ref, x1_ref, hm_ref, h_ref, a_ref, b_ref, act_ref, acc_scr):
        j = pl.program_id(1)

        @pl.when(j == 0)
        def _():
            h, _, _ = _rms_fwd(x_ref[...], g_ref[...])
            h_ref[...] = h.astype(BF16)
            acc_scr[...] = jnp.zeros_like(acc_scr)

        _ffn_accumulate(h_ref, acc_scr, wg_ref, wu_ref, wd_ref, a_ref, b_ref, act_ref)

        @pl.when(j == N_CHIP - 1)
        def _():
            x1 = x_ref[...] + 0.5 * acc_scr[...]
            x1_ref[...] = x1
            hm, _, _ = _rms_fwd(x1, gm_ref[...])
            hm_ref[...] = hm.astype(BF16)

    x1, hm, *saved = pl.pallas_call(
        body, name="ffn1_fwd", grid=(s // TM, N_CHIP),
        in_specs=[row, vec, vec] + _ffn_weight_specs(0),
        out_specs=[row, row] + saved_specs,
        out_shape=[jax.ShapeDtypeStruct((s, D_MODEL), F32), jax.ShapeDtypeStruct((s, D_MODEL), BF16)] + saved_shapes,
        scratch_shapes=[pltpu.VMEM((TM, D_MODEL), F32)],
        compiler_params=_params(48),
    )(x, g1, gmix, gu, gu, wd)
    return x1, hm, saved


def _ffn2_fwd_loss(x2, g2, gf, target, gu, wd):
    s = x2.shape[0]
    row = pl.BlockSpec((TM, D_MODEL), lambda i, j: (i, 0))
    vec = pl.BlockSpec((1, D_MODEL), lambda i, j: (0, 0))
    stat = pl.BlockSpec((8, D_MODEL), lambda i, j: (0, 0))
    saved_specs, saved_shapes = _ffn_saved(s)

    def body(x_ref, g_ref, gf_ref, t_ref, wg_ref, wu_ref, wd_ref, dx_ref, st_ref, h_ref, a_ref, b_ref, act_ref, acc_scr):
        i, j = pl.program_id(0), pl.program_id(1)

        @pl.when((i == 0) & (j == 0))
        def _():
            st_ref[...] = jnp.zeros_like(st_ref)

        @pl.when(j == 0)
        def _():
            h, _, _ = _rms_fwd(x_ref[...], g_ref[...])
            h_ref[...] = h.astype(BF16)
            acc_scr[...] = jnp.zeros_like(acc_scr)

        _ffn_accumulate(h_ref, acc_scr, wg_ref, wu_ref, wd_ref, a_ref, b_ref, act_ref)

        @pl.when(j == N_CHIP - 1)
        def _():
            x3 = x_ref[...] + 0.5 * acc_scr[...]
            y, xh, r = _rms_fwd(x3, gf_ref[...])
            err = y - t_ref[...]
            dx, dg = _rms_bwd(err * (1.0 / D_MODEL), xh, r, gf_ref[...])
            dx_ref[...] = dx
            st_ref[0:1, :] += dg
            st_ref[1:2, :] += jnp.sum(err * err, axis=0, keepdims=True)

    dx3, st, *saved = pl.pallas_call(
        body, name="ffn2_fwd_loss", grid=(s // TM, N_CHIP),
        in_specs=[row, vec, vec, row] + _ffn_weight_specs(1),
        out_specs=[row, stat] + saved_specs,
        out_shape=[jax.ShapeDtypeStruct((s, D_MODEL), F32), jax.ShapeDtypeStruct((8, D_MODEL), F32)] + saved_shapes,
        scratch_shapes=[pltpu.VMEM((TM, D_MODEL), F32)],
        compiler_params=_params(48),
    )(x2, g2, gf, target, gu, gu, wd)
    return dx3, st, saved


def _ffn_bwd(xin, g, dy, saved, gu, wd, f):
    s = xin.shape[0]
    hb, gate, up, act = saved
    row = pl.BlockSpec((TM, D_MODEL), lambda i, j: (i, 0))
    vec = pl.BlockSpec((1, D_MODEL), lambda i, j: (0, 0))
    stat = pl.BlockSpec((8, D_MODEL), lambda i, j: (0, 0))
    hid = pl.BlockSpec((None, TM, FFB), lambda i, j: (j, i, 0))

    def body(x_ref, g_ref, dy_ref, a_ref, b_ref, wg_ref, wu_ref, wd_ref, out_ref, dyh_ref, da_ref, db_ref, st_ref, dh_scr):
        i, j = pl.program_id(0), pl.program_id(1)

        @pl.when((i == 0) & (j == 0))
        def _():
            st_ref[...] = jnp.zeros_like(st_ref)

        @pl.when(j == 0)
        def _():
            dyh_ref[...] = (0.5 * dy_ref[...]).astype(BF16)
            dh_scr[...] = jnp.zeros_like(dh_scr)

        a = a_ref[...].astype(F32)
        b = b_ref[...].astype(F32)
        sg = jax.nn.sigmoid(a)
        dact = _dot_nt(dyh_ref[...], wd_ref[...])
        dab = (dact * b * (sg * (1.0 + a * (1.0 - sg)))).astype(BF16)
        dbb = (dact * (a * sg)).astype(BF16)
        da_ref[...] = dab
        db_ref[...] = dbb
        dh_scr[...] += _dot_nt(dab, wg_ref[...]) + _dot_nt(dbb, wu_ref[...])

        @pl.when(j == N_CHIP - 1)
        def _():
            _, xh, r = _rms_fwd(x_ref[...], g_ref[...])
            dx, dg = _rms_bwd(dh_scr[...], xh, r, g_ref[...])
            out_ref[...] = dy_ref[...] + dx
            st_ref[0:1, :] += dg

    hidden = jax.ShapeDtypeStruct((N_CHIP, s, FFB), BF16)
    dx, dyh, da, db, st = pl.pallas_call(
        body, name=f"ffn{f + 1}_bwd_dx", grid=(s // TM, N_CHIP),
        in_specs=[row, vec, row, hid, hid] + _ffn_weight_specs(f),
        out_specs=[row, row, hid, hid, stat],
        out_shape=[jax.ShapeDtypeStruct((s, D_MODEL), F32), jax.ShapeDtypeStruct((s, D_MODEL), BF16),
                   hidden, hidden, jax.ShapeDtypeStruct((8, D_MODEL), F32)],
        scratch_shapes=[pltpu.VMEM((TM, D_MODEL), F32)],
        compiler_params=_params(56),
    )(xin, g, dy, gate, up, gu, gu, wd)

    tok = pl.BlockSpec((TM, D_MODEL), lambda j, i: (i, 0))
    hid2 = pl.BlockSpec((None, TM, FFB), lambda j, i: (j, i, 0))
    gspecs = [pl.BlockSpec((None, D_MODEL, FFB), lambda j, i: (j, 0, 0)),
              pl.BlockSpec((None, D_MODEL, FFB), lambda j, i: (j, 0, 0)),
              pl.BlockSpec((None, FFB, D_MODEL), lambda j, i: (j, 0, 0))]

    def wbody(h_ref, dyh_ref, da_ref, db_ref, act_ref, dwg_ref, dwu_ref, dwd_ref):
        @pl.when(pl.program_id(1) == 0)
        def _():
            dwg_ref[...] = jnp.zeros_like(dwg_ref)
            dwu_ref[...] = jnp.zeros_like(dwu_ref)
            dwd_ref[...] = jnp.zeros_like(dwd_ref)

        hb = h_ref[...]
        dwg_ref[...] += _dot_tn(hb, da_ref[...])
        dwu_ref[...] += _dot_tn(hb, db_ref[...])
        dwd_ref[...] += _dot_tn(act_ref[...], dyh_ref[...])

    dwg, dwu, dwd = pl.pallas_call(
        wbody, name=f"ffn{f + 1}_bwd_dw", grid=(N_CHIP, s // TM),
        in_specs=[tok, tok, hid2, hid2, hid2], out_specs=gspecs,
        out_shape=[jax.ShapeDtypeStruct((N_CHIP, D_MODEL, FFB), F32),
                   jax.ShapeDtypeStruct((N_CHIP, D_MODEL, FFB), F32),
                   jax.ShapeDtypeStruct((N_CHIP, FFB, D_MODEL), F32)],
        compiler_params=_params(48),
    )(hb, dyh, da, db, act)
    return dx, dwg, dwu, dwd, st


def _rope_tables(s):
    half = HEAD_DIM // 2
    inv_freq = ROPE_THETA ** (-jnp.arange(half, dtype=F32) / half)
    ang = jnp.arange(s).astype(F32)[:, None] * inv_freq[None, :]
    cos, sin = jnp.cos(ang), jnp.sin(ang)
    cos2 = jnp.concatenate([cos, cos], axis=-1)
    sin2 = jnp.concatenate([-sin, sin], axis=-1)
    return jnp.tile(cos2, (1, LANES // HEAD_DIM)), jnp.tile(sin2, (1, LANES // HEAD_DIM))


def _rotate(t, cos, sin_signed):
    lane = lax.broadcasted_iota(jnp.int32, t.shape, 1)
    first = (lane % HEAD_DIM) < (HEAD_DIM // 2)
    partner = jnp.where(first, pltpu.roll(t, LANES - HEAD_DIM // 2, 1), pltpu.roll(t, HEAD_DIM // 2, 1))
    return t * cos + partner * sin_signed


def _proj_fwd(hm, win, cos, sin):
    s = hm.shape[0]
    n_sub = INB // LANES
    first_rot, last_rot = (3 * D_SB) // LANES, (3 * D_SB + 2 * D_DIL) // LANES

    def body(h_ref, w_ref, c_ref, s_ref, o_ref):
        j = pl.program_id(1)
        r = _dot(h_ref[...], w_ref[...])
        for c in range(n_sub):
            t = r[:, c * LANES:(c + 1) * LANES]
            col = j * n_sub + c
            rot = (col >= first_rot) & (col < last_rot)
            lanes = slice(c * LANES, (c + 1) * LANES)

            @pl.when(rot)
            def _():
                o_ref[:, lanes] = _rotate(t, c_ref[...], s_ref[...]).astype(BF16)

            @pl.when(jnp.logical_not(rot))
            def _():
                o_ref[:, lanes] = t.astype(BF16)

    return pl.pallas_call(
        body, name="proj_fwd", grid=(s // TM, N_CHIP),
        in_specs=[pl.BlockSpec((TM, D_MODEL), lambda i, j: (i, 0)),
                  pl.BlockSpec((None, D_MODEL, INB), lambda i, j: (j, 0, 0)),
                  pl.BlockSpec((TM, LANES), lambda i, j: (i, 0)),
                  pl.BlockSpec((TM, LANES), lambda i, j: (i, 0))],
        out_specs=pl.BlockSpec((TM, INB), lambda i, j: (i, j)),
        out_shape=jax.ShapeDtypeStruct((s, D_IN), BF16),
        compiler_params=_params(32),
    )(hm, win, cos, sin)


def _proj_bwd(x1, gmix, dqkv, win, dx2):
    s = x1.shape[0]
    row = pl.BlockSpec((TM, D_MODEL), lambda i, j: (i, 0))
    vec = pl.BlockSpec((1, D_MODEL), lambda i, j: (0, 0))

    def body(x_ref, g_ref, dq_ref, w_ref, dx2_ref, out_ref, dw_ref, st_ref, h_scr, dh_scr):
        i, j = pl.program_id(0), pl.program_id(1)

        @pl.when((i == 0) & (j == 0))
        def _():
            st_ref[...] = jnp.zeros_like(st_ref)
            dw_ref[...] = jnp.zeros_like(dw_ref)

        @pl.when(j == 0)
        def _():
            h, _, _ = _rms_fwd(x_ref[...], g_ref[...])
            h_scr[...] = h.astype(BF16)
            dh_scr[...] = jnp.zeros_like(dh_scr)

        dq = dq_ref[...]
        dw_ref[j] += _dot_tn(h_scr[...], dq)
        dh_scr[...] += _dot_nt(dq, w_ref[...])

        @pl.when(j == N_CHIP - 1)
        def _():
            _, xh, r = _rms_fwd(x_ref[...], g_ref[...])
            dx, dg = _rms_bwd(dh_scr[...], xh, r, g_ref[...])
            out_ref[...] = dx2_ref[...] + dx
            st_ref[0:1, :] += dg

    return pl.pallas_call(
        body, name="proj_bwd", grid=(s // TM, N_CHIP),
        in_specs=[row, vec, pl.BlockSpec((TM, INB), lambda i, j: (i, j)),
                  pl.BlockSpec((None, D_MODEL, INB), lambda i, j: (j, 0, 0)), row],
        out_specs=[row, pl.BlockSpec((N_CHIP, D_MODEL, INB), lambda i, j: (0, 0, 0)),
                   pl.BlockSpec((8, D_MODEL), lambda i, j: (0, 0))],
        out_shape=[jax.ShapeDtypeStruct((s, D_MODEL), F32),
                   jax.ShapeDtypeStruct((N_CHIP, D_MODEL, INB), F32),
                   jax.ShapeDtypeStruct((8, D_MODEL), F32)],
        scratch_shapes=[pltpu.VMEM((TM, D_MODEL), BF16), pltpu.VMEM((TM, D_MODEL), F32)],
        compiler_params=_params(56),
    )(x1, gmix, dqkv, win, dx2)


def _outproj_fwd(o_sb, o_dl, g_sb, g_dl, x1, wout):
    s = x1.shape[0]
    half = pl.BlockSpec((TM, D_SB), lambda i: (i, 0))
    row = pl.BlockSpec((TM, D_MODEL), lambda i: (i, 0))
    vec = pl.BlockSpec((1, D_SB), lambda i: (0, 0))

    def body(a_ref, b_ref, ga_ref, gb_ref, x_ref, w_ref, o_ref):
        ma, _, _ = _rms_fwd(a_ref[...], ga_ref[...])
        mb, _, _ = _rms_fwd(b_ref[...], gb_ref[...])
        o_ref[...] = (x_ref[...] + _dot(ma.astype(BF16), w_ref[0:D_SB, :])
                      + _dot(mb.astype(BF16), w_ref[D_SB:D_MODEL, :]))

    return pl.pallas_call(
        body, name="outproj_fwd", grid=(s // TM,),
        in_specs=[half, half, vec, vec, row, pl.BlockSpec((D_MODEL, D_MODEL), lambda i: (0, 0))],
        out_specs=row, out_shape=jax.ShapeDtypeStruct((s, D_MODEL), F32),
        compiler_params=_params(32),
    )(o_sb, o_dl, g_sb, g_dl, x1, wout)


def _outproj_bwd(dx2, o_sb, o_dl, g_sb, g_dl, wout):
    s = dx2.shape[0]
    half = pl.BlockSpec((TM, D_SB), lambda i: (i, 0))
    row = pl.BlockSpec((TM, D_MODEL), lambda i: (i, 0))
    vec = pl.BlockSpec((1, D_SB), lambda i: (0, 0))
    full = pl.BlockSpec((D_MODEL, D_MODEL), lambda i: (0, 0))

    def body(dy_ref, a_ref, b_ref, ga_ref, gb_ref, w_ref, da_ref, db_ref, dw_ref, st_ref):
        @pl.when(pl.program_id(0) == 0)
        def _():
            dw_ref[...] = jnp.zeros_like(dw_ref)
            st_ref[...] = jnp.zeros_like(st_ref)

        dy = dy_ref[...].astype(BF16)
        dm = _dot_nt(dy, w_ref[...])
        ma, xa, ra = _rms_fwd(a_ref[...], ga_ref[...])
        mb, xb, rb = _rms_fwd(b_ref[...], gb_ref[...])
        dw_ref[0:D_SB, :] += _dot_tn(ma.astype(BF16), dy)
        dw_ref[D_SB:D_MODEL, :] += _dot_tn(mb.astype(BF16), dy)
        da, dga = _rms_bwd(dm[:, 0:D_SB], xa, ra, ga_ref[...])
        db, dgb = _rms_bwd(dm[:, D_SB:D_MODEL], xb, rb, gb_ref[...])
        da_ref[...] = da
        db_ref[...] = db
        st_ref[0:1, :] += dga
        st_ref[1:2, :] += dgb

    return pl.pallas_call(
        body, name="outproj_bwd", grid=(s // TM,),
        in_specs=[row, half, half, vec, vec, full],
        out_specs=[half, half, full, pl.BlockSpec((8, D_SB), lambda i: (0, 0))],
        out_shape=[jax.ShapeDtypeStruct((s, D_SB), F32), jax.ShapeDtypeStruct((s, D_SB), F32),
                   jax.ShapeDtypeStruct((D_MODEL, D_MODEL), F32), jax.ShapeDtypeStruct((8, D_SB), F32)],
        compiler_params=_params(48),
    )(dx2, o_sb, o_dl, g_sb, g_dl, wout)


def _head_masks():
    lane = lax.broadcasted_iota(jnp.int32, (BLK, LANES), 1)
    return [lane < HEAD_DIM, lane >= HEAD_DIM]


def _keep(mask, a):
    return a * jnp.where(mask, 1.0, 0.0).astype(a.dtype)


def _suffix_matrices():
    r = lax.broadcasted_iota(jnp.int32, (BLK, BLK), 0)
    c = lax.broadcasted_iota(jnp.int32, (BLK, BLK), 1)
    ones = jnp.ones((BLK, BLK), BF16)
    excl = jnp.concatenate([(r > c).astype(BF16), ones], axis=1)
    incl = jnp.concatenate([(r >= c).astype(BF16), ones], axis=1)
    return excl, incl


def _blk(i):
    return pl.ds(pl.multiple_of(i * BLK, BLK), BLK)


def _alive(carry_m):
    return (jnp.max(carry_m) > DEAD).astype(jnp.int32)


def _more_keys(i, carry):
    return (carry[0] <= i) & (carry[1] > 0)


def _stack_heads(a):
    masks = _head_masks()
    return jnp.concatenate([_keep(masks[0], a), _keep(masks[1], a)], axis=0)


def _unstack_heads(a2):
    return jnp.where(_head_masks()[0], a2[:BLK], a2[BLK:])


def _head_rowsum(a):
    masks = _head_masks()
    return jnp.concatenate([jnp.sum(jnp.where(m, a, 0.0), axis=1, keepdims=True) for m in masks], axis=0)


SB_QB = 2
SB_ROWS = SB_QB * 2 * BLK


def _sb_rows(ref, i0, cast=None):
    tiles = [ref[_blk(i0 + t), :] for t in range(SB_QB)]
    return jnp.concatenate([_stack_heads(t if cast is None else t.astype(cast)) for t in tiles], axis=0)


def _sb_scores(q2, k, i, j, carry_m, u_excl):
    r = lax.broadcasted_iota(jnp.int32, (SB_ROWS, BLK), 0)
    row = (r & (BLK - 1)) + ((r >> 8) << 7)
    col = lax.broadcasted_iota(jnp.int32, (SB_ROWS, BLK), 1)
    valid = (j * BLK + col) < (i * BLK + row)
    z = _dot_nt(q2, k) * SCALE
    sp = jnp.maximum(z, 0.0) + jnp.log(1.0 + jnp.exp(-jnp.abs(z)))
    log_stay = jnp.where(valid, -sp, 0.0)
    log_beta = z - sp
    sums = _dot_split(log_stay, u_excl)
    later = carry_m + sums[:, :BLK]
    w = jnp.where(valid, jnp.exp(log_beta + later), 0.0)
    return valid, log_beta, w, carry_m + sums[:, BLK:]


def _sb_fwd(qkv):
    s = qkv.shape[0]
    nq = s // BLK
    pairs = D_SB // LANES
    col = lambda off: pl.BlockSpec((s, LANES), lambda p: (0, off + p))

    def body(q_ref, k_ref, v_ref, o_ref):
        u_excl, _ = _suffix_matrices()
        zero = jnp.zeros((SB_ROWS, LANES), F32)

        def q_block(ib, _):
            i = ib * SB_QB
            last = i + SB_QB - 1
            q2 = _sb_rows(q_ref, i)

            def k_block(carry):
                jj, _, carry_m, acc = carry
                j = last - jj
                _, _, w, carry_m = _sb_scores(q2, k_ref[_blk(j), :], i, j, carry_m, u_excl)
                return jj + 1, _alive(carry_m), carry_m, acc + _dot(w.astype(BF16), v_ref[_blk(j), :])

            _, _, _, acc = lax.while_loop(functools.partial(_more_keys, last), k_block,
                                          (jnp.int32(0), jnp.int32(1), zero, zero))
            for t in range(SB_QB):
                o_ref[_blk(i + t), :] = _unstack_heads(acc[2 * BLK * t:2 * BLK * (t + 1)])
            return 0

        lax.fori_loop(0, nq // SB_QB, q_block, 0)

    return pl.pallas_call(
        body, name="sb_fwd", grid=(pairs,),
        in_specs=[col(0), col(pairs), col(2 * pairs)],
        out_specs=pl.BlockSpec((s, LANES), lambda p: (0, p)),
        out_shape=jax.ShapeDtypeStruct((s, D_SB), F32),
        compiler_params=_params(48),
    )(qkv, qkv, qkv)


def _sb_bwd(qkv, o_sb, do_sb):
    s = qkv.shape[0]
    nq = s // BLK
    pairs = D_SB // LANES
    col = lambda off: pl.BlockSpec((s, LANES), lambda p: (0, off + p))
    own = pl.BlockSpec((s, LANES), lambda p: (0, p))

    def body(q_ref, k_ref, v_ref, o_ref, do_ref, dq_ref, dk_ref, dv_ref, dk_acc, dv_acc):
        u_excl, u_incl = _suffix_matrices()
        zero = jnp.zeros((SB_ROWS, LANES), F32)
        dk_acc[...] = jnp.zeros_like(dk_acc)
        dv_acc[...] = jnp.zeros_like(dv_acc)

        def q_block(ib, _):
            i = ib * SB_QB
            last = i + SB_QB - 1
            q2 = _sb_rows(q_ref, i)
            do2 = _sb_rows(do_ref, i, BF16)
            totals = [_head_rowsum(do_ref[_blk(i + t), :].astype(BF16).astype(F32) * o_ref[_blk(i + t), :])
                      for t in range(SB_QB)]
            total = jnp.broadcast_to(jnp.concatenate(totals, axis=0), (SB_ROWS, BLK))

            def k_block(carry):
                jj, _, carry_m, carry_g, dq = carry
                j = last - jj
                k = k_ref[_blk(j), :]
                valid, log_beta, w, carry_m = _sb_scores(q2, k, i, j, carry_m, u_excl)
                wb = w.astype(BF16)
                g = wb.astype(F32) * _dot_nt(do2, v_ref[_blk(j), :])
                sums = _dot_split(g, u_incl)
                before = total - (carry_g + sums[:, :BLK])
                dz = jnp.where(valid, g - jnp.exp(log_beta) * (g + before), 0.0)
                dzb = (dz * SCALE).astype(BF16)
                dk_acc[_blk(j), :] += _dot_tn(dzb, q2)
                dv_acc[_blk(j), :] += _dot_tn(wb, do2)
                return jj + 1, _alive(carry_m), carry_m, carry_g + sums[:, BLK:], dq + _dot(dzb, k)

            _, _, _, _, dq = lax.while_loop(functools.partial(_more_keys, last), k_block,
                                            (jnp.int32(0), jnp.int32(1), zero, zero, zero))
            for t in range(SB_QB):
                dq_ref[_blk(i + t), :] = _unstack_heads(dq[2 * BLK * t:2 * BLK * (t + 1)]).astype(BF16)
            return 0

        lax.fori_loop(0, nq // SB_QB, q_block, 0)
        dk_ref[...] = dk_acc[...].astype(BF16)
        dv_ref[...] = dv_acc[...].astype(BF16)

    return pl.pallas_call(
        body, name="sb_bwd", grid=(pairs,),
        in_specs=[col(0), col(pairs), col(2 * pairs), own, own],
        out_specs=[own, own, own],
        out_shape=[jax.ShapeDtypeStruct((s, D_SB), BF16)] * 3,
        scratch_shapes=[pltpu.VMEM((s, LANES), F32), pltpu.VMEM((s, LANES), F32)],
        compiler_params=_params(56),
    )(qkv, qkv, qkv, o_sb, do_sb)


def _fold_masks(b):
    row = lax.broadcasted_iota(jnp.int32, (2 * BLK, BLK), 0) & (BLK - 1)
    col = lax.broadcasted_iota(jnp.int32, (2 * BLK, BLK), 1)
    return col <= row, col <= row + jnp.where(b > 0, BLK, 0)


def _fold(own_side, both):
    return jnp.where(own_side, both[:, :BLK], both[:, BLK:])


def _unfold(own_side, tile):
    return jnp.concatenate([jnp.where(own_side, tile, 0.0), jnp.where(own_side, 0.0, tile)], axis=1).astype(BF16)


def _twice(a):
    return jnp.concatenate([a, a], axis=0)


def _fold_scores(b, q2, kc, kp):
    own_side, is_key = _fold_masks(b)
    z = jnp.where(is_key, _fold(own_side, _dot_nt(q2, jnp.concatenate([kc, kp], axis=0)) * SCALE), NEG)
    far = jnp.sum(q2.astype(F32) * _twice(kp).astype(F32), axis=1, keepdims=True) * SCALE
    return own_side, z, far + jnp.where(b > 0, 0.0, NEG)


def _dil_tiles(qf, kf, vf, d, t, nb):
    c, b = t // nb, t % nb
    start = c + d * BLK * b
    rows = pl.ds(start, BLK, stride=d)
    prev = pl.ds(jnp.where(b > 0, start - d * BLK, start), BLK, stride=d)
    bf = lambda ref, sl: ref[sl, :].astype(BF16)
    return b, rows, prev, _stack_heads(bf(qf, rows)), bf(kf, rows), bf(kf, prev), bf(vf, rows), bf(vf, prev)


def _lanes_of_heads(col2):
    return _unstack_heads(jnp.broadcast_to(col2, (2 * BLK, LANES)))


def _dilated_fwd(qkv):
    s = qkv.shape[0]
    pairs = D_DIL // LANES
    base = (3 * D_SB) // LANES
    col = lambda off: pl.BlockSpec((s, LANES), lambda p: (0, off + p))
    own = pl.BlockSpec((s, LANES), lambda p: (0, p))

    def body(q_ref, k_ref, v_ref, acc_ref, m_ref, qf, kf, vf, l_scr):
        qf[...] = q_ref[...].astype(F32)
        kf[...] = k_ref[...].astype(F32)
        vf[...] = v_ref[...].astype(F32)
        for d in DILATIONS:
            nb = s // (d * BLK)

            def block(t, _):
                b, rows, prev, q2, kc, kp, vc, vp = _dil_tiles(qf, kf, vf, d, t, nb)
                own_side, z, z_far = _fold_scores(b, q2, kc, kp)
                m = jnp.maximum(jnp.max(z, axis=1, keepdims=True), z_far)
                p, p_far = jnp.exp(z - m), jnp.exp(z_far - m)
                den = jnp.sum(p, axis=1, keepdims=True) + p_far
                acc = _unstack_heads(_dot(_unfold(own_side, p), jnp.concatenate([vc, vp], axis=0))
                                     + p_far * _twice(vp).astype(F32))
                m_t, l_t = _lanes_of_heads(m), _lanes_of_heads(den)
                if d == DILATIONS[0]:
                    m_ref[rows, :] = m_t
                    l_scr[rows, :] = l_t
                    acc_ref[rows, :] = acc
                else:
                    m_old = m_ref[rows, :]
                    m_new = jnp.maximum(m_old, m_t)
                    keep, add = jnp.exp(m_old - m_new), jnp.exp(m_t - m_new)
                    m_ref[rows, :] = m_new
                    l_scr[rows, :] = l_scr[rows, :] * keep + l_t * add
                    acc_ref[rows, :] = acc_ref[rows, :] * keep + acc * add
                return 0

            lax.fori_loop(0, s // BLK, block, 0, unroll=2)

        def finish(i, _):
            l = l_scr[_blk(i), :]
            acc_ref[_blk(i), :] = acc_ref[_blk(i), :] / l
            m_ref[_blk(i), :] = m_ref[_blk(i), :] + jnp.log(l)
            return 0

        lax.fori_loop(0, s // BLK, finish, 0)

    return pl.pallas_call(
        body, name="dilated_fwd", grid=(pairs,),
        in_specs=[col(base), col(base + pairs), col(base + 2 * pairs)],
        out_specs=[own, own],
        out_shape=[jax.ShapeDtypeStruct((s, D_DIL), F32)] * 2,
        scratch_shapes=[pltpu.VMEM((s, LANES), F32)] * 4,
        compiler_params=_params(56),
    )(qkv, qkv, qkv)


def _dilated_bwd(qkv, out, lse, dout):
    s = qkv.shape[0]
    pairs = D_DIL // LANES
    base = (3 * D_SB) // LANES
    once = pl.Buffered(1)
    col = lambda off: pl.BlockSpec((s, LANES), lambda p: (0, off + p), pipeline_mode=once)
    own = pl.BlockSpec((s, LANES), lambda p: (0, p), pipeline_mode=once)
    res = pl.BlockSpec((s, LANES), lambda p: (0, p))

    def body(q_ref, k_ref, v_ref, o_ref, l_ref, do_ref, dq_ref, dk_ref, dv_ref, qf, kf, vf):
        masks = _head_masks()
        qf[...] = q_ref[...].astype(F32)
        kf[...] = k_ref[...].astype(F32)
        vf[...] = v_ref[...].astype(F32)
        dq_ref[...] = jnp.zeros_like(dq_ref)
        dk_ref[...] = jnp.zeros_like(dk_ref)
        dv_ref[...] = jnp.zeros_like(dv_ref)
        for d in DILATIONS:
            nb = s // (d * BLK)

            def block(t, _):
                b, rows, prev, q2, kc, kp, vc, vp = _dil_tiles(qf, kf, vf, d, t, nb)
                own_side, z, z_far = _fold_scores(b, q2, kc, kp)
                do32 = do_ref[rows, :]
                do2 = _stack_heads(do32.astype(BF16))
                delta = _head_rowsum(do32 * o_ref[rows, :])
                lse_t = l_ref[rows, :]
                lse2 = jnp.concatenate([jnp.max(jnp.where(m, lse_t, NEG), axis=1, keepdims=True) for m in masks], axis=0)
                k2, v2 = jnp.concatenate([kc, kp], axis=0), jnp.concatenate([vc, vp], axis=0)
                w, w_far = jnp.exp(z - lse2), jnp.exp(z_far - lse2)
                dp_far = jnp.sum(do2.astype(F32) * _twice(vp).astype(F32), axis=1, keepdims=True)
                dz = _unfold(own_side, w * (_fold(own_side, _dot_nt(do2, v2)) - delta) * SCALE)
                dz_far = w_far * (dp_far - delta) * SCALE
                dq_ref[rows, :] += _unstack_heads(_dot(dz, k2) + dz_far * _twice(kp).astype(F32))
                dk2 = _dot_tn(dz, q2)
                dv2 = _dot_tn(_unfold(own_side, w), do2)
                far_k = dz_far * q2.astype(F32)
                far_v = w_far * do2.astype(F32)
                dk_ref[rows, :] += dk2[:BLK]
                dk_ref[prev, :] += dk2[BLK:] + far_k[:BLK] + far_k[BLK:]
                dv_ref[rows, :] += dv2[:BLK]
                dv_ref[prev, :] += dv2[BLK:] + far_v[:BLK] + far_v[BLK:]
                return 0

            lax.fori_loop(0, s // BLK, block, 0, unroll=2)

    return pl.pallas_call(
        body, name="dilated_bwd", grid=(pairs,),
        in_specs=[col(base), col(base + pairs), col(base + 2 * pairs), own, own, own],
        out_specs=[res, res, res],
        out_shape=[jax.ShapeDtypeStruct((s, D_DIL), F32)] * 3,
        scratch_shapes=[pltpu.VMEM((s, LANES), F32)] * 3,
        compiler_params=_params(60),
    )(qkv, qkv, qkv, out, lse, dout)


def _dilated_finish(grads, cos, sin):
    s = grads[0].shape[0]
    spec = pl.BlockSpec((TM, D_DIL), lambda i: (i, 0))
    tab = pl.BlockSpec((TM, LANES), lambda i: (i, 0))

    def body(dq_ref, dk_ref, dv_ref, c_ref, s_ref, oq_ref, ok_ref, ov_ref):
        for src, dst, rotated in ((dq_ref, oq_ref, True), (dk_ref, ok_ref, True), (dv_ref, ov_ref, False)):
            for c in range(D_DIL // LANES):
                lanes = slice(c * LANES, (c + 1) * LANES)
                piece = src[:, lanes]
                dst[:, lanes] = (_rotate(piece, c_ref[...], -s_ref[...]) if rotated else piece).astype(BF16)

    return pl.pallas_call(
        body, name="dilated_finish", grid=(s // TM,),
        in_specs=[spec] * 3 + [tab, tab], out_specs=[spec] * 3,
        out_shape=[jax.ShapeDtypeStruct((s, D_DIL), BF16)] * 3,
        compiler_params=_params(32),
    )(*grads, cos, sin)


def _place():
    x, y, c = lax.axis_index("x"), lax.axis_index("y"), lax.axis_index("c")
    return x, y, c, 2 * x + y


def _chip(k, c):
    return (k >> 1, k & 1, c)


def _half(ref, h):
    n = ref.shape[0] // 2
    return ref.at[pl.ds(h * n, n)]


def _all_gather(shards):
    na = len(shards)
    any_spec = pl.BlockSpec(memory_space=pl.ANY)

    def body(*refs):
        ins, outs = refs[:na], refs[na:2 * na]
        send_sem, recv_sem, local_sem = refs[2 * na:]
        x, y, c, k = _place()
        sibling = (x, y, 1 - c)
        started = []
        for a in range(na):
            cp = pltpu.make_async_copy(ins[a], outs[a].at[k], local_sem.at[a])
            cp.start()
            started.append(cp)

        def copy(a, slot, src, dst, to):
            return pltpu.make_async_remote_copy(src_ref=src, dst_ref=dst, send_sem=send_sem.at[a * 6 + slot],
                                                recv_sem=recv_sem.at[a * 6 + slot], device_id=to, device_id_type=MESH)

        sends = []
        for a in range(na):
            for j in range(1, N_CHIP):
                cp = copy(a, j - 1, _half(ins[a], c), _half(outs[a].at[k], c), _chip(k ^ j, c))
                cp.start()
                sends.append(cp)
        for j in range(1, N_CHIP):
            for a in range(na):
                landed = _half(outs[a].at[k ^ j], c)
                copy(a, j - 1, landed, landed, sibling).wait_recv()
                cp = copy(a, 2 + j, landed, landed, sibling)
                cp.start()
                sends.append(cp)
        for j in range(1, N_CHIP):
            for a in range(na):
                passed = _half(outs[a].at[k ^ j], 1 - c)
                copy(a, 2 + j, passed, passed, sibling).wait_recv()
        for cp in sends:
            cp.wait_send()
        for cp in started:
            cp.wait()

    return pl.pallas_call(
        body, name="weights_all_gather",
        in_specs=[any_spec] * na, out_specs=[any_spec] * na,
        out_shape=[jax.ShapeDtypeStruct((N_CHIP,) + a.shape, a.dtype) for a in shards],
        scratch_shapes=[pltpu.SemaphoreType.DMA((6 * na,)), pltpu.SemaphoreType.DMA((6 * na,)),
                        pltpu.SemaphoreType.DMA((na,))],
    )(*shards)


def _reduce_scatter(g, core, name):
    n, r, c = g.shape
    hr = r // 2
    once = pl.Buffered(1)
    in_specs = [pl.BlockSpec((n, hr, c), lambda i, core_ref: (0, core_ref[0], 0), pipeline_mode=once),
                pl.BlockSpec((n, hr, c), lambda i, core_ref: (0, 1 - core_ref[0], 0), pipeline_mode=once)]

    def body(core_ref, mine_ref, other_ref, out_ref, from_core, sums, sums_bf, from_chips, done, from_core2, send_sem, recv_sem):
        x, y, cc, k = _place()
        sibling = (x, y, 1 - cc)

        def copy(slot, src, dst, to):
            return pltpu.make_async_remote_copy(src_ref=src, dst_ref=dst, send_sem=send_sem.at[slot],
                                                recv_sem=recv_sem.at[slot], device_id=to, device_id_type=MESH)

        first = copy(0, other_ref, from_core, sibling)
        first.start()
        first.wait()
        total = mine_ref[...] + from_core[...]
        sums[...] = total
        sums_bf[...] = total.astype(BF16)
        sends = [copy(j, sums_bf.at[k ^ j], from_chips.at[j - 1], _chip(k ^ j, cc)) for j in range(1, N_CHIP)]
        for cp in sends:
            cp.start()
        for cp in sends:
            cp.wait()
        red = sums[k]
        for j in range(1, N_CHIP):
            red = red + from_chips[j - 1].astype(F32)
        done[...] = red
        last = copy(N_CHIP, done, from_core2, sibling)
        last.start()
        last.wait()
        row0 = pl.multiple_of(cc * hr, 8)
        row1 = pl.multiple_of((1 - cc) * hr, 8)
        out_ref[pl.ds(row0, hr), :] = red
        out_ref[pl.ds(row1, hr), :] = from_core2[...]

    grid_spec = pltpu.PrefetchScalarGridSpec(
        num_scalar_prefetch=1, grid=(1,), in_specs=in_specs,
        out_specs=pl.BlockSpec((r, c), lambda i, core_ref: (0, 0)),
        scratch_shapes=[pltpu.VMEM((n, hr, c), F32), pltpu.VMEM((n, hr, c), F32), pltpu.VMEM((n, hr, c), BF16),
                        pltpu.VMEM((N_CHIP - 1, hr, c), BF16), pltpu.VMEM((hr, c), F32), pltpu.VMEM((hr, c), F32),
                        pltpu.SemaphoreType.DMA((N_CHIP + 1,)), pltpu.SemaphoreType.DMA((N_CHIP + 1,))])
    return pl.pallas_call(
        body, name=name, grid_spec=grid_spec, out_shape=jax.ShapeDtypeStruct((r, c), F32),
        compiler_params=_params(56),
    )(core, g, g)


def _elementwise(fn, name, ins, n_out, rows):
    total, cols = ins[0].shape
    spec = pl.BlockSpec((rows, cols), lambda i: (i, 0))

    def body(*refs):
        res = fn(*[r[...] for r in refs[:len(ins)]])
        for o, v in zip(refs[len(ins):], res):
            o[...] = v

    return pl.pallas_call(
        body, name=name, grid=(total // rows,),
        in_specs=[spec] * len(ins), out_specs=[spec] * n_out,
        out_shape=[jax.ShapeDtypeStruct((total, cols), F32)] * n_out,
        compiler_params=_params(48),
    )(*ins)


def _adamw(w, g, m, v):
    m = ADAM_B1 * m + (1.0 - ADAM_B1) * g
    v = ADAM_B2 * v + (1.0 - ADAM_B2) * (g * g)
    m_hat = m / (1.0 - ADAM_B1 ** ADAM_STEP)
    v_hat = v / (1.0 - ADAM_B2 ** ADAM_STEP)
    delta = -ADAM_LR * (m_hat / (jnp.sqrt(v_hat) + ADAM_EPS) + ADAM_WD * w)
    return delta, m, v


def _reduce_and_update(grads, weights, moms, vels):
    core = lax.axis_index("c").astype(jnp.int32).reshape(1)
    full = [_reduce_scatter(g, core, f"grads_reduce_scatter_{a}") for a, g in enumerate(grads)]
    out = []
    for a, (g, w, m, v) in enumerate(zip(full, weights, moms, vels)):
        rows = g.shape[0] // 2
        out.append((g,) + tuple(_elementwise(lambda gg, ww, mm, vv: _adamw(ww, gg, mm, vv), f"adamw_{a}", [g, w, m, v], 3, rows)))
    return out


def _reduce_vectors(part, w, m, v):
    n_dev = 8

    def body(p_ref, w_ref, m_ref, v_ref, g_ref, d_ref, nm_ref, nv_ref, buf, send_sem, recv_sem):
        x, y, c, _ = _place()
        me = 4 * x + 2 * y + c
        buf[me] = p_ref[...]
        sends = []
        for off in range(1, n_dev):
            peer = me ^ off
            cp = pltpu.make_async_remote_copy(src_ref=p_ref, dst_ref=buf.at[me], send_sem=send_sem.at[off - 1],
                                              recv_sem=recv_sem.at[off - 1], device_id=(peer >> 2, (peer >> 1) & 1, peer & 1),
                                              device_id_type=MESH)
            cp.start()
            sends.append(cp)
        for off in range(1, n_dev):
            peer = me ^ off
            pltpu.make_async_remote_copy(src_ref=p_ref, dst_ref=buf.at[peer], send_sem=send_sem.at[off - 1],
                                         recv_sem=recv_sem.at[off - 1], device_id=(peer >> 2, (peer >> 1) & 1, peer & 1),
                                         device_id_type=MESH).wait_recv()
        for cp in sends:
            cp.wait_send()
        g = buf[0]
        for d in range(1, n_dev):
            g = g + buf[d]
        g_ref[...] = g
        delta, nm, nv = _adamw(w_ref[...], g, m_ref[...], v_ref[...])
        d_ref[...] = delta
        nm_ref[...] = nm
        nv_ref[...] = nv

    vm = pl.BlockSpec(memory_space=pltpu.VMEM)
    return pl.pallas_call(
        body, name="gains_all_reduce",
        in_specs=[vm] * 4, out_specs=[vm] * 4,
        out_shape=[jax.ShapeDtypeStruct(part.shape, F32)] * 4,
        scratch_shapes=[pltpu.VMEM((n_dev,) + part.shape, F32), pltpu.SemaphoreType.DMA((n_dev - 1,)),
                        pltpu.SemaphoreType.DMA((n_dev - 1,))],
    )(part, w, m, v)


def _pad_row(a):
    a = a.reshape(1, -1)
    return jnp.pad(a, ((0, 0), (0, D_MODEL - a.shape[1])))


def kernel(x, ffn1_norm, ffn1_w_gate, ffn1_w_up, ffn1_w_down, mix_norm, w_in, sb_out_norm, dil_out_norm, w_out, ffn2_norm, ffn2_w_gate, ffn2_w_up, ffn2_w_down, final_norm, loss_target, m_ffn1_norm, m_ffn1_w_gate, m_ffn1_w_up, m_ffn1_w_down, m_mix_norm, m_w_in, m_sb_out_norm, m_dil_out_norm, m_w_out, m_ffn2_norm, m_ffn2_w_gate, m_ffn2_w_up, m_ffn2_w_down, m_final_norm, v_ffn1_norm, v_ffn1_w_gate, v_ffn1_w_up, v_ffn1_w_down, v_mix_norm, v_w_in, v_sb_out_norm, v_dil_out_norm, v_w_out, v_ffn2_norm, v_ffn2_w_gate, v_ffn2_w_up, v_ffn2_w_down, v_final_norm):
    x = x[0]
    target = loss_target[0]
    s = x.shape[0]
    gf = final_norm.reshape(1, D_MODEL)
    cos, sin = _rope_tables(s)

    gu_shard = jnp.stack([ffn1_w_gate[0], ffn1_w_up[0], ffn2_w_gate[0], ffn2_w_up[0]]).astype(BF16)
    wd_shard = jnp.stack([ffn1_w_down[0], ffn2_w_down[0]]).astype(BF16)
    gu, wd, win, wout = _all_gather([gu_shard, wd_shard, w_in[0].astype(BF16), w_out[0].astype(BF16)])
    wout = wout.reshape(D_MODEL, D_MODEL)

    x1, hm, saved1 = _ffn1_fwd(x, ffn1_norm, mix_norm, gu, wd)
    qkv = _proj_fwd(hm, win, cos, sin)
    o_sb = _sb_fwd(qkv)
    o_dl, lse = _dilated_fwd(qkv)
    x2 = _outproj_fwd(o_sb, o_dl, sb_out_norm, dil_out_norm, x1, wout)
    dx3, st_final, saved2 = _ffn2_fwd_loss(x2, ffn2_norm, gf, target, gu, wd)

    dx2, dwg2, dwu2, dwd2, st_ffn2 = _ffn_bwd(x2, ffn2_norm, dx3, saved2, gu, wd, 1)
    do_sb, do_dl, dwout, st_out = _outproj_bwd(dx2, o_sb, o_dl, sb_out_norm, dil_out_norm, wout)
    dq_sb, dk_sb, dv_sb = _sb_bwd(qkv, o_sb, do_sb)
    dq_dl, dk_dl, dv_dl = _dilated_finish(_dilated_bwd(qkv, o_dl, lse, do_dl), cos, sin)
    dqkv = jnp.concatenate([dq_sb, dk_sb, dv_sb, dq_dl, dk_dl, dv_dl], axis=1)
    dx1, dwin, st_mix = _proj_bwd(x1, mix_norm, dqkv, win, dx2)
    grad_x, dwg1, dwu1, dwd1, st_ffn1 = _ffn_bwd(x, ffn1_norm, dx1, saved1, gu, wd, 0)

    names = ["ffn1_w_gate", "ffn1_w_up", "ffn1_w_down", "w_in", "w_out", "ffn2_w_gate", "ffn2_w_up", "ffn2_w_down"]
    grads = [dwg1, dwu1, dwd1, dwin, dwout.reshape(N_CHIP, OUTB, D_MODEL), dwg2, dwu2, dwd2]
    weights = [ffn1_w_gate[0], ffn1_w_up[0], ffn1_w_down[0], w_in[0], w_out[0], ffn2_w_gate[0], ffn2_w_up[0], ffn2_w_down[0]]
    moms = [m_ffn1_w_gate[0], m_ffn1_w_up[0], m_ffn1_w_down[0], m_w_in[0], m_w_out[0], m_ffn2_w_gate[0], m_ffn2_w_up[0], m_ffn2_w_down[0]]
    vels = [v_ffn1_w_gate[0], v_ffn1_w_up[0], v_ffn1_w_down[0], v_w_in[0], v_w_out[0], v_ffn2_w_gate[0], v_ffn2_w_up[0], v_ffn2_w_down[0]]
    mats = {n: tuple(t[None] for t in r) for n, r in zip(names, _reduce_and_update(grads, weights, moms, vels))}

    vec_names = ["ffn1_norm", "mix_norm", "sb_out_norm", "dil_out_norm", "ffn2_norm", "final_norm"]
    part = jnp.concatenate([st_ffn1[0:1], st_mix[0:1], _pad_row(st_out[0]), _pad_row(st_out[1]), st_ffn2[0:1],
                            st_final[0:1], st_final[1:2], jnp.zeros((1, D_MODEL), F32)], axis=0)
    pack = lambda arrs: jnp.concatenate([_pad_row(a) for a in arrs] + [jnp.zeros((2, D_MODEL), F32)], axis=0)
    g_vec, d_vec, m_vec, v_vec = _reduce_vectors(
        part,
        pack([ffn1_norm, mix_norm, sb_out_norm, dil_out_norm, ffn2_norm, final_norm]),
        pack([m_ffn1_norm, m_mix_norm, m_sb_out_norm, m_dil_out_norm, m_ffn2_norm, m_final_norm]),
        pack([v_ffn1_norm, v_mix_norm, v_sb_out_norm, v_dil_out_norm, v_ffn2_norm, v_final_norm]))
    like = {"ffn1_norm": ffn1_norm, "mix_norm": mix_norm, "sb_out_norm": sb_out_norm, "dil_out_norm": dil_out_norm,
            "ffn2_norm": ffn2_norm, "final_norm": final_norm}
    vecs = {n: tuple(t[i, :like[n].size].reshape(like[n].shape) for t in (g_vec, d_vec, m_vec, v_vec))
            for i, n in enumerate(vec_names)}
    loss = 0.5 * jnp.sum(g_vec[6]) / D_MODEL

    order = ["ffn1_norm", "ffn1_w_gate", "ffn1_w_up", "ffn1_w_down", "mix_norm", "w_in", "sb_out_norm", "dil_out_norm",
             "w_out", "ffn2_norm", "ffn2_w_gate", "ffn2_w_up", "ffn2_w_down", "final_norm"]
    both = {**mats, **vecs}
    return (loss, grad_x[None], *[both[n][0] for n in order], *[both[n][1] for n in order],
            *[both[n][2] for n in order], *[both[n][3] for n in order])
```

```python
import functools

import jax
import jax.numpy as jnp
from jax import lax
from jax.experimental import pallas as pl
from jax.experimental.pallas import tpu as pltpu

D_MODEL = 1024
D_FF = 2816
HEAD_DIM = 64
D_SB = 512
D_DIL = 512
D_IN = 3072
N_CHIP = 4
FFB = D_FF // N_CHIP
INB = D_IN // N_CHIP
OUTB = D_MODEL // N_CHIP
BLK = 128
LANES = 128
DILATIONS = (1, 4, 16)
ROPE_THETA = 10000.0
RMS_EPS = 1e-6
SCALE = HEAD_DIM ** -0.5
NEG = -1e30
DEAD = -104.0
ADAM_LR = 0.001
ADAM_B1 = 0.9
ADAM_B2 = 0.999
ADAM_EPS = 1e-08
ADAM_WD = 0.01
ADAM_STEP = 10
MESH = pl.DeviceIdType.MESH
F32 = jnp.float32
BF16 = jnp.bfloat16
TM = 512
TMF = 1024
ONCE = pl.Buffered(1)


def _params(vmem_mb):
    return pltpu.CompilerParams(vmem_limit_bytes=vmem_mb << 20)


def _dot(a, b):
    return jnp.dot(a, b, preferred_element_type=F32)


def _dot_nt(a, b):
    return lax.dot_general(a, b, (((1,), (1,)), ((), ())), preferred_element_type=F32)


def _dot_tn(a, b):
    return lax.dot_general(a, b, (((0,), (0,)), ((), ())), preferred_element_type=F32)


def _rms_fwd(x, g):
    r = lax.rsqrt(jnp.mean(x * x, axis=-1, keepdims=True) + RMS_EPS)
    xh = x * r
    return xh * g, xh, r


def _rms_bwd(dy, xh, r, g):
    dyg = dy * g
    dx = r * (dyg - xh * jnp.mean(dyg * xh, axis=-1, keepdims=True))
    return dx, jnp.sum(dy * xh, axis=0, keepdims=True)


def _split_bf16(a):
    hi = a.astype(BF16)
    return hi, (a - hi.astype(F32)).astype(BF16)


def _dot_split(a, b):
    hi, lo = _split_bf16(a)
    return _dot(hi, b) + _dot(lo, b)


def _ffn_weight_specs(f):
    return [pl.BlockSpec((None, None, D_MODEL, FFB), lambda i, j: (j, 2 * f, 0, 0)),
            pl.BlockSpec((None, None, D_MODEL, FFB), lambda i, j: (j, 2 * f + 1, 0, 0)),
            pl.BlockSpec((None, None, FFB, D_MODEL), lambda i, j: (j, f, 0, 0))]


def _ffn_saved(s):
    hidden = jax.ShapeDtypeStruct((N_CHIP, s, FFB), BF16)
    hid = pl.BlockSpec((None, TMF, FFB), lambda i, j: (j, i, 0))
    row = pl.BlockSpec((TMF, D_MODEL), lambda i, j: (i, 0), pipeline_mode=ONCE)
    return [row, hid, hid, hid], [jax.ShapeDtypeStruct((s, D_MODEL), BF16), hidden, hidden, hidden]


def _ffn_accumulate(h_ref, acc_scr, wg_ref, wu_ref, wd_ref, a_ref, b_ref, act_ref):
    h = h_ref[...]
    a = _dot(h, wg_ref[...])
    b = _dot(h, wu_ref[...])
    act = ((a * jax.nn.sigmoid(a)) * b).astype(BF16)
    a_ref[...] = a.astype(BF16)
    b_ref[...] = b.astype(BF16)
    act_ref[...] = act
    acc_scr[...] += _dot(act, wd_ref[...])


def _ffn1_fwd(x, g1, gmix, gu, wd):
    s = x.shape[0]
    row = pl.BlockSpec((TMF, D_MODEL), lambda i, j: (i, 0), pipeline_mode=ONCE)
    vec = pl.BlockSpec((1, D_MODEL), lambda i, j: (0, 0))
    saved_specs, saved_shapes = _ffn_saved(s)

    def body(x_ref, g_ref, gm_ref, wg_ref, wu_ref, wd_ref, x1_ref, hm_ref, h_ref, a_ref, b_ref, act_ref, acc_scr):
        j = pl.program_id(1)

        @pl.when(j == 0)
        def _():
            h, _, _ = _rms_fwd(x_ref[...], g_ref[...])
            h_ref[...] = h.astype(BF16)
            acc_scr[...] = jnp.zeros_like(acc_scr)

        _ffn_accumulate(h_ref, acc_scr, wg_ref, wu_ref, wd_ref, a_ref, b_ref, act_ref)

        @pl.when(j == N_CHIP - 1)
        def _():
            x1 = x_ref[...] + 0.5 * acc_scr[...]
            x1_ref[...] = x1
            hm, _, _ = _rms_fwd(x1, gm_ref[...])
            hm_ref[...] = hm.astype(BF16)

    x1, hm, *saved = pl.pallas_call(
        body, name="ffn1_fwd", grid=(s // TMF, N_CHIP),
        in_specs=[row, vec, vec] + _ffn_weight_specs(0),
        out_specs=[row, row] + saved_specs,
        out_shape=[jax.ShapeDtypeStruct((s, D_MODEL), F32), jax.ShapeDtypeStruct((s, D_MODEL), BF16)] + saved_shapes,
        scratch_shapes=[pltpu.VMEM((TMF, D_MODEL), F32)],
        compiler_params=_params(58),
    )(x, g1, gmix, gu, gu, wd)
    return x1, hm, saved


def _ffn2_fwd_loss(x2, g2, gf, target, gu, wd):
    s = x2.shape[0]
    row = pl.BlockSpec((TMF, D_MODEL), lambda i, j: (i, 0), pipeline_mode=ONCE)
    vec = pl.BlockSpec((1, D_MODEL), lambda i, j: (0, 0))
    stat = pl.BlockSpec((8, D_MODEL), lambda i, j: (0, 0))
    saved_specs, saved_shapes = _ffn_saved(s)

    def body(x_ref, g_ref, gf_ref, t_ref, wg_ref, wu_ref, wd_ref, dx_ref, st_ref, h_ref, a_ref, b_ref, act_ref, acc_scr):
        i, j = pl.program_id(0), pl.program_id(1)

        @pl.when((i == 0) & (j == 0))
        def _():
            st_ref[...] = jnp.zeros_like(st_ref)

        @pl.when(j == 0)
        def _():
            h, _, _ = _rms_fwd(x_ref[...], g_ref[...])
            h_ref[...] = h.astype(BF16)
            acc_scr[...] = jnp.zeros_like(acc_scr)

        _ffn_accumulate(h_ref, acc_scr, wg_ref, wu_ref, wd_ref, a_ref, b_ref, act_ref)

        @pl.when(j == N_CHIP - 1)
        def _():
            x3 = x_ref[...] + 0.5 * acc_scr[...]
            y, xh, r = _rms_fwd(x3, gf_ref[...])
            err = y - t_ref[...]
            dx, dg = _rms_bwd(err * (1.0 / D_MODEL), xh, r, gf_ref[...])
            dx_ref[...] = dx
            st_ref[0:1, :] += dg
            st_ref[1:2, :] += jnp.sum(err * err, axis=0, keepdims=True)

    dx3, st, *saved = pl.pallas_call(
        body, name="ffn2_fwd_loss", grid=(s // TMF, N_CHIP),
        in_specs=[row, vec, vec, row] + _ffn_weight_specs(1),
        out_specs=[row, stat] + saved_specs,
        out_shape=[jax.ShapeDtypeStruct((s, D_MODEL), F32), jax.ShapeDtypeStruct((8, D_MODEL), F32)] + saved_shapes,
        scratch_shapes=[pltpu.VMEM((TMF, D_MODEL), F32)],
        compiler_params=_params(58),
    )(x2, g2, gf, target, gu, gu, wd)
    return dx3, st, saved


def _ffn_bwd(xin, g, dy, saved, gu, wd, f):
    s = xin.shape[0]
    hb, gate, up, act = saved
    row = pl.BlockSpec((TMF, D_MODEL), lambda i, j: (i, 0), pipeline_mode=ONCE)
    vec = pl.BlockSpec((1, D_MODEL), lambda i, j: (0, 0))
    stat = pl.BlockSpec((8, D_MODEL), lambda i, j: (0, 0))
    hid = pl.BlockSpec((None, TMF, FFB), lambda i, j: (j, i, 0))

    def body(x_ref, g_ref, dy_ref, a_ref, b_ref, wg_ref, wu_ref, wd_ref, out_ref, dyh_ref, da_ref, db_ref, st_ref, dh_scr):
        i, j = pl.program_id(0), pl.program_id(1)

        @pl.when((i == 0) & (j == 0))
        def _():
            st_ref[...] = jnp.zeros_like(st_ref)

        @pl.when(j == 0)
        def _():
            dyh_ref[...] = (0.5 * dy_ref[...]).astype(BF16)
            dh_scr[...] = jnp.zeros_like(dh_scr)

        a = a_ref[...].astype(F32)
        b = b_ref[...].astype(F32)
        sg = jax.nn.sigmoid(a)
        dact = _dot_nt(dyh_ref[...], wd_ref[...])
        dab = (dact * b * (sg * (1.0 + a * (1.0 - sg)))).astype(BF16)
        dbb = (dact * (a * sg)).astype(BF16)
        da_ref[...] = dab
        db_ref[...] = dbb
        dh_scr[...] += _dot_nt(dab, wg_ref[...]) + _dot_nt(dbb, wu_ref[...])

        @pl.when(j == N_CHIP - 1)
        def _():
            _, xh, r = _rms_fwd(x_ref[...], g_ref[...])
            dx, dg = _rms_bwd(dh_scr[...], xh, r, g_ref[...])
            out_ref[...] = dy_ref[...] + dx
            st_ref[0:1, :] += dg

    hidden = jax.ShapeDtypeStruct((N_CHIP, s, FFB), BF16)
    dx, dyh, da, db, st = pl.pallas_call(
        body, name=f"ffn{f + 1}_bwd_dx", grid=(s // TMF, N_CHIP),
        in_specs=[row, vec, row, hid, hid] + _ffn_weight_specs(f),
        out_specs=[row, row, hid, hid, stat],
        out_shape=[jax.ShapeDtypeStruct((s, D_MODEL), F32), jax.ShapeDtypeStruct((s, D_MODEL), BF16),
                   hidden, hidden, jax.ShapeDtypeStruct((8, D_MODEL), F32)],
        scratch_shapes=[pltpu.VMEM((TMF, D_MODEL), F32)],
        compiler_params=_params(58),
    )(xin, g, dy, gate, up, gu, gu, wd)

    tok = pl.BlockSpec((TM, D_MODEL), lambda j, i: (i, 0))
    hid2 = pl.BlockSpec((None, TM, FFB), lambda j, i: (j, i, 0))
    gspecs = [pl.BlockSpec((None, D_MODEL, FFB), lambda j, i: (j, 0, 0)),
              pl.BlockSpec((None, D_MODEL, FFB), lambda j, i: (j, 0, 0)),
              pl.BlockSpec((None, FFB, D_MODEL), lambda j, i: (j, 0, 0))]

    def wbody(h_ref, dyh_ref, da_ref, db_ref, act_ref, dwg_ref, dwu_ref, dwd_ref):
        @pl.when(pl.program_id(1) == 0)
        def _():
            dwg_ref[...] = jnp.zeros_like(dwg_ref)
            dwu_ref[...] = jnp.zeros_like(dwu_ref)
            dwd_ref[...] = jnp.zeros_like(dwd_ref)

        hb = h_ref[...]
        dwg_ref[...] += _dot_tn(hb, da_ref[...])
        dwu_ref[...] += _dot_tn(hb, db_ref[...])
        dwd_ref[...] += _dot_tn(act_ref[...], dyh_ref[...])

    dwg, dwu, dwd = pl.pallas_call(
        wbody, name=f"ffn{f + 1}_bwd_dw", grid=(N_CHIP, s // TM),
        in_specs=[tok, tok, hid2, hid2, hid2], out_specs=gspecs,
        out_shape=[jax.ShapeDtypeStruct((N_CHIP, D_MODEL, FFB), F32),
                   jax.ShapeDtypeStruct((N_CHIP, D_MODEL, FFB), F32),
                   jax.ShapeDtypeStruct((N_CHIP, FFB, D_MODEL), F32)],
        compiler_params=_params(48),
    )(hb, dyh, da, db, act)
    return dx, dwg, dwu, dwd, st


def _rope_tables(s):
    half = HEAD_DIM // 2
    inv_freq = ROPE_THETA ** (-jnp.arange(half, dtype=F32) / half)
    ang = jnp.arange(s).astype(F32)[:, None] * inv_freq[None, :]
    cos, sin = jnp.cos(ang), jnp.sin(ang)
    cos2 = jnp.concatenate([cos, cos], axis=-1)
    sin2 = jnp.concatenate([-sin, sin], axis=-1)
    return jnp.tile(cos2, (1, LANES // HEAD_DIM)), jnp.tile(sin2, (1, LANES // HEAD_DIM))


def _rotate(t, cos, sin_signed):
    lane = lax.broadcasted_iota(jnp.int32, t.shape, 1)
    first = (lane % HEAD_DIM) < (HEAD_DIM // 2)
    partner = jnp.where(first, pltpu.roll(t, LANES - HEAD_DIM // 2, 1), pltpu.roll(t, HEAD_DIM // 2, 1))
    return t * cos + partner * sin_signed


def _proj_fwd(hm, win, cos, sin):
    s = hm.shape[0]
    n_sub = INB // LANES
    first_rot, last_rot = (3 * D_SB) // LANES, (3 * D_SB + 2 * D_DIL) // LANES

    def body(h_ref, w_ref, c_ref, s_ref, o_ref):
        j = pl.program_id(0)
        r = _dot(h_ref[...], w_ref[...])
        for c in range(n_sub):
            t = r[:, c * LANES:(c + 1) * LANES]
            col = j * n_sub + c
            rot = (col >= first_rot) & (col < last_rot)
            lanes = slice(c * LANES, (c + 1) * LANES)

            @pl.when(rot)
            def _():
                o_ref[:, lanes] = _rotate(t, c_ref[...], s_ref[...]).astype(BF16)

            @pl.when(jnp.logical_not(rot))
            def _():
                o_ref[:, lanes] = t.astype(BF16)

    return pl.pallas_call(
        body, name="proj_fwd", grid=(N_CHIP, s // TM),
        in_specs=[pl.BlockSpec((TM, D_MODEL), lambda j, i: (i, 0)),
                  pl.BlockSpec((None, D_MODEL, INB), lambda j, i: (j, 0, 0)),
                  pl.BlockSpec((TM, LANES), lambda j, i: (i, 0)),
                  pl.BlockSpec((TM, LANES), lambda j, i: (i, 0))],
        out_specs=pl.BlockSpec((TM, INB), lambda j, i: (i, j)),
        out_shape=jax.ShapeDtypeStruct((s, D_IN), BF16),
        compiler_params=_params(32),
    )(hm, win, cos, sin)


def _proj_bwd(x1, gmix, dqkv, win, dx2):
    s = x1.shape[0]
    row = pl.BlockSpec((TM, D_MODEL), lambda i, j: (i, 0))
    vec = pl.BlockSpec((1, D_MODEL), lambda i, j: (0, 0))

    def body(x_ref, g_ref, dq_ref, w_ref, dx2_ref, out_ref, dw_ref, st_ref, h_scr, dh_scr):
        i, j = pl.program_id(0), pl.program_id(1)

        @pl.when((i == 0) & (j == 0))
        def _():
            st_ref[...] = jnp.zeros_like(st_ref)
            dw_ref[...] = jnp.zeros_like(dw_ref)

        @pl.when(j == 0)
        def _():
            h, _, _ = _rms_fwd(x_ref[...], g_ref[...])
            h_scr[...] = h.astype(BF16)
            dh_scr[...] = jnp.zeros_like(dh_scr)

        dq = dq_ref[...]
        dw_ref[j] += _dot_tn(h_scr[...], dq)
        dh_scr[...] += _dot_nt(dq, w_ref[...])

        @pl.when(j == N_CHIP - 1)
        def _():
            _, xh, r = _rms_fwd(x_ref[...], g_ref[...])
            dx, dg = _rms_bwd(dh_scr[...], xh, r, g_ref[...])
            out_ref[...] = dx2_ref[...] + dx
            st_ref[0:1, :] += dg

    return pl.pallas_call(
        body, name="proj_bwd", grid=(s // TM, N_CHIP),
        in_specs=[row, vec, pl.BlockSpec((TM, INB), lambda i, j: (i, j)),
                  pl.BlockSpec((None, D_MODEL, INB), lambda i, j: (j, 0, 0)), row],
        out_specs=[row, pl.BlockSpec((N_CHIP, D_MODEL, INB), lambda i, j: (0, 0, 0)),
                   pl.BlockSpec((8, D_MODEL), lambda i, j: (0, 0))],
        out_shape=[jax.ShapeDtypeStruct((s, D_MODEL), F32),
                   jax.ShapeDtypeStruct((N_CHIP, D_MODEL, INB), F32),
                   jax.ShapeDtypeStruct((8, D_MODEL), F32)],
        scratch_shapes=[pltpu.VMEM((TM, D_MODEL), BF16), pltpu.VMEM((TM, D_MODEL), F32)],
        compiler_params=_params(56),
    )(x1, gmix, dqkv, win, dx2)


def _outproj_fwd(o_sb, o_dl, g_sb, g_dl, x1, wout):
    s = x1.shape[0]
    half = pl.BlockSpec((TM, D_SB), lambda i: (i, 0))
    row = pl.BlockSpec((TM, D_MODEL), lambda i: (i, 0))
    vec = pl.BlockSpec((1, D_SB), lambda i: (0, 0))

    def body(a_ref, b_ref, ga_ref, gb_ref, x_ref, w_ref, o_ref):
        ma, _, _ = _rms_fwd(a_ref[...], ga_ref[...])
        mb, _, _ = _rms_fwd(b_ref[...], gb_ref[...])
        o_ref[...] = (x_ref[...] + _dot(ma.astype(BF16), w_ref[0:D_SB, :])
                      + _dot(mb.astype(BF16), w_ref[D_SB:D_MODEL, :]))

    return pl.pallas_call(
        body, name="outproj_fwd", grid=(s // TM,),
        in_specs=[half, half, vec, vec, row, pl.BlockSpec((D_MODEL, D_MODEL), lambda i: (0, 0))],
        out_specs=row, out_shape=jax.ShapeDtypeStruct((s, D_MODEL), F32),
        compiler_params=_params(32),
    )(o_sb, o_dl, g_sb, g_dl, x1, wout)


def _outproj_bwd(dx2, o_sb, o_dl, g_sb, g_dl, wout):
    s = dx2.shape[0]
    half = pl.BlockSpec((TM, D_SB), lambda i: (i, 0))
    row = pl.BlockSpec((TM, D_MODEL), lambda i: (i, 0))
    vec = pl.BlockSpec((1, D_SB), lambda i: (0, 0))
    full = pl.BlockSpec((D_MODEL, D_MODEL), lambda i: (0, 0))

    def body(dy_ref, a_ref, b_ref, ga_ref, gb_ref, w_ref, da_ref, db_ref, dw_ref, st_ref):
        @pl.when(pl.program_id(0) == 0)
        def _():
            dw_ref[...] = jnp.zeros_like(dw_ref)
            st_ref[...] = jnp.zeros_like(st_ref)

        dy = dy_ref[...].astype(BF16)
        dm = _dot_nt(dy, w_ref[...])
        ma, xa, ra = _rms_fwd(a_ref[...], ga_ref[...])
        mb, xb, rb = _rms_fwd(b_ref[...], gb_ref[...])
        dw_ref[0:D_SB, :] += _dot_tn(ma.astype(BF16), dy)
        dw_ref[D_SB:D_MODEL, :] += _dot_tn(mb.astype(BF16), dy)
        da, dga = _rms_bwd(dm[:, 0:D_SB], xa, ra, ga_ref[...])
        db, dgb = _rms_bwd(dm[:, D_SB:D_MODEL], xb, rb, gb_ref[...])
        da_ref[...] = da
        db_ref[...] = db
        st_ref[0:1, :] += dga
        st_ref[1:2, :] += dgb

    return pl.pallas_call(
        body, name="outproj_bwd", grid=(s // TM,),
        in_specs=[row, half, half, vec, vec, full],
        out_specs=[half, half, full, pl.BlockSpec((8, D_SB), lambda i: (0, 0))],
        out_shape=[jax.ShapeDtypeStruct((s, D_SB), F32), jax.ShapeDtypeStruct((s, D_SB), F32),
                   jax.ShapeDtypeStruct((D_MODEL, D_MODEL), F32), jax.ShapeDtypeStruct((8, D_SB), F32)],
        compiler_params=_params(48),
    )(dx2, o_sb, o_dl, g_sb, g_dl, wout)


def _head_masks():
    lane = lax.broadcasted_iota(jnp.int32, (BLK, LANES), 1)
    return [lane < HEAD_DIM, lane >= HEAD_DIM]


def _keep(mask, a):
    return a * jnp.where(mask, 1.0, 0.0).astype(a.dtype)


def _suffix_matrices():
    r = lax.broadcasted_iota(jnp.int32, (BLK, BLK), 0)
    c = lax.broadcasted_iota(jnp.int32, (BLK, BLK), 1)
    ones = jnp.ones((BLK, BLK), BF16)
    excl = jnp.concatenate([(r > c).astype(BF16), ones], axis=1)
    incl = jnp.concatenate([(r >= c).astype(BF16), ones], axis=1)
    return excl, incl


def _blk(i):
    return pl.ds(pl.multiple_of(i * BLK, BLK), BLK)


def _alive(carry_m):
    return (jnp.max(carry_m) > DEAD).astype(jnp.int32)


def _more_keys(i, carry):
    return (carry[0] <= i) & (carry[1] > 0)


def _stack_heads(a):
    masks = _head_masks()
    return jnp.concatenate([_keep(masks[0], a), _keep(masks[1], a)], axis=0)


def _unstack_heads(a2):
    return jnp.where(_head_masks()[0], a2[:BLK], a2[BLK:])


def _head_rowsum(a):
    masks = _head_masks()
    return jnp.concatenate([jnp.sum(jnp.where(m, a, 0.0), axis=1, keepdims=True) for m in masks], axis=0)


SB_QB = 2
SB_ROWS = SB_QB * 2 * BLK


def _sb_rows(ref, i0, cast=None):
    tiles = [ref[_blk(i0 + t), :] for t in range(SB_QB)]
    return jnp.concatenate([_stack_heads(t if cast is None else t.astype(cast)) for t in tiles], axis=0)


def _sb_scores(q2, k, i, j, carry_m, u_excl):
    r = lax.broadcasted_iota(jnp.int32, (SB_ROWS, BLK), 0)
    row = (r & (BLK - 1)) + ((r >> 8) << 7)
    col = lax.broadcasted_iota(jnp.int32, (SB_ROWS, BLK), 1)
    valid = (j * BLK + col) < (i * BLK + row)
    z = _dot_nt(q2, k) * SCALE
    sp = jnp.maximum(z, 0.0) + jnp.log(1.0 + jnp.exp(-jnp.abs(z)))
    log_stay = jnp.where(valid, -sp, 0.0)
    log_beta = z - sp
    sums = _dot_split(log_stay, u_excl)
    later = carry_m + sums[:, :BLK]
    w = jnp.where(valid, jnp.exp(log_beta + later), 0.0)
    return valid, log_beta, w, carry_m + sums[:, BLK:]


def _sb_fwd(qkv):
    s = qkv.shape[0]
    nq = s // BLK
    pairs = D_SB // LANES
    col = lambda off: pl.BlockSpec((s, LANES), lambda p: (0, off + p))

    def body(q_ref, k_ref, v_ref, o_ref):
        u_excl, _ = _suffix_matrices()
        zero = jnp.zeros((SB_ROWS, LANES), F32)

        def q_block(ib, _):
            i = ib * SB_QB
            last = i + SB_QB - 1
            q2 = _sb_rows(q_ref, i)

            def k_block(carry):
                jj, _, carry_m, acc = carry
                j = last - jj
                _, _, w, carry_m = _sb_scores(q2, k_ref[_blk(j), :], i, j, carry_m, u_excl)
                return jj + 1, _alive(carry_m), carry_m, acc + _dot(w.astype(BF16), v_ref[_blk(j), :])

            _, _, _, acc = lax.while_loop(functools.partial(_more_keys, last), k_block,
                                          (jnp.int32(0), jnp.int32(1), zero, zero))
            for t in range(SB_QB):
                o_ref[_blk(i + t), :] = _unstack_heads(acc[2 * BLK * t:2 * BLK * (t + 1)])
            return 0

        lax.fori_loop(0, nq // SB_QB, q_block, 0)

    return pl.pallas_call(
        body, name="sb_fwd", grid=(pairs,),
        in_specs=[col(0), col(pairs), col(2 * pairs)],
        out_specs=pl.BlockSpec((s, LANES), lambda p: (0, p)),
        out_shape=jax.ShapeDtypeStruct((s, D_SB), F32),
        compiler_params=_params(48),
    )(qkv, qkv, qkv)


def _sb_bwd(qkv, o_sb, do_sb):
    s = qkv.shape[0]
    nq = s // BLK
    pairs = D_SB // LANES
    col = lambda off: pl.BlockSpec((s, LANES), lambda p: (0, off + p))
    own = pl.BlockSpec((s, LANES), lambda p: (0, p))

    def body(q_ref, k_ref, v_ref, o_ref, do_ref, dq_ref, dk_ref, dv_ref, dk_acc, dv_acc):
        u_excl, u_incl = _suffix_matrices()
        zero = jnp.zeros((SB_ROWS, LANES), F32)
        dk_acc[...] = jnp.zeros_like(dk_acc)
        dv_acc[...] = jnp.zeros_like(dv_acc)

        def q_block(ib, _):
            i = ib * SB_QB
            last = i + SB_QB - 1
            q2 = _sb_rows(q_ref, i)
            do2 = _sb_rows(do_ref, i, BF16)
            totals = [_head_rowsum(do_ref[_blk(i + t), :].astype(BF16).astype(F32) * o_ref[_blk(i + t), :])
                      for t in range(SB_QB)]
            total = jnp.broadcast_to(jnp.concatenate(totals, axis=0), (SB_ROWS, BLK))

            def k_block(carry):
                jj, _, carry_m, carry_g, dq = carry
                j = last - jj
                k = k_ref[_blk(j), :]
                valid, log_beta, w, carry_m = _sb_scores(q2, k, i, j, carry_m, u_excl)
                wb = w.astype(BF16)
                g = wb.astype(F32) * _dot_nt(do2, v_ref[_blk(j), :])
                sums = _dot_split(g, u_incl)
                before = total - (carry_g + sums[:, :BLK])
                dz = jnp.where(valid, g - jnp.exp(log_beta) * (g + before), 0.0)
                dzb = (dz * SCALE).astype(BF16)
                dk_acc[_blk(j), :] += _dot_tn(dzb, q2)
                dv_acc[_blk(j), :] += _dot_tn(wb, do2)
                return jj + 1, _alive(carry_m), carry_m, carry_g + sums[:, BLK:], dq + _dot(dzb, k)

            _, _, _, _, dq = lax.while_loop(functools.partial(_more_keys, last), k_block,
                                            (jnp.int32(0), jnp.int32(1), zero, zero, zero))
            for t in range(SB_QB):
                dq_ref[_blk(i + t), :] = _unstack_heads(dq[2 * BLK * t:2 * BLK * (t + 1)]).astype(BF16)
            return 0

        lax.fori_loop(0, nq // SB_QB, q_block, 0)
        dk_ref[...] = dk_acc[...].astype(BF16)
        dv_ref[...] = dv_acc[...].astype(BF16)

    return pl.pallas_call(
        body, name="sb_bwd", grid=(pairs,),
        in_specs=[col(0), col(pairs), col(2 * pairs), own, own],
        out_specs=[own, own, own],
        out_shape=[jax.ShapeDtypeStruct((s, D_SB), BF16)] * 3,
        scratch_shapes=[pltpu.VMEM((s, LANES), F32), pltpu.VMEM((s, LANES), F32)],
        compiler_params=_params(56),
    )(qkv, qkv, qkv, o_sb, do_sb)


def _band_masks(b):
    row = lax.broadcasted_iota(jnp.int32, (2 * BLK, BLK), 0) & (BLK - 1)
    col = lax.broadcasted_iota(jnp.int32, (2 * BLK, BLK), 1)
    return col <= row, (col - row) >= jnp.where(b > 0, 0, BLK)


def _dil_tiles(qf, kf, vf, d, t, nb):
    c, b = t // nb, t % nb
    start = c + d * BLK * b
    rows = pl.ds(start, BLK, stride=d)
    prev = pl.ds(jnp.where(b > 0, start - d * BLK, start), BLK, stride=d)
    bf = lambda ref, sl: ref[sl, :].astype(BF16)
    return b, rows, prev, _stack_heads(bf(qf, rows)), bf(kf, rows), bf(kf, prev), bf(vf, rows), bf(vf, prev)


def _lanes_of_heads(col2):
    return _unstack_heads(jnp.broadcast_to(col2, (2 * BLK, LANES)))


def _dilated_fwd(qkv):
    s = qkv.shape[0]
    pairs = D_DIL // LANES
    base = (3 * D_SB) // LANES
    col = lambda off: pl.BlockSpec((s, LANES), lambda p: (0, off + p))
    own = pl.BlockSpec((s, LANES), lambda p: (0, p))

    def body(q_ref, k_ref, v_ref, acc_ref, m_ref, qf, kf, vf, l_scr):
        qf[...] = q_ref[...].astype(F32)
        kf[...] = k_ref[...].astype(F32)
        vf[...] = v_ref[...].astype(F32)
        for d in DILATIONS:
            nb = s // (d * BLK)

            def block(t, _):
                b, rows, prev, q2, kc, kp, vc, vp = _dil_tiles(qf, kf, vf, d, t, nb)
                in_cur, in_prev = _band_masks(b)
                zc = jnp.where(in_cur, _dot_nt(q2, kc) * SCALE, NEG)
                zp = jnp.where(in_prev, _dot_nt(q2, kp) * SCALE, NEG)
                m = jnp.maximum(jnp.max(zc, axis=1, keepdims=True), jnp.max(zp, axis=1, keepdims=True))
                pc, pp = jnp.exp(zc - m), jnp.exp(zp - m)
                den = jnp.sum(pc, axis=1, keepdims=True) + jnp.sum(pp, axis=1, keepdims=True)
                acc = _unstack_heads(_dot(pc.astype(BF16), vc) + _dot(pp.astype(BF16), vp))
                m_t, l_t = _lanes_of_heads(m), _lanes_of_heads(den)
                if d == DILATIONS[0]:
                    m_ref[rows, :] = m_t
                    l_scr[rows, :] = l_t
                    acc_ref[rows, :] = acc
                else:
                    m_old = m_ref[rows, :]
                    m_new = jnp.maximum(m_old, m_t)
                    keep, add = jnp.exp(m_old - m_new), jnp.exp(m_t - m_new)
                    m_ref[rows, :] = m_new
                    l_scr[rows, :] = l_scr[rows, :] * keep + l_t * add
                    acc_ref[rows, :] = acc_ref[rows, :] * keep + acc * add
                return 0

            lax.fori_loop(0, s // BLK, block, 0, unroll=2)

        def finish(i, _):
            l = l_scr[_blk(i), :]
            acc_ref[_blk(i), :] = acc_ref[_blk(i), :] / l
            m_ref[_blk(i), :] = m_ref[_blk(i), :] + jnp.log(l)
            return 0

        lax.fori_loop(0, s // BLK, finish, 0)

    return pl.pallas_call(
        body, name="dilated_fwd", grid=(pairs,),
        in_specs=[col(base), col(base + pairs), col(base + 2 * pairs)],
        out_specs=[own, own],
        out_shape=[jax.ShapeDtypeStruct((s, D_DIL), F32)] * 2,
        scratch_shapes=[pltpu.VMEM((s, LANES), F32)] * 4,
        compiler_params=_params(56),
    )(qkv, qkv, qkv)


def _dilated_bwd(qkv, out, lse, dout):
    s = qkv.shape[0]
    pairs = D_DIL // LANES
    base = (3 * D_SB) // LANES
    once = pl.Buffered(1)
    col = lambda off: pl.BlockSpec((s, LANES), lambda p: (0, off + p), pipeline_mode=once)
    own = pl.BlockSpec((s, LANES), lambda p: (0, p), pipeline_mode=once)
    res = pl.BlockSpec((s, LANES), lambda p: (0, p))

    def body(q_ref, k_ref, v_ref, o_ref, l_ref, do_ref, dq_ref, dk_ref, dv_ref, qf, kf, vf):
        masks = _head_masks()
        qf[...] = q_ref[...].astype(F32)
        kf[...] = k_ref[...].astype(F32)
        vf[...] = v_ref[...].astype(F32)
        dq_ref[...] = jnp.zeros_like(dq_ref)
        dk_ref[...] = jnp.zeros_like(dk_ref)
        dv_ref[...] = jnp.zeros_like(dv_ref)
        for d in DILATIONS:
            nb = s // (d * BLK)

            def block(t, _):
                b, rows, prev, q2, kc, kp, vc, vp = _dil_tiles(qf, kf, vf, d, t, nb)
                in_cur, in_prev = _band_masks(b)
                do32 = do_ref[rows, :]
                do2 = _stack_heads(do32.astype(BF16))
                delta = _head_rowsum(do32 * o_ref[rows, :])
                lse_t = l_ref[rows, :]
                lse2 = jnp.concatenate([jnp.max(jnp.where(m, lse_t, NEG), axis=1, keepdims=True) for m in masks], axis=0)
                wc = jnp.exp(jnp.where(in_cur, _dot_nt(q2, kc) * SCALE, NEG) - lse2)
                wp = jnp.exp(jnp.where(in_prev, _dot_nt(q2, kp) * SCALE, NEG) - lse2)
                dzc = (wc * (_dot_nt(do2, vc) - delta) * SCALE).astype(BF16)
                dzp = (wp * (_dot_nt(do2, vp) - delta) * SCALE).astype(BF16)
                dq_ref[rows, :] += _unstack_heads(_dot(dzc, kc) + _dot(dzp, kp))
                dk_ref[rows, :] += _dot_tn(dzc, q2)
                dk_ref[prev, :] += _dot_tn(dzp, q2)
                dv_ref[rows, :] += _dot_tn(wc.astype(BF16), do2)
                dv_ref[prev, :] += _dot_tn(wp.astype(BF16), do2)
                return 0

            lax.fori_loop(0, s // BLK, block, 0, unroll=2)

    return pl.pallas_call(
        body, name="dilated_bwd", grid=(pairs,),
        in_specs=[col(base), col(base + pairs), col(base + 2 * pairs), own, own, own],
        out_specs=[res, res, res],
        out_shape=[jax.ShapeDtypeStruct((s, D_DIL), F32)] * 3,
        scratch_shapes=[pltpu.VMEM((s, LANES), F32)] * 3,
        compiler_params=_params(60),
    )(qkv, qkv, qkv, out, lse, dout)


def _dilated_finish(grads, cos, sin):
    s = grads[0].shape[0]
    spec = pl.BlockSpec((TM, D_DIL), lambda i: (i, 0))
    tab = pl.BlockSpec((TM, LANES), lambda i: (i, 0))

    def body(dq_ref, dk_ref, dv_ref, c_ref, s_ref, oq_ref, ok_ref, ov_ref):
        for src, dst, rotated in ((dq_ref, oq_ref, True), (dk_ref, ok_ref, True), (dv_ref, ov_ref, False)):
            for c in range(D_DIL // LANES):
                lanes = slice(c * LANES, (c + 1) * LANES)
                piece = src[:, lanes]
                dst[:, lanes] = (_rotate(piece, c_ref[...], -s_ref[...]) if rotated else piece).astype(BF16)

    return pl.pallas_call(
        body, name="dilated_finish", grid=(s // TM,),
        in_specs=[spec] * 3 + [tab, tab], out_specs=[spec] * 3,
        out_shape=[jax.ShapeDtypeStruct((s, D_DIL), BF16)] * 3,
        compiler_params=_params(32),
    )(*grads, cos, sin)


def _place():
    x, y, c = lax.axis_index("x"), lax.axis_index("y"), lax.axis_index("c")
    return x, y, c, 2 * x + y


def _chip(k, c):
    return (k >> 1, k & 1, c)


def _half(ref, h):
    n = ref.shape[0] // 2
    return ref.at[pl.ds(h * n, n)]


def _all_gather(shards):
    na = len(shards)
    any_spec = pl.BlockSpec(memory_space=pl.ANY)

    def body(*refs):
        ins, outs = refs[:na], refs[na:2 * na]
        send_sem, recv_sem, local_sem = refs[2 * na:]
        x, y, c, k = _place()
        sibling = (x, y, 1 - c)
        started = []
        for a in range(na):
            cp = pltpu.make_async_copy(ins[a], outs[a].at[k], local_sem.at[a])
            cp.start()
            started.append(cp)

        def copy(a, slot, src, dst, to):
            return pltpu.make_async_remote_copy(src_ref=src, dst_ref=dst, send_sem=send_sem.at[a * 6 + slot],
                                                recv_sem=recv_sem.at[a * 6 + slot], device_id=to, device_id_type=MESH)

        sends = []
        for a in range(na):
            for j in range(1, N_CHIP):
                cp = copy(a, j - 1, _half(ins[a], c), _half(outs[a].at[k], c), _chip(k ^ j, c))
                cp.start()
                sends.append(cp)
        for j in range(1, N_CHIP):
            for a in range(na):
                landed = _half(outs[a].at[k ^ j], c)
                copy(a, j - 1, landed, landed, sibling).wait_recv()
                cp = copy(a, 2 + j, landed, landed, sibling)
                cp.start()
                sends.append(cp)
        for j in range(1, N_CHIP):
            for a in range(na):
                passed = _half(outs[a].at[k ^ j], 1 - c)
                copy(a, 2 + j, passed, passed, sibling).wait_recv()
        for cp in sends:
            cp.wait_send()
        for cp in started:
            cp.wait()

    return pl.pallas_call(
        body, name="weights_all_gather",
        in_specs=[any_spec] * na, out_specs=[any_spec] * na,
        out_shape=[jax.ShapeDtypeStruct((N_CHIP,) + a.shape, a.dtype) for a in shards],
        scratch_shapes=[pltpu.SemaphoreType.DMA((6 * na,)), pltpu.SemaphoreType.DMA((6 * na,)),
                        pltpu.SemaphoreType.DMA((na,))],
    )(*shards)


def _reduce_scatter(g, core, name):
    n, r, c = g.shape
    hr = r // 2
    once = pl.Buffered(1)
    in_specs = [pl.BlockSpec((n, hr, c), lambda i, core_ref: (0, core_ref[0], 0), pipeline_mode=once),
                pl.BlockSpec((n, hr, c), lambda i, core_ref: (0, 1 - core_ref[0], 0), pipeline_mode=once)]

    def body(core_ref, mine_ref, other_ref, out_ref, from_core, sums, sums_bf, from_chips, done, from_core2, send_sem, recv_sem):
        x, y, cc, k = _place()
        sibling = (x, y, 1 - cc)

        def copy(slot, src, dst, to):
            return pltpu.make_async_remote_copy(src_ref=src, dst_ref=dst, send_sem=send_sem.at[slot],
                                                recv_sem=recv_sem.at[slot], device_id=to, device_id_type=MESH)

        first = copy(0, other_ref, from_core, sibling)
        first.start()
        first.wait()
        total = mine_ref[...] + from_core[...]
        sums[...] = total
        sums_bf[...] = total.astype(BF16)
        sends = [copy(j, sums_bf.at[k ^ j], from_chips.at[j - 1], _chip(k ^ j, cc)) for j in range(1, N_CHIP)]
        for cp in sends:
            cp.start()
        for cp in sends:
            cp.wait()
        red = sums[k]
        for j in range(1, N_CHIP):
            red = red + from_chips[j - 1].astype(F32)
        done[...] = red
        last = copy(N_CHIP, done, from_core2, sibling)
        last.start()
        last.wait()
        row0 = pl.multiple_of(cc * hr, 8)
        row1 = pl.multiple_of((1 - cc) * hr, 8)
        out_ref[pl.ds(row0, hr), :] = red
        out_ref[pl.ds(row1, hr), :] = from_core2[...]

    grid_spec = pltpu.PrefetchScalarGridSpec(
        num_scalar_prefetch=1, grid=(1,), in_specs=in_specs,
        out_specs=pl.BlockSpec((r, c), lambda i, core_ref: (0, 0)),
        scratch_shapes=[pltpu.VMEM((n, hr, c), F32), pltpu.VMEM((n, hr, c), F32), pltpu.VMEM((n, hr, c), BF16),
                        pltpu.VMEM((N_CHIP - 1, hr, c), BF16), pltpu.VMEM((hr, c), F32), pltpu.VMEM((hr, c), F32),
                        pltpu.SemaphoreType.DMA((N_CHIP + 1,)), pltpu.SemaphoreType.DMA((N_CHIP + 1,))])
    return pl.pallas_call(
        body, name=name, grid_spec=grid_spec, out_shape=jax.ShapeDtypeStruct((r, c), F32),
        compiler_params=_params(56),
    )(core, g, g)


def _elementwise(fn, name, ins, n_out, rows):
    total, cols = ins[0].shape
    spec = pl.BlockSpec((rows, cols), lambda i: (i, 0))

    def body(*refs):
        res = fn(*[r[...] for r in refs[:len(ins)]])
        for o, v in zip(refs[len(ins):], res):
            o[...] = v

    return pl.pallas_call(
        body, name=name, grid=(total // rows,),
        in_specs=[spec] * len(ins), out_specs=[spec] * n_out,
        out_shape=[jax.ShapeDtypeStruct((total, cols), F32)] * n_out,
        compiler_params=_params(48),
    )(*ins)


def _adamw(w, g, m, v):
    m = ADAM_B1 * m + (1.0 - ADAM_B1) * g
    v = ADAM_B2 * v + (1.0 - ADAM_B2) * (g * g)
    m_hat = m / (1.0 - ADAM_B1 ** ADAM_STEP)
    v_hat = v / (1.0 - ADAM_B2 ** ADAM_STEP)
    delta = -ADAM_LR * (m_hat / (jnp.sqrt(v_hat) + ADAM_EPS) + ADAM_WD * w)
    return delta, m, v


def _reduce_and_update(grads, weights, moms, vels):
    core = lax.axis_index("c").astype(jnp.int32).reshape(1)
    full = [_reduce_scatter(g, core, f"grads_reduce_scatter_{a}") for a, g in enumerate(grads)]
    out = []
    for a, (g, w, m, v) in enumerate(zip(full, weights, moms, vels)):
        rows = g.shape[0] // 2
        out.append((g,) + tuple(_elementwise(lambda gg, ww, mm, vv: _adamw(ww, gg, mm, vv), f"adamw_{a}", [g, w, m, v], 3, rows)))
    return out


def _reduce_vectors(part, w, m, v):
    n_dev = 8

    def body(p_ref, w_ref, m_ref, v_ref, g_ref, d_ref, nm_ref, nv_ref, buf, send_sem, recv_sem):
        x, y, c, _ = _place()
        me = 4 * x + 2 * y + c
        buf[me] = p_ref[...]
        sends = []
        for off in range(1, n_dev):
            peer = me ^ off
            cp = pltpu.make_async_remote_copy(src_ref=p_ref, dst_ref=buf.at[me], send_sem=send_sem.at[off - 1],
                                              recv_sem=recv_sem.at[off - 1], device_id=(peer >> 2, (peer >> 1) & 1, peer & 1),
                                              device_id_type=MESH)
            cp.start()
            sends.append(cp)
        for off in range(1, n_dev):
            peer = me ^ off
            pltpu.make_async_remote_copy(src_ref=p_ref, dst_ref=buf.at[peer], send_sem=send_sem.at[off - 1],
                                         recv_sem=recv_sem.at[off - 1], device_id=(peer >> 2, (peer >> 1) & 1, peer & 1),
                                         device_id_type=MESH).wait_recv()
        for cp in sends:
            cp.wait_send()
        g = buf[0]
        for d in range(1, n_dev):
            g = g + buf[d]
        g_ref[...] = g
        delta, nm, nv = _adamw(w_ref[...], g, m_ref[...], v_ref[...])
        d_ref[...] = delta
        nm_ref[...] = nm
        nv_ref[...] = nv

    vm = pl.BlockSpec(memory_space=pltpu.VMEM)
    return pl.pallas_call(
        body, name="gains_all_reduce",
        in_specs=[vm] * 4, out_specs=[vm] * 4,
        out_shape=[jax.ShapeDtypeStruct(part.shape, F32)] * 4,
        scratch_shapes=[pltpu.VMEM((n_dev,) + part.shape, F32), pltpu.SemaphoreType.DMA((n_dev - 1,)),
                        pltpu.SemaphoreType.DMA((n_dev - 1,))],
    )(part, w, m, v)


def _pad_row(a):
    a = a.reshape(1, -1)
    return jnp.pad(a, ((0, 0), (0, D_MODEL - a.shape[1])))


def kernel(x, ffn1_norm, ffn1_w_gate, ffn1_w_up, ffn1_w_down, mix_norm, w_in, sb_out_norm, dil_out_norm, w_out, ffn2_norm, ffn2_w_gate, ffn2_w_up, ffn2_w_down, final_norm, loss_target, m_ffn1_norm, m_ffn1_w_gate, m_ffn1_w_up, m_ffn1_w_down, m_mix_norm, m_w_in, m_sb_out_norm, m_dil_out_norm, m_w_out, m_ffn2_norm, m_ffn2_w_gate, m_ffn2_w_up, m_ffn2_w_down, m_final_norm, v_ffn1_norm, v_ffn1_w_gate, v_ffn1_w_up, v_ffn1_w_down, v_mix_norm, v_w_in, v_sb_out_norm, v_dil_out_norm, v_w_out, v_ffn2_norm, v_ffn2_w_gate, v_ffn2_w_up, v_ffn2_w_down, v_final_norm):
    x = x[0]
    target = loss_target[0]
    s = x.shape[0]
    gf = final_norm.reshape(1, D_MODEL)
    cos, sin = _rope_tables(s)

    gu_shard = jnp.stack([ffn1_w_gate[0], ffn1_w_up[0], ffn2_w_gate[0], ffn2_w_up[0]]).astype(BF16)
    wd_shard = jnp.stack([ffn1_w_down[0], ffn2_w_down[0]]).astype(BF16)
    gu, wd, win, wout = _all_gather([gu_shard, wd_shard, w_in[0].astype(BF16), w_out[0].astype(BF16)])
    wout = wout.reshape(D_MODEL, D_MODEL)

    x1, hm, saved1 = _ffn1_fwd(x, ffn1_norm, mix_norm, gu, wd)
    qkv = _proj_fwd(hm, win, cos, sin)
    o_sb = _sb_fwd(qkv)
    o_dl, lse = _dilated_fwd(qkv)
    x2 = _outproj_fwd(o_sb, o_dl, sb_out_norm, dil_out_norm, x1, wout)
    dx3, st_final, saved2 = _ffn2_fwd_loss(x2, ffn2_norm, gf, target, gu, wd)

    dx2, dwg2, dwu2, dwd2, st_ffn2 = _ffn_bwd(x2, ffn2_norm, dx3, saved2, gu, wd, 1)
    do_sb, do_dl, dwout, st_out = _outproj_bwd(dx2, o_sb, o_dl, sb_out_norm, dil_out_norm, wout)
    dq_sb, dk_sb, dv_sb = _sb_bwd(qkv, o_sb, do_sb)
    dq_dl, dk_dl, dv_dl = _dilated_finish(_dilated_bwd(qkv, o_dl, lse, do_dl), cos, sin)
    dqkv = jnp.concatenate([dq_sb, dk_sb, dv_sb, dq_dl, dk_dl, dv_dl], axis=1)
    dx1, dwin, st_mix = _proj_bwd(x1, mix_norm, dqkv, win, dx2)
    grad_x, dwg1, dwu1, dwd1, st_ffn1 = _ffn_bwd(x, ffn1_norm, dx1, saved1, gu, wd, 0)

    names = ["ffn1_w_gate", "ffn1_w_up", "ffn1_w_down", "w_in", "w_out", "ffn2_w_gate", "ffn2_w_up", "ffn2_w_down"]
    grads = [dwg1, dwu1, dwd1, dwin, dwout.reshape(N_CHIP, OUTB, D_MODEL), dwg2, dwu2, dwd2]
    weights = [ffn1_w_gate[0], ffn1_w_up[0], ffn1_w_down[0], w_in[0], w_out[0], ffn2_w_gate[0], ffn2_w_up[0], ffn2_w_down[0]]
    moms = [m_ffn1_w_gate[0], m_ffn1_w_up[0], m_ffn1_w_down[0], m_w_in[0], m_w_out[0], m_ffn2_w_gate[0], m_ffn2_w_up[0], m_ffn2_w_down[0]]
    vels = [v_ffn1_w_gate[0], v_ffn1_w_up[0], v_ffn1_w_down[0], v_w_in[0], v_w_out[0], v_ffn2_w_gate[0], v_ffn2_w_up[0], v_ffn2_w_down[0]]
    mats = {n: tuple(t[None] for t in r) for n, r in zip(names, _reduce_and_update(grads, weights, moms, vels))}

    vec_names = ["ffn1_norm", "mix_norm", "sb_out_norm", "dil_out_norm", "ffn2_norm", "final_norm"]
    part = jnp.concatenate([st_ffn1[0:1], st_mix[0:1], _pad_row(st_out[0]), _pad_row(st_out[1]), st_ffn2[0:1],
                            st_final[0:1], st_final[1:2], jnp.zeros((1, D_MODEL), F32)], axis=0)
    pack = lambda arrs: jnp.concatenate([_pad_row(a) for a in arrs] + [jnp.zeros((2, D_MODEL), F32)], axis=0)
    g_vec, d_vec, m_vec, v_vec = _reduce_vectors(
        part,
        pack([ffn1_norm, mix_norm, sb_out_norm, dil_out_norm, ffn2_norm, final_norm]),
        pack([m_ffn1_norm, m_mix_norm, m_sb_out_norm, m_dil_out_norm, m_ffn2_norm, m_final_norm]),
        pack([v_ffn1_norm, v_mix_norm, v_sb_out_norm, v_dil_out_norm, v_ffn2_norm, v_final_norm]))
    like = {"ffn1_norm": ffn1_norm, "mix_norm": mix_norm, "sb_out_norm": sb_out_norm, "dil_out_norm": dil_out_norm,
            "ffn2_norm": ffn2_norm, "final_norm": final_norm}
    vecs = {n: tuple(t[i, :like[n].size].reshape(like[n].shape) for t in (g_vec, d_vec, m_vec, v_vec))
            for i, n in enumerate(vec_names)}
    loss = 0.5 * jnp.sum(g_vec[6]) / D_MODEL

    order = ["ffn1_norm", "ffn1_w_gate", "ffn1_w_up", "ffn1_w_down", "mix_norm", "w_in", "sb_out_norm", "dil_out_norm",
             "w_out", "ffn2_norm", "ffn2_w_gate", "ffn2_w_up", "ffn2_w_down", "final_norm"]
    both = {**mats, **vecs}
    return (loss, grad_x[None], *[both[n][0] for n in order], *[both[n][1] for n in order],
            *[both[n][2] for n in order], *[both[n][3] for n in order])
```

```python
import functools

import jax
import jax.numpy as jnp
from jax import lax
from jax.experimental import pallas as pl
from jax.experimental.pallas import tpu as pltpu

D_MODEL = 1024
D_FF = 2816
HEAD_DIM = 64
D_SB = 512
D_DIL = 512
D_IN = 3072
N_CHIP = 4
FFB = D_FF // N_CHIP
INB = D_IN // N_CHIP
OUTB = D_MODEL // N_CHIP
BLK = 128
LANES = 128
DILATIONS = (1, 4, 16)
ROPE_THETA = 10000.0
RMS_EPS = 1e-6
SCALE = HEAD_DIM ** -0.5
NEG = -1e30
DEAD = -104.0
ADAM_LR = 0.001
ADAM_B1 = 0.9
ADAM_B2 = 0.999
ADAM_EPS = 1e-08
ADAM_WD = 0.01
ADAM_STEP = 10
MESH = pl.DeviceIdType.MESH
F32 = jnp.float32
BF16 = jnp.bfloat16
TM = 512


def _params(vmem_mb):
    return pltpu.CompilerParams(vmem_limit_bytes=vmem_mb << 20)


def _dot(a, b):
    return jnp.dot(a, b, preferred_element_type=F32)


def _dot_nt(a, b):
    return lax.dot_general(a, b, (((1,), (1,)), ((), ())), preferred_element_type=F32)


def _dot_tn(a, b):
    return lax.dot_general(a, b, (((0,), (0,)), ((), ())), preferred_element_type=F32)


def _rms_fwd(x, g):
    r = lax.rsqrt(jnp.mean(x * x, axis=-1, keepdims=True) + RMS_EPS)
    xh = x * r
    return xh * g, xh, r


def _rms_bwd(dy, xh, r, g):
    dyg = dy * g
    dx = r * (dyg - xh * jnp.mean(dyg * xh, axis=-1, keepdims=True))
    return dx, jnp.sum(dy * xh, axis=0, keepdims=True)


def _split_bf16(a):
    hi = a.astype(BF16)
    return hi, (a - hi.astype(F32)).astype(BF16)


def _dot_split(a, b):
    hi, lo = _split_bf16(a)
    return _dot(hi, b) + _dot(lo, b)


def _ffn_weight_specs(f):
    return [pl.BlockSpec((None, None, D_MODEL, FFB), lambda i, j: (j, 2 * f, 0, 0)),
            pl.BlockSpec((None, None, D_MODEL, FFB), lambda i, j: (j, 2 * f + 1, 0, 0)),
            pl.BlockSpec((None, None, FFB, D_MODEL), lambda i, j: (j, f, 0, 0))]


def _ffn_saved(s):
    hidden = jax.ShapeDtypeStruct((N_CHIP, s, FFB), BF16)
    hid = pl.BlockSpec((None, TM, FFB), lambda i, j: (j, i, 0))
    row = pl.BlockSpec((TM, D_MODEL), lambda i, j: (i, 0))
    return [row, hid, hid, hid], [jax.ShapeDtypeStruct((s, D_MODEL), BF16), hidden, hidden, hidden]


def _ffn_accumulate(h_ref, acc_scr, wg_ref, wu_ref, wd_ref, a_ref, b_ref, act_ref):
    h = h_ref[...]
    a = _dot(h, wg_ref[...])
    b = _dot(h, wu_ref[...])
    act = ((a * jax.nn.sigmoid(a)) * b).astype(BF16)
    a_ref[...] = a.astype(BF16)
    b_ref[...] = b.astype(BF16)
    act_ref[...] = act
    acc_scr[...] += _dot(act, wd_ref[...])


def _ffn1_fwd(x, g1, gmix, gu, wd):
    s = x.shape[0]
    row = pl.BlockSpec((TM, D_MODEL), lambda i, j: (i, 0))
    vec = pl.BlockSpec((1, D_MODEL), lambda i, j: (0, 0))
    saved_specs, saved_shapes = _ffn_saved(s)

    def body(x_ref, g_ref, gm_ref, wg_ref, wu_ref, wd_ref, x1_ref, hm_ref, h_ref, a_ref, b_ref, act_ref, acc_scr):
        j = pl.program_id(1)

        @pl.when(j == 0)
        def _():
            h, _, _ = _rms_fwd(x_ref[...], g_ref[...])
            h_ref[...] = h.astype(BF16)
            acc_scr[...] = jnp.zeros_like(acc_scr)

        _ffn_accumulate(h_ref, acc_scr, wg_ref, wu_ref, wd_ref, a_ref, b_ref, act_ref)

        @pl.when(j == N_CHIP - 1)
        def _():
            x1 = x_ref[...] + 0.5 * acc_scr[...]
            x1_ref[...] = x1
            hm, _, _ = _rms_fwd(x1, gm_ref[...])
            hm_ref[...] = hm.astype(BF16)

    x1, hm, *saved = pl.pallas_call(
        body, name="ffn1_fwd", grid=(s // TM, N_CHIP),
        in_specs=[row, vec, vec] + _ffn_weight_specs(0),
        out_specs=[row, row] + saved_specs,
        out_shape=[jax.ShapeDtypeStruct((s, D_MODEL), F32), jax.ShapeDtypeStruct((s, D_MODEL), BF16)] + saved_shapes,
        scratch_shapes=[pltpu.VMEM((TM, D_MODEL), F32)],
        compiler_params=_params(56),
    )(x, g1, gmix, gu, gu, wd)
    return x1, hm, saved


def _ffn2_fwd_loss(x2, g2, gf, target, gu, wd):
    s = x2.shape[0]
    row = pl.BlockSpec((TM, D_MODEL), lambda i, j: (i, 0))
    vec = pl.BlockSpec((1, D_MODEL), lambda i, j: (0, 0))
    stat = pl.BlockSpec((8, D_MODEL), lambda i, j: (0, 0))
    saved_specs, saved_shapes = _ffn_saved(s)

    def body(x_ref, g_ref, gf_ref, t_ref, wg_ref, wu_ref, wd_ref, dx_ref, st_ref, h_ref, a_ref, b_ref, act_ref, acc_scr):
        i, j = pl.program_id(0), pl.program_id(1)

        @pl.when((i == 0) & (j == 0))
        def _():
            st_ref[...] = jnp.zeros_like(st_ref)

        @pl.when(j == 0)
        def _():
            h, _, _ = _rms_fwd(x_ref[...], g_ref[...])
            h_ref[...] = h.astype(BF16)
            acc_scr[...] = jnp.zeros_like(acc_scr)

        _ffn_accumulate(h_ref, acc_scr, wg_ref, wu_ref, wd_ref, a_ref, b_ref, act_ref)

        @pl.when(j == N_CHIP - 1)
        def _():
            x3 = x_ref[...] + 0.5 * acc_scr[...]
            y, xh, r = _rms_fwd(x3, gf_ref[...])
            err = y - t_ref[...]
            dx, dg = _rms_bwd(err * (1.0 / D_MODEL), xh, r, gf_ref[...])
            dx_ref[...] = dx
            st_ref[0:1, :] += dg
            st_ref[1:2, :] += jnp.sum(err * err, axis=0, keepdims=True)

    dx3, st, *saved = pl.pallas_call(
        body, name="ffn2_fwd_loss", grid=(s // TM, N_CHIP),
        in_specs=[row, vec, vec, row] + _ffn_weight_specs(1),
        out_specs=[row, stat] + saved_specs,
        out_shape=[jax.ShapeDtypeStruct((s, D_MODEL), F32), jax.ShapeDtypeStruct((8, D_MODEL), F32)] + saved_shapes,
        scratch_shapes=[pltpu.VMEM((TM, D_MODEL), F32)],
        compiler_params=_params(56),
    )(x2, g2, gf, target, gu, gu, wd)
    return dx3, st, saved


def _ffn_bwd(xin, g, dy, saved, gu, wd, f):
    s = xin.shape[0]
    hb, gate, up, act = saved
    row = pl.BlockSpec((TM, D_MODEL), lambda i, j: (i, 0))
    vec = pl.BlockSpec((1, D_MODEL), lambda i, j: (0, 0))
    stat = pl.BlockSpec((8, D_MODEL), lambda i, j: (0, 0))
    hid = pl.BlockSpec((None, TM, FFB), lambda i, j: (j, i, 0))

    def body(x_ref, g_ref, dy_ref, a_ref, b_ref, wg_ref, wu_ref, wd_ref, out_ref, dyh_ref, da_ref, db_ref, st_ref, dh_scr):
        i, j = pl.program_id(0), pl.program_id(1)

        @pl.when((i == 0) & (j == 0))
        def _():
            st_ref[...] = jnp.zeros_like(st_ref)

        @pl.when(j == 0)
        def _():
            dyh_ref[...] = (0.5 * dy_ref[...]).astype(BF16)
            dh_scr[...] = jnp.zeros_like(dh_scr)

        a = a_ref[...].astype(F32)
        b = b_ref[...].astype(F32)
        sg = jax.nn.sigmoid(a)
        dact = _dot_nt(dyh_ref[...], wd_ref[...])
        dab = (dact * b * (sg * (1.0 + a * (1.0 - sg)))).astype(BF16)
        dbb = (dact * (a * sg)).astype(BF16)
        da_ref[...] = dab
        db_ref[...] = dbb
        dh_scr[...] += _dot_nt(dab, wg_ref[...]) + _dot_nt(dbb, wu_ref[...])

        @pl.when(j == N_CHIP - 1)
        def _():
            _, xh, r = _rms_fwd(x_ref[...], g_ref[...])
            dx, dg = _rms_bwd(dh_scr[...], xh, r, g_ref[...])
            out_ref[...] = dy_ref[...] + dx
            st_ref[0:1, :] += dg

    hidden = jax.ShapeDtypeStruct((N_CHIP, s, FFB), BF16)
    dx, dyh, da, db, st = pl.pallas_call(
        body, name=f"ffn{f + 1}_bwd_dx", grid=(s // TM, N_CHIP),
        in_specs=[row, vec, row, hid, hid] + _ffn_weight_specs(f),
        out_specs=[row, row, hid, hid, stat],
        out_shape=[jax.ShapeDtypeStruct((s, D_MODEL), F32), jax.ShapeDtypeStruct((s, D_MODEL), BF16),
                   hidden, hidden, jax.ShapeDtypeStruct((8, D_MODEL), F32)],
        scratch_shapes=[pltpu.VMEM((TM, D_MODEL), F32)],
        compiler_params=_params(56),
    )(xin, g, dy, gate, up, gu, gu, wd)

    tok = pl.BlockSpec((TM, D_MODEL), lambda j, i: (i, 0))
    hid2 = pl.BlockSpec((None, TM, FFB), lambda j, i: (j, i, 0))
    gspecs = [pl.BlockSpec((None, D_MODEL, FFB), lambda j, i: (j, 0, 0)),
              pl.BlockSpec((None, D_MODEL, FFB), lambda j, i: (j, 0, 0)),
              pl.BlockSpec((None, FFB, D_MODEL), lambda j, i: (j, 0, 0))]

    def wbody(h_ref, dyh_ref, da_ref, db_ref, act_ref, dwg_ref, dwu_ref, dwd_ref):
        @pl.when(pl.program_id(1) == 0)
        def _():
            dwg_ref[...] = jnp.zeros_like(dwg_ref)
            dwu_ref[...] = jnp.zeros_like(dwu_ref)
            dwd_ref[...] = jnp.zeros_like(dwd_ref)

        hb = h_ref[...]
        dwg_ref[...] += _dot_tn(hb, da_ref[...])
        dwu_ref[...] += _dot_tn(hb, db_ref[...])
        dwd_ref[...] += _dot_tn(act_ref[...], dyh_ref[...])

    dwg, dwu, dwd = pl.pallas_call(
        wbody, name=f"ffn{f + 1}_bwd_dw", grid=(N_CHIP, s // TM),
        in_specs=[tok, tok, hid2, hid2, hid2], out_specs=gspecs,
        out_shape=[jax.ShapeDtypeStruct((N_CHIP, D_MODEL, FFB), F32),
                   jax.ShapeDtypeStruct((N_CHIP, D_MODEL, FFB), F32),
                   jax.ShapeDtypeStruct((N_CHIP, FFB, D_MODEL), F32)],
        compiler_params=_params(48),
    )(hb, dyh, da, db, act)
    return dx, dwg, dwu, dwd, st


def _rope_tables(s):
    half = HEAD_DIM // 2
    inv_freq = ROPE_THETA ** (-jnp.arange(half, dtype=F32) / half)
    ang = jnp.arange(s).astype(F32)[:, None] * inv_freq[None, :]
    cos, sin = jnp.cos(ang), jnp.sin(ang)
    cos2 = jnp.concatenate([cos, cos], axis=-1)
    sin2 = jnp.concatenate([-sin, sin], axis=-1)
    return jnp.tile(cos2, (1, LANES // HEAD_DIM)), jnp.tile(sin2, (1, LANES // HEAD_DIM))


def _rotate(t, cos, sin_signed):
    lane = lax.broadcasted_iota(jnp.int32, t.shape, 1)
    first = (lane % HEAD_DIM) < (HEAD_DIM // 2)
    partner = jnp.where(first, pltpu.roll(t, LANES - HEAD_DIM // 2, 1), pltpu.roll(t, HEAD_DIM // 2, 1))
    return t * cos + partner * sin_signed


def _proj_fwd(hm, win, cos, sin):
    s = hm.shape[0]
    n_sub = INB // LANES
    first_rot, last_rot = (3 * D_SB) // LANES, (3 * D_SB + 2 * D_DIL) // LANES

    def body(h_ref, w_ref, c_ref, s_ref, o_ref):
        j = pl.program_id(1)
        r = _dot(h_ref[...], w_ref[...])
        for c in range(n_sub):
            t = r[:, c * LANES:(c + 1) * LANES]
            col = j * n_sub + c
            rot = (col >= first_rot) & (col < last_rot)
            lanes = slice(c * LANES, (c + 1) * LANES)

            @pl.when(rot)
            def _():
                o_ref[:, lanes] = _rotate(t, c_ref[...], s_ref[...]).astype(BF16)

            @pl.when(jnp.logical_not(rot))
            def _():
                o_ref[:, lanes] = t.astype(BF16)

    return pl.pallas_call(
        body, name="proj_fwd", grid=(s // TM, N_CHIP),
        in_specs=[pl.BlockSpec((TM, D_MODEL), lambda i, j: (i, 0)),
                  pl.BlockSpec((None, D_MODEL, INB), lambda i, j: (j, 0, 0)),
                  pl.BlockSpec((TM, LANES), lambda i, j: (i, 0)),
                  pl.BlockSpec((TM, LANES), lambda i, j: (i, 0))],
        out_specs=pl.BlockSpec((TM, INB), lambda i, j: (i, j)),
        out_shape=jax.ShapeDtypeStruct((s, D_IN), BF16),
        compiler_params=_params(32),
    )(hm, win, cos, sin)


def _proj_bwd(x1, gmix, dqkv, win, dx2):
    s = x1.shape[0]
    row = pl.BlockSpec((TM, D_MODEL), lambda i, j: (i, 0))
    vec = pl.BlockSpec((1, D_MODEL), lambda i, j: (0, 0))

    def body(x_ref, g_ref, dq_ref, w_ref, dx2_ref, out_ref, dw_ref, st_ref, h_scr, dh_scr):
        i, j = pl.program_id(0), pl.program_id(1)

        @pl.when((i == 0) & (j == 0))
        def _():
            st_ref[...] = jnp.zeros_like(st_ref)
            dw_ref[...] = jnp.zeros_like(dw_ref)

        @pl.when(j == 0)
        def _():
            h, _, _ = _rms_fwd(x_ref[...], g_ref[...])
            h_scr[...] = h.astype(BF16)
            dh_scr[...] = jnp.zeros_like(dh_scr)

        dq = dq_ref[...]
        dw_ref[j] += _dot_tn(h_scr[...], dq)
        dh_scr[...] += _dot_nt(dq, w_ref[...])

        @pl.when(j == N_CHIP - 1)
        def _():
            _, xh, r = _rms_fwd(x_ref[...], g_ref[...])
            dx, dg = _rms_bwd(dh_scr[...], xh, r, g_ref[...])
            out_ref[...] = dx2_ref[...] + dx
            st_ref[0:1, :] += dg

    return pl.pallas_call(
        body, name="proj_bwd", grid=(s // TM, N_CHIP),
        in_specs=[row, vec, pl.BlockSpec((TM, INB), lambda i, j: (i, j)),
                  pl.BlockSpec((None, D_MODEL, INB), lambda i, j: (j, 0, 0)), row],
        out_specs=[row, pl.BlockSpec((N_CHIP, D_MODEL, INB), lambda i, j: (0, 0, 0)),
                   pl.BlockSpec((8, D_MODEL), lambda i, j: (0, 0))],
        out_shape=[jax.ShapeDtypeStruct((s, D_MODEL), F32),
                   jax.ShapeDtypeStruct((N_CHIP, D_MODEL, INB), F32),
                   jax.ShapeDtypeStruct((8, D_MODEL), F32)],
        scratch_shapes=[pltpu.VMEM((TM, D_MODEL), BF16), pltpu.VMEM((TM, D_MODEL), F32)],
        compiler_params=_params(56),
    )(x1, gmix, dqkv, win, dx2)


def _outproj_fwd(o_sb, o_dl, g_sb, g_dl, x1, wout):
    s = x1.shape[0]
    half = pl.BlockSpec((TM, D_SB), lambda i: (i, 0))
    row = pl.BlockSpec((TM, D_MODEL), lambda i: (i, 0))
    vec = pl.BlockSpec((1, D_SB), lambda i: (0, 0))

    def body(a_ref, b_ref, ga_ref, gb_ref, x_ref, w_ref, o_ref):
        ma, _, _ = _rms_fwd(a_ref[...], ga_ref[...])
        mb, _, _ = _rms_fwd(b_ref[...], gb_ref[...])
        o_ref[...] = (x_ref[...] + _dot(ma.astype(BF16), w_ref[0:D_SB, :])
                      + _dot(mb.astype(BF16), w_ref[D_SB:D_MODEL, :]))

    return pl.pallas_call(
        body, name="outproj_fwd", grid=(s // TM,),
        in_specs=[half, half, vec, vec, row, pl.BlockSpec((D_MODEL, D_MODEL), lambda i: (0, 0))],
        out_specs=row, out_shape=jax.ShapeDtypeStruct((s, D_MODEL), F32),
        compiler_params=_params(32),
    )(o_sb, o_dl, g_sb, g_dl, x1, wout)


def _outproj_bwd(dx2, o_sb, o_dl, g_sb, g_dl, wout):
    s = dx2.shape[0]
    half = pl.BlockSpec((TM, D_SB), lambda i: (i, 0))
    row = pl.BlockSpec((TM, D_MODEL), lambda i: (i, 0))
    vec = pl.BlockSpec((1, D_SB), lambda i: (0, 0))
    full = pl.BlockSpec((D_MODEL, D_MODEL), lambda i: (0, 0))

    def body(dy_ref, a_ref, b_ref, ga_ref, gb_ref, w_ref, da_ref, db_ref, dl_ref, dw_ref, st_ref):
        @pl.when(pl.program_id(0) == 0)
        def _():
            dw_ref[...] = jnp.zeros_like(dw_ref)
            st_ref[...] = jnp.zeros_like(st_ref)

        dy = dy_ref[...].astype(BF16)
        dm = _dot_nt(dy, w_ref[...])
        ma, xa, ra = _rms_fwd(a_ref[...], ga_ref[...])
        mb, xb, rb = _rms_fwd(b_ref[...], gb_ref[...])
        dw_ref[0:D_SB, :] += _dot_tn(ma.astype(BF16), dy)
        dw_ref[D_SB:D_MODEL, :] += _dot_tn(mb.astype(BF16), dy)
        da, dga = _rms_bwd(dm[:, 0:D_SB], xa, ra, ga_ref[...])
        db, dgb = _rms_bwd(dm[:, D_SB:D_MODEL], xb, rb, gb_ref[...])
        da_ref[...] = da
        db_ref[...] = db
        r = lax.broadcasted_iota(jnp.int32, (LANES, LANES), 0) >= HEAD_DIM
        c = lax.broadcasted_iota(jnp.int32, (LANES, LANES), 1) >= HEAD_DIM
        same_head = jnp.where(r == c, 1.0, 0.0).astype(BF16)
        prod = db * b_ref[...]
        for k in range(D_DIL // LANES):
            lanes = slice(k * LANES, (k + 1) * LANES)
            dl_ref[:, lanes] = _dot_split(prod[:, lanes], same_head)
        st_ref[0:1, :] += dga
        st_ref[1:2, :] += dgb

    return pl.pallas_call(
        body, name="outproj_bwd", grid=(s // TM,),
        in_specs=[row, half, half, vec, vec, full],
        out_specs=[half, half, half, full, pl.BlockSpec((8, D_SB), lambda i: (0, 0))],
        out_shape=[jax.ShapeDtypeStruct((s, D_SB), F32), jax.ShapeDtypeStruct((s, D_SB), F32),
                   jax.ShapeDtypeStruct((s, D_DIL), F32),
                   jax.ShapeDtypeStruct((D_MODEL, D_MODEL), F32), jax.ShapeDtypeStruct((8, D_SB), F32)],
        compiler_params=_params(48),
    )(dx2, o_sb, o_dl, g_sb, g_dl, wout)


def _head_masks():
    lane = lax.broadcasted_iota(jnp.int32, (BLK, LANES), 1)
    return [lane < HEAD_DIM, lane >= HEAD_DIM]


def _keep(mask, a):
    return a * jnp.where(mask, 1.0, 0.0).astype(a.dtype)


def _suffix_matrices():
    r = lax.broadcasted_iota(jnp.int32, (BLK, BLK), 0)
    c = lax.broadcasted_iota(jnp.int32, (BLK, BLK), 1)
    ones = jnp.ones((BLK, BLK), BF16)
    excl = jnp.concatenate([(r > c).astype(BF16), ones], axis=1)
    incl = jnp.concatenate([(r >= c).astype(BF16), ones], axis=1)
    return excl, incl


def _blk(i):
    return pl.ds(pl.multiple_of(i * BLK, BLK), BLK)


def _alive(carry_m):
    return (jnp.max(carry_m) > DEAD).astype(jnp.int32)


def _more_keys(i, carry):
    return (carry[0] <= i) & (carry[1] > 0)


def _stack_heads(a):
    masks = _head_masks()
    return jnp.concatenate([_keep(masks[0], a), _keep(masks[1], a)], axis=0)


def _unstack_heads(a2):
    return jnp.where(_head_masks()[0], a2[:BLK], a2[BLK:])


def _head_rowsum(a):
    masks = _head_masks()
    return jnp.concatenate([jnp.sum(jnp.where(m, a, 0.0), axis=1, keepdims=True) for m in masks], axis=0)


SB_QB = 2
SB_ROWS = SB_QB * 2 * BLK


def _sb_rows(ref, i0, cast=None):
    tiles = [ref[_blk(i0 + t), :] for t in range(SB_QB)]
    return jnp.concatenate([_stack_heads(t if cast is None else t.astype(cast)) for t in tiles], axis=0)


def _sb_scores(q2, k, i, j, carry_m, u_excl):
    r = lax.broadcasted_iota(jnp.int32, (SB_ROWS, BLK), 0)
    row = (r & (BLK - 1)) + ((r >> 8) << 7)
    col = lax.broadcasted_iota(jnp.int32, (SB_ROWS, BLK), 1)
    valid = (j * BLK + col) < (i * BLK + row)
    z = _dot_nt(q2, k) * SCALE
    sp = jnp.maximum(z, 0.0) + jnp.log(1.0 + jnp.exp(-jnp.abs(z)))
    log_stay = jnp.where(valid, -sp, 0.0)
    log_beta = z - sp
    sums = _dot_split(log_stay, u_excl)
    later = carry_m + sums[:, :BLK]
    w = jnp.where(valid, jnp.exp(log_beta + later), 0.0)
    return valid, log_beta, w, carry_m + sums[:, BLK:]


def _sb_fwd(qkv):
    s = qkv.shape[0]
    nq = s // BLK
    pairs = D_SB // LANES
    col = lambda off: pl.BlockSpec((s, LANES), lambda p: (0, off + p))

    def body(q_ref, k_ref, v_ref, o_ref):
        u_excl, _ = _suffix_matrices()
        zero = jnp.zeros((SB_ROWS, LANES), F32)

        def q_block(ib, _):
            i = ib * SB_QB
            last = i + SB_QB - 1
            q2 = _sb_rows(q_ref, i)

            def k_block(carry):
                jj, _, carry_m, acc = carry
                j = last - jj
                _, _, w, carry_m = _sb_scores(q2, k_ref[_blk(j), :], i, j, carry_m, u_excl)
                return jj + 1, _alive(carry_m), carry_m, acc + _dot(w.astype(BF16), v_ref[_blk(j), :])

            _, _, _, acc = lax.while_loop(functools.partial(_more_keys, last), k_block,
                                          (jnp.int32(0), jnp.int32(1), zero, zero))
            for t in range(SB_QB):
                o_ref[_blk(i + t), :] = _unstack_heads(acc[2 * BLK * t:2 * BLK * (t + 1)])
            return 0

        lax.fori_loop(0, nq // SB_QB, q_block, 0)

    return pl.pallas_call(
        body, name="sb_fwd", grid=(pairs,),
        in_specs=[col(0), col(pairs), col(2 * pairs)],
        out_specs=pl.BlockSpec((s, LANES), lambda p: (0, p)),
        out_shape=jax.ShapeDtypeStruct((s, D_SB), F32),
        compiler_params=_params(48),
    )(qkv, qkv, qkv)


def _sb_bwd(qkv, o_sb, do_sb):
    s = qkv.shape[0]
    nq = s // BLK
    pairs = D_SB // LANES
    col = lambda off: pl.BlockSpec((s, LANES), lambda p: (0, off + p))
    own = pl.BlockSpec((s, LANES), lambda p: (0, p))

    def body(q_ref, k_ref, v_ref, o_ref, do_ref, dq_ref, dk_ref, dv_ref, dk_acc, dv_acc):
        u_excl, u_incl = _suffix_matrices()
        zero = jnp.zeros((SB_ROWS, LANES), F32)
        dk_acc[...] = jnp.zeros_like(dk_acc)
        dv_acc[...] = jnp.zeros_like(dv_acc)

        def q_block(ib, _):
            i = ib * SB_QB
            last = i + SB_QB - 1
            q2 = _sb_rows(q_ref, i)
            do2 = _sb_rows(do_ref, i, BF16)
            totals = [_head_rowsum(do_ref[_blk(i + t), :].astype(BF16).astype(F32) * o_ref[_blk(i + t), :])
                      for t in range(SB_QB)]
            total = jnp.broadcast_to(jnp.concatenate(totals, axis=0), (SB_ROWS, BLK))

            def k_block(carry):
                jj, _, carry_m, carry_g, dq = carry
                j = last - jj
                k = k_ref[_blk(j), :]
                valid, log_beta, w, carry_m = _sb_scores(q2, k, i, j, carry_m, u_excl)
                wb = w.astype(BF16)
                g = wb.astype(F32) * _dot_nt(do2, v_ref[_blk(j), :])
                sums = _dot_split(g, u_incl)
                before = total - (carry_g + sums[:, :BLK])
                dz = jnp.where(valid, g - jnp.exp(log_beta) * (g + before), 0.0)
                dzb = (dz * SCALE).astype(BF16)
                dk_acc[_blk(j), :] += _dot_tn(dzb, q2)
                dv_acc[_blk(j), :] += _dot_tn(wb, do2)
                return jj + 1, _alive(carry_m), carry_m, carry_g + sums[:, BLK:], dq + _dot(dzb, k)

            _, _, _, _, dq = lax.while_loop(functools.partial(_more_keys, last), k_block,
                                            (jnp.int32(0), jnp.int32(1), zero, zero, zero))
            for t in range(SB_QB):
                dq_ref[_blk(i + t), :] = _unstack_heads(dq[2 * BLK * t:2 * BLK * (t + 1)]).astype(BF16)
            return 0

        lax.fori_loop(0, nq // SB_QB, q_block, 0)
        dk_ref[...] = dk_acc[...].astype(BF16)
        dv_ref[...] = dv_acc[...].astype(BF16)

    return pl.pallas_call(
        body, name="sb_bwd", grid=(pairs,),
        in_specs=[col(0), col(pairs), col(2 * pairs), own, own],
        out_specs=[own, own, own],
        out_shape=[jax.ShapeDtypeStruct((s, D_SB), BF16)] * 3,
        scratch_shapes=[pltpu.VMEM((s, LANES), F32), pltpu.VMEM((s, LANES), F32)],
        compiler_params=_params(56),
    )(qkv, qkv, qkv, o_sb, do_sb)


DIL_UNROLL = 4


def _band_masks(b):
    row = lax.broadcasted_iota(jnp.int32, (2 * BLK, BLK), 0) & (BLK - 1)
    col = lax.broadcasted_iota(jnp.int32, (2 * BLK, BLK), 1)
    return col <= row, (col - row) >= jnp.where(b > 0, 0, BLK)


def _dil_tiles(qf, kf, vf, d, t, nb):
    c, b = t // nb, t % nb
    start = c + d * BLK * b
    rows = pl.ds(start, BLK, stride=d)
    prev = pl.ds(jnp.where(b > 0, start - d * BLK, start), BLK, stride=d)
    bf = lambda ref, sl: ref[sl, :].astype(BF16)
    return b, rows, prev, _stack_heads(bf(qf, rows)), bf(kf, rows), bf(kf, prev), bf(vf, rows), bf(vf, prev)


def _lanes_of_heads(col2):
    return _unstack_heads(jnp.broadcast_to(col2, (2 * BLK, LANES)))


def _dilated_fwd(qkv):
    s = qkv.shape[0]
    pairs = D_DIL // LANES
    base = (3 * D_SB) // LANES
    col = lambda off: pl.BlockSpec((s, LANES), lambda p: (0, off + p))
    own = pl.BlockSpec((s, LANES), lambda p: (0, p))

    def body(q_ref, k_ref, v_ref, acc_ref, m_ref, qf, kf, vf, l_scr):
        qf[...] = q_ref[...].astype(F32)
        kf[...] = k_ref[...].astype(F32)
        vf[...] = v_ref[...].astype(F32)
        for d in DILATIONS:
            nb = s // (d * BLK)

            def block(t, _):
                b, rows, prev, q2, kc, kp, vc, vp = _dil_tiles(qf, kf, vf, d, t, nb)
                in_cur, in_prev = _band_masks(b)
                zc = jnp.where(in_cur, _dot_nt(q2, kc) * SCALE, NEG)
                zp = jnp.where(in_prev, _dot_nt(q2, kp) * SCALE, NEG)
                m = jnp.maximum(jnp.max(zc, axis=1, keepdims=True), jnp.max(zp, axis=1, keepdims=True))
                pc, pp = jnp.exp(zc - m), jnp.exp(zp - m)
                den = jnp.sum(pc, axis=1, keepdims=True) + jnp.sum(pp, axis=1, keepdims=True)
                acc = _unstack_heads(_dot(pc.astype(BF16), vc) + _dot(pp.astype(BF16), vp))
                m_t, l_t = _lanes_of_heads(m), _lanes_of_heads(den)
                if d == DILATIONS[0]:
                    m_ref[rows, :] = m_t
                    l_scr[rows, :] = l_t
                    acc_ref[rows, :] = acc
                else:
                    m_old = m_ref[rows, :]
                    m_new = jnp.maximum(m_old, m_t)
                    keep, add = jnp.exp(m_old - m_new), jnp.exp(m_t - m_new)
                    m_ref[rows, :] = m_new
                    l_scr[rows, :] = l_scr[rows, :] * keep + l_t * add
                    acc_ref[rows, :] = acc_ref[rows, :] * keep + acc * add
                return 0

            lax.fori_loop(0, s // BLK, block, 0, unroll=DIL_UNROLL)

        def finish(i, _):
            l = l_scr[_blk(i), :]
            acc_ref[_blk(i), :] = acc_ref[_blk(i), :] / l
            m_ref[_blk(i), :] = m_ref[_blk(i), :] + jnp.log(l)
            return 0

        lax.fori_loop(0, s // BLK, finish, 0)

    return pl.pallas_call(
        body, name="dilated_fwd", grid=(pairs,),
        in_specs=[col(base), col(base + pairs), col(base + 2 * pairs)],
        out_specs=[own, own],
        out_shape=[jax.ShapeDtypeStruct((s, D_DIL), F32)] * 2,
        scratch_shapes=[pltpu.VMEM((s, LANES), F32)] * 4,
        compiler_params=_params(56),
    )(qkv, qkv, qkv)


def _stack_lanes(t):
    other = pltpu.roll(t, HEAD_DIM, 1)
    first = _head_masks()[0]
    return jnp.concatenate([jnp.where(first, t, other), jnp.where(first, other, t)], axis=0)


def _dilated_bwd(qkv, delta, lse, dout):
    s = qkv.shape[0]
    pairs = D_DIL // LANES
    base = (3 * D_SB) // LANES
    once = pl.Buffered(1)
    col = lambda off: pl.BlockSpec((s, LANES), lambda p: (0, off + p), pipeline_mode=once)
    own = pl.BlockSpec((s, LANES), lambda p: (0, p), pipeline_mode=once)
    res = pl.BlockSpec((s, LANES), lambda p: (0, p))

    def body(q_ref, k_ref, v_ref, dl_ref, l_ref, do_ref, dq_ref, dk_ref, dv_ref, qf, kf, vf):
        qf[...] = q_ref[...].astype(F32)
        kf[...] = k_ref[...].astype(F32)
        vf[...] = v_ref[...].astype(F32)
        dq_ref[...] = jnp.zeros_like(dq_ref)
        dk_ref[...] = jnp.zeros_like(dk_ref)
        dv_ref[...] = jnp.zeros_like(dv_ref)
        for d in DILATIONS:
            nb = s // (d * BLK)

            def block(t, _):
                b, rows, prev, q2, kc, kp, vc, vp = _dil_tiles(qf, kf, vf, d, t, nb)
                in_cur, in_prev = _band_masks(b)
                do2 = _stack_heads(do_ref[rows, :].astype(BF16))
                delta = _stack_lanes(dl_ref[rows, :])
                lse2 = _stack_lanes(l_ref[rows, :])
                wc = jnp.exp(jnp.where(in_cur, _dot_nt(q2, kc) * SCALE, NEG) - lse2)
                wp = jnp.exp(jnp.where(in_prev, _dot_nt(q2, kp) * SCALE, NEG) - lse2)
                dzc = (wc * (_dot_nt(do2, vc) - delta) * SCALE).astype(BF16)
                dzp = (wp * (_dot_nt(do2, vp) - delta) * SCALE).astype(BF16)
                dq_ref[rows, :] += _unstack_heads(_dot(dzc, kc) + _dot(dzp, kp))
                dk_ref[rows, :] += _dot_tn(dzc, q2)
                dk_ref[prev, :] += _dot_tn(dzp, q2)
                dv_ref[rows, :] += _dot_tn(wc.astype(BF16), do2)
                dv_ref[prev, :] += _dot_tn(wp.astype(BF16), do2)
                return 0

            lax.fori_loop(0, s // BLK, block, 0, unroll=DIL_UNROLL)

    return pl.pallas_call(
        body, name="dilated_bwd", grid=(pairs,),
        in_specs=[col(base), col(base + pairs), col(base + 2 * pairs), own, own, own],
        out_specs=[res, res, res],
        out_shape=[jax.ShapeDtypeStruct((s, D_DIL), F32)] * 3,
        scratch_shapes=[pltpu.VMEM((s, LANES), F32)] * 3,
        compiler_params=_params(60),
    )(qkv, qkv, qkv, delta, lse, dout)


def _dilated_finish(grads, cos, sin):
    s = grads[0].shape[0]
    spec = pl.BlockSpec((TM, D_DIL), lambda i: (i, 0))
    tab = pl.BlockSpec((TM, LANES), lambda i: (i, 0))

    def body(dq_ref, dk_ref, dv_ref, c_ref, s_ref, oq_ref, ok_ref, ov_ref):
        for src, dst, rotated in ((dq_ref, oq_ref, True), (dk_ref, ok_ref, True), (dv_ref, ov_ref, False)):
            for c in range(D_DIL // LANES):
                lanes = slice(c * LANES, (c + 1) * LANES)
                piece = src[:, lanes]
                dst[:, lanes] = (_rotate(piece, c_ref[...], -s_ref[...]) if rotated else piece).astype(BF16)

    return pl.pallas_call(
        body, name="dilated_finish", grid=(s // TM,),
        in_specs=[spec] * 3 + [tab, tab], out_specs=[spec] * 3,
        out_shape=[jax.ShapeDtypeStruct((s, D_DIL), BF16)] * 3,
        compiler_params=_params(32),
    )(*grads, cos, sin)


def _place():
    x, y, c = lax.axis_index("x"), lax.axis_index("y"), lax.axis_index("c")
    return x, y, c, 2 * x + y


def _chip(k, c):
    return (k >> 1, k & 1, c)


def _half(ref, h):
    n = ref.shape[0] // 2
    return ref.at[pl.ds(h * n, n)]


def _all_gather(shards):
    na = len(shards)
    any_spec = pl.BlockSpec(memory_space=pl.ANY)

    def body(*refs):
        ins, outs = refs[:na], refs[na:2 * na]
        send_sem, recv_sem, local_sem = refs[2 * na:]
        x, y, c, k = _place()
        sibling = (x, y, 1 - c)
        started = []
        for a in range(na):
            cp = pltpu.make_async_copy(ins[a], outs[a].at[k], local_sem.at[a])
            cp.start()
            started.append(cp)

        def copy(a, slot, src, dst, to):
            return pltpu.make_async_remote_copy(src_ref=src, dst_ref=dst, send_sem=send_sem.at[a * 6 + slot],
                                                recv_sem=recv_sem.at[a * 6 + slot], device_id=to, device_id_type=MESH)

        sends = []
        for a in range(na):
            for j in range(1, N_CHIP):
                cp = copy(a, j - 1, _half(ins[a], c), _half(outs[a].at[k], c), _chip(k ^ j, c))
                cp.start()
                sends.append(cp)
        for j in range(1, N_CHIP):
            for a in range(na):
                landed = _half(outs[a].at[k ^ j], c)
                copy(a, j - 1, landed, landed, sibling).wait_recv()
                cp = copy(a, 2 + j, landed, landed, sibling)
                cp.start()
                sends.append(cp)
        for j in range(1, N_CHIP):
            for a in range(na):
                passed = _half(outs[a].at[k ^ j], 1 - c)
                copy(a, 2 + j, passed, passed, sibling).wait_recv()
        for cp in sends:
            cp.wait_send()
        for cp in started:
            cp.wait()

    return pl.pallas_call(
        body, name="weights_all_gather",
        in_specs=[any_spec] * na, out_specs=[any_spec] * na,
        out_shape=[jax.ShapeDtypeStruct((N_CHIP,) + a.shape, a.dtype) for a in shards],
        scratch_shapes=[pltpu.SemaphoreType.DMA((6 * na,)), pltpu.SemaphoreType.DMA((6 * na,)),
                        pltpu.SemaphoreType.DMA((na,))],
    )(*shards)


def _reduce_scatter(g, core, name):
    n, r, c = g.shape
    hr = r // 2
    once = pl.Buffered(1)
    in_specs = [pl.BlockSpec((n, hr, c), lambda i, core_ref: (0, core_ref[0], 0), pipeline_mode=once),
                pl.BlockSpec((n, hr, c), lambda i, core_ref: (0, 1 - core_ref[0], 0), pipeline_mode=once)]

    def body(core_ref, mine_ref, other_ref, out_ref, from_core, sums, sums_bf, from_chips, done, from_core2, send_sem, recv_sem):
        x, y, cc, k = _place()
        sibling = (x, y, 1 - cc)

        def copy(slot, src, dst, to):
            return pltpu.make_async_remote_copy(src_ref=src, dst_ref=dst, send_sem=send_sem.at[slot],
                                                recv_sem=recv_sem.at[slot], device_id=to, device_id_type=MESH)

        first = copy(0, other_ref, from_core, sibling)
        first.start()
        first.wait()
        total = mine_ref[...] + from_core[...]
        sums[...] = total
        sums_bf[...] = total.astype(BF16)
        sends = [copy(j, sums_bf.at[k ^ j], from_chips.at[j - 1], _chip(k ^ j, cc)) for j in range(1, N_CHIP)]
        for cp in sends:
            cp.start()
        for cp in sends:
            cp.wait()
        red = sums[k]
        for j in range(1, N_CHIP):
            red = red + from_chips[j - 1].astype(F32)
        done[...] = red
        last = copy(N_CHIP, done, from_core2, sibling)
        last.start()
        last.wait()
        row0 = pl.multiple_of(cc * hr, 8)
        row1 = pl.multiple_of((1 - cc) * hr, 8)
        out_ref[pl.ds(row0, hr), :] = red
        out_ref[pl.ds(row1, hr), :] = from_core2[...]

    grid_spec = pltpu.PrefetchScalarGridSpec(
        num_scalar_prefetch=1, grid=(1,), in_specs=in_specs,
        out_specs=pl.BlockSpec((r, c), lambda i, core_ref: (0, 0)),
        scratch_shapes=[pltpu.VMEM((n, hr, c), F32), pltpu.VMEM((n, hr, c), F32), pltpu.VMEM((n, hr, c), BF16),
                        pltpu.VMEM((N_CHIP - 1, hr, c), BF16), pltpu.VMEM((hr, c), F32), pltpu.VMEM((hr, c), F32),
                        pltpu.SemaphoreType.DMA((N_CHIP + 1,)), pltpu.SemaphoreType.DMA((N_CHIP + 1,))])
    return pl.pallas_call(
        body, name=name, grid_spec=grid_spec, out_shape=jax.ShapeDtypeStruct((r, c), F32),
        compiler_params=_params(56),
    )(core, g, g)


def _elementwise(fn, name, ins, n_out, rows):
    total, cols = ins[0].shape
    spec = pl.BlockSpec((rows, cols), lambda i: (i, 0))

    def body(*refs):
        res = fn(*[r[...] for r in refs[:len(ins)]])
        for o, v in zip(refs[len(ins):], res):
            o[...] = v

    return pl.pallas_call(
        body, name=name, grid=(total // rows,),
        in_specs=[spec] * len(ins), out_specs=[spec] * n_out,
        out_shape=[jax.ShapeDtypeStruct((total, cols), F32)] * n_out,
        compiler_params=_params(48),
    )(*ins)


def _adamw(w, g, m, v):
    m = ADAM_B1 * m + (1.0 - ADAM_B1) * g
    v = ADAM_B2 * v + (1.0 - ADAM_B2) * (g * g)
    m_hat = m / (1.0 - ADAM_B1 ** ADAM_STEP)
    v_hat = v / (1.0 - ADAM_B2 ** ADAM_STEP)
    delta = -ADAM_LR * (m_hat / (jnp.sqrt(v_hat) + ADAM_EPS) + ADAM_WD * w)
    return delta, m, v


def _reduce_and_update(grads, weights, moms, vels):
    core = lax.axis_index("c").astype(jnp.int32).reshape(1)
    full = [_reduce_scatter(g, core, f"grads_reduce_scatter_{a}") for a, g in enumerate(grads)]
    out = []
    for a, (g, w, m, v) in enumerate(zip(full, weights, moms, vels)):
        rows = g.shape[0] // 2
        out.append((g,) + tuple(_elementwise(lambda gg, ww, mm, vv: _adamw(ww, gg, mm, vv), f"adamw_{a}", [g, w, m, v], 3, rows)))
    return out


def _reduce_vectors(part, w, m, v):
    n_dev = 8

    def body(p_ref, w_ref, m_ref, v_ref, g_ref, d_ref, nm_ref, nv_ref, buf, send_sem, recv_sem):
        x, y, c, _ = _place()
        me = 4 * x + 2 * y + c
        buf[me] = p_ref[...]
        sends = []
        for off in range(1, n_dev):
            peer = me ^ off
            cp = pltpu.make_async_remote_copy(src_ref=p_ref, dst_ref=buf.at[me], send_sem=send_sem.at[off - 1],
                                              recv_sem=recv_sem.at[off - 1], device_id=(peer >> 2, (peer >> 1) & 1, peer & 1),
                                              device_id_type=MESH)
            cp.start()
            sends.append(cp)
        for off in range(1, n_dev):
            peer = me ^ off
            pltpu.make_async_remote_copy(src_ref=p_ref, dst_ref=buf.at[peer], send_sem=send_sem.at[off - 1],
                                         recv_sem=recv_sem.at[off - 1], device_id=(peer >> 2, (peer >> 1) & 1, peer & 1),
                                         device_id_type=MESH).wait_recv()
        for cp in sends:
            cp.wait_send()
        g = buf[0]
        for d in range(1, n_dev):
            g = g + buf[d]
        g_ref[...] = g
        delta, nm, nv = _adamw(w_ref[...], g, m_ref[...], v_ref[...])
        d_ref[...] = delta
        nm_ref[...] = nm
        nv_ref[...] = nv

    vm = pl.BlockSpec(memory_space=pltpu.VMEM)
    return pl.pallas_call(
        body, name="gains_all_reduce",
        in_specs=[vm] * 4, out_specs=[vm] * 4,
        out_shape=[jax.ShapeDtypeStruct(part.shape, F32)] * 4,
        scratch_shapes=[pltpu.VMEM((n_dev,) + part.shape, F32), pltpu.SemaphoreType.DMA((n_dev - 1,)),
                        pltpu.SemaphoreType.DMA((n_dev - 1,))],
    )(part, w, m, v)


def _pad_row(a):
    a = a.reshape(1, -1)
    return jnp.pad(a, ((0, 0), (0, D_MODEL - a.shape[1])))


def kernel(x, ffn1_norm, ffn1_w_gate, ffn1_w_up, ffn1_w_down, mix_norm, w_in, sb_out_norm, dil_out_norm, w_out, ffn2_norm, ffn2_w_gate, ffn2_w_up, ffn2_w_down, final_norm, loss_target, m_ffn1_norm, m_ffn1_w_gate, m_ffn1_w_up, m_ffn1_w_down, m_mix_norm, m_w_in, m_sb_out_norm, m_dil_out_norm, m_w_out, m_ffn2_norm, m_ffn2_w_gate, m_ffn2_w_up, m_ffn2_w_down, m_final_norm, v_ffn1_norm, v_ffn1_w_gate, v_ffn1_w_up, v_ffn1_w_down, v_mix_norm, v_w_in, v_sb_out_norm, v_dil_out_norm, v_w_out, v_ffn2_norm, v_ffn2_w_gate, v_ffn2_w_up, v_ffn2_w_down, v_final_norm):
    x = x[0]
    target = loss_target[0]
    s = x.shape[0]
    gf = final_norm.reshape(1, D_MODEL)
    cos, sin = _rope_tables(s)

    gu_shard = jnp.stack([ffn1_w_gate[0], ffn1_w_up[0], ffn2_w_gate[0], ffn2_w_up[0]]).astype(BF16)
    wd_shard = jnp.stack([ffn1_w_down[0], ffn2_w_down[0]]).astype(BF16)
    gu, wd, win, wout = _all_gather([gu_shard, wd_shard, w_in[0].astype(BF16), w_out[0].astype(BF16)])
    wout = wout.reshape(D_MODEL, D_MODEL)

    x1, hm, saved1 = _ffn1_fwd(x, ffn1_norm, mix_norm, gu, wd)
    qkv = _proj_fwd(hm, win, cos, sin)
    o_sb = _sb_fwd(qkv)
    o_dl, lse = _dilated_fwd(qkv)
    x2 = _outproj_fwd(o_sb, o_dl, sb_out_norm, dil_out_norm, x1, wout)
    dx3, st_final, saved2 = _ffn2_fwd_loss(x2, ffn2_norm, gf, target, gu, wd)

    dx2, dwg2, dwu2, dwd2, st_ffn2 = _ffn_bwd(x2, ffn2_norm, dx3, saved2, gu, wd, 1)
    do_sb, do_dl, delta_dl, dwout, st_out = _outproj_bwd(dx2, o_sb, o_dl, sb_out_norm, dil_out_norm, wout)
    dq_sb, dk_sb, dv_sb = _sb_bwd(qkv, o_sb, do_sb)
    dq_dl, dk_dl, dv_dl = _dilated_finish(_dilated_bwd(qkv, delta_dl, lse, do_dl), cos, sin)
    dqkv = jnp.concatenate([dq_sb, dk_sb, dv_sb, dq_dl, dk_dl, dv_dl], axis=1)
    dx1, dwin, st_mix = _proj_bwd(x1, mix_norm, dqkv, win, dx2)
    grad_x, dwg1, dwu1, dwd1, st_ffn1 = _ffn_bwd(x, ffn1_norm, dx1, saved1, gu, wd, 0)

    names = ["ffn1_w_gate", "ffn1_w_up", "ffn1_w_down", "w_in", "w_out", "ffn2_w_gate", "ffn2_w_up", "ffn2_w_down"]
    grads = [dwg1, dwu1, dwd1, dwin, dwout.reshape(N_CHIP, OUTB, D_MODEL), dwg2, dwu2, dwd2]
    weights = [ffn1_w_gate[0], ffn1_w_up[0], ffn1_w_down[0], w_in[0], w_out[0], ffn2_w_gate[0], ffn2_w_up[0], ffn2_w_down[0]]
    moms = [m_ffn1_w_gate[0], m_ffn1_w_up[0], m_ffn1_w_down[0], m_w_in[0], m_w_out[0], m_ffn2_w_gate[0], m_ffn2_w_up[0], m_ffn2_w_down[0]]
    vels = [v_ffn1_w_gate[0], v_ffn1_w_up[0], v_ffn1_w_down[0], v_w_in[0], v_w_out[0], v_ffn2_w_gate[0], v_ffn2_w_up[0], v_ffn2_w_down[0]]
    mats = {n: tuple(t[None] for t in r) for n, r in zip(names, _reduce_and_update(grads, weights, moms, vels))}

    vec_names = ["ffn1_norm", "mix_norm", "sb_out_norm", "dil_out_norm", "ffn2_norm", "final_norm"]
    part = jnp.concatenate([st_ffn1[0:1], st_mix[0:1], _pad_row(st_out[0]), _pad_row(st_out[1]), st_ffn2[0:1],
                            st_final[0:1], st_final[1:2], jnp.zeros((1, D_MODEL), F32)], axis=0)
    pack = lambda arrs: jnp.concatenate([_pad_row(a) for a in arrs] + [jnp.zeros((2, D_MODEL), F32)], axis=0)
    g_vec, d_vec, m_vec, v_vec = _reduce_vectors(
        part,
        pack([ffn1_norm, mix_norm, sb_out_norm, dil_out_norm, ffn2_norm, final_norm]),
        pack([m_ffn1_norm, m_mix_norm, m_sb_out_norm, m_dil_out_norm, m_ffn2_norm, m_final_norm]),
        pack([v_ffn1_norm, v_mix_norm, v_sb_out_norm, v_dil_out_norm, v_ffn2_norm, v_final_norm]))
    like = {"ffn1_norm": ffn1_norm, "mix_norm": mix_norm, "sb_out_norm": sb_out_norm, "dil_out_norm": dil_out_norm,
            "ffn2_norm": ffn2_norm, "final_norm": final_norm}
    vecs = {n: tuple(t[i, :like[n].size].reshape(like[n].shape) for t in (g_vec, d_vec, m_vec, v_vec))
            for i, n in enumerate(vec_names)}
    loss = 0.5 * jnp.sum(g_vec[6]) / D_MODEL

    order = ["ffn1_norm", "ffn1_w_gate", "ffn1_w_up", "ffn1_w_down", "mix_norm", "w_in", "sb_out_norm", "dil_out_norm",
             "w_out", "ffn2_norm", "ffn2_w_gate", "ffn2_w_up", "ffn2_w_down", "final_norm"]
    both = {**mats, **vecs}
    return (loss, grad_x[None], *[both[n][0] for n in order], *[both[n][1] for n in order],
            *[both[n][2] for n in order], *[both[n][3] for n in order])
```

```python
import functools

import jax
import jax.numpy as jnp
from jax import lax
from jax.experimental import pallas as pl
from jax.experimental.pallas import tpu as pltpu

D_MODEL = 1024
D_FF = 2816
HEAD_DIM = 64
D_SB = 512
D_DIL = 512
D_IN = 3072
N_CHIP = 4
FFB = D_FF // N_CHIP
INB = D_IN // N_CHIP
OUTB = D_MODEL // N_CHIP
BLK = 128
LANES = 128
DILATIONS = (1, 4, 16)
ROPE_THETA = 10000.0
RMS_EPS = 1e-6
SCALE = HEAD_DIM ** -0.5
NEG = -1e30
DEAD = -104.0
ADAM_LR = 0.001
ADAM_B1 = 0.9
ADAM_B2 = 0.999
ADAM_EPS = 1e-08
ADAM_WD = 0.01
ADAM_STEP = 10
MESH = pl.DeviceIdType.MESH
F32 = jnp.float32
BF16 = jnp.bfloat16
TM = 512


def _params(vmem_mb):
    return pltpu.CompilerParams(vmem_limit_bytes=vmem_mb << 20)


def _dot(a, b):
    return jnp.dot(a, b, preferred_element_type=F32)


def _dot_nt(a, b):
    return lax.dot_general(a, b, (((1,), (1,)), ((), ())), preferred_element_type=F32)


def _dot_tn(a, b):
    return lax.dot_general(a, b, (((0,), (0,)), ((), ())), preferred_element_type=F32)


def _rms_fwd(x, g):
    r = lax.rsqrt(jnp.mean(x * x, axis=-1, keepdims=True) + RMS_EPS)
    xh = x * r
    return xh * g, xh, r


def _rms_bwd(dy, xh, r, g):
    dyg = dy * g
    dx = r * (dyg - xh * jnp.mean(dyg * xh, axis=-1, keepdims=True))
    return dx, jnp.sum(dy * xh, axis=0, keepdims=True)


def _split_bf16(a):
    hi = a.astype(BF16)
    return hi, (a - hi.astype(F32)).astype(BF16)


def _dot_split(a, b):
    hi, lo = _split_bf16(a)
    return _dot(hi, b) + _dot(lo, b)


def _ffn_weight_specs():
    return [pl.BlockSpec((None, D_MODEL, FFB), lambda i, j: (j, 0, 0)),
            pl.BlockSpec((None, D_MODEL, FFB), lambda i, j: (j, 0, 0)),
            pl.BlockSpec((None, FFB, D_MODEL), lambda i, j: (j, 0, 0))]


def _ffn_saved(s):
    hidden = jax.ShapeDtypeStruct((N_CHIP, s, FFB), BF16)
    hid = pl.BlockSpec((None, TM, FFB), lambda i, j: (j, i, 0))
    row = pl.BlockSpec((TM, D_MODEL), lambda i, j: (i, 0))
    return [row, hid, hid, hid], [jax.ShapeDtypeStruct((s, D_MODEL), BF16), hidden, hidden, hidden]


def _ffn_accumulate(h_ref, acc_scr, wg_ref, wu_ref, wd_ref, a_ref, b_ref, act_ref):
    h = h_ref[...]
    a = _dot(h, wg_ref[...])
    b = _dot(h, wu_ref[...])
    act = ((a * jax.nn.sigmoid(a)) * b).astype(BF16)
    a_ref[...] = a.astype(BF16)
    b_ref[...] = b.astype(BF16)
    act_ref[...] = act
    acc_scr[...] += _dot(act, wd_ref[...])


def _ffn1_fwd(x, g1, gmix, gu, wd):
    s = x.shape[0]
    row = pl.BlockSpec((TM, D_MODEL), lambda i, j: (i, 0))
    vec = pl.BlockSpec((1, D_MODEL), lambda i, j: (0, 0))
    saved_specs, saved_shapes = _ffn_saved(s)

    def body(x_ref, g_ref, gm_ref, wg_ref, wu_ref, wd_ref, x1_ref, hm_ref, h_ref, a_ref, b_ref, act_ref, acc_scr):
        j = pl.program_id(1)

        @pl.when(j == 0)
        def _():
            h, _, _ = _rms_fwd(x_ref[...], g_ref[...])
            h_ref[...] = h.astype(BF16)
            acc_scr[...] = jnp.zeros_like(acc_scr)

        _ffn_accumulate(h_ref, acc_scr, wg_ref, wu_ref, wd_ref, a_ref, b_ref, act_ref)

        @pl.when(j == N_CHIP - 1)
        def _():
            x1 = x_ref[...] + 0.5 * acc_scr[...]
            x1_ref[...] = x1
            hm, _, _ = _rms_fwd(x1, gm_ref[...])
            hm_ref[...] = hm.astype(BF16)

    x1, hm, *saved = pl.pallas_call(
        body, name="ffn1_fwd", grid=(s // TM, N_CHIP),
        in_specs=[row, vec, vec] + _ffn_weight_specs(),
        out_specs=[row, row] + saved_specs,
        out_shape=[jax.ShapeDtypeStruct((s, D_MODEL), F32), jax.ShapeDtypeStruct((s, D_MODEL), BF16)] + saved_shapes,
        scratch_shapes=[pltpu.VMEM((TM, D_MODEL), F32)],
        compiler_params=_params(56),
    )(x, g1, gmix, gu[0], gu[1], wd)
    return x1, hm, saved


def _ffn2_fwd_loss(x2, g2, gf, target, gu, wd):
    s = x2.shape[0]
    row = pl.BlockSpec((TM, D_MODEL), lambda i, j: (i, 0))
    vec = pl.BlockSpec((1, D_MODEL), lambda i, j: (0, 0))
    stat = pl.BlockSpec((8, D_MODEL), lambda i, j: (0, 0))
    saved_specs, saved_shapes = _ffn_saved(s)

    def body(x_ref, g_ref, gf_ref, t_ref, wg_ref, wu_ref, wd_ref, dx_ref, st_ref, h_ref, a_ref, b_ref, act_ref, acc_scr):
        i, j = pl.program_id(0), pl.program_id(1)

        @pl.when((i == 0) & (j == 0))
        def _():
            st_ref[...] = jnp.zeros_like(st_ref)

        @pl.when(j == 0)
        def _():
            h, _, _ = _rms_fwd(x_ref[...], g_ref[...])
            h_ref[...] = h.astype(BF16)
            acc_scr[...] = jnp.zeros_like(acc_scr)

        _ffn_accumulate(h_ref, acc_scr, wg_ref, wu_ref, wd_ref, a_ref, b_ref, act_ref)

        @pl.when(j == N_CHIP - 1)
        def _():
            x3 = x_ref[...] + 0.5 * acc_scr[...]
            y, xh, r = _rms_fwd(x3, gf_ref[...])
            err = y - t_ref[...]
            dx, dg = _rms_bwd(err * (1.0 / D_MODEL), xh, r, gf_ref[...])
            dx_ref[...] = dx
            st_ref[0:1, :] += dg
            st_ref[1:2, :] += jnp.sum(err * err, axis=0, keepdims=True)

    dx3, st, *saved = pl.pallas_call(
        body, name="ffn2_fwd_loss", grid=(s // TM, N_CHIP),
        in_specs=[row, vec, vec, row] + _ffn_weight_specs(),
        out_specs=[row, stat] + saved_specs,
        out_shape=[jax.ShapeDtypeStruct((s, D_MODEL), F32), jax.ShapeDtypeStruct((8, D_MODEL), F32)] + saved_shapes,
        scratch_shapes=[pltpu.VMEM((TM, D_MODEL), F32)],
        compiler_params=_params(56),
    )(x2, g2, gf, target, gu[0], gu[1], wd)
    return dx3, st, saved


def _ffn_bwd(xin, g, dy, saved, gu, wd, f):
    s = xin.shape[0]
    hb, gate, up, act = saved
    row = pl.BlockSpec((TM, D_MODEL), lambda i, j: (i, 0))
    vec = pl.BlockSpec((1, D_MODEL), lambda i, j: (0, 0))
    stat = pl.BlockSpec((8, D_MODEL), lambda i, j: (0, 0))
    hid = pl.BlockSpec((None, TM, FFB), lambda i, j: (j, i, 0))

    def body(x_ref, g_ref, dy_ref, a_ref, b_ref, wg_ref, wu_ref, wd_ref, out_ref, dyh_ref, da_ref, db_ref, st_ref, dh_scr):
        i, j = pl.program_id(0), pl.program_id(1)

        @pl.when((i == 0) & (j == 0))
        def _():
            st_ref[...] = jnp.zeros_like(st_ref)

        @pl.when(j == 0)
        def _():
            dyh_ref[...] = (0.5 * dy_ref[...]).astype(BF16)
            dh_scr[...] = jnp.zeros_like(dh_scr)

        a = a_ref[...].astype(F32)
        b = b_ref[...].astype(F32)
        sg = jax.nn.sigmoid(a)
        dact = _dot_nt(dyh_ref[...], wd_ref[...])
        dab = (dact * b * (sg * (1.0 + a * (1.0 - sg)))).astype(BF16)
        dbb = (dact * (a * sg)).astype(BF16)
        da_ref[...] = dab
        db_ref[...] = dbb
        dh_scr[...] += _dot_nt(dab, wg_ref[...]) + _dot_nt(dbb, wu_ref[...])

        @pl.when(j == N_CHIP - 1)
        def _():
            _, xh, r = _rms_fwd(x_ref[...], g_ref[...])
            dx, dg = _rms_bwd(dh_scr[...], xh, r, g_ref[...])
            out_ref[...] = dy_ref[...] + dx
            st_ref[0:1, :] += dg

    hidden = jax.ShapeDtypeStruct((N_CHIP, s, FFB), BF16)
    dx, dyh, da, db, st = pl.pallas_call(
        body, name=f"ffn{f + 1}_bwd_dx", grid=(s // TM, N_CHIP),
        in_specs=[row, vec, row, hid, hid] + _ffn_weight_specs(),
        out_specs=[row, row, hid, hid, stat],
        out_shape=[jax.ShapeDtypeStruct((s, D_MODEL), F32), jax.ShapeDtypeStruct((s, D_MODEL), BF16),
                   hidden, hidden, jax.ShapeDtypeStruct((8, D_MODEL), F32)],
        scratch_shapes=[pltpu.VMEM((TM, D_MODEL), F32)],
        compiler_params=_params(56),
    )(xin, g, dy, gate, up, gu[0], gu[1], wd)

    tok = pl.BlockSpec((TM, D_MODEL), lambda j, i: (i, 0))
    hid2 = pl.BlockSpec((None, TM, FFB), lambda j, i: (j, i, 0))
    gspecs = [pl.BlockSpec((None, D_MODEL, FFB), lambda j, i: (j, 0, 0)),
              pl.BlockSpec((None, D_MODEL, FFB), lambda j, i: (j, 0, 0)),
              pl.BlockSpec((None, FFB, D_MODEL), lambda j, i: (j, 0, 0))]

    def wbody(h_ref, dyh_ref, da_ref, db_ref, act_ref, dwg_ref, dwu_ref, dwd_ref):
        @pl.when(pl.program_id(1) == 0)
        def _():
            dwg_ref[...] = jnp.zeros_like(dwg_ref)
            dwu_ref[...] = jnp.zeros_like(dwu_ref)
            dwd_ref[...] = jnp.zeros_like(dwd_ref)

        hb = h_ref[...]
        dwg_ref[...] += _dot_tn(hb, da_ref[...])
        dwu_ref[...] += _dot_tn(hb, db_ref[...])
        dwd_ref[...] += _dot_tn(act_ref[...], dyh_ref[...])

    dwg, dwu, dwd = pl.pallas_call(
        wbody, name=f"ffn{f + 1}_bwd_dw", grid=(N_CHIP, s // TM),
        in_specs=[tok, tok, hid2, hid2, hid2], out_specs=gspecs,
        out_shape=[jax.ShapeDtypeStruct((N_CHIP, D_MODEL, FFB), F32),
                   jax.ShapeDtypeStruct((N_CHIP, D_MODEL, FFB), F32),
                   jax.ShapeDtypeStruct((N_CHIP, FFB, D_MODEL), F32)],
        compiler_params=_params(48),
    )(hb, dyh, da, db, act)
    return dx, dwg, dwu, dwd, st


def _rope_tables(s):
    half = HEAD_DIM // 2
    inv_freq = ROPE_THETA ** (-jnp.arange(half, dtype=F32) / half)
    ang = jnp.arange(s).astype(F32)[:, None] * inv_freq[None, :]
    cos, sin = jnp.cos(ang), jnp.sin(ang)
    cos2 = jnp.concatenate([cos, cos], axis=-1)
    sin2 = jnp.concatenate([-sin, sin], axis=-1)
    return jnp.tile(cos2, (1, LANES // HEAD_DIM)), jnp.tile(sin2, (1, LANES // HEAD_DIM))


def _rotate(t, cos, sin_signed):
    lane = lax.broadcasted_iota(jnp.int32, t.shape, 1)
    first = (lane % HEAD_DIM) < (HEAD_DIM // 2)
    partner = jnp.where(first, pltpu.roll(t, LANES - HEAD_DIM // 2, 1), pltpu.roll(t, HEAD_DIM // 2, 1))
    return t * cos + partner * sin_signed


def _proj_fwd(hm, win, cos, sin):
    s = hm.shape[0]
    n_sub = INB // LANES
    first_rot, last_rot = (3 * D_SB) // LANES, (3 * D_SB + 2 * D_DIL) // LANES

    def body(h_ref, w_ref, c_ref, s_ref, o_ref):
        j = pl.program_id(1)
        r = _dot(h_ref[...], w_ref[...])
        for c in range(n_sub):
            t = r[:, c * LANES:(c + 1) * LANES]
            col = j * n_sub + c
            rot = (col >= first_rot) & (col < last_rot)
            lanes = slice(c * LANES, (c + 1) * LANES)

            @pl.when(rot)
            def _():
                o_ref[:, lanes] = _rotate(t, c_ref[...], s_ref[...]).astype(BF16)

            @pl.when(jnp.logical_not(rot))
            def _():
                o_ref[:, lanes] = t.astype(BF16)

    return pl.pallas_call(
        body, name="proj_fwd", grid=(s // TM, N_CHIP),
        in_specs=[pl.BlockSpec((TM, D_MODEL), lambda i, j: (i, 0)),
                  pl.BlockSpec((None, D_MODEL, INB), lambda i, j: (j, 0, 0)),
                  pl.BlockSpec((TM, LANES), lambda i, j: (i, 0)),
                  pl.BlockSpec((TM, LANES), lambda i, j: (i, 0))],
        out_specs=pl.BlockSpec((TM, INB), lambda i, j: (i, j)),
        out_shape=jax.ShapeDtypeStruct((s, D_IN), BF16),
        compiler_params=_params(32),
    )(hm, win, cos, sin)


def _proj_bwd(x1, gmix, dqkv, win, dx2):
    s = x1.shape[0]
    row = pl.BlockSpec((TM, D_MODEL), lambda i, j: (i, 0))
    vec = pl.BlockSpec((1, D_MODEL), lambda i, j: (0, 0))

    def body(x_ref, g_ref, dq_ref, w_ref, dx2_ref, out_ref, dw_ref, st_ref, h_scr, dh_scr):
        i, j = pl.program_id(0), pl.program_id(1)

        @pl.when((i == 0) & (j == 0))
        def _():
            st_ref[...] = jnp.zeros_like(st_ref)
            dw_ref[...] = jnp.zeros_like(dw_ref)

        @pl.when(j == 0)
        def _():
            h, _, _ = _rms_fwd(x_ref[...], g_ref[...])
            h_scr[...] = h.astype(BF16)
            dh_scr[...] = jnp.zeros_like(dh_scr)

        dq = dq_ref[...]
        dw_ref[j] += _dot_tn(h_scr[...], dq)
        dh_scr[...] += _dot_nt(dq, w_ref[...])

        @pl.when(j == N_CHIP - 1)
        def _():
            _, xh, r = _rms_fwd(x_ref[...], g_ref[...])
            dx, dg = _rms_bwd(dh_scr[...], xh, r, g_ref[...])
            out_ref[...] = dx2_ref[...] + dx
            st_ref[0:1, :] += dg

    return pl.pallas_call(
        body, name="proj_bwd", grid=(s // TM, N_CHIP),
        in_specs=[row, vec, pl.BlockSpec((TM, INB), lambda i, j: (i, j)),
                  pl.BlockSpec((None, D_MODEL, INB), lambda i, j: (j, 0, 0)), row],
        out_specs=[row, pl.BlockSpec((N_CHIP, D_MODEL, INB), lambda i, j: (0, 0, 0)),
                   pl.BlockSpec((8, D_MODEL), lambda i, j: (0, 0))],
        out_shape=[jax.ShapeDtypeStruct((s, D_MODEL), F32),
                   jax.ShapeDtypeStruct((N_CHIP, D_MODEL, INB), F32),
                   jax.ShapeDtypeStruct((8, D_MODEL), F32)],
        scratch_shapes=[pltpu.VMEM((TM, D_MODEL), BF16), pltpu.VMEM((TM, D_MODEL), F32)],
        compiler_params=_params(56),
    )(x1, gmix, dqkv, win, dx2)


def _outproj_fwd(o_sb, o_dl, g_sb, g_dl, x1, wout):
    s = x1.shape[0]
    half = pl.BlockSpec((TM, D_SB), lambda i: (i, 0))
    row = pl.BlockSpec((TM, D_MODEL), lambda i: (i, 0))
    vec = pl.BlockSpec((1, D_SB), lambda i: (0, 0))

    def body(a_ref, b_ref, ga_ref, gb_ref, x_ref, w_ref, o_ref):
        ma, _, _ = _rms_fwd(a_ref[...], ga_ref[...])
        mb, _, _ = _rms_fwd(b_ref[...], gb_ref[...])
        o_ref[...] = (x_ref[...] + _dot(ma.astype(BF16), w_ref[0:D_SB, :])
                      + _dot(mb.astype(BF16), w_ref[D_SB:D_MODEL, :]))

    return pl.pallas_call(
        body, name="outproj_fwd", grid=(s // TM,),
        in_specs=[half, half, vec, vec, row, pl.BlockSpec((D_MODEL, D_MODEL), lambda i: (0, 0))],
        out_specs=row, out_shape=jax.ShapeDtypeStruct((s, D_MODEL), F32),
        compiler_params=_params(32),
    )(o_sb, o_dl, g_sb, g_dl, x1, wout)


def _outproj_bwd(dx2, o_sb, o_dl, g_sb, g_dl, wout):
    s = dx2.shape[0]
    half = pl.BlockSpec((TM, D_SB), lambda i: (i, 0))
    row = pl.BlockSpec((TM, D_MODEL), lambda i: (i, 0))
    vec = pl.BlockSpec((1, D_SB), lambda i: (0, 0))
    full = pl.BlockSpec((D_MODEL, D_MODEL), lambda i: (0, 0))

    def body(dy_ref, a_ref, b_ref, ga_ref, gb_ref, w_ref, da_ref, db_ref, dl_ref, dw_ref, st_ref):
        @pl.when(pl.program_id(0) == 0)
        def _():
            dw_ref[...] = jnp.zeros_like(dw_ref)
            st_ref[...] = jnp.zeros_like(st_ref)

        dy = dy_ref[...].astype(BF16)
        dm = _dot_nt(dy, w_ref[...])
        ma, xa, ra = _rms_fwd(a_ref[...], ga_ref[...])
        mb, xb, rb = _rms_fwd(b_ref[...], gb_ref[...])
        dw_ref[0:D_SB, :] += _dot_tn(ma.astype(BF16), dy)
        dw_ref[D_SB:D_MODEL, :] += _dot_tn(mb.astype(BF16), dy)
        da, dga = _rms_bwd(dm[:, 0:D_SB], xa, ra, ga_ref[...])
        db, dgb = _rms_bwd(dm[:, D_SB:D_MODEL], xb, rb, gb_ref[...])
        da_ref[...] = da
        db_ref[...] = db
        r = lax.broadcasted_iota(jnp.int32, (LANES, LANES), 0) >= HEAD_DIM
        c = lax.broadcasted_iota(jnp.int32, (LANES, LANES), 1) >= HEAD_DIM
        same_head = jnp.where(r == c, 1.0, 0.0).astype(BF16)
        prod = db * b_ref[...]
        for k in range(D_DIL // LANES):
            lanes = slice(k * LANES, (k + 1) * LANES)
            dl_ref[:, lanes] = _dot_split(prod[:, lanes], same_head)
        st_ref[0:1, :] += dga
        st_ref[1:2, :] += dgb

    return pl.pallas_call(
        body, name="outproj_bwd", grid=(s // TM,),
        in_specs=[row, half, half, vec, vec, full],
        out_specs=[half, half, half, full, pl.BlockSpec((8, D_SB), lambda i: (0, 0))],
        out_shape=[jax.ShapeDtypeStruct((s, D_SB), F32), jax.ShapeDtypeStruct((s, D_SB), F32),
                   jax.ShapeDtypeStruct((s, D_DIL), F32),
                   jax.ShapeDtypeStruct((D_MODEL, D_MODEL), F32), jax.ShapeDtypeStruct((8, D_SB), F32)],
        compiler_params=_params(48),
    )(dx2, o_sb, o_dl, g_sb, g_dl, wout)


def _head_masks():
    lane = lax.broadcasted_iota(jnp.int32, (BLK, LANES), 1)
    return [lane < HEAD_DIM, lane >= HEAD_DIM]


def _keep(mask, a):
    return a * jnp.where(mask, 1.0, 0.0).astype(a.dtype)


def _suffix_matrices():
    r = lax.broadcasted_iota(jnp.int32, (BLK, BLK), 0)
    c = lax.broadcasted_iota(jnp.int32, (BLK, BLK), 1)
    ones = jnp.ones((BLK, BLK), BF16)
    excl = jnp.concatenate([(r > c).astype(BF16), ones], axis=1)
    incl = jnp.concatenate([(r >= c).astype(BF16), ones], axis=1)
    return excl, incl


def _blk(i):
    return pl.ds(pl.multiple_of(i * BLK, BLK), BLK)


def _alive(carry_m):
    return (jnp.max(carry_m) > DEAD).astype(jnp.int32)


def _more_keys(last, carry):
    return (carry[0] * SB_KB <= last) & (carry[1] > 0)


def _stack_heads(a):
    masks = _head_masks()
    return jnp.concatenate([_keep(masks[0], a), _keep(masks[1], a)], axis=0)


def _unstack_heads(a2):
    return jnp.where(_head_masks()[0], a2[:BLK], a2[BLK:])


def _head_rowsum(a):
    masks = _head_masks()
    return jnp.concatenate([jnp.sum(jnp.where(m, a, 0.0), axis=1, keepdims=True) for m in masks], axis=0)


SB_QB = 2
SB_ROWS = SB_QB * 2 * BLK
SB_KB = 2
PAST_START = 1 << 30


def _sb_rows(ref, i0, cast=None):
    tiles = [ref[_blk(i0 + t), :] for t in range(SB_QB)]
    return jnp.concatenate([_stack_heads(t if cast is None else t.astype(cast)) for t in tiles], axis=0)


def _sb_scores(q2, k, i, j, carry_m, u_excl):
    r = lax.broadcasted_iota(jnp.int32, (SB_ROWS, BLK), 0)
    row = (r & (BLK - 1)) + ((r >> 8) << 7)
    col = lax.broadcasted_iota(jnp.int32, (SB_ROWS, BLK), 1)
    valid = (jnp.where(j >= 0, j * BLK, PAST_START) + col) < (i * BLK + row)
    z = _dot_nt(q2, k) * SCALE
    sp = jnp.maximum(z, 0.0) + jnp.log(1.0 + jnp.exp(-jnp.abs(z)))
    log_stay = jnp.where(valid, -sp, 0.0)
    log_beta = z - sp
    sums = _dot_split(log_stay, u_excl)
    later = carry_m + sums[:, :BLK]
    w = jnp.where(valid, jnp.exp(log_beta + later), 0.0)
    return valid, log_beta, w, carry_m + sums[:, BLK:]


def _sb_fwd(qkv):
    s = qkv.shape[0]
    nq = s // BLK
    pairs = D_SB // LANES
    col = lambda off: pl.BlockSpec((s, LANES), lambda p: (0, off + p))

    def body(q_ref, k_ref, v_ref, o_ref):
        u_excl, _ = _suffix_matrices()
        zero = jnp.zeros((SB_ROWS, LANES), F32)

        def q_block(ib, _):
            i = ib * SB_QB
            last = i + SB_QB - 1
            q2 = _sb_rows(q_ref, i)

            def k_block(carry):
                jj, _, carry_m, acc = carry
                for t in range(SB_KB):
                    j = last - jj * SB_KB - t
                    at = _blk(jnp.maximum(j, 0))
                    _, _, w, carry_m = _sb_scores(q2, k_ref[at, :], i, j, carry_m, u_excl)
                    acc = acc + _dot(w.astype(BF16), v_ref[at, :])
                return jj + 1, _alive(carry_m), carry_m, acc

            _, _, _, acc = lax.while_loop(functools.partial(_more_keys, last), k_block,
                                          (jnp.int32(0), jnp.int32(1), zero, zero))
            for t in range(SB_QB):
                o_ref[_blk(i + t), :] = _unstack_heads(acc[2 * BLK * t:2 * BLK * (t + 1)])
            return 0

        lax.fori_loop(0, nq // SB_QB, q_block, 0)

    return pl.pallas_call(
        body, name="sb_fwd", grid=(pairs,),
        in_specs=[col(0), col(pairs), col(2 * pairs)],
        out_specs=pl.BlockSpec((s, LANES), lambda p: (0, p)),
        out_shape=jax.ShapeDtypeStruct((s, D_SB), F32),
        compiler_params=_params(48),
    )(qkv, qkv, qkv)


def _sb_bwd(qkv, o_sb, do_sb):
    s = qkv.shape[0]
    nq = s // BLK
    pairs = D_SB // LANES
    col = lambda off: pl.BlockSpec((s, LANES), lambda p: (0, off + p))
    own = pl.BlockSpec((s, LANES), lambda p: (0, p))

    def body(q_ref, k_ref, v_ref, o_ref, do_ref, dq_ref, dk_ref, dv_ref, dk_acc, dv_acc):
        u_excl, u_incl = _suffix_matrices()
        zero = jnp.zeros((SB_ROWS, LANES), F32)
        dk_acc[...] = jnp.zeros_like(dk_acc)
        dv_acc[...] = jnp.zeros_like(dv_acc)

        def q_block(ib, _):
            i = ib * SB_QB
            last = i + SB_QB - 1
            q2 = _sb_rows(q_ref, i)
            do2 = _sb_rows(do_ref, i, BF16)
            totals = [_head_rowsum(do_ref[_blk(i + t), :].astype(BF16).astype(F32) * o_ref[_blk(i + t), :])
                      for t in range(SB_QB)]
            total = jnp.broadcast_to(jnp.concatenate(totals, axis=0), (SB_ROWS, BLK))

            def k_block(carry):
                jj, _, carry_m, carry_g, dq = carry
                for t in range(SB_KB):
                    j = last - jj * SB_KB - t
                    at = _blk(jnp.maximum(j, 0))
                    k = k_ref[at, :]
                    valid, log_beta, w, carry_m = _sb_scores(q2, k, i, j, carry_m, u_excl)
                    wb = w.astype(BF16)
                    g = wb.astype(F32) * _dot_nt(do2, v_ref[at, :])
                    sums = _dot_split(g, u_incl)
                    before = total - (carry_g + sums[:, :BLK])
                    dz = jnp.where(valid, g - jnp.exp(log_beta) * (g + before), 0.0)
                    dzb = (dz * SCALE).astype(BF16)
                    dk_acc[at, :] += _dot_tn(dzb, q2)
                    dv_acc[at, :] += _dot_tn(wb, do2)
                    carry_g = carry_g + sums[:, BLK:]
                    dq = dq + _dot(dzb, k)
                return jj + 1, _alive(carry_m), carry_m, carry_g, dq

            _, _, _, _, dq = lax.while_loop(functools.partial(_more_keys, last), k_block,
                                            (jnp.int32(0), jnp.int32(1), zero, zero, zero))
            for t in range(SB_QB):
                dq_ref[_blk(i + t), :] = _unstack_heads(dq[2 * BLK * t:2 * BLK * (t + 1)]).astype(BF16)
            return 0

        lax.fori_loop(0, nq // SB_QB, q_block, 0)
        dk_ref[...] = dk_acc[...].astype(BF16)
        dv_ref[...] = dv_acc[...].astype(BF16)

    return pl.pallas_call(
        body, name="sb_bwd", grid=(pairs,),
        in_specs=[col(0), col(pairs), col(2 * pairs), own, own],
        out_specs=[own, own, own],
        out_shape=[jax.ShapeDtypeStruct((s, D_SB), BF16)] * 3,
        scratch_shapes=[pltpu.VMEM((s, LANES), F32), pltpu.VMEM((s, LANES), F32)],
        compiler_params=_params(56),
    )(qkv, qkv, qkv, o_sb, do_sb)


DIL_UNROLL = 4


def _band_masks(b):
    row = lax.broadcasted_iota(jnp.int32, (2 * BLK, BLK), 0) & (BLK - 1)
    col = lax.broadcasted_iota(jnp.int32, (2 * BLK, BLK), 1)
    return col <= row, (col - row) >= jnp.where(b > 0, 0, BLK)


def _dil_tiles(qf, kf, vf, d, t, nb):
    c, b = t // nb, t % nb
    start = c + d * BLK * b
    rows = pl.ds(start, BLK, stride=d)
    prev = pl.ds(jnp.where(b > 0, start - d * BLK, start), BLK, stride=d)
    bf = lambda ref, sl: ref[sl, :].astype(BF16)
    return b, rows, prev, _stack_heads(bf(qf, rows)), bf(kf, rows), bf(kf, prev), bf(vf, rows), bf(vf, prev)


def _lanes_of_heads(col2):
    return _unstack_heads(jnp.broadcast_to(col2, (2 * BLK, LANES)))


def _dilated_fwd(qkv):
    s = qkv.shape[0]
    pairs = D_DIL // LANES
    base = (3 * D_SB) // LANES
    col = lambda off: pl.BlockSpec((s, LANES), lambda p: (0, off + p))
    own = pl.BlockSpec((s, LANES), lambda p: (0, p))

    def body(q_ref, k_ref, v_ref, acc_ref, m_ref, qf, kf, vf, l_scr):
        qf[...] = q_ref[...].astype(F32)
        kf[...] = k_ref[...].astype(F32)
        vf[...] = v_ref[...].astype(F32)
        for d in DILATIONS:
            nb = s // (d * BLK)

            def block(t, _):
                b, rows, prev, q2, kc, kp, vc, vp = _dil_tiles(qf, kf, vf, d, t, nb)
                in_cur, in_prev = _band_masks(b)
                zc = jnp.where(in_cur, _dot_nt(q2, kc) * SCALE, NEG)
                zp = jnp.where(in_prev, _dot_nt(q2, kp) * SCALE, NEG)
                m = jnp.maximum(jnp.max(zc, axis=1, keepdims=True), jnp.max(zp, axis=1, keepdims=True))
                pc, pp = jnp.exp(zc - m), jnp.exp(zp - m)
                den = jnp.sum(pc, axis=1, keepdims=True) + jnp.sum(pp, axis=1, keepdims=True)
                acc = _unstack_heads(_dot(pc.astype(BF16), vc) + _dot(pp.astype(BF16), vp))
                m_t, l_t = _lanes_of_heads(m), _lanes_of_heads(den)
                if d == DILATIONS[0]:
                    m_ref[rows, :] = m_t
                    l_scr[rows, :] = l_t
                    acc_ref[rows, :] = acc
                else:
                    m_old = m_ref[rows, :]
                    m_new = jnp.maximum(m_old, m_t)
                    keep, add = jnp.exp(m_old - m_new), jnp.exp(m_t - m_new)
                    m_ref[rows, :] = m_new
                    l_scr[rows, :] = l_scr[rows, :] * keep + l_t * add
                    acc_ref[rows, :] = acc_ref[rows, :] * keep + acc * add
                return 0

            lax.fori_loop(0, s // BLK, block, 0, unroll=DIL_UNROLL)

        def finish(i, _):
            l = l_scr[_blk(i), :]
            acc_ref[_blk(i), :] = acc_ref[_blk(i), :] / l
            m_ref[_blk(i), :] = m_ref[_blk(i), :] + jnp.log(l)
            return 0

        lax.fori_loop(0, s // BLK, finish, 0)

    return pl.pallas_call(
        body, name="dilated_fwd", grid=(pairs,),
        in_specs=[col(base), col(base + pairs), col(base + 2 * pairs)],
        out_specs=[own, own],
        out_shape=[jax.ShapeDtypeStruct((s, D_DIL), F32)] * 2,
        scratch_shapes=[pltpu.VMEM((s, LANES), F32)] * 4,
        compiler_params=_params(56),
    )(qkv, qkv, qkv)


def _stack_lanes(t):
    other = pltpu.roll(t, HEAD_DIM, 1)
    first = _head_masks()[0]
    return jnp.concatenate([jnp.where(first, t, other), jnp.where(first, other, t)], axis=0)


def _dilated_bwd(qkv, delta, lse, dout):
    s = qkv.shape[0]
    pairs = D_DIL // LANES
    base = (3 * D_SB) // LANES
    once = pl.Buffered(1)
    col = lambda off: pl.BlockSpec((s, LANES), lambda p: (0, off + p), pipeline_mode=once)
    own = pl.BlockSpec((s, LANES), lambda p: (0, p), pipeline_mode=once)
    res = pl.BlockSpec((s, LANES), lambda p: (0, p))

    def body(q_ref, k_ref, v_ref, dl_ref, l_ref, do_ref, dq_ref, dk_ref, dv_ref, qf, kf, vf):
        qf[...] = q_ref[...].astype(F32)
        kf[...] = k_ref[...].astype(F32)
        vf[...] = v_ref[...].astype(F32)
        dq_ref[...] = jnp.zeros_like(dq_ref)
        dk_ref[...] = jnp.zeros_like(dk_ref)
        dv_ref[...] = jnp.zeros_like(dv_ref)
        for d in DILATIONS:
            nb = s // (d * BLK)

            def block(t, _):
                b, rows, prev, q2, kc, kp, vc, vp = _dil_tiles(qf, kf, vf, d, t, nb)
                in_cur, in_prev = _band_masks(b)
                do2 = _stack_heads(do_ref[rows, :].astype(BF16))
                delta = _stack_lanes(dl_ref[rows, :])
                lse2 = _stack_lanes(l_ref[rows, :])
                wc = jnp.exp(jnp.where(in_cur, _dot_nt(q2, kc) * SCALE, NEG) - lse2)
                wp = jnp.exp(jnp.where(in_prev, _dot_nt(q2, kp) * SCALE, NEG) - lse2)
                dzc = (wc * (_dot_nt(do2, vc) - delta) * SCALE).astype(BF16)
                dzp = (wp * (_dot_nt(do2, vp) - delta) * SCALE).astype(BF16)
                dq_ref[rows, :] += _unstack_heads(_dot(dzc, kc) + _dot(dzp, kp))
                dk_ref[rows, :] += _dot_tn(dzc, q2)
                dk_ref[prev, :] += _dot_tn(dzp, q2)
                dv_ref[rows, :] += _dot_tn(wc.astype(BF16), do2)
                dv_ref[prev, :] += _dot_tn(wp.astype(BF16), do2)
                return 0

            lax.fori_loop(0, s // BLK, block, 0, unroll=DIL_UNROLL)

    return pl.pallas_call(
        body, name="dilated_bwd", grid=(pairs,),
        in_specs=[col(base), col(base + pairs), col(base + 2 * pairs), own, own, own],
        out_specs=[res, res, res],
        out_shape=[jax.ShapeDtypeStruct((s, D_DIL), F32)] * 3,
        scratch_shapes=[pltpu.VMEM((s, LANES), F32)] * 3,
        compiler_params=_params(60),
    )(qkv, qkv, qkv, delta, lse, dout)


def _dilated_finish(grads, cos, sin):
    s = grads[0].shape[0]
    spec = pl.BlockSpec((TM, D_DIL), lambda i: (i, 0))
    tab = pl.BlockSpec((TM, LANES), lambda i: (i, 0))

    def body(dq_ref, dk_ref, dv_ref, c_ref, s_ref, oq_ref, ok_ref, ov_ref):
        for src, dst, rotated in ((dq_ref, oq_ref, True), (dk_ref, ok_ref, True), (dv_ref, ov_ref, False)):
            for c in range(D_DIL // LANES):
                lanes = slice(c * LANES, (c + 1) * LANES)
                piece = src[:, lanes]
                dst[:, lanes] = (_rotate(piece, c_ref[...], -s_ref[...]) if rotated else piece).astype(BF16)

    return pl.pallas_call(
        body, name="dilated_finish", grid=(s // TM,),
        in_specs=[spec] * 3 + [tab, tab], out_specs=[spec] * 3,
        out_shape=[jax.ShapeDtypeStruct((s, D_DIL), BF16)] * 3,
        compiler_params=_params(32),
    )(*grads, cos, sin)


def _place():
    x, y, c = lax.axis_index("x"), lax.axis_index("y"), lax.axis_index("c")
    return x, y, c, 2 * x + y


def _chip(k, c):
    return (k >> 1, k & 1, c)


def _half(ref, h):
    n = ref.shape[0] // 2
    return ref.at[pl.ds(h * n, n)]


def _all_gather(shards):
    na = len(shards)
    any_spec = pl.BlockSpec(memory_space=pl.ANY)

    def body(*refs):
        ins, outs = refs[:na], refs[na:2 * na]
        send_sem, recv_sem, local_sem = refs[2 * na:]
        x, y, c, k = _place()
        sibling = (x, y, 1 - c)
        started = []
        for a in range(na):
            cp = pltpu.make_async_copy(ins[a], outs[a].at[k], local_sem.at[a])
            cp.start()
            started.append(cp)

        def copy(a, slot, src, dst, to):
            return pltpu.make_async_remote_copy(src_ref=src, dst_ref=dst, send_sem=send_sem.at[a * 6 + slot],
                                                recv_sem=recv_sem.at[a * 6 + slot], device_id=to, device_id_type=MESH)

        sends = []
        for a in range(na):
            for j in range(1, N_CHIP):
                cp = copy(a, j - 1, _half(ins[a], c), _half(outs[a].at[k], c), _chip(k ^ j, c))
                cp.start()
                sends.append(cp)
        for j in range(1, N_CHIP):
            for a in range(na):
                landed = _half(outs[a].at[k ^ j], c)
                copy(a, j - 1, landed, landed, sibling).wait_recv()
                cp = copy(a, 2 + j, landed, landed, sibling)
                cp.start()
                sends.append(cp)
        for j in range(1, N_CHIP):
            for a in range(na):
                passed = _half(outs[a].at[k ^ j], 1 - c)
                copy(a, 2 + j, passed, passed, sibling).wait_recv()
        for cp in sends:
            cp.wait_send()
        for cp in started:
            cp.wait()

    return pl.pallas_call(
        body, name="weights_all_gather",
        in_specs=[any_spec] * na, out_specs=[any_spec] * na,
        out_shape=[jax.ShapeDtypeStruct((N_CHIP,) + a.shape, a.dtype) for a in shards],
        scratch_shapes=[pltpu.SemaphoreType.DMA((6 * na,)), pltpu.SemaphoreType.DMA((6 * na,)),
                        pltpu.SemaphoreType.DMA((na,))],
    )(*shards)


def _reduce_scatter(g, core, name):
    n, r, c = g.shape
    hr = r // 2
    once = pl.Buffered(1)
    in_specs = [pl.BlockSpec((n, hr, c), lambda i, core_ref: (0, core_ref[0], 0), pipeline_mode=once),
                pl.BlockSpec((n, hr, c), lambda i, core_ref: (0, 1 - core_ref[0], 0), pipeline_mode=once)]

    def body(core_ref, mine_ref, other_ref, out_ref, from_core, sums, sums_bf, from_chips, done, from_core2, send_sem, recv_sem):
        x, y, cc, k = _place()
        sibling = (x, y, 1 - cc)

        def copy(slot, src, dst, to):
            return pltpu.make_async_remote_copy(src_ref=src, dst_ref=dst, send_sem=send_sem.at[slot],
                                                recv_sem=recv_sem.at[slot], device_id=to, device_id_type=MESH)

        first = copy(0, other_ref, from_core, sibling)
        first.start()
        first.wait()
        total = mine_ref[...] + from_core[...]
        sums[...] = total
        sums_bf[...] = total.astype(BF16)
        sends = [copy(j, sums_bf.at[k ^ j], from_chips.at[j - 1], _chip(k ^ j, cc)) for j in range(1, N_CHIP)]
        for cp in sends:
            cp.start()
        for cp in sends:
            cp.wait()
        red = sums[k]
        for j in range(1, N_CHIP):
            red = red + from_chips[j - 1].astype(F32)
        done[...] = red
        last = copy(N_CHIP, done, from_core2, sibling)
        last.start()
        last.wait()
        row0 = pl.multiple_of(cc * hr, 8)
        row1 = pl.multiple_of((1 - cc) * hr, 8)
        out_ref[pl.ds(row0, hr), :] = red
        out_ref[pl.ds(row1, hr), :] = from_core2[...]

    grid_spec = pltpu.PrefetchScalarGridSpec(
        num_scalar_prefetch=1, grid=(1,), in_specs=in_specs,
        out_specs=pl.BlockSpec((r, c), lambda i, core_ref: (0, 0)),
        scratch_shapes=[pltpu.VMEM((n, hr, c), F32), pltpu.VMEM((n, hr, c), F32), pltpu.VMEM((n, hr, c), BF16),
                        pltpu.VMEM((N_CHIP - 1, hr, c), BF16), pltpu.VMEM((hr, c), F32), pltpu.VMEM((hr, c), F32),
                        pltpu.SemaphoreType.DMA((N_CHIP + 1,)), pltpu.SemaphoreType.DMA((N_CHIP + 1,))])
    return pl.pallas_call(
        body, name=name, grid_spec=grid_spec, out_shape=jax.ShapeDtypeStruct((r, c), F32),
        compiler_params=_params(56),
    )(core, g, g)


def _elementwise(fn, name, ins, n_out, rows):
    total, cols = ins[0].shape
    spec = pl.BlockSpec((rows, cols), lambda i: (i, 0))

    def body(*refs):
        res = fn(*[r[...] for r in refs[:len(ins)]])
        for o, v in zip(refs[len(ins):], res):
            o[...] = v

    return pl.pallas_call(
        body, name=name, grid=(total // rows,),
        in_specs=[spec] * len(ins), out_specs=[spec] * n_out,
        out_shape=[jax.ShapeDtypeStruct((total, cols), F32)] * n_out,
        compiler_params=_params(48),
    )(*ins)


def _adamw(w, g, m, v):
    m = ADAM_B1 * m + (1.0 - ADAM_B1) * g
    v = ADAM_B2 * v + (1.0 - ADAM_B2) * (g * g)
    m_hat = m / (1.0 - ADAM_B1 ** ADAM_STEP)
    v_hat = v / (1.0 - ADAM_B2 ** ADAM_STEP)
    delta = -ADAM_LR * (m_hat / (jnp.sqrt(v_hat) + ADAM_EPS) + ADAM_WD * w)
    return delta, m, v


def _reduce_and_update(grads, weights, moms, vels):
    core = lax.axis_index("c").astype(jnp.int32).reshape(1)
    full = [_reduce_scatter(g, core, f"grads_reduce_scatter_{a}") for a, g in enumerate(grads)]
    out = []
    for a, (g, w, m, v) in enumerate(zip(full, weights, moms, vels)):
        rows = g.shape[0] // 2
        out.append((g,) + tuple(_elementwise(lambda gg, ww, mm, vv: _adamw(ww, gg, mm, vv), f"adamw_{a}", [g, w, m, v], 3, rows)))
    return out


def _reduce_vectors(part, w, m, v):
    n_dev = 8

    def body(p_ref, w_ref, m_ref, v_ref, g_ref, d_ref, nm_ref, nv_ref, buf, send_sem, recv_sem):
        x, y, c, _ = _place()
        me = 4 * x + 2 * y + c
        buf[me] = p_ref[...]
        sends = []
        for off in range(1, n_dev):
            peer = me ^ off
            cp = pltpu.make_async_remote_copy(src_ref=p_ref, dst_ref=buf.at[me], send_sem=send_sem.at[off - 1],
                                              recv_sem=recv_sem.at[off - 1], device_id=(peer >> 2, (peer >> 1) & 1, peer & 1),
                                              device_id_type=MESH)
            cp.start()
            sends.append(cp)
        for off in range(1, n_dev):
            peer = me ^ off
            pltpu.make_async_remote_copy(src_ref=p_ref, dst_ref=buf.at[peer], send_sem=send_sem.at[off - 1],
                                         recv_sem=recv_sem.at[off - 1], device_id=(peer >> 2, (peer >> 1) & 1, peer & 1),
                                         device_id_type=MESH).wait_recv()
        for cp in sends:
            cp.wait_send()
        g = buf[0]
        for d in range(1, n_dev):
            g = g + buf[d]
        g_ref[...] = g
        delta, nm, nv = _adamw(w_ref[...], g, m_ref[...], v_ref[...])
        d_ref[...] = delta
        nm_ref[...] = nm
        nv_ref[...] = nv

    vm = pl.BlockSpec(memory_space=pltpu.VMEM)
    return pl.pallas_call(
        body, name="gains_all_reduce",
        in_specs=[vm] * 4, out_specs=[vm] * 4,
        out_shape=[jax.ShapeDtypeStruct(part.shape, F32)] * 4,
        scratch_shapes=[pltpu.VMEM((n_dev,) + part.shape, F32), pltpu.SemaphoreType.DMA((n_dev - 1,)),
                        pltpu.SemaphoreType.DMA((n_dev - 1,))],
    )(part, w, m, v)


def _pad_row(a):
    a = a.reshape(1, -1)
    return jnp.pad(a, ((0, 0), (0, D_MODEL - a.shape[1])))


def kernel(x, ffn1_norm, ffn1_w_gate, ffn1_w_up, ffn1_w_down, mix_norm, w_in, sb_out_norm, dil_out_norm, w_out, ffn2_norm, ffn2_w_gate, ffn2_w_up, ffn2_w_down, final_norm, loss_target, m_ffn1_norm, m_ffn1_w_gate, m_ffn1_w_up, m_ffn1_w_down, m_mix_norm, m_w_in, m_sb_out_norm, m_dil_out_norm, m_w_out, m_ffn2_norm, m_ffn2_w_gate, m_ffn2_w_up, m_ffn2_w_down, m_final_norm, v_ffn1_norm, v_ffn1_w_gate, v_ffn1_w_up, v_ffn1_w_down, v_mix_norm, v_w_in, v_sb_out_norm, v_dil_out_norm, v_w_out, v_ffn2_norm, v_ffn2_w_gate, v_ffn2_w_up, v_ffn2_w_down, v_final_norm):
    x = x[0]
    target = loss_target[0]
    s = x.shape[0]
    gf = final_norm.reshape(1, D_MODEL)
    cos, sin = _rope_tables(s)

    shards = [ffn1_w_gate, ffn1_w_up, ffn1_w_down, w_in, w_out, ffn2_w_gate, ffn2_w_up, ffn2_w_down]
    wg1, wu1, wd1, win, wout, wg2, wu2, wd2 = _all_gather([w[0].astype(BF16) for w in shards])
    wout = wout.reshape(D_MODEL, D_MODEL)

    x1, hm, saved1 = _ffn1_fwd(x, ffn1_norm, mix_norm, (wg1, wu1), wd1)
    qkv = _proj_fwd(hm, win, cos, sin)
    o_sb = _sb_fwd(qkv)
    o_dl, lse = _dilated_fwd(qkv)
    x2 = _outproj_fwd(o_sb, o_dl, sb_out_norm, dil_out_norm, x1, wout)
    dx3, st_final, saved2 = _ffn2_fwd_loss(x2, ffn2_norm, gf, target, (wg2, wu2), wd2)

    dx2, dwg2, dwu2, dwd2, st_ffn2 = _ffn_bwd(x2, ffn2_norm, dx3, saved2, (wg2, wu2), wd2, 1)
    do_sb, do_dl, delta_dl, dwout, st_out = _outproj_bwd(dx2, o_sb, o_dl, sb_out_norm, dil_out_norm, wout)
    dq_sb, dk_sb, dv_sb = _sb_bwd(qkv, o_sb, do_sb)
    dq_dl, dk_dl, dv_dl = _dilated_finish(_dilated_bwd(qkv, delta_dl, lse, do_dl), cos, sin)
    dqkv = jnp.concatenate([dq_sb, dk_sb, dv_sb, dq_dl, dk_dl, dv_dl], axis=1)
    dx1, dwin, st_mix = _proj_bwd(x1, mix_norm, dqkv, win, dx2)
    grad_x, dwg1, dwu1, dwd1, st_ffn1 = _ffn_bwd(x, ffn1_norm, dx1, saved1, (wg1, wu1), wd1, 0)

    names = ["ffn1_w_gate", "ffn1_w_up", "ffn1_w_down", "w_in", "w_out", "ffn2_w_gate", "ffn2_w_up", "ffn2_w_down"]
    grads = [dwg1, dwu1, dwd1, dwin, dwout.reshape(N_CHIP, OUTB, D_MODEL), dwg2, dwu2, dwd2]
    weights = [ffn1_w_gate[0], ffn1_w_up[0], ffn1_w_down[0], w_in[0], w_out[0], ffn2_w_gate[0], ffn2_w_up[0], ffn2_w_down[0]]
    moms = [m_ffn1_w_gate[0], m_ffn1_w_up[0], m_ffn1_w_down[0], m_w_in[0], m_w_out[0], m_ffn2_w_gate[0], m_ffn2_w_up[0], m_ffn2_w_down[0]]
    vels = [v_ffn1_w_gate[0], v_ffn1_w_up[0], v_ffn1_w_down[0], v_w_in[0], v_w_out[0], v_ffn2_w_gate[0], v_ffn2_w_up[0], v_ffn2_w_down[0]]
    mats = {n: tuple(t[None] for t in r) for n, r in zip(names, _reduce_and_update(grads, weights, moms, vels))}

    vec_names = ["ffn1_norm", "mix_norm", "sb_out_norm", "dil_out_norm", "ffn2_norm", "final_norm"]
    part = jnp.concatenate([st_ffn1[0:1], st_mix[0:1], _pad_row(st_out[0]), _pad_row(st_out[1]), st_ffn2[0:1],
                            st_final[0:1], st_final[1:2], jnp.zeros((1, D_MODEL), F32)], axis=0)
    pack = lambda arrs: jnp.concatenate([_pad_row(a) for a in arrs] + [jnp.zeros((2, D_MODEL), F32)], axis=0)
    g_vec, d_vec, m_vec, v_vec = _reduce_vectors(
        part,
        pack([ffn1_norm, mix_norm, sb_out_norm, dil_out_norm, ffn2_norm, final_norm]),
        pack([m_ffn1_norm, m_mix_norm, m_sb_out_norm, m_dil_out_norm, m_ffn2_norm, m_final_norm]),
        pack([v_ffn1_norm, v_mix_norm, v_sb_out_norm, v_dil_out_norm, v_ffn2_norm, v_final_norm]))
    like = {"ffn1_norm": ffn1_norm, "mix_norm": mix_norm, "sb_out_norm": sb_out_norm, "dil_out_norm": dil_out_norm,
            "ffn2_norm": ffn2_norm, "final_norm": final_norm}
    vecs = {n: tuple(t[i, :like[n].size].reshape(like[n].shape) for t in (g_vec, d_vec, m_vec, v_vec))
            for i, n in enumerate(vec_names)}
    loss = 0.5 * jnp.sum(g_vec[6]) / D_MODEL

    order = ["ffn1_norm", "ffn1_w_gate", "ffn1_w_up", "ffn1_w_down", "mix_norm", "w_in", "sb_out_norm", "dil_out_norm",
             "w_out", "ffn2_norm", "ffn2_w_gate", "ffn2_w_up", "ffn2_w_down", "final_norm"]
    both = {**mats, **vecs}
    return (loss, grad_x[None], *[both[n][0] for n in order], *[both[n][1] for n in order],
            *[both[n][2] for n in order], *[both[n][3] for n in order])
```

```python
import functools

import jax
import jax.numpy as jnp
from jax import lax
from jax.experimental import pallas as pl
from jax.experimental.pallas import tpu as pltpu

D_MODEL = 1024
D_FF = 2816
HEAD_DIM = 64
D_SB = 512
D_DIL = 512
D_IN = 3072
N_CHIP = 4
FFB = D_FF // N_CHIP
INB = D_IN // N_CHIP
OUTB = D_MODEL // N_CHIP
BLK = 128
LANES = 128
DILATIONS = (1, 4, 16)
ROPE_THETA = 10000.0
RMS_EPS = 1e-6
SCALE = HEAD_DIM ** -0.5
NEG = -1e30
DEAD = -104.0
ADAM_LR = 0.001
ADAM_B1 = 0.9
ADAM_B2 = 0.999
ADAM_EPS = 1e-08
ADAM_WD = 0.01
ADAM_STEP = 10
MESH = pl.DeviceIdType.MESH
F32 = jnp.float32
BF16 = jnp.bfloat16
TM = 512


def _params(vmem_mb):
    return pltpu.CompilerParams(vmem_limit_bytes=vmem_mb << 20)


def _dot(a, b):
    return jnp.dot(a, b, preferred_element_type=F32)


def _dot_nt(a, b):
    return lax.dot_general(a, b, (((1,), (1,)), ((), ())), preferred_element_type=F32)


def _dot_tn(a, b):
    return lax.dot_general(a, b, (((0,), (0,)), ((), ())), preferred_element_type=F32)


def _rms_fwd(x, g):
    r = lax.rsqrt(jnp.mean(x * x, axis=-1, keepdims=True) + RMS_EPS)
    xh = x * r
    return xh * g, xh, r


def _rms_bwd(dy, xh, r, g):
    dyg = dy * g
    dx = r * (dyg - xh * jnp.mean(dyg * xh, axis=-1, keepdims=True))
    return dx, jnp.sum(dy * xh, axis=0, keepdims=True)


def _split_bf16(a):
    hi = a.astype(BF16)
    return hi, (a - hi.astype(F32)).astype(BF16)


def _dot_split(a, b):
    hi, lo = _split_bf16(a)
    return _dot(hi, b) + _dot(lo, b)


def _ffn_weight_specs():
    return [pl.BlockSpec((None, D_MODEL, FFB), lambda i, j: (j, 0, 0)),
            pl.BlockSpec((None, D_MODEL, FFB), lambda i, j: (j, 0, 0)),
            pl.BlockSpec((None, FFB, D_MODEL), lambda i, j: (j, 0, 0))]


def _ffn_saved(s):
    hidden = jax.ShapeDtypeStruct((N_CHIP, s, FFB), BF16)
    hid = pl.BlockSpec((None, TM, FFB), lambda i, j: (j, i, 0))
    row = pl.BlockSpec((TM, D_MODEL), lambda i, j: (i, 0))
    return [row, hid, hid, hid], [jax.ShapeDtypeStruct((s, D_MODEL), BF16), hidden, hidden, hidden]


def _ffn_accumulate(h_ref, acc_scr, wg_ref, wu_ref, wd_ref, a_ref, b_ref, act_ref):
    h = h_ref[...]
    a = _dot(h, wg_ref[...])
    b = _dot(h, wu_ref[...])
    act = ((a * jax.nn.sigmoid(a)) * b).astype(BF16)
    a_ref[...] = a.astype(BF16)
    b_ref[...] = b.astype(BF16)
    act_ref[...] = act
    acc_scr[...] += _dot(act, wd_ref[...])


def _ffn1_fwd(x, g1, gmix, gu, wd):
    s = x.shape[0]
    row = pl.BlockSpec((TM, D_MODEL), lambda i, j: (i, 0))
    vec = pl.BlockSpec((1, D_MODEL), lambda i, j: (0, 0))
    saved_specs, saved_shapes = _ffn_saved(s)

    def body(x_ref, g_ref, gm_ref, wg_ref, wu_ref, wd_ref, x1_ref, hm_ref, h_ref, a_ref, b_ref, act_ref, acc_scr):
        j = pl.program_id(1)

        @pl.when(j == 0)
        def _():
            h, _, _ = _rms_fwd(x_ref[...], g_ref[...])
            h_ref[...] = h.astype(BF16)
            acc_scr[...] = jnp.zeros_like(acc_scr)

        _ffn_accumulate(h_ref, acc_scr, wg_ref, wu_ref, wd_ref, a_ref, b_ref, act_ref)

        @pl.when(j == N_CHIP - 1)
        def _():
            x1 = x_ref[...] + 0.5 * acc_scr[...]
            x1_ref[...] = x1
            hm, _, _ = _rms_fwd(x1, gm_ref[...])
            hm_ref[...] = hm.astype(BF16)

    x1, hm, *saved = pl.pallas_call(
        body, name="ffn1_fwd", grid=(s // TM, N_CHIP),
        in_specs=[row, vec, vec] + _ffn_weight_specs(),
        out_specs=[row, row] + saved_specs,
        out_shape=[jax.ShapeDtypeStruct((s, D_MODEL), F32), jax.ShapeDtypeStruct((s, D_MODEL), BF16)] + saved_shapes,
        scratch_shapes=[pltpu.VMEM((TM, D_MODEL), F32)],
        compiler_params=_params(56),
    )(x, g1, gmix, gu[0], gu[1], wd)
    return x1, hm, saved


def _ffn2_fwd_loss(x2, g2, gf, target, gu, wd):
    s = x2.shape[0]
    row = pl.BlockSpec((TM, D_MODEL), lambda i, j: (i, 0))
    vec = pl.BlockSpec((1, D_MODEL), lambda i, j: (0, 0))
    stat = pl.BlockSpec((8, D_MODEL), lambda i, j: (0, 0))
    saved_specs, saved_shapes = _ffn_saved(s)

    def body(x_ref, g_ref, gf_ref, t_ref, wg_ref, wu_ref, wd_ref, dx_ref, st_ref, h_ref, a_ref, b_ref, act_ref, acc_scr):
        i, j = pl.program_id(0), pl.program_id(1)

        @pl.when((i == 0) & (j == 0))
        def _():
            st_ref[...] = jnp.zeros_like(st_ref)

        @pl.when(j == 0)
        def _():
            h, _, _ = _rms_fwd(x_ref[...], g_ref[...])
            h_ref[...] = h.astype(BF16)
            acc_scr[...] = jnp.zeros_like(acc_scr)

        _ffn_accumulate(h_ref, acc_scr, wg_ref, wu_ref, wd_ref, a_ref, b_ref, act_ref)

        @pl.when(j == N_CHIP - 1)
        def _():
            x3 = x_ref[...] + 0.5 * acc_scr[...]
            y, xh, r = _rms_fwd(x3, gf_ref[...])
            err = y - t_ref[...]
            dx, dg = _rms_bwd(err * (1.0 / D_MODEL), xh, r, gf_ref[...])
            dx_ref[...] = dx
            st_ref[0:1, :] += dg
            st_ref[1:2, :] += jnp.sum(err * err, axis=0, keepdims=True)

    dx3, st, *saved = pl.pallas_call(
        body, name="ffn2_fwd_loss", grid=(s // TM, N_CHIP),
        in_specs=[row, vec, vec, row] + _ffn_weight_specs(),
        out_specs=[row, stat] + saved_specs,
        out_shape=[jax.ShapeDtypeStruct((s, D_MODEL), F32), jax.ShapeDtypeStruct((8, D_MODEL), F32)] + saved_shapes,
        scratch_shapes=[pltpu.VMEM((TM, D_MODEL), F32)],
        compiler_params=_params(56),
    )(x2, g2, gf, target, gu[0], gu[1], wd)
    return dx3, st, saved


def _ffn_bwd(xin, g, dy, saved, gu, wd, f):
    s = xin.shape[0]
    hb, gate, up, act = saved
    row = pl.BlockSpec((TM, D_MODEL), lambda i, j: (i, 0))
    vec = pl.BlockSpec((1, D_MODEL), lambda i, j: (0, 0))
    stat = pl.BlockSpec((8, D_MODEL), lambda i, j: (0, 0))
    hid = pl.BlockSpec((None, TM, FFB), lambda i, j: (j, i, 0))

    def body(x_ref, g_ref, dy_ref, a_ref, b_ref, wg_ref, wu_ref, wd_ref, out_ref, dyh_ref, da_ref, db_ref, st_ref, dh_scr):
        i, j = pl.program_id(0), pl.program_id(1)

        @pl.when((i == 0) & (j == 0))
        def _():
            st_ref[...] = jnp.zeros_like(st_ref)

        @pl.when(j == 0)
        def _():
            dyh_ref[...] = (0.5 * dy_ref[...]).astype(BF16)
            dh_scr[...] = jnp.zeros_like(dh_scr)

        a = a_ref[...].astype(F32)
        b = b_ref[...].astype(F32)
        sg = jax.nn.sigmoid(a)
        dact = _dot_nt(dyh_ref[...], wd_ref[...])
        dab = (dact * b * (sg * (1.0 + a * (1.0 - sg)))).astype(BF16)
        dbb = (dact * (a * sg)).astype(BF16)
        da_ref[...] = dab
        db_ref[...] = dbb
        dh_scr[...] += _dot_nt(dab, wg_ref[...]) + _dot_nt(dbb, wu_ref[...])

        @pl.when(j == N_CHIP - 1)
        def _():
            _, xh, r = _rms_fwd(x_ref[...], g_ref[...])
            dx, dg = _rms_bwd(dh_scr[...], xh, r, g_ref[...])
            out_ref[...] = dy_ref[...] + dx
            st_ref[0:1, :] += dg

    hidden = jax.ShapeDtypeStruct((N_CHIP, s, FFB), BF16)
    dx, dyh, da, db, st = pl.pallas_call(
        body, name=f"ffn{f + 1}_bwd_dx", grid=(s // TM, N_CHIP),
        in_specs=[row, vec, row, hid, hid] + _ffn_weight_specs(),
        out_specs=[row, row, hid, hid, stat],
        out_shape=[jax.ShapeDtypeStruct((s, D_MODEL), F32), jax.ShapeDtypeStruct((s, D_MODEL), BF16),
                   hidden, hidden, jax.ShapeDtypeStruct((8, D_MODEL), F32)],
        scratch_shapes=[pltpu.VMEM((TM, D_MODEL), F32)],
        compiler_params=_params(56),
    )(xin, g, dy, gate, up, gu[0], gu[1], wd)

    tok = pl.BlockSpec((TM, D_MODEL), lambda j, i: (i, 0))
    hid2 = pl.BlockSpec((None, TM, FFB), lambda j, i: (j, i, 0))
    gspecs = [pl.BlockSpec((None, D_MODEL, FFB), lambda j, i: (j, 0, 0)),
              pl.BlockSpec((None, D_MODEL, FFB), lambda j, i: (j, 0, 0)),
              pl.BlockSpec((None, FFB, D_MODEL), lambda j, i: (j, 0, 0))]

    def wbody(h_ref, dyh_ref, da_ref, db_ref, act_ref, dwg_ref, dwu_ref, dwd_ref):
        @pl.when(pl.program_id(1) == 0)
        def _():
            dwg_ref[...] = jnp.zeros_like(dwg_ref)
            dwu_ref[...] = jnp.zeros_like(dwu_ref)
            dwd_ref[...] = jnp.zeros_like(dwd_ref)

        hb = h_ref[...]
        dwg_ref[...] += _dot_tn(hb, da_ref[...])
        dwu_ref[...] += _dot_tn(hb, db_ref[...])
        dwd_ref[...] += _dot_tn(act_ref[...], dyh_ref[...])

    dwg, dwu, dwd = pl.pallas_call(
        wbody, name=f"ffn{f + 1}_bwd_dw", grid=(N_CHIP, s // TM),
        in_specs=[tok, tok, hid2, hid2, hid2], out_specs=gspecs,
        out_shape=[jax.ShapeDtypeStruct((N_CHIP, D_MODEL, FFB), F32),
                   jax.ShapeDtypeStruct((N_CHIP, D_MODEL, FFB), F32),
                   jax.ShapeDtypeStruct((N_CHIP, FFB, D_MODEL), F32)],
        compiler_params=_params(48),
    )(hb, dyh, da, db, act)
    return dx, dwg, dwu, dwd, st


def _rope_tables(s):
    half = HEAD_DIM // 2
    inv_freq = ROPE_THETA ** (-jnp.arange(half, dtype=F32) / half)
    ang = jnp.arange(s).astype(F32)[:, None] * inv_freq[None, :]
    cos, sin = jnp.cos(ang), jnp.sin(ang)
    cos2 = jnp.concatenate([cos, cos], axis=-1)
    sin2 = jnp.concatenate([-sin, sin], axis=-1)
    return jnp.tile(cos2, (1, LANES // HEAD_DIM)), jnp.tile(sin2, (1, LANES // HEAD_DIM))


def _rotate(t, cos, sin_signed):
    lane = lax.broadcasted_iota(jnp.int32, t.shape, 1)
    first = (lane % HEAD_DIM) < (HEAD_DIM // 2)
    partner = jnp.where(first, pltpu.roll(t, LANES - HEAD_DIM // 2, 1), pltpu.roll(t, HEAD_DIM // 2, 1))
    return t * cos + partner * sin_signed


def _proj_fwd(hm, win, cos, sin):
    s = hm.shape[0]
    n_sub = INB // LANES
    first_rot, last_rot = (3 * D_SB) // LANES, (3 * D_SB + 2 * D_DIL) // LANES

    def body(h_ref, w_ref, c_ref, s_ref, o_ref):
        j = pl.program_id(1)
        r = _dot(h_ref[...], w_ref[...])
        for c in range(n_sub):
            t = r[:, c * LANES:(c + 1) * LANES]
            col = j * n_sub + c
            rot = (col >= first_rot) & (col < last_rot)
            lanes = slice(c * LANES, (c + 1) * LANES)

            @pl.when(rot)
            def _():
                o_ref[:, lanes] = _rotate(t, c_ref[...], s_ref[...]).astype(BF16)

            @pl.when(jnp.logical_not(rot))
            def _():
                o_ref[:, lanes] = t.astype(BF16)

    return pl.pallas_call(
        body, name="proj_fwd", grid=(s // TM, N_CHIP),
        in_specs=[pl.BlockSpec((TM, D_MODEL), lambda i, j: (i, 0)),
                  pl.BlockSpec((None, D_MODEL, INB), lambda i, j: (j, 0, 0)),
                  pl.BlockSpec((TM, LANES), lambda i, j: (i, 0)),
                  pl.BlockSpec((TM, LANES), lambda i, j: (i, 0))],
        out_specs=pl.BlockSpec((TM, INB), lambda i, j: (i, j)),
        out_shape=jax.ShapeDtypeStruct((s, D_IN), BF16),
        compiler_params=_params(32),
    )(hm, win, cos, sin)


def _proj_bwd(x1, gmix, dqkv, win, dx2):
    s = x1.shape[0]
    row = pl.BlockSpec((TM, D_MODEL), lambda i, j: (i, 0))
    vec = pl.BlockSpec((1, D_MODEL), lambda i, j: (0, 0))

    def body(x_ref, g_ref, dq_ref, w_ref, dx2_ref, out_ref, dw_ref, st_ref, h_scr, dh_scr):
        i, j = pl.program_id(0), pl.program_id(1)

        @pl.when((i == 0) & (j == 0))
        def _():
            st_ref[...] = jnp.zeros_like(st_ref)
            dw_ref[...] = jnp.zeros_like(dw_ref)

        @pl.when(j == 0)
        def _():
            h, _, _ = _rms_fwd(x_ref[...], g_ref[...])
            h_scr[...] = h.astype(BF16)
            dh_scr[...] = jnp.zeros_like(dh_scr)

        dq = dq_ref[...]
        dw_ref[j] += _dot_tn(h_scr[...], dq)
        dh_scr[...] += _dot_nt(dq, w_ref[...])

        @pl.when(j == N_CHIP - 1)
        def _():
            _, xh, r = _rms_fwd(x_ref[...], g_ref[...])
            dx, dg = _rms_bwd(dh_scr[...], xh, r, g_ref[...])
            out_ref[...] = dx2_ref[...] + dx
            st_ref[0:1, :] += dg

    return pl.pallas_call(
        body, name="proj_bwd", grid=(s // TM, N_CHIP),
        in_specs=[row, vec, pl.BlockSpec((TM, INB), lambda i, j: (i, j)),
                  pl.BlockSpec((None, D_MODEL, INB), lambda i, j: (j, 0, 0)), row],
        out_specs=[row, pl.BlockSpec((N_CHIP, D_MODEL, INB), lambda i, j: (0, 0, 0)),
                   pl.BlockSpec((8, D_MODEL), lambda i, j: (0, 0))],
        out_shape=[jax.ShapeDtypeStruct((s, D_MODEL), F32),
                   jax.ShapeDtypeStruct((N_CHIP, D_MODEL, INB), F32),
                   jax.ShapeDtypeStruct((8, D_MODEL), F32)],
        scratch_shapes=[pltpu.VMEM((TM, D_MODEL), BF16), pltpu.VMEM((TM, D_MODEL), F32)],
        compiler_params=_params(56),
    )(x1, gmix, dqkv, win, dx2)


def _outproj_fwd(o_sb, o_dl, g_sb, g_dl, x1, wout):
    s = x1.shape[0]
    half = pl.BlockSpec((TM, D_SB), lambda i: (i, 0))
    row = pl.BlockSpec((TM, D_MODEL), lambda i: (i, 0))
    vec = pl.BlockSpec((1, D_SB), lambda i: (0, 0))

    def body(a_ref, b_ref, ga_ref, gb_ref, x_ref, w_ref, o_ref):
        ma, _, _ = _rms_fwd(a_ref[...], ga_ref[...])
        mb, _, _ = _rms_fwd(b_ref[...], gb_ref[...])
        o_ref[...] = (x_ref[...] + _dot(ma.astype(BF16), w_ref[0:D_SB, :])
                      + _dot(mb.astype(BF16), w_ref[D_SB:D_MODEL, :]))

    return pl.pallas_call(
        body, name="outproj_fwd", grid=(s // TM,),
        in_specs=[half, half, vec, vec, row, pl.BlockSpec((D_MODEL, D_MODEL), lambda i: (0, 0))],
        out_specs=row, out_shape=jax.ShapeDtypeStruct((s, D_MODEL), F32),
        compiler_params=_params(32),
    )(o_sb, o_dl, g_sb, g_dl, x1, wout)


def _outproj_bwd(dx2, o_sb, o_dl, g_sb, g_dl, wout):
    s = dx2.shape[0]
    half = pl.BlockSpec((TM, D_SB), lambda i: (i, 0))
    row = pl.BlockSpec((TM, D_MODEL), lambda i: (i, 0))
    vec = pl.BlockSpec((1, D_SB), lambda i: (0, 0))
    full = pl.BlockSpec((D_MODEL, D_MODEL), lambda i: (0, 0))

    def body(dy_ref, a_ref, b_ref, ga_ref, gb_ref, w_ref, da_ref, db_ref, dl_ref, dw_ref, st_ref):
        @pl.when(pl.program_id(0) == 0)
        def _():
            dw_ref[...] = jnp.zeros_like(dw_ref)
            st_ref[...] = jnp.zeros_like(st_ref)

        dy = dy_ref[...].astype(BF16)
        dm = _dot_nt(dy, w_ref[...])
        ma, xa, ra = _rms_fwd(a_ref[...], ga_ref[...])
        mb, xb, rb = _rms_fwd(b_ref[...], gb_ref[...])
        dw_ref[0:D_SB, :] += _dot_tn(ma.astype(BF16), dy)
        dw_ref[D_SB:D_MODEL, :] += _dot_tn(mb.astype(BF16), dy)
        da, dga = _rms_bwd(dm[:, 0:D_SB], xa, ra, ga_ref[...])
        db, dgb = _rms_bwd(dm[:, D_SB:D_MODEL], xb, rb, gb_ref[...])
        da_ref[...] = da
        db_ref[...] = db
        r = lax.broadcasted_iota(jnp.int32, (LANES, LANES), 0) >= HEAD_DIM
        c = lax.broadcasted_iota(jnp.int32, (LANES, LANES), 1) >= HEAD_DIM
        same_head = jnp.where(r == c, 1.0, 0.0).astype(BF16)
        prod = db * b_ref[...]
        for k in range(D_DIL // LANES):
            lanes = slice(k * LANES, (k + 1) * LANES)
            dl_ref[:, lanes] = _dot_split(prod[:, lanes], same_head)
        st_ref[0:1, :] += dga
        st_ref[1:2, :] += dgb

    return pl.pallas_call(
        body, name="outproj_bwd", grid=(s // TM,),
        in_specs=[row, half, half, vec, vec, full],
        out_specs=[half, half, half, full, pl.BlockSpec((8, D_SB), lambda i: (0, 0))],
        out_shape=[jax.ShapeDtypeStruct((s, D_SB), F32), jax.ShapeDtypeStruct((s, D_SB), F32),
                   jax.ShapeDtypeStruct((s, D_DIL), F32),
                   jax.ShapeDtypeStruct((D_MODEL, D_MODEL), F32), jax.ShapeDtypeStruct((8, D_SB), F32)],
        compiler_params=_params(48),
    )(dx2, o_sb, o_dl, g_sb, g_dl, wout)


def _head_masks():
    lane = lax.broadcasted_iota(jnp.int32, (BLK, LANES), 1)
    return [lane < HEAD_DIM, lane >= HEAD_DIM]


def _keep(mask, a):
    return a * jnp.where(mask, 1.0, 0.0).astype(a.dtype)


def _suffix_matrices():
    r = lax.broadcasted_iota(jnp.int32, (BLK, BLK), 0)
    c = lax.broadcasted_iota(jnp.int32, (BLK, BLK), 1)
    ones = jnp.ones((BLK, BLK), BF16)
    excl = jnp.concatenate([(r > c).astype(BF16), ones], axis=1)
    incl = jnp.concatenate([(r >= c).astype(BF16), ones], axis=1)
    return excl, incl


def _blk(i):
    return pl.ds(pl.multiple_of(i * BLK, BLK), BLK)


def _alive(carry_m):
    return (jnp.max(carry_m) > DEAD).astype(jnp.int32)


def _more_keys(last, carry):
    return (carry[0] * SB_KB <= last) & (carry[1] > 0)


def _stack_heads(a):
    masks = _head_masks()
    return jnp.concatenate([_keep(masks[0], a), _keep(masks[1], a)], axis=0)


def _unstack_heads(a2):
    return jnp.where(_head_masks()[0], a2[:BLK], a2[BLK:])


def _head_rowsum(a):
    masks = _head_masks()
    return jnp.concatenate([jnp.sum(jnp.where(m, a, 0.0), axis=1, keepdims=True) for m in masks], axis=0)


SB_QB = 2
SB_ROWS = SB_QB * 2 * BLK
SB_KB = 4
PAST_START = 1 << 30


def _sb_rows(ref, i0, cast=None):
    tiles = [ref[_blk(i0 + t), :] for t in range(SB_QB)]
    return jnp.concatenate([_stack_heads(t if cast is None else t.astype(cast)) for t in tiles], axis=0)


def _sb_scores(q2, k, i, j, carry_m, u_excl):
    r = lax.broadcasted_iota(jnp.int32, (SB_ROWS, BLK), 0)
    row = (r & (BLK - 1)) + ((r >> 8) << 7)
    col = lax.broadcasted_iota(jnp.int32, (SB_ROWS, BLK), 1)
    valid = (jnp.where(j >= 0, j * BLK, PAST_START) + col) < (i * BLK + row)
    z = _dot_nt(q2, k) * SCALE
    sp = jnp.maximum(z, 0.0) + jnp.log(1.0 + jnp.exp(-jnp.abs(z)))
    log_stay = jnp.where(valid, -sp, 0.0)
    log_beta = z - sp
    sums = _dot_split(log_stay, u_excl)
    later = carry_m + sums[:, :BLK]
    w = jnp.where(valid, jnp.exp(log_beta + later), 0.0)
    return valid, log_beta, w, carry_m + sums[:, BLK:]


def _sb_fwd(qkv):
    s = qkv.shape[0]
    nq = s // BLK
    pairs = D_SB // LANES
    col = lambda off: pl.BlockSpec((s, LANES), lambda p: (0, off + p))

    def body(q_ref, k_ref, v_ref, o_ref):
        u_excl, _ = _suffix_matrices()
        zero = jnp.zeros((SB_ROWS, LANES), F32)

        def q_block(ib, _):
            i = ib * SB_QB
            last = i + SB_QB - 1
            q2 = _sb_rows(q_ref, i)

            def k_block(carry):
                jj, _, carry_m, acc = carry
                for t in range(SB_KB):
                    j = last - jj * SB_KB - t
                    at = _blk(jnp.maximum(j, 0))
                    _, _, w, carry_m = _sb_scores(q2, k_ref[at, :], i, j, carry_m, u_excl)
                    acc = acc + _dot(w.astype(BF16), v_ref[at, :])
                return jj + 1, _alive(carry_m), carry_m, acc

            _, _, _, acc = lax.while_loop(functools.partial(_more_keys, last), k_block,
                                          (jnp.int32(0), jnp.int32(1), zero, zero))
            for t in range(SB_QB):
                o_ref[_blk(i + t), :] = _unstack_heads(acc[2 * BLK * t:2 * BLK * (t + 1)])
            return 0

        lax.fori_loop(0, nq // SB_QB, q_block, 0)

    return pl.pallas_call(
        body, name="sb_fwd", grid=(pairs,),
        in_specs=[col(0), col(pairs), col(2 * pairs)],
        out_specs=pl.BlockSpec((s, LANES), lambda p: (0, p)),
        out_shape=jax.ShapeDtypeStruct((s, D_SB), F32),
        compiler_params=_params(48),
    )(qkv, qkv, qkv)


def _sb_bwd(qkv, o_sb, do_sb):
    s = qkv.shape[0]
    nq = s // BLK
    pairs = D_SB // LANES
    col = lambda off: pl.BlockSpec((s, LANES), lambda p: (0, off + p))
    own = pl.BlockSpec((s, LANES), lambda p: (0, p))

    def body(q_ref, k_ref, v_ref, o_ref, do_ref, dq_ref, dk_ref, dv_ref, dk_acc, dv_acc):
        u_excl, u_incl = _suffix_matrices()
        zero = jnp.zeros((SB_ROWS, LANES), F32)
        dk_acc[...] = jnp.zeros_like(dk_acc)
        dv_acc[...] = jnp.zeros_like(dv_acc)

        def q_block(ib, _):
            i = ib * SB_QB
            last = i + SB_QB - 1
            q2 = _sb_rows(q_ref, i)
            do2 = _sb_rows(do_ref, i, BF16)
            totals = [_head_rowsum(do_ref[_blk(i + t), :].astype(BF16).astype(F32) * o_ref[_blk(i + t), :])
                      for t in range(SB_QB)]
            total = jnp.broadcast_to(jnp.concatenate(totals, axis=0), (SB_ROWS, BLK))

            def k_block(carry):
                jj, _, carry_m, carry_g, dq = carry
                for t in range(SB_KB):
                    j = last - jj * SB_KB - t
                    at = _blk(jnp.maximum(j, 0))
                    k = k_ref[at, :]
                    valid, log_beta, w, carry_m = _sb_scores(q2, k, i, j, carry_m, u_excl)
                    wb = w.astype(BF16)
                    g = wb.astype(F32) * _dot_nt(do2, v_ref[at, :])
                    sums = _dot_split(g, u_incl)
                    before = total - (carry_g + sums[:, :BLK])
                    dz = jnp.where(valid, g - jnp.exp(log_beta) * (g + before), 0.0)
                    dzb = (dz * SCALE).astype(BF16)
                    dk_acc[at, :] += _dot_tn(dzb, q2)
                    dv_acc[at, :] += _dot_tn(wb, do2)
                    carry_g = carry_g + sums[:, BLK:]
                    dq = dq + _dot(dzb, k)
                return jj + 1, _alive(carry_m), carry_m, carry_g, dq

            _, _, _, _, dq = lax.while_loop(functools.partial(_more_keys, last), k_block,
                                            (jnp.int32(0), jnp.int32(1), zero, zero, zero))
            for t in range(SB_QB):
                dq_ref[_blk(i + t), :] = _unstack_heads(dq[2 * BLK * t:2 * BLK * (t + 1)]).astype(BF16)
            return 0

        lax.fori_loop(0, nq // SB_QB, q_block, 0)
        dk_ref[...] = dk_acc[...].astype(BF16)
        dv_ref[...] = dv_acc[...].astype(BF16)

    return pl.pallas_call(
        body, name="sb_bwd", grid=(pairs,),
        in_specs=[col(0), col(pairs), col(2 * pairs), own, own],
        out_specs=[own, own, own],
        out_shape=[jax.ShapeDtypeStruct((s, D_SB), BF16)] * 3,
        scratch_shapes=[pltpu.VMEM((s, LANES), F32), pltpu.VMEM((s, LANES), F32)],
        compiler_params=_params(56),
    )(qkv, qkv, qkv, o_sb, do_sb)


DIL_UNROLL = 4


def _band_masks(b):
    row = lax.broadcasted_iota(jnp.int32, (2 * BLK, BLK), 0) & (BLK - 1)
    col = lax.broadcasted_iota(jnp.int32, (2 * BLK, BLK), 1)
    return col <= row, (col - row) >= jnp.where(b > 0, 0, BLK)


def _dil_tiles(qf, kf, vf, d, t, nb):
    c, b = t // nb, t % nb
    start = c + d * BLK * b
    rows = pl.ds(start, BLK, stride=d)
    prev = pl.ds(jnp.where(b > 0, start - d * BLK, start), BLK, stride=d)
    bf = lambda ref, sl: ref[sl, :].astype(BF16)
    return b, rows, prev, _stack_heads(bf(qf, rows)), bf(kf, rows), bf(kf, prev), bf(vf, rows), bf(vf, prev)


def _lanes_of_heads(col2):
    return _unstack_heads(jnp.broadcast_to(col2, (2 * BLK, LANES)))


def _dilated_fwd(qkv):
    s = qkv.shape[0]
    pairs = D_DIL // LANES
    base = (3 * D_SB) // LANES
    col = lambda off: pl.BlockSpec((s, LANES), lambda p: (0, off + p))
    own = pl.BlockSpec((s, LANES), lambda p: (0, p))

    def body(q_ref, k_ref, v_ref, acc_ref, m_ref, qf, kf, vf, l_scr):
        qf[...] = q_ref[...].astype(F32)
        kf[...] = k_ref[...].astype(F32)
        vf[...] = v_ref[...].astype(F32)
        for d in DILATIONS:
            nb = s // (d * BLK)

            def block(t, _):
                b, rows, prev, q2, kc, kp, vc, vp = _dil_tiles(qf, kf, vf, d, t, nb)
                in_cur, in_prev = _band_masks(b)
                zc = jnp.where(in_cur, _dot_nt(q2, kc) * SCALE, NEG)
                zp = jnp.where(in_prev, _dot_nt(q2, kp) * SCALE, NEG)
                m = jnp.maximum(jnp.max(zc, axis=1, keepdims=True), jnp.max(zp, axis=1, keepdims=True))
                pc, pp = jnp.exp(zc - m), jnp.exp(zp - m)
                den = jnp.sum(pc, axis=1, keepdims=True) + jnp.sum(pp, axis=1, keepdims=True)
                acc = _unstack_heads(_dot(pc.astype(BF16), vc) + _dot(pp.astype(BF16), vp))
                m_t, l_t = _lanes_of_heads(m), _lanes_of_heads(den)
                if d == DILATIONS[0]:
                    m_ref[rows, :] = m_t
                    l_scr[rows, :] = l_t
                    acc_ref[rows, :] = acc
                else:
                    m_old = m_ref[rows, :]
                    m_new = jnp.maximum(m_old, m_t)
                    keep, add = jnp.exp(m_old - m_new), jnp.exp(m_t - m_new)
                    m_ref[rows, :] = m_new
                    l_scr[rows, :] = l_scr[rows, :] * keep + l_t * add
                    acc_ref[rows, :] = acc_ref[rows, :] * keep + acc * add
                return 0

            lax.fori_loop(0, s // BLK, block, 0, unroll=DIL_UNROLL)

        def finish(i, _):
            l = l_scr[_blk(i), :]
            acc_ref[_blk(i), :] = acc_ref[_blk(i), :] / l
            m_ref[_blk(i), :] = m_ref[_blk(i), :] + jnp.log(l)
            return 0

        lax.fori_loop(0, s // BLK, finish, 0)

    return pl.pallas_call(
        body, name="dilated_fwd", grid=(pairs,),
        in_specs=[col(base), col(base + pairs), col(base + 2 * pairs)],
        out_specs=[own, own],
        out_shape=[jax.ShapeDtypeStruct((s, D_DIL), F32)] * 2,
        scratch_shapes=[pltpu.VMEM((s, LANES), F32)] * 4,
        compiler_params=_params(56),
    )(qkv, qkv, qkv)


def _stack_lanes(t):
    other = pltpu.roll(t, HEAD_DIM, 1)
    first = _head_masks()[0]
    return jnp.concatenate([jnp.where(first, t, other), jnp.where(first, other, t)], axis=0)


def _dilated_bwd(qkv, delta, lse, dout):
    s = qkv.shape[0]
    pairs = D_DIL // LANES
    base = (3 * D_SB) // LANES
    once = pl.Buffered(1)
    col = lambda off: pl.BlockSpec((s, LANES), lambda p: (0, off + p), pipeline_mode=once)
    own = pl.BlockSpec((s, LANES), lambda p: (0, p), pipeline_mode=once)
    res = pl.BlockSpec((s, LANES), lambda p: (0, p))

    def body(q_ref, k_ref, v_ref, dl_ref, l_ref, do_ref, dq_ref, dk_ref, dv_ref, qf, kf, vf):
        qf[...] = q_ref[...].astype(F32)
        kf[...] = k_ref[...].astype(F32)
        vf[...] = v_ref[...].astype(F32)
        dq_ref[...] = jnp.zeros_like(dq_ref)
        dk_ref[...] = jnp.zeros_like(dk_ref)
        dv_ref[...] = jnp.zeros_like(dv_ref)
        for d in DILATIONS:
            nb = s // (d * BLK)

            def block(t, _):
                b, rows, prev, q2, kc, kp, vc, vp = _dil_tiles(qf, kf, vf, d, t, nb)
                in_cur, in_prev = _band_masks(b)
                do2 = _stack_heads(do_ref[rows, :].astype(BF16))
                delta = _stack_lanes(dl_ref[rows, :])
                lse2 = _stack_lanes(l_ref[rows, :])
                wc = jnp.exp(jnp.where(in_cur, _dot_nt(q2, kc) * SCALE, NEG) - lse2)
                wp = jnp.exp(jnp.where(in_prev, _dot_nt(q2, kp) * SCALE, NEG) - lse2)
                dzc = (wc * (_dot_nt(do2, vc) - delta) * SCALE).astype(BF16)
                dzp = (wp * (_dot_nt(do2, vp) - delta) * SCALE).astype(BF16)
                dq_ref[rows, :] += _unstack_heads(_dot(dzc, kc) + _dot(dzp, kp))
                dk_ref[rows, :] += _dot_tn(dzc, q2)
                dk_ref[prev, :] += _dot_tn(dzp, q2)
                dv_ref[rows, :] += _dot_tn(wc.astype(BF16), do2)
                dv_ref[prev, :] += _dot_tn(wp.astype(BF16), do2)
                return 0

            lax.fori_loop(0, s // BLK, block, 0, unroll=DIL_UNROLL)

    return pl.pallas_call(
        body, name="dilated_bwd", grid=(pairs,),
        in_specs=[col(base), col(base + pairs), col(base + 2 * pairs), own, own, own],
        out_specs=[res, res, res],
        out_shape=[jax.ShapeDtypeStruct((s, D_DIL), F32)] * 3,
        scratch_shapes=[pltpu.VMEM((s, LANES), F32)] * 3,
        compiler_params=_params(60),
    )(qkv, qkv, qkv, delta, lse, dout)


def _dilated_finish(grads, cos, sin):
    s = grads[0].shape[0]
    spec = pl.BlockSpec((TM, D_DIL), lambda i: (i, 0))
    tab = pl.BlockSpec((TM, LANES), lambda i: (i, 0))

    def body(dq_ref, dk_ref, dv_ref, c_ref, s_ref, oq_ref, ok_ref, ov_ref):
        for src, dst, rotated in ((dq_ref, oq_ref, True), (dk_ref, ok_ref, True), (dv_ref, ov_ref, False)):
            for c in range(D_DIL // LANES):
                lanes = slice(c * LANES, (c + 1) * LANES)
                piece = src[:, lanes]
                dst[:, lanes] = (_rotate(piece, c_ref[...], -s_ref[...]) if rotated else piece).astype(BF16)

    return pl.pallas_call(
        body, name="dilated_finish", grid=(s // TM,),
        in_specs=[spec] * 3 + [tab, tab], out_specs=[spec] * 3,
        out_shape=[jax.ShapeDtypeStruct((s, D_DIL), BF16)] * 3,
        compiler_params=_params(32),
    )(*grads, cos, sin)


def _place():
    x, y, c = lax.axis_index("x"), lax.axis_index("y"), lax.axis_index("c")
    return x, y, c, 2 * x + y


def _chip(k, c):
    return (k >> 1, k & 1, c)


def _half(ref, h):
    n = ref.shape[0] // 2
    return ref.at[pl.ds(h * n, n)]


def _all_gather(shards):
    na = len(shards)
    any_spec = pl.BlockSpec(memory_space=pl.ANY)

    def body(*refs):
        ins, outs = refs[:na], refs[na:2 * na]
        send_sem, recv_sem, local_sem = refs[2 * na:]
        x, y, c, k = _place()
        sibling = (x, y, 1 - c)
        started = []
        for a in range(na):
            cp = pltpu.make_async_copy(ins[a], outs[a].at[k], local_sem.at[a])
            cp.start()
            started.append(cp)

        def copy(a, slot, src, dst, to):
            return pltpu.make_async_remote_copy(src_ref=src, dst_ref=dst, send_sem=send_sem.at[a * 6 + slot],
                                                recv_sem=recv_sem.at[a * 6 + slot], device_id=to, device_id_type=MESH)

        sends = []
        for a in range(na):
            for j in range(1, N_CHIP):
                cp = copy(a, j - 1, _half(ins[a], c), _half(outs[a].at[k], c), _chip(k ^ j, c))
                cp.start()
                sends.append(cp)
        for j in range(1, N_CHIP):
            for a in range(na):
                landed = _half(outs[a].at[k ^ j], c)
                copy(a, j - 1, landed, landed, sibling).wait_recv()
                cp = copy(a, 2 + j, landed, landed, sibling)
                cp.start()
                sends.append(cp)
        for j in range(1, N_CHIP):
            for a in range(na):
                passed = _half(outs[a].at[k ^ j], 1 - c)
                copy(a, 2 + j, passed, passed, sibling).wait_recv()
        for cp in sends:
            cp.wait_send()
        for cp in started:
            cp.wait()

    return pl.pallas_call(
        body, name="weights_all_gather",
        in_specs=[any_spec] * na, out_specs=[any_spec] * na,
        out_shape=[jax.ShapeDtypeStruct((N_CHIP,) + a.shape, a.dtype) for a in shards],
        scratch_shapes=[pltpu.SemaphoreType.DMA((6 * na,)), pltpu.SemaphoreType.DMA((6 * na,)),
                        pltpu.SemaphoreType.DMA((na,))],
    )(*shards)


def _reduce_scatter(g, core, name):
    n, r, c = g.shape
    hr = r // 2
    once = pl.Buffered(1)
    in_specs = [pl.BlockSpec((n, hr, c), lambda i, core_ref: (0, core_ref[0], 0), pipeline_mode=once),
                pl.BlockSpec((n, hr, c), lambda i, core_ref: (0, 1 - core_ref[0], 0), pipeline_mode=once)]

    def body(core_ref, mine_ref, other_ref, out_ref, from_core, sums, sums_bf, from_chips, done, from_core2, send_sem, recv_sem):
        x, y, cc, k = _place()
        sibling = (x, y, 1 - cc)

        def copy(slot, src, dst, to):
            return pltpu.make_async_remote_copy(src_ref=src, dst_ref=dst, send_sem=send_sem.at[slot],
                                                recv_sem=recv_sem.at[slot], device_id=to, device_id_type=MESH)

        first = copy(0, other_ref, from_core, sibling)
        first.start()
        first.wait()
        total = mine_ref[...] + from_core[...]
        sums[...] = total
        sums_bf[...] = total.astype(BF16)
        sends = [copy(j, sums_bf.at[k ^ j], from_chips.at[j - 1], _chip(k ^ j, cc)) for j in range(1, N_CHIP)]
        for cp in sends:
            cp.start()
        for cp in sends:
            cp.wait()
        red = sums[k]
        for j in range(1, N_CHIP):
            red = red + from_chips[j - 1].astype(F32)
        done[...] = red
        last = copy(N_CHIP, done, from_core2, sibling)
        last.start()
        last.wait()
        row0 = pl.multiple_of(cc * hr, 8)
        row1 = pl.multiple_of((1 - cc) * hr, 8)
        out_ref[pl.ds(row0, hr), :] = red
        out_ref[pl.ds(row1, hr), :] = from_core2[...]

    grid_spec = pltpu.PrefetchScalarGridSpec(
        num_scalar_prefetch=1, grid=(1,), in_specs=in_specs,
        out_specs=pl.BlockSpec((r, c), lambda i, core_ref: (0, 0)),
        scratch_shapes=[pltpu.VMEM((n, hr, c), F32), pltpu.VMEM((n, hr, c), F32), pltpu.VMEM((n, hr, c), BF16),
                        pltpu.VMEM((N_CHIP - 1, hr, c), BF16), pltpu.VMEM((hr, c), F32), pltpu.VMEM((hr, c), F32),
                        pltpu.SemaphoreType.DMA((N_CHIP + 1,)), pltpu.SemaphoreType.DMA((N_CHIP + 1,))])
    return pl.pallas_call(
        body, name=name, grid_spec=grid_spec, out_shape=jax.ShapeDtypeStruct((r, c), F32),
        compiler_params=_params(56),
    )(core, g, g)


def _elementwise(fn, name, ins, n_out, rows):
    total, cols = ins[0].shape
    spec = pl.BlockSpec((rows, cols), lambda i: (i, 0))

    def body(*refs):
        res = fn(*[r[...] for r in refs[:len(ins)]])
        for o, v in zip(refs[len(ins):], res):
            o[...] = v

    return pl.pallas_call(
        body, name=name, grid=(total // rows,),
        in_specs=[spec] * len(ins), out_specs=[spec] * n_out,
        out_shape=[jax.ShapeDtypeStruct((total, cols), F32)] * n_out,
        compiler_params=_params(48),
    )(*ins)


def _adamw(w, g, m, v):
    m = ADAM_B1 * m + (1.0 - ADAM_B1) * g
    v = ADAM_B2 * v + (1.0 - ADAM_B2) * (g * g)
    m_hat = m / (1.0 - ADAM_B1 ** ADAM_STEP)
    v_hat = v / (1.0 - ADAM_B2 ** ADAM_STEP)
    delta = -ADAM_LR * (m_hat / (jnp.sqrt(v_hat) + ADAM_EPS) + ADAM_WD * w)
    return delta, m, v


def _reduce_and_update(grads, weights, moms, vels):
    core = lax.axis_index("c").astype(jnp.int32).reshape(1)
    full = [_reduce_scatter(g, core, f"grads_reduce_scatter_{a}") for a, g in enumerate(grads)]
    out = []
    for a, (g, w, m, v) in enumerate(zip(full, weights, moms, vels)):
        rows = g.shape[0] // 2
        out.append((g,) + tuple(_elementwise(lambda gg, ww, mm, vv: _adamw(ww, gg, mm, vv), f"adamw_{a}", [g, w, m, v], 3, rows)))
    return out


def _reduce_vectors(part, w, m, v):
    n_dev = 8

    def body(p_ref, w_ref, m_ref, v_ref, g_ref, d_ref, nm_ref, nv_ref, buf, send_sem, recv_sem):
        x, y, c, _ = _place()
        me = 4 * x + 2 * y + c
        buf[me] = p_ref[...]
        sends = []
        for off in range(1, n_dev):
            peer = me ^ off
            cp = pltpu.make_async_remote_copy(src_ref=p_ref, dst_ref=buf.at[me], send_sem=send_sem.at[off - 1],
                                              recv_sem=recv_sem.at[off - 1], device_id=(peer >> 2, (peer >> 1) & 1, peer & 1),
                                              device_id_type=MESH)
            cp.start()
            sends.append(cp)
        for off in range(1, n_dev):
            peer = me ^ off
            pltpu.make_async_remote_copy(src_ref=p_ref, dst_ref=buf.at[peer], send_sem=send_sem.at[off - 1],
                                         recv_sem=recv_sem.at[off - 1], device_id=(peer >> 2, (peer >> 1) & 1, peer & 1),
                                         device_id_type=MESH).wait_recv()
        for cp in sends:
            cp.wait_send()
        g = buf[0]
        for d in range(1, n_dev):
            g = g + buf[d]
        g_ref[...] = g
        delta, nm, nv = _adamw(w_ref[...], g, m_ref[...], v_ref[...])
        d_ref[...] = delta
        nm_ref[...] = nm
        nv_ref[...] = nv

    vm = pl.BlockSpec(memory_space=pltpu.VMEM)
    return pl.pallas_call(
        body, name="gains_all_reduce",
        in_specs=[vm] * 4, out_specs=[vm] * 4,
        out_shape=[jax.ShapeDtypeStruct(part.shape, F32)] * 4,
        scratch_shapes=[pltpu.VMEM((n_dev,) + part.shape, F32), pltpu.SemaphoreType.DMA((n_dev - 1,)),
                        pltpu.SemaphoreType.DMA((n_dev - 1,))],
    )(part, w, m, v)


def _pad_row(a):
    a = a.reshape(1, -1)
    return jnp.pad(a, ((0, 0), (0, D_MODEL - a.shape[1])))


def kernel(x, ffn1_norm, ffn1_w_gate, ffn1_w_up, ffn1_w_down, mix_norm, w_in, sb_out_norm, dil_out_norm, w_out, ffn2_norm, ffn2_w_gate, ffn2_w_up, ffn2_w_down, final_norm, loss_target, m_ffn1_norm, m_ffn1_w_gate, m_ffn1_w_up, m_ffn1_w_down, m_mix_norm, m_w_in, m_sb_out_norm, m_dil_out_norm, m_w_out, m_ffn2_norm, m_ffn2_w_gate, m_ffn2_w_up, m_ffn2_w_down, m_final_norm, v_ffn1_norm, v_ffn1_w_gate, v_ffn1_w_up, v_ffn1_w_down, v_mix_norm, v_w_in, v_sb_out_norm, v_dil_out_norm, v_w_out, v_ffn2_norm, v_ffn2_w_gate, v_ffn2_w_up, v_ffn2_w_down, v_final_norm):
    x = x[0]
    target = loss_target[0]
    s = x.shape[0]
    gf = final_norm.reshape(1, D_MODEL)
    cos, sin = _rope_tables(s)

    shards = [ffn1_w_gate, ffn1_w_up, ffn1_w_down, w_in, w_out, ffn2_w_gate, ffn2_w_up, ffn2_w_down]
    wg1, wu1, wd1, win, wout, wg2, wu2, wd2 = _all_gather([w[0].astype(BF16) for w in shards])
    wout = wout.reshape(D_MODEL, D_MODEL)

    x1, hm, saved1 = _ffn1_fwd(x, ffn1_norm, mix_norm, (wg1, wu1), wd1)
    qkv = _proj_fwd(hm, win, cos, sin)
    o_sb = _sb_fwd(qkv)
    o_dl, lse = _dilated_fwd(qkv)
    x2 = _outproj_fwd(o_sb, o_dl, sb_out_norm, dil_out_norm, x1, wout)
    dx3, st_final, saved2 = _ffn2_fwd_loss(x2, ffn2_norm, gf, target, (wg2, wu2), wd2)

    dx2, dwg2, dwu2, dwd2, st_ffn2 = _ffn_bwd(x2, ffn2_norm, dx3, saved2, (wg2, wu2), wd2, 1)
    do_sb, do_dl, delta_dl, dwout, st_out = _outproj_bwd(dx2, o_sb, o_dl, sb_out_norm, dil_out_norm, wout)
    dq_sb, dk_sb, dv_sb = _sb_bwd(qkv, o_sb, do_sb)
    dq_dl, dk_dl, dv_dl = _dilated_finish(_dilated_bwd(qkv, delta_dl, lse, do_dl), cos, sin)
    dqkv = jnp.concatenate([dq_sb, dk_sb, dv_sb, dq_dl, dk_dl, dv_dl], axis=1)
    dx1, dwin, st_mix = _proj_bwd(x1, mix_norm, dqkv, win, dx2)
    grad_x, dwg1, dwu1, dwd1, st_ffn1 = _ffn_bwd(x, ffn1_norm, dx1, saved1, (wg1, wu1), wd1, 0)

    names = ["ffn1_w_gate", "ffn1_w_up", "ffn1_w_down", "w_in", "w_out", "ffn2_w_gate", "ffn2_w_up", "ffn2_w_down"]
    grads = [dwg1, dwu1, dwd1, dwin, dwout.reshape(N_CHIP, OUTB, D_MODEL), dwg2, dwu2, dwd2]
    weights = [ffn1_w_gate[0], ffn1_w_up[0], ffn1_w_down[0], w_in[0], w_out[0], ffn2_w_gate[0], ffn2_w_up[0], ffn2_w_down[0]]
    moms = [m_ffn1_w_gate[0], m_ffn1_w_up[0], m_ffn1_w_down[0], m_w_in[0], m_w_out[0], m_ffn2_w_gate[0], m_ffn2_w_up[0], m_ffn2_w_down[0]]
    vels = [v_ffn1_w_gate[0], v_ffn1_w_up[0], v_ffn1_w_down[0], v_w_in[0], v_w_out[0], v_ffn2_w_gate[0], v_ffn2_w_up[0], v_ffn2_w_down[0]]
    mats = {n: tuple(t[None] for t in r) for n, r in zip(names, _reduce_and_update(grads, weights, moms, vels))}

    vec_names = ["ffn1_norm", "mix_norm", "sb_out_norm", "dil_out_norm", "ffn2_norm", "final_norm"]
    part = jnp.concatenate([st_ffn1[0:1], st_mix[0:1], _pad_row(st_out[0]), _pad_row(st_out[1]), st_ffn2[0:1],
                            st_final[0:1], st_final[1:2], jnp.zeros((1, D_MODEL), F32)], axis=0)
    pack = lambda arrs: jnp.concatenate([_pad_row(a) for a in arrs] + [jnp.zeros((2, D_MODEL), F32)], axis=0)
    g_vec, d_vec, m_vec, v_vec = _reduce_vectors(
        part,
        pack([ffn1_norm, mix_norm, sb_out_norm, dil_out_norm, ffn2_norm, final_norm]),
        pack([m_ffn1_norm, m_mix_norm, m_sb_out_norm, m_dil_out_norm, m_ffn2_norm, m_final_norm]),
        pack([v_ffn1_norm, v_mix_norm, v_sb_out_norm, v_dil_out_norm, v_ffn2_norm, v_final_norm]))
    like = {"ffn1_norm": ffn1_norm, "mix_norm": mix_norm, "sb_out_norm": sb_out_norm, "dil_out_norm": dil_out_norm,
            "ffn2_norm": ffn2_norm, "final_norm": final_norm}
    vecs = {n: tuple(t[i, :like[n].size].reshape(like[n].shape) for t in (g_vec, d_vec, m_vec, v_vec))
            for i, n in enumerate(vec_names)}
    loss = 0.5 * jnp.sum(g_vec[6]) / D_MODEL

    order = ["ffn1_norm", "ffn1_w_gate", "ffn1_w_up", "ffn1_w_down", "mix_norm", "w_in", "sb_out_norm", "dil_out_norm",
             "w_out", "ffn2_norm", "ffn2_w_gate", "ffn2_w_up", "ffn2_w_down", "final_norm"]
    both = {**mats, **vecs}
    return (loss, grad_x[None], *[both[n][0] for n in order], *[both[n][1] for n in order],
            *[both[n][2] for n in order], *[both[n][3] for n in order])
```

```python
import functools

import jax
import jax.numpy as jnp
from jax import lax
from jax.experimental import pallas as pl
from jax.experimental.pallas import tpu as pltpu

D_MODEL = 1024
D_FF = 2816
HEAD_DIM = 64
D_SB = 512
D_DIL = 512
D_IN = 3072
N_CHIP = 4
FFB = D_FF // N_CHIP
INB = D_IN // N_CHIP
OUTB = D_MODEL // N_CHIP
BLK = 128
LANES = 128
DILATIONS = (1, 4, 16)
ROPE_THETA = 10000.0
RMS_EPS = 1e-6
SCALE = HEAD_DIM ** -0.5
NEG = -1e30
DEAD = -104.0
ADAM_LR = 0.001
ADAM_B1 = 0.9
ADAM_B2 = 0.999
ADAM_EPS = 1e-08
ADAM_WD = 0.01
ADAM_STEP = 10
MESH = pl.DeviceIdType.MESH
F32 = jnp.float32
BF16 = jnp.bfloat16
TM = 512


def _params(vmem_mb):
    return pltpu.CompilerParams(vmem_limit_bytes=vmem_mb << 20)


def _dot(a, b):
    return jnp.dot(a, b, preferred_element_type=F32)


def _dot_nt(a, b):
    return lax.dot_general(a, b, (((1,), (1,)), ((), ())), preferred_element_type=F32)


def _dot_tn(a, b):
    return lax.dot_general(a, b, (((0,), (0,)), ((), ())), preferred_element_type=F32)


def _rms_fwd(x, g):
    r = lax.rsqrt(jnp.mean(x * x, axis=-1, keepdims=True) + RMS_EPS)
    xh = x * r
    return xh * g, xh, r


def _rms_bwd(dy, xh, r, g):
    dyg = dy * g
    dx = r * (dyg - xh * jnp.mean(dyg * xh, axis=-1, keepdims=True))
    return dx, jnp.sum(dy * xh, axis=0, keepdims=True)


def _split_bf16(a):
    hi = a.astype(BF16)
    return hi, (a - hi.astype(F32)).astype(BF16)


def _dot_split(a, b2):
    hi, lo = _split_bf16(a)
    return _dot(jnp.concatenate([hi, lo], axis=1), b2)


def _ffn_weight_specs():
    return [pl.BlockSpec((None, D_MODEL, FFB), lambda i, j: (j, 0, 0)),
            pl.BlockSpec((None, D_MODEL, FFB), lambda i, j: (j, 0, 0)),
            pl.BlockSpec((None, FFB, D_MODEL), lambda i, j: (j, 0, 0))]


def _ffn_saved(s):
    hidden = jax.ShapeDtypeStruct((N_CHIP, s, FFB), BF16)
    hid = pl.BlockSpec((None, TM, FFB), lambda i, j: (j, i, 0))
    row = pl.BlockSpec((TM, D_MODEL), lambda i, j: (i, 0))
    return [row, hid, hid, hid], [jax.ShapeDtypeStruct((s, D_MODEL), BF16), hidden, hidden, hidden]


def _ffn_accumulate(h_ref, acc_scr, wg_ref, wu_ref, wd_ref, a_ref, b_ref, act_ref):
    h = h_ref[...]
    a = _dot(h, wg_ref[...])
    b = _dot(h, wu_ref[...])
    act = ((a * jax.nn.sigmoid(a)) * b).astype(BF16)
    a_ref[...] = a.astype(BF16)
    b_ref[...] = b.astype(BF16)
    act_ref[...] = act
    acc_scr[...] += _dot(act, wd_ref[...])


def _ffn1_fwd(x, g1, gmix, gu, wd):
    s = x.shape[0]
    row = pl.BlockSpec((TM, D_MODEL), lambda i, j: (i, 0))
    vec = pl.BlockSpec((1, D_MODEL), lambda i, j: (0, 0))
    saved_specs, saved_shapes = _ffn_saved(s)

    def body(x_ref, g_ref, gm_ref, wg_ref, wu_ref, wd_ref, x1_ref, hm_ref, h_ref, a_ref, b_ref, act_ref, acc_scr):
        j = pl.program_id(1)

        @pl.when(j == 0)
        def _():
            h, _, _ = _rms_fwd(x_ref[...], g_ref[...])
            h_ref[...] = h.astype(BF16)
            acc_scr[...] = jnp.zeros_like(acc_scr)

        _ffn_accumulate(h_ref, acc_scr, wg_ref, wu_ref, wd_ref, a_ref, b_ref, act_ref)

        @pl.when(j == N_CHIP - 1)
        def _():
            x1 = x_ref[...] + 0.5 * acc_scr[...]
            x1_ref[...] = x1
            hm, _, _ = _rms_fwd(x1, gm_ref[...])
            hm_ref[...] = hm.astype(BF16)

    x1, hm, *saved = pl.pallas_call(
        body, name="ffn1_fwd", grid=(s // TM, N_CHIP),
        in_specs=[row, vec, vec] + _ffn_weight_specs(),
        out_specs=[row, row] + saved_specs,
        out_shape=[jax.ShapeDtypeStruct((s, D_MODEL), F32), jax.ShapeDtypeStruct((s, D_MODEL), BF16)] + saved_shapes,
        scratch_shapes=[pltpu.VMEM((TM, D_MODEL), F32)],
        compiler_params=_params(56),
    )(x, g1, gmix, gu[0], gu[1], wd)
    return x1, hm, saved


def _ffn2_fwd_loss(x2, g2, gf, target, gu, wd):
    s = x2.shape[0]
    row = pl.BlockSpec((TM, D_MODEL), lambda i, j: (i, 0))
    vec = pl.BlockSpec((1, D_MODEL), lambda i, j: (0, 0))
    stat = pl.BlockSpec((8, D_MODEL), lambda i, j: (0, 0))
    saved_specs, saved_shapes = _ffn_saved(s)

    def body(x_ref, g_ref, gf_ref, t_ref, wg_ref, wu_ref, wd_ref, dx_ref, st_ref, h_ref, a_ref, b_ref, act_ref, acc_scr):
        i, j = pl.program_id(0), pl.program_id(1)

        @pl.when((i == 0) & (j == 0))
        def _():
            st_ref[...] = jnp.zeros_like(st_ref)

        @pl.when(j == 0)
        def _():
            h, _, _ = _rms_fwd(x_ref[...], g_ref[...])
            h_ref[...] = h.astype(BF16)
            acc_scr[...] = jnp.zeros_like(acc_scr)

        _ffn_accumulate(h_ref, acc_scr, wg_ref, wu_ref, wd_ref, a_ref, b_ref, act_ref)

        @pl.when(j == N_CHIP - 1)
        def _():
            x3 = x_ref[...] + 0.5 * acc_scr[...]
            y, xh, r = _rms_fwd(x3, gf_ref[...])
            err = y - t_ref[...]
            dx, dg = _rms_bwd(err * (1.0 / D_MODEL), xh, r, gf_ref[...])
            dx_ref[...] = dx
            st_ref[0:1, :] += dg
            st_ref[1:2, :] += jnp.sum(err * err, axis=0, keepdims=True)

    dx3, st, *saved = pl.pallas_call(
        body, name="ffn2_fwd_loss", grid=(s // TM, N_CHIP),
        in_specs=[row, vec, vec, row] + _ffn_weight_specs(),
        out_specs=[row, stat] + saved_specs,
        out_shape=[jax.ShapeDtypeStruct((s, D_MODEL), F32), jax.ShapeDtypeStruct((8, D_MODEL), F32)] + saved_shapes,
        scratch_shapes=[pltpu.VMEM((TM, D_MODEL), F32)],
        compiler_params=_params(56),
    )(x2, g2, gf, target, gu[0], gu[1], wd)
    return dx3, st, saved


def _ffn_bwd(xin, g, dy, saved, gu, wd, f):
    s = xin.shape[0]
    hb, gate, up, act = saved
    row = pl.BlockSpec((TM, D_MODEL), lambda i, j: (i, 0))
    vec = pl.BlockSpec((1, D_MODEL), lambda i, j: (0, 0))
    stat = pl.BlockSpec((8, D_MODEL), lambda i, j: (0, 0))
    hid = pl.BlockSpec((None, TM, FFB), lambda i, j: (j, i, 0))

    def body(x_ref, g_ref, dy_ref, a_ref, b_ref, wg_ref, wu_ref, wd_ref, out_ref, dyh_ref, da_ref, db_ref, st_ref, dh_scr):
        i, j = pl.program_id(0), pl.program_id(1)

        @pl.when((i == 0) & (j == 0))
        def _():
            st_ref[...] = jnp.zeros_like(st_ref)

        @pl.when(j == 0)
        def _():
            dyh_ref[...] = (0.5 * dy_ref[...]).astype(BF16)
            dh_scr[...] = jnp.zeros_like(dh_scr)

        a = a_ref[...].astype(F32)
        b = b_ref[...].astype(F32)
        sg = jax.nn.sigmoid(a)
        dact = _dot_nt(dyh_ref[...], wd_ref[...])
        dab = (dact * b * (sg * (1.0 + a * (1.0 - sg)))).astype(BF16)
        dbb = (dact * (a * sg)).astype(BF16)
        da_ref[...] = dab
        db_ref[...] = dbb
        dh_scr[...] += _dot_nt(dab, wg_ref[...]) + _dot_nt(dbb, wu_ref[...])

        @pl.when(j == N_CHIP - 1)
        def _():
            _, xh, r = _rms_fwd(x_ref[...], g_ref[...])
            dx, dg = _rms_bwd(dh_scr[...], xh, r, g_ref[...])
            out_ref[...] = dy_ref[...] + dx
            st_ref[0:1, :] += dg

    hidden = jax.ShapeDtypeStruct((N_CHIP, s, FFB), BF16)
    dx, dyh, da, db, st = pl.pallas_call(
        body, name=f"ffn{f + 1}_bwd_dx", grid=(s // TM, N_CHIP),
        in_specs=[row, vec, row, hid, hid] + _ffn_weight_specs(),
        out_specs=[row, row, hid, hid, stat],
        out_shape=[jax.ShapeDtypeStruct((s, D_MODEL), F32), jax.ShapeDtypeStruct((s, D_MODEL), BF16),
                   hidden, hidden, jax.ShapeDtypeStruct((8, D_MODEL), F32)],
        scratch_shapes=[pltpu.VMEM((TM, D_MODEL), F32)],
        compiler_params=_params(56),
    )(xin, g, dy, gate, up, gu[0], gu[1], wd)

    tok = pl.BlockSpec((TM, D_MODEL), lambda j, i: (i, 0))
    hid2 = pl.BlockSpec((None, TM, FFB), lambda j, i: (j, i, 0))
    gspecs = [pl.BlockSpec((None, D_MODEL, FFB), lambda j, i: (j, 0, 0)),
              pl.BlockSpec((None, D_MODEL, FFB), lambda j, i: (j, 0, 0)),
              pl.BlockSpec((None, FFB, D_MODEL), lambda j, i: (j, 0, 0))]

    def wbody(h_ref, dyh_ref, da_ref, db_ref, act_ref, dwg_ref, dwu_ref, dwd_ref):
        @pl.when(pl.program_id(1) == 0)
        def _():
            dwg_ref[...] = jnp.zeros_like(dwg_ref)
            dwu_ref[...] = jnp.zeros_like(dwu_ref)
            dwd_ref[...] = jnp.zeros_like(dwd_ref)

        hb = h_ref[...]
        dwg_ref[...] += _dot_tn(hb, da_ref[...])
        dwu_ref[...] += _dot_tn(hb, db_ref[...])
        dwd_ref[...] += _dot_tn(act_ref[...], dyh_ref[...])

    dwg, dwu, dwd = pl.pallas_call(
        wbody, name=f"ffn{f + 1}_bwd_dw", grid=(N_CHIP, s // TM),
        in_specs=[tok, tok, hid2, hid2, hid2], out_specs=gspecs,
        out_shape=[jax.ShapeDtypeStruct((N_CHIP, D_MODEL, FFB), F32),
                   jax.ShapeDtypeStruct((N_CHIP, D_MODEL, FFB), F32),
                   jax.ShapeDtypeStruct((N_CHIP, FFB, D_MODEL), F32)],
        compiler_params=_params(48),
    )(hb, dyh, da, db, act)
    return dx, dwg, dwu, dwd, st


def _rope_tables(s):
    half = HEAD_DIM // 2
    inv_freq = ROPE_THETA ** (-jnp.arange(half, dtype=F32) / half)
    ang = jnp.arange(s).astype(F32)[:, None] * inv_freq[None, :]
    cos, sin = jnp.cos(ang), jnp.sin(ang)
    cos2 = jnp.concatenate([cos, cos], axis=-1)
    sin2 = jnp.concatenate([-sin, sin], axis=-1)
    return jnp.tile(cos2, (1, LANES // HEAD_DIM)), jnp.tile(sin2, (1, LANES // HEAD_DIM))


def _rotate(t, cos, sin_signed):
    lane = lax.broadcasted_iota(jnp.int32, t.shape, 1)
    first = (lane % HEAD_DIM) < (HEAD_DIM // 2)
    partner = jnp.where(first, pltpu.roll(t, LANES - HEAD_DIM // 2, 1), pltpu.roll(t, HEAD_DIM // 2, 1))
    return t * cos + partner * sin_signed


def _proj_fwd(hm, win, cos, sin):
    s = hm.shape[0]
    n_sub = INB // LANES
    first_rot, last_rot = (3 * D_SB) // LANES, (3 * D_SB + 2 * D_DIL) // LANES

    def body(h_ref, w_ref, c_ref, s_ref, o_ref):
        j = pl.program_id(1)
        r = _dot(h_ref[...], w_ref[...])
        for c in range(n_sub):
            t = r[:, c * LANES:(c + 1) * LANES]
            col = j * n_sub + c
            rot = (col >= first_rot) & (col < last_rot)
            lanes = slice(c * LANES, (c + 1) * LANES)

            @pl.when(rot)
            def _():
                o_ref[:, lanes] = _rotate(t, c_ref[...], s_ref[...]).astype(BF16)

            @pl.when(jnp.logical_not(rot))
            def _():
                o_ref[:, lanes] = t.astype(BF16)

    return pl.pallas_call(
        body, name="proj_fwd", grid=(s // TM, N_CHIP),
        in_specs=[pl.BlockSpec((TM, D_MODEL), lambda i, j: (i, 0)),
                  pl.BlockSpec((None, D_MODEL, INB), lambda i, j: (j, 0, 0)),
                  pl.BlockSpec((TM, LANES), lambda i, j: (i, 0)),
                  pl.BlockSpec((TM, LANES), lambda i, j: (i, 0))],
        out_specs=pl.BlockSpec((TM, INB), lambda i, j: (i, j)),
        out_shape=jax.ShapeDtypeStruct((s, D_IN), BF16),
        compiler_params=_params(32),
    )(hm, win, cos, sin)


def _proj_bwd(x1, gmix, dqkv, win, dx2):
    s = x1.shape[0]
    row = pl.BlockSpec((TM, D_MODEL), lambda i, j: (i, 0))
    vec = pl.BlockSpec((1, D_MODEL), lambda i, j: (0, 0))

    def body(x_ref, g_ref, dq_ref, w_ref, dx2_ref, out_ref, dw_ref, st_ref, h_scr, dh_scr):
        i, j = pl.program_id(0), pl.program_id(1)

        @pl.when((i == 0) & (j == 0))
        def _():
            st_ref[...] = jnp.zeros_like(st_ref)
            dw_ref[...] = jnp.zeros_like(dw_ref)

        @pl.when(j == 0)
        def _():
            h, _, _ = _rms_fwd(x_ref[...], g_ref[...])
            h_scr[...] = h.astype(BF16)
            dh_scr[...] = jnp.zeros_like(dh_scr)

        dq = dq_ref[...]
        dw_ref[j] += _dot_tn(h_scr[...], dq)
        dh_scr[...] += _dot_nt(dq, w_ref[...])

        @pl.when(j == N_CHIP - 1)
        def _():
            _, xh, r = _rms_fwd(x_ref[...], g_ref[...])
            dx, dg = _rms_bwd(dh_scr[...], xh, r, g_ref[...])
            out_ref[...] = dx2_ref[...] + dx
            st_ref[0:1, :] += dg

    return pl.pallas_call(
        body, name="proj_bwd", grid=(s // TM, N_CHIP),
        in_specs=[row, vec, pl.BlockSpec((TM, INB), lambda i, j: (i, j)),
                  pl.BlockSpec((None, D_MODEL, INB), lambda i, j: (j, 0, 0)), row],
        out_specs=[row, pl.BlockSpec((N_CHIP, D_MODEL, INB), lambda i, j: (0, 0, 0)),
                   pl.BlockSpec((8, D_MODEL), lambda i, j: (0, 0))],
        out_shape=[jax.ShapeDtypeStruct((s, D_MODEL), F32),
                   jax.ShapeDtypeStruct((N_CHIP, D_MODEL, INB), F32),
                   jax.ShapeDtypeStruct((8, D_MODEL), F32)],
        scratch_shapes=[pltpu.VMEM((TM, D_MODEL), BF16), pltpu.VMEM((TM, D_MODEL), F32)],
        compiler_params=_params(56),
    )(x1, gmix, dqkv, win, dx2)


def _outproj_fwd(o_sb, o_dl, g_sb, g_dl, x1, wout):
    s = x1.shape[0]
    half = pl.BlockSpec((TM, D_SB), lambda i: (i, 0))
    row = pl.BlockSpec((TM, D_MODEL), lambda i: (i, 0))
    vec = pl.BlockSpec((1, D_SB), lambda i: (0, 0))

    def body(a_ref, b_ref, ga_ref, gb_ref, x_ref, w_ref, o_ref):
        ma, _, _ = _rms_fwd(a_ref[...], ga_ref[...])
        mb, _, _ = _rms_fwd(b_ref[...], gb_ref[...])
        o_ref[...] = (x_ref[...] + _dot(ma.astype(BF16), w_ref[0:D_SB, :])
                      + _dot(mb.astype(BF16), w_ref[D_SB:D_MODEL, :]))

    return pl.pallas_call(
        body, name="outproj_fwd", grid=(s // TM,),
        in_specs=[half, half, vec, vec, row, pl.BlockSpec((D_MODEL, D_MODEL), lambda i: (0, 0))],
        out_specs=row, out_shape=jax.ShapeDtypeStruct((s, D_MODEL), F32),
        compiler_params=_params(32),
    )(o_sb, o_dl, g_sb, g_dl, x1, wout)


def _outproj_bwd(dx2, o_sb, o_dl, g_sb, g_dl, wout):
    s = dx2.shape[0]
    half = pl.BlockSpec((TM, D_SB), lambda i: (i, 0))
    row = pl.BlockSpec((TM, D_MODEL), lambda i: (i, 0))
    vec = pl.BlockSpec((1, D_SB), lambda i: (0, 0))
    full = pl.BlockSpec((D_MODEL, D_MODEL), lambda i: (0, 0))

    def body(dy_ref, a_ref, b_ref, ga_ref, gb_ref, w_ref, da_ref, db_ref, dl_ref, dw_ref, st_ref):
        @pl.when(pl.program_id(0) == 0)
        def _():
            dw_ref[...] = jnp.zeros_like(dw_ref)
            st_ref[...] = jnp.zeros_like(st_ref)

        dy = dy_ref[...].astype(BF16)
        dm = _dot_nt(dy, w_ref[...])
        ma, xa, ra = _rms_fwd(a_ref[...], ga_ref[...])
        mb, xb, rb = _rms_fwd(b_ref[...], gb_ref[...])
        dw_ref[0:D_SB, :] += _dot_tn(ma.astype(BF16), dy)
        dw_ref[D_SB:D_MODEL, :] += _dot_tn(mb.astype(BF16), dy)
        da, dga = _rms_bwd(dm[:, 0:D_SB], xa, ra, ga_ref[...])
        db, dgb = _rms_bwd(dm[:, D_SB:D_MODEL], xb, rb, gb_ref[...])
        da_ref[...] = da
        db_ref[...] = db
        r = lax.broadcasted_iota(jnp.int32, (LANES, LANES), 0) >= HEAD_DIM
        c = lax.broadcasted_iota(jnp.int32, (LANES, LANES), 1) >= HEAD_DIM
        same_head = jnp.where(r == c, 1.0, 0.0).astype(BF16)
        same_head = jnp.concatenate([same_head, same_head], axis=0)
        prod = db * b_ref[...]
        for k in range(D_DIL // LANES):
            lanes = slice(k * LANES, (k + 1) * LANES)
            dl_ref[:, lanes] = _dot_split(prod[:, lanes], same_head)
        st_ref[0:1, :] += dga
        st_ref[1:2, :] += dgb

    return pl.pallas_call(
        body, name="outproj_bwd", grid=(s // TM,),
        in_specs=[row, half, half, vec, vec, full],
        out_specs=[half, half, half, full, pl.BlockSpec((8, D_SB), lambda i: (0, 0))],
        out_shape=[jax.ShapeDtypeStruct((s, D_SB), F32), jax.ShapeDtypeStruct((s, D_SB), F32),
                   jax.ShapeDtypeStruct((s, D_DIL), F32),
                   jax.ShapeDtypeStruct((D_MODEL, D_MODEL), F32), jax.ShapeDtypeStruct((8, D_SB), F32)],
        compiler_params=_params(48),
    )(dx2, o_sb, o_dl, g_sb, g_dl, wout)


def _head_masks():
    lane = lax.broadcasted_iota(jnp.int32, (BLK, LANES), 1)
    return [lane < HEAD_DIM, lane >= HEAD_DIM]


def _keep(mask, a):
    return a * jnp.where(mask, 1.0, 0.0).astype(a.dtype)


def _suffix_matrices():
    r = lax.broadcasted_iota(jnp.int32, (2 * BLK, BLK), 0) & (BLK - 1)
    c = lax.broadcasted_iota(jnp.int32, (2 * BLK, BLK), 1)
    ones = jnp.ones((2 * BLK, BLK), BF16)
    excl = jnp.concatenate([(r > c).astype(BF16), ones], axis=1)
    incl = jnp.concatenate([(r >= c).astype(BF16), ones], axis=1)
    return excl, incl


def _blk(i):
    return pl.ds(pl.multiple_of(i * BLK, BLK), BLK)


def _alive(carry_m):
    return (jnp.max(carry_m) > DEAD).astype(jnp.int32)


def _more_keys(last, carry):
    return (carry[0] * SB_KB <= last) & (carry[1] > 0)


def _stack_heads(a):
    masks = _head_masks()
    return jnp.concatenate([_keep(masks[0], a), _keep(masks[1], a)], axis=0)


def _unstack_heads(a2):
    return jnp.where(_head_masks()[0], a2[:BLK], a2[BLK:])


def _head_rowsum(a):
    masks = _head_masks()
    return jnp.concatenate([jnp.sum(jnp.where(m, a, 0.0), axis=1, keepdims=True) for m in masks], axis=0)


SB_QB = 2
SB_ROWS = SB_QB * 2 * BLK
SB_KB = 4
PAST_START = 1 << 30


def _sb_rows(ref, i0, cast=None):
    tiles = [ref[_blk(i0 + t), :] for t in range(SB_QB)]
    return jnp.concatenate([_stack_heads(t if cast is None else t.astype(cast)) for t in tiles], axis=0)


def _sb_scores(q2, k, i, j, carry_m, u_excl):
    r = lax.broadcasted_iota(jnp.int32, (SB_ROWS, BLK), 0)
    row = (r & (BLK - 1)) + ((r >> 8) << 7)
    col = lax.broadcasted_iota(jnp.int32, (SB_ROWS, BLK), 1)
    valid = (jnp.where(j >= 0, j * BLK, PAST_START) + col) < (i * BLK + row)
    z = _dot_nt(q2, k) * SCALE
    sp = jnp.maximum(z, 0.0) + jnp.log(1.0 + jnp.exp(-jnp.abs(z)))
    log_stay = jnp.where(valid, -sp, 0.0)
    log_beta = z - sp
    sums = _dot_split(log_stay, u_excl)
    later = carry_m + sums[:, :BLK]
    w = jnp.where(valid, jnp.exp(log_beta + later), 0.0)
    return valid, log_beta, w, carry_m + sums[:, BLK:]


def _sb_fwd(qkv):
    s = qkv.shape[0]
    nq = s // BLK
    pairs = D_SB // LANES
    col = lambda off: pl.BlockSpec((s, LANES), lambda p: (0, off + p))

    def body(q_ref, k_ref, v_ref, o_ref):
        u_excl, _ = _suffix_matrices()
        zero = jnp.zeros((SB_ROWS, LANES), F32)

        def q_block(ib, _):
            i = ib * SB_QB
            last = i + SB_QB - 1
            q2 = _sb_rows(q_ref, i)

            def k_block(carry):
                jj, _, carry_m, acc = carry
                for t in range(SB_KB):
                    j = last - jj * SB_KB - t
                    at = _blk(jnp.maximum(j, 0))
                    _, _, w, carry_m = _sb_scores(q2, k_ref[at, :], i, j, carry_m, u_excl)
                    acc = acc + _dot(w.astype(BF16), v_ref[at, :])
                return jj + 1, _alive(carry_m), carry_m, acc

            _, _, _, acc = lax.while_loop(functools.partial(_more_keys, last), k_block,
                                          (jnp.int32(0), jnp.int32(1), zero, zero))
            for t in range(SB_QB):
                o_ref[_blk(i + t), :] = _unstack_heads(acc[2 * BLK * t:2 * BLK * (t + 1)])
            return 0

        lax.fori_loop(0, nq // SB_QB, q_block, 0)

    return pl.pallas_call(
        body, name="sb_fwd", grid=(pairs,),
        in_specs=[col(0), col(pairs), col(2 * pairs)],
        out_specs=pl.BlockSpec((s, LANES), lambda p: (0, p)),
        out_shape=jax.ShapeDtypeStruct((s, D_SB), F32),
        compiler_params=_params(48),
    )(qkv, qkv, qkv)


def _sb_bwd(qkv, o_sb, do_sb):
    s = qkv.shape[0]
    nq = s // BLK
    pairs = D_SB // LANES
    col = lambda off: pl.BlockSpec((s, LANES), lambda p: (0, off + p))
    own = pl.BlockSpec((s, LANES), lambda p: (0, p))

    def body(q_ref, k_ref, v_ref, o_ref, do_ref, dq_ref, dk_ref, dv_ref, dk_acc, dv_acc):
        u_excl, u_incl = _suffix_matrices()
        zero = jnp.zeros((SB_ROWS, LANES), F32)
        dk_acc[...] = jnp.zeros_like(dk_acc)
        dv_acc[...] = jnp.zeros_like(dv_acc)

        def q_block(ib, _):
            i = ib * SB_QB
            last = i + SB_QB - 1
            q2 = _sb_rows(q_ref, i)
            do2 = _sb_rows(do_ref, i, BF16)
            totals = [_head_rowsum(do_ref[_blk(i + t), :].astype(BF16).astype(F32) * o_ref[_blk(i + t), :])
                      for t in range(SB_QB)]
            total = jnp.broadcast_to(jnp.concatenate(totals, axis=0), (SB_ROWS, BLK))

            def k_block(carry):
                jj, _, carry_m, carry_g, dq = carry
                for t in range(SB_KB):
                    j = last - jj * SB_KB - t
                    at = _blk(jnp.maximum(j, 0))
                    k = k_ref[at, :]
                    valid, log_beta, w, carry_m = _sb_scores(q2, k, i, j, carry_m, u_excl)
                    wb = w.astype(BF16)
                    g = wb.astype(F32) * _dot_nt(do2, v_ref[at, :])
                    sums = _dot_split(g, u_incl)
                    before = total - (carry_g + sums[:, :BLK])
                    dz = jnp.where(valid, g - jnp.exp(log_beta) * (g + before), 0.0)
                    dzb = (dz * SCALE).astype(BF16)
                    dk_acc[at, :] += _dot_tn(dzb, q2)
                    dv_acc[at, :] += _dot_tn(wb, do2)
                    carry_g = carry_g + sums[:, BLK:]
                    dq = dq + _dot(dzb, k)
                return jj + 1, _alive(carry_m), carry_m, carry_g, dq

            _, _, _, _, dq = lax.while_loop(functools.partial(_more_keys, last), k_block,
                                            (jnp.int32(0), jnp.int32(1), zero, zero, zero))
            for t in range(SB_QB):
                dq_ref[_blk(i + t), :] = _unstack_heads(dq[2 * BLK * t:2 * BLK * (t + 1)]).astype(BF16)
            return 0

        lax.fori_loop(0, nq // SB_QB, q_block, 0)
        dk_ref[...] = dk_acc[...].astype(BF16)
        dv_ref[...] = dv_acc[...].astype(BF16)

    return pl.pallas_call(
        body, name="sb_bwd", grid=(pairs,),
        in_specs=[col(0), col(pairs), col(2 * pairs), own, own],
        out_specs=[own, own, own],
        out_shape=[jax.ShapeDtypeStruct((s, D_SB), BF16)] * 3,
        scratch_shapes=[pltpu.VMEM((s, LANES), F32), pltpu.VMEM((s, LANES), F32)],
        compiler_params=_params(56),
    )(qkv, qkv, qkv, o_sb, do_sb)


DIL_UNROLL = 4


def _band_masks(b):
    row = lax.broadcasted_iota(jnp.int32, (2 * BLK, BLK), 0) & (BLK - 1)
    col = lax.broadcasted_iota(jnp.int32, (2 * BLK, BLK), 1)
    return col <= row, (col - row) >= jnp.where(b > 0, 0, BLK)


def _dil_tiles(qf, kf, vf, d, t, nb):
    c, b = t // nb, t % nb
    start = c + d * BLK * b
    rows = pl.ds(start, BLK, stride=d)
    prev = pl.ds(jnp.where(b > 0, start - d * BLK, start), BLK, stride=d)
    bf = lambda ref, sl: ref[sl, :].astype(BF16)
    return b, rows, prev, _stack_heads(bf(qf, rows)), bf(kf, rows), bf(kf, prev), bf(vf, rows), bf(vf, prev)


def _lanes_of_heads(col2):
    return _unstack_heads(jnp.broadcast_to(col2, (2 * BLK, LANES)))


def _dilated_fwd(qkv):
    s = qkv.shape[0]
    pairs = D_DIL // LANES
    base = (3 * D_SB) // LANES
    col = lambda off: pl.BlockSpec((s, LANES), lambda p: (0, off + p))
    own = pl.BlockSpec((s, LANES), lambda p: (0, p))

    def body(q_ref, k_ref, v_ref, acc_ref, m_ref, qf, kf, vf, l_scr):
        qf[...] = q_ref[...].astype(F32)
        kf[...] = k_ref[...].astype(F32)
        vf[...] = v_ref[...].astype(F32)
        for d in DILATIONS:
            nb = s // (d * BLK)

            def block(t, _):
                b, rows, prev, q2, kc, kp, vc, vp = _dil_tiles(qf, kf, vf, d, t, nb)
                in_cur, in_prev = _band_masks(b)
                zc = jnp.where(in_cur, _dot_nt(q2, kc) * SCALE, NEG)
                zp = jnp.where(in_prev, _dot_nt(q2, kp) * SCALE, NEG)
                m = jnp.maximum(jnp.max(zc, axis=1, keepdims=True), jnp.max(zp, axis=1, keepdims=True))
                pc, pp = jnp.exp(zc - m), jnp.exp(zp - m)
                den = jnp.sum(pc, axis=1, keepdims=True) + jnp.sum(pp, axis=1, keepdims=True)
                acc = _unstack_heads(_dot(pc.astype(BF16), vc) + _dot(pp.astype(BF16), vp))
                m_t, l_t = _lanes_of_heads(m), _lanes_of_heads(den)
                if d == DILATIONS[0]:
                    m_ref[rows, :] = m_t
                    l_scr[rows, :] = l_t
                    acc_ref[rows, :] = acc
                else:
                    m_old = m_ref[rows, :]
                    m_new = jnp.maximum(m_old, m_t)
                    keep, add = jnp.exp(m_old - m_new), jnp.exp(m_t - m_new)
                    m_ref[rows, :] = m_new
                    l_scr[rows, :] = l_scr[rows, :] * keep + l_t * add
                    acc_ref[rows, :] = acc_ref[rows, :] * keep + acc * add
                return 0

            lax.fori_loop(0, s // BLK, block, 0, unroll=DIL_UNROLL)

        def finish(i, _):
            l = l_scr[_blk(i), :]
            acc_ref[_blk(i), :] = acc_ref[_blk(i), :] / l
            m_ref[_blk(i), :] = m_ref[_blk(i), :] + jnp.log(l)
            return 0

        lax.fori_loop(0, s // BLK, finish, 0)

    return pl.pallas_call(
        body, name="dilated_fwd", grid=(pairs,),
        in_specs=[col(base), col(base + pairs), col(base + 2 * pairs)],
        out_specs=[own, own],
        out_shape=[jax.ShapeDtypeStruct((s, D_DIL), F32)] * 2,
        scratch_shapes=[pltpu.VMEM((s, LANES), F32)] * 4,
        compiler_params=_params(56),
    )(qkv, qkv, qkv)


def _stack_lanes(t):
    other = pltpu.roll(t, HEAD_DIM, 1)
    first = _head_masks()[0]
    return jnp.concatenate([jnp.where(first, t, other), jnp.where(first, other, t)], axis=0)


def _dilated_bwd(qkv, delta, lse, dout):
    s = qkv.shape[0]
    pairs = D_DIL // LANES
    base = (3 * D_SB) // LANES
    once = pl.Buffered(1)
    col = lambda off: pl.BlockSpec((s, LANES), lambda p: (0, off + p), pipeline_mode=once)
    own = pl.BlockSpec((s, LANES), lambda p: (0, p), pipeline_mode=once)
    res = pl.BlockSpec((s, LANES), lambda p: (0, p))

    def body(q_ref, k_ref, v_ref, dl_ref, l_ref, do_ref, dq_ref, dk_ref, dv_ref, qf, kf, vf):
        qf[...] = q_ref[...].astype(F32)
        kf[...] = k_ref[...].astype(F32)
        vf[...] = v_ref[...].astype(F32)
        dq_ref[...] = jnp.zeros_like(dq_ref)
        dk_ref[...] = jnp.zeros_like(dk_ref)
        dv_ref[...] = jnp.zeros_like(dv_ref)
        for d in DILATIONS:
            nb = s // (d * BLK)

            def block(t, _):
                b, rows, prev, q2, kc, kp, vc, vp = _dil_tiles(qf, kf, vf, d, t, nb)
                in_cur, in_prev = _band_masks(b)
                do2 = _stack_heads(do_ref[rows, :].astype(BF16))
                delta = _stack_lanes(dl_ref[rows, :])
                lse2 = _stack_lanes(l_ref[rows, :])
                wc = jnp.exp(jnp.where(in_cur, _dot_nt(q2, kc) * SCALE, NEG) - lse2)
                wp = jnp.exp(jnp.where(in_prev, _dot_nt(q2, kp) * SCALE, NEG) - lse2)
                dzc = (wc * (_dot_nt(do2, vc) - delta) * SCALE).astype(BF16)
                dzp = (wp * (_dot_nt(do2, vp) - delta) * SCALE).astype(BF16)
                dq_ref[rows, :] += _unstack_heads(_dot(dzc, kc) + _dot(dzp, kp))
                dk_ref[rows, :] += _dot_tn(dzc, q2)
                dk_ref[prev, :] += _dot_tn(dzp, q2)
                dv_ref[rows, :] += _dot_tn(wc.astype(BF16), do2)
                dv_ref[prev, :] += _dot_tn(wp.astype(BF16), do2)
                return 0

            lax.fori_loop(0, s // BLK, block, 0, unroll=DIL_UNROLL)

    return pl.pallas_call(
        body, name="dilated_bwd", grid=(pairs,),
        in_specs=[col(base), col(base + pairs), col(base + 2 * pairs), own, own, own],
        out_specs=[res, res, res],
        out_shape=[jax.ShapeDtypeStruct((s, D_DIL), F32)] * 3,
        scratch_shapes=[pltpu.VMEM((s, LANES), F32)] * 3,
        compiler_params=_params(60),
    )(qkv, qkv, qkv, delta, lse, dout)


def _dilated_finish(grads, cos, sin):
    s = grads[0].shape[0]
    spec = pl.BlockSpec((TM, D_DIL), lambda i: (i, 0))
    tab = pl.BlockSpec((TM, LANES), lambda i: (i, 0))

    def body(dq_ref, dk_ref, dv_ref, c_ref, s_ref, oq_ref, ok_ref, ov_ref):
        for src, dst, rotated in ((dq_ref, oq_ref, True), (dk_ref, ok_ref, True), (dv_ref, ov_ref, False)):
            for c in range(D_DIL // LANES):
                lanes = slice(c * LANES, (c + 1) * LANES)
                piece = src[:, lanes]
                dst[:, lanes] = (_rotate(piece, c_ref[...], -s_ref[...]) if rotated else piece).astype(BF16)

    return pl.pallas_call(
        body, name="dilated_finish", grid=(s // TM,),
        in_specs=[spec] * 3 + [tab, tab], out_specs=[spec] * 3,
        out_shape=[jax.ShapeDtypeStruct((s, D_DIL), BF16)] * 3,
        compiler_params=_params(32),
    )(*grads, cos, sin)


def _place():
    x, y, c = lax.axis_index("x"), lax.axis_index("y"), lax.axis_index("c")
    return x, y, c, 2 * x + y


def _chip(k, c):
    return (k >> 1, k & 1, c)


def _half(ref, h):
    n = ref.shape[0] // 2
    return ref.at[pl.ds(h * n, n)]


def _all_gather(shards):
    na = len(shards)
    any_spec = pl.BlockSpec(memory_space=pl.ANY)

    def body(*refs):
        ins, outs = refs[:na], refs[na:2 * na]
        send_sem, recv_sem, local_sem = refs[2 * na:]
        x, y, c, k = _place()
        sibling = (x, y, 1 - c)
        started = []
        for a in range(na):
            cp = pltpu.make_async_copy(ins[a], outs[a].at[k], local_sem.at[a])
            cp.start()
            started.append(cp)

        def copy(a, slot, src, dst, to):
            return pltpu.make_async_remote_copy(src_ref=src, dst_ref=dst, send_sem=send_sem.at[a * 6 + slot],
                                                recv_sem=recv_sem.at[a * 6 + slot], device_id=to, device_id_type=MESH)

        sends = []
        for a in range(na):
            for j in range(1, N_CHIP):
                cp = copy(a, j - 1, _half(ins[a], c), _half(outs[a].at[k], c), _chip(k ^ j, c))
                cp.start()
                sends.append(cp)
        for j in range(1, N_CHIP):
            for a in range(na):
                landed = _half(outs[a].at[k ^ j], c)
                copy(a, j - 1, landed, landed, sibling).wait_recv()
                cp = copy(a, 2 + j, landed, landed, sibling)
                cp.start()
                sends.append(cp)
        for j in range(1, N_CHIP):
            for a in range(na):
                passed = _half(outs[a].at[k ^ j], 1 - c)
                copy(a, 2 + j, passed, passed, sibling).wait_recv()
        for cp in sends:
            cp.wait_send()
        for cp in started:
            cp.wait()

    return pl.pallas_call(
        body, name="weights_all_gather",
        in_specs=[any_spec] * na, out_specs=[any_spec] * na,
        out_shape=[jax.ShapeDtypeStruct((N_CHIP,) + a.shape, a.dtype) for a in shards],
        scratch_shapes=[pltpu.SemaphoreType.DMA((6 * na,)), pltpu.SemaphoreType.DMA((6 * na,)),
                        pltpu.SemaphoreType.DMA((na,))],
    )(*shards)


def _reduce_scatter(g, core, name):
    n, r, c = g.shape
    hr = r // 2
    once = pl.Buffered(1)
    in_specs = [pl.BlockSpec((n, hr, c), lambda i, core_ref: (0, core_ref[0], 0), pipeline_mode=once),
                pl.BlockSpec((n, hr, c), lambda i, core_ref: (0, 1 - core_ref[0], 0), pipeline_mode=once)]

    def body(core_ref, mine_ref, other_ref, out_ref, from_core, sums, sums_bf, from_chips, done, from_core2, send_sem, recv_sem):
        x, y, cc, k = _place()
        sibling = (x, y, 1 - cc)

        def copy(slot, src, dst, to):
            return pltpu.make_async_remote_copy(src_ref=src, dst_ref=dst, send_sem=send_sem.at[slot],
                                                recv_sem=recv_sem.at[slot], device_id=to, device_id_type=MESH)

        first = copy(0, other_ref, from_core, sibling)
        first.start()
        first.wait()
        total = mine_ref[...] + from_core[...]
        sums[...] = total
        sums_bf[...] = total.astype(BF16)
        sends = [copy(j, sums_bf.at[k ^ j], from_chips.at[j - 1], _chip(k ^ j, cc)) for j in range(1, N_CHIP)]
        for cp in sends:
            cp.start()
        for cp in sends:
            cp.wait()
        red = sums[k]
        for j in range(1, N_CHIP):
            red = red + from_chips[j - 1].astype(F32)
        done[...] = red
        last = copy(N_CHIP, done, from_core2, sibling)
        last.start()
        last.wait()
        row0 = pl.multiple_of(cc * hr, 8)
        row1 = pl.multiple_of((1 - cc) * hr, 8)
        out_ref[pl.ds(row0, hr), :] = red
        out_ref[pl.ds(row1, hr), :] = from_core2[...]

    grid_spec = pltpu.PrefetchScalarGridSpec(
        num_scalar_prefetch=1, grid=(1,), in_specs=in_specs,
        out_specs=pl.BlockSpec((r, c), lambda i, core_ref: (0, 0)),
        scratch_shapes=[pltpu.VMEM((n, hr, c), F32), pltpu.VMEM((n, hr, c), F32), pltpu.VMEM((n, hr, c), BF16),
                        pltpu.VMEM((N_CHIP - 1, hr, c), BF16), pltpu.VMEM((hr, c), F32), pltpu.VMEM((hr, c), F32),
                        pltpu.SemaphoreType.DMA((N_CHIP + 1,)), pltpu.SemaphoreType.DMA((N_CHIP + 1,))])
    return pl.pallas_call(
        body, name=name, grid_spec=grid_spec, out_shape=jax.ShapeDtypeStruct((r, c), F32),
        compiler_params=_params(56),
    )(core, g, g)


def _elementwise(fn, name, ins, n_out, rows):
    total, cols = ins[0].shape
    spec = pl.BlockSpec((rows, cols), lambda i: (i, 0))

    def body(*refs):
        res = fn(*[r[...] for r in refs[:len(ins)]])
        for o, v in zip(refs[len(ins):], res):
            o[...] = v

    return pl.pallas_call(
        body, name=name, grid=(total // rows,),
        in_specs=[spec] * len(ins), out_specs=[spec] * n_out,
        out_shape=[jax.ShapeDtypeStruct((total, cols), F32)] * n_out,
        compiler_params=_params(48),
    )(*ins)


def _adamw(w, g, m, v):
    m = ADAM_B1 * m + (1.0 - ADAM_B1) * g
    v = ADAM_B2 * v + (1.0 - ADAM_B2) * (g * g)
    m_hat = m / (1.0 - ADAM_B1 ** ADAM_STEP)
    v_hat = v / (1.0 - ADAM_B2 ** ADAM_STEP)
    delta = -ADAM_LR * (m_hat / (jnp.sqrt(v_hat) + ADAM_EPS) + ADAM_WD * w)
    return delta, m, v


def _reduce_and_update(grads, weights, moms, vels):
    core = lax.axis_index("c").astype(jnp.int32).reshape(1)
    full = [_reduce_scatter(g, core, f"grads_reduce_scatter_{a}") for a, g in enumerate(grads)]
    out = []
    for a, (g, w, m, v) in enumerate(zip(full, weights, moms, vels)):
        rows = g.shape[0] // 2
        out.append((g,) + tuple(_elementwise(lambda gg, ww, mm, vv: _adamw(ww, gg, mm, vv), f"adamw_{a}", [g, w, m, v], 3, rows)))
    return out


def _reduce_vectors(part, w, m, v):
    n_dev = 8

    def body(p_ref, w_ref, m_ref, v_ref, g_ref, d_ref, nm_ref, nv_ref, buf, send_sem, recv_sem):
        x, y, c, _ = _place()
        me = 4 * x + 2 * y + c
        buf[me] = p_ref[...]
        sends = []
        for off in range(1, n_dev):
            peer = me ^ off
            cp = pltpu.make_async_remote_copy(src_ref=p_ref, dst_ref=buf.at[me], send_sem=send_sem.at[off - 1],
                                              recv_sem=recv_sem.at[off - 1], device_id=(peer >> 2, (peer >> 1) & 1, peer & 1),
                                              device_id_type=MESH)
            cp.start()
            sends.append(cp)
        for off in range(1, n_dev):
            peer = me ^ off
            pltpu.make_async_remote_copy(src_ref=p_ref, dst_ref=buf.at[peer], send_sem=send_sem.at[off - 1],
                                         recv_sem=recv_sem.at[off - 1], device_id=(peer >> 2, (peer >> 1) & 1, peer & 1),
                                         device_id_type=MESH).wait_recv()
        for cp in sends:
            cp.wait_send()
        g = buf[0]
        for d in range(1, n_dev):
            g = g + buf[d]
        g_ref[...] = g
        delta, nm, nv = _adamw(w_ref[...], g, m_ref[...], v_ref[...])
        d_ref[...] = delta
        nm_ref[...] = nm
        nv_ref[...] = nv

    vm = pl.BlockSpec(memory_space=pltpu.VMEM)
    return pl.pallas_call(
        body, name="gains_all_reduce",
        in_specs=[vm] * 4, out_specs=[vm] * 4,
        out_shape=[jax.ShapeDtypeStruct(part.shape, F32)] * 4,
        scratch_shapes=[pltpu.VMEM((n_dev,) + part.shape, F32), pltpu.SemaphoreType.DMA((n_dev - 1,)),
                        pltpu.SemaphoreType.DMA((n_dev - 1,))],
    )(part, w, m, v)


def _pad_row(a):
    a = a.reshape(1, -1)
    return jnp.pad(a, ((0, 0), (0, D_MODEL - a.shape[1])))


def kernel(x, ffn1_norm, ffn1_w_gate, ffn1_w_up, ffn1_w_down, mix_norm, w_in, sb_out_norm, dil_out_norm, w_out, ffn2_norm, ffn2_w_gate, ffn2_w_up, ffn2_w_down, final_norm, loss_target, m_ffn1_norm, m_ffn1_w_gate, m_ffn1_w_up, m_ffn1_w_down, m_mix_norm, m_w_in, m_sb_out_norm, m_dil_out_norm, m_w_out, m_ffn2_norm, m_ffn2_w_gate, m_ffn2_w_up, m_ffn2_w_down, m_final_norm, v_ffn1_norm, v_ffn1_w_gate, v_ffn1_w_up, v_ffn1_w_down, v_mix_norm, v_w_in, v_sb_out_norm, v_dil_out_norm, v_w_out, v_ffn2_norm, v_ffn2_w_gate, v_ffn2_w_up, v_ffn2_w_down, v_final_norm):
    x = x[0]
    target = loss_target[0]
    s = x.shape[0]
    gf = final_norm.reshape(1, D_MODEL)
    cos, sin = _rope_tables(s)

    shards = [ffn1_w_gate, ffn1_w_up, ffn1_w_down, w_in, w_out, ffn2_w_gate, ffn2_w_up, ffn2_w_down]
    wg1, wu1, wd1, win, wout, wg2, wu2, wd2 = _all_gather([w[0].astype(BF16) for w in shards])
    wout = wout.reshape(D_MODEL, D_MODEL)

    x1, hm, saved1 = _ffn1_fwd(x, ffn1_norm, mix_norm, (wg1, wu1), wd1)
    qkv = _proj_fwd(hm, win, cos, sin)
    o_sb = _sb_fwd(qkv)
    o_dl, lse = _dilated_fwd(qkv)
    x2 = _outproj_fwd(o_sb, o_dl, sb_out_norm, dil_out_norm, x1, wout)
    dx3, st_final, saved2 = _ffn2_fwd_loss(x2, ffn2_norm, gf, target, (wg2, wu2), wd2)

    dx2, dwg2, dwu2, dwd2, st_ffn2 = _ffn_bwd(x2, ffn2_norm, dx3, saved2, (wg2, wu2), wd2, 1)
    do_sb, do_dl, delta_dl, dwout, st_out = _outproj_bwd(dx2, o_sb, o_dl, sb_out_norm, dil_out_norm, wout)
    dq_sb, dk_sb, dv_sb = _sb_bwd(qkv, o_sb, do_sb)
    dq_dl, dk_dl, dv_dl = _dilated_finish(_dilated_bwd(qkv, delta_dl, lse, do_dl), cos, sin)
    dqkv = jnp.concatenate([dq_sb, dk_sb, dv_sb, dq_dl, dk_dl, dv_dl], axis=1)
    dx1, dwin, st_mix = _proj_bwd(x1, mix_norm, dqkv, win, dx2)
    grad_x, dwg1, dwu1, dwd1, st_ffn1 = _ffn_bwd(x, ffn1_norm, dx1, saved1, (wg1, wu1), wd1, 0)

    names = ["ffn1_w_gate", "ffn1_w_up", "ffn1_w_down", "w_in", "w_out", "ffn2_w_gate", "ffn2_w_up", "ffn2_w_down"]
    grads = [dwg1, dwu1, dwd1, dwin, dwout.reshape(N_CHIP, OUTB, D_MODEL), dwg2, dwu2, dwd2]
    weights = [ffn1_w_gate[0], ffn1_w_up[0], ffn1_w_down[0], w_in[0], w_out[0], ffn2_w_gate[0], ffn2_w_up[0], ffn2_w_down[0]]
    moms = [m_ffn1_w_gate[0], m_ffn1_w_up[0], m_ffn1_w_down[0], m_w_in[0], m_w_out[0], m_ffn2_w_gate[0], m_ffn2_w_up[0], m_ffn2_w_down[0]]
    vels = [v_ffn1_w_gate[0], v_ffn1_w_up[0], v_ffn1_w_down[0], v_w_in[0], v_w_out[0], v_ffn2_w_gate[0], v_ffn2_w_up[0], v_ffn2_w_down[0]]
    mats = {n: tuple(t[None] for t in r) for n, r in zip(names, _reduce_and_update(grads, weights, moms, vels))}

    vec_names = ["ffn1_norm", "mix_norm", "sb_out_norm", "dil_out_norm", "ffn2_norm", "final_norm"]
    part = jnp.concatenate([st_ffn1[0:1], st_mix[0:1], _pad_row(st_out[0]), _pad_row(st_out[1]), st_ffn2[0:1],
                            st_final[0:1], st_final[1:2], jnp.zeros((1, D_MODEL), F32)], axis=0)
    pack = lambda arrs: jnp.concatenate([_pad_row(a) for a in arrs] + [jnp.zeros((2, D_MODEL), F32)], axis=0)
    g_vec, d_vec, m_vec, v_vec = _reduce_vectors(
        part,
        pack([ffn1_norm, mix_norm, sb_out_norm, dil_out_norm, ffn2_norm, final_norm]),
        pack([m_ffn1_norm, m_mix_norm, m_sb_out_norm, m_dil_out_norm, m_ffn2_norm, m_final_norm]),
        pack([v_ffn1_norm, v_mix_norm, v_sb_out_norm, v_dil_out_norm, v_ffn2_norm, v_final_norm]))
    like = {"ffn1_norm": ffn1_norm, "mix_norm": mix_norm, "sb_out_norm": sb_out_norm, "dil_out_norm": dil_out_norm,
            "ffn2_norm": ffn2_norm, "final_norm": final_norm}
    vecs = {n: tuple(t[i, :like[n].size].reshape(like[n].shape) for t in (g_vec, d_vec, m_vec, v_vec))
            for i, n in enumerate(vec_names)}
    loss = 0.5 * jnp.sum(g_vec[6]) / D_MODEL

    order = ["ffn1_norm", "ffn1_w_gate", "ffn1_w_up", "ffn1_w_down", "mix_norm", "w_in", "sb_out_norm", "dil_out_norm",
             "w_out", "ffn2_norm", "ffn2_w_gate", "ffn2_w_up", "ffn2_w_down", "final_norm"]
    both = {**mats, **vecs}
    return (loss, grad_x[None], *[both[n][0] for n in order], *[both[n][1] for n in order],
            *[both[n][2] for n in order], *[both[n][3] for n in order])
```

```python
import functools

import jax
import jax.numpy as jnp
from jax import lax
from jax.experimental import pallas as pl
from jax.experimental.pallas import tpu as pltpu

D_MODEL = 1024
D_FF = 2816
HEAD_DIM = 64
D_SB = 512
D_DIL = 512
D_IN = 3072
N_CHIP = 4
FFB = D_FF // N_CHIP
INB = D_IN // N_CHIP
OUTB = D_MODEL // N_CHIP
BLK = 128
LANES = 128
DILATIONS = (1, 4, 16)
ROPE_THETA = 10000.0
RMS_EPS = 1e-6
SCALE = HEAD_DIM ** -0.5
NEG = -1e30
DEAD = -104.0
ADAM_LR = 0.001
ADAM_B1 = 0.9
ADAM_B2 = 0.999
ADAM_EPS = 1e-08
ADAM_WD = 0.01
ADAM_STEP = 10
MESH = pl.DeviceIdType.MESH
F32 = jnp.float32
BF16 = jnp.bfloat16
TM = 512


def _params(vmem_mb):
    return pltpu.CompilerParams(vmem_limit_bytes=vmem_mb << 20)


def _dot(a, b):
    return jnp.dot(a, b, preferred_element_type=F32)


def _dot_nt(a, b):
    return lax.dot_general(a, b, (((1,), (1,)), ((), ())), preferred_element_type=F32)


def _dot_tn(a, b):
    return lax.dot_general(a, b, (((0,), (0,)), ((), ())), preferred_element_type=F32)


def _rms_fwd(x, g):
    r = lax.rsqrt(jnp.mean(x * x, axis=-1, keepdims=True) + RMS_EPS)
    xh = x * r
    return xh * g, xh, r


def _rms_bwd(dy, xh, r, g):
    dyg = dy * g
    dx = r * (dyg - xh * jnp.mean(dyg * xh, axis=-1, keepdims=True))
    return dx, jnp.sum(dy * xh, axis=0, keepdims=True)


def _split_bf16(a):
    hi = a.astype(BF16)
    return hi, (a - hi.astype(F32)).astype(BF16)


def _dot_split(a, b2):
    hi, lo = _split_bf16(a)
    return _dot(jnp.concatenate([hi, lo], axis=1), b2)


def _ffn_weight_specs():
    return [pl.BlockSpec((None, D_MODEL, FFB), lambda i, j: (j, 0, 0)),
            pl.BlockSpec((None, D_MODEL, FFB), lambda i, j: (j, 0, 0)),
            pl.BlockSpec((None, FFB, D_MODEL), lambda i, j: (j, 0, 0))]


def _ffn_saved(s):
    hidden = jax.ShapeDtypeStruct((N_CHIP, s, FFB), BF16)
    hid = pl.BlockSpec((None, TM, FFB), lambda i, j: (j, i, 0))
    row = pl.BlockSpec((TM, D_MODEL), lambda i, j: (i, 0))
    return [row, hid, hid, hid], [jax.ShapeDtypeStruct((s, D_MODEL), BF16), hidden, hidden, hidden]


def _ffn_accumulate(h_ref, acc_scr, wg_ref, wu_ref, wd_ref, a_ref, b_ref, act_ref):
    h = h_ref[...]
    a = _dot(h, wg_ref[...])
    b = _dot(h, wu_ref[...])
    act = ((a * jax.nn.sigmoid(a)) * b).astype(BF16)
    a_ref[...] = a.astype(BF16)
    b_ref[...] = b.astype(BF16)
    act_ref[...] = act
    acc_scr[...] += _dot(act, wd_ref[...])


def _host_gather_before(gather, i, j, steps):
    @pl.when((i == 0) & (j == 0))
    def _():
        gather.start()

    @pl.when((i == steps // 2) & (j == 0))
    def _():
        gather.forward()


def _host_gather_after(gather, i, j, steps):
    @pl.when((i == steps - 1) & (j == N_CHIP - 1))
    def _():
        gather.finish()


def _ffn1_fwd(x, g1, gmix, gu, wd, later_shards):
    s = x.shape[0]
    row = pl.BlockSpec((TM, D_MODEL), lambda i, j: (i, 0))
    vec = pl.BlockSpec((1, D_MODEL), lambda i, j: (0, 0))
    saved_specs, saved_shapes = _ffn_saved(s)
    n = len(later_shards)
    any_spec = pl.BlockSpec(memory_space=pl.ANY)

    def body(*refs):
        x_ref, g_ref, gm_ref, wg_ref, wu_ref, wd_ref = refs[:6]
        shard_refs, refs = refs[6:6 + n], refs[6 + n:]
        x1_ref, hm_ref, h_ref, a_ref, b_ref, act_ref = refs[:6]
        gathered_refs, acc_scr, gather_scratch = refs[6:6 + n], refs[6 + n], refs[7 + n:]
        gather = _BackgroundGather(shard_refs, gathered_refs, gather_scratch)
        i, j = pl.program_id(0), pl.program_id(1)
        _host_gather_before(gather, i, j, s // TM)

        @pl.when(j == 0)
        def _():
            h, _, _ = _rms_fwd(x_ref[...], g_ref[...])
            h_ref[...] = h.astype(BF16)
            acc_scr[...] = jnp.zeros_like(acc_scr)

        _ffn_accumulate(h_ref, acc_scr, wg_ref, wu_ref, wd_ref, a_ref, b_ref, act_ref)

        @pl.when(j == N_CHIP - 1)
        def _():
            x1 = x_ref[...] + 0.5 * acc_scr[...]
            x1_ref[...] = x1
            hm, _, _ = _rms_fwd(x1, gm_ref[...])
            hm_ref[...] = hm.astype(BF16)

        _host_gather_after(gather, i, j, s // TM)

    x1, hm, h, a, b, act, *gathered = pl.pallas_call(
        body, name="ffn1_fwd", grid=(s // TM, N_CHIP),
        in_specs=[row, vec, vec] + _ffn_weight_specs() + [any_spec] * n,
        out_specs=[row, row] + saved_specs + [any_spec] * n,
        out_shape=([jax.ShapeDtypeStruct((s, D_MODEL), F32), jax.ShapeDtypeStruct((s, D_MODEL), BF16)] + saved_shapes
                   + _BackgroundGather.out_shapes(later_shards)),
        scratch_shapes=[pltpu.VMEM((TM, D_MODEL), F32)] + _BackgroundGather.scratch_shapes(later_shards),
        compiler_params=_params(58),
    )(x, g1, gmix, gu[0], gu[1], wd, *later_shards)
    return x1, hm, [h, a, b, act], gathered


def _ffn2_fwd_loss(x2, g2, gf, target, gu, wd):
    s = x2.shape[0]
    row = pl.BlockSpec((TM, D_MODEL), lambda i, j: (i, 0))
    vec = pl.BlockSpec((1, D_MODEL), lambda i, j: (0, 0))
    stat = pl.BlockSpec((8, D_MODEL), lambda i, j: (0, 0))
    saved_specs, saved_shapes = _ffn_saved(s)

    def body(x_ref, g_ref, gf_ref, t_ref, wg_ref, wu_ref, wd_ref, dx_ref, st_ref, h_ref, a_ref, b_ref, act_ref, acc_scr):
        i, j = pl.program_id(0), pl.program_id(1)

        @pl.when((i == 0) & (j == 0))
        def _():
            st_ref[...] = jnp.zeros_like(st_ref)

        @pl.when(j == 0)
        def _():
            h, _, _ = _rms_fwd(x_ref[...], g_ref[...])
            h_ref[...] = h.astype(BF16)
            acc_scr[...] = jnp.zeros_like(acc_scr)

        _ffn_accumulate(h_ref, acc_scr, wg_ref, wu_ref, wd_ref, a_ref, b_ref, act_ref)

        @pl.when(j == N_CHIP - 1)
        def _():
            x3 = x_ref[...] + 0.5 * acc_scr[...]
            y, xh, r = _rms_fwd(x3, gf_ref[...])
            err = y - t_ref[...]
            dx, dg = _rms_bwd(err * (1.0 / D_MODEL), xh, r, gf_ref[...])
            dx_ref[...] = dx
            st_ref[0:1, :] += dg
            st_ref[1:2, :] += jnp.sum(err * err, axis=0, keepdims=True)

    dx3, st, *saved = pl.pallas_call(
        body, name="ffn2_fwd_loss", grid=(s // TM, N_CHIP),
        in_specs=[row, vec, vec, row] + _ffn_weight_specs(),
        out_specs=[row, stat] + saved_specs,
        out_shape=[jax.ShapeDtypeStruct((s, D_MODEL), F32), jax.ShapeDtypeStruct((8, D_MODEL), F32)] + saved_shapes,
        scratch_shapes=[pltpu.VMEM((TM, D_MODEL), F32)],
        compiler_params=_params(56),
    )(x2, g2, gf, target, gu[0], gu[1], wd)
    return dx3, st, saved


def _ffn_bwd(xin, g, dy, saved, gu, wd, f):
    s = xin.shape[0]
    hb, gate, up, act = saved
    row = pl.BlockSpec((TM, D_MODEL), lambda i, j: (i, 0))
    vec = pl.BlockSpec((1, D_MODEL), lambda i, j: (0, 0))
    stat = pl.BlockSpec((8, D_MODEL), lambda i, j: (0, 0))
    hid = pl.BlockSpec((None, TM, FFB), lambda i, j: (j, i, 0))

    def body(x_ref, g_ref, dy_ref, a_ref, b_ref, wg_ref, wu_ref, wd_ref, out_ref, dyh_ref, da_ref, db_ref, st_ref, dh_scr):
        i, j = pl.program_id(0), pl.program_id(1)

        @pl.when((i == 0) & (j == 0))
        def _():
            st_ref[...] = jnp.zeros_like(st_ref)

        @pl.when(j == 0)
        def _():
            dyh_ref[...] = (0.5 * dy_ref[...]).astype(BF16)
            dh_scr[...] = jnp.zeros_like(dh_scr)

        a = a_ref[...].astype(F32)
        b = b_ref[...].astype(F32)
        sg = jax.nn.sigmoid(a)
        dact = _dot_nt(dyh_ref[...], wd_ref[...])
        dab = (dact * b * (sg * (1.0 + a * (1.0 - sg)))).astype(BF16)
        dbb = (dact * (a * sg)).astype(BF16)
        da_ref[...] = dab
        db_ref[...] = dbb
        dh_scr[...] += _dot_nt(dab, wg_ref[...]) + _dot_nt(dbb, wu_ref[...])

        @pl.when(j == N_CHIP - 1)
        def _():
            _, xh, r = _rms_fwd(x_ref[...], g_ref[...])
            dx, dg = _rms_bwd(dh_scr[...], xh, r, g_ref[...])
            out_ref[...] = dy_ref[...] + dx
            st_ref[0:1, :] += dg

    hidden = jax.ShapeDtypeStruct((N_CHIP, s, FFB), BF16)
    dx, dyh, da, db, st = pl.pallas_call(
        body, name=f"ffn{f + 1}_bwd_dx", grid=(s // TM, N_CHIP),
        in_specs=[row, vec, row, hid, hid] + _ffn_weight_specs(),
        out_specs=[row, row, hid, hid, stat],
        out_shape=[jax.ShapeDtypeStruct((s, D_MODEL), F32), jax.ShapeDtypeStruct((s, D_MODEL), BF16),
                   hidden, hidden, jax.ShapeDtypeStruct((8, D_MODEL), F32)],
        scratch_shapes=[pltpu.VMEM((TM, D_MODEL), F32)],
        compiler_params=_params(56),
    )(xin, g, dy, gate, up, gu[0], gu[1], wd)

    tok = pl.BlockSpec((TM, D_MODEL), lambda j, i: (i, 0))
    hid2 = pl.BlockSpec((None, TM, FFB), lambda j, i: (j, i, 0))
    gspecs = [pl.BlockSpec((None, D_MODEL, FFB), lambda j, i: (j, 0, 0)),
              pl.BlockSpec((None, D_MODEL, FFB), lambda j, i: (j, 0, 0)),
              pl.BlockSpec((None, FFB, D_MODEL), lambda j, i: (j, 0, 0))]

    def wbody(h_ref, dyh_ref, da_ref, db_ref, act_ref, dwg_ref, dwu_ref, dwd_ref):
        @pl.when(pl.program_id(1) == 0)
        def _():
            dwg_ref[...] = jnp.zeros_like(dwg_ref)
            dwu_ref[...] = jnp.zeros_like(dwu_ref)
            dwd_ref[...] = jnp.zeros_like(dwd_ref)

        hb = h_ref[...]
        dwg_ref[...] += _dot_tn(hb, da_ref[...])
        dwu_ref[...] += _dot_tn(hb, db_ref[...])
        dwd_ref[...] += _dot_tn(act_ref[...], dyh_ref[...])

    dwg, dwu, dwd = pl.pallas_call(
        wbody, name=f"ffn{f + 1}_bwd_dw", grid=(N_CHIP, s // TM),
        in_specs=[tok, tok, hid2, hid2, hid2], out_specs=gspecs,
        out_shape=[jax.ShapeDtypeStruct((N_CHIP, D_MODEL, FFB), F32),
                   jax.ShapeDtypeStruct((N_CHIP, D_MODEL, FFB), F32),
                   jax.ShapeDtypeStruct((N_CHIP, FFB, D_MODEL), F32)],
        compiler_params=_params(48),
    )(hb, dyh, da, db, act)
    return dx, dwg, dwu, dwd, st


def _rope_tables(s):
    half = HEAD_DIM // 2
    inv_freq = ROPE_THETA ** (-jnp.arange(half, dtype=F32) / half)
    ang = jnp.arange(s).astype(F32)[:, None] * inv_freq[None, :]
    cos, sin = jnp.cos(ang), jnp.sin(ang)
    cos2 = jnp.concatenate([cos, cos], axis=-1)
    sin2 = jnp.concatenate([-sin, sin], axis=-1)
    return jnp.tile(cos2, (1, LANES // HEAD_DIM)), jnp.tile(sin2, (1, LANES // HEAD_DIM))


def _rotate(t, cos, sin_signed):
    lane = lax.broadcasted_iota(jnp.int32, t.shape, 1)
    first = (lane % HEAD_DIM) < (HEAD_DIM // 2)
    partner = jnp.where(first, pltpu.roll(t, LANES - HEAD_DIM // 2, 1), pltpu.roll(t, HEAD_DIM // 2, 1))
    return t * cos + partner * sin_signed


def _proj_fwd(hm, win, cos, sin, later_shards):
    s = hm.shape[0]
    n_sub = INB // LANES
    first_rot, last_rot = (3 * D_SB) // LANES, (3 * D_SB + 2 * D_DIL) // LANES
    n = len(later_shards)
    any_spec = pl.BlockSpec(memory_space=pl.ANY)

    def body(*refs):
        h_ref, w_ref, c_ref, s_ref = refs[:4]
        shard_refs, o_ref, gathered_refs, gather_scratch = refs[4:4 + n], refs[4 + n], refs[5 + n:5 + 2 * n], refs[5 + 2 * n:]
        gather = _BackgroundGather(shard_refs, gathered_refs, gather_scratch)
        i, j = pl.program_id(0), pl.program_id(1)
        _host_gather_before(gather, i, j, s // TM)
        r = _dot(h_ref[...], w_ref[...])
        for c in range(n_sub):
            t = r[:, c * LANES:(c + 1) * LANES]
            col = j * n_sub + c
            rot = (col >= first_rot) & (col < last_rot)
            lanes = slice(c * LANES, (c + 1) * LANES)

            @pl.when(rot)
            def _():
                o_ref[:, lanes] = _rotate(t, c_ref[...], s_ref[...]).astype(BF16)

            @pl.when(jnp.logical_not(rot))
            def _():
                o_ref[:, lanes] = t.astype(BF16)

        _host_gather_after(gather, i, j, s // TM)

    qkv, *gathered = pl.pallas_call(
        body, name="proj_fwd", grid=(s // TM, N_CHIP),
        in_specs=[pl.BlockSpec((TM, D_MODEL), lambda i, j: (i, 0)),
                  pl.BlockSpec((None, D_MODEL, INB), lambda i, j: (j, 0, 0)),
                  pl.BlockSpec((TM, LANES), lambda i, j: (i, 0)),
                  pl.BlockSpec((TM, LANES), lambda i, j: (i, 0))] + [any_spec] * n,
        out_specs=[pl.BlockSpec((TM, INB), lambda i, j: (i, j))] + [any_spec] * n,
        out_shape=[jax.ShapeDtypeStruct((s, D_IN), BF16)] + _BackgroundGather.out_shapes(later_shards),
        scratch_shapes=_BackgroundGather.scratch_shapes(later_shards),
        compiler_params=_params(48),
    )(hm, win, cos, sin, *later_shards)
    return qkv, gathered


def _proj_bwd(x1, gmix, dqkv, win, dx2):
    s = x1.shape[0]
    row = pl.BlockSpec((TM, D_MODEL), lambda i, j: (i, 0))
    vec = pl.BlockSpec((1, D_MODEL), lambda i, j: (0, 0))

    def body(x_ref, g_ref, dq_ref, w_ref, dx2_ref, out_ref, dw_ref, st_ref, h_scr, dh_scr):
        i, j = pl.program_id(0), pl.program_id(1)

        @pl.when((i == 0) & (j == 0))
        def _():
            st_ref[...] = jnp.zeros_like(st_ref)
            dw_ref[...] = jnp.zeros_like(dw_ref)

        @pl.when(j == 0)
        def _():
            h, _, _ = _rms_fwd(x_ref[...], g_ref[...])
            h_scr[...] = h.astype(BF16)
            dh_scr[...] = jnp.zeros_like(dh_scr)

        dq = dq_ref[...]
        dw_ref[j] += _dot_tn(h_scr[...], dq)
        dh_scr[...] += _dot_nt(dq, w_ref[...])

        @pl.when(j == N_CHIP - 1)
        def _():
            _, xh, r = _rms_fwd(x_ref[...], g_ref[...])
            dx, dg = _rms_bwd(dh_scr[...], xh, r, g_ref[...])
            out_ref[...] = dx2_ref[...] + dx
            st_ref[0:1, :] += dg

    return pl.pallas_call(
        body, name="proj_bwd", grid=(s // TM, N_CHIP),
        in_specs=[row, vec, pl.BlockSpec((TM, INB), lambda i, j: (i, j)),
                  pl.BlockSpec((None, D_MODEL, INB), lambda i, j: (j, 0, 0)), row],
        out_specs=[row, pl.BlockSpec((N_CHIP, D_MODEL, INB), lambda i, j: (0, 0, 0)),
                   pl.BlockSpec((8, D_MODEL), lambda i, j: (0, 0))],
        out_shape=[jax.ShapeDtypeStruct((s, D_MODEL), F32),
                   jax.ShapeDtypeStruct((N_CHIP, D_MODEL, INB), F32),
                   jax.ShapeDtypeStruct((8, D_MODEL), F32)],
        scratch_shapes=[pltpu.VMEM((TM, D_MODEL), BF16), pltpu.VMEM((TM, D_MODEL), F32)],
        compiler_params=_params(56),
    )(x1, gmix, dqkv, win, dx2)


def _outproj_fwd(o_sb, o_dl, g_sb, g_dl, x1, wout):
    s = x1.shape[0]
    half = pl.BlockSpec((TM, D_SB), lambda i: (i, 0))
    row = pl.BlockSpec((TM, D_MODEL), lambda i: (i, 0))
    vec = pl.BlockSpec((1, D_SB), lambda i: (0, 0))

    def body(a_ref, b_ref, ga_ref, gb_ref, x_ref, w_ref, o_ref):
        ma, _, _ = _rms_fwd(a_ref[...], ga_ref[...])
        mb, _, _ = _rms_fwd(b_ref[...], gb_ref[...])
        o_ref[...] = (x_ref[...] + _dot(ma.astype(BF16), w_ref[0:D_SB, :])
                      + _dot(mb.astype(BF16), w_ref[D_SB:D_MODEL, :]))

    return pl.pallas_call(
        body, name="outproj_fwd", grid=(s // TM,),
        in_specs=[half, half, vec, vec, row, pl.BlockSpec((D_MODEL, D_MODEL), lambda i: (0, 0))],
        out_specs=row, out_shape=jax.ShapeDtypeStruct((s, D_MODEL), F32),
        compiler_params=_params(32),
    )(o_sb, o_dl, g_sb, g_dl, x1, wout)


def _outproj_bwd(dx2, o_sb, o_dl, g_sb, g_dl, wout):
    s = dx2.shape[0]
    half = pl.BlockSpec((TM, D_SB), lambda i: (i, 0))
    row = pl.BlockSpec((TM, D_MODEL), lambda i: (i, 0))
    vec = pl.BlockSpec((1, D_SB), lambda i: (0, 0))
    full = pl.BlockSpec((D_MODEL, D_MODEL), lambda i: (0, 0))

    def body(dy_ref, a_ref, b_ref, ga_ref, gb_ref, w_ref, da_ref, db_ref, dl_ref, dw_ref, st_ref):
        @pl.when(pl.program_id(0) == 0)
        def _():
            dw_ref[...] = jnp.zeros_like(dw_ref)
            st_ref[...] = jnp.zeros_like(st_ref)

        dy = dy_ref[...].astype(BF16)
        dm = _dot_nt(dy, w_ref[...])
        ma, xa, ra = _rms_fwd(a_ref[...], ga_ref[...])
        mb, xb, rb = _rms_fwd(b_ref[...], gb_ref[...])
        dw_ref[0:D_SB, :] += _dot_tn(ma.astype(BF16), dy)
        dw_ref[D_SB:D_MODEL, :] += _dot_tn(mb.astype(BF16), dy)
        da, dga = _rms_bwd(dm[:, 0:D_SB], xa, ra, ga_ref[...])
        db, dgb = _rms_bwd(dm[:, D_SB:D_MODEL], xb, rb, gb_ref[...])
        da_ref[...] = da
        db_ref[...] = db
        r = lax.broadcasted_iota(jnp.int32, (LANES, LANES), 0) >= HEAD_DIM
        c = lax.broadcasted_iota(jnp.int32, (LANES, LANES), 1) >= HEAD_DIM
        same_head = jnp.where(r == c, 1.0, 0.0).astype(BF16)
        same_head = jnp.concatenate([same_head, same_head], axis=0)
        prod = db * b_ref[...]
        for k in range(D_DIL // LANES):
            lanes = slice(k * LANES, (k + 1) * LANES)
            dl_ref[:, lanes] = _dot_split(prod[:, lanes], same_head)
        st_ref[0:1, :] += dga
        st_ref[1:2, :] += dgb

    return pl.pallas_call(
        body, name="outproj_bwd", grid=(s // TM,),
        in_specs=[row, half, half, vec, vec, full],
        out_specs=[half, half, half, full, pl.BlockSpec((8, D_SB), lambda i: (0, 0))],
        out_shape=[jax.ShapeDtypeStruct((s, D_SB), F32), jax.ShapeDtypeStruct((s, D_SB), F32),
                   jax.ShapeDtypeStruct((s, D_DIL), F32),
                   jax.ShapeDtypeStruct((D_MODEL, D_MODEL), F32), jax.ShapeDtypeStruct((8, D_SB), F32)],
        compiler_params=_params(48),
    )(dx2, o_sb, o_dl, g_sb, g_dl, wout)


def _head_masks():
    lane = lax.broadcasted_iota(jnp.int32, (BLK, LANES), 1)
    return [lane < HEAD_DIM, lane >= HEAD_DIM]


def _keep(mask, a):
    return a * jnp.where(mask, 1.0, 0.0).astype(a.dtype)


def _suffix_matrices():
    r = lax.broadcasted_iota(jnp.int32, (2 * BLK, BLK), 0) & (BLK - 1)
    c = lax.broadcasted_iota(jnp.int32, (2 * BLK, BLK), 1)
    ones = jnp.ones((2 * BLK, BLK), BF16)
    excl = jnp.concatenate([(r > c).astype(BF16), ones], axis=1)
    incl = jnp.concatenate([(r >= c).astype(BF16), ones], axis=1)
    return excl, incl


def _blk(i):
    return pl.ds(pl.multiple_of(i * BLK, BLK), BLK)


def _alive(carry_m):
    return (jnp.max(carry_m) > DEAD).astype(jnp.int32)


def _more_keys(last, carry):
    return (carry[0] * SB_KB <= last) & (carry[1] > 0)


def _stack_heads(a):
    masks = _head_masks()
    return jnp.concatenate([_keep(masks[0], a), _keep(masks[1], a)], axis=0)


def _unstack_heads(a2):
    return jnp.where(_head_masks()[0], a2[:BLK], a2[BLK:])


def _head_rowsum(a):
    masks = _head_masks()
    return jnp.concatenate([jnp.sum(jnp.where(m, a, 0.0), axis=1, keepdims=True) for m in masks], axis=0)


SB_QB = 2
SB_ROWS = SB_QB * 2 * BLK
SB_KB = 4
PAST_START = 1 << 30


def _sb_rows(ref, i0, cast=None):
    tiles = [ref[_blk(i0 + t), :] for t in range(SB_QB)]
    return jnp.concatenate([_stack_heads(t if cast is None else t.astype(cast)) for t in tiles], axis=0)


def _sb_scores(q2, k, i, j, carry_m, u_excl):
    r = lax.broadcasted_iota(jnp.int32, (SB_ROWS, BLK), 0)
    row = (r & (BLK - 1)) + ((r >> 8) << 7)
    col = lax.broadcasted_iota(jnp.int32, (SB_ROWS, BLK), 1)
    valid = (jnp.where(j >= 0, j * BLK, PAST_START) + col) < (i * BLK + row)
    z = _dot_nt(q2, k) * SCALE
    sp = jnp.maximum(z, 0.0) + jnp.log(1.0 + jnp.exp(-jnp.abs(z)))
    log_stay = jnp.where(valid, -sp, 0.0)
    log_beta = z - sp
    sums = _dot_split(log_stay, u_excl)
    later = carry_m + sums[:, :BLK]
    w = jnp.where(valid, jnp.exp(log_beta + later), 0.0)
    return valid, log_beta, w, carry_m + sums[:, BLK:]


def _sb_fwd(qkv):
    s = qkv.shape[0]
    nq = s // BLK
    pairs = D_SB // LANES
    col = lambda off: pl.BlockSpec((s, LANES), lambda p: (0, off + p))

    def body(q_ref, k_ref, v_ref, o_ref):
        u_excl, _ = _suffix_matrices()
        zero = jnp.zeros((SB_ROWS, LANES), F32)

        def q_block(ib, _):
            i = ib * SB_QB
            last = i + SB_QB - 1
            q2 = _sb_rows(q_ref, i)

            def k_block(carry):
                jj, _, carry_m, acc = carry
                for t in range(SB_KB):
                    j = last - jj * SB_KB - t
                    at = _blk(jnp.maximum(j, 0))
                    _, _, w, carry_m = _sb_scores(q2, k_ref[at, :], i, j, carry_m, u_excl)
                    acc = acc + _dot(w.astype(BF16), v_ref[at, :])
                return jj + 1, _alive(carry_m), carry_m, acc

            _, _, _, acc = lax.while_loop(functools.partial(_more_keys, last), k_block,
                                          (jnp.int32(0), jnp.int32(1), zero, zero))
            for t in range(SB_QB):
                o_ref[_blk(i + t), :] = _unstack_heads(acc[2 * BLK * t:2 * BLK * (t + 1)])
            return 0

        lax.fori_loop(0, nq // SB_QB, q_block, 0)

    return pl.pallas_call(
        body, name="sb_fwd", grid=(pairs,),
        in_specs=[col(0), col(pairs), col(2 * pairs)],
        out_specs=pl.BlockSpec((s, LANES), lambda p: (0, p)),
        out_shape=jax.ShapeDtypeStruct((s, D_SB), F32),
        compiler_params=_params(48),
    )(qkv, qkv, qkv)


def _sb_bwd(qkv, o_sb, do_sb):
    s = qkv.shape[0]
    nq = s // BLK
    pairs = D_SB // LANES
    col = lambda off: pl.BlockSpec((s, LANES), lambda p: (0, off + p))
    own = pl.BlockSpec((s, LANES), lambda p: (0, p))

    def body(q_ref, k_ref, v_ref, o_ref, do_ref, dq_ref, dk_ref, dv_ref, dk_acc, dv_acc):
        u_excl, u_incl = _suffix_matrices()
        zero = jnp.zeros((SB_ROWS, LANES), F32)
        dk_acc[...] = jnp.zeros_like(dk_acc)
        dv_acc[...] = jnp.zeros_like(dv_acc)

        def q_block(ib, _):
            i = ib * SB_QB
            last = i + SB_QB - 1
            q2 = _sb_rows(q_ref, i)
            do2 = _sb_rows(do_ref, i, BF16)
            totals = [_head_rowsum(do_ref[_blk(i + t), :].astype(BF16).astype(F32) * o_ref[_blk(i + t), :])
                      for t in range(SB_QB)]
            total = jnp.broadcast_to(jnp.concatenate(totals, axis=0), (SB_ROWS, BLK))

            def k_block(carry):
                jj, _, carry_m, carry_g, dq = carry
                for t in range(SB_KB):
                    j = last - jj * SB_KB - t
                    at = _blk(jnp.maximum(j, 0))
                    k = k_ref[at, :]
                    valid, log_beta, w, carry_m = _sb_scores(q2, k, i, j, carry_m, u_excl)
                    wb = w.astype(BF16)
                    g = wb.astype(F32) * _dot_nt(do2, v_ref[at, :])
                    sums = _dot_split(g, u_incl)
                    before = total - (carry_g + sums[:, :BLK])
                    dz = jnp.where(valid, g - jnp.exp(log_beta) * (g + before), 0.0)
                    dzb = (dz * SCALE).astype(BF16)
                    dk_acc[at, :] += _dot_tn(dzb, q2)
                    dv_acc[at, :] += _dot_tn(wb, do2)
                    carry_g = carry_g + sums[:, BLK:]
                    dq = dq + _dot(dzb, k)
                return jj + 1, _alive(carry_m), carry_m, carry_g, dq

            _, _, _, _, dq = lax.while_loop(functools.partial(_more_keys, last), k_block,
                                            (jnp.int32(0), jnp.int32(1), zero, zero, zero))
            for t in range(SB_QB):
                dq_ref[_blk(i + t), :] = _unstack_heads(dq[2 * BLK * t:2 * BLK * (t + 1)]).astype(BF16)
            return 0

        lax.fori_loop(0, nq // SB_QB, q_block, 0)
        dk_ref[...] = dk_acc[...].astype(BF16)
        dv_ref[...] = dv_acc[...].astype(BF16)

    return pl.pallas_call(
        body, name="sb_bwd", grid=(pairs,),
        in_specs=[col(0), col(pairs), col(2 * pairs), own, own],
        out_specs=[own, own, own],
        out_shape=[jax.ShapeDtypeStruct((s, D_SB), BF16)] * 3,
        scratch_shapes=[pltpu.VMEM((s, LANES), F32), pltpu.VMEM((s, LANES), F32)],
        compiler_params=_params(56),
    )(qkv, qkv, qkv, o_sb, do_sb)


DIL_UNROLL = 4


def _band_masks(b):
    row = lax.broadcasted_iota(jnp.int32, (2 * BLK, BLK), 0) & (BLK - 1)
    col = lax.broadcasted_iota(jnp.int32, (2 * BLK, BLK), 1)
    return col <= row, (col - row) >= jnp.where(b > 0, 0, BLK)


def _dil_tiles(qf, kf, vf, d, t, nb):
    c, b = t // nb, t % nb
    start = c + d * BLK * b
    rows = pl.ds(start, BLK, stride=d)
    prev = pl.ds(jnp.where(b > 0, start - d * BLK, start), BLK, stride=d)
    bf = lambda ref, sl: ref[sl, :].astype(BF16)
    return b, rows, prev, _stack_heads(bf(qf, rows)), bf(kf, rows), bf(kf, prev), bf(vf, rows), bf(vf, prev)


def _lanes_of_heads(col2):
    return _unstack_heads(jnp.broadcast_to(col2, (2 * BLK, LANES)))


def _dilated_fwd(qkv):
    s = qkv.shape[0]
    pairs = D_DIL // LANES
    base = (3 * D_SB) // LANES
    col = lambda off: pl.BlockSpec((s, LANES), lambda p: (0, off + p))
    own = pl.BlockSpec((s, LANES), lambda p: (0, p))

    def body(q_ref, k_ref, v_ref, acc_ref, m_ref, qf, kf, vf, l_scr):
        qf[...] = q_ref[...].astype(F32)
        kf[...] = k_ref[...].astype(F32)
        vf[...] = v_ref[...].astype(F32)
        for d in DILATIONS:
            nb = s // (d * BLK)

            def block(t, _):
                b, rows, prev, q2, kc, kp, vc, vp = _dil_tiles(qf, kf, vf, d, t, nb)
                in_cur, in_prev = _band_masks(b)
                zc = jnp.where(in_cur, _dot_nt(q2, kc) * SCALE, NEG)
                zp = jnp.where(in_prev, _dot_nt(q2, kp) * SCALE, NEG)
                m = jnp.maximum(jnp.max(zc, axis=1, keepdims=True), jnp.max(zp, axis=1, keepdims=True))
                pc, pp = jnp.exp(zc - m), jnp.exp(zp - m)
                den = jnp.sum(pc, axis=1, keepdims=True) + jnp.sum(pp, axis=1, keepdims=True)
                acc = _unstack_heads(_dot(pc.astype(BF16), vc) + _dot(pp.astype(BF16), vp))
                m_t, l_t = _lanes_of_heads(m), _lanes_of_heads(den)
                if d == DILATIONS[0]:
                    m_ref[rows, :] = m_t
                    l_scr[rows, :] = l_t
                    acc_ref[rows, :] = acc
                else:
                    m_old = m_ref[rows, :]
                    m_new = jnp.maximum(m_old, m_t)
                    keep, add = jnp.exp(m_old - m_new), jnp.exp(m_t - m_new)
                    m_ref[rows, :] = m_new
                    l_scr[rows, :] = l_scr[rows, :] * keep + l_t * add
                    acc_ref[rows, :] = acc_ref[rows, :] * keep + acc * add
                return 0

            lax.fori_loop(0, s // BLK, block, 0, unroll=DIL_UNROLL)

        def finish(i, _):
            l = l_scr[_blk(i), :]
            acc_ref[_blk(i), :] = acc_ref[_blk(i), :] / l
            m_ref[_blk(i), :] = m_ref[_blk(i), :] + jnp.log(l)
            return 0

        lax.fori_loop(0, s // BLK, finish, 0)

    return pl.pallas_call(
        body, name="dilated_fwd", grid=(pairs,),
        in_specs=[col(base), col(base + pairs), col(base + 2 * pairs)],
        out_specs=[own, own],
        out_shape=[jax.ShapeDtypeStruct((s, D_DIL), F32)] * 2,
        scratch_shapes=[pltpu.VMEM((s, LANES), F32)] * 4,
        compiler_params=_params(56),
    )(qkv, qkv, qkv)


def _stack_lanes(t):
    other = pltpu.roll(t, HEAD_DIM, 1)
    first = _head_masks()[0]
    return jnp.concatenate([jnp.where(first, t, other), jnp.where(first, other, t)], axis=0)


def _dilated_bwd(qkv, delta, lse, dout):
    s = qkv.shape[0]
    pairs = D_DIL // LANES
    base = (3 * D_SB) // LANES
    once = pl.Buffered(1)
    col = lambda off: pl.BlockSpec((s, LANES), lambda p: (0, off + p), pipeline_mode=once)
    own = pl.BlockSpec((s, LANES), lambda p: (0, p), pipeline_mode=once)
    res = pl.BlockSpec((s, LANES), lambda p: (0, p))

    def body(q_ref, k_ref, v_ref, dl_ref, l_ref, do_ref, dq_ref, dk_ref, dv_ref, qf, kf, vf):
        qf[...] = q_ref[...].astype(F32)
        kf[...] = k_ref[...].astype(F32)
        vf[...] = v_ref[...].astype(F32)
        dq_ref[...] = jnp.zeros_like(dq_ref)
        dk_ref[...] = jnp.zeros_like(dk_ref)
        dv_ref[...] = jnp.zeros_like(dv_ref)
        for d in DILATIONS:
            nb = s // (d * BLK)

            def block(t, _):
                b, rows, prev, q2, kc, kp, vc, vp = _dil_tiles(qf, kf, vf, d, t, nb)
                in_cur, in_prev = _band_masks(b)
                do2 = _stack_heads(do_ref[rows, :].astype(BF16))
                delta = _stack_lanes(dl_ref[rows, :])
                lse2 = _stack_lanes(l_ref[rows, :])
                wc = jnp.exp(jnp.where(in_cur, _dot_nt(q2, kc) * SCALE, NEG) - lse2)
                wp = jnp.exp(jnp.where(in_prev, _dot_nt(q2, kp) * SCALE, NEG) - lse2)
                dzc = (wc * (_dot_nt(do2, vc) - delta) * SCALE).astype(BF16)
                dzp = (wp * (_dot_nt(do2, vp) - delta) * SCALE).astype(BF16)
                dq_ref[rows, :] += _unstack_heads(_dot(dzc, kc) + _dot(dzp, kp))
                dk_ref[rows, :] += _dot_tn(dzc, q2)
                dk_ref[prev, :] += _dot_tn(dzp, q2)
                dv_ref[rows, :] += _dot_tn(wc.astype(BF16), do2)
                dv_ref[prev, :] += _dot_tn(wp.astype(BF16), do2)
                return 0

            lax.fori_loop(0, s // BLK, block, 0, unroll=DIL_UNROLL)

    return pl.pallas_call(
        body, name="dilated_bwd", grid=(pairs,),
        in_specs=[col(base), col(base + pairs), col(base + 2 * pairs), own, own, own],
        out_specs=[res, res, res],
        out_shape=[jax.ShapeDtypeStruct((s, D_DIL), F32)] * 3,
        scratch_shapes=[pltpu.VMEM((s, LANES), F32)] * 3,
        compiler_params=_params(60),
    )(qkv, qkv, qkv, delta, lse, dout)


def _dilated_finish(grads, cos, sin):
    s = grads[0].shape[0]
    spec = pl.BlockSpec((TM, D_DIL), lambda i: (i, 0))
    tab = pl.BlockSpec((TM, LANES), lambda i: (i, 0))

    def body(dq_ref, dk_ref, dv_ref, c_ref, s_ref, oq_ref, ok_ref, ov_ref):
        for src, dst, rotated in ((dq_ref, oq_ref, True), (dk_ref, ok_ref, True), (dv_ref, ov_ref, False)):
            for c in range(D_DIL // LANES):
                lanes = slice(c * LANES, (c + 1) * LANES)
                piece = src[:, lanes]
                dst[:, lanes] = (_rotate(piece, c_ref[...], -s_ref[...]) if rotated else piece).astype(BF16)

    return pl.pallas_call(
        body, name="dilated_finish", grid=(s // TM,),
        in_specs=[spec] * 3 + [tab, tab], out_specs=[spec] * 3,
        out_shape=[jax.ShapeDtypeStruct((s, D_DIL), BF16)] * 3,
        compiler_params=_params(32),
    )(*grads, cos, sin)


def _place():
    x, y, c = lax.axis_index("x"), lax.axis_index("y"), lax.axis_index("c")
    return x, y, c, 2 * x + y


def _chip(k, c):
    return (k >> 1, k & 1, c)


def _half(ref, h):
    n = ref.shape[0] // 2
    return ref.at[pl.ds(h * n, n)]


def _all_gather(shards):
    na = len(shards)
    any_spec = pl.BlockSpec(memory_space=pl.ANY)

    def body(*refs):
        ins, outs = refs[:na], refs[na:2 * na]
        send_sem, recv_sem, local_sem = refs[2 * na:]
        x, y, c, k = _place()
        sibling = (x, y, 1 - c)
        started = []
        for a in range(na):
            cp = pltpu.make_async_copy(ins[a], outs[a].at[k], local_sem.at[a])
            cp.start()
            started.append(cp)

        def copy(a, slot, src, dst, to):
            return pltpu.make_async_remote_copy(src_ref=src, dst_ref=dst, send_sem=send_sem.at[a * 6 + slot],
                                                recv_sem=recv_sem.at[a * 6 + slot], device_id=to, device_id_type=MESH)

        sends = []
        for a in range(na):
            for j in range(1, N_CHIP):
                cp = copy(a, j - 1, _half(ins[a], c), _half(outs[a].at[k], c), _chip(k ^ j, c))
                cp.start()
                sends.append(cp)
        for j in range(1, N_CHIP):
            for a in range(na):
                landed = _half(outs[a].at[k ^ j], c)
                copy(a, j - 1, landed, landed, sibling).wait_recv()
                cp = copy(a, 2 + j, landed, landed, sibling)
                cp.start()
                sends.append(cp)
        for j in range(1, N_CHIP):
            for a in range(na):
                passed = _half(outs[a].at[k ^ j], 1 - c)
                copy(a, 2 + j, passed, passed, sibling).wait_recv()
        for cp in sends:
            cp.wait_send()
        for cp in started:
            cp.wait()

    return pl.pallas_call(
        body, name="weights_all_gather",
        in_specs=[any_spec] * na, out_specs=[any_spec] * na,
        out_shape=[jax.ShapeDtypeStruct((N_CHIP,) + a.shape, a.dtype) for a in shards],
        scratch_shapes=[pltpu.SemaphoreType.DMA((6 * na,)), pltpu.SemaphoreType.DMA((6 * na,)),
                        pltpu.SemaphoreType.DMA((na,))],
    )(*shards)


class _BackgroundGather:
    def __init__(self, ins, outs, scratch):
        n = self.n = len(ins)
        self.ins, self.outs = ins, outs
        self.mine, self.landed, self.passed = scratch[0:3 * n:3], scratch[1:3 * n:3], scratch[2:3 * n:3]
        self.send_sem, self.recv_sem, self.local_sem = scratch[3 * n:3 * n + 3]
        x, y, self.c, self.k = _place()
        self.sibling = (x, y, 1 - self.c)

    @staticmethod
    def scratch_shapes(shards):
        shapes = []
        for a in shards:
            half = (N_CHIP - 1, a.shape[0] // 2, a.shape[1])
            shapes += [pltpu.VMEM(a.shape, a.dtype), pltpu.VMEM(half, a.dtype), pltpu.VMEM(half, a.dtype)]
        n = len(shards)
        return shapes + [pltpu.SemaphoreType.DMA((6 * n,)), pltpu.SemaphoreType.DMA((6 * n,)),
                         pltpu.SemaphoreType.DMA((8 * n,))]

    @staticmethod
    def out_shapes(shards):
        return [jax.ShapeDtypeStruct((N_CHIP,) + a.shape, a.dtype) for a in shards]

    def _remote(self, a, slot, src, dst, to):
        return pltpu.make_async_remote_copy(src_ref=src, dst_ref=dst, send_sem=self.send_sem.at[6 * a + slot],
                                            recv_sem=self.recv_sem.at[6 * a + slot], device_id=to, device_id_type=MESH)

    def _local(self, a, slot, src, dst):
        return pltpu.make_async_copy(src, dst, self.local_sem.at[8 * a + slot])

    def _ici(self, a, j):
        return self._remote(a, j - 1, _half(self.mine[a], self.c), self.landed[a].at[j - 1], _chip(self.k ^ j, self.c))

    def _to_sibling(self, a, j):
        return self._remote(a, 2 + j, self.landed[a].at[j - 1], self.passed[a].at[j - 1], self.sibling)

    def _own(self, a):
        return self._local(a, 0, self.ins[a], self.outs[a].at[self.k])

    def _load(self, a):
        return self._local(a, 1, self.ins[a], self.mine[a])

    def _store_landed(self, a, j):
        return self._local(a, 1 + j, self.landed[a].at[j - 1], _half(self.outs[a].at[self.k ^ j], self.c))

    def _store_passed(self, a, j):
        return self._local(a, 4 + j, self.passed[a].at[j - 1], _half(self.outs[a].at[self.k ^ j], 1 - self.c))

    def start(self):
        for a in range(self.n):
            self._own(a).start()
            self._load(a).start()
        for a in range(self.n):
            self._load(a).wait()
            for j in range(1, N_CHIP):
                self._ici(a, j).start()

    def forward(self):
        for j in range(1, N_CHIP):
            for a in range(self.n):
                self._ici(a, j).wait_recv()
                self._to_sibling(a, j).start()
                self._store_landed(a, j).start()

    def finish(self):
        for j in range(1, N_CHIP):
            for a in range(self.n):
                self._to_sibling(a, j).wait_recv()
                self._store_passed(a, j).start()
        for a in range(self.n):
            for j in range(1, N_CHIP):
                self._ici(a, j).wait_send()
                self._to_sibling(a, j).wait_send()
                self._store_landed(a, j).wait()
                self._store_passed(a, j).wait()
            self._own(a).wait()


def _reduce_scatter(g, core, name):
    n, r, c = g.shape
    hr = r // 2
    once = pl.Buffered(1)
    in_specs = [pl.BlockSpec((n, hr, c), lambda i, core_ref: (0, core_ref[0], 0), pipeline_mode=once),
                pl.BlockSpec((n, hr, c), lambda i, core_ref: (0, 1 - core_ref[0], 0), pipeline_mode=once)]

    def body(core_ref, mine_ref, other_ref, out_ref, from_core, sums, sums_bf, from_chips, done, from_core2, send_sem, recv_sem):
        x, y, cc, k = _place()
        sibling = (x, y, 1 - cc)

        def copy(slot, src, dst, to):
            return pltpu.make_async_remote_copy(src_ref=src, dst_ref=dst, send_sem=send_sem.at[slot],
                                                recv_sem=recv_sem.at[slot], device_id=to, device_id_type=MESH)

        first = copy(0, other_ref, from_core, sibling)
        first.start()
        first.wait()
        total = mine_ref[...] + from_core[...]
        sums[...] = total
        sums_bf[...] = total.astype(BF16)
        sends = [copy(j, sums_bf.at[k ^ j], from_chips.at[j - 1], _chip(k ^ j, cc)) for j in range(1, N_CHIP)]
        for cp in sends:
            cp.start()
        for cp in sends:
            cp.wait()
        red = sums[k]
        for j in range(1, N_CHIP):
            red = red + from_chips[j - 1].astype(F32)
        done[...] = red
        last = copy(N_CHIP, done, from_core2, sibling)
        last.start()
        last.wait()
        row0 = pl.multiple_of(cc * hr, 8)
        row1 = pl.multiple_of((1 - cc) * hr, 8)
        out_ref[pl.ds(row0, hr), :] = red
        out_ref[pl.ds(row1, hr), :] = from_core2[...]

    grid_spec = pltpu.PrefetchScalarGridSpec(
        num_scalar_prefetch=1, grid=(1,), in_specs=in_specs,
        out_specs=pl.BlockSpec((r, c), lambda i, core_ref: (0, 0)),
        scratch_shapes=[pltpu.VMEM((n, hr, c), F32), pltpu.VMEM((n, hr, c), F32), pltpu.VMEM((n, hr, c), BF16),
                        pltpu.VMEM((N_CHIP - 1, hr, c), BF16), pltpu.VMEM((hr, c), F32), pltpu.VMEM((hr, c), F32),
                        pltpu.SemaphoreType.DMA((N_CHIP + 1,)), pltpu.SemaphoreType.DMA((N_CHIP + 1,))])
    return pl.pallas_call(
        body, name=name, grid_spec=grid_spec, out_shape=jax.ShapeDtypeStruct((r, c), F32),
        compiler_params=_params(56),
    )(core, g, g)


def _elementwise(fn, name, ins, n_out, rows):
    total, cols = ins[0].shape
    spec = pl.BlockSpec((rows, cols), lambda i: (i, 0))

    def body(*refs):
        res = fn(*[r[...] for r in refs[:len(ins)]])
        for o, v in zip(refs[len(ins):], res):
            o[...] = v

    return pl.pallas_call(
        body, name=name, grid=(total // rows,),
        in_specs=[spec] * len(ins), out_specs=[spec] * n_out,
        out_shape=[jax.ShapeDtypeStruct((total, cols), F32)] * n_out,
        compiler_params=_params(48),
    )(*ins)


def _adamw(w, g, m, v):
    m = ADAM_B1 * m + (1.0 - ADAM_B1) * g
    v = ADAM_B2 * v + (1.0 - ADAM_B2) * (g * g)
    m_hat = m / (1.0 - ADAM_B1 ** ADAM_STEP)
    v_hat = v / (1.0 - ADAM_B2 ** ADAM_STEP)
    delta = -ADAM_LR * (m_hat / (jnp.sqrt(v_hat) + ADAM_EPS) + ADAM_WD * w)
    return delta, m, v


def _reduce_and_update(grads, weights, moms, vels):
    core = lax.axis_index("c").astype(jnp.int32).reshape(1)
    full = [_reduce_scatter(g, core, f"grads_reduce_scatter_{a}") for a, g in enumerate(grads)]
    out = []
    for a, (g, w, m, v) in enumerate(zip(full, weights, moms, vels)):
        rows = g.shape[0] // 2
        out.append((g,) + tuple(_elementwise(lambda gg, ww, mm, vv: _adamw(ww, gg, mm, vv), f"adamw_{a}", [g, w, m, v], 3, rows)))
    return out


def _reduce_vectors(part, w, m, v):
    n_dev = 8

    def body(p_ref, w_ref, m_ref, v_ref, g_ref, d_ref, nm_ref, nv_ref, buf, send_sem, recv_sem):
        x, y, c, _ = _place()
        me = 4 * x + 2 * y + c
        buf[me] = p_ref[...]
        sends = []
        for off in range(1, n_dev):
            peer = me ^ off
            cp = pltpu.make_async_remote_copy(src_ref=p_ref, dst_ref=buf.at[me], send_sem=send_sem.at[off - 1],
                                              recv_sem=recv_sem.at[off - 1], device_id=(peer >> 2, (peer >> 1) & 1, peer & 1),
                                              device_id_type=MESH)
            cp.start()
            sends.append(cp)
        for off in range(1, n_dev):
            peer = me ^ off
            pltpu.make_async_remote_copy(src_ref=p_ref, dst_ref=buf.at[peer], send_sem=send_sem.at[off - 1],
                                         recv_sem=recv_sem.at[off - 1], device_id=(peer >> 2, (peer >> 1) & 1, peer & 1),
                                         device_id_type=MESH).wait_recv()
        for cp in sends:
            cp.wait_send()
        g = buf[0]
        for d in range(1, n_dev):
            g = g + buf[d]
        g_ref[...] = g
        delta, nm, nv = _adamw(w_ref[...], g, m_ref[...], v_ref[...])
        d_ref[...] = delta
        nm_ref[...] = nm
        nv_ref[...] = nv

    vm = pl.BlockSpec(memory_space=pltpu.VMEM)
    return pl.pallas_call(
        body, name="gains_all_reduce",
        in_specs=[vm] * 4, out_specs=[vm] * 4,
        out_shape=[jax.ShapeDtypeStruct(part.shape, F32)] * 4,
        scratch_shapes=[pltpu.VMEM((n_dev,) + part.shape, F32), pltpu.SemaphoreType.DMA((n_dev - 1,)),
                        pltpu.SemaphoreType.DMA((n_dev - 1,))],
    )(part, w, m, v)


def _pad_row(a):
    a = a.reshape(1, -1)
    return jnp.pad(a, ((0, 0), (0, D_MODEL - a.shape[1])))


def kernel(x, ffn1_norm, ffn1_w_gate, ffn1_w_up, ffn1_w_down, mix_norm, w_in, sb_out_norm, dil_out_norm, w_out, ffn2_norm, ffn2_w_gate, ffn2_w_up, ffn2_w_down, final_norm, loss_target, m_ffn1_norm, m_ffn1_w_gate, m_ffn1_w_up, m_ffn1_w_down, m_mix_norm, m_w_in, m_sb_out_norm, m_dil_out_norm, m_w_out, m_ffn2_norm, m_ffn2_w_gate, m_ffn2_w_up, m_ffn2_w_down, m_final_norm, v_ffn1_norm, v_ffn1_w_gate, v_ffn1_w_up, v_ffn1_w_down, v_mix_norm, v_w_in, v_sb_out_norm, v_dil_out_norm, v_w_out, v_ffn2_norm, v_ffn2_w_gate, v_ffn2_w_up, v_ffn2_w_down, v_final_norm):
    x = x[0]
    target = loss_target[0]
    s = x.shape[0]
    gf = final_norm.reshape(1, D_MODEL)
    cos, sin = _rope_tables(s)

    shard = lambda w: w[0].astype(BF16)
    wg1, wu1, wd1 = _all_gather([shard(ffn1_w_gate), shard(ffn1_w_up), shard(ffn1_w_down)])

    x1, hm, saved1, (win, wout) = _ffn1_fwd(x, ffn1_norm, mix_norm, (wg1, wu1), wd1, [shard(w_in), shard(w_out)])
    wout = wout.reshape(D_MODEL, D_MODEL)
    qkv, (wg2, wu2, wd2) = _proj_fwd(hm, win, cos, sin, [shard(ffn2_w_gate), shard(ffn2_w_up), shard(ffn2_w_down)])
    o_sb = _sb_fwd(qkv)
    o_dl, lse = _dilated_fwd(qkv)
    x2 = _outproj_fwd(o_sb, o_dl, sb_out_norm, dil_out_norm, x1, wout)
    dx3, st_final, saved2 = _ffn2_fwd_loss(x2, ffn2_norm, gf, target, (wg2, wu2), wd2)

    dx2, dwg2, dwu2, dwd2, st_ffn2 = _ffn_bwd(x2, ffn2_norm, dx3, saved2, (wg2, wu2), wd2, 1)
    do_sb, do_dl, delta_dl, dwout, st_out = _outproj_bwd(dx2, o_sb, o_dl, sb_out_norm, dil_out_norm, wout)
    dq_sb, dk_sb, dv_sb = _sb_bwd(qkv, o_sb, do_sb)
    dq_dl, dk_dl, dv_dl = _dilated_finish(_dilated_bwd(qkv, delta_dl, lse, do_dl), cos, sin)
    dqkv = jnp.concatenate([dq_sb, dk_sb, dv_sb, dq_dl, dk_dl, dv_dl], axis=1)
    dx1, dwin, st_mix = _proj_bwd(x1, mix_norm, dqkv, win, dx2)
    grad_x, dwg1, dwu1, dwd1, st_ffn1 = _ffn_bwd(x, ffn1_norm, dx1, saved1, (wg1, wu1), wd1, 0)

    names = ["ffn1_w_gate", "ffn1_w_up", "ffn1_w_down", "w_in", "w_out", "ffn2_w_gate", "ffn2_w_up", "ffn2_w_down"]
    grads = [dwg1, dwu1, dwd1, dwin, dwout.reshape(N_CHIP, OUTB, D_MODEL), dwg2, dwu2, dwd2]
    weights = [ffn1_w_gate[0], ffn1_w_up[0], ffn1_w_down[0], w_in[0], w_out[0], ffn2_w_gate[0], ffn2_w_up[0], ffn2_w_down[0]]
    moms = [m_ffn1_w_gate[0], m_ffn1_w_up[0], m_ffn1_w_down[0], m_w_in[0], m_w_out[0], m_ffn2_w_gate[0], m_ffn2_w_up[0], m_ffn2_w_down[0]]
    vels = [v_ffn1_w_gate[0], v_ffn1_w_up[0], v_ffn1_w_down[0], v_w_in[0], v_w_out[0], v_ffn2_w_gate[0], v_ffn2_w_up[0], v_ffn2_w_down[0]]
    mats = {n: tuple(t[None] for t in r) for n, r in zip(names, _reduce_and_update(grads, weights, moms, vels))}

    vec_names = ["ffn1_norm", "mix_norm", "sb_out_norm", "dil_out_norm", "ffn2_norm", "final_norm"]
    part = jnp.concatenate([st_ffn1[0:1], st_mix[0:1], _pad_row(st_out[0]), _pad_row(st_out[1]), st_ffn2[0:1],
                            st_final[0:1], st_final[1:2], jnp.zeros((1, D_MODEL), F32)], axis=0)
    pack = lambda arrs: jnp.concatenate([_pad_row(a) for a in arrs] + [jnp.zeros((2, D_MODEL), F32)], axis=0)
    g_vec, d_vec, m_vec, v_vec = _reduce_vectors(
        part,
        pack([ffn1_norm, mix_norm, sb_out_norm, dil_out_norm, ffn2_norm, final_norm]),
        pack([m_ffn1_norm, m_mix_norm, m_sb_out_norm, m_dil_out_norm, m_ffn2_norm, m_final_norm]),
        pack([v_ffn1_norm, v_mix_norm, v_sb_out_norm, v_dil_out_norm, v_ffn2_norm, v_final_norm]))
    like = {"ffn1_norm": ffn1_norm, "mix_norm": mix_norm, "sb_out_norm": sb_out_norm, "dil_out_norm": dil_out_norm,
            "ffn2_norm": ffn2_norm, "final_norm": final_norm}
    vecs = {n: tuple(t[i, :like[n].size].reshape(like[n].shape) for t in (g_vec, d_vec, m_vec, v_vec))
            for i, n in enumerate(vec_names)}
    loss = 0.5 * jnp.sum(g_vec[6]) / D_MODEL

    order = ["ffn1_norm", "ffn1_w_gate", "ffn1_w_up", "ffn1_w_down", "mix_norm", "w_in", "sb_out_norm", "dil_out_norm",
             "w_out", "ffn2_norm", "ffn2_w_gate", "ffn2_w_up", "ffn2_w_down", "final_norm"]
    both = {**mats, **vecs}
    return (loss, grad_x[None], *[both[n][0] for n in order], *[both[n][1] for n in order],
            *[both[n][2] for n in order], *[both[n][3] for n in order])
```

```python
import functools

import jax
import jax.numpy as jnp
from jax import lax
from jax.experimental import pallas as pl
from jax.experimental.pallas import tpu as pltpu

D_MODEL = 1024
D_FF = 2816
HEAD_DIM = 64
D_SB = 512
D_DIL = 512
D_IN = 3072
N_CHIP = 4
FFB = D_FF // N_CHIP
INB = D_IN // N_CHIP
OUTB = D_MODEL // N_CHIP
BLK = 128
LANES = 128
DILATIONS = (1, 4, 16)
ROPE_THETA = 10000.0
RMS_EPS = 1e-6
SCALE = HEAD_DIM ** -0.5
NEG = -1e30
DEAD = -104.0
ADAM_LR = 0.001
ADAM_B1 = 0.9
ADAM_B2 = 0.999
ADAM_EPS = 1e-08
ADAM_WD = 0.01
ADAM_STEP = 10
MESH = pl.DeviceIdType.MESH
F32 = jnp.float32
BF16 = jnp.bfloat16
TM = 512


def _params(vmem_mb):
    return pltpu.CompilerParams(vmem_limit_bytes=vmem_mb << 20)


def _dot(a, b):
    return jnp.dot(a, b, preferred_element_type=F32)


def _dot_nt(a, b):
    return lax.dot_general(a, b, (((1,), (1,)), ((), ())), preferred_element_type=F32)


def _dot_tn(a, b):
    return lax.dot_general(a, b, (((0,), (0,)), ((), ())), preferred_element_type=F32)


def _rms_fwd(x, g):
    r = lax.rsqrt(jnp.mean(x * x, axis=-1, keepdims=True) + RMS_EPS)
    xh = x * r
    return xh * g, xh, r


def _rms_bwd(dy, xh, r, g):
    dyg = dy * g
    dx = r * (dyg - xh * jnp.mean(dyg * xh, axis=-1, keepdims=True))
    return dx, jnp.sum(dy * xh, axis=0, keepdims=True)


def _split_bf16(a):
    hi = a.astype(BF16)
    return hi, (a - hi.astype(F32)).astype(BF16)


def _dot_split(a, b2):
    hi, lo = _split_bf16(a)
    return _dot(jnp.concatenate([hi, lo], axis=1), b2)


def _ffn_weight_specs():
    return [pl.BlockSpec((None, D_MODEL, FFB), lambda i, j: (j, 0, 0)),
            pl.BlockSpec((None, D_MODEL, FFB), lambda i, j: (j, 0, 0)),
            pl.BlockSpec((None, FFB, D_MODEL), lambda i, j: (j, 0, 0))]


def _ffn_saved(s):
    hidden = jax.ShapeDtypeStruct((N_CHIP, s, FFB), BF16)
    hid = pl.BlockSpec((None, TM, FFB), lambda i, j: (j, i, 0))
    row = pl.BlockSpec((TM, D_MODEL), lambda i, j: (i, 0))
    return [row, hid, hid, hid], [jax.ShapeDtypeStruct((s, D_MODEL), BF16), hidden, hidden, hidden]


def _ffn_accumulate(h_ref, acc_scr, wg_ref, wu_ref, wd_ref, a_ref, b_ref, act_ref):
    h = h_ref[...]
    a = _dot(h, wg_ref[...])
    b = _dot(h, wu_ref[...])
    act = ((a * jax.nn.sigmoid(a)) * b).astype(BF16)
    a_ref[...] = a.astype(BF16)
    b_ref[...] = b.astype(BF16)
    act_ref[...] = act
    acc_scr[...] += _dot(act, wd_ref[...])


def _host_gather_before(gather, i, j, steps):
    @pl.when((i == 0) & (j == 0))
    def _():
        gather.start()

    @pl.when((i == (3 * steps) // 4) & (j == 0))
    def _():
        gather.forward()


def _host_gather_after(gather, i, j, steps):
    @pl.when((i == steps - 1) & (j == N_CHIP - 1))
    def _():
        gather.finish()


def _ffn1_fwd(x, g1, gmix, gu, wd, later_shards):
    s = x.shape[0]
    row = pl.BlockSpec((TM, D_MODEL), lambda i, j: (i, 0))
    vec = pl.BlockSpec((1, D_MODEL), lambda i, j: (0, 0))
    saved_specs, saved_shapes = _ffn_saved(s)
    n = len(later_shards)
    any_spec = pl.BlockSpec(memory_space=pl.ANY)

    def body(*refs):
        x_ref, g_ref, gm_ref, wg_ref, wu_ref, wd_ref = refs[:6]
        shard_refs, refs = refs[6:6 + n], refs[6 + n:]
        x1_ref, hm_ref, h_ref, a_ref, b_ref, act_ref = refs[:6]
        gathered_refs, acc_scr, gather_scratch = refs[6:6 + n], refs[6 + n], refs[7 + n:]
        gather = _BackgroundGather(shard_refs, gathered_refs, gather_scratch)
        i, j = pl.program_id(0), pl.program_id(1)
        _host_gather_before(gather, i, j, s // TM)

        @pl.when(j == 0)
        def _():
            h, _, _ = _rms_fwd(x_ref[...], g_ref[...])
            h_ref[...] = h.astype(BF16)
            acc_scr[...] = jnp.zeros_like(acc_scr)

        _ffn_accumulate(h_ref, acc_scr, wg_ref, wu_ref, wd_ref, a_ref, b_ref, act_ref)

        @pl.when(j == N_CHIP - 1)
        def _():
            x1 = x_ref[...] + 0.5 * acc_scr[...]
            x1_ref[...] = x1
            hm, _, _ = _rms_fwd(x1, gm_ref[...])
            hm_ref[...] = hm.astype(BF16)

        _host_gather_after(gather, i, j, s // TM)

    x1, hm, h, a, b, act, *gathered = pl.pallas_call(
        body, name="ffn1_fwd", grid=(s // TM, N_CHIP),
        in_specs=[row, vec, vec] + _ffn_weight_specs() + [any_spec] * n,
        out_specs=[row, row] + saved_specs + [any_spec] * n,
        out_shape=([jax.ShapeDtypeStruct((s, D_MODEL), F32), jax.ShapeDtypeStruct((s, D_MODEL), BF16)] + saved_shapes
                   + _BackgroundGather.out_shapes(later_shards)),
        scratch_shapes=[pltpu.VMEM((TM, D_MODEL), F32)] + _BackgroundGather.scratch_shapes(later_shards),
        compiler_params=_params(58),
    )(x, g1, gmix, gu[0], gu[1], wd, *later_shards)
    return x1, hm, [h, a, b, act], gathered


def _ffn2_fwd_loss(x2, g2, gf, target, gu, wd):
    s = x2.shape[0]
    row = pl.BlockSpec((TM, D_MODEL), lambda i, j: (i, 0))
    vec = pl.BlockSpec((1, D_MODEL), lambda i, j: (0, 0))
    stat = pl.BlockSpec((8, D_MODEL), lambda i, j: (0, 0))
    saved_specs, saved_shapes = _ffn_saved(s)

    def body(x_ref, g_ref, gf_ref, t_ref, wg_ref, wu_ref, wd_ref, dx_ref, st_ref, h_ref, a_ref, b_ref, act_ref, acc_scr):
        i, j = pl.program_id(0), pl.program_id(1)

        @pl.when((i == 0) & (j == 0))
        def _():
            st_ref[...] = jnp.zeros_like(st_ref)

        @pl.when(j == 0)
        def _():
            h, _, _ = _rms_fwd(x_ref[...], g_ref[...])
            h_ref[...] = h.astype(BF16)
            acc_scr[...] = jnp.zeros_like(acc_scr)

        _ffn_accumulate(h_ref, acc_scr, wg_ref, wu_ref, wd_ref, a_ref, b_ref, act_ref)

        @pl.when(j == N_CHIP - 1)
        def _():
            x3 = x_ref[...] + 0.5 * acc_scr[...]
            y, xh, r = _rms_fwd(x3, gf_ref[...])
            err = y - t_ref[...]
            dx, dg = _rms_bwd(err * (1.0 / D_MODEL), xh, r, gf_ref[...])
            dx_ref[...] = dx
            st_ref[0:1, :] += dg
            st_ref[1:2, :] += jnp.sum(err * err, axis=0, keepdims=True)

    dx3, st, *saved = pl.pallas_call(
        body, name="ffn2_fwd_loss", grid=(s // TM, N_CHIP),
        in_specs=[row, vec, vec, row] + _ffn_weight_specs(),
        out_specs=[row, stat] + saved_specs,
        out_shape=[jax.ShapeDtypeStruct((s, D_MODEL), F32), jax.ShapeDtypeStruct((8, D_MODEL), F32)] + saved_shapes,
        scratch_shapes=[pltpu.VMEM((TM, D_MODEL), F32)],
        compiler_params=_params(56),
    )(x2, g2, gf, target, gu[0], gu[1], wd)
    return dx3, st, saved


def _ffn_bwd(xin, g, dy, saved, gu, wd, f):
    s = xin.shape[0]
    hb, gate, up, act = saved
    row = pl.BlockSpec((TM, D_MODEL), lambda i, j: (i, 0))
    vec = pl.BlockSpec((1, D_MODEL), lambda i, j: (0, 0))
    stat = pl.BlockSpec((8, D_MODEL), lambda i, j: (0, 0))
    hid = pl.BlockSpec((None, TM, FFB), lambda i, j: (j, i, 0))

    def body(x_ref, g_ref, dy_ref, a_ref, b_ref, wg_ref, wu_ref, wd_ref, out_ref, dyh_ref, da_ref, db_ref, st_ref, dh_scr):
        i, j = pl.program_id(0), pl.program_id(1)

        @pl.when((i == 0) & (j == 0))
        def _():
            st_ref[...] = jnp.zeros_like(st_ref)

        @pl.when(j == 0)
        def _():
            dyh_ref[...] = (0.5 * dy_ref[...]).astype(BF16)
            dh_scr[...] = jnp.zeros_like(dh_scr)

        a = a_ref[...].astype(F32)
        b = b_ref[...].astype(F32)
        sg = jax.nn.sigmoid(a)
        dact = _dot_nt(dyh_ref[...], wd_ref[...])
        dab = (dact * b * (sg * (1.0 + a * (1.0 - sg)))).astype(BF16)
        dbb = (dact * (a * sg)).astype(BF16)
        da_ref[...] = dab
        db_ref[...] = dbb
        dh_scr[...] += _dot_nt(dab, wg_ref[...]) + _dot_nt(dbb, wu_ref[...])

        @pl.when(j == N_CHIP - 1)
        def _():
            _, xh, r = _rms_fwd(x_ref[...], g_ref[...])
            dx, dg = _rms_bwd(dh_scr[...], xh, r, g_ref[...])
            out_ref[...] = dy_ref[...] + dx
            st_ref[0:1, :] += dg

    hidden = jax.ShapeDtypeStruct((N_CHIP, s, FFB), BF16)
    dx, dyh, da, db, st = pl.pallas_call(
        body, name=f"ffn{f + 1}_bwd_dx", grid=(s // TM, N_CHIP),
        in_specs=[row, vec, row, hid, hid] + _ffn_weight_specs(),
        out_specs=[row, row, hid, hid, stat],
        out_shape=[jax.ShapeDtypeStruct((s, D_MODEL), F32), jax.ShapeDtypeStruct((s, D_MODEL), BF16),
                   hidden, hidden, jax.ShapeDtypeStruct((8, D_MODEL), F32)],
        scratch_shapes=[pltpu.VMEM((TM, D_MODEL), F32)],
        compiler_params=_params(56),
    )(xin, g, dy, gate, up, gu[0], gu[1], wd)

    tok = pl.BlockSpec((TM, D_MODEL), lambda j, i: (i, 0))
    hid2 = pl.BlockSpec((None, TM, FFB), lambda j, i: (j, i, 0))
    gspecs = [pl.BlockSpec((None, D_MODEL, FFB), lambda j, i: (j, 0, 0)),
              pl.BlockSpec((None, D_MODEL, FFB), lambda j, i: (j, 0, 0)),
              pl.BlockSpec((None, FFB, D_MODEL), lambda j, i: (j, 0, 0))]

    def wbody(h_ref, dyh_ref, da_ref, db_ref, act_ref, dwg_ref, dwu_ref, dwd_ref):
        @pl.when(pl.program_id(1) == 0)
        def _():
            dwg_ref[...] = jnp.zeros_like(dwg_ref)
            dwu_ref[...] = jnp.zeros_like(dwu_ref)
            dwd_ref[...] = jnp.zeros_like(dwd_ref)

        hb = h_ref[...]
        dwg_ref[...] += _dot_tn(hb, da_ref[...])
        dwu_ref[...] += _dot_tn(hb, db_ref[...])
        dwd_ref[...] += _dot_tn(act_ref[...], dyh_ref[...])

    dwg, dwu, dwd = pl.pallas_call(
        wbody, name=f"ffn{f + 1}_bwd_dw", grid=(N_CHIP, s // TM),
        in_specs=[tok, tok, hid2, hid2, hid2], out_specs=gspecs,
        out_shape=[jax.ShapeDtypeStruct((N_CHIP, D_MODEL, FFB), F32),
                   jax.ShapeDtypeStruct((N_CHIP, D_MODEL, FFB), F32),
                   jax.ShapeDtypeStruct((N_CHIP, FFB, D_MODEL), F32)],
        compiler_params=_params(48),
    )(hb, dyh, da, db, act)
    return dx, dwg, dwu, dwd, st


def _rope_tables(s):
    half = HEAD_DIM // 2
    inv_freq = ROPE_THETA ** (-jnp.arange(half, dtype=F32) / half)
    ang = jnp.arange(s).astype(F32)[:, None] * inv_freq[None, :]
    cos, sin = jnp.cos(ang), jnp.sin(ang)
    cos2 = jnp.concatenate([cos, cos], axis=-1)
    sin2 = jnp.concatenate([-sin, sin], axis=-1)
    return jnp.tile(cos2, (1, LANES // HEAD_DIM)), jnp.tile(sin2, (1, LANES // HEAD_DIM))


def _rotate(t, cos, sin_signed):
    lane = lax.broadcasted_iota(jnp.int32, t.shape, 1)
    first = (lane % HEAD_DIM) < (HEAD_DIM // 2)
    partner = jnp.where(first, pltpu.roll(t, LANES - HEAD_DIM // 2, 1), pltpu.roll(t, HEAD_DIM // 2, 1))
    return t * cos + partner * sin_signed


def _proj_fwd(hm, win, cos, sin, later_shards):
    s = hm.shape[0]
    n_sub = INB // LANES
    first_rot, last_rot = (3 * D_SB) // LANES, (3 * D_SB + 2 * D_DIL) // LANES
    n = len(later_shards)
    any_spec = pl.BlockSpec(memory_space=pl.ANY)

    def body(*refs):
        h_ref, w_ref, c_ref, s_ref = refs[:4]
        shard_refs, o_ref, gathered_refs, gather_scratch = refs[4:4 + n], refs[4 + n], refs[5 + n:5 + 2 * n], refs[5 + 2 * n:]
        gather = _BackgroundGather(shard_refs, gathered_refs, gather_scratch)
        i, j = pl.program_id(0), pl.program_id(1)
        _host_gather_before(gather, i, j, s // TM)
        r = _dot(h_ref[...], w_ref[...])
        for c in range(n_sub):
            t = r[:, c * LANES:(c + 1) * LANES]
            col = j * n_sub + c
            rot = (col >= first_rot) & (col < last_rot)
            lanes = slice(c * LANES, (c + 1) * LANES)

            @pl.when(rot)
            def _():
                o_ref[:, lanes] = _rotate(t, c_ref[...], s_ref[...]).astype(BF16)

            @pl.when(jnp.logical_not(rot))
            def _():
                o_ref[:, lanes] = t.astype(BF16)

        _host_gather_after(gather, i, j, s // TM)

    qkv, *gathered = pl.pallas_call(
        body, name="proj_fwd", grid=(s // TM, N_CHIP),
        in_specs=[pl.BlockSpec((TM, D_MODEL), lambda i, j: (i, 0)),
                  pl.BlockSpec((None, D_MODEL, INB), lambda i, j: (j, 0, 0)),
                  pl.BlockSpec((TM, LANES), lambda i, j: (i, 0)),
                  pl.BlockSpec((TM, LANES), lambda i, j: (i, 0))] + [any_spec] * n,
        out_specs=[pl.BlockSpec((TM, INB), lambda i, j: (i, j))] + [any_spec] * n,
        out_shape=[jax.ShapeDtypeStruct((s, D_IN), BF16)] + _BackgroundGather.out_shapes(later_shards),
        scratch_shapes=_BackgroundGather.scratch_shapes(later_shards),
        compiler_params=_params(48),
    )(hm, win, cos, sin, *later_shards)
    return qkv, gathered


def _proj_bwd(x1, gmix, dqkv, win, dx2):
    s = x1.shape[0]
    row = pl.BlockSpec((TM, D_MODEL), lambda i, j: (i, 0))
    vec = pl.BlockSpec((1, D_MODEL), lambda i, j: (0, 0))

    def body(x_ref, g_ref, dq_ref, w_ref, dx2_ref, out_ref, dw_ref, st_ref, h_scr, dh_scr):
        i, j = pl.program_id(0), pl.program_id(1)

        @pl.when((i == 0) & (j == 0))
        def _():
            st_ref[...] = jnp.zeros_like(st_ref)
            dw_ref[...] = jnp.zeros_like(dw_ref)

        @pl.when(j == 0)
        def _():
            h, _, _ = _rms_fwd(x_ref[...], g_ref[...])
            h_scr[...] = h.astype(BF16)
            dh_scr[...] = jnp.zeros_like(dh_scr)

        dq = dq_ref[...]
        dw_ref[j] += _dot_tn(h_scr[...], dq)
        dh_scr[...] += _dot_nt(dq, w_ref[...])

        @pl.when(j == N_CHIP - 1)
        def _():
            _, xh, r = _rms_fwd(x_ref[...], g_ref[...])
            dx, dg = _rms_bwd(dh_scr[...], xh, r, g_ref[...])
            out_ref[...] = dx2_ref[...] + dx
            st_ref[0:1, :] += dg

    return pl.pallas_call(
        body, name="proj_bwd", grid=(s // TM, N_CHIP),
        in_specs=[row, vec, pl.BlockSpec((TM, INB), lambda i, j: (i, j)),
                  pl.BlockSpec((None, D_MODEL, INB), lambda i, j: (j, 0, 0)), row],
        out_specs=[row, pl.BlockSpec((N_CHIP, D_MODEL, INB), lambda i, j: (0, 0, 0)),
                   pl.BlockSpec((8, D_MODEL), lambda i, j: (0, 0))],
        out_shape=[jax.ShapeDtypeStruct((s, D_MODEL), F32),
                   jax.ShapeDtypeStruct((N_CHIP, D_MODEL, INB), F32),
                   jax.ShapeDtypeStruct((8, D_MODEL), F32)],
        scratch_shapes=[pltpu.VMEM((TM, D_MODEL), BF16), pltpu.VMEM((TM, D_MODEL), F32)],
        compiler_params=_params(56),
    )(x1, gmix, dqkv, win, dx2)


def _outproj_fwd(o_sb, o_dl, g_sb, g_dl, x1, wout):
    s = x1.shape[0]
    half = pl.BlockSpec((TM, D_SB), lambda i: (i, 0))
    row = pl.BlockSpec((TM, D_MODEL), lambda i: (i, 0))
    vec = pl.BlockSpec((1, D_SB), lambda i: (0, 0))

    def body(a_ref, b_ref, ga_ref, gb_ref, x_ref, w_ref, o_ref):
        ma, _, _ = _rms_fwd(a_ref[...], ga_ref[...])
        mb, _, _ = _rms_fwd(b_ref[...], gb_ref[...])
        o_ref[...] = (x_ref[...] + _dot(ma.astype(BF16), w_ref[0:D_SB, :])
                      + _dot(mb.astype(BF16), w_ref[D_SB:D_MODEL, :]))

    return pl.pallas_call(
        body, name="outproj_fwd", grid=(s // TM,),
        in_specs=[half, half, vec, vec, row, pl.BlockSpec((D_MODEL, D_MODEL), lambda i: (0, 0))],
        out_specs=row, out_shape=jax.ShapeDtypeStruct((s, D_MODEL), F32),
        compiler_params=_params(32),
    )(o_sb, o_dl, g_sb, g_dl, x1, wout)


def _outproj_bwd(dx2, o_sb, o_dl, g_sb, g_dl, wout):
    s = dx2.shape[0]
    half = pl.BlockSpec((TM, D_SB), lambda i: (i, 0))
    row = pl.BlockSpec((TM, D_MODEL), lambda i: (i, 0))
    vec = pl.BlockSpec((1, D_SB), lambda i: (0, 0))
    full = pl.BlockSpec((D_MODEL, D_MODEL), lambda i: (0, 0))

    def body(dy_ref, a_ref, b_ref, ga_ref, gb_ref, w_ref, da_ref, db_ref, dl_ref, dw_ref, st_ref):
        @pl.when(pl.program_id(0) == 0)
        def _():
            dw_ref[...] = jnp.zeros_like(dw_ref)
            st_ref[...] = jnp.zeros_like(st_ref)

        dy = dy_ref[...].astype(BF16)
        dm = _dot_nt(dy, w_ref[...])
        ma, xa, ra = _rms_fwd(a_ref[...], ga_ref[...])
        mb, xb, rb = _rms_fwd(b_ref[...], gb_ref[...])
        dw_ref[0:D_SB, :] += _dot_tn(ma.astype(BF16), dy)
        dw_ref[D_SB:D_MODEL, :] += _dot_tn(mb.astype(BF16), dy)
        da, dga = _rms_bwd(dm[:, 0:D_SB], xa, ra, ga_ref[...])
        db, dgb = _rms_bwd(dm[:, D_SB:D_MODEL], xb, rb, gb_ref[...])
        da_ref[...] = da
        db_ref[...] = db
        r = lax.broadcasted_iota(jnp.int32, (LANES, LANES), 0) >= HEAD_DIM
        c = lax.broadcasted_iota(jnp.int32, (LANES, LANES), 1) >= HEAD_DIM
        same_head = jnp.where(r == c, 1.0, 0.0).astype(BF16)
        same_head = jnp.concatenate([same_head, same_head], axis=0)
        prod = db * b_ref[...]
        for k in range(D_DIL // LANES):
            lanes = slice(k * LANES, (k + 1) * LANES)
            dl_ref[:, lanes] = _dot_split(prod[:, lanes], same_head)
        st_ref[0:1, :] += dga
        st_ref[1:2, :] += dgb

    return pl.pallas_call(
        body, name="outproj_bwd", grid=(s // TM,),
        in_specs=[row, half, half, vec, vec, full],
        out_specs=[half, half, half, full, pl.BlockSpec((8, D_SB), lambda i: (0, 0))],
        out_shape=[jax.ShapeDtypeStruct((s, D_SB), F32), jax.ShapeDtypeStruct((s, D_SB), F32),
                   jax.ShapeDtypeStruct((s, D_DIL), F32),
                   jax.ShapeDtypeStruct((D_MODEL, D_MODEL), F32), jax.ShapeDtypeStruct((8, D_SB), F32)],
        compiler_params=_params(48),
    )(dx2, o_sb, o_dl, g_sb, g_dl, wout)


def _head_masks():
    lane = lax.broadcasted_iota(jnp.int32, (BLK, LANES), 1)
    return [lane < HEAD_DIM, lane >= HEAD_DIM]


def _keep(mask, a):
    return a * jnp.where(mask, 1.0, 0.0).astype(a.dtype)


def _suffix_matrices():
    r = lax.broadcasted_iota(jnp.int32, (2 * BLK, BLK), 0) & (BLK - 1)
    c = lax.broadcasted_iota(jnp.int32, (2 * BLK, BLK), 1)
    ones = jnp.ones((2 * BLK, BLK), BF16)
    excl = jnp.concatenate([(r > c).astype(BF16), ones], axis=1)
    incl = jnp.concatenate([(r >= c).astype(BF16), ones], axis=1)
    return excl, incl


def _blk(i):
    return pl.ds(pl.multiple_of(i * BLK, BLK), BLK)


def _alive(carry_m):
    return (jnp.max(carry_m) > DEAD).astype(jnp.int32)


def _more_keys(last, carry):
    return (carry[0] * SB_KB <= last) & (carry[1] > 0)


def _stack_heads(a):
    masks = _head_masks()
    return jnp.concatenate([_keep(masks[0], a), _keep(masks[1], a)], axis=0)


def _unstack_heads(a2):
    return jnp.where(_head_masks()[0], a2[:BLK], a2[BLK:])


def _head_rowsum(a):
    masks = _head_masks()
    return jnp.concatenate([jnp.sum(jnp.where(m, a, 0.0), axis=1, keepdims=True) for m in masks], axis=0)


SB_QB = 2
SB_ROWS = SB_QB * 2 * BLK
SB_KB = 4
PAST_START = 1 << 30


def _sb_rows(ref, i0, cast=None):
    tiles = [ref[_blk(i0 + t), :] for t in range(SB_QB)]
    return jnp.concatenate([_stack_heads(t if cast is None else t.astype(cast)) for t in tiles], axis=0)


def _sb_scores(q2, k, i, j, carry_m, u_excl):
    r = lax.broadcasted_iota(jnp.int32, (SB_ROWS, BLK), 0)
    row = (r & (BLK - 1)) + ((r >> 8) << 7)
    col = lax.broadcasted_iota(jnp.int32, (SB_ROWS, BLK), 1)
    valid = (jnp.where(j >= 0, j * BLK, PAST_START) + col) < (i * BLK + row)
    z = _dot_nt(q2, k) * SCALE
    sp = jnp.maximum(z, 0.0) + jnp.log(1.0 + jnp.exp(-jnp.abs(z)))
    log_stay = jnp.where(valid, -sp, 0.0)
    log_beta = z - sp
    sums = _dot_split(log_stay, u_excl)
    later = carry_m + sums[:, :BLK]
    w = jnp.where(valid, jnp.exp(log_beta + later), 0.0)
    return valid, log_beta, w, carry_m + sums[:, BLK:]


def _sb_fwd(qkv):
    s = qkv.shape[0]
    nq = s // BLK
    pairs = D_SB // LANES
    col = lambda off: pl.BlockSpec((s, LANES), lambda p: (0, off + p))

    def body(q_ref, k_ref, v_ref, o_ref):
        u_excl, _ = _suffix_matrices()
        zero = jnp.zeros((SB_ROWS, LANES), F32)

        def q_block(ib, _):
            i = ib * SB_QB
            last = i + SB_QB - 1
            q2 = _sb_rows(q_ref, i)

            def k_block(carry):
                jj, _, carry_m, acc = carry
                for t in range(SB_KB):
                    j = last - jj * SB_KB - t
                    at = _blk(jnp.maximum(j, 0))
                    _, _, w, carry_m = _sb_scores(q2, k_ref[at, :], i, j, carry_m, u_excl)
                    acc = acc + _dot(w.astype(BF16), v_ref[at, :])
                return jj + 1, _alive(carry_m), carry_m, acc

            _, _, _, acc = lax.while_loop(functools.partial(_more_keys, last), k_block,
                                          (jnp.int32(0), jnp.int32(1), zero, zero))
            for t in range(SB_QB):
                o_ref[_blk(i + t), :] = _unstack_heads(acc[2 * BLK * t:2 * BLK * (t + 1)])
            return 0

        lax.fori_loop(0, nq // SB_QB, q_block, 0)

    return pl.pallas_call(
        body, name="sb_fwd", grid=(pairs,),
        in_specs=[col(0), col(pairs), col(2 * pairs)],
        out_specs=pl.BlockSpec((s, LANES), lambda p: (0, p)),
        out_shape=jax.ShapeDtypeStruct((s, D_SB), F32),
        compiler_params=_params(48),
    )(qkv, qkv, qkv)


def _sb_bwd(qkv, o_sb, do_sb):
    s = qkv.shape[0]
    nq = s // BLK
    pairs = D_SB // LANES
    col = lambda off: pl.BlockSpec((s, LANES), lambda p: (0, off + p))
    own = pl.BlockSpec((s, LANES), lambda p: (0, p))

    def body(q_ref, k_ref, v_ref, o_ref, do_ref, dq_ref, dk_ref, dv_ref, dk_acc, dv_acc):
        u_excl, u_incl = _suffix_matrices()
        zero = jnp.zeros((SB_ROWS, LANES), F32)
        dk_acc[...] = jnp.zeros_like(dk_acc)
        dv_acc[...] = jnp.zeros_like(dv_acc)

        def q_block(ib, _):
            i = ib * SB_QB
            last = i + SB_QB - 1
            q2 = _sb_rows(q_ref, i)
            do2 = _sb_rows(do_ref, i, BF16)
            totals = [_head_rowsum(do_ref[_blk(i + t), :].astype(BF16).astype(F32) * o_ref[_blk(i + t), :])
                      for t in range(SB_QB)]
            total = jnp.broadcast_to(jnp.concatenate(totals, axis=0), (SB_ROWS, BLK))

            def k_block(carry):
                jj, _, carry_m, carry_g, dq = carry
                for t in range(SB_KB):
                    j = last - jj * SB_KB - t
                    at = _blk(jnp.maximum(j, 0))
                    k = k_ref[at, :]
                    valid, log_beta, w, carry_m = _sb_scores(q2, k, i, j, carry_m, u_excl)
                    wb = w.astype(BF16)
                    g = wb.astype(F32) * _dot_nt(do2, v_ref[at, :])
                    sums = _dot_split(g, u_incl)
                    before = total - (carry_g + sums[:, :BLK])
                    dz = jnp.where(valid, g - jnp.exp(log_beta) * (g + before), 0.0)
                    dzb = (dz * SCALE).astype(BF16)
                    dk_acc[at, :] += _dot_tn(dzb, q2)
                    dv_acc[at, :] += _dot_tn(wb, do2)
                    carry_g = carry_g + sums[:, BLK:]
                    dq = dq + _dot(dzb, k)
                return jj + 1, _alive(carry_m), carry_m, carry_g, dq

            _, _, _, _, dq = lax.while_loop(functools.partial(_more_keys, last), k_block,
                                            (jnp.int32(0), jnp.int32(1), zero, zero, zero))
            for t in range(SB_QB):
                dq_ref[_blk(i + t), :] = _unstack_heads(dq[2 * BLK * t:2 * BLK * (t + 1)]).astype(BF16)
            return 0

        lax.fori_loop(0, nq // SB_QB, q_block, 0)
        dk_ref[...] = dk_acc[...].astype(BF16)
        dv_ref[...] = dv_acc[...].astype(BF16)

    return pl.pallas_call(
        body, name="sb_bwd", grid=(pairs,),
        in_specs=[col(0), col(pairs), col(2 * pairs), own, own],
        out_specs=[own, own, own],
        out_shape=[jax.ShapeDtypeStruct((s, D_SB), BF16)] * 3,
        scratch_shapes=[pltpu.VMEM((s, LANES), F32), pltpu.VMEM((s, LANES), F32)],
        compiler_params=_params(56),
    )(qkv, qkv, qkv, o_sb, do_sb)


DIL_UNROLL = 4


def _band_masks(b):
    row = lax.broadcasted_iota(jnp.int32, (2 * BLK, BLK), 0) & (BLK - 1)
    col = lax.broadcasted_iota(jnp.int32, (2 * BLK, BLK), 1)
    return col <= row, (col - row) >= jnp.where(b > 0, 0, BLK)


def _dil_tiles(qf, kf, vf, d, t, nb):
    c, b = t // nb, t % nb
    start = c + d * BLK * b
    rows = pl.ds(start, BLK, stride=d)
    prev = pl.ds(jnp.where(b > 0, start - d * BLK, start), BLK, stride=d)
    bf = lambda ref, sl: ref[sl, :].astype(BF16)
    return b, rows, prev, _stack_heads(bf(qf, rows)), bf(kf, rows), bf(kf, prev), bf(vf, rows), bf(vf, prev)


def _lanes_of_heads(col2):
    return _unstack_heads(jnp.broadcast_to(col2, (2 * BLK, LANES)))


def _dilated_fwd(qkv):
    s = qkv.shape[0]
    pairs = D_DIL // LANES
    base = (3 * D_SB) // LANES
    col = lambda off: pl.BlockSpec((s, LANES), lambda p: (0, off + p))
    own = pl.BlockSpec((s, LANES), lambda p: (0, p))

    def body(q_ref, k_ref, v_ref, acc_ref, m_ref, qf, kf, vf, l_scr):
        qf[...] = q_ref[...].astype(F32)
        kf[...] = k_ref[...].astype(F32)
        vf[...] = v_ref[...].astype(F32)
        for d in DILATIONS:
            nb = s // (d * BLK)

            def block(t, _):
                b, rows, prev, q2, kc, kp, vc, vp = _dil_tiles(qf, kf, vf, d, t, nb)
                in_cur, in_prev = _band_masks(b)
                zc = jnp.where(in_cur, _dot_nt(q2, kc) * SCALE, NEG)
                zp = jnp.where(in_prev, _dot_nt(q2, kp) * SCALE, NEG)
                m = jnp.maximum(jnp.max(zc, axis=1, keepdims=True), jnp.max(zp, axis=1, keepdims=True))
                pc, pp = jnp.exp(zc - m), jnp.exp(zp - m)
                den = jnp.sum(pc, axis=1, keepdims=True) + jnp.sum(pp, axis=1, keepdims=True)
                acc = _unstack_heads(_dot(pc.astype(BF16), vc) + _dot(pp.astype(BF16), vp))
                m_t, l_t = _lanes_of_heads(m), _lanes_of_heads(den)
                if d == DILATIONS[0]:
                    m_ref[rows, :] = m_t
                    l_scr[rows, :] = l_t
                    acc_ref[rows, :] = acc
                else:
                    m_old = m_ref[rows, :]
                    m_new = jnp.maximum(m_old, m_t)
                    keep, add = jnp.exp(m_old - m_new), jnp.exp(m_t - m_new)
                    m_ref[rows, :] = m_new
                    l_scr[rows, :] = l_scr[rows, :] * keep + l_t * add
                    acc_ref[rows, :] = acc_ref[rows, :] * keep + acc * add
                return 0

            lax.fori_loop(0, s // BLK, block, 0, unroll=DIL_UNROLL)

        def finish(i, _):
            l = l_scr[_blk(i), :]
            acc_ref[_blk(i), :] = acc_ref[_blk(i), :] / l
            m_ref[_blk(i), :] = m_ref[_blk(i), :] + jnp.log(l)
            return 0

        lax.fori_loop(0, s // BLK, finish, 0)

    return pl.pallas_call(
        body, name="dilated_fwd", grid=(pairs,),
        in_specs=[col(base), col(base + pairs), col(base + 2 * pairs)],
        out_specs=[own, own],
        out_shape=[jax.ShapeDtypeStruct((s, D_DIL), F32)] * 2,
        scratch_shapes=[pltpu.VMEM((s, LANES), F32)] * 4,
        compiler_params=_params(56),
    )(qkv, qkv, qkv)


def _stack_lanes(t):
    other = pltpu.roll(t, HEAD_DIM, 1)
    first = _head_masks()[0]
    return jnp.concatenate([jnp.where(first, t, other), jnp.where(first, other, t)], axis=0)


def _dilated_bwd(qkv, delta, lse, dout):
    s = qkv.shape[0]
    pairs = D_DIL // LANES
    base = (3 * D_SB) // LANES
    once = pl.Buffered(1)
    col = lambda off: pl.BlockSpec((s, LANES), lambda p: (0, off + p), pipeline_mode=once)
    own = pl.BlockSpec((s, LANES), lambda p: (0, p), pipeline_mode=once)
    res = pl.BlockSpec((s, LANES), lambda p: (0, p))

    def body(q_ref, k_ref, v_ref, dl_ref, l_ref, do_ref, dq_ref, dk_ref, dv_ref, qf, kf, vf):
        qf[...] = q_ref[...].astype(F32)
        kf[...] = k_ref[...].astype(F32)
        vf[...] = v_ref[...].astype(F32)
        dq_ref[...] = jnp.zeros_like(dq_ref)
        dk_ref[...] = jnp.zeros_like(dk_ref)
        dv_ref[...] = jnp.zeros_like(dv_ref)
        for d in DILATIONS:
            nb = s // (d * BLK)

            def block(t, _):
                b, rows, prev, q2, kc, kp, vc, vp = _dil_tiles(qf, kf, vf, d, t, nb)
                in_cur, in_prev = _band_masks(b)
                do2 = _stack_heads(do_ref[rows, :].astype(BF16))
                delta = _stack_lanes(dl_ref[rows, :])
                lse2 = _stack_lanes(l_ref[rows, :])
                wc = jnp.exp(jnp.where(in_cur, _dot_nt(q2, kc) * SCALE, NEG) - lse2)
                wp = jnp.exp(jnp.where(in_prev, _dot_nt(q2, kp) * SCALE, NEG) - lse2)
                dzc = (wc * (_dot_nt(do2, vc) - delta) * SCALE).astype(BF16)
                dzp = (wp * (_dot_nt(do2, vp) - delta) * SCALE).astype(BF16)
                dq_ref[rows, :] += _unstack_heads(_dot(dzc, kc) + _dot(dzp, kp))
                dk_ref[rows, :] += _dot_tn(dzc, q2)
                dk_ref[prev, :] += _dot_tn(dzp, q2)
                dv_ref[rows, :] += _dot_tn(wc.astype(BF16), do2)
                dv_ref[prev, :] += _dot_tn(wp.astype(BF16), do2)
                return 0

            lax.fori_loop(0, s // BLK, block, 0, unroll=DIL_UNROLL)

    return pl.pallas_call(
        body, name="dilated_bwd", grid=(pairs,),
        in_specs=[col(base), col(base + pairs), col(base + 2 * pairs), own, own, own],
        out_specs=[res, res, res],
        out_shape=[jax.ShapeDtypeStruct((s, D_DIL), F32)] * 3,
        scratch_shapes=[pltpu.VMEM((s, LANES), F32)] * 3,
        compiler_params=_params(60),
    )(qkv, qkv, qkv, delta, lse, dout)


def _dilated_finish(grads, cos, sin):
    s = grads[0].shape[0]
    spec = pl.BlockSpec((TM, D_DIL), lambda i: (i, 0))
    tab = pl.BlockSpec((TM, LANES), lambda i: (i, 0))

    def body(dq_ref, dk_ref, dv_ref, c_ref, s_ref, oq_ref, ok_ref, ov_ref):
        for src, dst, rotated in ((dq_ref, oq_ref, True), (dk_ref, ok_ref, True), (dv_ref, ov_ref, False)):
            for c in range(D_DIL // LANES):
                lanes = slice(c * LANES, (c + 1) * LANES)
                piece = src[:, lanes]
                dst[:, lanes] = (_rotate(piece, c_ref[...], -s_ref[...]) if rotated else piece).astype(BF16)

    return pl.pallas_call(
        body, name="dilated_finish", grid=(s // TM,),
        in_specs=[spec] * 3 + [tab, tab], out_specs=[spec] * 3,
        out_shape=[jax.ShapeDtypeStruct((s, D_DIL), BF16)] * 3,
        compiler_params=_params(32),
    )(*grads, cos, sin)


def _place():
    x, y, c = lax.axis_index("x"), lax.axis_index("y"), lax.axis_index("c")
    return x, y, c, 2 * x + y


def _chip(k, c):
    return (k >> 1, k & 1, c)


def _half(ref, h):
    n = ref.shape[0] // 2
    return ref.at[pl.ds(h * n, n)]


class _BackgroundGather:
    def __init__(self, ins, outs, scratch):
        n = self.n = len(ins)
        self.ins, self.outs = ins, outs
        self.mine, self.landed, self.passed = scratch[0:3 * n:3], scratch[1:3 * n:3], scratch[2:3 * n:3]
        self.send_sem, self.recv_sem, self.local_sem = scratch[3 * n:3 * n + 3]
        x, y, self.c, self.k = _place()
        self.sibling = (x, y, 1 - self.c)

    @staticmethod
    def scratch_shapes(shards):
        shapes = []
        for a in shards:
            half = (N_CHIP - 1, a.shape[0] // 2, a.shape[1])
            shapes += [pltpu.VMEM(a.shape, a.dtype), pltpu.VMEM(half, a.dtype), pltpu.VMEM(half, a.dtype)]
        n = len(shards)
        return shapes + [pltpu.SemaphoreType.DMA((6 * n,)), pltpu.SemaphoreType.DMA((6 * n,)),
                         pltpu.SemaphoreType.DMA((8 * n,))]

    @staticmethod
    def out_shapes(shards):
        return [jax.ShapeDtypeStruct((N_CHIP,) + a.shape, a.dtype) for a in shards]

    def _remote(self, a, slot, src, dst, to):
        return pltpu.make_async_remote_copy(src_ref=src, dst_ref=dst, send_sem=self.send_sem.at[6 * a + slot],
                                            recv_sem=self.recv_sem.at[6 * a + slot], device_id=to, device_id_type=MESH)

    def _local(self, a, slot, src, dst):
        return pltpu.make_async_copy(src, dst, self.local_sem.at[8 * a + slot])

    def _ici(self, a, j):
        return self._remote(a, j - 1, _half(self.mine[a], self.c), self.landed[a].at[j - 1], _chip(self.k ^ j, self.c))

    def _to_sibling(self, a, j):
        return self._remote(a, 2 + j, self.landed[a].at[j - 1], self.passed[a].at[j - 1], self.sibling)

    def _own(self, a):
        return self._local(a, 0, self.ins[a], self.outs[a].at[self.k])

    def _load(self, a):
        return self._local(a, 1, self.ins[a], self.mine[a])

    def _store_landed(self, a, j):
        return self._local(a, 1 + j, self.landed[a].at[j - 1], _half(self.outs[a].at[self.k ^ j], self.c))

    def _store_passed(self, a, j):
        return self._local(a, 4 + j, self.passed[a].at[j - 1], _half(self.outs[a].at[self.k ^ j], 1 - self.c))

    def start(self):
        for a in range(self.n):
            self._own(a).start()
            self._load(a).start()
        for a in range(self.n):
            self._load(a).wait()
            for j in range(1, N_CHIP):
                self._ici(a, j).start()

    def forward(self):
        for j in range(1, N_CHIP):
            for a in range(self.n):
                self._ici(a, j).wait_recv()
                self._to_sibling(a, j).start()
                self._store_landed(a, j).start()

    def finish(self):
        for j in range(1, N_CHIP):
            for a in range(self.n):
                self._to_sibling(a, j).wait_recv()
                self._store_passed(a, j).start()
        for a in range(self.n):
            for j in range(1, N_CHIP):
                self._ici(a, j).wait_send()
                self._to_sibling(a, j).wait_send()
                self._store_landed(a, j).wait()
                self._store_passed(a, j).wait()
            self._own(a).wait()


def _all_gather(shards):
    n = len(shards)
    any_spec = pl.BlockSpec(memory_space=pl.ANY)

    def body(*refs):
        gather = _BackgroundGather(refs[:n], refs[n:2 * n], refs[2 * n:])
        gather.start()
        gather.forward()
        gather.finish()

    return pl.pallas_call(
        body, name="weights_all_gather",
        in_specs=[any_spec] * n, out_specs=[any_spec] * n,
        out_shape=_BackgroundGather.out_shapes(shards),
        scratch_shapes=_BackgroundGather.scratch_shapes(shards),
        compiler_params=_params(32),
    )(*shards)


def _reduce_scatter(g, core, name):
    n, r, c = g.shape
    hr = r // 2
    once = pl.Buffered(1)
    in_specs = [pl.BlockSpec((n, hr, c), lambda i, core_ref: (0, core_ref[0], 0), pipeline_mode=once),
                pl.BlockSpec((n, hr, c), lambda i, core_ref: (0, 1 - core_ref[0], 0), pipeline_mode=once)]

    def body(core_ref, mine_ref, other_ref, out_ref, from_core, sums, sums_bf, from_chips, done, from_core2, send_sem, recv_sem):
        x, y, cc, k = _place()
        sibling = (x, y, 1 - cc)

        def copy(slot, src, dst, to):
            return pltpu.make_async_remote_copy(src_ref=src, dst_ref=dst, send_sem=send_sem.at[slot],
                                                recv_sem=recv_sem.at[slot], device_id=to, device_id_type=MESH)

        first = copy(0, other_ref, from_core, sibling)
        first.start()
        first.wait()
        total = mine_ref[...] + from_core[...]
        sums[...] = total
        sums_bf[...] = total.astype(BF16)
        sends = [copy(j, sums_bf.at[k ^ j], from_chips.at[j - 1], _chip(k ^ j, cc)) for j in range(1, N_CHIP)]
        for cp in sends:
            cp.start()
        for cp in sends:
            cp.wait()
        red = sums[k]
        for j in range(1, N_CHIP):
            red = red + from_chips[j - 1].astype(F32)
        done[...] = red
        last = copy(N_CHIP, done, from_core2, sibling)
        last.start()
        last.wait()
        row0 = pl.multiple_of(cc * hr, 8)
        row1 = pl.multiple_of((1 - cc) * hr, 8)
        out_ref[pl.ds(row0, hr), :] = red
        out_ref[pl.ds(row1, hr), :] = from_core2[...]

    grid_spec = pltpu.PrefetchScalarGridSpec(
        num_scalar_prefetch=1, grid=(1,), in_specs=in_specs,
        out_specs=pl.BlockSpec((r, c), lambda i, core_ref: (0, 0)),
        scratch_shapes=[pltpu.VMEM((n, hr, c), F32), pltpu.VMEM((n, hr, c), F32), pltpu.VMEM((n, hr, c), BF16),
                        pltpu.VMEM((N_CHIP - 1, hr, c), BF16), pltpu.VMEM((hr, c), F32), pltpu.VMEM((hr, c), F32),
                        pltpu.SemaphoreType.DMA((N_CHIP + 1,)), pltpu.SemaphoreType.DMA((N_CHIP + 1,))])
    return pl.pallas_call(
        body, name=name, grid_spec=grid_spec, out_shape=jax.ShapeDtypeStruct((r, c), F32),
        compiler_params=_params(56),
    )(core, g, g)


def _elementwise(fn, name, ins, n_out, rows):
    total, cols = ins[0].shape
    spec = pl.BlockSpec((rows, cols), lambda i: (i, 0))

    def body(*refs):
        res = fn(*[r[...] for r in refs[:len(ins)]])
        for o, v in zip(refs[len(ins):], res):
            o[...] = v

    return pl.pallas_call(
        body, name=name, grid=(total // rows,),
        in_specs=[spec] * len(ins), out_specs=[spec] * n_out,
        out_shape=[jax.ShapeDtypeStruct((total, cols), F32)] * n_out,
        compiler_params=_params(48),
    )(*ins)


def _adamw(w, g, m, v):
    m = ADAM_B1 * m + (1.0 - ADAM_B1) * g
    v = ADAM_B2 * v + (1.0 - ADAM_B2) * (g * g)
    m_hat = m / (1.0 - ADAM_B1 ** ADAM_STEP)
    v_hat = v / (1.0 - ADAM_B2 ** ADAM_STEP)
    delta = -ADAM_LR * (m_hat / (jnp.sqrt(v_hat) + ADAM_EPS) + ADAM_WD * w)
    return delta, m, v


def _reduce_and_update(grads, weights, moms, vels):
    core = lax.axis_index("c").astype(jnp.int32).reshape(1)
    full = [_reduce_scatter(g, core, f"grads_reduce_scatter_{a}") for a, g in enumerate(grads)]
    out = []
    for a, (g, w, m, v) in enumerate(zip(full, weights, moms, vels)):
        rows = g.shape[0] // 2
        out.append((g,) + tuple(_elementwise(lambda gg, ww, mm, vv: _adamw(ww, gg, mm, vv), f"adamw_{a}", [g, w, m, v], 3, rows)))
    return out


def _reduce_vectors(part, w, m, v):
    n_dev = 8

    def body(p_ref, w_ref, m_ref, v_ref, g_ref, d_ref, nm_ref, nv_ref, buf, send_sem, recv_sem):
        x, y, c, _ = _place()
        me = 4 * x + 2 * y + c
        buf[me] = p_ref[...]
        sends = []
        for off in range(1, n_dev):
            peer = me ^ off
            cp = pltpu.make_async_remote_copy(src_ref=p_ref, dst_ref=buf.at[me], send_sem=send_sem.at[off - 1],
                                              recv_sem=recv_sem.at[off - 1], device_id=(peer >> 2, (peer >> 1) & 1, peer & 1),
                                              device_id_type=MESH)
            cp.start()
            sends.append(cp)
        for off in range(1, n_dev):
            peer = me ^ off
            pltpu.make_async_remote_copy(src_ref=p_ref, dst_ref=buf.at[peer], send_sem=send_sem.at[off - 1],
                                         recv_sem=recv_sem.at[off - 1], device_id=(peer >> 2, (peer >> 1) & 1, peer & 1),
                                         device_id_type=MESH).wait_recv()
        for cp in sends:
            cp.wait_send()
        g = buf[0]
        for d in range(1, n_dev):
            g = g + buf[d]
        g_ref[...] = g
        delta, nm, nv = _adamw(w_ref[...], g, m_ref[...], v_ref[...])
        d_ref[...] = delta
        nm_ref[...] = nm
        nv_ref[...] = nv

    vm = pl.BlockSpec(memory_space=pltpu.VMEM)
    return pl.pallas_call(
        body, name="gains_all_reduce",
        in_specs=[vm] * 4, out_specs=[vm] * 4,
        out_shape=[jax.ShapeDtypeStruct(part.shape, F32)] * 4,
        scratch_shapes=[pltpu.VMEM((n_dev,) + part.shape, F32), pltpu.SemaphoreType.DMA((n_dev - 1,)),
                        pltpu.SemaphoreType.DMA((n_dev - 1,))],
    )(part, w, m, v)


def _pad_row(a):
    a = a.reshape(1, -1)
    return jnp.pad(a, ((0, 0), (0, D_MODEL - a.shape[1])))


def kernel(x, ffn1_norm, ffn1_w_gate, ffn1_w_up, ffn1_w_down, mix_norm, w_in, sb_out_norm, dil_out_norm, w_out, ffn2_norm, ffn2_w_gate, ffn2_w_up, ffn2_w_down, final_norm, loss_target, m_ffn1_norm, m_ffn1_w_gate, m_ffn1_w_up, m_ffn1_w_down, m_mix_norm, m_w_in, m_sb_out_norm, m_dil_out_norm, m_w_out, m_ffn2_norm, m_ffn2_w_gate, m_ffn2_w_up, m_ffn2_w_down, m_final_norm, v_ffn1_norm, v_ffn1_w_gate, v_ffn1_w_up, v_ffn1_w_down, v_mix_norm, v_w_in, v_sb_out_norm, v_dil_out_norm, v_w_out, v_ffn2_norm, v_ffn2_w_gate, v_ffn2_w_up, v_ffn2_w_down, v_final_norm):
    x = x[0]
    target = loss_target[0]
    s = x.shape[0]
    gf = final_norm.reshape(1, D_MODEL)
    cos, sin = _rope_tables(s)

    shard = lambda w: w[0].astype(BF16)
    wg1, wu1, wd1 = _all_gather([shard(ffn1_w_gate), shard(ffn1_w_up), shard(ffn1_w_down)])

    x1, hm, saved1, (win, wout, wd2) = _ffn1_fwd(x, ffn1_norm, mix_norm, (wg1, wu1), wd1,
                                                 [shard(w_in), shard(w_out), shard(ffn2_w_down)])
    wout = wout.reshape(D_MODEL, D_MODEL)
    qkv, (wg2, wu2) = _proj_fwd(hm, win, cos, sin, [shard(ffn2_w_gate), shard(ffn2_w_up)])
    o_sb = _sb_fwd(qkv)
    o_dl, lse = _dilated_fwd(qkv)
    x2 = _outproj_fwd(o_sb, o_dl, sb_out_norm, dil_out_norm, x1, wout)
    dx3, st_final, saved2 = _ffn2_fwd_loss(x2, ffn2_norm, gf, target, (wg2, wu2), wd2)

    dx2, dwg2, dwu2, dwd2, st_ffn2 = _ffn_bwd(x2, ffn2_norm, dx3, saved2, (wg2, wu2), wd2, 1)
    do_sb, do_dl, delta_dl, dwout, st_out = _outproj_bwd(dx2, o_sb, o_dl, sb_out_norm, dil_out_norm, wout)
    dq_sb, dk_sb, dv_sb = _sb_bwd(qkv, o_sb, do_sb)
    dq_dl, dk_dl, dv_dl = _dilated_finish(_dilated_bwd(qkv, delta_dl, lse, do_dl), cos, sin)
    dqkv = jnp.concatenate([dq_sb, dk_sb, dv_sb, dq_dl, dk_dl, dv_dl], axis=1)
    dx1, dwin, st_mix = _proj_bwd(x1, mix_norm, dqkv, win, dx2)
    grad_x, dwg1, dwu1, dwd1, st_ffn1 = _ffn_bwd(x, ffn1_norm, dx1, saved1, (wg1, wu1), wd1, 0)

    names = ["ffn1_w_gate", "ffn1_w_up", "ffn1_w_down", "w_in", "w_out", "ffn2_w_gate", "ffn2_w_up", "ffn2_w_down"]
    grads = [dwg1, dwu1, dwd1, dwin, dwout.reshape(N_CHIP, OUTB, D_MODEL), dwg2, dwu2, dwd2]
    weights = [ffn1_w_gate[0], ffn1_w_up[0], ffn1_w_down[0], w_in[0], w_out[0], ffn2_w_gate[0], ffn2_w_up[0], ffn2_w_down[0]]
    moms = [m_ffn1_w_gate[0], m_ffn1_w_up[0], m_ffn1_w_down[0], m_w_in[0], m_w_out[0], m_ffn2_w_gate[0], m_ffn2_w_up[0], m_ffn2_w_down[0]]
    vels = [v_ffn1_w_gate[0], v_ffn1_w_up[0], v_ffn1_w_down[0], v_w_in[0], v_w_out[0], v_ffn2_w_gate[0], v_ffn2_w_up[0], v_ffn2_w_down[0]]
    mats = {n: tuple(t[None] for t in r) for n, r in zip(names, _reduce_and_update(grads, weights, moms, vels))}

    vec_names = ["ffn1_norm", "mix_norm", "sb_out_norm", "dil_out_norm", "ffn2_norm", "final_norm"]
    part = jnp.concatenate([st_ffn1[0:1], st_mix[0:1], _pad_row(st_out[0]), _pad_row(st_out[1]), st_ffn2[0:1],
                            st_final[0:1], st_final[1:2], jnp.zeros((1, D_MODEL), F32)], axis=0)
    pack = lambda arrs: jnp.concatenate([_pad_row(a) for a in arrs] + [jnp.zeros((2, D_MODEL), F32)], axis=0)
    g_vec, d_vec, m_vec, v_vec = _reduce_vectors(
        part,
        pack([ffn1_norm, mix_norm, sb_out_norm, dil_out_norm, ffn2_norm, final_norm]),
        pack([m_ffn1_norm, m_mix_norm, m_sb_out_norm, m_dil_out_norm, m_ffn2_norm, m_final_norm]),
        pack([v_ffn1_norm, v_mix_norm, v_sb_out_norm, v_dil_out_norm, v_ffn2_norm, v_final_norm]))
    like = {"ffn1_norm": ffn1_norm, "mix_norm": mix_norm, "sb_out_norm": sb_out_norm, "dil_out_norm": dil_out_norm,
            "ffn2_norm": ffn2_norm, "final_norm": final_norm}
    vecs = {n: tuple(t[i, :like[n].size].reshape(like[n].shape) for t in (g_vec, d_vec, m_vec, v_vec))
            for i, n in enumerate(vec_names)}
    loss = 0.5 * jnp.sum(g_vec[6]) / D_MODEL

    order = ["ffn1_norm", "ffn1_w_gate", "ffn1_w_up", "ffn1_w_down", "mix_norm", "w_in", "sb_out_norm", "dil_out_norm",
             "w_out", "ffn2_norm", "ffn2_w_gate", "ffn2_w_up", "ffn2_w_down", "final_norm"]
    both = {**mats, **vecs}
    return (loss, grad_x[None], *[both[n][0] for n in order], *[both[n][1] for n in order],
            *[both[n][2] for n in order], *[both[n][3] for n in order])
```

```python
import functools

import jax
import jax.numpy as jnp
from jax import lax
from jax.experimental import pallas as pl
from jax.experimental.pallas import tpu as pltpu

D_MODEL = 1024
D_FF = 2816
HEAD_DIM = 64
D_SB = 512
D_DIL = 512
D_IN = 3072
N_CHIP = 4
FFB = D_FF // N_CHIP
INB = D_IN // N_CHIP
OUTB = D_MODEL // N_CHIP
BLK = 128
LANES = 128
DILATIONS = (1, 4, 16)
ROPE_THETA = 10000.0
RMS_EPS = 1e-6
SCALE = HEAD_DIM ** -0.5
NEG = -1e30
DEAD = -104.0
ADAM_LR = 0.001
ADAM_B1 = 0.9
ADAM_B2 = 0.999
ADAM_EPS = 1e-08
ADAM_WD = 0.01
ADAM_STEP = 10
MESH = pl.DeviceIdType.MESH
F32 = jnp.float32
BF16 = jnp.bfloat16
TM = 512


def _params(vmem_mb):
    return pltpu.CompilerParams(vmem_limit_bytes=vmem_mb << 20)


def _dot(a, b):
    return jnp.dot(a, b, preferred_element_type=F32)


def _dot_nt(a, b):
    return lax.dot_general(a, b, (((1,), (1,)), ((), ())), preferred_element_type=F32)


def _dot_tn(a, b):
    return lax.dot_general(a, b, (((0,), (0,)), ((), ())), preferred_element_type=F32)


def _rms_fwd(x, g):
    r = lax.rsqrt(jnp.mean(x * x, axis=-1, keepdims=True) + RMS_EPS)
    xh = x * r
    return xh * g, xh, r


def _rms_bwd(dy, xh, r, g):
    dyg = dy * g
    dx = r * (dyg - xh * jnp.mean(dyg * xh, axis=-1, keepdims=True))
    return dx, jnp.sum(dy * xh, axis=0, keepdims=True)


def _split_bf16(a):
    hi = a.astype(BF16)
    return hi, (a - hi.astype(F32)).astype(BF16)


def _dot_split(a, b2):
    hi, lo = _split_bf16(a)
    return _dot(jnp.concatenate([hi, lo], axis=1), b2)


def _ffn_weight_specs():
    return [pl.BlockSpec((None, D_MODEL, FFB), lambda i, j: (j, 0, 0)),
            pl.BlockSpec((None, D_MODEL, FFB), lambda i, j: (j, 0, 0)),
            pl.BlockSpec((None, FFB, D_MODEL), lambda i, j: (j, 0, 0))]


def _ffn_saved(s):
    hidden = jax.ShapeDtypeStruct((N_CHIP, s, FFB), BF16)
    hid = pl.BlockSpec((None, TM, FFB), lambda i, j: (j, i, 0))
    row = pl.BlockSpec((TM, D_MODEL), lambda i, j: (i, 0))
    return [row, hid, hid, hid], [jax.ShapeDtypeStruct((s, D_MODEL), BF16), hidden, hidden, hidden]


def _ffn_accumulate(h_ref, acc_scr, wg_ref, wu_ref, wd_ref, a_ref, b_ref, act_ref):
    h = h_ref[...]
    a = _dot(h, wg_ref[...])
    b = _dot(h, wu_ref[...])
    act = ((a * jax.nn.sigmoid(a)) * b).astype(BF16)
    a_ref[...] = a.astype(BF16)
    b_ref[...] = b.astype(BF16)
    act_ref[...] = act
    acc_scr[...] += _dot(act, wd_ref[...])


def _host_gather_before(gather, i, j, steps):
    @pl.when((i == 0) & (j == 0))
    def _():
        gather.start()

    @pl.when((i == (3 * steps) // 4) & (j == 0))
    def _():
        gather.forward()


def _host_gather_after(gather, i, j, steps):
    @pl.when((i == steps - 1) & (j == N_CHIP - 1))
    def _():
        gather.finish()


def _ffn1_fwd(x, g1, gmix, gu, wd, later_shards):
    s = x.shape[0]
    row = pl.BlockSpec((TM, D_MODEL), lambda i, j: (i, 0))
    vec = pl.BlockSpec((1, D_MODEL), lambda i, j: (0, 0))
    saved_specs, saved_shapes = _ffn_saved(s)
    n = len(later_shards)
    any_spec = pl.BlockSpec(memory_space=pl.ANY)

    def body(*refs):
        x_ref, g_ref, gm_ref, wg_ref, wu_ref, wd_ref = refs[:6]
        shard_refs, refs = refs[6:6 + n], refs[6 + n:]
        x1_ref, hm_ref, h_ref, a_ref, b_ref, act_ref = refs[:6]
        gathered_refs, acc_scr, gather_scratch = refs[6:6 + n], refs[6 + n], refs[7 + n:]
        gather = _BackgroundGather(shard_refs, gathered_refs, gather_scratch)
        i, j = pl.program_id(0), pl.program_id(1)
        _host_gather_before(gather, i, j, s // TM)

        @pl.when(j == 0)
        def _():
            h, _, _ = _rms_fwd(x_ref[...], g_ref[...])
            h_ref[...] = h.astype(BF16)
            acc_scr[...] = jnp.zeros_like(acc_scr)

        _ffn_accumulate(h_ref, acc_scr, wg_ref, wu_ref, wd_ref, a_ref, b_ref, act_ref)

        @pl.when(j == N_CHIP - 1)
        def _():
            x1 = x_ref[...] + 0.5 * acc_scr[...]
            x1_ref[...] = x1
            hm, _, _ = _rms_fwd(x1, gm_ref[...])
            hm_ref[...] = hm.astype(BF16)

        _host_gather_after(gather, i, j, s // TM)

    x1, hm, h, a, b, act, *gathered = pl.pallas_call(
        body, name="ffn1_fwd", grid=(s // TM, N_CHIP),
        in_specs=[row, vec, vec] + _ffn_weight_specs() + [any_spec] * n,
        out_specs=[row, row] + saved_specs + [any_spec] * n,
        out_shape=([jax.ShapeDtypeStruct((s, D_MODEL), F32), jax.ShapeDtypeStruct((s, D_MODEL), BF16)] + saved_shapes
                   + _BackgroundGather.out_shapes(later_shards)),
        scratch_shapes=[pltpu.VMEM((TM, D_MODEL), F32)] + _BackgroundGather.scratch_shapes(later_shards),
        compiler_params=_params(58),
    )(x, g1, gmix, gu[0], gu[1], wd, *later_shards)
    return x1, hm, [h, a, b, act], gathered


def _ffn2_fwd_loss(x2, g2, gf, target, gu, wd):
    s = x2.shape[0]
    row = pl.BlockSpec((TM, D_MODEL), lambda i, j: (i, 0))
    vec = pl.BlockSpec((1, D_MODEL), lambda i, j: (0, 0))
    stat = pl.BlockSpec((8, D_MODEL), lambda i, j: (0, 0))
    saved_specs, saved_shapes = _ffn_saved(s)

    def body(x_ref, g_ref, gf_ref, t_ref, wg_ref, wu_ref, wd_ref, dx_ref, st_ref, h_ref, a_ref, b_ref, act_ref, acc_scr):
        i, j = pl.program_id(0), pl.program_id(1)

        @pl.when((i == 0) & (j == 0))
        def _():
            st_ref[...] = jnp.zeros_like(st_ref)

        @pl.when(j == 0)
        def _():
            h, _, _ = _rms_fwd(x_ref[...], g_ref[...])
            h_ref[...] = h.astype(BF16)
            acc_scr[...] = jnp.zeros_like(acc_scr)

        _ffn_accumulate(h_ref, acc_scr, wg_ref, wu_ref, wd_ref, a_ref, b_ref, act_ref)

        @pl.when(j == N_CHIP - 1)
        def _():
            x3 = x_ref[...] + 0.5 * acc_scr[...]
            y, xh, r = _rms_fwd(x3, gf_ref[...])
            err = y - t_ref[...]
            dx, dg = _rms_bwd(err * (1.0 / D_MODEL), xh, r, gf_ref[...])
            dx_ref[...] = dx
            st_ref[0:1, :] += dg
            st_ref[1:2, :] += jnp.sum(err * err, axis=0, keepdims=True)

    dx3, st, *saved = pl.pallas_call(
        body, name="ffn2_fwd_loss", grid=(s // TM, N_CHIP),
        in_specs=[row, vec, vec, row] + _ffn_weight_specs(),
        out_specs=[row, stat] + saved_specs,
        out_shape=[jax.ShapeDtypeStruct((s, D_MODEL), F32), jax.ShapeDtypeStruct((8, D_MODEL), F32)] + saved_shapes,
        scratch_shapes=[pltpu.VMEM((TM, D_MODEL), F32)],
        compiler_params=_params(56),
    )(x2, g2, gf, target, gu[0], gu[1], wd)
    return dx3, st, saved


def _ffn_bwd(xin, g, dy, saved, gu, wd, f):
    s = xin.shape[0]
    hb, gate, up, act = saved
    row = pl.BlockSpec((TM, D_MODEL), lambda i, j: (i, 0))
    vec = pl.BlockSpec((1, D_MODEL), lambda i, j: (0, 0))
    stat = pl.BlockSpec((8, D_MODEL), lambda i, j: (0, 0))
    hid = pl.BlockSpec((None, TM, FFB), lambda i, j: (j, i, 0))

    def body(x_ref, g_ref, dy_ref, a_ref, b_ref, wg_ref, wu_ref, wd_ref, out_ref, dyh_ref, da_ref, db_ref, st_ref, dh_scr):
        i, j = pl.program_id(0), pl.program_id(1)

        @pl.when((i == 0) & (j == 0))
        def _():
            st_ref[...] = jnp.zeros_like(st_ref)

        @pl.when(j == 0)
        def _():
            dyh_ref[...] = (0.5 * dy_ref[...]).astype(BF16)
            dh_scr[...] = jnp.zeros_like(dh_scr)

        a = a_ref[...].astype(F32)
        b = b_ref[...].astype(F32)
        sg = jax.nn.sigmoid(a)
        dact = _dot_nt(dyh_ref[...], wd_ref[...])
        dab = (dact * b * (sg * (1.0 + a * (1.0 - sg)))).astype(BF16)
        dbb = (dact * (a * sg)).astype(BF16)
        da_ref[...] = dab
        db_ref[...] = dbb
        dh_scr[...] += _dot_nt(dab, wg_ref[...]) + _dot_nt(dbb, wu_ref[...])

        @pl.when(j == N_CHIP - 1)
        def _():
            _, xh, r = _rms_fwd(x_ref[...], g_ref[...])
            dx, dg = _rms_bwd(dh_scr[...], xh, r, g_ref[...])
            out_ref[...] = dy_ref[...] + dx
            st_ref[0:1, :] += dg

    hidden = jax.ShapeDtypeStruct((N_CHIP, s, FFB), BF16)
    dx, dyh, da, db, st = pl.pallas_call(
        body, name=f"ffn{f + 1}_bwd_dx", grid=(s // TM, N_CHIP),
        in_specs=[row, vec, row, hid, hid] + _ffn_weight_specs(),
        out_specs=[row, row, hid, hid, stat],
        out_shape=[jax.ShapeDtypeStruct((s, D_MODEL), F32), jax.ShapeDtypeStruct((s, D_MODEL), BF16),
                   hidden, hidden, jax.ShapeDtypeStruct((8, D_MODEL), F32)],
        scratch_shapes=[pltpu.VMEM((TM, D_MODEL), F32)],
        compiler_params=_params(56),
    )(xin, g, dy, gate, up, gu[0], gu[1], wd)

    tok = pl.BlockSpec((TM, D_MODEL), lambda j, i: (i, 0))
    hid2 = pl.BlockSpec((None, TM, FFB), lambda j, i: (j, i, 0))
    gspecs = [pl.BlockSpec((None, D_MODEL, FFB), lambda j, i: (j, 0, 0)),
              pl.BlockSpec((None, D_MODEL, FFB), lambda j, i: (j, 0, 0)),
              pl.BlockSpec((None, FFB, D_MODEL), lambda j, i: (j, 0, 0))]

    def wbody(h_ref, dyh_ref, da_ref, db_ref, act_ref, dwg_ref, dwu_ref, dwd_ref):
        @pl.when(pl.program_id(1) == 0)
        def _():
            dwg_ref[...] = jnp.zeros_like(dwg_ref)
            dwu_ref[...] = jnp.zeros_like(dwu_ref)
            dwd_ref[...] = jnp.zeros_like(dwd_ref)

        hb = h_ref[...]
        dwg_ref[...] += _dot_tn(hb, da_ref[...])
        dwu_ref[...] += _dot_tn(hb, db_ref[...])
        dwd_ref[...] += _dot_tn(act_ref[...], dyh_ref[...])

    dwg, dwu, dwd = pl.pallas_call(
        wbody, name=f"ffn{f + 1}_bwd_dw", grid=(N_CHIP, s // TM),
        in_specs=[tok, tok, hid2, hid2, hid2], out_specs=gspecs,
        out_shape=[jax.ShapeDtypeStruct((N_CHIP, D_MODEL, FFB), F32),
                   jax.ShapeDtypeStruct((N_CHIP, D_MODEL, FFB), F32),
                   jax.ShapeDtypeStruct((N_CHIP, FFB, D_MODEL), F32)],
        compiler_params=_params(48),
    )(hb, dyh, da, db, act)
    return dx, dwg, dwu, dwd, st


def _rope_tables(s):
    half = HEAD_DIM // 2
    inv_freq = ROPE_THETA ** (-jnp.arange(half, dtype=F32) / half)
    ang = jnp.arange(s).astype(F32)[:, None] * inv_freq[None, :]
    cos, sin = jnp.cos(ang), jnp.sin(ang)
    cos2 = jnp.concatenate([cos, cos], axis=-1)
    sin2 = jnp.concatenate([-sin, sin], axis=-1)
    return jnp.tile(cos2, (1, LANES // HEAD_DIM)), jnp.tile(sin2, (1, LANES // HEAD_DIM))


def _rotate(t, cos, sin_signed):
    lane = lax.broadcasted_iota(jnp.int32, t.shape, 1)
    first = (lane % HEAD_DIM) < (HEAD_DIM // 2)
    partner = jnp.where(first, pltpu.roll(t, LANES - HEAD_DIM // 2, 1), pltpu.roll(t, HEAD_DIM // 2, 1))
    return t * cos + partner * sin_signed


def _proj_fwd(hm, win, cos, sin, later_shards):
    s = hm.shape[0]
    n_sub = INB // LANES
    first_rot, last_rot = (3 * D_SB) // LANES, (3 * D_SB + 2 * D_DIL) // LANES
    n = len(later_shards)
    any_spec = pl.BlockSpec(memory_space=pl.ANY)

    def body(*refs):
        h_ref, w_ref, c_ref, s_ref = refs[:4]
        shard_refs, o_ref, gathered_refs, gather_scratch = refs[4:4 + n], refs[4 + n], refs[5 + n:5 + 2 * n], refs[5 + 2 * n:]
        gather = _BackgroundGather(shard_refs, gathered_refs, gather_scratch)
        i, j = pl.program_id(0), pl.program_id(1)
        _host_gather_before(gather, i, j, s // TM)
        r = _dot(h_ref[...], w_ref[...])
        for c in range(n_sub):
            t = r[:, c * LANES:(c + 1) * LANES]
            col = j * n_sub + c
            rot = (col >= first_rot) & (col < last_rot)
            lanes = slice(c * LANES, (c + 1) * LANES)

            @pl.when(rot)
            def _():
                o_ref[:, lanes] = _rotate(t, c_ref[...], s_ref[...]).astype(BF16)

            @pl.when(jnp.logical_not(rot))
            def _():
                o_ref[:, lanes] = t.astype(BF16)

        _host_gather_after(gather, i, j, s // TM)

    qkv, *gathered = pl.pallas_call(
        body, name="proj_fwd", grid=(s // TM, N_CHIP),
        in_specs=[pl.BlockSpec((TM, D_MODEL), lambda i, j: (i, 0)),
                  pl.BlockSpec((None, D_MODEL, INB), lambda i, j: (j, 0, 0)),
                  pl.BlockSpec((TM, LANES), lambda i, j: (i, 0)),
                  pl.BlockSpec((TM, LANES), lambda i, j: (i, 0))] + [any_spec] * n,
        out_specs=[pl.BlockSpec((TM, INB), lambda i, j: (i, j))] + [any_spec] * n,
        out_shape=[jax.ShapeDtypeStruct((s, D_IN), BF16)] + _BackgroundGather.out_shapes(later_shards),
        scratch_shapes=_BackgroundGather.scratch_shapes(later_shards),
        compiler_params=_params(48),
    )(hm, win, cos, sin, *later_shards)
    return qkv, gathered


def _proj_bwd(x1, gmix, dqkv, win, dx2):
    s = x1.shape[0]
    row = pl.BlockSpec((TM, D_MODEL), lambda i, j: (i, 0))
    vec = pl.BlockSpec((1, D_MODEL), lambda i, j: (0, 0))

    def body(x_ref, g_ref, dq_ref, w_ref, dx2_ref, out_ref, dw_ref, st_ref, h_scr, dh_scr):
        i, j = pl.program_id(0), pl.program_id(1)

        @pl.when((i == 0) & (j == 0))
        def _():
            st_ref[...] = jnp.zeros_like(st_ref)
            dw_ref[...] = jnp.zeros_like(dw_ref)

        @pl.when(j == 0)
        def _():
            h, _, _ = _rms_fwd(x_ref[...], g_ref[...])
            h_scr[...] = h.astype(BF16)
            dh_scr[...] = jnp.zeros_like(dh_scr)

        dq = dq_ref[...]
        dw_ref[j] += _dot_tn(h_scr[...], dq)
        dh_scr[...] += _dot_nt(dq, w_ref[...])

        @pl.when(j == N_CHIP - 1)
        def _():
            _, xh, r = _rms_fwd(x_ref[...], g_ref[...])
            dx, dg = _rms_bwd(dh_scr[...], xh, r, g_ref[...])
            out_ref[...] = dx2_ref[...] + dx
            st_ref[0:1, :] += dg

    return pl.pallas_call(
        body, name="proj_bwd", grid=(s // TM, N_CHIP),
        in_specs=[row, vec, pl.BlockSpec((TM, INB), lambda i, j: (i, j)),
                  pl.BlockSpec((None, D_MODEL, INB), lambda i, j: (j, 0, 0)), row],
        out_specs=[row, pl.BlockSpec((N_CHIP, D_MODEL, INB), lambda i, j: (0, 0, 0)),
                   pl.BlockSpec((8, D_MODEL), lambda i, j: (0, 0))],
        out_shape=[jax.ShapeDtypeStruct((s, D_MODEL), F32),
                   jax.ShapeDtypeStruct((N_CHIP, D_MODEL, INB), F32),
                   jax.ShapeDtypeStruct((8, D_MODEL), F32)],
        scratch_shapes=[pltpu.VMEM((TM, D_MODEL), BF16), pltpu.VMEM((TM, D_MODEL), F32)],
        compiler_params=_params(56),
    )(x1, gmix, dqkv, win, dx2)


def _outproj_fwd(o_sb, o_dl, g_sb, g_dl, x1, wout):
    s = x1.shape[0]
    half = pl.BlockSpec((TM, D_SB), lambda i: (i, 0))
    row = pl.BlockSpec((TM, D_MODEL), lambda i: (i, 0))
    vec = pl.BlockSpec((1, D_SB), lambda i: (0, 0))

    def body(a_ref, b_ref, ga_ref, gb_ref, x_ref, w_ref, o_ref):
        ma, _, _ = _rms_fwd(a_ref[...], ga_ref[...])
        mb, _, _ = _rms_fwd(b_ref[...], gb_ref[...])
        o_ref[...] = (x_ref[...] + _dot(ma.astype(BF16), w_ref[0:D_SB, :])
                      + _dot(mb.astype(BF16), w_ref[D_SB:D_MODEL, :]))

    return pl.pallas_call(
        body, name="outproj_fwd", grid=(s // TM,),
        in_specs=[half, half, vec, vec, row, pl.BlockSpec((D_MODEL, D_MODEL), lambda i: (0, 0))],
        out_specs=row, out_shape=jax.ShapeDtypeStruct((s, D_MODEL), F32),
        compiler_params=_params(32),
    )(o_sb, o_dl, g_sb, g_dl, x1, wout)


def _outproj_bwd(dx2, o_sb, o_dl, g_sb, g_dl, wout):
    s = dx2.shape[0]
    half = pl.BlockSpec((TM, D_SB), lambda i: (i, 0))
    row = pl.BlockSpec((TM, D_MODEL), lambda i: (i, 0))
    vec = pl.BlockSpec((1, D_SB), lambda i: (0, 0))
    full = pl.BlockSpec((D_MODEL, D_MODEL), lambda i: (0, 0))

    def body(dy_ref, a_ref, b_ref, ga_ref, gb_ref, w_ref, da_ref, db_ref, dl_ref, dw_ref, st_ref):
        @pl.when(pl.program_id(0) == 0)
        def _():
            dw_ref[...] = jnp.zeros_like(dw_ref)
            st_ref[...] = jnp.zeros_like(st_ref)

        dy = dy_ref[...].astype(BF16)
        dm = _dot_nt(dy, w_ref[...])
        ma, xa, ra = _rms_fwd(a_ref[...], ga_ref[...])
        mb, xb, rb = _rms_fwd(b_ref[...], gb_ref[...])
        dw_ref[0:D_SB, :] += _dot_tn(ma.astype(BF16), dy)
        dw_ref[D_SB:D_MODEL, :] += _dot_tn(mb.astype(BF16), dy)
        da, dga = _rms_bwd(dm[:, 0:D_SB], xa, ra, ga_ref[...])
        db, dgb = _rms_bwd(dm[:, D_SB:D_MODEL], xb, rb, gb_ref[...])
        da_ref[...] = da
        db_ref[...] = db
        r = lax.broadcasted_iota(jnp.int32, (LANES, LANES), 0) >= HEAD_DIM
        c = lax.broadcasted_iota(jnp.int32, (LANES, LANES), 1) >= HEAD_DIM
        same_head = jnp.where(r == c, 1.0, 0.0).astype(BF16)
        same_head = jnp.concatenate([same_head, same_head], axis=0)
        prod = db * b_ref[...]
        for k in range(D_DIL // LANES):
            lanes = slice(k * LANES, (k + 1) * LANES)
            dl_ref[:, lanes] = _dot_split(prod[:, lanes], same_head)
        st_ref[0:1, :] += dga
        st_ref[1:2, :] += dgb

    return pl.pallas_call(
        body, name="outproj_bwd", grid=(s // TM,),
        in_specs=[row, half, half, vec, vec, full],
        out_specs=[half, half, half, full, pl.BlockSpec((8, D_SB), lambda i: (0, 0))],
        out_shape=[jax.ShapeDtypeStruct((s, D_SB), F32), jax.ShapeDtypeStruct((s, D_SB), F32),
                   jax.ShapeDtypeStruct((s, D_DIL), F32),
                   jax.ShapeDtypeStruct((D_MODEL, D_MODEL), F32), jax.ShapeDtypeStruct((8, D_SB), F32)],
        compiler_params=_params(48),
    )(dx2, o_sb, o_dl, g_sb, g_dl, wout)


def _head_masks():
    lane = lax.broadcasted_iota(jnp.int32, (BLK, LANES), 1)
    return [lane < HEAD_DIM, lane >= HEAD_DIM]


def _keep(mask, a):
    return a * jnp.where(mask, 1.0, 0.0).astype(a.dtype)


def _suffix_matrices():
    r = lax.broadcasted_iota(jnp.int32, (2 * BLK, BLK), 0) & (BLK - 1)
    c = lax.broadcasted_iota(jnp.int32, (2 * BLK, BLK), 1)
    ones = jnp.ones((2 * BLK, BLK), BF16)
    excl = jnp.concatenate([(r > c).astype(BF16), ones], axis=1)
    incl = jnp.concatenate([(r >= c).astype(BF16), ones], axis=1)
    return excl, incl


def _blk(i):
    return pl.ds(pl.multiple_of(i * BLK, BLK), BLK)


def _alive(carry_m):
    return (jnp.max(carry_m) > DEAD).astype(jnp.int32)


def _more_keys(last, carry):
    return (carry[0] * SB_KB <= last) & (carry[1] > 0)


def _stack_heads(a):
    masks = _head_masks()
    return jnp.concatenate([_keep(masks[0], a), _keep(masks[1], a)], axis=0)


def _unstack_heads(a2):
    return jnp.where(_head_masks()[0], a2[:BLK], a2[BLK:])


def _head_rowsum(a):
    masks = _head_masks()
    return jnp.concatenate([jnp.sum(jnp.where(m, a, 0.0), axis=1, keepdims=True) for m in masks], axis=0)


SB_QB = 2
SB_ROWS = SB_QB * 2 * BLK
SB_KB = 4
PAST_START = 1 << 30


def _sb_rows(ref, i0, cast=None):
    tiles = [ref[_blk(i0 + t), :] for t in range(SB_QB)]
    return jnp.concatenate([_stack_heads(t if cast is None else t.astype(cast)) for t in tiles], axis=0)


def _sb_scores(q2, k, i, j, carry_m, u_excl):
    r = lax.broadcasted_iota(jnp.int32, (SB_ROWS, BLK), 0)
    row = (r & (BLK - 1)) + ((r >> 8) << 7)
    col = lax.broadcasted_iota(jnp.int32, (SB_ROWS, BLK), 1)
    valid = (jnp.where(j >= 0, j * BLK, PAST_START) + col) < (i * BLK + row)
    z = _dot_nt(q2, k) * SCALE
    sp = jnp.maximum(z, 0.0) + jnp.log(1.0 + jnp.exp(-jnp.abs(z)))
    log_stay = jnp.where(valid, -sp, 0.0)
    log_beta = z - sp
    sums = _dot_split(log_stay, u_excl)
    later = carry_m + sums[:, :BLK]
    w = jnp.where(valid, jnp.exp(log_beta + later), 0.0)
    return valid, log_beta, w, carry_m + sums[:, BLK:]


def _sb_fwd(qkv):
    s = qkv.shape[0]
    nq = s // BLK
    pairs = D_SB // LANES
    col = lambda off: pl.BlockSpec((s, LANES), lambda p: (0, off + p))

    def body(q_ref, k_ref, v_ref, o_ref):
        u_excl, _ = _suffix_matrices()
        zero = jnp.zeros((SB_ROWS, LANES), F32)

        def q_block(ib, _):
            i = ib * SB_QB
            last = i + SB_QB - 1
            q2 = _sb_rows(q_ref, i)

            def k_block(carry):
                jj, _, carry_m, acc = carry
                for t in range(SB_KB):
                    j = last - jj * SB_KB - t
                    at = _blk(jnp.maximum(j, 0))
                    _, _, w, carry_m = _sb_scores(q2, k_ref[at, :], i, j, carry_m, u_excl)
                    acc = acc + _dot(w.astype(BF16), v_ref[at, :])
                return jj + 1, _alive(carry_m), carry_m, acc

            _, _, _, acc = lax.while_loop(functools.partial(_more_keys, last), k_block,
                                          (jnp.int32(0), jnp.int32(1), zero, zero))
            for t in range(SB_QB):
                o_ref[_blk(i + t), :] = _unstack_heads(acc[2 * BLK * t:2 * BLK * (t + 1)])
            return 0

        lax.fori_loop(0, nq // SB_QB, q_block, 0)

    return pl.pallas_call(
        body, name="sb_fwd", grid=(pairs,),
        in_specs=[col(0), col(pairs), col(2 * pairs)],
        out_specs=pl.BlockSpec((s, LANES), lambda p: (0, p)),
        out_shape=jax.ShapeDtypeStruct((s, D_SB), F32),
        compiler_params=_params(48),
    )(qkv, qkv, qkv)


def _sb_bwd(qkv, o_sb, do_sb):
    s = qkv.shape[0]
    nq = s // BLK
    pairs = D_SB // LANES
    col = lambda off: pl.BlockSpec((s, LANES), lambda p: (0, off + p))
    own = pl.BlockSpec((s, LANES), lambda p: (0, p))

    def body(q_ref, k_ref, v_ref, o_ref, do_ref, dq_ref, dk_ref, dv_ref, dk_acc, dv_acc):
        u_excl, u_incl = _suffix_matrices()
        zero = jnp.zeros((SB_ROWS, LANES), F32)
        dk_acc[...] = jnp.zeros_like(dk_acc)
        dv_acc[...] = jnp.zeros_like(dv_acc)

        def q_block(ib, _):
            i = ib * SB_QB
            last = i + SB_QB - 1
            q2 = _sb_rows(q_ref, i)
            do2 = _sb_rows(do_ref, i, BF16)
            totals = [_head_rowsum(do_ref[_blk(i + t), :].astype(BF16).astype(F32) * o_ref[_blk(i + t), :])
                      for t in range(SB_QB)]
            total = jnp.broadcast_to(jnp.concatenate(totals, axis=0), (SB_ROWS, BLK))

            def k_block(carry):
                jj, _, carry_m, carry_g, dq = carry
                for t in range(SB_KB):
                    j = last - jj * SB_KB - t
                    at = _blk(jnp.maximum(j, 0))
                    k = k_ref[at, :]
                    valid, log_beta, w, carry_m = _sb_scores(q2, k, i, j, carry_m, u_excl)
                    wb = w.astype(BF16)
                    g = wb.astype(F32) * _dot_nt(do2, v_ref[at, :])
                    sums = _dot_split(g, u_incl)
                    before = total - (carry_g + sums[:, :BLK])
                    dz = jnp.where(valid, g - jnp.exp(log_beta) * (g + before), 0.0)
                    dzb = (dz * SCALE).astype(BF16)
                    dk_acc[at, :] += _dot_tn(dzb, q2)
                    dv_acc[at, :] += _dot_tn(wb, do2)
                    carry_g = carry_g + sums[:, BLK:]
                    dq = dq + _dot(dzb, k)
                return jj + 1, _alive(carry_m), carry_m, carry_g, dq

            _, _, _, _, dq = lax.while_loop(functools.partial(_more_keys, last), k_block,
                                            (jnp.int32(0), jnp.int32(1), zero, zero, zero))
            for t in range(SB_QB):
                dq_ref[_blk(i + t), :] = _unstack_heads(dq[2 * BLK * t:2 * BLK * (t + 1)]).astype(BF16)
            return 0

        lax.fori_loop(0, nq // SB_QB, q_block, 0)
        dk_ref[...] = dk_acc[...].astype(BF16)
        dv_ref[...] = dv_acc[...].astype(BF16)

    return pl.pallas_call(
        body, name="sb_bwd", grid=(pairs,),
        in_specs=[col(0), col(pairs), col(2 * pairs), own, own],
        out_specs=[own, own, own],
        out_shape=[jax.ShapeDtypeStruct((s, D_SB), BF16)] * 3,
        scratch_shapes=[pltpu.VMEM((s, LANES), F32), pltpu.VMEM((s, LANES), F32)],
        compiler_params=_params(56),
    )(qkv, qkv, qkv, o_sb, do_sb)


DIL_UNROLL = 8


def _band_masks(b):
    row = lax.broadcasted_iota(jnp.int32, (2 * BLK, BLK), 0) & (BLK - 1)
    col = lax.broadcasted_iota(jnp.int32, (2 * BLK, BLK), 1)
    return col <= row, (col - row) >= jnp.where(b > 0, 0, BLK)


def _dil_tiles(qf, kf, vf, d, t, nb):
    c, b = t // nb, t % nb
    start = c + d * BLK * b
    rows = pl.ds(start, BLK, stride=d)
    prev = pl.ds(jnp.where(b > 0, start - d * BLK, start), BLK, stride=d)
    bf = lambda ref, sl: ref[sl, :].astype(BF16)
    return b, rows, prev, _stack_heads(bf(qf, rows)), bf(kf, rows), bf(kf, prev), bf(vf, rows), bf(vf, prev)


def _lanes_of_heads(col2):
    return _unstack_heads(jnp.broadcast_to(col2, (2 * BLK, LANES)))


def _dilated_fwd(qkv):
    s = qkv.shape[0]
    pairs = D_DIL // LANES
    base = (3 * D_SB) // LANES
    col = lambda off: pl.BlockSpec((s, LANES), lambda p: (0, off + p))
    own = pl.BlockSpec((s, LANES), lambda p: (0, p))

    def body(q_ref, k_ref, v_ref, acc_ref, m_ref, qf, kf, vf, l_scr):
        qf[...] = q_ref[...].astype(F32)
        kf[...] = k_ref[...].astype(F32)
        vf[...] = v_ref[...].astype(F32)
        for d in DILATIONS:
            nb = s // (d * BLK)

            def block(t, _):
                b, rows, prev, q2, kc, kp, vc, vp = _dil_tiles(qf, kf, vf, d, t, nb)
                in_cur, in_prev = _band_masks(b)
                zc = jnp.where(in_cur, _dot_nt(q2, kc) * SCALE, NEG)
                zp = jnp.where(in_prev, _dot_nt(q2, kp) * SCALE, NEG)
                m = jnp.maximum(jnp.max(zc, axis=1, keepdims=True), jnp.max(zp, axis=1, keepdims=True))
                pc, pp = jnp.exp(zc - m), jnp.exp(zp - m)
                den = jnp.sum(pc, axis=1, keepdims=True) + jnp.sum(pp, axis=1, keepdims=True)
                acc = _unstack_heads(_dot(pc.astype(BF16), vc) + _dot(pp.astype(BF16), vp))
                m_t, l_t = _lanes_of_heads(m), _lanes_of_heads(den)
                if d == DILATIONS[0]:
                    m_ref[rows, :] = m_t
                    l_scr[rows, :] = l_t
                    acc_ref[rows, :] = acc
                else:
                    m_old = m_ref[rows, :]
                    m_new = jnp.maximum(m_old, m_t)
                    keep, add = jnp.exp(m_old - m_new), jnp.exp(m_t - m_new)
                    m_ref[rows, :] = m_new
                    l_scr[rows, :] = l_scr[rows, :] * keep + l_t * add
                    acc_ref[rows, :] = acc_ref[rows, :] * keep + acc * add
                return 0

            lax.fori_loop(0, s // BLK, block, 0, unroll=DIL_UNROLL)

        def finish(i, _):
            l = l_scr[_blk(i), :]
            acc_ref[_blk(i), :] = acc_ref[_blk(i), :] / l
            m_ref[_blk(i), :] = m_ref[_blk(i), :] + jnp.log(l)
            return 0

        lax.fori_loop(0, s // BLK, finish, 0)

    return pl.pallas_call(
        body, name="dilated_fwd", grid=(pairs,),
        in_specs=[col(base), col(base + pairs), col(base + 2 * pairs)],
        out_specs=[own, own],
        out_shape=[jax.ShapeDtypeStruct((s, D_DIL), F32)] * 2,
        scratch_shapes=[pltpu.VMEM((s, LANES), F32)] * 4,
        compiler_params=_params(56),
    )(qkv, qkv, qkv)


def _stack_lanes(t):
    other = pltpu.roll(t, HEAD_DIM, 1)
    first = _head_masks()[0]
    return jnp.concatenate([jnp.where(first, t, other), jnp.where(first, other, t)], axis=0)


def _dilated_bwd(qkv, delta, lse, dout):
    s = qkv.shape[0]
    pairs = D_DIL // LANES
    base = (3 * D_SB) // LANES
    once = pl.Buffered(1)
    col = lambda off: pl.BlockSpec((s, LANES), lambda p: (0, off + p), pipeline_mode=once)
    own = pl.BlockSpec((s, LANES), lambda p: (0, p), pipeline_mode=once)
    res = pl.BlockSpec((s, LANES), lambda p: (0, p))

    def body(q_ref, k_ref, v_ref, dl_ref, l_ref, do_ref, dq_ref, dk_ref, dv_ref, qf, kf, vf):
        qf[...] = q_ref[...].astype(F32)
        kf[...] = k_ref[...].astype(F32)
        vf[...] = v_ref[...].astype(F32)
        dq_ref[...] = jnp.zeros_like(dq_ref)
        dk_ref[...] = jnp.zeros_like(dk_ref)
        dv_ref[...] = jnp.zeros_like(dv_ref)
        for d in DILATIONS:
            nb = s // (d * BLK)

            def block(t, _):
                b, rows, prev, q2, kc, kp, vc, vp = _dil_tiles(qf, kf, vf, d, t, nb)
                in_cur, in_prev = _band_masks(b)
                do2 = _stack_heads(do_ref[rows, :].astype(BF16))
                delta = _stack_lanes(dl_ref[rows, :])
                lse2 = _stack_lanes(l_ref[rows, :])
                wc = jnp.exp(jnp.where(in_cur, _dot_nt(q2, kc) * SCALE, NEG) - lse2)
                wp = jnp.exp(jnp.where(in_prev, _dot_nt(q2, kp) * SCALE, NEG) - lse2)
                dzc = (wc * (_dot_nt(do2, vc) - delta) * SCALE).astype(BF16)
                dzp = (wp * (_dot_nt(do2, vp) - delta) * SCALE).astype(BF16)
                dq_ref[rows, :] += _unstack_heads(_dot(dzc, kc) + _dot(dzp, kp))
                dk_ref[rows, :] += _dot_tn(dzc, q2)
                dk_ref[prev, :] += _dot_tn(dzp, q2)
                dv_ref[rows, :] += _dot_tn(wc.astype(BF16), do2)
                dv_ref[prev, :] += _dot_tn(wp.astype(BF16), do2)
                return 0

            lax.fori_loop(0, s // BLK, block, 0, unroll=DIL_UNROLL)

    return pl.pallas_call(
        body, name="dilated_bwd", grid=(pairs,),
        in_specs=[col(base), col(base + pairs), col(base + 2 * pairs), own, own, own],
        out_specs=[res, res, res],
        out_shape=[jax.ShapeDtypeStruct((s, D_DIL), F32)] * 3,
        scratch_shapes=[pltpu.VMEM((s, LANES), F32)] * 3,
        compiler_params=_params(60),
    )(qkv, qkv, qkv, delta, lse, dout)


def _dilated_finish(grads, cos, sin):
    s = grads[0].shape[0]
    spec = pl.BlockSpec((TM, D_DIL), lambda i: (i, 0))
    tab = pl.BlockSpec((TM, LANES), lambda i: (i, 0))

    def body(dq_ref, dk_ref, dv_ref, c_ref, s_ref, oq_ref, ok_ref, ov_ref):
        for src, dst, rotated in ((dq_ref, oq_ref, True), (dk_ref, ok_ref, True), (dv_ref, ov_ref, False)):
            for c in range(D_DIL // LANES):
                lanes = slice(c * LANES, (c + 1) * LANES)
                piece = src[:, lanes]
                dst[:, lanes] = (_rotate(piece, c_ref[...], -s_ref[...]) if rotated else piece).astype(BF16)

    return pl.pallas_call(
        body, name="dilated_finish", grid=(s // TM,),
        in_specs=[spec] * 3 + [tab, tab], out_specs=[spec] * 3,
        out_shape=[jax.ShapeDtypeStruct((s, D_DIL), BF16)] * 3,
        compiler_params=_params(32),
    )(*grads, cos, sin)


def _place():
    x, y, c = lax.axis_index("x"), lax.axis_index("y"), lax.axis_index("c")
    return x, y, c, 2 * x + y


def _chip(k, c):
    return (k >> 1, k & 1, c)


def _half(ref, h):
    n = ref.shape[0] // 2
    return ref.at[pl.ds(h * n, n)]


class _BackgroundGather:
    def __init__(self, ins, outs, scratch):
        n = self.n = len(ins)
        self.ins, self.outs = ins, outs
        self.mine, self.landed, self.passed = scratch[0:3 * n:3], scratch[1:3 * n:3], scratch[2:3 * n:3]
        self.send_sem, self.recv_sem, self.local_sem = scratch[3 * n:3 * n + 3]
        x, y, self.c, self.k = _place()
        self.sibling = (x, y, 1 - self.c)

    @staticmethod
    def scratch_shapes(shards):
        shapes = []
        for a in shards:
            half = (N_CHIP - 1, a.shape[0] // 2, a.shape[1])
            shapes += [pltpu.VMEM(a.shape, a.dtype), pltpu.VMEM(half, a.dtype), pltpu.VMEM(half, a.dtype)]
        n = len(shards)
        return shapes + [pltpu.SemaphoreType.DMA((6 * n,)), pltpu.SemaphoreType.DMA((6 * n,)),
                         pltpu.SemaphoreType.DMA((8 * n,))]

    @staticmethod
    def out_shapes(shards):
        return [jax.ShapeDtypeStruct((N_CHIP,) + a.shape, a.dtype) for a in shards]

    def _remote(self, a, slot, src, dst, to):
        return pltpu.make_async_remote_copy(src_ref=src, dst_ref=dst, send_sem=self.send_sem.at[6 * a + slot],
                                            recv_sem=self.recv_sem.at[6 * a + slot], device_id=to, device_id_type=MESH)

    def _local(self, a, slot, src, dst):
        return pltpu.make_async_copy(src, dst, self.local_sem.at[8 * a + slot])

    def _ici(self, a, j):
        return self._remote(a, j - 1, _half(self.mine[a], self.c), self.landed[a].at[j - 1], _chip(self.k ^ j, self.c))

    def _to_sibling(self, a, j):
        return self._remote(a, 2 + j, self.landed[a].at[j - 1], self.passed[a].at[j - 1], self.sibling)

    def _own(self, a):
        return self._local(a, 0, self.ins[a], self.outs[a].at[self.k])

    def _load(self, a):
        return self._local(a, 1, self.ins[a], self.mine[a])

    def _store_landed(self, a, j):
        return self._local(a, 1 + j, self.landed[a].at[j - 1], _half(self.outs[a].at[self.k ^ j], self.c))

    def _store_passed(self, a, j):
        return self._local(a, 4 + j, self.passed[a].at[j - 1], _half(self.outs[a].at[self.k ^ j], 1 - self.c))

    def start(self):
        for a in range(self.n):
            self._own(a).start()
            self._load(a).start()
        for a in range(self.n):
            self._load(a).wait()
            for j in range(1, N_CHIP):
                self._ici(a, j).start()

    def forward(self):
        for j in range(1, N_CHIP):
            for a in range(self.n):
                self._ici(a, j).wait_recv()
                self._to_sibling(a, j).start()
                self._store_landed(a, j).start()

    def finish(self):
        for j in range(1, N_CHIP):
            for a in range(self.n):
                self._to_sibling(a, j).wait_recv()
                self._store_passed(a, j).start()
        for a in range(self.n):
            for j in range(1, N_CHIP):
                self._ici(a, j).wait_send()
                self._to_sibling(a, j).wait_send()
                self._store_landed(a, j).wait()
                self._store_passed(a, j).wait()
            self._own(a).wait()


def _all_gather(shards):
    n = len(shards)
    any_spec = pl.BlockSpec(memory_space=pl.ANY)

    def body(*refs):
        gather = _BackgroundGather(refs[:n], refs[n:2 * n], refs[2 * n:])
        gather.start()
        gather.forward()
        gather.finish()

    return pl.pallas_call(
        body, name="weights_all_gather",
        in_specs=[any_spec] * n, out_specs=[any_spec] * n,
        out_shape=_BackgroundGather.out_shapes(shards),
        scratch_shapes=_BackgroundGather.scratch_shapes(shards),
        compiler_params=_params(32),
    )(*shards)


def _reduce_scatter(g, core, name):
    n, r, c = g.shape
    hr = r // 2
    once = pl.Buffered(1)
    in_specs = [pl.BlockSpec((n, hr, c), lambda i, core_ref: (0, core_ref[0], 0), pipeline_mode=once),
                pl.BlockSpec((n, hr, c), lambda i, core_ref: (0, 1 - core_ref[0], 0), pipeline_mode=once)]

    def body(core_ref, mine_ref, other_ref, out_ref, from_core, sums, sums_bf, from_chips, done, from_core2, send_sem, recv_sem):
        x, y, cc, k = _place()
        sibling = (x, y, 1 - cc)

        def copy(slot, src, dst, to):
            return pltpu.make_async_remote_copy(src_ref=src, dst_ref=dst, send_sem=send_sem.at[slot],
                                                recv_sem=recv_sem.at[slot], device_id=to, device_id_type=MESH)

        first = copy(0, other_ref, from_core, sibling)
        first.start()
        first.wait()
        total = mine_ref[...] + from_core[...]
        sums[...] = total
        sums_bf[...] = total.astype(BF16)
        sends = [copy(j, sums_bf.at[k ^ j], from_chips.at[j - 1], _chip(k ^ j, cc)) for j in range(1, N_CHIP)]
        for cp in sends:
            cp.start()
        for cp in sends:
            cp.wait()
        red = sums[k]
        for j in range(1, N_CHIP):
            red = red + from_chips[j - 1].astype(F32)
        done[...] = red
        last = copy(N_CHIP, done, from_core2, sibling)
        last.start()
        last.wait()
        row0 = pl.multiple_of(cc * hr, 8)
        row1 = pl.multiple_of((1 - cc) * hr, 8)
        out_ref[pl.ds(row0, hr), :] = red
        out_ref[pl.ds(row1, hr), :] = from_core2[...]

    grid_spec = pltpu.PrefetchScalarGridSpec(
        num_scalar_prefetch=1, grid=(1,), in_specs=in_specs,
        out_specs=pl.BlockSpec((r, c), lambda i, core_ref: (0, 0)),
        scratch_shapes=[pltpu.VMEM((n, hr, c), F32), pltpu.VMEM((n, hr, c), F32), pltpu.VMEM((n, hr, c), BF16),
                        pltpu.VMEM((N_CHIP - 1, hr, c), BF16), pltpu.VMEM((hr, c), F32), pltpu.VMEM((hr, c), F32),
                        pltpu.SemaphoreType.DMA((N_CHIP + 1,)), pltpu.SemaphoreType.DMA((N_CHIP + 1,))])
    return pl.pallas_call(
        body, name=name, grid_spec=grid_spec, out_shape=jax.ShapeDtypeStruct((r, c), F32),
        compiler_params=_params(56),
    )(core, g, g)


def _elementwise(fn, name, ins, n_out, rows):
    total, cols = ins[0].shape
    spec = pl.BlockSpec((rows, cols), lambda i: (i, 0))

    def body(*refs):
        res = fn(*[r[...] for r in refs[:len(ins)]])
        for o, v in zip(refs[len(ins):], res):
            o[...] = v

    return pl.pallas_call(
        body, name=name, grid=(total // rows,),
        in_specs=[spec] * len(ins), out_specs=[spec] * n_out,
        out_shape=[jax.ShapeDtypeStruct((total, cols), F32)] * n_out,
        compiler_params=_params(48),
    )(*ins)


def _adamw(w, g, m, v):
    m = ADAM_B1 * m + (1.0 - ADAM_B1) * g
    v = ADAM_B2 * v + (1.0 - ADAM_B2) * (g * g)
    m_hat = m / (1.0 - ADAM_B1 ** ADAM_STEP)
    v_hat = v / (1.0 - ADAM_B2 ** ADAM_STEP)
    delta = -ADAM_LR * (m_hat / (jnp.sqrt(v_hat) + ADAM_EPS) + ADAM_WD * w)
    return delta, m, v


def _reduce_and_update(grads, weights, moms, vels):
    core = lax.axis_index("c").astype(jnp.int32).reshape(1)
    full = [_reduce_scatter(g, core, f"grads_reduce_scatter_{a}") for a, g in enumerate(grads)]
    out = []
    for a, (g, w, m, v) in enumerate(zip(full, weights, moms, vels)):
        rows = g.shape[0] // 2
        out.append((g,) + tuple(_elementwise(lambda gg, ww, mm, vv: _adamw(ww, gg, mm, vv), f"adamw_{a}", [g, w, m, v], 3, rows)))
    return out


def _reduce_vectors(part, w, m, v):
    n_dev = 8

    def body(p_ref, w_ref, m_ref, v_ref, g_ref, d_ref, nm_ref, nv_ref, buf, send_sem, recv_sem):
        x, y, c, _ = _place()
        me = 4 * x + 2 * y + c
        buf[me] = p_ref[...]
        sends = []
        for off in range(1, n_dev):
            peer = me ^ off
            cp = pltpu.make_async_remote_copy(src_ref=p_ref, dst_ref=buf.at[me], send_sem=send_sem.at[off - 1],
                                              recv_sem=recv_sem.at[off - 1], device_id=(peer >> 2, (peer >> 1) & 1, peer & 1),
                                              device_id_type=MESH)
            cp.start()
            sends.append(cp)
        for off in range(1, n_dev):
            peer = me ^ off
            pltpu.make_async_remote_copy(src_ref=p_ref, dst_ref=buf.at[peer], send_sem=send_sem.at[off - 1],
                                         recv_sem=recv_sem.at[off - 1], device_id=(peer >> 2, (peer >> 1) & 1, peer & 1),
                                         device_id_type=MESH).wait_recv()
        for cp in sends:
            cp.wait_send()
        g = buf[0]
        for d in range(1, n_dev):
            g = g + buf[d]
        g_ref[...] = g
        delta, nm, nv = _adamw(w_ref[...], g, m_ref[...], v_ref[...])
        d_ref[...] = delta
        nm_ref[...] = nm
        nv_ref[...] = nv

    vm = pl.BlockSpec(memory_space=pltpu.VMEM)
    return pl.pallas_call(
        body, name="gains_all_reduce",
        in_specs=[vm] * 4, out_specs=[vm] * 4,
        out_shape=[jax.ShapeDtypeStruct(part.shape, F32)] * 4,
        scratch_shapes=[pltpu.VMEM((n_dev,) + part.shape, F32), pltpu.SemaphoreType.DMA((n_dev - 1,)),
                        pltpu.SemaphoreType.DMA((n_dev - 1,))],
    )(part, w, m, v)


def _pad_row(a):
    a = a.reshape(1, -1)
    return jnp.pad(a, ((0, 0), (0, D_MODEL - a.shape[1])))


def kernel(x, ffn1_norm, ffn1_w_gate, ffn1_w_up, ffn1_w_down, mix_norm, w_in, sb_out_norm, dil_out_norm, w_out, ffn2_norm, ffn2_w_gate, ffn2_w_up, ffn2_w_down, final_norm, loss_target, m_ffn1_norm, m_ffn1_w_gate, m_ffn1_w_up, m_ffn1_w_down, m_mix_norm, m_w_in, m_sb_out_norm, m_dil_out_norm, m_w_out, m_ffn2_norm, m_ffn2_w_gate, m_ffn2_w_up, m_ffn2_w_down, m_final_norm, v_ffn1_norm, v_ffn1_w_gate, v_ffn1_w_up, v_ffn1_w_down, v_mix_norm, v_w_in, v_sb_out_norm, v_dil_out_norm, v_w_out, v_ffn2_norm, v_ffn2_w_gate, v_ffn2_w_up, v_ffn2_w_down, v_final_norm):
    x = x[0]
    target = loss_target[0]
    s = x.shape[0]
    gf = final_norm.reshape(1, D_MODEL)
    cos, sin = _rope_tables(s)

    shard = lambda w: w[0].astype(BF16)
    wg1, wu1, wd1 = _all_gather([shard(ffn1_w_gate), shard(ffn1_w_up), shard(ffn1_w_down)])

    x1, hm, saved1, (win, wout, wd2) = _ffn1_fwd(x, ffn1_norm, mix_norm, (wg1, wu1), wd1,
                                                 [shard(w_in), shard(w_out), shard(ffn2_w_down)])
    wout = wout.reshape(D_MODEL, D_MODEL)
    qkv, (wg2, wu2) = _proj_fwd(hm, win, cos, sin, [shard(ffn2_w_gate), shard(ffn2_w_up)])
    o_sb = _sb_fwd(qkv)
    o_dl, lse = _dilated_fwd(qkv)
    x2 = _outproj_fwd(o_sb, o_dl, sb_out_norm, dil_out_norm, x1, wout)
    dx3, st_final, saved2 = _ffn2_fwd_loss(x2, ffn2_norm, gf, target, (wg2, wu2), wd2)

    dx2, dwg2, dwu2, dwd2, st_ffn2 = _ffn_bwd(x2, ffn2_norm, dx3, saved2, (wg2, wu2), wd2, 1)
    do_sb, do_dl, delta_dl, dwout, st_out = _outproj_bwd(dx2, o_sb, o_dl, sb_out_norm, dil_out_norm, wout)
    dq_sb, dk_sb, dv_sb = _sb_bwd(qkv, o_sb, do_sb)
    dq_dl, dk_dl, dv_dl = _dilated_finish(_dilated_bwd(qkv, delta_dl, lse, do_dl), cos, sin)
    dqkv = jnp.concatenate([dq_sb, dk_sb, dv_sb, dq_dl, dk_dl, dv_dl], axis=1)
    dx1, dwin, st_mix = _proj_bwd(x1, mix_norm, dqkv, win, dx2)
    grad_x, dwg1, dwu1, dwd1, st_ffn1 = _ffn_bwd(x, ffn1_norm, dx1, saved1, (wg1, wu1), wd1, 0)

    names = ["ffn1_w_gate", "ffn1_w_up", "ffn1_w_down", "w_in", "w_out", "ffn2_w_gate", "ffn2_w_up", "ffn2_w_down"]
    grads = [dwg1, dwu1, dwd1, dwin, dwout.reshape(N_CHIP, OUTB, D_MODEL), dwg2, dwu2, dwd2]
    weights = [ffn1_w_gate[0], ffn1_w_up[0], ffn1_w_down[0], w_in[0], w_out[0], ffn2_w_gate[0], ffn2_w_up[0], ffn2_w_down[0]]
    moms = [m_ffn1_w_gate[0], m_ffn1_w_up[0], m_ffn1_w_down[0], m_w_in[0], m_w_out[0], m_ffn2_w_gate[0], m_ffn2_w_up[0], m_ffn2_w_down[0]]
    vels = [v_ffn1_w_gate[0], v_ffn1_w_up[0], v_ffn1_w_down[0], v_w_in[0], v_w_out[0], v_ffn2_w_gate[0], v_ffn2_w_up[0], v_ffn2_w_down[0]]
    mats = {n: tuple(t[None] for t in r) for n, r in zip(names, _reduce_and_update(grads, weights, moms, vels))}

    vec_names = ["ffn1_norm", "mix_norm", "sb_out_norm", "dil_out_norm", "ffn2_norm", "final_norm"]
    part = jnp.concatenate([st_ffn1[0:1], st_mix[0:1], _pad_row(st_out[0]), _pad_row(st_out[1]), st_ffn2[0:1],
                            st_final[0:1], st_final[1:2], jnp.zeros((1, D_MODEL), F32)], axis=0)
    pack = lambda arrs: jnp.concatenate([_pad_row(a) for a in arrs] + [jnp.zeros((2, D_MODEL), F32)], axis=0)
    g_vec, d_vec, m_vec, v_vec = _reduce_vectors(
        part,
        pack([ffn1_norm, mix_norm, sb_out_norm, dil_out_norm, ffn2_norm, final_norm]),
        pack([m_ffn1_norm, m_mix_norm, m_sb_out_norm, m_dil_out_norm, m_ffn2_norm, m_final_norm]),
        pack([v_ffn1_norm, v_mix_norm, v_sb_out_norm, v_dil_out_norm, v_ffn2_norm, v_final_norm]))
    like = {"ffn1_norm": ffn1_norm, "mix_norm": mix_norm, "sb_out_norm": sb_out_norm, "dil_out_norm": dil_out_norm,
            "ffn2_norm": ffn2_norm, "final_norm": final_norm}
    vecs = {n: tuple(t[i, :like[n].size].reshape(like[n].shape) for t in (g_vec, d_vec, m_vec, v_vec))
            for i, n in enumerate(vec_names)}
    loss = 0.5 * jnp.sum(g_vec[6]) / D_MODEL

    order = ["ffn1_norm", "ffn1_w_gate", "ffn1_w_up", "ffn1_w_down", "mix_norm", "w_in", "sb_out_norm", "dil_out_norm",
             "w_out", "ffn2_norm", "ffn2_w_gate", "ffn2_w_up", "ffn2_w_down", "final_norm"]
    both = {**mats, **vecs}
    return (loss, grad_x[None], *[both[n][0] for n in order], *[both[n][1] for n in order],
            *[both[n][2] for n in order], *[both[n][3] for n in order])
```

```python
import functools

import jax
import jax.numpy as jnp
from jax import lax
from jax.experimental import pallas as pl
from jax.experimental.pallas import tpu as pltpu

D_MODEL = 1024
D_FF = 2816
HEAD_DIM = 64
D_SB = 512
D_DIL = 512
D_IN = 3072
N_CHIP = 4
FFB = D_FF // N_CHIP
INB = D_IN // N_CHIP
OUTB = D_MODEL // N_CHIP
BLK = 128
LANES = 128
DILATIONS = (1, 4, 16)
ROPE_THETA = 10000.0
RMS_EPS = 1e-6
SCALE = HEAD_DIM ** -0.5
NEG = -1e30
DEAD = -104.0
ADAM_LR = 0.001
ADAM_B1 = 0.9
ADAM_B2 = 0.999
ADAM_EPS = 1e-08
ADAM_WD = 0.01
ADAM_STEP = 10
MESH = pl.DeviceIdType.MESH
F32 = jnp.float32
BF16 = jnp.bfloat16
TM = 512


def _params(vmem_mb):
    return pltpu.CompilerParams(vmem_limit_bytes=vmem_mb << 20)


def _dot(a, b):
    return jnp.dot(a, b, preferred_element_type=F32)


def _dot_nt(a, b):
    return lax.dot_general(a, b, (((1,), (1,)), ((), ())), preferred_element_type=F32)


def _dot_tn(a, b):
    return lax.dot_general(a, b, (((0,), (0,)), ((), ())), preferred_element_type=F32)


def _rms_fwd(x, g):
    r = lax.rsqrt(jnp.mean(x * x, axis=-1, keepdims=True) + RMS_EPS)
    xh = x * r
    return xh * g, xh, r


def _rms_bwd(dy, xh, r, g):
    dyg = dy * g
    dx = r * (dyg - xh * jnp.mean(dyg * xh, axis=-1, keepdims=True))
    return dx, jnp.sum(dy * xh, axis=0, keepdims=True)


def _split_bf16(a):
    hi = a.astype(BF16)
    return hi, (a - hi.astype(F32)).astype(BF16)


def _dot_split(a, b2):
    hi, lo = _split_bf16(a)
    return _dot(jnp.concatenate([hi, lo], axis=1), b2)


def _ffn_weight_specs():
    return [pl.BlockSpec((None, D_MODEL, FFB), lambda i, j: (j, 0, 0)),
            pl.BlockSpec((None, D_MODEL, FFB), lambda i, j: (j, 0, 0)),
            pl.BlockSpec((None, FFB, D_MODEL), lambda i, j: (j, 0, 0))]


def _ffn_saved(s):
    hidden = jax.ShapeDtypeStruct((N_CHIP, s, FFB), BF16)
    hid = pl.BlockSpec((None, TM, FFB), lambda i, j: (j, i, 0))
    row = pl.BlockSpec((TM, D_MODEL), lambda i, j: (i, 0))
    return [row, hid, hid, hid], [jax.ShapeDtypeStruct((s, D_MODEL), BF16), hidden, hidden, hidden]


def _ffn_accumulate(h_ref, acc_scr, wg_ref, wu_ref, wd_ref, a_ref, b_ref, act_ref):
    h = h_ref[...]
    a = _dot(h, wg_ref[...])
    b = _dot(h, wu_ref[...])
    act = ((a * jax.nn.sigmoid(a)) * b).astype(BF16)
    a_ref[...] = a.astype(BF16)
    b_ref[...] = b.astype(BF16)
    act_ref[...] = act
    acc_scr[...] += _dot(act, wd_ref[...])


def _host_gather_before(gather, i, j, steps):
    @pl.when((i == 0) & (j == 0))
    def _():
        gather.start()

    @pl.when((i == (3 * steps) // 4) & (j == 0))
    def _():
        gather.forward()


def _host_gather_after(gather, i, j, steps):
    @pl.when((i == steps - 1) & (j == N_CHIP - 1))
    def _():
        gather.finish()


def _ffn1_fwd(x, g1, gmix, gu, wd, later_shards):
    s = x.shape[0]
    row = pl.BlockSpec((TM, D_MODEL), lambda i, j: (i, 0))
    vec = pl.BlockSpec((1, D_MODEL), lambda i, j: (0, 0))
    saved_specs, saved_shapes = _ffn_saved(s)
    n = len(later_shards)
    any_spec = pl.BlockSpec(memory_space=pl.ANY)

    def body(*refs):
        x_ref, g_ref, gm_ref, wg_ref, wu_ref, wd_ref = refs[:6]
        shard_refs, refs = refs[6:6 + n], refs[6 + n:]
        x1_ref, hm_ref, h_ref, a_ref, b_ref, act_ref = refs[:6]
        gathered_refs, acc_scr, gather_scratch = refs[6:6 + n], refs[6 + n], refs[7 + n:]
        gather = _BackgroundGather(shard_refs, gathered_refs, gather_scratch)
        i, j = pl.program_id(0), pl.program_id(1)
        _host_gather_before(gather, i, j, s // TM)

        @pl.when(j == 0)
        def _():
            h, _, _ = _rms_fwd(x_ref[...], g_ref[...])
            h_ref[...] = h.astype(BF16)
            acc_scr[...] = jnp.zeros_like(acc_scr)

        _ffn_accumulate(h_ref, acc_scr, wg_ref, wu_ref, wd_ref, a_ref, b_ref, act_ref)

        @pl.when(j == N_CHIP - 1)
        def _():
            x1 = x_ref[...] + 0.5 * acc_scr[...]
            x1_ref[...] = x1
            hm, _, _ = _rms_fwd(x1, gm_ref[...])
            hm_ref[...] = hm.astype(BF16)

        _host_gather_after(gather, i, j, s // TM)

    x1, hm, h, a, b, act, *gathered = pl.pallas_call(
        body, name="ffn1_fwd", grid=(s // TM, N_CHIP),
        in_specs=[row, vec, vec] + _ffn_weight_specs() + [any_spec] * n,
        out_specs=[row, row] + saved_specs + [any_spec] * n,
        out_shape=([jax.ShapeDtypeStruct((s, D_MODEL), F32), jax.ShapeDtypeStruct((s, D_MODEL), BF16)] + saved_shapes
                   + _BackgroundGather.out_shapes(later_shards)),
        scratch_shapes=[pltpu.VMEM((TM, D_MODEL), F32)] + _BackgroundGather.scratch_shapes(later_shards),
        compiler_params=_params(58),
    )(x, g1, gmix, gu[0], gu[1], wd, *later_shards)
    return x1, hm, [h, a, b, act], gathered


def _ffn2_fwd_loss(x2, g2, gf, target, gu, wd):
    s = x2.shape[0]
    row = pl.BlockSpec((TM, D_MODEL), lambda i, j: (i, 0))
    vec = pl.BlockSpec((1, D_MODEL), lambda i, j: (0, 0))
    stat = pl.BlockSpec((8, D_MODEL), lambda i, j: (0, 0))
    saved_specs, saved_shapes = _ffn_saved(s)

    def body(x_ref, g_ref, gf_ref, t_ref, wg_ref, wu_ref, wd_ref, dx_ref, st_ref, h_ref, a_ref, b_ref, act_ref, acc_scr):
        i, j = pl.program_id(0), pl.program_id(1)

        @pl.when((i == 0) & (j == 0))
        def _():
            st_ref[...] = jnp.zeros_like(st_ref)

        @pl.when(j == 0)
        def _():
            h, _, _ = _rms_fwd(x_ref[...], g_ref[...])
            h_ref[...] = h.astype(BF16)
            acc_scr[...] = jnp.zeros_like(acc_scr)

        _ffn_accumulate(h_ref, acc_scr, wg_ref, wu_ref, wd_ref, a_ref, b_ref, act_ref)

        @pl.when(j == N_CHIP - 1)
        def _():
            x3 = x_ref[...] + 0.5 * acc_scr[...]
            y, xh, r = _rms_fwd(x3, gf_ref[...])
            err = y - t_ref[...]
            dx, dg = _rms_bwd(err * (1.0 / D_MODEL), xh, r, gf_ref[...])
            dx_ref[...] = dx
            st_ref[0:1, :] += dg
            st_ref[1:2, :] += jnp.sum(err * err, axis=0, keepdims=True)

    dx3, st, *saved = pl.pallas_call(
        body, name="ffn2_fwd_loss", grid=(s // TM, N_CHIP),
        in_specs=[row, vec, vec, row] + _ffn_weight_specs(),
        out_specs=[row, stat] + saved_specs,
        out_shape=[jax.ShapeDtypeStruct((s, D_MODEL), F32), jax.ShapeDtypeStruct((8, D_MODEL), F32)] + saved_shapes,
        scratch_shapes=[pltpu.VMEM((TM, D_MODEL), F32)],
        compiler_params=_params(56),
    )(x2, g2, gf, target, gu[0], gu[1], wd)
    return dx3, st, saved


def _ffn_bwd(xin, g, dy, saved, gu, wd, f):
    s = xin.shape[0]
    hb, gate, up, act = saved
    row = pl.BlockSpec((TM, D_MODEL), lambda i, j: (i, 0))
    vec = pl.BlockSpec((1, D_MODEL), lambda i, j: (0, 0))
    stat = pl.BlockSpec((8, D_MODEL), lambda i, j: (0, 0))
    hid = pl.BlockSpec((None, TM, FFB), lambda i, j: (j, i, 0))

    def body(x_ref, g_ref, dy_ref, a_ref, b_ref, wg_ref, wu_ref, wd_ref, out_ref, dyh_ref, da_ref, db_ref, st_ref, dh_scr):
        i, j = pl.program_id(0), pl.program_id(1)

        @pl.when((i == 0) & (j == 0))
        def _():
            st_ref[...] = jnp.zeros_like(st_ref)

        @pl.when(j == 0)
        def _():
            dyh_ref[...] = (0.5 * dy_ref[...]).astype(BF16)
            dh_scr[...] = jnp.zeros_like(dh_scr)

        a = a_ref[...].astype(F32)
        b = b_ref[...].astype(F32)
        sg = jax.nn.sigmoid(a)
        dact = _dot_nt(dyh_ref[...], wd_ref[...])
        dab = (dact * b * (sg * (1.0 + a * (1.0 - sg)))).astype(BF16)
        dbb = (dact * (a * sg)).astype(BF16)
        da_ref[...] = dab
        db_ref[...] = dbb
        dh_scr[...] += _dot_nt(dab, wg_ref[...]) + _dot_nt(dbb, wu_ref[...])

        @pl.when(j == N_CHIP - 1)
        def _():
            _, xh, r = _rms_fwd(x_ref[...], g_ref[...])
            dx, dg = _rms_bwd(dh_scr[...], xh, r, g_ref[...])
            out_ref[...] = dy_ref[...] + dx
            st_ref[0:1, :] += dg

    hidden = jax.ShapeDtypeStruct((N_CHIP, s, FFB), BF16)
    dx, dyh, da, db, st = pl.pallas_call(
        body, name=f"ffn{f + 1}_bwd_dx", grid=(s // TM, N_CHIP),
        in_specs=[row, vec, row, hid, hid] + _ffn_weight_specs(),
        out_specs=[row, row, hid, hid, stat],
        out_shape=[jax.ShapeDtypeStruct((s, D_MODEL), F32), jax.ShapeDtypeStruct((s, D_MODEL), BF16),
                   hidden, hidden, jax.ShapeDtypeStruct((8, D_MODEL), F32)],
        scratch_shapes=[pltpu.VMEM((TM, D_MODEL), F32)],
        compiler_params=_params(56),
    )(xin, g, dy, gate, up, gu[0], gu[1], wd)

    tok = pl.BlockSpec((TM, D_MODEL), lambda j, i: (i, 0))
    hid2 = pl.BlockSpec((None, TM, FFB), lambda j, i: (j, i, 0))
    gspecs = [pl.BlockSpec((None, D_MODEL, FFB), lambda j, i: (j, 0, 0)),
              pl.BlockSpec((None, D_MODEL, FFB), lambda j, i: (j, 0, 0)),
              pl.BlockSpec((None, FFB, D_MODEL), lambda j, i: (j, 0, 0))]

    def wbody(h_ref, dyh_ref, da_ref, db_ref, act_ref, dwg_ref, dwu_ref, dwd_ref):
        @pl.when(pl.program_id(1) == 0)
        def _():
            dwg_ref[...] = jnp.zeros_like(dwg_ref)
            dwu_ref[...] = jnp.zeros_like(dwu_ref)
            dwd_ref[...] = jnp.zeros_like(dwd_ref)

        hb = h_ref[...]
        dwg_ref[...] += _dot_tn(hb, da_ref[...])
        dwu_ref[...] += _dot_tn(hb, db_ref[...])
        dwd_ref[...] += _dot_tn(act_ref[...], dyh_ref[...])

    dwg, dwu, dwd = pl.pallas_call(
        wbody, name=f"ffn{f + 1}_bwd_dw", grid=(N_CHIP, s // TM),
        in_specs=[tok, tok, hid2, hid2, hid2], out_specs=gspecs,
        out_shape=[jax.ShapeDtypeStruct((N_CHIP, D_MODEL, FFB), F32),
                   jax.ShapeDtypeStruct((N_CHIP, D_MODEL, FFB), F32),
                   jax.ShapeDtypeStruct((N_CHIP, FFB, D_MODEL), F32)],
        compiler_params=_params(48),
    )(hb, dyh, da, db, act)
    return dx, dwg, dwu, dwd, st


def _rope_tables(s):
    half = HEAD_DIM // 2
    inv_freq = ROPE_THETA ** (-jnp.arange(half, dtype=F32) / half)
    ang = jnp.arange(s).astype(F32)[:, None] * inv_freq[None, :]
    cos, sin = jnp.cos(ang), jnp.sin(ang)
    cos2 = jnp.concatenate([cos, cos], axis=-1)
    sin2 = jnp.concatenate([-sin, sin], axis=-1)
    return jnp.tile(cos2, (1, LANES // HEAD_DIM)), jnp.tile(sin2, (1, LANES // HEAD_DIM))


def _rotate(t, cos, sin_signed):
    lane = lax.broadcasted_iota(jnp.int32, t.shape, 1)
    first = (lane % HEAD_DIM) < (HEAD_DIM // 2)
    partner = jnp.where(first, pltpu.roll(t, LANES - HEAD_DIM // 2, 1), pltpu.roll(t, HEAD_DIM // 2, 1))
    return t * cos + partner * sin_signed


def _proj_fwd(hm, win, cos, sin, later_shards):
    s = hm.shape[0]
    n_sub = INB // LANES
    first_rot, last_rot = (3 * D_SB) // LANES, (3 * D_SB + 2 * D_DIL) // LANES
    n = len(later_shards)
    any_spec = pl.BlockSpec(memory_space=pl.ANY)

    def body(*refs):
        h_ref, w_ref, c_ref, s_ref = refs[:4]
        shard_refs, o_ref, gathered_refs, gather_scratch = refs[4:4 + n], refs[4 + n], refs[5 + n:5 + 2 * n], refs[5 + 2 * n:]
        gather = _BackgroundGather(shard_refs, gathered_refs, gather_scratch)
        i, j = pl.program_id(0), pl.program_id(1)
        _host_gather_before(gather, i, j, s // TM)
        r = _dot(h_ref[...], w_ref[...])
        for c in range(n_sub):
            t = r[:, c * LANES:(c + 1) * LANES]
            col = j * n_sub + c
            rot = (col >= first_rot) & (col < last_rot)
            lanes = slice(c * LANES, (c + 1) * LANES)

            @pl.when(rot)
            def _():
                o_ref[:, lanes] = _rotate(t, c_ref[...], s_ref[...]).astype(BF16)

            @pl.when(jnp.logical_not(rot))
            def _():
                o_ref[:, lanes] = t.astype(BF16)

        _host_gather_after(gather, i, j, s // TM)

    qkv, *gathered = pl.pallas_call(
        body, name="proj_fwd", grid=(s // TM, N_CHIP),
        in_specs=[pl.BlockSpec((TM, D_MODEL), lambda i, j: (i, 0)),
                  pl.BlockSpec((None, D_MODEL, INB), lambda i, j: (j, 0, 0)),
                  pl.BlockSpec((TM, LANES), lambda i, j: (i, 0)),
                  pl.BlockSpec((TM, LANES), lambda i, j: (i, 0))] + [any_spec] * n,
        out_specs=[pl.BlockSpec((TM, INB), lambda i, j: (i, j))] + [any_spec] * n,
        out_shape=[jax.ShapeDtypeStruct((s, D_IN), BF16)] + _BackgroundGather.out_shapes(later_shards),
        scratch_shapes=_BackgroundGather.scratch_shapes(later_shards),
        compiler_params=_params(48),
    )(hm, win, cos, sin, *later_shards)
    return qkv, gathered


def _proj_bwd(x1, gmix, dqkv, win, dx2):
    s = x1.shape[0]
    row = pl.BlockSpec((TM, D_MODEL), lambda i, j: (i, 0))
    vec = pl.BlockSpec((1, D_MODEL), lambda i, j: (0, 0))

    def body(x_ref, g_ref, dq_ref, w_ref, dx2_ref, out_ref, dw_ref, st_ref, h_scr, dh_scr):
        i, j = pl.program_id(0), pl.program_id(1)

        @pl.when((i == 0) & (j == 0))
        def _():
            st_ref[...] = jnp.zeros_like(st_ref)
            dw_ref[...] = jnp.zeros_like(dw_ref)

        @pl.when(j == 0)
        def _():
            h, _, _ = _rms_fwd(x_ref[...], g_ref[...])
            h_scr[...] = h.astype(BF16)
            dh_scr[...] = jnp.zeros_like(dh_scr)

        dq = dq_ref[...]
        dw_ref[j] += _dot_tn(h_scr[...], dq)
        dh_scr[...] += _dot_nt(dq, w_ref[...])

        @pl.when(j == N_CHIP - 1)
        def _():
            _, xh, r = _rms_fwd(x_ref[...], g_ref[...])
            dx, dg = _rms_bwd(dh_scr[...], xh, r, g_ref[...])
            out_ref[...] = dx2_ref[...] + dx
            st_ref[0:1, :] += dg

    return pl.pallas_call(
        body, name="proj_bwd", grid=(s // TM, N_CHIP),
        in_specs=[row, vec, pl.BlockSpec((TM, INB), lambda i, j: (i, j)),
                  pl.BlockSpec((None, D_MODEL, INB), lambda i, j: (j, 0, 0)), row],
        out_specs=[row, pl.BlockSpec((N_CHIP, D_MODEL, INB), lambda i, j: (0, 0, 0)),
                   pl.BlockSpec((8, D_MODEL), lambda i, j: (0, 0))],
        out_shape=[jax.ShapeDtypeStruct((s, D_MODEL), F32),
                   jax.ShapeDtypeStruct((N_CHIP, D_MODEL, INB), F32),
                   jax.ShapeDtypeStruct((8, D_MODEL), F32)],
        scratch_shapes=[pltpu.VMEM((TM, D_MODEL), BF16), pltpu.VMEM((TM, D_MODEL), F32)],
        compiler_params=_params(56),
    )(x1, gmix, dqkv, win, dx2)


def _outproj_fwd(o_sb, o_dl, g_sb, g_dl, x1, wout):
    s = x1.shape[0]
    half = pl.BlockSpec((TM, D_SB), lambda i: (i, 0))
    row = pl.BlockSpec((TM, D_MODEL), lambda i: (i, 0))
    vec = pl.BlockSpec((1, D_SB), lambda i: (0, 0))

    def body(a_ref, b_ref, ga_ref, gb_ref, x_ref, w_ref, o_ref):
        ma, _, _ = _rms_fwd(a_ref[...], ga_ref[...])
        mb, _, _ = _rms_fwd(b_ref[...], gb_ref[...])
        o_ref[...] = (x_ref[...] + _dot(ma.astype(BF16), w_ref[0:D_SB, :])
                      + _dot(mb.astype(BF16), w_ref[D_SB:D_MODEL, :]))

    return pl.pallas_call(
        body, name="outproj_fwd", grid=(s // TM,),
        in_specs=[half, half, vec, vec, row, pl.BlockSpec((D_MODEL, D_MODEL), lambda i: (0, 0))],
        out_specs=row, out_shape=jax.ShapeDtypeStruct((s, D_MODEL), F32),
        compiler_params=_params(32),
    )(o_sb, o_dl, g_sb, g_dl, x1, wout)


def _outproj_bwd(dx2, o_sb, o_dl, g_sb, g_dl, wout):
    s = dx2.shape[0]
    half = pl.BlockSpec((TM, D_SB), lambda i: (i, 0))
    row = pl.BlockSpec((TM, D_MODEL), lambda i: (i, 0))
    vec = pl.BlockSpec((1, D_SB), lambda i: (0, 0))
    full = pl.BlockSpec((D_MODEL, D_MODEL), lambda i: (0, 0))

    def body(dy_ref, a_ref, b_ref, ga_ref, gb_ref, w_ref, da_ref, db_ref, dl_ref, dw_ref, st_ref):
        @pl.when(pl.program_id(0) == 0)
        def _():
            dw_ref[...] = jnp.zeros_like(dw_ref)
            st_ref[...] = jnp.zeros_like(st_ref)

        dy = dy_ref[...].astype(BF16)
        dm = _dot_nt(dy, w_ref[...])
        ma, xa, ra = _rms_fwd(a_ref[...], ga_ref[...])
        mb, xb, rb = _rms_fwd(b_ref[...], gb_ref[...])
        dw_ref[0:D_SB, :] += _dot_tn(ma.astype(BF16), dy)
        dw_ref[D_SB:D_MODEL, :] += _dot_tn(mb.astype(BF16), dy)
        da, dga = _rms_bwd(dm[:, 0:D_SB], xa, ra, ga_ref[...])
        db, dgb = _rms_bwd(dm[:, D_SB:D_MODEL], xb, rb, gb_ref[...])
        da_ref[...] = da
        db_ref[...] = db
        r = lax.broadcasted_iota(jnp.int32, (LANES, LANES), 0) >= HEAD_DIM
        c = lax.broadcasted_iota(jnp.int32, (LANES, LANES), 1) >= HEAD_DIM
        same_head = jnp.where(r == c, 1.0, 0.0).astype(BF16)
        same_head = jnp.concatenate([same_head, same_head], axis=0)
        prod = db * b_ref[...]
        for k in range(D_DIL // LANES):
            lanes = slice(k * LANES, (k + 1) * LANES)
            dl_ref[:, lanes] = _dot_split(prod[:, lanes], same_head)
        st_ref[0:1, :] += dga
        st_ref[1:2, :] += dgb

    return pl.pallas_call(
        body, name="outproj_bwd", grid=(s // TM,),
        in_specs=[row, half, half, vec, vec, full],
        out_specs=[half, half, half, full, pl.BlockSpec((8, D_SB), lambda i: (0, 0))],
        out_shape=[jax.ShapeDtypeStruct((s, D_SB), F32), jax.ShapeDtypeStruct((s, D_SB), F32),
                   jax.ShapeDtypeStruct((s, D_DIL), F32),
                   jax.ShapeDtypeStruct((D_MODEL, D_MODEL), F32), jax.ShapeDtypeStruct((8, D_SB), F32)],
        compiler_params=_params(48),
    )(dx2, o_sb, o_dl, g_sb, g_dl, wout)


def _head_masks():
    lane = lax.broadcasted_iota(jnp.int32, (BLK, LANES), 1)
    return [lane < HEAD_DIM, lane >= HEAD_DIM]


def _keep(mask, a):
    return a * jnp.where(mask, 1.0, 0.0).astype(a.dtype)


def _suffix_matrices():
    r = lax.broadcasted_iota(jnp.int32, (2 * BLK, BLK), 0) & (BLK - 1)
    c = lax.broadcasted_iota(jnp.int32, (2 * BLK, BLK), 1)
    ones = jnp.ones((2 * BLK, BLK), BF16)
    excl = jnp.concatenate([(r > c).astype(BF16), ones], axis=1)
    incl = jnp.concatenate([(r >= c).astype(BF16), ones], axis=1)
    return excl, incl


def _blk(i):
    return pl.ds(pl.multiple_of(i * BLK, BLK), BLK)


def _alive(carry_m):
    return (jnp.max(carry_m) > DEAD).astype(jnp.int32)


def _more_keys(last, carry):
    return (carry[0] * SB_KB <= last) & (carry[1] > 0)


def _stack_heads(a):
    masks = _head_masks()
    return jnp.concatenate([_keep(masks[0], a), _keep(masks[1], a)], axis=0)


def _unstack_heads(a2):
    return jnp.where(_head_masks()[0], a2[:BLK], a2[BLK:])


def _head_rowsum(a):
    masks = _head_masks()
    return jnp.concatenate([jnp.sum(jnp.where(m, a, 0.0), axis=1, keepdims=True) for m in masks], axis=0)


SB_QB = 2
SB_ROWS = SB_QB * 2 * BLK
SB_KB = 4
PAST_START = 1 << 30


def _sb_rows(ref, i0, cast=None):
    tiles = [ref[_blk(i0 + t), :] for t in range(SB_QB)]
    return jnp.concatenate([_stack_heads(t if cast is None else t.astype(cast)) for t in tiles], axis=0)


_SB_LATER_ROWS = (SB_QB - 1) * 2 * BLK


def _put_rows(full, rows, part):
    return part if rows.start == 0 else jnp.concatenate([full[:rows.start], part], axis=0)


def _sb_scores(q2, k, i, j, carry_m, u_excl):
    r = lax.broadcasted_iota(jnp.int32, (q2.shape[0], BLK), 0)
    row = (r & (BLK - 1)) + ((r >> 8) << 7)
    col = lax.broadcasted_iota(jnp.int32, (q2.shape[0], BLK), 1)
    valid = (jnp.where(j >= 0, j * BLK, PAST_START) + col) < (i * BLK + row)
    z = _dot_nt(q2, k) * SCALE
    sp = jnp.maximum(z, 0.0) + jnp.log(1.0 + jnp.exp(-jnp.abs(z)))
    log_stay = jnp.where(valid, -sp, 0.0)
    log_beta = z - sp
    sums = _dot_split(log_stay, u_excl)
    later = carry_m + sums[:, :BLK]
    w = jnp.where(valid, jnp.exp(log_beta + later), 0.0)
    return valid, log_beta, w, carry_m + sums[:, BLK:]


def _sb_fwd(qkv):
    s = qkv.shape[0]
    nq = s // BLK
    pairs = D_SB // LANES
    col = lambda off: pl.BlockSpec((s, LANES), lambda p: (0, off + p))

    def body(q_ref, k_ref, v_ref, o_ref):
        u_excl, _ = _suffix_matrices()
        zero = jnp.zeros((SB_ROWS, LANES), F32)

        def q_block(ib, _):
            i = ib * SB_QB
            last = i + SB_QB - 1
            q2 = _sb_rows(q_ref, i)

            def trip(jj, carry_m, acc, first):
                for t in range(SB_KB):
                    j = last - jj * SB_KB - t
                    at = _blk(jnp.maximum(j, 0))
                    rows = slice(_SB_LATER_ROWS, SB_ROWS) if first and t == 0 else slice(0, SB_ROWS)
                    base = i + rows.start // (2 * BLK)
                    _, _, w, part = _sb_scores(q2[rows], k_ref[at, :], base, j, carry_m[rows], u_excl)
                    carry_m = _put_rows(carry_m, rows, part)
                    acc = _put_rows(acc, rows, acc[rows] + _dot(w.astype(BF16), v_ref[at, :]))
                return carry_m, acc

            def k_block(carry):
                carry_m, acc = trip(carry[0], carry[2], carry[3], False)
                return carry[0] + 1, _alive(carry_m), carry_m, acc

            carry_m, acc = trip(0, zero, zero, True)
            _, _, _, acc = lax.while_loop(functools.partial(_more_keys, last), k_block,
                                          (jnp.int32(1), _alive(carry_m), carry_m, acc))
            for t in range(SB_QB):
                o_ref[_blk(i + t), :] = _unstack_heads(acc[2 * BLK * t:2 * BLK * (t + 1)])
            return 0

        lax.fori_loop(0, nq // SB_QB, q_block, 0)

    return pl.pallas_call(
        body, name="sb_fwd", grid=(pairs,),
        in_specs=[col(0), col(pairs), col(2 * pairs)],
        out_specs=pl.BlockSpec((s, LANES), lambda p: (0, p)),
        out_shape=jax.ShapeDtypeStruct((s, D_SB), F32),
        compiler_params=_params(48),
    )(qkv, qkv, qkv)


def _sb_bwd(qkv, o_sb, do_sb):
    s = qkv.shape[0]
    nq = s // BLK
    pairs = D_SB // LANES
    col = lambda off: pl.BlockSpec((s, LANES), lambda p: (0, off + p))
    own = pl.BlockSpec((s, LANES), lambda p: (0, p))

    def body(q_ref, k_ref, v_ref, o_ref, do_ref, dq_ref, dk_ref, dv_ref, dk_acc, dv_acc):
        u_excl, u_incl = _suffix_matrices()
        zero = jnp.zeros((SB_ROWS, LANES), F32)
        dk_acc[...] = jnp.zeros_like(dk_acc)
        dv_acc[...] = jnp.zeros_like(dv_acc)

        def q_block(ib, _):
            i = ib * SB_QB
            last = i + SB_QB - 1
            q2 = _sb_rows(q_ref, i)
            do2 = _sb_rows(do_ref, i, BF16)
            totals = [_head_rowsum(do_ref[_blk(i + t), :].astype(BF16).astype(F32) * o_ref[_blk(i + t), :])
                      for t in range(SB_QB)]
            total = jnp.broadcast_to(jnp.concatenate(totals, axis=0), (SB_ROWS, BLK))

            def trip(jj, carry_m, carry_g, dq, first):
                for t in range(SB_KB):
                    j = last - jj * SB_KB - t
                    at = _blk(jnp.maximum(j, 0))
                    k = k_ref[at, :]
                    rows = slice(_SB_LATER_ROWS, SB_ROWS) if first and t == 0 else slice(0, SB_ROWS)
                    base = i + rows.start // (2 * BLK)
                    valid, log_beta, w, part_m = _sb_scores(q2[rows], k, base, j, carry_m[rows], u_excl)
                    wb = w.astype(BF16)
                    g = wb.astype(F32) * _dot_nt(do2[rows], v_ref[at, :])
                    sums = _dot_split(g, u_incl)
                    before = total[rows] - (carry_g[rows] + sums[:, :BLK])
                    dz = jnp.where(valid, g - jnp.exp(log_beta) * (g + before), 0.0)
                    dzb = (dz * SCALE).astype(BF16)
                    dk_acc[at, :] += _dot_tn(dzb, q2[rows])
                    dv_acc[at, :] += _dot_tn(wb, do2[rows])
                    carry_m = _put_rows(carry_m, rows, part_m)
                    carry_g = _put_rows(carry_g, rows, carry_g[rows] + sums[:, BLK:])
                    dq = _put_rows(dq, rows, dq[rows] + _dot(dzb, k))
                return carry_m, carry_g, dq

            def k_block(carry):
                carry_m, carry_g, dq = trip(carry[0], carry[2], carry[3], carry[4], False)
                return carry[0] + 1, _alive(carry_m), carry_m, carry_g, dq

            carry_m, carry_g, dq = trip(0, zero, zero, zero, True)
            _, _, _, _, dq = lax.while_loop(functools.partial(_more_keys, last), k_block,
                                            (jnp.int32(1), _alive(carry_m), carry_m, carry_g, dq))
            for t in range(SB_QB):
                dq_ref[_blk(i + t), :] = _unstack_heads(dq[2 * BLK * t:2 * BLK * (t + 1)]).astype(BF16)
            return 0

        lax.fori_loop(0, nq // SB_QB, q_block, 0)
        dk_ref[...] = dk_acc[...].astype(BF16)
        dv_ref[...] = dv_acc[...].astype(BF16)

    return pl.pallas_call(
        body, name="sb_bwd", grid=(pairs,),
        in_specs=[col(0), col(pairs), col(2 * pairs), own, own],
        out_specs=[own, own, own],
        out_shape=[jax.ShapeDtypeStruct((s, D_SB), BF16)] * 3,
        scratch_shapes=[pltpu.VMEM((s, LANES), F32), pltpu.VMEM((s, LANES), F32)],
        compiler_params=_params(56),
    )(qkv, qkv, qkv, o_sb, do_sb)


DIL_UNROLL = 8


def _band_masks(b):
    row = lax.broadcasted_iota(jnp.int32, (2 * BLK, BLK), 0) & (BLK - 1)
    col = lax.broadcasted_iota(jnp.int32, (2 * BLK, BLK), 1)
    return col <= row, (col - row) >= jnp.where(b > 0, 0, BLK)


def _dil_tiles(qf, kf, vf, d, t, nb):
    c, b = t // nb, t % nb
    start = c + d * BLK * b
    rows = pl.ds(start, BLK, stride=d)
    prev = pl.ds(jnp.where(b > 0, start - d * BLK, start), BLK, stride=d)
    bf = lambda ref, sl: ref[sl, :].astype(BF16)
    return b, rows, prev, _stack_heads(bf(qf, rows)), bf(kf, rows), bf(kf, prev), bf(vf, rows), bf(vf, prev)


def _lanes_of_heads(col2):
    return _unstack_heads(jnp.broadcast_to(col2, (2 * BLK, LANES)))


def _dilated_fwd(qkv):
    s = qkv.shape[0]
    pairs = D_DIL // LANES
    base = (3 * D_SB) // LANES
    col = lambda off: pl.BlockSpec((s, LANES), lambda p: (0, off + p))
    own = pl.BlockSpec((s, LANES), lambda p: (0, p))

    def body(q_ref, k_ref, v_ref, acc_ref, m_ref, qf, kf, vf, l_scr):
        qf[...] = q_ref[...].astype(F32)
        kf[...] = k_ref[...].astype(F32)
        vf[...] = v_ref[...].astype(F32)
        for d in DILATIONS:
            nb = s // (d * BLK)

            def block(t, _):
                b, rows, prev, q2, kc, kp, vc, vp = _dil_tiles(qf, kf, vf, d, t, nb)
                in_cur, in_prev = _band_masks(b)
                zc = jnp.where(in_cur, _dot_nt(q2, kc) * SCALE, NEG)
                zp = jnp.where(in_prev, _dot_nt(q2, kp) * SCALE, NEG)
                m = jnp.maximum(jnp.max(zc, axis=1, keepdims=True), jnp.max(zp, axis=1, keepdims=True))
                pc, pp = jnp.exp(zc - m), jnp.exp(zp - m)
                den = jnp.sum(pc, axis=1, keepdims=True) + jnp.sum(pp, axis=1, keepdims=True)
                acc = _unstack_heads(_dot(pc.astype(BF16), vc) + _dot(pp.astype(BF16), vp))
                m_t, l_t = _lanes_of_heads(m), _lanes_of_heads(den)
                if d == DILATIONS[0]:
                    m_ref[rows, :] = m_t
                    l_scr[rows, :] = l_t
                    acc_ref[rows, :] = acc
                else:
                    m_old = m_ref[rows, :]
                    m_new = jnp.maximum(m_old, m_t)
                    keep, add = jnp.exp(m_old - m_new), jnp.exp(m_t - m_new)
                    m_ref[rows, :] = m_new
                    l_scr[rows, :] = l_scr[rows, :] * keep + l_t * add
                    acc_ref[rows, :] = acc_ref[rows, :] * keep + acc * add
                return 0

            lax.fori_loop(0, s // BLK, block, 0, unroll=DIL_UNROLL)

        def finish(i, _):
            l = l_scr[_blk(i), :]
            acc_ref[_blk(i), :] = acc_ref[_blk(i), :] / l
            m_ref[_blk(i), :] = m_ref[_blk(i), :] + jnp.log(l)
            return 0

        lax.fori_loop(0, s // BLK, finish, 0)

    return pl.pallas_call(
        body, name="dilated_fwd", grid=(pairs,),
        in_specs=[col(base), col(base + pairs), col(base + 2 * pairs)],
        out_specs=[own, own],
        out_shape=[jax.ShapeDtypeStruct((s, D_DIL), F32)] * 2,
        scratch_shapes=[pltpu.VMEM((s, LANES), F32)] * 4,
        compiler_params=_params(56),
    )(qkv, qkv, qkv)


def _stack_lanes(t):
    other = pltpu.roll(t, HEAD_DIM, 1)
    first = _head_masks()[0]
    return jnp.concatenate([jnp.where(first, t, other), jnp.where(first, other, t)], axis=0)


def _dilated_bwd(qkv, delta, lse, dout):
    s = qkv.shape[0]
    pairs = D_DIL // LANES
    base = (3 * D_SB) // LANES
    once = pl.Buffered(1)
    col = lambda off: pl.BlockSpec((s, LANES), lambda p: (0, off + p), pipeline_mode=once)
    own = pl.BlockSpec((s, LANES), lambda p: (0, p), pipeline_mode=once)
    res = pl.BlockSpec((s, LANES), lambda p: (0, p))

    def body(q_ref, k_ref, v_ref, dl_ref, l_ref, do_ref, dq_ref, dk_ref, dv_ref, qf, kf, vf):
        qf[...] = q_ref[...].astype(F32)
        kf[...] = k_ref[...].astype(F32)
        vf[...] = v_ref[...].astype(F32)
        dq_ref[...] = jnp.zeros_like(dq_ref)
        dk_ref[...] = jnp.zeros_like(dk_ref)
        dv_ref[...] = jnp.zeros_like(dv_ref)
        for d in DILATIONS:
            nb = s // (d * BLK)

            def block(t, _):
                b, rows, prev, q2, kc, kp, vc, vp = _dil_tiles(qf, kf, vf, d, t, nb)
                in_cur, in_prev = _band_masks(b)
                do2 = _stack_heads(do_ref[rows, :].astype(BF16))
                delta = _stack_lanes(dl_ref[rows, :])
                lse2 = _stack_lanes(l_ref[rows, :])
                wc = jnp.exp(jnp.where(in_cur, _dot_nt(q2, kc) * SCALE, NEG) - lse2)
                wp = jnp.exp(jnp.where(in_prev, _dot_nt(q2, kp) * SCALE, NEG) - lse2)
                dzc = (wc * (_dot_nt(do2, vc) - delta) * SCALE).astype(BF16)
                dzp = (wp * (_dot_nt(do2, vp) - delta) * SCALE).astype(BF16)
                dq_ref[rows, :] += _unstack_heads(_dot(dzc, kc) + _dot(dzp, kp))
                dk_ref[rows, :] += _dot_tn(dzc, q2)
                dk_ref[prev, :] += _dot_tn(dzp, q2)
                dv_ref[rows, :] += _dot_tn(wc.astype(BF16), do2)
                dv_ref[prev, :] += _dot_tn(wp.astype(BF16), do2)
                return 0

            lax.fori_loop(0, s // BLK, block, 0, unroll=DIL_UNROLL)

    return pl.pallas_call(
        body, name="dilated_bwd", grid=(pairs,),
        in_specs=[col(base), col(base + pairs), col(base + 2 * pairs), own, own, own],
        out_specs=[res, res, res],
        out_shape=[jax.ShapeDtypeStruct((s, D_DIL), F32)] * 3,
        scratch_shapes=[pltpu.VMEM((s, LANES), F32)] * 3,
        compiler_params=_params(60),
    )(qkv, qkv, qkv, delta, lse, dout)


def _dilated_finish(grads, cos, sin):
    s = grads[0].shape[0]
    spec = pl.BlockSpec((TM, D_DIL), lambda i: (i, 0))
    tab = pl.BlockSpec((TM, LANES), lambda i: (i, 0))

    def body(dq_ref, dk_ref, dv_ref, c_ref, s_ref, oq_ref, ok_ref, ov_ref):
        for src, dst, rotated in ((dq_ref, oq_ref, True), (dk_ref, ok_ref, True), (dv_ref, ov_ref, False)):
            for c in range(D_DIL // LANES):
                lanes = slice(c * LANES, (c + 1) * LANES)
                piece = src[:, lanes]
                dst[:, lanes] = (_rotate(piece, c_ref[...], -s_ref[...]) if rotated else piece).astype(BF16)

    return pl.pallas_call(
        body, name="dilated_finish", grid=(s // TM,),
        in_specs=[spec] * 3 + [tab, tab], out_specs=[spec] * 3,
        out_shape=[jax.ShapeDtypeStruct((s, D_DIL), BF16)] * 3,
        compiler_params=_params(32),
    )(*grads, cos, sin)


def _place():
    x, y, c = lax.axis_index("x"), lax.axis_index("y"), lax.axis_index("c")
    return x, y, c, 2 * x + y


def _chip(k, c):
    return (k >> 1, k & 1, c)


def _half(ref, h):
    n = ref.shape[0] // 2
    return ref.at[pl.ds(h * n, n)]


class _BackgroundGather:
    def __init__(self, ins, outs, scratch):
        n = self.n = len(ins)
        self.ins, self.outs = ins, outs
        self.mine, self.landed, self.passed = scratch[0:3 * n:3], scratch[1:3 * n:3], scratch[2:3 * n:3]
        self.send_sem, self.recv_sem, self.local_sem = scratch[3 * n:3 * n + 3]
        x, y, self.c, self.k = _place()
        self.sibling = (x, y, 1 - self.c)

    @staticmethod
    def scratch_shapes(shards):
        shapes = []
        for a in shards:
            half = (N_CHIP - 1, a.shape[0] // 2, a.shape[1])
            shapes += [pltpu.VMEM(a.shape, a.dtype), pltpu.VMEM(half, a.dtype), pltpu.VMEM(half, a.dtype)]
        n = len(shards)
        return shapes + [pltpu.SemaphoreType.DMA((6 * n,)), pltpu.SemaphoreType.DMA((6 * n,)),
                         pltpu.SemaphoreType.DMA((8 * n,))]

    @staticmethod
    def out_shapes(shards):
        return [jax.ShapeDtypeStruct((N_CHIP,) + a.shape, a.dtype) for a in shards]

    def _remote(self, a, slot, src, dst, to):
        return pltpu.make_async_remote_copy(src_ref=src, dst_ref=dst, send_sem=self.send_sem.at[6 * a + slot],
                                            recv_sem=self.recv_sem.at[6 * a + slot], device_id=to, device_id_type=MESH)

    def _local(self, a, slot, src, dst):
        return pltpu.make_async_copy(src, dst, self.local_sem.at[8 * a + slot])

    def _ici(self, a, j):
        return self._remote(a, j - 1, _half(self.mine[a], self.c), self.landed[a].at[j - 1], _chip(self.k ^ j, self.c))

    def _to_sibling(self, a, j):
        return self._remote(a, 2 + j, self.landed[a].at[j - 1], self.passed[a].at[j - 1], self.sibling)

    def _own(self, a):
        return self._local(a, 0, self.ins[a], self.outs[a].at[self.k])

    def _load(self, a):
        return self._local(a, 1, self.ins[a], self.mine[a])

    def _store_landed(self, a, j):
        return self._local(a, 1 + j, self.landed[a].at[j - 1], _half(self.outs[a].at[self.k ^ j], self.c))

    def _store_passed(self, a, j):
        return self._local(a, 4 + j, self.passed[a].at[j - 1], _half(self.outs[a].at[self.k ^ j], 1 - self.c))

    def start(self):
        for a in range(self.n):
            self._own(a).start()
            self._load(a).start()
        for a in range(self.n):
            self._load(a).wait()
            for j in range(1, N_CHIP):
                self._ici(a, j).start()

    def forward(self):
        for j in range(1, N_CHIP):
            for a in range(self.n):
                self._ici(a, j).wait_recv()
                self._to_sibling(a, j).start()
                self._store_landed(a, j).start()

    def finish(self):
        for j in range(1, N_CHIP):
            for a in range(self.n):
                self._to_sibling(a, j).wait_recv()
                self._store_passed(a, j).start()
        for a in range(self.n):
            for j in range(1, N_CHIP):
                self._ici(a, j).wait_send()
                self._to_sibling(a, j).wait_send()
                self._store_landed(a, j).wait()
                self._store_passed(a, j).wait()
            self._own(a).wait()


def _all_gather(shards):
    n = len(shards)
    any_spec = pl.BlockSpec(memory_space=pl.ANY)

    def body(*refs):
        gather = _BackgroundGather(refs[:n], refs[n:2 * n], refs[2 * n:])
        gather.start()
        gather.forward()
        gather.finish()

    return pl.pallas_call(
        body, name="weights_all_gather",
        in_specs=[any_spec] * n, out_specs=[any_spec] * n,
        out_shape=_BackgroundGather.out_shapes(shards),
        scratch_shapes=_BackgroundGather.scratch_shapes(shards),
        compiler_params=_params(32),
    )(*shards)


def _reduce_scatter(g, core, name):
    n, r, c = g.shape
    hr = r // 2
    once = pl.Buffered(1)
    in_specs = [pl.BlockSpec((n, hr, c), lambda i, core_ref: (0, core_ref[0], 0), pipeline_mode=once),
                pl.BlockSpec((n, hr, c), lambda i, core_ref: (0, 1 - core_ref[0], 0), pipeline_mode=once)]

    def body(core_ref, mine_ref, other_ref, out_ref, from_core, sums_bf, from_chips, done, from_core2, send_sem, recv_sem):
        x, y, cc, k = _place()
        sibling = (x, y, 1 - cc)

        def copy(slot, src, dst, to):
            return pltpu.make_async_remote_copy(src_ref=src, dst_ref=dst, send_sem=send_sem.at[slot],
                                                recv_sem=recv_sem.at[slot], device_id=to, device_id_type=MESH)

        from_sibling = [copy(j, other_ref.at[k ^ j], from_core.at[k ^ j], sibling) for j in range(N_CHIP)]
        for j in (1, 2, 3, 0):
            from_sibling[j].start()
        sends = []
        for j in range(1, N_CHIP):
            from_sibling[j].wait()
            sums_bf[j - 1] = (mine_ref[k ^ j] + from_core[k ^ j]).astype(BF16)
            cp = copy(N_CHIP - 1 + j, sums_bf.at[j - 1], from_chips.at[j - 1], _chip(k ^ j, cc))
            cp.start()
            sends.append(cp)
        from_sibling[0].wait()
        red = mine_ref[k] + from_core[k]
        for j in range(1, N_CHIP):
            sends[j - 1].wait()
            red = red + from_chips[j - 1].astype(F32)
        done[...] = red
        last = copy(2 * N_CHIP - 1, done, from_core2, sibling)
        last.start()
        last.wait()
        row0 = pl.multiple_of(cc * hr, 8)
        row1 = pl.multiple_of((1 - cc) * hr, 8)
        out_ref[pl.ds(row0, hr), :] = red
        out_ref[pl.ds(row1, hr), :] = from_core2[...]

    grid_spec = pltpu.PrefetchScalarGridSpec(
        num_scalar_prefetch=1, grid=(1,), in_specs=in_specs,
        out_specs=pl.BlockSpec((r, c), lambda i, core_ref: (0, 0)),
        scratch_shapes=[pltpu.VMEM((n, hr, c), F32), pltpu.VMEM((N_CHIP - 1, hr, c), BF16),
                        pltpu.VMEM((N_CHIP - 1, hr, c), BF16), pltpu.VMEM((hr, c), F32), pltpu.VMEM((hr, c), F32),
                        pltpu.SemaphoreType.DMA((2 * N_CHIP,)), pltpu.SemaphoreType.DMA((2 * N_CHIP,))])
    return pl.pallas_call(
        body, name=name, grid_spec=grid_spec, out_shape=jax.ShapeDtypeStruct((r, c), F32),
        compiler_params=_params(56),
    )(core, g, g)


def _elementwise(fn, name, ins, n_out, rows):
    total, cols = ins[0].shape
    spec = pl.BlockSpec((rows, cols), lambda i: (i, 0))

    def body(*refs):
        res = fn(*[r[...] for r in refs[:len(ins)]])
        for o, v in zip(refs[len(ins):], res):
            o[...] = v

    return pl.pallas_call(
        body, name=name, grid=(total // rows,),
        in_specs=[spec] * len(ins), out_specs=[spec] * n_out,
        out_shape=[jax.ShapeDtypeStruct((total, cols), F32)] * n_out,
        compiler_params=_params(48),
    )(*ins)


def _adamw(w, g, m, v):
    m = ADAM_B1 * m + (1.0 - ADAM_B1) * g
    v = ADAM_B2 * v + (1.0 - ADAM_B2) * (g * g)
    m_hat = m / (1.0 - ADAM_B1 ** ADAM_STEP)
    v_hat = v / (1.0 - ADAM_B2 ** ADAM_STEP)
    delta = -ADAM_LR * (m_hat / (jnp.sqrt(v_hat) + ADAM_EPS) + ADAM_WD * w)
    return delta, m, v


def _reduce_and_update(grads, weights, moms, vels):
    core = lax.axis_index("c").astype(jnp.int32).reshape(1)
    full = [_reduce_scatter(g, core, f"grads_reduce_scatter_{a}") for a, g in enumerate(grads)]
    out = []
    for a, (g, w, m, v) in enumerate(zip(full, weights, moms, vels)):
        rows = g.shape[0] // 2
        out.append((g,) + tuple(_elementwise(lambda gg, ww, mm, vv: _adamw(ww, gg, mm, vv), f"adamw_{a}", [g, w, m, v], 3, rows)))
    return out


def _reduce_vectors(part, w, m, v):
    n_dev = 8

    def body(p_ref, w_ref, m_ref, v_ref, g_ref, d_ref, nm_ref, nv_ref, buf, send_sem, recv_sem):
        x, y, c, _ = _place()
        me = 4 * x + 2 * y + c
        buf[me] = p_ref[...]
        sends = []
        for off in range(1, n_dev):
            peer = me ^ off
            cp = pltpu.make_async_remote_copy(src_ref=p_ref, dst_ref=buf.at[me], send_sem=send_sem.at[off - 1],
                                              recv_sem=recv_sem.at[off - 1], device_id=(peer >> 2, (peer >> 1) & 1, peer & 1),
                                              device_id_type=MESH)
            cp.start()
            sends.append(cp)
        for off in range(1, n_dev):
            peer = me ^ off
            pltpu.make_async_remote_copy(src_ref=p_ref, dst_ref=buf.at[peer], send_sem=send_sem.at[off - 1],
                                         recv_sem=recv_sem.at[off - 1], device_id=(peer >> 2, (peer >> 1) & 1, peer & 1),
                                         device_id_type=MESH).wait_recv()
        for cp in sends:
            cp.wait_send()
        g = buf[0]
        for d in range(1, n_dev):
            g = g + buf[d]
        g_ref[...] = g
        delta, nm, nv = _adamw(w_ref[...], g, m_ref[...], v_ref[...])
        d_ref[...] = delta
        nm_ref[...] = nm
        nv_ref[...] = nv

    vm = pl.BlockSpec(memory_space=pltpu.VMEM)
    return pl.pallas_call(
        body, name="gains_all_reduce",
        in_specs=[vm] * 4, out_specs=[vm] * 4,
        out_shape=[jax.ShapeDtypeStruct(part.shape, F32)] * 4,
        scratch_shapes=[pltpu.VMEM((n_dev,) + part.shape, F32), pltpu.SemaphoreType.DMA((n_dev - 1,)),
                        pltpu.SemaphoreType.DMA((n_dev - 1,))],
    )(part, w, m, v)


def _pad_row(a):
    a = a.reshape(1, -1)
    return jnp.pad(a, ((0, 0), (0, D_MODEL - a.shape[1])))


def kernel(x, ffn1_norm, ffn1_w_gate, ffn1_w_up, ffn1_w_down, mix_norm, w_in, sb_out_norm, dil_out_norm, w_out, ffn2_norm, ffn2_w_gate, ffn2_w_up, ffn2_w_down, final_norm, loss_target, m_ffn1_norm, m_ffn1_w_gate, m_ffn1_w_up, m_ffn1_w_down, m_mix_norm, m_w_in, m_sb_out_norm, m_dil_out_norm, m_w_out, m_ffn2_norm, m_ffn2_w_gate, m_ffn2_w_up, m_ffn2_w_down, m_final_norm, v_ffn1_norm, v_ffn1_w_gate, v_ffn1_w_up, v_ffn1_w_down, v_mix_norm, v_w_in, v_sb_out_norm, v_dil_out_norm, v_w_out, v_ffn2_norm, v_ffn2_w_gate, v_ffn2_w_up, v_ffn2_w_down, v_final_norm):
    x = x[0]
    target = loss_target[0]
    s = x.shape[0]
    gf = final_norm.reshape(1, D_MODEL)
    cos, sin = _rope_tables(s)

    shard = lambda w: w[0].astype(BF16)
    wg1, wu1, wd1 = _all_gather([shard(ffn1_w_gate), shard(ffn1_w_up), shard(ffn1_w_down)])

    x1, hm, saved1, (win, wout, wd2) = _ffn1_fwd(x, ffn1_norm, mix_norm, (wg1, wu1), wd1,
                                                 [shard(w_in), shard(w_out), shard(ffn2_w_down)])
    wout = wout.reshape(D_MODEL, D_MODEL)
    qkv, (wg2, wu2) = _proj_fwd(hm, win, cos, sin, [shard(ffn2_w_gate), shard(ffn2_w_up)])
    o_sb = _sb_fwd(qkv)
    o_dl, lse = _dilated_fwd(qkv)
    x2 = _outproj_fwd(o_sb, o_dl, sb_out_norm, dil_out_norm, x1, wout)
    dx3, st_final, saved2 = _ffn2_fwd_loss(x2, ffn2_norm, gf, target, (wg2, wu2), wd2)

    dx2, dwg2, dwu2, dwd2, st_ffn2 = _ffn_bwd(x2, ffn2_norm, dx3, saved2, (wg2, wu2), wd2, 1)
    do_sb, do_dl, delta_dl, dwout, st_out = _outproj_bwd(dx2, o_sb, o_dl, sb_out_norm, dil_out_norm, wout)
    dq_sb, dk_sb, dv_sb = _sb_bwd(qkv, o_sb, do_sb)
    dq_dl, dk_dl, dv_dl = _dilated_finish(_dilated_bwd(qkv, delta_dl, lse, do_dl), cos, sin)
    dqkv = jnp.concatenate([dq_sb, dk_sb, dv_sb, dq_dl, dk_dl, dv_dl], axis=1)
    dx1, dwin, st_mix = _proj_bwd(x1, mix_norm, dqkv, win, dx2)
    grad_x, dwg1, dwu1, dwd1, st_ffn1 = _ffn_bwd(x, ffn1_norm, dx1, saved1, (wg1, wu1), wd1, 0)

    names = ["ffn1_w_gate", "ffn1_w_up", "ffn1_w_down", "w_in", "w_out", "ffn2_w_gate", "ffn2_w_up", "ffn2_w_down"]
    grads = [dwg1, dwu1, dwd1, dwin, dwout.reshape(N_CHIP, OUTB, D_MODEL), dwg2, dwu2, dwd2]
    weights = [ffn1_w_gate[0], ffn1_w_up[0], ffn1_w_down[0], w_in[0], w_out[0], ffn2_w_gate[0], ffn2_w_up[0], ffn2_w_down[0]]
    moms = [m_ffn1_w_gate[0], m_ffn1_w_up[0], m_ffn1_w_down[0], m_w_in[0], m_w_out[0], m_ffn2_w_gate[0], m_ffn2_w_up[0], m_ffn2_w_down[0]]
    vels = [v_ffn1_w_gate[0], v_ffn1_w_up[0], v_ffn1_w_down[0], v_w_in[0], v_w_out[0], v_ffn2_w_gate[0], v_ffn2_w_up[0], v_ffn2_w_down[0]]
    mats = {n: tuple(t[None] for t in r) for n, r in zip(names, _reduce_and_update(grads, weights, moms, vels))}

    vec_names = ["ffn1_norm", "mix_norm", "sb_out_norm", "dil_out_norm", "ffn2_norm", "final_norm"]
    part = jnp.concatenate([st_ffn1[0:1], st_mix[0:1], _pad_row(st_out[0]), _pad_row(st_out[1]), st_ffn2[0:1],
                            st_final[0:1], st_final[1:2], jnp.zeros((1, D_MODEL), F32)], axis=0)
    pack = lambda arrs: jnp.concatenate([_pad_row(a) for a in arrs] + [jnp.zeros((2, D_MODEL), F32)], axis=0)
    g_vec, d_vec, m_vec, v_vec = _reduce_vectors(
        part,
        pack([ffn1_norm, mix_norm, sb_out_norm, dil_out_norm, ffn2_norm, final_norm]),
        pack([m_ffn1_norm, m_mix_norm, m_sb_out_norm, m_dil_out_norm, m_ffn2_norm, m_final_norm]),
        pack([v_ffn1_norm, v_mix_norm, v_sb_out_norm, v_dil_out_norm, v_ffn2_norm, v_final_norm]))
    like = {"ffn1_norm": ffn1_norm, "mix_norm": mix_norm, "sb_out_norm": sb_out_norm, "dil_out_norm": dil_out_norm,
            "ffn2_norm": ffn2_norm, "final_norm": final_norm}
    vecs = {n: tuple(t[i, :like[n].size].reshape(like[n].shape) for t in (g_vec, d_vec, m_vec, v_vec))
            for i, n in enumerate(vec_names)}
    loss = 0.5 * jnp.sum(g_vec[6]) / D_MODEL

    order = ["ffn1_norm", "ffn1_w_gate", "ffn1_w_up", "ffn1_w_down", "mix_norm", "w_in", "sb_out_norm", "dil_out_norm",
             "w_out", "ffn2_norm", "ffn2_w_gate", "ffn2_w_up", "ffn2_w_down", "final_norm"]
    both = {**mats, **vecs}
    return (loss, grad_x[None], *[both[n][0] for n in order], *[both[n][1] for n in order],
            *[both[n][2] for n in order], *[both[n][3] for n in order])
```

```python
import functools

import jax
import jax.numpy as jnp
from jax import lax
from jax.experimental import pallas as pl
from jax.experimental.pallas import tpu as pltpu

D_MODEL = 1024
D_FF = 2816
HEAD_DIM = 64
D_SB = 512
D_DIL = 512
D_IN = 3072
N_CHIP = 4
FFB = D_FF // N_CHIP
INB = D_IN // N_CHIP
OUTB = D_MODEL // N_CHIP
BLK = 128
LANES = 128
DILATIONS = (1, 4, 16)
ROPE_THETA = 10000.0
RMS_EPS = 1e-6
SCALE = HEAD_DIM ** -0.5
NEG = -1e30
DEAD = -104.0
ADAM_LR = 0.001
ADAM_B1 = 0.9
ADAM_B2 = 0.999
ADAM_EPS = 1e-08
ADAM_WD = 0.01
ADAM_STEP = 10
MESH = pl.DeviceIdType.MESH
F32 = jnp.float32
BF16 = jnp.bfloat16
TM = 512


def _params(vmem_mb):
    return pltpu.CompilerParams(vmem_limit_bytes=vmem_mb << 20)


def _dot(a, b):
    return jnp.dot(a, b, preferred_element_type=F32)


def _dot_nt(a, b):
    return lax.dot_general(a, b, (((1,), (1,)), ((), ())), preferred_element_type=F32)


def _dot_tn(a, b):
    return lax.dot_general(a, b, (((0,), (0,)), ((), ())), preferred_element_type=F32)


def _rms_fwd(x, g):
    r = lax.rsqrt(jnp.mean(x * x, axis=-1, keepdims=True) + RMS_EPS)
    xh = x * r
    return xh * g, xh, r


def _rms_bwd(dy, xh, r, g):
    dyg = dy * g
    dx = r * (dyg - xh * jnp.mean(dyg * xh, axis=-1, keepdims=True))
    return dx, jnp.sum(dy * xh, axis=0, keepdims=True)


def _split_bf16(a):
    hi = a.astype(BF16)
    return hi, (a - hi.astype(F32)).astype(BF16)


def _dot_split(a, b2):
    hi, lo = _split_bf16(a)
    return _dot(jnp.concatenate([hi, lo], axis=1), b2)


def _ffn_weight_specs():
    return [pl.BlockSpec((None, FFB, D_MODEL), lambda i, j: (j, 0, 0))] * 3


def _ffn_saved(s):
    hidden = jax.ShapeDtypeStruct((N_CHIP, s, FFB), BF16)
    hid = pl.BlockSpec((None, TM, FFB), lambda i, j: (j, i, 0))
    row = pl.BlockSpec((TM, D_MODEL), lambda i, j: (i, 0))
    return [row, hid, hid, hid], [jax.ShapeDtypeStruct((s, D_MODEL), BF16), hidden, hidden, hidden]


def _ffn_accumulate(h_ref, acc_scr, wg_ref, wu_ref, wd_ref, a_ref, b_ref, act_ref):
    h = h_ref[...]
    a = _dot_nt(h, wg_ref[...])
    b = _dot_nt(h, wu_ref[...])
    act = ((a * jax.nn.sigmoid(a)) * b).astype(BF16)
    a_ref[...] = a.astype(BF16)
    b_ref[...] = b.astype(BF16)
    act_ref[...] = act
    acc_scr[...] += _dot(act, wd_ref[...])


def _host_gather_before(gather, i, j, steps):
    @pl.when((i == 0) & (j == 0))
    def _():
        gather.start()

    @pl.when((i == (3 * steps) // 4) & (j == 0))
    def _():
        gather.forward()


def _host_gather_after(gather, i, j, steps):
    @pl.when((i == steps - 1) & (j == N_CHIP - 1))
    def _():
        gather.finish()


def _ffn1_fwd(x, g1, gmix, gu, wd, later_shards):
    s = x.shape[0]
    row = pl.BlockSpec((TM, D_MODEL), lambda i, j: (i, 0))
    vec = pl.BlockSpec((1, D_MODEL), lambda i, j: (0, 0))
    saved_specs, saved_shapes = _ffn_saved(s)
    n = len(later_shards)
    any_spec = pl.BlockSpec(memory_space=pl.ANY)

    def body(*refs):
        x_ref, g_ref, gm_ref, wg_ref, wu_ref, wd_ref = refs[:6]
        shard_refs, refs = refs[6:6 + n], refs[6 + n:]
        x1_ref, hm_ref, h_ref, a_ref, b_ref, act_ref = refs[:6]
        gathered_refs, acc_scr, gather_scratch = refs[6:6 + n], refs[6 + n], refs[7 + n:]
        gather = _BackgroundGather(shard_refs, gathered_refs, gather_scratch)
        i, j = pl.program_id(0), pl.program_id(1)
        _host_gather_before(gather, i, j, s // TM)

        @pl.when(j == 0)
        def _():
            h, _, _ = _rms_fwd(x_ref[...], g_ref[...])
            h_ref[...] = h.astype(BF16)
            acc_scr[...] = jnp.zeros_like(acc_scr)

        _ffn_accumulate(h_ref, acc_scr, wg_ref, wu_ref, wd_ref, a_ref, b_ref, act_ref)

        @pl.when(j == N_CHIP - 1)
        def _():
            x1 = x_ref[...] + 0.5 * acc_scr[...]
            x1_ref[...] = x1
            hm, _, _ = _rms_fwd(x1, gm_ref[...])
            hm_ref[...] = hm.astype(BF16)

        _host_gather_after(gather, i, j, s // TM)

    x1, hm, h, a, b, act, *gathered = pl.pallas_call(
        body, name="ffn1_fwd", grid=(s // TM, N_CHIP),
        in_specs=[row, vec, vec] + _ffn_weight_specs() + [any_spec] * n,
        out_specs=[row, row] + saved_specs + [any_spec] * n,
        out_shape=([jax.ShapeDtypeStruct((s, D_MODEL), F32), jax.ShapeDtypeStruct((s, D_MODEL), BF16)] + saved_shapes
                   + _BackgroundGather.out_shapes(later_shards)),
        scratch_shapes=[pltpu.VMEM((TM, D_MODEL), F32)] + _BackgroundGather.scratch_shapes(later_shards),
        compiler_params=_params(58),
    )(x, g1, gmix, gu[0], gu[1], wd, *later_shards)
    return x1, hm, [h, a, b, act], gathered


def _ffn2_fwd_loss(x2, g2, gf, target, gu, wd):
    s = x2.shape[0]
    row = pl.BlockSpec((TM, D_MODEL), lambda i, j: (i, 0))
    vec = pl.BlockSpec((1, D_MODEL), lambda i, j: (0, 0))
    stat = pl.BlockSpec((8, D_MODEL), lambda i, j: (0, 0))
    saved_specs, saved_shapes = _ffn_saved(s)

    def body(x_ref, g_ref, gf_ref, t_ref, wg_ref, wu_ref, wd_ref, dx_ref, st_ref, h_ref, a_ref, b_ref, act_ref, acc_scr):
        i, j = pl.program_id(0), pl.program_id(1)

        @pl.when((i == 0) & (j == 0))
        def _():
            st_ref[...] = jnp.zeros_like(st_ref)

        @pl.when(j == 0)
        def _():
            h, _, _ = _rms_fwd(x_ref[...], g_ref[...])
            h_ref[...] = h.astype(BF16)
            acc_scr[...] = jnp.zeros_like(acc_scr)

        _ffn_accumulate(h_ref, acc_scr, wg_ref, wu_ref, wd_ref, a_ref, b_ref, act_ref)

        @pl.when(j == N_CHIP - 1)
        def _():
            x3 = x_ref[...] + 0.5 * acc_scr[...]
            y, xh, r = _rms_fwd(x3, gf_ref[...])
            err = y - t_ref[...]
            dx, dg = _rms_bwd(err * (1.0 / D_MODEL), xh, r, gf_ref[...])
            dx_ref[...] = dx
            st_ref[0:1, :] += dg
            st_ref[1:2, :] += jnp.sum(err * err, axis=0, keepdims=True)

    dx3, st, *saved = pl.pallas_call(
        body, name="ffn2_fwd_loss", grid=(s // TM, N_CHIP),
        in_specs=[row, vec, vec, row] + _ffn_weight_specs(),
        out_specs=[row, stat] + saved_specs,
        out_shape=[jax.ShapeDtypeStruct((s, D_MODEL), F32), jax.ShapeDtypeStruct((8, D_MODEL), F32)] + saved_shapes,
        scratch_shapes=[pltpu.VMEM((TM, D_MODEL), F32)],
        compiler_params=_params(56),
    )(x2, g2, gf, target, gu[0], gu[1], wd)
    return dx3, st, saved


def _ffn_bwd(xin, g, dy, saved, gu, wd, f):
    s = xin.shape[0]
    hb, gate, up, act = saved
    row = pl.BlockSpec((TM, D_MODEL), lambda i, j: (i, 0))
    vec = pl.BlockSpec((1, D_MODEL), lambda i, j: (0, 0))
    stat = pl.BlockSpec((8, D_MODEL), lambda i, j: (0, 0))
    hid = pl.BlockSpec((None, TM, FFB), lambda i, j: (j, i, 0))

    def body(x_ref, g_ref, dy_ref, a_ref, b_ref, wg_ref, wu_ref, wd_ref, out_ref, dyh_ref, da_ref, db_ref, st_ref, dh_scr):
        i, j = pl.program_id(0), pl.program_id(1)

        @pl.when((i == 0) & (j == 0))
        def _():
            st_ref[...] = jnp.zeros_like(st_ref)

        @pl.when(j == 0)
        def _():
            dyh_ref[...] = (0.5 * dy_ref[...]).astype(BF16)
            dh_scr[...] = jnp.zeros_like(dh_scr)

        a = a_ref[...].astype(F32)
        b = b_ref[...].astype(F32)
        sg = jax.nn.sigmoid(a)
        dact = _dot_nt(dyh_ref[...], wd_ref[...])
        dab = (dact * b * (sg * (1.0 + a * (1.0 - sg)))).astype(BF16)
        dbb = (dact * (a * sg)).astype(BF16)
        da_ref[...] = dab
        db_ref[...] = dbb
        dh_scr[...] += _dot(dab, wg_ref[...]) + _dot(dbb, wu_ref[...])

        @pl.when(j == N_CHIP - 1)
        def _():
            _, xh, r = _rms_fwd(x_ref[...], g_ref[...])
            dx, dg = _rms_bwd(dh_scr[...], xh, r, g_ref[...])
            out_ref[...] = dy_ref[...] + dx
            st_ref[0:1, :] += dg

    hidden = jax.ShapeDtypeStruct((N_CHIP, s, FFB), BF16)
    dx, dyh, da, db, st = pl.pallas_call(
        body, name=f"ffn{f + 1}_bwd_dx", grid=(s // TM, N_CHIP),
        in_specs=[row, vec, row, hid, hid] + _ffn_weight_specs(),
        out_specs=[row, row, hid, hid, stat],
        out_shape=[jax.ShapeDtypeStruct((s, D_MODEL), F32), jax.ShapeDtypeStruct((s, D_MODEL), BF16),
                   hidden, hidden, jax.ShapeDtypeStruct((8, D_MODEL), F32)],
        scratch_shapes=[pltpu.VMEM((TM, D_MODEL), F32)],
        compiler_params=_params(56),
    )(xin, g, dy, gate, up, gu[0], gu[1], wd)

    tok = pl.BlockSpec((TM, D_MODEL), lambda j, i: (i, 0))
    hid2 = pl.BlockSpec((None, TM, FFB), lambda j, i: (j, i, 0))
    gspecs = [pl.BlockSpec((None, FFB, D_MODEL), lambda j, i: (j, 0, 0))] * 3

    def wbody(h_ref, dyh_ref, da_ref, db_ref, act_ref, dwg_ref, dwu_ref, dwd_ref):
        @pl.when(pl.program_id(1) == 0)
        def _():
            dwg_ref[...] = jnp.zeros_like(dwg_ref)
            dwu_ref[...] = jnp.zeros_like(dwu_ref)
            dwd_ref[...] = jnp.zeros_like(dwd_ref)

        hb = h_ref[...]
        dwg_ref[...] += _dot_tn(da_ref[...], hb)
        dwu_ref[...] += _dot_tn(db_ref[...], hb)
        dwd_ref[...] += _dot_tn(act_ref[...], dyh_ref[...])

    dwg, dwu, dwd = pl.pallas_call(
        wbody, name=f"ffn{f + 1}_bwd_dw", grid=(N_CHIP, s // TM),
        in_specs=[tok, tok, hid2, hid2, hid2], out_specs=gspecs,
        out_shape=[jax.ShapeDtypeStruct((N_CHIP, FFB, D_MODEL), F32)] * 3,
        compiler_params=_params(48),
    )(hb, dyh, da, db, act)
    return dx, dwg, dwu, dwd, st


def _rope_tables(s):
    half = HEAD_DIM // 2
    inv_freq = ROPE_THETA ** (-jnp.arange(half, dtype=F32) / half)
    ang = jnp.arange(s).astype(F32)[:, None] * inv_freq[None, :]
    cos, sin = jnp.cos(ang), jnp.sin(ang)
    cos2 = jnp.concatenate([cos, cos], axis=-1)
    sin2 = jnp.concatenate([-sin, sin], axis=-1)
    return jnp.tile(cos2, (1, LANES // HEAD_DIM)), jnp.tile(sin2, (1, LANES // HEAD_DIM))


def _rotate(t, cos, sin_signed):
    lane = lax.broadcasted_iota(jnp.int32, t.shape, 1)
    first = (lane % HEAD_DIM) < (HEAD_DIM // 2)
    partner = jnp.where(first, pltpu.roll(t, LANES - HEAD_DIM // 2, 1), pltpu.roll(t, HEAD_DIM // 2, 1))
    return t * cos + partner * sin_signed


def _proj_fwd(hm, win, cos, sin, later_shards):
    s = hm.shape[0]
    n_sub = INB // LANES
    first_rot, last_rot = (3 * D_SB) // LANES, (3 * D_SB + 2 * D_DIL) // LANES
    n = len(later_shards)
    any_spec = pl.BlockSpec(memory_space=pl.ANY)

    def body(*refs):
        h_ref, w_ref, c_ref, s_ref = refs[:4]
        shard_refs, o_ref, gathered_refs, gather_scratch = refs[4:4 + n], refs[4 + n], refs[5 + n:5 + 2 * n], refs[5 + 2 * n:]
        gather = _BackgroundGather(shard_refs, gathered_refs, gather_scratch)
        i, j = pl.program_id(0), pl.program_id(1)
        _host_gather_before(gather, i, j, s // TM)
        r = _dot(h_ref[...], w_ref[...])
        for c in range(n_sub):
            t = r[:, c * LANES:(c + 1) * LANES]
            col = j * n_sub + c
            rot = (col >= first_rot) & (col < last_rot)
            lanes = slice(c * LANES, (c + 1) * LANES)

            @pl.when(rot)
            def _():
                o_ref[:, lanes] = _rotate(t, c_ref[...], s_ref[...]).astype(BF16)

            @pl.when(jnp.logical_not(rot))
            def _():
                o_ref[:, lanes] = t.astype(BF16)

        _host_gather_after(gather, i, j, s // TM)

    qkv, *gathered = pl.pallas_call(
        body, name="proj_fwd", grid=(s // TM, N_CHIP),
        in_specs=[pl.BlockSpec((TM, D_MODEL), lambda i, j: (i, 0)),
                  pl.BlockSpec((None, D_MODEL, INB), lambda i, j: (j, 0, 0)),
                  pl.BlockSpec((TM, LANES), lambda i, j: (i, 0)),
                  pl.BlockSpec((TM, LANES), lambda i, j: (i, 0))] + [any_spec] * n,
        out_specs=[pl.BlockSpec((TM, INB), lambda i, j: (i, j))] + [any_spec] * n,
        out_shape=[jax.ShapeDtypeStruct((s, D_IN), BF16)] + _BackgroundGather.out_shapes(later_shards),
        scratch_shapes=_BackgroundGather.scratch_shapes(later_shards),
        compiler_params=_params(48),
    )(hm, win, cos, sin, *later_shards)
    return qkv, gathered


def _proj_bwd(x1, gmix, dqkv, win, dx2):
    s = x1.shape[0]
    row = pl.BlockSpec((TM, D_MODEL), lambda i, j: (i, 0))
    vec = pl.BlockSpec((1, D_MODEL), lambda i, j: (0, 0))

    def body(x_ref, g_ref, dq_ref, w_ref, dx2_ref, out_ref, dw_ref, st_ref, h_scr, dh_scr):
        i, j = pl.program_id(0), pl.program_id(1)

        @pl.when((i == 0) & (j == 0))
        def _():
            st_ref[...] = jnp.zeros_like(st_ref)
            dw_ref[...] = jnp.zeros_like(dw_ref)

        @pl.when(j == 0)
        def _():
            h, _, _ = _rms_fwd(x_ref[...], g_ref[...])
            h_scr[...] = h.astype(BF16)
            dh_scr[...] = jnp.zeros_like(dh_scr)

        dq = dq_ref[...]
        dw_ref[j] += _dot_tn(h_scr[...], dq)
        dh_scr[...] += _dot_nt(dq, w_ref[...])

        @pl.when(j == N_CHIP - 1)
        def _():
            _, xh, r = _rms_fwd(x_ref[...], g_ref[...])
            dx, dg = _rms_bwd(dh_scr[...], xh, r, g_ref[...])
            out_ref[...] = dx2_ref[...] + dx
            st_ref[0:1, :] += dg

    return pl.pallas_call(
        body, name="proj_bwd", grid=(s // TM, N_CHIP),
        in_specs=[row, vec, pl.BlockSpec((TM, INB), lambda i, j: (i, j)),
                  pl.BlockSpec((None, D_MODEL, INB), lambda i, j: (j, 0, 0)), row],
        out_specs=[row, pl.BlockSpec((N_CHIP, D_MODEL, INB), lambda i, j: (0, 0, 0)),
                   pl.BlockSpec((8, D_MODEL), lambda i, j: (0, 0))],
        out_shape=[jax.ShapeDtypeStruct((s, D_MODEL), F32),
                   jax.ShapeDtypeStruct((N_CHIP, D_MODEL, INB), F32),
                   jax.ShapeDtypeStruct((8, D_MODEL), F32)],
        scratch_shapes=[pltpu.VMEM((TM, D_MODEL), BF16), pltpu.VMEM((TM, D_MODEL), F32)],
        compiler_params=_params(56),
    )(x1, gmix, dqkv, win, dx2)


def _outproj_fwd(o_sb, o_dl, g_sb, g_dl, x1, wout):
    s = x1.shape[0]
    half = pl.BlockSpec((TM, D_SB), lambda i: (i, 0))
    row = pl.BlockSpec((TM, D_MODEL), lambda i: (i, 0))
    vec = pl.BlockSpec((1, D_SB), lambda i: (0, 0))

    def body(a_ref, b_ref, ga_ref, gb_ref, x_ref, w_ref, o_ref):
        ma, _, _ = _rms_fwd(a_ref[...], ga_ref[...])
        mb, _, _ = _rms_fwd(b_ref[...], gb_ref[...])
        o_ref[...] = (x_ref[...] + _dot(ma.astype(BF16), w_ref[0:D_SB, :])
                      + _dot(mb.astype(BF16), w_ref[D_SB:D_MODEL, :]))

    return pl.pallas_call(
        body, name="outproj_fwd", grid=(s // TM,),
        in_specs=[half, half, vec, vec, row, pl.BlockSpec((D_MODEL, D_MODEL), lambda i: (0, 0))],
        out_specs=row, out_shape=jax.ShapeDtypeStruct((s, D_MODEL), F32),
        compiler_params=_params(32),
    )(o_sb, o_dl, g_sb, g_dl, x1, wout)


def _outproj_bwd(dx2, o_sb, o_dl, g_sb, g_dl, wout):
    s = dx2.shape[0]
    half = pl.BlockSpec((TM, D_SB), lambda i: (i, 0))
    row = pl.BlockSpec((TM, D_MODEL), lambda i: (i, 0))
    vec = pl.BlockSpec((1, D_SB), lambda i: (0, 0))
    full = pl.BlockSpec((D_MODEL, D_MODEL), lambda i: (0, 0))

    def body(dy_ref, a_ref, b_ref, ga_ref, gb_ref, w_ref, da_ref, db_ref, dl_ref, dw_ref, st_ref):
        @pl.when(pl.program_id(0) == 0)
        def _():
            dw_ref[...] = jnp.zeros_like(dw_ref)
            st_ref[...] = jnp.zeros_like(st_ref)

        dy = dy_ref[...].astype(BF16)
        dm = _dot_nt(dy, w_ref[...])
        ma, xa, ra = _rms_fwd(a_ref[...], ga_ref[...])
        mb, xb, rb = _rms_fwd(b_ref[...], gb_ref[...])
        dw_ref[0:D_SB, :] += _dot_tn(ma.astype(BF16), dy)
        dw_ref[D_SB:D_MODEL, :] += _dot_tn(mb.astype(BF16), dy)
        da, dga = _rms_bwd(dm[:, 0:D_SB], xa, ra, ga_ref[...])
        db, dgb = _rms_bwd(dm[:, D_SB:D_MODEL], xb, rb, gb_ref[...])
        da_ref[...] = da
        db_ref[...] = db
        r = lax.broadcasted_iota(jnp.int32, (LANES, LANES), 0) >= HEAD_DIM
        c = lax.broadcasted_iota(jnp.int32, (LANES, LANES), 1) >= HEAD_DIM
        same_head = jnp.where(r == c, 1.0, 0.0).astype(BF16)
        same_head = jnp.concatenate([same_head, same_head], axis=0)
        prod = db * b_ref[...]
        for k in range(D_DIL // LANES):
            lanes = slice(k * LANES, (k + 1) * LANES)
            dl_ref[:, lanes] = _dot_split(prod[:, lanes], same_head)
        st_ref[0:1, :] += dga
        st_ref[1:2, :] += dgb

    return pl.pallas_call(
        body, name="outproj_bwd", grid=(s // TM,),
        in_specs=[row, half, half, vec, vec, full],
        out_specs=[half, half, half, full, pl.BlockSpec((8, D_SB), lambda i: (0, 0))],
        out_shape=[jax.ShapeDtypeStruct((s, D_SB), F32), jax.ShapeDtypeStruct((s, D_SB), F32),
                   jax.ShapeDtypeStruct((s, D_DIL), F32),
                   jax.ShapeDtypeStruct((D_MODEL, D_MODEL), F32), jax.ShapeDtypeStruct((8, D_SB), F32)],
        compiler_params=_params(48),
    )(dx2, o_sb, o_dl, g_sb, g_dl, wout)


def _head_masks():
    lane = lax.broadcasted_iota(jnp.int32, (BLK, LANES), 1)
    return [lane < HEAD_DIM, lane >= HEAD_DIM]


def _keep(mask, a):
    return a * jnp.where(mask, 1.0, 0.0).astype(a.dtype)


def _suffix_matrices():
    r = lax.broadcasted_iota(jnp.int32, (2 * BLK, BLK), 0) & (BLK - 1)
    c = lax.broadcasted_iota(jnp.int32, (2 * BLK, BLK), 1)
    ones = jnp.ones((2 * BLK, BLK), BF16)
    excl = jnp.concatenate([(r > c).astype(BF16), ones], axis=1)
    incl = jnp.concatenate([(r >= c).astype(BF16), ones], axis=1)
    return excl, incl


def _blk(i):
    return pl.ds(pl.multiple_of(i * BLK, BLK), BLK)


def _alive(carry_m):
    return (jnp.max(carry_m) > DEAD).astype(jnp.int32)


def _more_keys(last, carry):
    return (carry[0] * SB_KB <= last) & (carry[1] > 0)


def _stack_heads(a):
    masks = _head_masks()
    return jnp.concatenate([_keep(masks[0], a), _keep(masks[1], a)], axis=0)


def _unstack_heads(a2):
    return jnp.where(_head_masks()[0], a2[:BLK], a2[BLK:])


def _head_rowsum(a):
    masks = _head_masks()
    return jnp.concatenate([jnp.sum(jnp.where(m, a, 0.0), axis=1, keepdims=True) for m in masks], axis=0)


SB_QB = 2
SB_ROWS = SB_QB * 2 * BLK
SB_KB = 4
PAST_START = 1 << 30


def _sb_rows(ref, i0, cast=None):
    tiles = [ref[_blk(i0 + t), :] for t in range(SB_QB)]
    return jnp.concatenate([_stack_heads(t if cast is None else t.astype(cast)) for t in tiles], axis=0)


_SB_LATER_ROWS = (SB_QB - 1) * 2 * BLK


def _put_rows(full, rows, part):
    return part if rows.start == 0 else jnp.concatenate([full[:rows.start], part], axis=0)


def _sb_scores(q2, k, i, j, carry_m, u_excl):
    r = lax.broadcasted_iota(jnp.int32, (q2.shape[0], BLK), 0)
    row = (r & (BLK - 1)) + ((r >> 8) << 7)
    col = lax.broadcasted_iota(jnp.int32, (q2.shape[0], BLK), 1)
    valid = (jnp.where(j >= 0, j * BLK, PAST_START) + col) < (i * BLK + row)
    z = _dot_nt(q2, k) * SCALE
    sp = jnp.maximum(z, 0.0) + jnp.log(1.0 + jnp.exp(-jnp.abs(z)))
    log_stay = jnp.where(valid, -sp, 0.0)
    log_beta = z - sp
    sums = _dot_split(log_stay, u_excl)
    later = carry_m + sums[:, :BLK]
    w = jnp.where(valid, jnp.exp(log_beta + later), 0.0)
    return valid, log_beta, w, carry_m + sums[:, BLK:]


def _sb_fwd(qkv):
    s = qkv.shape[0]
    nq = s // BLK
    pairs = D_SB // LANES
    col = lambda off: pl.BlockSpec((s, LANES), lambda p: (0, off + p))

    def body(q_ref, k_ref, v_ref, o_ref):
        u_excl, _ = _suffix_matrices()
        zero = jnp.zeros((SB_ROWS, LANES), F32)

        def q_block(ib, _):
            i = ib * SB_QB
            last = i + SB_QB - 1
            q2 = _sb_rows(q_ref, i)

            def trip(jj, carry_m, acc, first):
                for t in range(SB_KB):
                    j = last - jj * SB_KB - t
                    at = _blk(jnp.maximum(j, 0))
                    rows = slice(_SB_LATER_ROWS, SB_ROWS) if first and t == 0 else slice(0, SB_ROWS)
                    base = i + rows.start // (2 * BLK)
                    _, _, w, part = _sb_scores(q2[rows], k_ref[at, :], base, j, carry_m[rows], u_excl)
                    carry_m = _put_rows(carry_m, rows, part)
                    acc = _put_rows(acc, rows, acc[rows] + _dot(w.astype(BF16), v_ref[at, :]))
                return carry_m, acc

            def k_block(carry):
                carry_m, acc = trip(carry[0], carry[2], carry[3], False)
                return carry[0] + 1, _alive(carry_m), carry_m, acc

            carry_m, acc = trip(0, zero, zero, True)
            _, _, _, acc = lax.while_loop(functools.partial(_more_keys, last), k_block,
                                          (jnp.int32(1), _alive(carry_m), carry_m, acc))
            for t in range(SB_QB):
                o_ref[_blk(i + t), :] = _unstack_heads(acc[2 * BLK * t:2 * BLK * (t + 1)])
            return 0

        lax.fori_loop(0, nq // SB_QB, q_block, 0)

    return pl.pallas_call(
        body, name="sb_fwd", grid=(pairs,),
        in_specs=[col(0), col(pairs), col(2 * pairs)],
        out_specs=pl.BlockSpec((s, LANES), lambda p: (0, p)),
        out_shape=jax.ShapeDtypeStruct((s, D_SB), F32),
        compiler_params=_params(48),
    )(qkv, qkv, qkv)


def _sb_bwd(qkv, o_sb, do_sb):
    s = qkv.shape[0]
    nq = s // BLK
    pairs = D_SB // LANES
    col = lambda off: pl.BlockSpec((s, LANES), lambda p: (0, off + p))
    own = pl.BlockSpec((s, LANES), lambda p: (0, p))

    def body(q_ref, k_ref, v_ref, o_ref, do_ref, dq_ref, dk_ref, dv_ref, dk_acc, dv_acc):
        u_excl, u_incl = _suffix_matrices()
        zero = jnp.zeros((SB_ROWS, LANES), F32)
        dk_acc[...] = jnp.zeros_like(dk_acc)
        dv_acc[...] = jnp.zeros_like(dv_acc)

        def q_block(ib, _):
            i = ib * SB_QB
            last = i + SB_QB - 1
            q2 = _sb_rows(q_ref, i)
            do2 = _sb_rows(do_ref, i, BF16)
            totals = [_head_rowsum(do_ref[_blk(i + t), :].astype(BF16).astype(F32) * o_ref[_blk(i + t), :])
                      for t in range(SB_QB)]
            total = jnp.broadcast_to(jnp.concatenate(totals, axis=0), (SB_ROWS, BLK))

            def trip(jj, carry_m, carry_g, dq, first):
                for t in range(SB_KB):
                    j = last - jj * SB_KB - t
                    at = _blk(jnp.maximum(j, 0))
                    k = k_ref[at, :]
                    rows = slice(_SB_LATER_ROWS, SB_ROWS) if first and t == 0 else slice(0, SB_ROWS)
                    base = i + rows.start // (2 * BLK)
                    valid, log_beta, w, part_m = _sb_scores(q2[rows], k, base, j, carry_m[rows], u_excl)
                    wb = w.astype(BF16)
                    g = wb.astype(F32) * _dot_nt(do2[rows], v_ref[at, :])
                    sums = _dot_split(g, u_incl)
                    before = total[rows] - (carry_g[rows] + sums[:, :BLK])
                    dz = jnp.where(valid, g - jnp.exp(log_beta) * (g + before), 0.0)
                    dzb = (dz * SCALE).astype(BF16)
                    dk_acc[at, :] += _dot_tn(dzb, q2[rows])
                    dv_acc[at, :] += _dot_tn(wb, do2[rows])
                    carry_m = _put_rows(carry_m, rows, part_m)
                    carry_g = _put_rows(carry_g, rows, carry_g[rows] + sums[:, BLK:])
                    dq = _put_rows(dq, rows, dq[rows] + _dot(dzb, k))
                return carry_m, carry_g, dq

            def k_block(carry):
                carry_m, carry_g, dq = trip(carry[0], carry[2], carry[3], carry[4], False)
                return carry[0] + 1, _alive(carry_m), carry_m, carry_g, dq

            carry_m, carry_g, dq = trip(0, zero, zero, zero, True)
            _, _, _, _, dq = lax.while_loop(functools.partial(_more_keys, last), k_block,
                                            (jnp.int32(1), _alive(carry_m), carry_m, carry_g, dq))
            for t in range(SB_QB):
                dq_ref[_blk(i + t), :] = _unstack_heads(dq[2 * BLK * t:2 * BLK * (t + 1)]).astype(BF16)
            return 0

        lax.fori_loop(0, nq // SB_QB, q_block, 0)
        dk_ref[...] = dk_acc[...].astype(BF16)
        dv_ref[...] = dv_acc[...].astype(BF16)

    return pl.pallas_call(
        body, name="sb_bwd", grid=(pairs,),
        in_specs=[col(0), col(pairs), col(2 * pairs), own, own],
        out_specs=[own, own, own],
        out_shape=[jax.ShapeDtypeStruct((s, D_SB), BF16)] * 3,
        scratch_shapes=[pltpu.VMEM((s, LANES), F32), pltpu.VMEM((s, LANES), F32)],
        compiler_params=_params(56),
    )(qkv, qkv, qkv, o_sb, do_sb)


DIL_UNROLL = 8


def _band_masks(b):
    row = lax.broadcasted_iota(jnp.int32, (2 * BLK, BLK), 0) & (BLK - 1)
    col = lax.broadcasted_iota(jnp.int32, (2 * BLK, BLK), 1)
    return col <= row, (col - row) >= jnp.where(b > 0, 0, BLK)


def _dil_tiles(qf, kf, vf, d, t, nb):
    c, b = t // nb, t % nb
    start = c + d * BLK * b
    rows = pl.ds(start, BLK, stride=d)
    prev = pl.ds(jnp.where(b > 0, start - d * BLK, start), BLK, stride=d)
    bf = lambda ref, sl: ref[sl, :].astype(BF16)
    return b, rows, prev, _stack_heads(bf(qf, rows)), bf(kf, rows), bf(kf, prev), bf(vf, rows), bf(vf, prev)


def _lanes_of_heads(col2):
    return _unstack_heads(jnp.broadcast_to(col2, (2 * BLK, LANES)))


def _dilated_fwd(qkv):
    s = qkv.shape[0]
    pairs = D_DIL // LANES
    base = (3 * D_SB) // LANES
    col = lambda off: pl.BlockSpec((s, LANES), lambda p: (0, off + p))
    own = pl.BlockSpec((s, LANES), lambda p: (0, p))

    def body(q_ref, k_ref, v_ref, acc_ref, m_ref, qf, kf, vf, l_scr):
        qf[...] = q_ref[...].astype(F32)
        kf[...] = k_ref[...].astype(F32)
        vf[...] = v_ref[...].astype(F32)
        for d in DILATIONS:
            nb = s // (d * BLK)

            def block(t, _):
                b, rows, prev, q2, kc, kp, vc, vp = _dil_tiles(qf, kf, vf, d, t, nb)
                in_cur, in_prev = _band_masks(b)
                zc = jnp.where(in_cur, _dot_nt(q2, kc) * SCALE, NEG)
                zp = jnp.where(in_prev, _dot_nt(q2, kp) * SCALE, NEG)
                m = jnp.maximum(jnp.max(zc, axis=1, keepdims=True), jnp.max(zp, axis=1, keepdims=True))
                pc, pp = jnp.exp(zc - m), jnp.exp(zp - m)
                den = jnp.sum(pc, axis=1, keepdims=True) + jnp.sum(pp, axis=1, keepdims=True)
                acc = _unstack_heads(_dot(pc.astype(BF16), vc) + _dot(pp.astype(BF16), vp))
                m_t, l_t = _lanes_of_heads(m), _lanes_of_heads(den)
                if d == DILATIONS[0]:
                    m_ref[rows, :] = m_t
                    l_scr[rows, :] = l_t
                    acc_ref[rows, :] = acc
                else:
                    m_old = m_ref[rows, :]
                    m_new = jnp.maximum(m_old, m_t)
                    keep, add = jnp.exp(m_old - m_new), jnp.exp(m_t - m_new)
                    m_ref[rows, :] = m_new
                    l_scr[rows, :] = l_scr[rows, :] * keep + l_t * add
                    acc_ref[rows, :] = acc_ref[rows, :] * keep + acc * add
                return 0

            lax.fori_loop(0, s // BLK, block, 0, unroll=DIL_UNROLL)

        def finish(i, _):
            l = l_scr[_blk(i), :]
            acc_ref[_blk(i), :] = acc_ref[_blk(i), :] / l
            m_ref[_blk(i), :] = m_ref[_blk(i), :] + jnp.log(l)
            return 0

        lax.fori_loop(0, s // BLK, finish, 0)

    return pl.pallas_call(
        body, name="dilated_fwd", grid=(pairs,),
        in_specs=[col(base), col(base + pairs), col(base + 2 * pairs)],
        out_specs=[own, own],
        out_shape=[jax.ShapeDtypeStruct((s, D_DIL), F32)] * 2,
        scratch_shapes=[pltpu.VMEM((s, LANES), F32)] * 4,
        compiler_params=_params(56),
    )(qkv, qkv, qkv)


def _stack_lanes(t):
    other = pltpu.roll(t, HEAD_DIM, 1)
    first = _head_masks()[0]
    return jnp.concatenate([jnp.where(first, t, other), jnp.where(first, other, t)], axis=0)


def _dilated_bwd(qkv, delta, lse, dout):
    s = qkv.shape[0]
    pairs = D_DIL // LANES
    base = (3 * D_SB) // LANES
    once = pl.Buffered(1)
    col = lambda off: pl.BlockSpec((s, LANES), lambda p: (0, off + p), pipeline_mode=once)
    own = pl.BlockSpec((s, LANES), lambda p: (0, p), pipeline_mode=once)
    res = pl.BlockSpec((s, LANES), lambda p: (0, p))

    def body(q_ref, k_ref, v_ref, dl_ref, l_ref, do_ref, dq_ref, dk_ref, dv_ref, qf, kf, vf):
        qf[...] = q_ref[...].astype(F32)
        kf[...] = k_ref[...].astype(F32)
        vf[...] = v_ref[...].astype(F32)
        dq_ref[...] = jnp.zeros_like(dq_ref)
        dk_ref[...] = jnp.zeros_like(dk_ref)
        dv_ref[...] = jnp.zeros_like(dv_ref)
        for d in DILATIONS:
            nb = s // (d * BLK)

            def block(t, _):
                b, rows, prev, q2, kc, kp, vc, vp = _dil_tiles(qf, kf, vf, d, t, nb)
                in_cur, in_prev = _band_masks(b)
                do2 = _stack_heads(do_ref[rows, :].astype(BF16))
                delta = _stack_lanes(dl_ref[rows, :])
                lse2 = _stack_lanes(l_ref[rows, :])
                wc = jnp.exp(jnp.where(in_cur, _dot_nt(q2, kc) * SCALE, NEG) - lse2)
                wp = jnp.exp(jnp.where(in_prev, _dot_nt(q2, kp) * SCALE, NEG) - lse2)
                dzc = (wc * (_dot_nt(do2, vc) - delta) * SCALE).astype(BF16)
                dzp = (wp * (_dot_nt(do2, vp) - delta) * SCALE).astype(BF16)
                dq_ref[rows, :] += _unstack_heads(_dot(dzc, kc) + _dot(dzp, kp))
                dk_ref[rows, :] += _dot_tn(dzc, q2)
                dk_ref[prev, :] += _dot_tn(dzp, q2)
                dv_ref[rows, :] += _dot_tn(wc.astype(BF16), do2)
                dv_ref[prev, :] += _dot_tn(wp.astype(BF16), do2)
                return 0

            lax.fori_loop(0, s // BLK, block, 0, unroll=DIL_UNROLL)

    return pl.pallas_call(
        body, name="dilated_bwd", grid=(pairs,),
        in_specs=[col(base), col(base + pairs), col(base + 2 * pairs), own, own, own],
        out_specs=[res, res, res],
        out_shape=[jax.ShapeDtypeStruct((s, D_DIL), F32)] * 3,
        scratch_shapes=[pltpu.VMEM((s, LANES), F32)] * 3,
        compiler_params=_params(60),
    )(qkv, qkv, qkv, delta, lse, dout)


def _dilated_finish(grads, cos, sin):
    s = grads[0].shape[0]
    spec = pl.BlockSpec((TM, D_DIL), lambda i: (i, 0))
    tab = pl.BlockSpec((TM, LANES), lambda i: (i, 0))

    def body(dq_ref, dk_ref, dv_ref, c_ref, s_ref, oq_ref, ok_ref, ov_ref):
        for src, dst, rotated in ((dq_ref, oq_ref, True), (dk_ref, ok_ref, True), (dv_ref, ov_ref, False)):
            for c in range(D_DIL // LANES):
                lanes = slice(c * LANES, (c + 1) * LANES)
                piece = src[:, lanes]
                dst[:, lanes] = (_rotate(piece, c_ref[...], -s_ref[...]) if rotated else piece).astype(BF16)

    return pl.pallas_call(
        body, name="dilated_finish", grid=(s // TM,),
        in_specs=[spec] * 3 + [tab, tab], out_specs=[spec] * 3,
        out_shape=[jax.ShapeDtypeStruct((s, D_DIL), BF16)] * 3,
        compiler_params=_params(32),
    )(*grads, cos, sin)


def _place():
    x, y, c = lax.axis_index("x"), lax.axis_index("y"), lax.axis_index("c")
    return x, y, c, 2 * x + y


def _chip(k, c):
    return (k >> 1, k & 1, c)


def _half(ref, h):
    n = ref.shape[0] // 2
    return ref.at[pl.ds(h * n, n)]


class _BackgroundGather:
    def __init__(self, ins, outs, scratch):
        n = self.n = len(ins)
        self.ins, self.outs = ins, outs
        self.mine, self.landed, self.passed = scratch[0:3 * n:3], scratch[1:3 * n:3], scratch[2:3 * n:3]
        self.send_sem, self.recv_sem, self.local_sem = scratch[3 * n:3 * n + 3]
        x, y, self.c, self.k = _place()
        self.sibling = (x, y, 1 - self.c)

    @staticmethod
    def scratch_shapes(shards):
        shapes = []
        for a in shards:
            half = (N_CHIP - 1, a.shape[0] // 2, a.shape[1])
            shapes += [pltpu.VMEM(a.shape, a.dtype), pltpu.VMEM(half, a.dtype), pltpu.VMEM(half, a.dtype)]
        n = len(shards)
        return shapes + [pltpu.SemaphoreType.DMA((6 * n,)), pltpu.SemaphoreType.DMA((6 * n,)),
                         pltpu.SemaphoreType.DMA((8 * n,))]

    @staticmethod
    def out_shapes(shards):
        return [jax.ShapeDtypeStruct((N_CHIP,) + a.shape, a.dtype) for a in shards]

    def _remote(self, a, slot, src, dst, to):
        return pltpu.make_async_remote_copy(src_ref=src, dst_ref=dst, send_sem=self.send_sem.at[6 * a + slot],
                                            recv_sem=self.recv_sem.at[6 * a + slot], device_id=to, device_id_type=MESH)

    def _local(self, a, slot, src, dst):
        return pltpu.make_async_copy(src, dst, self.local_sem.at[8 * a + slot])

    def _ici(self, a, j):
        return self._remote(a, j - 1, _half(self.mine[a], self.c), self.landed[a].at[j - 1], _chip(self.k ^ j, self.c))

    def _to_sibling(self, a, j):
        return self._remote(a, 2 + j, self.landed[a].at[j - 1], self.passed[a].at[j - 1], self.sibling)

    def _own(self, a):
        return self._local(a, 0, self.ins[a], self.outs[a].at[self.k])

    def _load(self, a):
        return self._local(a, 1, self.ins[a], self.mine[a])

    def _store_landed(self, a, j):
        return self._local(a, 1 + j, self.landed[a].at[j - 1], _half(self.outs[a].at[self.k ^ j], self.c))

    def _store_passed(self, a, j):
        return self._local(a, 4 + j, self.passed[a].at[j - 1], _half(self.outs[a].at[self.k ^ j], 1 - self.c))

    def start(self):
        for a in range(self.n):
            self._own(a).start()
            self._load(a).start()
        for a in range(self.n):
            self._load(a).wait()
            for j in range(1, N_CHIP):
                self._ici(a, j).start()

    def forward(self):
        for j in range(1, N_CHIP):
            for a in range(self.n):
                self._ici(a, j).wait_recv()
                self._to_sibling(a, j).start()
                self._store_landed(a, j).start()

    def finish(self):
        for j in range(1, N_CHIP):
            for a in range(self.n):
                self._to_sibling(a, j).wait_recv()
                self._store_passed(a, j).start()
        for a in range(self.n):
            for j in range(1, N_CHIP):
                self._ici(a, j).wait_send()
                self._to_sibling(a, j).wait_send()
                self._store_landed(a, j).wait()
                self._store_passed(a, j).wait()
            self._own(a).wait()


def _all_gather(shards):
    n = len(shards)
    any_spec = pl.BlockSpec(memory_space=pl.ANY)

    def body(*refs):
        gather = _BackgroundGather(refs[:n], refs[n:2 * n], refs[2 * n:])
        gather.start()
        gather.forward()
        gather.finish()

    return pl.pallas_call(
        body, name="weights_all_gather",
        in_specs=[any_spec] * n, out_specs=[any_spec] * n,
        out_shape=_BackgroundGather.out_shapes(shards),
        scratch_shapes=_BackgroundGather.scratch_shapes(shards),
        compiler_params=_params(32),
    )(*shards)


def _reduce_scatter(g, core, name):
    n, r, c = g.shape
    hr = r // 2
    once = pl.Buffered(1)
    in_specs = [pl.BlockSpec((n, hr, c), lambda i, core_ref: (0, core_ref[0], 0), pipeline_mode=once),
                pl.BlockSpec((n, hr, c), lambda i, core_ref: (0, 1 - core_ref[0], 0), pipeline_mode=once)]

    def body(core_ref, mine_ref, other_ref, out_ref, from_core, sums_bf, from_chips, done, from_core2, send_sem, recv_sem):
        x, y, cc, k = _place()
        sibling = (x, y, 1 - cc)

        def copy(slot, src, dst, to):
            return pltpu.make_async_remote_copy(src_ref=src, dst_ref=dst, send_sem=send_sem.at[slot],
                                                recv_sem=recv_sem.at[slot], device_id=to, device_id_type=MESH)

        from_sibling = [copy(j, other_ref.at[k ^ j], from_core.at[k ^ j], sibling) for j in range(N_CHIP)]
        for j in (1, 2, 3, 0):
            from_sibling[j].start()
        sends = []
        for j in range(1, N_CHIP):
            from_sibling[j].wait()
            sums_bf[j - 1] = (mine_ref[k ^ j] + from_core[k ^ j]).astype(BF16)
            cp = copy(N_CHIP - 1 + j, sums_bf.at[j - 1], from_chips.at[j - 1], _chip(k ^ j, cc))
            cp.start()
            sends.append(cp)
        from_sibling[0].wait()
        red = mine_ref[k] + from_core[k]
        for j in range(1, N_CHIP):
            sends[j - 1].wait()
            red = red + from_chips[j - 1].astype(F32)
        done[...] = red
        last = copy(2 * N_CHIP - 1, done, from_core2, sibling)
        last.start()
        last.wait()
        row0 = pl.multiple_of(cc * hr, 8)
        row1 = pl.multiple_of((1 - cc) * hr, 8)
        out_ref[pl.ds(row0, hr), :] = red
        out_ref[pl.ds(row1, hr), :] = from_core2[...]

    grid_spec = pltpu.PrefetchScalarGridSpec(
        num_scalar_prefetch=1, grid=(1,), in_specs=in_specs,
        out_specs=pl.BlockSpec((r, c), lambda i, core_ref: (0, 0)),
        scratch_shapes=[pltpu.VMEM((n, hr, c), F32), pltpu.VMEM((N_CHIP - 1, hr, c), BF16),
                        pltpu.VMEM((N_CHIP - 1, hr, c), BF16), pltpu.VMEM((hr, c), F32), pltpu.VMEM((hr, c), F32),
                        pltpu.SemaphoreType.DMA((2 * N_CHIP,)), pltpu.SemaphoreType.DMA((2 * N_CHIP,))])
    return pl.pallas_call(
        body, name=name, grid_spec=grid_spec, out_shape=jax.ShapeDtypeStruct((r, c), F32),
        compiler_params=_params(56),
    )(core, g, g)


def _elementwise(fn, name, ins, n_out, rows):
    total, cols = ins[0].shape
    spec = pl.BlockSpec((rows, cols), lambda i: (i, 0))

    def body(*refs):
        res = fn(*[r[...] for r in refs[:len(ins)]])
        for o, v in zip(refs[len(ins):], res):
            o[...] = v

    return pl.pallas_call(
        body, name=name, grid=(total // rows,),
        in_specs=[spec] * len(ins), out_specs=[spec] * n_out,
        out_shape=[jax.ShapeDtypeStruct((total, cols), F32)] * n_out,
        compiler_params=_params(48),
    )(*ins)


def _adamw(w, g, m, v):
    m = ADAM_B1 * m + (1.0 - ADAM_B1) * g
    v = ADAM_B2 * v + (1.0 - ADAM_B2) * (g * g)
    m_hat = m / (1.0 - ADAM_B1 ** ADAM_STEP)
    v_hat = v / (1.0 - ADAM_B2 ** ADAM_STEP)
    delta = -ADAM_LR * (m_hat / (jnp.sqrt(v_hat) + ADAM_EPS) + ADAM_WD * w)
    return delta, m, v


def _reduce_and_update(grads, weights, moms, vels):
    core = lax.axis_index("c").astype(jnp.int32).reshape(1)
    full = [_reduce_scatter(g, core, f"grads_reduce_scatter_{a}") for a, g in enumerate(grads)]
    out = []
    for a, (g, w, m, v) in enumerate(zip(full, weights, moms, vels)):
        rows = g.shape[0] // 2
        out.append((g,) + tuple(_elementwise(lambda gg, ww, mm, vv: _adamw(ww, gg, mm, vv), f"adamw_{a}", [g, w, m, v], 3, rows)))
    return out


def _reduce_vectors(part, w, m, v):
    n_dev = 8

    def body(p_ref, w_ref, m_ref, v_ref, g_ref, d_ref, nm_ref, nv_ref, buf, send_sem, recv_sem):
        x, y, c, _ = _place()
        me = 4 * x + 2 * y + c
        buf[me] = p_ref[...]
        sends = []
        for off in range(1, n_dev):
            peer = me ^ off
            cp = pltpu.make_async_remote_copy(src_ref=p_ref, dst_ref=buf.at[me], send_sem=send_sem.at[off - 1],
                                              recv_sem=recv_sem.at[off - 1], device_id=(peer >> 2, (peer >> 1) & 1, peer & 1),
                                              device_id_type=MESH)
            cp.start()
            sends.append(cp)
        for off in range(1, n_dev):
            peer = me ^ off
            pltpu.make_async_remote_copy(src_ref=p_ref, dst_ref=buf.at[peer], send_sem=send_sem.at[off - 1],
                                         recv_sem=recv_sem.at[off - 1], device_id=(peer >> 2, (peer >> 1) & 1, peer & 1),
                                         device_id_type=MESH).wait_recv()
        for cp in sends:
            cp.wait_send()
        g = buf[0]
        for d in range(1, n_dev):
            g = g + buf[d]
        g_ref[...] = g
        delta, nm, nv = _adamw(w_ref[...], g, m_ref[...], v_ref[...])
        d_ref[...] = delta
        nm_ref[...] = nm
        nv_ref[...] = nv

    vm = pl.BlockSpec(memory_space=pltpu.VMEM)
    return pl.pallas_call(
        body, name="gains_all_reduce",
        in_specs=[vm] * 4, out_specs=[vm] * 4,
        out_shape=[jax.ShapeDtypeStruct(part.shape, F32)] * 4,
        scratch_shapes=[pltpu.VMEM((n_dev,) + part.shape, F32), pltpu.SemaphoreType.DMA((n_dev - 1,)),
                        pltpu.SemaphoreType.DMA((n_dev - 1,))],
    )(part, w, m, v)


def _pad_row(a):
    a = a.reshape(1, -1)
    return jnp.pad(a, ((0, 0), (0, D_MODEL - a.shape[1])))


def kernel(x, ffn1_norm, ffn1_w_gate, ffn1_w_up, ffn1_w_down, mix_norm, w_in, sb_out_norm, dil_out_norm, w_out, ffn2_norm, ffn2_w_gate, ffn2_w_up, ffn2_w_down, final_norm, loss_target, m_ffn1_norm, m_ffn1_w_gate, m_ffn1_w_up, m_ffn1_w_down, m_mix_norm, m_w_in, m_sb_out_norm, m_dil_out_norm, m_w_out, m_ffn2_norm, m_ffn2_w_gate, m_ffn2_w_up, m_ffn2_w_down, m_final_norm, v_ffn1_norm, v_ffn1_w_gate, v_ffn1_w_up, v_ffn1_w_down, v_mix_norm, v_w_in, v_sb_out_norm, v_dil_out_norm, v_w_out, v_ffn2_norm, v_ffn2_w_gate, v_ffn2_w_up, v_ffn2_w_down, v_final_norm):
    x = x[0]
    target = loss_target[0]
    s = x.shape[0]
    gf = final_norm.reshape(1, D_MODEL)
    cos, sin = _rope_tables(s)

    flip = lambda a: a[0].T
    shard = lambda w: w[0].astype(BF16)
    shard_t = lambda w: flip(w).astype(BF16)
    wg1, wu1, wd1 = _all_gather([shard_t(ffn1_w_gate), shard_t(ffn1_w_up), shard(ffn1_w_down)])

    x1, hm, saved1, (win, wout, wd2) = _ffn1_fwd(x, ffn1_norm, mix_norm, (wg1, wu1), wd1,
                                                 [shard(w_in), shard(w_out), shard(ffn2_w_down)])
    wout = wout.reshape(D_MODEL, D_MODEL)
    qkv, (wg2, wu2) = _proj_fwd(hm, win, cos, sin, [shard_t(ffn2_w_gate), shard_t(ffn2_w_up)])
    o_sb = _sb_fwd(qkv)
    o_dl, lse = _dilated_fwd(qkv)
    x2 = _outproj_fwd(o_sb, o_dl, sb_out_norm, dil_out_norm, x1, wout)
    dx3, st_final, saved2 = _ffn2_fwd_loss(x2, ffn2_norm, gf, target, (wg2, wu2), wd2)

    dx2, dwg2, dwu2, dwd2, st_ffn2 = _ffn_bwd(x2, ffn2_norm, dx3, saved2, (wg2, wu2), wd2, 1)
    do_sb, do_dl, delta_dl, dwout, st_out = _outproj_bwd(dx2, o_sb, o_dl, sb_out_norm, dil_out_norm, wout)
    dq_sb, dk_sb, dv_sb = _sb_bwd(qkv, o_sb, do_sb)
    dq_dl, dk_dl, dv_dl = _dilated_finish(_dilated_bwd(qkv, delta_dl, lse, do_dl), cos, sin)
    dqkv = jnp.concatenate([dq_sb, dk_sb, dv_sb, dq_dl, dk_dl, dv_dl], axis=1)
    dx1, dwin, st_mix = _proj_bwd(x1, mix_norm, dqkv, win, dx2)
    grad_x, dwg1, dwu1, dwd1, st_ffn1 = _ffn_bwd(x, ffn1_norm, dx1, saved1, (wg1, wu1), wd1, 0)

    names = ["ffn1_w_gate", "ffn1_w_up", "ffn1_w_down", "w_in", "w_out", "ffn2_w_gate", "ffn2_w_up", "ffn2_w_down"]
    grads = [dwg1, dwu1, dwd1, dwin, dwout.reshape(N_CHIP, OUTB, D_MODEL), dwg2, dwu2, dwd2]
    flipped = {"ffn1_w_gate", "ffn1_w_up", "ffn2_w_gate", "ffn2_w_up"}
    place = lambda n, a: flip(a) if n in flipped else a[0]
    weights = [place(n, a) for n, a in zip(names, [ffn1_w_gate, ffn1_w_up, ffn1_w_down, w_in, w_out, ffn2_w_gate, ffn2_w_up, ffn2_w_down])]
    moms = [place(n, a) for n, a in zip(names, [m_ffn1_w_gate, m_ffn1_w_up, m_ffn1_w_down, m_w_in, m_w_out, m_ffn2_w_gate, m_ffn2_w_up, m_ffn2_w_down])]
    vels = [place(n, a) for n, a in zip(names, [v_ffn1_w_gate, v_ffn1_w_up, v_ffn1_w_down, v_w_in, v_w_out, v_ffn2_w_gate, v_ffn2_w_up, v_ffn2_w_down])]
    mats = {n: tuple((t.T if n in flipped else t)[None] for t in r)
            for n, r in zip(names, _reduce_and_update(grads, weights, moms, vels))}

    vec_names = ["ffn1_norm", "mix_norm", "sb_out_norm", "dil_out_norm", "ffn2_norm", "final_norm"]
    part = jnp.concatenate([st_ffn1[0:1], st_mix[0:1], _pad_row(st_out[0]), _pad_row(st_out[1]), st_ffn2[0:1],
                            st_final[0:1], st_final[1:2], jnp.zeros((1, D_MODEL), F32)], axis=0)
    pack = lambda arrs: jnp.concatenate([_pad_row(a) for a in arrs] + [jnp.zeros((2, D_MODEL), F32)], axis=0)
    g_vec, d_vec, m_vec, v_vec = _reduce_vectors(
        part,
        pack([ffn1_norm, mix_norm, sb_out_norm, dil_out_norm, ffn2_norm, final_norm]),
        pack([m_ffn1_norm, m_mix_norm, m_sb_out_norm, m_dil_out_norm, m_ffn2_norm, m_final_norm]),
        pack([v_ffn1_norm, v_mix_norm, v_sb_out_norm, v_dil_out_norm, v_ffn2_norm, v_final_norm]))
    like = {"ffn1_norm": ffn1_norm, "mix_norm": mix_norm, "sb_out_norm": sb_out_norm, "dil_out_norm": dil_out_norm,
            "ffn2_norm": ffn2_norm, "final_norm": final_norm}
    vecs = {n: tuple(t[i, :like[n].size].reshape(like[n].shape) for t in (g_vec, d_vec, m_vec, v_vec))
            for i, n in enumerate(vec_names)}
    loss = 0.5 * jnp.sum(g_vec[6]) / D_MODEL

    order = ["ffn1_norm", "ffn1_w_gate", "ffn1_w_up", "ffn1_w_down", "mix_norm", "w_in", "sb_out_norm", "dil_out_norm",
             "w_out", "ffn2_norm", "ffn2_w_gate", "ffn2_w_up", "ffn2_w_down", "final_norm"]
    both = {**mats, **vecs}
    return (loss, grad_x[None], *[both[n][0] for n in order], *[both[n][1] for n in order],
            *[both[n][2] for n in order], *[both[n][3] for n in order])
```

```python
import functools

import jax
import jax.numpy as jnp
from jax import lax
from jax.experimental import pallas as pl
from jax.experimental.pallas import tpu as pltpu

D_MODEL = 1024
D_FF = 2816
HEAD_DIM = 64
D_SB = 512
D_DIL = 512
D_IN = 3072
N_CHIP = 4
FFB = D_FF // N_CHIP
INB = D_IN // N_CHIP
OUTB = D_MODEL // N_CHIP
BLK = 128
LANES = 128
DILATIONS = (1, 4, 16)
ROPE_THETA = 10000.0
RMS_EPS = 1e-6
SCALE = HEAD_DIM ** -0.5
NEG = -1e30
DEAD = -104.0
ADAM_LR = 0.001
ADAM_B1 = 0.9
ADAM_B2 = 0.999
ADAM_EPS = 1e-08
ADAM_WD = 0.01
ADAM_STEP = 10
MESH = pl.DeviceIdType.MESH
F32 = jnp.float32
BF16 = jnp.bfloat16
TM = 512


def _params(vmem_mb):
    return pltpu.CompilerParams(vmem_limit_bytes=vmem_mb << 20)


def _dot(a, b):
    return jnp.dot(a, b, preferred_element_type=F32)


def _dot_nt(a, b):
    return lax.dot_general(a, b, (((1,), (1,)), ((), ())), preferred_element_type=F32)


def _dot_tn(a, b):
    return lax.dot_general(a, b, (((0,), (0,)), ((), ())), preferred_element_type=F32)


def _rms_fwd(x, g):
    r = lax.rsqrt(jnp.mean(x * x, axis=-1, keepdims=True) + RMS_EPS)
    xh = x * r
    return xh * g, xh, r


def _rms_bwd(dy, xh, r, g):
    dyg = dy * g
    dx = r * (dyg - xh * jnp.mean(dyg * xh, axis=-1, keepdims=True))
    return dx, jnp.sum(dy * xh, axis=0, keepdims=True)


def _split_bf16(a):
    hi = a.astype(BF16)
    return hi, (a - hi.astype(F32)).astype(BF16)


def _dot_split(a, b2):
    hi, lo = _split_bf16(a)
    return _dot(jnp.concatenate([hi, lo], axis=1), b2)


def _ffn_weight_specs():
    return [pl.BlockSpec((None, FFB, D_MODEL), lambda i, j: (j, 0, 0))] * 3


def _ffn_saved(s):
    hidden = jax.ShapeDtypeStruct((N_CHIP, s, FFB), BF16)
    hid = pl.BlockSpec((None, TM, FFB), lambda i, j: (j, i, 0))
    row = pl.BlockSpec((TM, D_MODEL), lambda i, j: (i, 0))
    return [row, hid, hid, hid], [jax.ShapeDtypeStruct((s, D_MODEL), BF16), hidden, hidden, hidden]


def _ffn_accumulate(h_ref, acc_scr, wg_ref, wu_ref, wd_ref, a_ref, b_ref, act_ref):
    h = h_ref[...]
    a = _dot_nt(h, wg_ref[...])
    b = _dot_nt(h, wu_ref[...])
    act = ((a * jax.nn.sigmoid(a)) * b).astype(BF16)
    a_ref[...] = a.astype(BF16)
    b_ref[...] = b.astype(BF16)
    act_ref[...] = act
    acc_scr[...] += _dot(act, wd_ref[...])


def _host_gather_before(gather, i, j, steps):
    @pl.when((i == 0) & (j == 0))
    def _():
        gather.start()

    @pl.when((i == (3 * steps) // 4) & (j == 0))
    def _():
        gather.forward()


def _host_gather_after(gather, i, j, steps):
    @pl.when((i == steps - 1) & (j == N_CHIP - 1))
    def _():
        gather.finish()


def _ffn1_fwd(x, g1, gmix, gu, wd, later_shards):
    s = x.shape[0]
    row = pl.BlockSpec((TM, D_MODEL), lambda i, j: (i, 0))
    vec = pl.BlockSpec((1, D_MODEL), lambda i, j: (0, 0))
    saved_specs, saved_shapes = _ffn_saved(s)
    n = len(later_shards)
    any_spec = pl.BlockSpec(memory_space=pl.ANY)

    def body(*refs):
        x_ref, g_ref, gm_ref, wg_ref, wu_ref, wd_ref = refs[:6]
        shard_refs, refs = refs[6:6 + n], refs[6 + n:]
        x1_ref, hm_ref, h_ref, a_ref, b_ref, act_ref = refs[:6]
        gathered_refs, acc_scr, gather_scratch = refs[6:6 + n], refs[6 + n], refs[7 + n:]
        gather = _BackgroundGather(shard_refs, gathered_refs, gather_scratch)
        i, j = pl.program_id(0), pl.program_id(1)
        _host_gather_before(gather, i, j, s // TM)

        @pl.when(j == 0)
        def _():
            h, _, _ = _rms_fwd(x_ref[...], g_ref[...])
            h_ref[...] = h.astype(BF16)
            acc_scr[...] = jnp.zeros_like(acc_scr)

        _ffn_accumulate(h_ref, acc_scr, wg_ref, wu_ref, wd_ref, a_ref, b_ref, act_ref)

        @pl.when(j == N_CHIP - 1)
        def _():
            x1 = x_ref[...] + 0.5 * acc_scr[...]
            x1_ref[...] = x1
            hm, _, _ = _rms_fwd(x1, gm_ref[...])
            hm_ref[...] = hm.astype(BF16)

        _host_gather_after(gather, i, j, s // TM)

    x1, hm, h, a, b, act, *gathered = pl.pallas_call(
        body, name="ffn1_fwd", grid=(s // TM, N_CHIP),
        in_specs=[row, vec, vec] + _ffn_weight_specs() + [any_spec] * n,
        out_specs=[row, row] + saved_specs + [any_spec] * n,
        out_shape=([jax.ShapeDtypeStruct((s, D_MODEL), F32), jax.ShapeDtypeStruct((s, D_MODEL), BF16)] + saved_shapes
                   + _BackgroundGather.out_shapes(later_shards)),
        scratch_shapes=[pltpu.VMEM((TM, D_MODEL), F32)] + _BackgroundGather.scratch_shapes(later_shards),
        compiler_params=_params(58),
    )(x, g1, gmix, gu[0], gu[1], wd, *later_shards)
    return x1, hm, [h, a, b, act], gathered


def _ffn2_fwd_loss(x2, g2, gf, target, gu, wd):
    s = x2.shape[0]
    row = pl.BlockSpec((TM, D_MODEL), lambda i, j: (i, 0))
    vec = pl.BlockSpec((1, D_MODEL), lambda i, j: (0, 0))
    stat = pl.BlockSpec((8, D_MODEL), lambda i, j: (0, 0))
    saved_specs, saved_shapes = _ffn_saved(s)

    def body(x_ref, g_ref, gf_ref, t_ref, wg_ref, wu_ref, wd_ref, dx_ref, st_ref, h_ref, a_ref, b_ref, act_ref, acc_scr):
        i, j = pl.program_id(0), pl.program_id(1)

        @pl.when((i == 0) & (j == 0))
        def _():
            st_ref[...] = jnp.zeros_like(st_ref)

        @pl.when(j == 0)
        def _():
            h, _, _ = _rms_fwd(x_ref[...], g_ref[...])
            h_ref[...] = h.astype(BF16)
            acc_scr[...] = jnp.zeros_like(acc_scr)

        _ffn_accumulate(h_ref, acc_scr, wg_ref, wu_ref, wd_ref, a_ref, b_ref, act_ref)

        @pl.when(j == N_CHIP - 1)
        def _():
            x3 = x_ref[...] + 0.5 * acc_scr[...]
            y, xh, r = _rms_fwd(x3, gf_ref[...])
            err = y - t_ref[...]
            dx, dg = _rms_bwd(err * (1.0 / D_MODEL), xh, r, gf_ref[...])
            dx_ref[...] = dx
            st_ref[0:1, :] += dg
            st_ref[1:2, :] += jnp.sum(err * err, axis=0, keepdims=True)

    dx3, st, *saved = pl.pallas_call(
        body, name="ffn2_fwd_loss", grid=(s // TM, N_CHIP),
        in_specs=[row, vec, vec, row] + _ffn_weight_specs(),
        out_specs=[row, stat] + saved_specs,
        out_shape=[jax.ShapeDtypeStruct((s, D_MODEL), F32), jax.ShapeDtypeStruct((8, D_MODEL), F32)] + saved_shapes,
        scratch_shapes=[pltpu.VMEM((TM, D_MODEL), F32)],
        compiler_params=_params(56),
    )(x2, g2, gf, target, gu[0], gu[1], wd)
    return dx3, st, saved


def _ffn_bwd(xin, g, dy, saved, gu, wd, f):
    s = xin.shape[0]
    hb, gate, up, act = saved
    row = pl.BlockSpec((TM, D_MODEL), lambda i, j: (i, 0))
    vec = pl.BlockSpec((1, D_MODEL), lambda i, j: (0, 0))
    stat = pl.BlockSpec((8, D_MODEL), lambda i, j: (0, 0))
    hid = pl.BlockSpec((None, TM, FFB), lambda i, j: (j, i, 0))

    def body(x_ref, g_ref, dy_ref, a_ref, b_ref, wg_ref, wu_ref, wd_ref, out_ref, dyh_ref, da_ref, db_ref, st_ref, dh_scr):
        i, j = pl.program_id(0), pl.program_id(1)

        @pl.when((i == 0) & (j == 0))
        def _():
            st_ref[...] = jnp.zeros_like(st_ref)

        @pl.when(j == 0)
        def _():
            dyh_ref[...] = (0.5 * dy_ref[...]).astype(BF16)
            dh_scr[...] = jnp.zeros_like(dh_scr)

        a = a_ref[...].astype(F32)
        b = b_ref[...].astype(F32)
        sg = jax.nn.sigmoid(a)
        dact = _dot_nt(dyh_ref[...], wd_ref[...])
        dab = (dact * b * (sg * (1.0 + a * (1.0 - sg)))).astype(BF16)
        dbb = (dact * (a * sg)).astype(BF16)
        da_ref[...] = dab
        db_ref[...] = dbb
        dh_scr[...] += _dot(dab, wg_ref[...]) + _dot(dbb, wu_ref[...])

        @pl.when(j == N_CHIP - 1)
        def _():
            _, xh, r = _rms_fwd(x_ref[...], g_ref[...])
            dx, dg = _rms_bwd(dh_scr[...], xh, r, g_ref[...])
            out_ref[...] = dy_ref[...] + dx
            st_ref[0:1, :] += dg

    hidden = jax.ShapeDtypeStruct((N_CHIP, s, FFB), BF16)
    dx, dyh, da, db, st = pl.pallas_call(
        body, name=f"ffn{f + 1}_bwd_dx", grid=(s // TM, N_CHIP),
        in_specs=[row, vec, row, hid, hid] + _ffn_weight_specs(),
        out_specs=[row, row, hid, hid, stat],
        out_shape=[jax.ShapeDtypeStruct((s, D_MODEL), F32), jax.ShapeDtypeStruct((s, D_MODEL), BF16),
                   hidden, hidden, jax.ShapeDtypeStruct((8, D_MODEL), F32)],
        scratch_shapes=[pltpu.VMEM((TM, D_MODEL), F32)],
        compiler_params=_params(56),
    )(xin, g, dy, gate, up, gu[0], gu[1], wd)

    tok = pl.BlockSpec((TM, D_MODEL), lambda j, i: (i, 0))
    hid2 = pl.BlockSpec((None, TM, FFB), lambda j, i: (j, i, 0))
    gspecs = [pl.BlockSpec((None, FFB, D_MODEL), lambda j, i: (j, 0, 0))] * 3

    def wbody(h_ref, dyh_ref, da_ref, db_ref, act_ref, dwg_ref, dwu_ref, dwd_ref):
        @pl.when(pl.program_id(1) == 0)
        def _():
            dwg_ref[...] = jnp.zeros_like(dwg_ref)
            dwu_ref[...] = jnp.zeros_like(dwu_ref)
            dwd_ref[...] = jnp.zeros_like(dwd_ref)

        hb = h_ref[...]
        dwg_ref[...] += _dot_tn(da_ref[...], hb)
        dwu_ref[...] += _dot_tn(db_ref[...], hb)
        dwd_ref[...] += _dot_tn(act_ref[...], dyh_ref[...])

    dwg, dwu, dwd = pl.pallas_call(
        wbody, name=f"ffn{f + 1}_bwd_dw", grid=(N_CHIP, s // TM),
        in_specs=[tok, tok, hid2, hid2, hid2], out_specs=gspecs,
        out_shape=[jax.ShapeDtypeStruct((N_CHIP, FFB, D_MODEL), F32)] * 3,
        compiler_params=_params(48),
    )(hb, dyh, da, db, act)
    return dx, dwg, dwu, dwd, st


def _rope_tables(s):
    half = HEAD_DIM // 2
    inv_freq = ROPE_THETA ** (-jnp.arange(half, dtype=F32) / half)
    ang = jnp.arange(s).astype(F32)[:, None] * inv_freq[None, :]
    cos, sin = jnp.cos(ang), jnp.sin(ang)
    cos2 = jnp.concatenate([cos, cos], axis=-1)
    sin2 = jnp.concatenate([-sin, sin], axis=-1)
    return jnp.tile(cos2, (1, LANES // HEAD_DIM)), jnp.tile(sin2, (1, LANES // HEAD_DIM))


def _rotate(t, cos, sin_signed):
    lane = lax.broadcasted_iota(jnp.int32, t.shape, 1)
    first = (lane % HEAD_DIM) < (HEAD_DIM // 2)
    partner = jnp.where(first, pltpu.roll(t, LANES - HEAD_DIM // 2, 1), pltpu.roll(t, HEAD_DIM // 2, 1))
    return t * cos + partner * sin_signed


def _proj_fwd(hm, win, cos, sin, later_shards):
    s = hm.shape[0]
    n_sub = INB // LANES
    first_rot, last_rot = (3 * D_SB) // LANES, (3 * D_SB + 2 * D_DIL) // LANES
    n = len(later_shards)
    any_spec = pl.BlockSpec(memory_space=pl.ANY)

    def body(*refs):
        h_ref, w_ref, c_ref, s_ref = refs[:4]
        shard_refs, o_ref, gathered_refs, gather_scratch = refs[4:4 + n], refs[4 + n], refs[5 + n:5 + 2 * n], refs[5 + 2 * n:]
        gather = _BackgroundGather(shard_refs, gathered_refs, gather_scratch)
        i, j = pl.program_id(0), pl.program_id(1)
        _host_gather_before(gather, i, j, s // TM)
        r = _dot(h_ref[...], w_ref[...])
        for c in range(n_sub):
            t = r[:, c * LANES:(c + 1) * LANES]
            col = j * n_sub + c
            rot = (col >= first_rot) & (col < last_rot)
            lanes = slice(c * LANES, (c + 1) * LANES)

            @pl.when(rot)
            def _():
                o_ref[:, lanes] = _rotate(t, c_ref[...], s_ref[...]).astype(BF16)

            @pl.when(jnp.logical_not(rot))
            def _():
                o_ref[:, lanes] = t.astype(BF16)

        _host_gather_after(gather, i, j, s // TM)

    qkv, *gathered = pl.pallas_call(
        body, name="proj_fwd", grid=(s // TM, N_CHIP),
        in_specs=[pl.BlockSpec((TM, D_MODEL), lambda i, j: (i, 0)),
                  pl.BlockSpec((None, D_MODEL, INB), lambda i, j: (j, 0, 0)),
                  pl.BlockSpec((TM, LANES), lambda i, j: (i, 0)),
                  pl.BlockSpec((TM, LANES), lambda i, j: (i, 0))] + [any_spec] * n,
        out_specs=[pl.BlockSpec((TM, INB), lambda i, j: (i, j))] + [any_spec] * n,
        out_shape=[jax.ShapeDtypeStruct((s, D_IN), BF16)] + _BackgroundGather.out_shapes(later_shards),
        scratch_shapes=_BackgroundGather.scratch_shapes(later_shards),
        compiler_params=_params(48),
    )(hm, win, cos, sin, *later_shards)
    return qkv, gathered


def _proj_bwd(x1, gmix, dqkv_sb, dqkv_dl, win, dx2):
    s = x1.shape[0]
    row = pl.BlockSpec((TM, D_MODEL), lambda i, j: (i, 0))
    vec = pl.BlockSpec((1, D_MODEL), lambda i, j: (0, 0))
    per_group = N_CHIP // 2

    def body(x_ref, g_ref, dsb_ref, ddl_ref, w_ref, dx2_ref, out_ref, dw_ref, st_ref, h_scr, dh_scr, dq_ref):
        i, j = pl.program_id(0), pl.program_id(1)

        @pl.when(j < per_group)
        def _():
            dq_ref[...] = dsb_ref[...]

        @pl.when(j >= per_group)
        def _():
            dq_ref[...] = ddl_ref[...]

        @pl.when((i == 0) & (j == 0))
        def _():
            st_ref[...] = jnp.zeros_like(st_ref)
            dw_ref[...] = jnp.zeros_like(dw_ref)

        @pl.when(j == 0)
        def _():
            h, _, _ = _rms_fwd(x_ref[...], g_ref[...])
            h_scr[...] = h.astype(BF16)
            dh_scr[...] = jnp.zeros_like(dh_scr)

        dq = dq_ref[...]
        dw_ref[j] += _dot_tn(h_scr[...], dq)
        dh_scr[...] += _dot_nt(dq, w_ref[...])

        @pl.when(j == N_CHIP - 1)
        def _():
            _, xh, r = _rms_fwd(x_ref[...], g_ref[...])
            dx, dg = _rms_bwd(dh_scr[...], xh, r, g_ref[...])
            out_ref[...] = dx2_ref[...] + dx
            st_ref[0:1, :] += dg

    return pl.pallas_call(
        body, name="proj_bwd", grid=(s // TM, N_CHIP),
        in_specs=[row, vec,
                  pl.BlockSpec((TM, INB), lambda i, j: (i, jnp.minimum(j, per_group - 1))),
                  pl.BlockSpec((TM, INB), lambda i, j: (i, jnp.maximum(j - per_group, 0))),
                  pl.BlockSpec((None, D_MODEL, INB), lambda i, j: (j, 0, 0)), row],
        out_specs=[row, pl.BlockSpec((N_CHIP, D_MODEL, INB), lambda i, j: (0, 0, 0)),
                   pl.BlockSpec((8, D_MODEL), lambda i, j: (0, 0))],
        out_shape=[jax.ShapeDtypeStruct((s, D_MODEL), F32),
                   jax.ShapeDtypeStruct((N_CHIP, D_MODEL, INB), F32),
                   jax.ShapeDtypeStruct((8, D_MODEL), F32)],
        scratch_shapes=[pltpu.VMEM((TM, D_MODEL), BF16), pltpu.VMEM((TM, D_MODEL), F32), pltpu.VMEM((TM, INB), BF16)],
        compiler_params=_params(56),
    )(x1, gmix, dqkv_sb, dqkv_dl, win, dx2)


def _outproj_fwd(o_sb, o_dl, g_sb, g_dl, x1, wout):
    s = x1.shape[0]
    half = pl.BlockSpec((TM, D_SB), lambda i: (i, 0))
    row = pl.BlockSpec((TM, D_MODEL), lambda i: (i, 0))
    vec = pl.BlockSpec((1, D_SB), lambda i: (0, 0))

    def body(a_ref, b_ref, ga_ref, gb_ref, x_ref, w_ref, o_ref):
        ma, _, _ = _rms_fwd(a_ref[...], ga_ref[...])
        mb, _, _ = _rms_fwd(b_ref[...], gb_ref[...])
        o_ref[...] = (x_ref[...] + _dot(ma.astype(BF16), w_ref[0:D_SB, :])
                      + _dot(mb.astype(BF16), w_ref[D_SB:D_MODEL, :]))

    return pl.pallas_call(
        body, name="outproj_fwd", grid=(s // TM,),
        in_specs=[half, half, vec, vec, row, pl.BlockSpec((D_MODEL, D_MODEL), lambda i: (0, 0))],
        out_specs=row, out_shape=jax.ShapeDtypeStruct((s, D_MODEL), F32),
        compiler_params=_params(32),
    )(o_sb, o_dl, g_sb, g_dl, x1, wout)


def _outproj_bwd(dx2, o_sb, o_dl, g_sb, g_dl, wout):
    s = dx2.shape[0]
    half = pl.BlockSpec((TM, D_SB), lambda i: (i, 0))
    row = pl.BlockSpec((TM, D_MODEL), lambda i: (i, 0))
    vec = pl.BlockSpec((1, D_SB), lambda i: (0, 0))
    full = pl.BlockSpec((D_MODEL, D_MODEL), lambda i: (0, 0))

    def body(dy_ref, a_ref, b_ref, ga_ref, gb_ref, w_ref, da_ref, db_ref, dl_ref, dw_ref, st_ref):
        @pl.when(pl.program_id(0) == 0)
        def _():
            dw_ref[...] = jnp.zeros_like(dw_ref)
            st_ref[...] = jnp.zeros_like(st_ref)

        dy = dy_ref[...].astype(BF16)
        dm = _dot_nt(dy, w_ref[...])
        ma, xa, ra = _rms_fwd(a_ref[...], ga_ref[...])
        mb, xb, rb = _rms_fwd(b_ref[...], gb_ref[...])
        dw_ref[0:D_SB, :] += _dot_tn(ma.astype(BF16), dy)
        dw_ref[D_SB:D_MODEL, :] += _dot_tn(mb.astype(BF16), dy)
        da, dga = _rms_bwd(dm[:, 0:D_SB], xa, ra, ga_ref[...])
        db, dgb = _rms_bwd(dm[:, D_SB:D_MODEL], xb, rb, gb_ref[...])
        da_ref[...] = da
        db_ref[...] = db
        r = lax.broadcasted_iota(jnp.int32, (LANES, LANES), 0) >= HEAD_DIM
        c = lax.broadcasted_iota(jnp.int32, (LANES, LANES), 1) >= HEAD_DIM
        same_head = jnp.where(r == c, 1.0, 0.0).astype(BF16)
        same_head = jnp.concatenate([same_head, same_head], axis=0)
        prod = db * b_ref[...]
        for k in range(D_DIL // LANES):
            lanes = slice(k * LANES, (k + 1) * LANES)
            dl_ref[:, lanes] = _dot_split(prod[:, lanes], same_head)
        st_ref[0:1, :] += dga
        st_ref[1:2, :] += dgb

    return pl.pallas_call(
        body, name="outproj_bwd", grid=(s // TM,),
        in_specs=[row, half, half, vec, vec, full],
        out_specs=[half, half, half, full, pl.BlockSpec((8, D_SB), lambda i: (0, 0))],
        out_shape=[jax.ShapeDtypeStruct((s, D_SB), F32), jax.ShapeDtypeStruct((s, D_SB), F32),
                   jax.ShapeDtypeStruct((s, D_DIL), F32),
                   jax.ShapeDtypeStruct((D_MODEL, D_MODEL), F32), jax.ShapeDtypeStruct((8, D_SB), F32)],
        compiler_params=_params(48),
    )(dx2, o_sb, o_dl, g_sb, g_dl, wout)


def _head_masks():
    lane = lax.broadcasted_iota(jnp.int32, (BLK, LANES), 1)
    return [lane < HEAD_DIM, lane >= HEAD_DIM]


def _keep(mask, a):
    return a * jnp.where(mask, 1.0, 0.0).astype(a.dtype)


def _suffix_matrices():
    r = lax.broadcasted_iota(jnp.int32, (2 * BLK, BLK), 0) & (BLK - 1)
    c = lax.broadcasted_iota(jnp.int32, (2 * BLK, BLK), 1)
    ones = jnp.ones((2 * BLK, BLK), BF16)
    excl = jnp.concatenate([(r > c).astype(BF16), ones], axis=1)
    incl = jnp.concatenate([(r >= c).astype(BF16), ones], axis=1)
    return excl, incl


def _blk(i):
    return pl.ds(pl.multiple_of(i * BLK, BLK), BLK)


def _alive(carry_m):
    return (jnp.max(carry_m) > DEAD).astype(jnp.int32)


def _more_keys(last, carry):
    return (carry[0] * SB_KB <= last) & (carry[1] > 0)


def _stack_heads(a):
    masks = _head_masks()
    return jnp.concatenate([_keep(masks[0], a), _keep(masks[1], a)], axis=0)


def _unstack_heads(a2):
    return jnp.where(_head_masks()[0], a2[:BLK], a2[BLK:])


def _head_rowsum(a):
    masks = _head_masks()
    return jnp.concatenate([jnp.sum(jnp.where(m, a, 0.0), axis=1, keepdims=True) for m in masks], axis=0)


SB_QB = 2
SB_ROWS = SB_QB * 2 * BLK
SB_KB = 4
PAST_START = 1 << 30


def _sb_rows(ref, i0, cast=None):
    tiles = [ref[_blk(i0 + t), :] for t in range(SB_QB)]
    return jnp.concatenate([_stack_heads(t if cast is None else t.astype(cast)) for t in tiles], axis=0)


_SB_LATER_ROWS = (SB_QB - 1) * 2 * BLK


def _put_rows(full, rows, part):
    return part if rows.start == 0 else jnp.concatenate([full[:rows.start], part], axis=0)


def _sb_scores(q2, k, i, j, carry_m, u_excl):
    r = lax.broadcasted_iota(jnp.int32, (q2.shape[0], BLK), 0)
    row = (r & (BLK - 1)) + ((r >> 8) << 7)
    col = lax.broadcasted_iota(jnp.int32, (q2.shape[0], BLK), 1)
    valid = (jnp.where(j >= 0, j * BLK, PAST_START) + col) < (i * BLK + row)
    z = _dot_nt(q2, k) * SCALE
    sp = jnp.maximum(z, 0.0) + jnp.log(1.0 + jnp.exp(-jnp.abs(z)))
    log_stay = jnp.where(valid, -sp, 0.0)
    log_beta = z - sp
    sums = _dot_split(log_stay, u_excl)
    later = carry_m + sums[:, :BLK]
    w = jnp.where(valid, jnp.exp(log_beta + later), 0.0)
    return valid, log_beta, w, carry_m + sums[:, BLK:]


def _sb_fwd(qkv):
    s = qkv.shape[0]
    nq = s // BLK
    pairs = D_SB // LANES
    col = lambda off: pl.BlockSpec((s, LANES), lambda p: (0, off + p))

    def body(q_ref, k_ref, v_ref, o_ref):
        u_excl, _ = _suffix_matrices()
        zero = jnp.zeros((SB_ROWS, LANES), F32)

        def q_block(ib, _):
            i = ib * SB_QB
            last = i + SB_QB - 1
            q2 = _sb_rows(q_ref, i)

            def trip(jj, carry_m, acc, first):
                for t in range(SB_KB):
                    j = last - jj * SB_KB - t
                    at = _blk(jnp.maximum(j, 0))
                    rows = slice(_SB_LATER_ROWS, SB_ROWS) if first and t == 0 else slice(0, SB_ROWS)
                    base = i + rows.start // (2 * BLK)
                    _, _, w, part = _sb_scores(q2[rows], k_ref[at, :], base, j, carry_m[rows], u_excl)
                    carry_m = _put_rows(carry_m, rows, part)
                    acc = _put_rows(acc, rows, acc[rows] + _dot(w.astype(BF16), v_ref[at, :]))
                return carry_m, acc

            def k_block(carry):
                carry_m, acc = trip(carry[0], carry[2], carry[3], False)
                return carry[0] + 1, _alive(carry_m), carry_m, acc

            carry_m, acc = trip(0, zero, zero, True)
            _, _, _, acc = lax.while_loop(functools.partial(_more_keys, last), k_block,
                                          (jnp.int32(1), _alive(carry_m), carry_m, acc))
            for t in range(SB_QB):
                o_ref[_blk(i + t), :] = _unstack_heads(acc[2 * BLK * t:2 * BLK * (t + 1)])
            return 0

        lax.fori_loop(0, nq // SB_QB, q_block, 0)

    return pl.pallas_call(
        body, name="sb_fwd", grid=(pairs,),
        in_specs=[col(0), col(pairs), col(2 * pairs)],
        out_specs=pl.BlockSpec((s, LANES), lambda p: (0, p)),
        out_shape=jax.ShapeDtypeStruct((s, D_SB), F32),
        compiler_params=_params(48),
    )(qkv, qkv, qkv)


def _sb_bwd(qkv, o_sb, do_sb):
    s = qkv.shape[0]
    nq = s // BLK
    pairs = D_SB // LANES
    col = lambda off: pl.BlockSpec((s, LANES), lambda p, w: (0, off + p))
    own = pl.BlockSpec((s, LANES), lambda p, w: (0, p))

    def body(q_ref, k_ref, v_ref, o_ref, do_ref, out_ref, dq_acc, dk_acc, dv_acc):
        which = pl.program_id(1)

        @pl.when(which == 0)
        def _():
            walk(q_ref, k_ref, v_ref, o_ref, do_ref, dq_acc, dk_acc, dv_acc)
            out_ref[...] = dq_acc[...]

        @pl.when(which == 1)
        def _():
            out_ref[...] = dk_acc[...].astype(BF16)

        @pl.when(which == 2)
        def _():
            out_ref[...] = dv_acc[...].astype(BF16)

    def walk(q_ref, k_ref, v_ref, o_ref, do_ref, dq_ref, dk_acc, dv_acc):
        u_excl, u_incl = _suffix_matrices()
        zero = jnp.zeros((SB_ROWS, LANES), F32)
        dk_acc[...] = jnp.zeros_like(dk_acc)
        dv_acc[...] = jnp.zeros_like(dv_acc)

        def q_block(ib, _):
            i = ib * SB_QB
            last = i + SB_QB - 1
            q2 = _sb_rows(q_ref, i)
            do2 = _sb_rows(do_ref, i, BF16)
            totals = [_head_rowsum(do_ref[_blk(i + t), :].astype(BF16).astype(F32) * o_ref[_blk(i + t), :])
                      for t in range(SB_QB)]
            total = jnp.broadcast_to(jnp.concatenate(totals, axis=0), (SB_ROWS, BLK))

            def trip(jj, carry_m, carry_g, dq, first):
                for t in range(SB_KB):
                    j = last - jj * SB_KB - t
                    at = _blk(jnp.maximum(j, 0))
                    k = k_ref[at, :]
                    rows = slice(_SB_LATER_ROWS, SB_ROWS) if first and t == 0 else slice(0, SB_ROWS)
                    base = i + rows.start // (2 * BLK)
                    valid, log_beta, w, part_m = _sb_scores(q2[rows], k, base, j, carry_m[rows], u_excl)
                    wb = w.astype(BF16)
                    g = wb.astype(F32) * _dot_nt(do2[rows], v_ref[at, :])
                    sums = _dot_split(g, u_incl)
                    before = total[rows] - (carry_g[rows] + sums[:, :BLK])
                    dz = jnp.where(valid, g - jnp.exp(log_beta) * (g + before), 0.0)
                    dzb = (dz * SCALE).astype(BF16)
                    dk_acc[at, :] += _dot_tn(dzb, q2[rows])
                    dv_acc[at, :] += _dot_tn(wb, do2[rows])
                    carry_m = _put_rows(carry_m, rows, part_m)
                    carry_g = _put_rows(carry_g, rows, carry_g[rows] + sums[:, BLK:])
                    dq = _put_rows(dq, rows, dq[rows] + _dot(dzb, k))
                return carry_m, carry_g, dq

            def k_block(carry):
                carry_m, carry_g, dq = trip(carry[0], carry[2], carry[3], carry[4], False)
                return carry[0] + 1, _alive(carry_m), carry_m, carry_g, dq

            carry_m, carry_g, dq = trip(0, zero, zero, zero, True)
            _, _, _, _, dq = lax.while_loop(functools.partial(_more_keys, last), k_block,
                                            (jnp.int32(1), _alive(carry_m), carry_m, carry_g, dq))
            for t in range(SB_QB):
                dq_ref[_blk(i + t), :] = _unstack_heads(dq[2 * BLK * t:2 * BLK * (t + 1)]).astype(BF16)
            return 0

        lax.fori_loop(0, nq // SB_QB, q_block, 0)

    return pl.pallas_call(
        body, name="sb_bwd", grid=(pairs, 3),
        in_specs=[col(0), col(pairs), col(2 * pairs), own, own],
        out_specs=pl.BlockSpec((s, LANES), lambda p, w: (0, w * pairs + p)),
        out_shape=jax.ShapeDtypeStruct((s, 3 * D_SB), BF16),
        scratch_shapes=[pltpu.VMEM((s, LANES), BF16), pltpu.VMEM((s, LANES), F32), pltpu.VMEM((s, LANES), F32)],
        compiler_params=_params(58),
    )(qkv, qkv, qkv, o_sb, do_sb)


DIL_UNROLL = 8


def _band_masks(b):
    row = lax.broadcasted_iota(jnp.int32, (2 * BLK, BLK), 0) & (BLK - 1)
    col = lax.broadcasted_iota(jnp.int32, (2 * BLK, BLK), 1)
    return col <= row, (col - row) >= jnp.where(b > 0, 0, BLK)


def _dil_tiles(qf, kf, vf, d, t, nb):
    c, b = t // nb, t % nb
    start = c + d * BLK * b
    rows = pl.ds(start, BLK, stride=d)
    prev = pl.ds(jnp.where(b > 0, start - d * BLK, start), BLK, stride=d)
    bf = lambda ref, sl: ref[sl, :].astype(BF16)
    return b, rows, prev, _stack_heads(bf(qf, rows)), bf(kf, rows), bf(kf, prev), bf(vf, rows), bf(vf, prev)


def _lanes_of_heads(col2):
    return _unstack_heads(jnp.broadcast_to(col2, (2 * BLK, LANES)))


def _dilated_fwd(qkv):
    s = qkv.shape[0]
    pairs = D_DIL // LANES
    base = (3 * D_SB) // LANES
    col = lambda off: pl.BlockSpec((s, LANES), lambda p: (0, off + p))
    own = pl.BlockSpec((s, LANES), lambda p: (0, p))

    def body(q_ref, k_ref, v_ref, acc_ref, m_ref, qf, kf, vf, l_scr):
        qf[...] = q_ref[...].astype(F32)
        kf[...] = k_ref[...].astype(F32)
        vf[...] = v_ref[...].astype(F32)
        for d in DILATIONS:
            nb = s // (d * BLK)

            def block(t, _):
                b, rows, prev, q2, kc, kp, vc, vp = _dil_tiles(qf, kf, vf, d, t, nb)
                in_cur, in_prev = _band_masks(b)
                zc = jnp.where(in_cur, _dot_nt(q2, kc) * SCALE, NEG)
                zp = jnp.where(in_prev, _dot_nt(q2, kp) * SCALE, NEG)
                m = jnp.maximum(jnp.max(zc, axis=1, keepdims=True), jnp.max(zp, axis=1, keepdims=True))
                pc, pp = jnp.exp(zc - m), jnp.exp(zp - m)
                den = jnp.sum(pc, axis=1, keepdims=True) + jnp.sum(pp, axis=1, keepdims=True)
                acc = _unstack_heads(_dot(pc.astype(BF16), vc) + _dot(pp.astype(BF16), vp))
                m_t, l_t = _lanes_of_heads(m), _lanes_of_heads(den)
                if d == DILATIONS[0]:
                    m_ref[rows, :] = m_t
                    l_scr[rows, :] = l_t
                    acc_ref[rows, :] = acc
                else:
                    m_old = m_ref[rows, :]
                    m_new = jnp.maximum(m_old, m_t)
                    keep, add = jnp.exp(m_old - m_new), jnp.exp(m_t - m_new)
                    m_ref[rows, :] = m_new
                    l_scr[rows, :] = l_scr[rows, :] * keep + l_t * add
                    acc_ref[rows, :] = acc_ref[rows, :] * keep + acc * add
                return 0

            lax.fori_loop(0, s // BLK, block, 0, unroll=DIL_UNROLL)

        def finish(i, _):
            l = l_scr[_blk(i), :]
            acc_ref[_blk(i), :] = acc_ref[_blk(i), :] / l
            m_ref[_blk(i), :] = m_ref[_blk(i), :] + jnp.log(l)
            return 0

        lax.fori_loop(0, s // BLK, finish, 0)

    return pl.pallas_call(
        body, name="dilated_fwd", grid=(pairs,),
        in_specs=[col(base), col(base + pairs), col(base + 2 * pairs)],
        out_specs=[own, own],
        out_shape=[jax.ShapeDtypeStruct((s, D_DIL), F32)] * 2,
        scratch_shapes=[pltpu.VMEM((s, LANES), F32)] * 4,
        compiler_params=_params(56),
    )(qkv, qkv, qkv)


def _stack_lanes(t):
    other = pltpu.roll(t, HEAD_DIM, 1)
    first = _head_masks()[0]
    return jnp.concatenate([jnp.where(first, t, other), jnp.where(first, other, t)], axis=0)


def _dilated_bwd(qkv, delta, lse, dout):
    s = qkv.shape[0]
    pairs = D_DIL // LANES
    base = (3 * D_SB) // LANES
    once = pl.Buffered(1)
    col = lambda off: pl.BlockSpec((s, LANES), lambda p: (0, off + p), pipeline_mode=once)
    own = pl.BlockSpec((s, LANES), lambda p: (0, p), pipeline_mode=once)
    res = pl.BlockSpec((s, LANES), lambda p: (0, p))

    def body(q_ref, k_ref, v_ref, dl_ref, l_ref, do_ref, dq_ref, dk_ref, dv_ref, qf, kf, vf):
        qf[...] = q_ref[...].astype(F32)
        kf[...] = k_ref[...].astype(F32)
        vf[...] = v_ref[...].astype(F32)
        dq_ref[...] = jnp.zeros_like(dq_ref)
        dk_ref[...] = jnp.zeros_like(dk_ref)
        dv_ref[...] = jnp.zeros_like(dv_ref)
        for d in DILATIONS:
            nb = s // (d * BLK)

            def block(t, _):
                b, rows, prev, q2, kc, kp, vc, vp = _dil_tiles(qf, kf, vf, d, t, nb)
                in_cur, in_prev = _band_masks(b)
                do2 = _stack_heads(do_ref[rows, :].astype(BF16))
                delta = _stack_lanes(dl_ref[rows, :])
                lse2 = _stack_lanes(l_ref[rows, :])
                wc = jnp.exp(jnp.where(in_cur, _dot_nt(q2, kc) * SCALE, NEG) - lse2)
                wp = jnp.exp(jnp.where(in_prev, _dot_nt(q2, kp) * SCALE, NEG) - lse2)
                dzc = (wc * (_dot_nt(do2, vc) - delta) * SCALE).astype(BF16)
                dzp = (wp * (_dot_nt(do2, vp) - delta) * SCALE).astype(BF16)
                dq_ref[rows, :] += _unstack_heads(_dot(dzc, kc) + _dot(dzp, kp))
                dk_ref[rows, :] += _dot_tn(dzc, q2)
                dk_ref[prev, :] += _dot_tn(dzp, q2)
                dv_ref[rows, :] += _dot_tn(wc.astype(BF16), do2)
                dv_ref[prev, :] += _dot_tn(wp.astype(BF16), do2)
                return 0

            lax.fori_loop(0, s // BLK, block, 0, unroll=DIL_UNROLL)

    return pl.pallas_call(
        body, name="dilated_bwd", grid=(pairs,),
        in_specs=[col(base), col(base + pairs), col(base + 2 * pairs), own, own, own],
        out_specs=[res, res, res],
        out_shape=[jax.ShapeDtypeStruct((s, D_DIL), F32)] * 3,
        scratch_shapes=[pltpu.VMEM((s, LANES), F32)] * 3,
        compiler_params=_params(60),
    )(qkv, qkv, qkv, delta, lse, dout)


def _dilated_finish(grads, cos, sin):
    s = grads[0].shape[0]
    spec = pl.BlockSpec((TM, D_DIL), lambda i: (i, 0))
    tab = pl.BlockSpec((TM, LANES), lambda i: (i, 0))

    def body(dq_ref, dk_ref, dv_ref, c_ref, s_ref, out_ref):
        for t, (src, rotated) in enumerate(((dq_ref, True), (dk_ref, True), (dv_ref, False))):
            for c in range(D_DIL // LANES):
                piece = src[:, c * LANES:(c + 1) * LANES]
                at = t * D_DIL + c * LANES
                out_ref[:, at:at + LANES] = (_rotate(piece, c_ref[...], -s_ref[...]) if rotated else piece).astype(BF16)

    return pl.pallas_call(
        body, name="dilated_finish", grid=(s // TM,),
        in_specs=[spec] * 3 + [tab, tab], out_specs=pl.BlockSpec((TM, 3 * D_DIL), lambda i: (i, 0)),
        out_shape=jax.ShapeDtypeStruct((s, 3 * D_DIL), BF16),
        compiler_params=_params(32),
    )(*grads, cos, sin)


def _place():
    x, y, c = lax.axis_index("x"), lax.axis_index("y"), lax.axis_index("c")
    return x, y, c, 2 * x + y


def _chip(k, c):
    return (k >> 1, k & 1, c)


def _half(ref, h):
    n = ref.shape[0] // 2
    return ref.at[pl.ds(h * n, n)]


class _BackgroundGather:
    def __init__(self, ins, outs, scratch):
        n = self.n = len(ins)
        self.ins, self.outs = ins, outs
        self.mine, self.landed, self.passed = scratch[0:3 * n:3], scratch[1:3 * n:3], scratch[2:3 * n:3]
        self.send_sem, self.recv_sem, self.local_sem = scratch[3 * n:3 * n + 3]
        x, y, self.c, self.k = _place()
        self.sibling = (x, y, 1 - self.c)

    @staticmethod
    def scratch_shapes(shards):
        shapes = []
        for a in shards:
            half = (N_CHIP - 1, a.shape[0] // 2, a.shape[1])
            shapes += [pltpu.VMEM(a.shape, a.dtype), pltpu.VMEM(half, a.dtype), pltpu.VMEM(half, a.dtype)]
        n = len(shards)
        return shapes + [pltpu.SemaphoreType.DMA((6 * n,)), pltpu.SemaphoreType.DMA((6 * n,)),
                         pltpu.SemaphoreType.DMA((8 * n,))]

    @staticmethod
    def out_shapes(shards):
        return [jax.ShapeDtypeStruct((N_CHIP,) + a.shape, a.dtype) for a in shards]

    def _remote(self, a, slot, src, dst, to):
        return pltpu.make_async_remote_copy(src_ref=src, dst_ref=dst, send_sem=self.send_sem.at[6 * a + slot],
                                            recv_sem=self.recv_sem.at[6 * a + slot], device_id=to, device_id_type=MESH)

    def _local(self, a, slot, src, dst):
        return pltpu.make_async_copy(src, dst, self.local_sem.at[8 * a + slot])

    def _ici(self, a, j):
        return self._remote(a, j - 1, _half(self.mine[a], self.c), self.landed[a].at[j - 1], _chip(self.k ^ j, self.c))

    def _to_sibling(self, a, j):
        return self._remote(a, 2 + j, self.landed[a].at[j - 1], self.passed[a].at[j - 1], self.sibling)

    def _own(self, a):
        return self._local(a, 0, self.ins[a], self.outs[a].at[self.k])

    def _load(self, a):
        return self._local(a, 1, self.ins[a], self.mine[a])

    def _store_landed(self, a, j):
        return self._local(a, 1 + j, self.landed[a].at[j - 1], _half(self.outs[a].at[self.k ^ j], self.c))

    def _store_passed(self, a, j):
        return self._local(a, 4 + j, self.passed[a].at[j - 1], _half(self.outs[a].at[self.k ^ j], 1 - self.c))

    def start(self):
        for a in range(self.n):
            self._own(a).start()
            self._load(a).start()
        for a in range(self.n):
            self._load(a).wait()
            for j in range(1, N_CHIP):
                self._ici(a, j).start()

    def forward(self):
        for j in range(1, N_CHIP):
            for a in range(self.n):
                self._ici(a, j).wait_recv()
                self._to_sibling(a, j).start()
                self._store_landed(a, j).start()

    def finish(self):
        for j in range(1, N_CHIP):
            for a in range(self.n):
                self._to_sibling(a, j).wait_recv()
                self._store_passed(a, j).start()
        for a in range(self.n):
            for j in range(1, N_CHIP):
                self._ici(a, j).wait_send()
                self._to_sibling(a, j).wait_send()
                self._store_landed(a, j).wait()
                self._store_passed(a, j).wait()
            self._own(a).wait()


def _all_gather(shards):
    n = len(shards)
    any_spec = pl.BlockSpec(memory_space=pl.ANY)

    def body(*refs):
        gather = _BackgroundGather(refs[:n], refs[n:2 * n], refs[2 * n:])
        gather.start()
        gather.forward()
        gather.finish()

    return pl.pallas_call(
        body, name="weights_all_gather",
        in_specs=[any_spec] * n, out_specs=[any_spec] * n,
        out_shape=_BackgroundGather.out_shapes(shards),
        scratch_shapes=_BackgroundGather.scratch_shapes(shards),
        compiler_params=_params(32),
    )(*shards)


def _reduce_scatter(g, core, name):
    n, r, c = g.shape
    hr = r // 2
    once = pl.Buffered(1)
    in_specs = [pl.BlockSpec((n, hr, c), lambda i, core_ref: (0, core_ref[0], 0), pipeline_mode=once),
                pl.BlockSpec((n, hr, c), lambda i, core_ref: (0, 1 - core_ref[0], 0), pipeline_mode=once)]

    def body(core_ref, mine_ref, other_ref, out_ref, from_core, sums_bf, from_chips, done, from_core2, send_sem, recv_sem):
        x, y, cc, k = _place()
        sibling = (x, y, 1 - cc)

        def copy(slot, src, dst, to):
            return pltpu.make_async_remote_copy(src_ref=src, dst_ref=dst, send_sem=send_sem.at[slot],
                                                recv_sem=recv_sem.at[slot], device_id=to, device_id_type=MESH)

        from_sibling = [copy(j, other_ref.at[k ^ j], from_core.at[k ^ j], sibling) for j in range(N_CHIP)]
        for j in (1, 2, 3, 0):
            from_sibling[j].start()
        sends = []
        for j in range(1, N_CHIP):
            from_sibling[j].wait()
            sums_bf[j - 1] = (mine_ref[k ^ j] + from_core[k ^ j]).astype(BF16)
            cp = copy(N_CHIP - 1 + j, sums_bf.at[j - 1], from_chips.at[j - 1], _chip(k ^ j, cc))
            cp.start()
            sends.append(cp)
        from_sibling[0].wait()
        red = mine_ref[k] + from_core[k]
        for j in range(1, N_CHIP):
            sends[j - 1].wait()
            red = red + from_chips[j - 1].astype(F32)
        done[...] = red
        last = copy(2 * N_CHIP - 1, done, from_core2, sibling)
        last.start()
        last.wait()
        row0 = pl.multiple_of(cc * hr, 8)
        row1 = pl.multiple_of((1 - cc) * hr, 8)
        out_ref[pl.ds(row0, hr), :] = red
        out_ref[pl.ds(row1, hr), :] = from_core2[...]

    grid_spec = pltpu.PrefetchScalarGridSpec(
        num_scalar_prefetch=1, grid=(1,), in_specs=in_specs,
        out_specs=pl.BlockSpec((r, c), lambda i, core_ref: (0, 0)),
        scratch_shapes=[pltpu.VMEM((n, hr, c), F32), pltpu.VMEM((N_CHIP - 1, hr, c), BF16),
                        pltpu.VMEM((N_CHIP - 1, hr, c), BF16), pltpu.VMEM((hr, c), F32), pltpu.VMEM((hr, c), F32),
                        pltpu.SemaphoreType.DMA((2 * N_CHIP,)), pltpu.SemaphoreType.DMA((2 * N_CHIP,))])
    return pl.pallas_call(
        body, name=name, grid_spec=grid_spec, out_shape=jax.ShapeDtypeStruct((r, c), F32),
        compiler_params=_params(56),
    )(core, g, g)


def _elementwise(fn, name, ins, n_out, rows):
    total, cols = ins[0].shape
    spec = pl.BlockSpec((rows, cols), lambda i: (i, 0))

    def body(*refs):
        res = fn(*[r[...] for r in refs[:len(ins)]])
        for o, v in zip(refs[len(ins):], res):
            o[...] = v

    return pl.pallas_call(
        body, name=name, grid=(total // rows,),
        in_specs=[spec] * len(ins), out_specs=[spec] * n_out,
        out_shape=[jax.ShapeDtypeStruct((total, cols), F32)] * n_out,
        compiler_params=_params(48),
    )(*ins)


def _adamw(w, g, m, v):
    m = ADAM_B1 * m + (1.0 - ADAM_B1) * g
    v = ADAM_B2 * v + (1.0 - ADAM_B2) * (g * g)
    m_hat = m / (1.0 - ADAM_B1 ** ADAM_STEP)
    v_hat = v / (1.0 - ADAM_B2 ** ADAM_STEP)
    delta = -ADAM_LR * (m_hat / (jnp.sqrt(v_hat) + ADAM_EPS) + ADAM_WD * w)
    return delta, m, v


def _reduce_and_update(grads, weights, moms, vels):
    core = lax.axis_index("c").astype(jnp.int32).reshape(1)
    full = [_reduce_scatter(g, core, f"grads_reduce_scatter_{a}") for a, g in enumerate(grads)]
    out = []
    for a, (g, w, m, v) in enumerate(zip(full, weights, moms, vels)):
        rows = g.shape[0] // 2
        out.append((g,) + tuple(_elementwise(lambda gg, ww, mm, vv: _adamw(ww, gg, mm, vv), f"adamw_{a}", [g, w, m, v], 3, rows)))
    return out


def _reduce_vectors(part, w, m, v):
    n_dev = 8

    def body(p_ref, w_ref, m_ref, v_ref, g_ref, d_ref, nm_ref, nv_ref, buf, send_sem, recv_sem):
        x, y, c, _ = _place()
        me = 4 * x + 2 * y + c
        buf[me] = p_ref[...]
        sends = []
        for off in range(1, n_dev):
            peer = me ^ off
            cp = pltpu.make_async_remote_copy(src_ref=p_ref, dst_ref=buf.at[me], send_sem=send_sem.at[off - 1],
                                              recv_sem=recv_sem.at[off - 1], device_id=(peer >> 2, (peer >> 1) & 1, peer & 1),
                                              device_id_type=MESH)
            cp.start()
            sends.append(cp)
        for off in range(1, n_dev):
            peer = me ^ off
            pltpu.make_async_remote_copy(src_ref=p_ref, dst_ref=buf.at[peer], send_sem=send_sem.at[off - 1],
                                         recv_sem=recv_sem.at[off - 1], device_id=(peer >> 2, (peer >> 1) & 1, peer & 1),
                                         device_id_type=MESH).wait_recv()
        for cp in sends:
            cp.wait_send()
        g = buf[0]
        for d in range(1, n_dev):
            g = g + buf[d]
        g_ref[...] = g
        delta, nm, nv = _adamw(w_ref[...], g, m_ref[...], v_ref[...])
        d_ref[...] = delta
        nm_ref[...] = nm
        nv_ref[...] = nv

    vm = pl.BlockSpec(memory_space=pltpu.VMEM)
    return pl.pallas_call(
        body, name="gains_all_reduce",
        in_specs=[vm] * 4, out_specs=[vm] * 4,
        out_shape=[jax.ShapeDtypeStruct(part.shape, F32)] * 4,
        scratch_shapes=[pltpu.VMEM((n_dev,) + part.shape, F32), pltpu.SemaphoreType.DMA((n_dev - 1,)),
                        pltpu.SemaphoreType.DMA((n_dev - 1,))],
    )(part, w, m, v)


def _pad_row(a):
    a = a.reshape(1, -1)
    return jnp.pad(a, ((0, 0), (0, D_MODEL - a.shape[1])))


def kernel(x, ffn1_norm, ffn1_w_gate, ffn1_w_up, ffn1_w_down, mix_norm, w_in, sb_out_norm, dil_out_norm, w_out, ffn2_norm, ffn2_w_gate, ffn2_w_up, ffn2_w_down, final_norm, loss_target, m_ffn1_norm, m_ffn1_w_gate, m_ffn1_w_up, m_ffn1_w_down, m_mix_norm, m_w_in, m_sb_out_norm, m_dil_out_norm, m_w_out, m_ffn2_norm, m_ffn2_w_gate, m_ffn2_w_up, m_ffn2_w_down, m_final_norm, v_ffn1_norm, v_ffn1_w_gate, v_ffn1_w_up, v_ffn1_w_down, v_mix_norm, v_w_in, v_sb_out_norm, v_dil_out_norm, v_w_out, v_ffn2_norm, v_ffn2_w_gate, v_ffn2_w_up, v_ffn2_w_down, v_final_norm):
    x = x[0]
    target = loss_target[0]
    s = x.shape[0]
    gf = final_norm.reshape(1, D_MODEL)
    cos, sin = _rope_tables(s)

    flip = lambda a: a[0].T
    shard = lambda w: w[0].astype(BF16)
    shard_t = lambda w: flip(w).astype(BF16)
    wg1, wu1, wd1 = _all_gather([shard_t(ffn1_w_gate), shard_t(ffn1_w_up), shard(ffn1_w_down)])

    x1, hm, saved1, (win, wout, wd2) = _ffn1_fwd(x, ffn1_norm, mix_norm, (wg1, wu1), wd1,
                                                 [shard(w_in), shard(w_out), shard(ffn2_w_down)])
    wout = wout.reshape(D_MODEL, D_MODEL)
    qkv, (wg2, wu2) = _proj_fwd(hm, win, cos, sin, [shard_t(ffn2_w_gate), shard_t(ffn2_w_up)])
    o_sb = _sb_fwd(qkv)
    o_dl, lse = _dilated_fwd(qkv)
    x2 = _outproj_fwd(o_sb, o_dl, sb_out_norm, dil_out_norm, x1, wout)
    dx3, st_final, saved2 = _ffn2_fwd_loss(x2, ffn2_norm, gf, target, (wg2, wu2), wd2)

    dx2, dwg2, dwu2, dwd2, st_ffn2 = _ffn_bwd(x2, ffn2_norm, dx3, saved2, (wg2, wu2), wd2, 1)
    do_sb, do_dl, delta_dl, dwout, st_out = _outproj_bwd(dx2, o_sb, o_dl, sb_out_norm, dil_out_norm, wout)
    dqkv_sb = _sb_bwd(qkv, o_sb, do_sb)
    dqkv_dl = _dilated_finish(_dilated_bwd(qkv, delta_dl, lse, do_dl), cos, sin)
    dx1, dwin, st_mix = _proj_bwd(x1, mix_norm, dqkv_sb, dqkv_dl, win, dx2)
    grad_x, dwg1, dwu1, dwd1, st_ffn1 = _ffn_bwd(x, ffn1_norm, dx1, saved1, (wg1, wu1), wd1, 0)

    names = ["ffn1_w_gate", "ffn1_w_up", "ffn1_w_down", "w_in", "w_out", "ffn2_w_gate", "ffn2_w_up", "ffn2_w_down"]
    grads = [dwg1, dwu1, dwd1, dwin, dwout.reshape(N_CHIP, OUTB, D_MODEL), dwg2, dwu2, dwd2]
    flipped = {"ffn1_w_gate", "ffn1_w_up", "ffn2_w_gate", "ffn2_w_up"}
    place = lambda n, a: flip(a) if n in flipped else a[0]
    weights = [place(n, a) for n, a in zip(names, [ffn1_w_gate, ffn1_w_up, ffn1_w_down, w_in, w_out, ffn2_w_gate, ffn2_w_up, ffn2_w_down])]
    moms = [place(n, a) for n, a in zip(names, [m_ffn1_w_gate, m_ffn1_w_up, m_ffn1_w_down, m_w_in, m_w_out, m_ffn2_w_gate, m_ffn2_w_up, m_ffn2_w_down])]
    vels = [place(n, a) for n, a in zip(names, [v_ffn1_w_gate, v_ffn1_w_up, v_ffn1_w_down, v_w_in, v_w_out, v_ffn2_w_gate, v_ffn2_w_up, v_ffn2_w_down])]
    mats = {n: tuple((t.T if n in flipped else t)[None] for t in r)
            for n, r in zip(names, _reduce_and_update(grads, weights, moms, vels))}

    vec_names = ["ffn1_norm", "mix_norm", "sb_out_norm", "dil_out_norm", "ffn2_norm", "final_norm"]
    part = jnp.concatenate([st_ffn1[0:1], st_mix[0:1], _pad_row(st_out[0]), _pad_row(st_out[1]), st_ffn2[0:1],
                            st_final[0:1], st_final[1:2], jnp.zeros((1, D_MODEL), F32)], axis=0)
    pack = lambda arrs: jnp.concatenate([_pad_row(a) for a in arrs] + [jnp.zeros((2, D_MODEL), F32)], axis=0)
    g_vec, d_vec, m_vec, v_vec = _reduce_vectors(
        part,
        pack([ffn1_norm, mix_norm, sb_out_norm, dil_out_norm, ffn2_norm, final_norm]),
        pack([m_ffn1_norm, m_mix_norm, m_sb_out_norm, m_dil_out_norm, m_ffn2_norm, m_final_norm]),
        pack([v_ffn1_norm, v_mix_norm, v_sb_out_norm, v_dil_out_norm, v_ffn2_norm, v_final_norm]))
    like = {"ffn1_norm": ffn1_norm, "mix_norm": mix_norm, "sb_out_norm": sb_out_norm, "dil_out_norm": dil_out_norm,
            "ffn2_norm": ffn2_norm, "final_norm": final_norm}
    vecs = {n: tuple(t[i, :like[n].size].reshape(like[n].shape) for t in (g_vec, d_vec, m_vec, v_vec))
            for i, n in enumerate(vec_names)}
    loss = 0.5 * jnp.sum(g_vec[6]) / D_MODEL

    order = ["ffn1_norm", "ffn1_w_gate", "ffn1_w_up", "ffn1_w_down", "mix_norm", "w_in", "sb_out_norm", "dil_out_norm",
             "w_out", "ffn2_norm", "ffn2_w_gate", "ffn2_w_up", "ffn2_w_down", "final_norm"]
    both = {**mats, **vecs}
    return (loss, grad_x[None], *[both[n][0] for n in order], *[both[n][1] for n in order],
            *[both[n][2] for n in order], *[both[n][3] for n in order])
```

```python
import functools

import jax
import jax.numpy as jnp
from jax import lax
from jax.experimental import pallas as pl
from jax.experimental.pallas import tpu as pltpu

D_MODEL = 1024
D_FF = 2816
HEAD_DIM = 64
D_SB = 512
D_DIL = 512
D_IN = 3072
N_CHIP = 4
FFB = D_FF // N_CHIP
INB = D_IN // N_CHIP
OUTB = D_MODEL // N_CHIP
BLK = 128
LANES = 128
DILATIONS = (1, 4, 16)
ROPE_THETA = 10000.0
RMS_EPS = 1e-6
SCALE = HEAD_DIM ** -0.5
NEG = -1e30
DEAD = -104.0
ADAM_LR = 0.001
ADAM_B1 = 0.9
ADAM_B2 = 0.999
ADAM_EPS = 1e-08
ADAM_WD = 0.01
ADAM_STEP = 10
MESH = pl.DeviceIdType.MESH
F32 = jnp.float32
BF16 = jnp.bfloat16
TM = 512


def _params(vmem_mb):
    return pltpu.CompilerParams(vmem_limit_bytes=vmem_mb << 20)


def _dot(a, b):
    return jnp.dot(a, b, preferred_element_type=F32)


def _dot_nt(a, b):
    return lax.dot_general(a, b, (((1,), (1,)), ((), ())), preferred_element_type=F32)


def _dot_tn(a, b):
    return lax.dot_general(a, b, (((0,), (0,)), ((), ())), preferred_element_type=F32)


def _rms_fwd(x, g):
    r = lax.rsqrt(jnp.mean(x * x, axis=-1, keepdims=True) + RMS_EPS)
    xh = x * r
    return xh * g, xh, r


def _rms_bwd(dy, xh, r, g):
    dyg = dy * g
    dx = r * (dyg - xh * jnp.mean(dyg * xh, axis=-1, keepdims=True))
    return dx, jnp.sum(dy * xh, axis=0, keepdims=True)


def _split_bf16(a):
    hi = a.astype(BF16)
    return hi, (a - hi.astype(F32)).astype(BF16)


def _dot_split(a, b2):
    hi, lo = _split_bf16(a)
    return _dot(jnp.concatenate([hi, lo], axis=1), b2)


def _ffn_weight_specs():
    return [pl.BlockSpec((None, FFB, D_MODEL), lambda i, j: (j, 0, 0))] * 3


def _ffn_saved(s):
    hidden = jax.ShapeDtypeStruct((N_CHIP, s, FFB), BF16)
    hid = pl.BlockSpec((None, TM, FFB), lambda i, j: (j, i, 0))
    row = pl.BlockSpec((TM, D_MODEL), lambda i, j: (i, 0))
    return [row, hid, hid, hid], [jax.ShapeDtypeStruct((s, D_MODEL), BF16), hidden, hidden, hidden]


def _ffn_accumulate(h_ref, acc_scr, wg_ref, wu_ref, wd_ref, a_ref, b_ref, act_ref):
    h = h_ref[...]
    a = _dot_nt(h, wg_ref[...])
    b = _dot_nt(h, wu_ref[...])
    act = ((a * jax.nn.sigmoid(a)) * b).astype(BF16)
    a_ref[...] = a.astype(BF16)
    b_ref[...] = b.astype(BF16)
    act_ref[...] = act
    acc_scr[...] += _dot(act, wd_ref[...])


def _host_gather_before(gather, i, j, steps):
    @pl.when((i == 0) & (j == 0))
    def _():
        gather.start()

    @pl.when((i == (3 * steps) // 4) & (j == 0))
    def _():
        gather.forward()


def _host_gather_after(gather, i, j, steps):
    @pl.when((i == steps - 1) & (j == N_CHIP - 1))
    def _():
        gather.finish()


def _ffn1_fwd(x, g1, gmix, gu, wd, later_shards):
    s = x.shape[0]
    row = pl.BlockSpec((TM, D_MODEL), lambda i, j: (i, 0))
    vec = pl.BlockSpec((1, D_MODEL), lambda i, j: (0, 0))
    saved_specs, saved_shapes = _ffn_saved(s)
    n = len(later_shards)
    any_spec = pl.BlockSpec(memory_space=pl.ANY)

    def body(*refs):
        x_ref, g_ref, gm_ref, wg_ref, wu_ref, wd_ref = refs[:6]
        shard_refs, refs = refs[6:6 + n], refs[6 + n:]
        x1_ref, hm_ref, h_ref, a_ref, b_ref, act_ref = refs[:6]
        gathered_refs, acc_scr, gather_scratch = refs[6:6 + n], refs[6 + n], refs[7 + n:]
        gather = _BackgroundGather(shard_refs, gathered_refs, gather_scratch)
        i, j = pl.program_id(0), pl.program_id(1)
        _host_gather_before(gather, i, j, s // TM)

        @pl.when(j == 0)
        def _():
            h, _, _ = _rms_fwd(x_ref[...], g_ref[...])
            h_ref[...] = h.astype(BF16)
            acc_scr[...] = jnp.zeros_like(acc_scr)

        _ffn_accumulate(h_ref, acc_scr, wg_ref, wu_ref, wd_ref, a_ref, b_ref, act_ref)

        @pl.when(j == N_CHIP - 1)
        def _():
            x1 = x_ref[...] + 0.5 * acc_scr[...]
            x1_ref[...] = x1
            hm, _, _ = _rms_fwd(x1, gm_ref[...])
            hm_ref[...] = hm.astype(BF16)

        _host_gather_after(gather, i, j, s // TM)

    x1, hm, h, a, b, act, *gathered = pl.pallas_call(
        body, name="ffn1_fwd", grid=(s // TM, N_CHIP),
        in_specs=[row, vec, vec] + _ffn_weight_specs() + [any_spec] * n,
        out_specs=[row, row] + saved_specs + [any_spec] * n,
        out_shape=([jax.ShapeDtypeStruct((s, D_MODEL), F32), jax.ShapeDtypeStruct((s, D_MODEL), BF16)] + saved_shapes
                   + _BackgroundGather.out_shapes(later_shards)),
        scratch_shapes=[pltpu.VMEM((TM, D_MODEL), F32)] + _BackgroundGather.scratch_shapes(later_shards),
        compiler_params=_params(58),
    )(x, g1, gmix, gu[0], gu[1], wd, *later_shards)
    return x1, hm, [h, a, b, act], gathered


def _ffn2_fwd_loss(x2, g2, gf, target, gu, wd):
    s = x2.shape[0]
    row = pl.BlockSpec((TM, D_MODEL), lambda i, j: (i, 0))
    vec = pl.BlockSpec((1, D_MODEL), lambda i, j: (0, 0))
    stat = pl.BlockSpec((8, D_MODEL), lambda i, j: (0, 0))
    saved_specs, saved_shapes = _ffn_saved(s)

    def body(x_ref, g_ref, gf_ref, t_ref, wg_ref, wu_ref, wd_ref, dx_ref, st_ref, h_ref, a_ref, b_ref, act_ref, acc_scr):
        i, j = pl.program_id(0), pl.program_id(1)

        @pl.when((i == 0) & (j == 0))
        def _():
            st_ref[...] = jnp.zeros_like(st_ref)

        @pl.when(j == 0)
        def _():
            h, _, _ = _rms_fwd(x_ref[...], g_ref[...])
            h_ref[...] = h.astype(BF16)
            acc_scr[...] = jnp.zeros_like(acc_scr)

        _ffn_accumulate(h_ref, acc_scr, wg_ref, wu_ref, wd_ref, a_ref, b_ref, act_ref)

        @pl.when(j == N_CHIP - 1)
        def _():
            x3 = x_ref[...] + 0.5 * acc_scr[...]
            y, xh, r = _rms_fwd(x3, gf_ref[...])
            err = y - t_ref[...]
            dx, dg = _rms_bwd(err * (1.0 / D_MODEL), xh, r, gf_ref[...])
            dx_ref[...] = dx
            st_ref[0:1, :] += dg
            st_ref[1:2, :] += jnp.sum(err * err, axis=0, keepdims=True)

    dx3, st, *saved = pl.pallas_call(
        body, name="ffn2_fwd_loss", grid=(s // TM, N_CHIP),
        in_specs=[row, vec, vec, row] + _ffn_weight_specs(),
        out_specs=[row, stat] + saved_specs,
        out_shape=[jax.ShapeDtypeStruct((s, D_MODEL), F32), jax.ShapeDtypeStruct((8, D_MODEL), F32)] + saved_shapes,
        scratch_shapes=[pltpu.VMEM((TM, D_MODEL), F32)],
        compiler_params=_params(56),
    )(x2, g2, gf, target, gu[0], gu[1], wd)
    return dx3, st, saved


def _ffn_bwd(xin, g, dy, saved, gu, wd, f):
    s = xin.shape[0]
    hb, gate, up, act = saved
    row = pl.BlockSpec((TM, D_MODEL), lambda i, j: (i, 0))
    vec = pl.BlockSpec((1, D_MODEL), lambda i, j: (0, 0))
    stat = pl.BlockSpec((8, D_MODEL), lambda i, j: (0, 0))
    hid = pl.BlockSpec((None, TM, FFB), lambda i, j: (j, i, 0))

    def body(x_ref, g_ref, dy_ref, a_ref, b_ref, wg_ref, wu_ref, wd_ref, out_ref, dyh_ref, da_ref, db_ref, st_ref, dh_scr):
        i, j = pl.program_id(0), pl.program_id(1)

        @pl.when((i == 0) & (j == 0))
        def _():
            st_ref[...] = jnp.zeros_like(st_ref)

        @pl.when(j == 0)
        def _():
            dyh_ref[...] = (0.5 * dy_ref[...]).astype(BF16)
            dh_scr[...] = jnp.zeros_like(dh_scr)

        a = a_ref[...].astype(F32)
        b = b_ref[...].astype(F32)
        sg = jax.nn.sigmoid(a)
        dact = _dot_nt(dyh_ref[...], wd_ref[...])
        dab = (dact * b * (sg * (1.0 + a * (1.0 - sg)))).astype(BF16)
        dbb = (dact * (a * sg)).astype(BF16)
        da_ref[...] = dab
        db_ref[...] = dbb
        dh_scr[...] += _dot(dab, wg_ref[...]) + _dot(dbb, wu_ref[...])

        @pl.when(j == N_CHIP - 1)
        def _():
            _, xh, r = _rms_fwd(x_ref[...], g_ref[...])
            dx, dg = _rms_bwd(dh_scr[...], xh, r, g_ref[...])
            out_ref[...] = dy_ref[...] + dx
            st_ref[0:1, :] += dg

    hidden = jax.ShapeDtypeStruct((N_CHIP, s, FFB), BF16)
    dx, dyh, da, db, st = pl.pallas_call(
        body, name=f"ffn{f + 1}_bwd_dx", grid=(s // TM, N_CHIP),
        in_specs=[row, vec, row, hid, hid] + _ffn_weight_specs(),
        out_specs=[row, row, hid, hid, stat],
        out_shape=[jax.ShapeDtypeStruct((s, D_MODEL), F32), jax.ShapeDtypeStruct((s, D_MODEL), BF16),
                   hidden, hidden, jax.ShapeDtypeStruct((8, D_MODEL), F32)],
        scratch_shapes=[pltpu.VMEM((TM, D_MODEL), F32)],
        compiler_params=_params(56),
    )(xin, g, dy, gate, up, gu[0], gu[1], wd)

    tok = pl.BlockSpec((TM, D_MODEL), lambda j, i: (i, 0))
    hid2 = pl.BlockSpec((None, TM, FFB), lambda j, i: (j, i, 0))
    gspecs = [pl.BlockSpec((None, FFB, D_MODEL), lambda j, i: (j, 0, 0))] * 3

    def wbody(h_ref, dyh_ref, da_ref, db_ref, act_ref, dwg_ref, dwu_ref, dwd_ref):
        @pl.when(pl.program_id(1) == 0)
        def _():
            dwg_ref[...] = jnp.zeros_like(dwg_ref)
            dwu_ref[...] = jnp.zeros_like(dwu_ref)
            dwd_ref[...] = jnp.zeros_like(dwd_ref)

        hb = h_ref[...]
        dwg_ref[...] += _dot_tn(da_ref[...], hb)
        dwu_ref[...] += _dot_tn(db_ref[...], hb)
        dwd_ref[...] += _dot_tn(act_ref[...], dyh_ref[...])

    dwg, dwu, dwd = pl.pallas_call(
        wbody, name=f"ffn{f + 1}_bwd_dw", grid=(N_CHIP, s // TM),
        in_specs=[tok, tok, hid2, hid2, hid2], out_specs=gspecs,
        out_shape=[jax.ShapeDtypeStruct((N_CHIP, FFB, D_MODEL), F32)] * 3,
        compiler_params=_params(48),
    )(hb, dyh, da, db, act)
    return dx, dwg, dwu, dwd, st


def _rope_tables(s):
    half = HEAD_DIM // 2
    inv_freq = ROPE_THETA ** (-jnp.arange(half, dtype=F32) / half)
    ang = jnp.arange(s).astype(F32)[:, None] * inv_freq[None, :]
    cos, sin = jnp.cos(ang), jnp.sin(ang)
    cos2 = jnp.concatenate([cos, cos], axis=-1)
    sin2 = jnp.concatenate([-sin, sin], axis=-1)
    return jnp.tile(cos2, (1, LANES // HEAD_DIM)), jnp.tile(sin2, (1, LANES // HEAD_DIM))


def _rotate(t, cos, sin_signed):
    lane = lax.broadcasted_iota(jnp.int32, t.shape, 1)
    first = (lane % HEAD_DIM) < (HEAD_DIM // 2)
    partner = jnp.where(first, pltpu.roll(t, LANES - HEAD_DIM // 2, 1), pltpu.roll(t, HEAD_DIM // 2, 1))
    return t * cos + partner * sin_signed


def _proj_fwd(hm, win, cos, sin, later_shards):
    s = hm.shape[0]
    tm = 2 * TM
    n_sub = INB // LANES
    first_rot, last_rot = (3 * D_SB) // LANES, (3 * D_SB + 2 * D_DIL) // LANES
    n = len(later_shards)
    any_spec = pl.BlockSpec(memory_space=pl.ANY)

    def body(*refs):
        h_ref, w_ref, c_ref, s_ref = refs[:4]
        shard_refs, o_ref, gathered_refs, gather_scratch = refs[4:4 + n], refs[4 + n], refs[5 + n:5 + 2 * n], refs[5 + 2 * n:]
        gather = _BackgroundGather(shard_refs, gathered_refs, gather_scratch)
        i, j = pl.program_id(0), pl.program_id(1)
        _host_gather_before(gather, i, j, s // tm)
        r = _dot(h_ref[...], w_ref[...])
        for c in range(n_sub):
            t = r[:, c * LANES:(c + 1) * LANES]
            col = j * n_sub + c
            rot = (col >= first_rot) & (col < last_rot)
            lanes = slice(c * LANES, (c + 1) * LANES)

            @pl.when(rot)
            def _():
                o_ref[:, lanes] = _rotate(t, c_ref[...], s_ref[...]).astype(BF16)

            @pl.when(jnp.logical_not(rot))
            def _():
                o_ref[:, lanes] = t.astype(BF16)

        _host_gather_after(gather, i, j, s // tm)

    qkv, *gathered = pl.pallas_call(
        body, name="proj_fwd", grid=(s // tm, N_CHIP),
        in_specs=[pl.BlockSpec((tm, D_MODEL), lambda i, j: (i, 0)),
                  pl.BlockSpec((None, D_MODEL, INB), lambda i, j: (j, 0, 0)),
                  pl.BlockSpec((tm, LANES), lambda i, j: (i, 0)),
                  pl.BlockSpec((tm, LANES), lambda i, j: (i, 0))] + [any_spec] * n,
        out_specs=[pl.BlockSpec((tm, INB), lambda i, j: (i, j))] + [any_spec] * n,
        out_shape=[jax.ShapeDtypeStruct((s, D_IN), BF16)] + _BackgroundGather.out_shapes(later_shards),
        scratch_shapes=_BackgroundGather.scratch_shapes(later_shards),
        compiler_params=_params(48),
    )(hm, win, cos, sin, *later_shards)
    return qkv, gathered


def _proj_bwd(x1, gmix, dqkv_sb, dqkv_dl, win, dx2):
    s = x1.shape[0]
    row = pl.BlockSpec((TM, D_MODEL), lambda i, j: (i, 0))
    vec = pl.BlockSpec((1, D_MODEL), lambda i, j: (0, 0))
    per_group = N_CHIP // 2

    def body(x_ref, g_ref, dsb_ref, ddl_ref, w_ref, dx2_ref, out_ref, dw_ref, st_ref, h_scr, dh_scr, dq_ref):
        i, j = pl.program_id(0), pl.program_id(1)

        @pl.when(j < per_group)
        def _():
            dq_ref[...] = dsb_ref[...]

        @pl.when(j >= per_group)
        def _():
            dq_ref[...] = ddl_ref[...]

        @pl.when((i == 0) & (j == 0))
        def _():
            st_ref[...] = jnp.zeros_like(st_ref)
            dw_ref[...] = jnp.zeros_like(dw_ref)

        @pl.when(j == 0)
        def _():
            h, _, _ = _rms_fwd(x_ref[...], g_ref[...])
            h_scr[...] = h.astype(BF16)
            dh_scr[...] = jnp.zeros_like(dh_scr)

        dq = dq_ref[...]
        dw_ref[j] += _dot_tn(h_scr[...], dq)
        dh_scr[...] += _dot_nt(dq, w_ref[...])

        @pl.when(j == N_CHIP - 1)
        def _():
            _, xh, r = _rms_fwd(x_ref[...], g_ref[...])
            dx, dg = _rms_bwd(dh_scr[...], xh, r, g_ref[...])
            out_ref[...] = dx2_ref[...] + dx
            st_ref[0:1, :] += dg

    return pl.pallas_call(
        body, name="proj_bwd", grid=(s // TM, N_CHIP),
        in_specs=[row, vec,
                  pl.BlockSpec((TM, INB), lambda i, j: (i, jnp.minimum(j, per_group - 1))),
                  pl.BlockSpec((TM, INB), lambda i, j: (i, jnp.maximum(j - per_group, 0))),
                  pl.BlockSpec((None, D_MODEL, INB), lambda i, j: (j, 0, 0)), row],
        out_specs=[row, pl.BlockSpec((N_CHIP, D_MODEL, INB), lambda i, j: (0, 0, 0)),
                   pl.BlockSpec((8, D_MODEL), lambda i, j: (0, 0))],
        out_shape=[jax.ShapeDtypeStruct((s, D_MODEL), F32),
                   jax.ShapeDtypeStruct((N_CHIP, D_MODEL, INB), F32),
                   jax.ShapeDtypeStruct((8, D_MODEL), F32)],
        scratch_shapes=[pltpu.VMEM((TM, D_MODEL), BF16), pltpu.VMEM((TM, D_MODEL), F32), pltpu.VMEM((TM, INB), BF16)],
        compiler_params=_params(56),
    )(x1, gmix, dqkv_sb, dqkv_dl, win, dx2)


def _outproj_fwd(o_sb, o_dl, g_sb, g_dl, x1, wout):
    s = x1.shape[0]
    half = pl.BlockSpec((TM, D_SB), lambda i: (i, 0))
    row = pl.BlockSpec((TM, D_MODEL), lambda i: (i, 0))
    vec = pl.BlockSpec((1, D_SB), lambda i: (0, 0))

    def body(a_ref, b_ref, ga_ref, gb_ref, x_ref, w_ref, o_ref):
        ma, _, _ = _rms_fwd(a_ref[...], ga_ref[...])
        mb, _, _ = _rms_fwd(b_ref[...], gb_ref[...])
        o_ref[...] = (x_ref[...] + _dot(ma.astype(BF16), w_ref[0:D_SB, :])
                      + _dot(mb.astype(BF16), w_ref[D_SB:D_MODEL, :]))

    return pl.pallas_call(
        body, name="outproj_fwd", grid=(s // TM,),
        in_specs=[half, half, vec, vec, row, pl.BlockSpec((D_MODEL, D_MODEL), lambda i: (0, 0))],
        out_specs=row, out_shape=jax.ShapeDtypeStruct((s, D_MODEL), F32),
        compiler_params=_params(32),
    )(o_sb, o_dl, g_sb, g_dl, x1, wout)


def _outproj_bwd(dx2, o_sb, o_dl, g_sb, g_dl, wout):
    s = dx2.shape[0]
    half = pl.BlockSpec((TM, D_SB), lambda i: (i, 0))
    row = pl.BlockSpec((TM, D_MODEL), lambda i: (i, 0))
    vec = pl.BlockSpec((1, D_SB), lambda i: (0, 0))
    full = pl.BlockSpec((D_MODEL, D_MODEL), lambda i: (0, 0))

    def body(dy_ref, a_ref, b_ref, ga_ref, gb_ref, w_ref, da_ref, db_ref, dl_ref, dw_ref, st_ref):
        @pl.when(pl.program_id(0) == 0)
        def _():
            dw_ref[...] = jnp.zeros_like(dw_ref)
            st_ref[...] = jnp.zeros_like(st_ref)

        dy = dy_ref[...].astype(BF16)
        dm = _dot_nt(dy, w_ref[...])
        ma, xa, ra = _rms_fwd(a_ref[...], ga_ref[...])
        mb, xb, rb = _rms_fwd(b_ref[...], gb_ref[...])
        dw_ref[0:D_SB, :] += _dot_tn(ma.astype(BF16), dy)
        dw_ref[D_SB:D_MODEL, :] += _dot_tn(mb.astype(BF16), dy)
        da, dga = _rms_bwd(dm[:, 0:D_SB], xa, ra, ga_ref[...])
        db, dgb = _rms_bwd(dm[:, D_SB:D_MODEL], xb, rb, gb_ref[...])
        da_ref[...] = da
        db_ref[...] = db
        r = lax.broadcasted_iota(jnp.int32, (LANES, LANES), 0) >= HEAD_DIM
        c = lax.broadcasted_iota(jnp.int32, (LANES, LANES), 1) >= HEAD_DIM
        same_head = jnp.where(r == c, 1.0, 0.0).astype(BF16)
        same_head = jnp.concatenate([same_head, same_head], axis=0)
        prod = db * b_ref[...]
        for k in range(D_DIL // LANES):
            lanes = slice(k * LANES, (k + 1) * LANES)
            dl_ref[:, lanes] = _dot_split(prod[:, lanes], same_head)
        st_ref[0:1, :] += dga
        st_ref[1:2, :] += dgb

    return pl.pallas_call(
        body, name="outproj_bwd", grid=(s // TM,),
        in_specs=[row, half, half, vec, vec, full],
        out_specs=[half, half, half, full, pl.BlockSpec((8, D_SB), lambda i: (0, 0))],
        out_shape=[jax.ShapeDtypeStruct((s, D_SB), F32), jax.ShapeDtypeStruct((s, D_SB), F32),
                   jax.ShapeDtypeStruct((s, D_DIL), F32),
                   jax.ShapeDtypeStruct((D_MODEL, D_MODEL), F32), jax.ShapeDtypeStruct((8, D_SB), F32)],
        compiler_params=_params(48),
    )(dx2, o_sb, o_dl, g_sb, g_dl, wout)


def _head_masks():
    lane = lax.broadcasted_iota(jnp.int32, (BLK, LANES), 1)
    return [lane < HEAD_DIM, lane >= HEAD_DIM]


def _keep(mask, a):
    return a * jnp.where(mask, 1.0, 0.0).astype(a.dtype)


def _suffix_matrices():
    r = lax.broadcasted_iota(jnp.int32, (2 * BLK, BLK), 0) & (BLK - 1)
    c = lax.broadcasted_iota(jnp.int32, (2 * BLK, BLK), 1)
    ones = jnp.ones((2 * BLK, BLK), BF16)
    excl = jnp.concatenate([(r > c).astype(BF16), ones], axis=1)
    incl = jnp.concatenate([(r >= c).astype(BF16), ones], axis=1)
    return excl, incl


def _blk(i):
    return pl.ds(pl.multiple_of(i * BLK, BLK), BLK)


def _alive(carry_m):
    return (jnp.max(carry_m) > DEAD).astype(jnp.int32)


def _more_keys(last, carry):
    return (carry[0] * SB_KB <= last) & (carry[1] > 0)


def _stack_heads(a):
    masks = _head_masks()
    return jnp.concatenate([_keep(masks[0], a), _keep(masks[1], a)], axis=0)


def _unstack_heads(a2):
    return jnp.where(_head_masks()[0], a2[:BLK], a2[BLK:])


def _head_rowsum(a):
    masks = _head_masks()
    return jnp.concatenate([jnp.sum(jnp.where(m, a, 0.0), axis=1, keepdims=True) for m in masks], axis=0)


SB_QB = 2
SB_ROWS = SB_QB * 2 * BLK
SB_KB = 4
PAST_START = 1 << 30


def _sb_rows(ref, i0, cast=None):
    tiles = [ref[_blk(i0 + t), :] for t in range(SB_QB)]
    return jnp.concatenate([_stack_heads(t if cast is None else t.astype(cast)) for t in tiles], axis=0)


_SB_LATER_ROWS = (SB_QB - 1) * 2 * BLK


def _put_rows(full, rows, part):
    return part if rows.start == 0 else jnp.concatenate([full[:rows.start], part], axis=0)


def _sb_scores(q2, k, i, j, carry_m, u_excl):
    r = lax.broadcasted_iota(jnp.int32, (q2.shape[0], BLK), 0)
    row = (r & (BLK - 1)) + ((r >> 8) << 7)
    col = lax.broadcasted_iota(jnp.int32, (q2.shape[0], BLK), 1)
    valid = (jnp.where(j >= 0, j * BLK, PAST_START) + col) < (i * BLK + row)
    z = _dot_nt(q2, k) * SCALE
    sp = jnp.maximum(z, 0.0) + jnp.log(1.0 + jnp.exp(-jnp.abs(z)))
    log_stay = jnp.where(valid, -sp, 0.0)
    log_beta = z - sp
    sums = _dot_split(log_stay, u_excl)
    later = carry_m + sums[:, :BLK]
    w = jnp.where(valid, jnp.exp(log_beta + later), 0.0)
    return valid, log_beta, w, carry_m + sums[:, BLK:]


def _sb_fwd(qkv):
    s = qkv.shape[0]
    nq = s // BLK
    pairs = D_SB // LANES
    col = lambda off: pl.BlockSpec((s, LANES), lambda p: (0, off + p))

    def body(q_ref, k_ref, v_ref, o_ref):
        u_excl, _ = _suffix_matrices()
        zero = jnp.zeros((SB_ROWS, LANES), F32)

        def q_block(ib, _):
            i = ib * SB_QB
            last = i + SB_QB - 1
            q2 = _sb_rows(q_ref, i)

            def trip(jj, carry_m, acc, first):
                for t in range(SB_KB):
                    j = last - jj * SB_KB - t
                    at = _blk(jnp.maximum(j, 0))
                    rows = slice(_SB_LATER_ROWS, SB_ROWS) if first and t == 0 else slice(0, SB_ROWS)
                    base = i + rows.start // (2 * BLK)
                    _, _, w, part = _sb_scores(q2[rows], k_ref[at, :], base, j, carry_m[rows], u_excl)
                    carry_m = _put_rows(carry_m, rows, part)
                    acc = _put_rows(acc, rows, acc[rows] + _dot(w.astype(BF16), v_ref[at, :]))
                return carry_m, acc

            def k_block(carry):
                carry_m, acc = trip(carry[0], carry[2], carry[3], False)
                return carry[0] + 1, _alive(carry_m), carry_m, acc

            carry_m, acc = trip(0, zero, zero, True)
            _, _, _, acc = lax.while_loop(functools.partial(_more_keys, last), k_block,
                                          (jnp.int32(1), _alive(carry_m), carry_m, acc))
            for t in range(SB_QB):
                o_ref[_blk(i + t), :] = _unstack_heads(acc[2 * BLK * t:2 * BLK * (t + 1)])
            return 0

        lax.fori_loop(0, nq // SB_QB, q_block, 0)

    return pl.pallas_call(
        body, name="sb_fwd", grid=(pairs,),
        in_specs=[col(0), col(pairs), col(2 * pairs)],
        out_specs=pl.BlockSpec((s, LANES), lambda p: (0, p)),
        out_shape=jax.ShapeDtypeStruct((s, D_SB), F32),
        compiler_params=_params(48),
    )(qkv, qkv, qkv)


def _sb_bwd(qkv, o_sb, do_sb):
    s = qkv.shape[0]
    nq = s // BLK
    pairs = D_SB // LANES
    col = lambda off: pl.BlockSpec((s, LANES), lambda p, w: (0, off + p))
    own = pl.BlockSpec((s, LANES), lambda p, w: (0, p))

    def body(q_ref, k_ref, v_ref, o_ref, do_ref, out_ref, dq_acc, dk_acc, dv_acc):
        which = pl.program_id(1)

        @pl.when(which == 0)
        def _():
            walk(q_ref, k_ref, v_ref, o_ref, do_ref, dq_acc, dk_acc, dv_acc)
            out_ref[...] = dq_acc[...]

        @pl.when(which == 1)
        def _():
            out_ref[...] = dk_acc[...].astype(BF16)

        @pl.when(which == 2)
        def _():
            out_ref[...] = dv_acc[...].astype(BF16)

    def walk(q_ref, k_ref, v_ref, o_ref, do_ref, dq_ref, dk_acc, dv_acc):
        u_excl, u_incl = _suffix_matrices()
        zero = jnp.zeros((SB_ROWS, LANES), F32)
        dk_acc[...] = jnp.zeros_like(dk_acc)
        dv_acc[...] = jnp.zeros_like(dv_acc)

        def q_block(ib, _):
            i = ib * SB_QB
            last = i + SB_QB - 1
            q2 = _sb_rows(q_ref, i)
            do2 = _sb_rows(do_ref, i, BF16)
            totals = [_head_rowsum(do_ref[_blk(i + t), :].astype(BF16).astype(F32) * o_ref[_blk(i + t), :])
                      for t in range(SB_QB)]
            total = jnp.broadcast_to(jnp.concatenate(totals, axis=0), (SB_ROWS, BLK))

            def trip(jj, carry_m, carry_g, dq, first):
                for t in range(SB_KB):
                    j = last - jj * SB_KB - t
                    at = _blk(jnp.maximum(j, 0))
                    k = k_ref[at, :]
                    rows = slice(_SB_LATER_ROWS, SB_ROWS) if first and t == 0 else slice(0, SB_ROWS)
                    base = i + rows.start // (2 * BLK)
                    valid, log_beta, w, part_m = _sb_scores(q2[rows], k, base, j, carry_m[rows], u_excl)
                    wb = w.astype(BF16)
                    g = wb.astype(F32) * _dot_nt(do2[rows], v_ref[at, :])
                    sums = _dot_split(g, u_incl)
                    before = total[rows] - (carry_g[rows] + sums[:, :BLK])
                    dz = jnp.where(valid, g - jnp.exp(log_beta) * (g + before), 0.0)
                    dzb = (dz * SCALE).astype(BF16)
                    dk_acc[at, :] += _dot_tn(dzb, q2[rows])
                    dv_acc[at, :] += _dot_tn(wb, do2[rows])
                    carry_m = _put_rows(carry_m, rows, part_m)
                    carry_g = _put_rows(carry_g, rows, carry_g[rows] + sums[:, BLK:])
                    dq = _put_rows(dq, rows, dq[rows] + _dot(dzb, k))
                return carry_m, carry_g, dq

            def k_block(carry):
                carry_m, carry_g, dq = trip(carry[0], carry[2], carry[3], carry[4], False)
                return carry[0] + 1, _alive(carry_m), carry_m, carry_g, dq

            carry_m, carry_g, dq = trip(0, zero, zero, zero, True)
            _, _, _, _, dq = lax.while_loop(functools.partial(_more_keys, last), k_block,
                                            (jnp.int32(1), _alive(carry_m), carry_m, carry_g, dq))
            for t in range(SB_QB):
                dq_ref[_blk(i + t), :] = _unstack_heads(dq[2 * BLK * t:2 * BLK * (t + 1)]).astype(BF16)
            return 0

        lax.fori_loop(0, nq // SB_QB, q_block, 0)

    return pl.pallas_call(
        body, name="sb_bwd", grid=(pairs, 3),
        in_specs=[col(0), col(pairs), col(2 * pairs), own, own],
        out_specs=pl.BlockSpec((s, LANES), lambda p, w: (0, w * pairs + p)),
        out_shape=jax.ShapeDtypeStruct((s, 3 * D_SB), BF16),
        scratch_shapes=[pltpu.VMEM((s, LANES), BF16), pltpu.VMEM((s, LANES), F32), pltpu.VMEM((s, LANES), F32)],
        compiler_params=_params(58),
    )(qkv, qkv, qkv, o_sb, do_sb)


DIL_UNROLL = 8


def _band_masks(b):
    row = lax.broadcasted_iota(jnp.int32, (2 * BLK, BLK), 0) & (BLK - 1)
    col = lax.broadcasted_iota(jnp.int32, (2 * BLK, BLK), 1)
    return col <= row, (col - row) >= jnp.where(b > 0, 0, BLK)


def _dil_tiles(qf, kf, vf, d, t, nb):
    c, b = t // nb, t % nb
    start = c + d * BLK * b
    rows = pl.ds(start, BLK, stride=d)
    prev = pl.ds(jnp.where(b > 0, start - d * BLK, start), BLK, stride=d)
    bf = lambda ref, sl: ref[sl, :].astype(BF16)
    return b, rows, prev, _stack_heads(bf(qf, rows)), bf(kf, rows), bf(kf, prev), bf(vf, rows), bf(vf, prev)


def _lanes_of_heads(col2):
    return _unstack_heads(jnp.broadcast_to(col2, (2 * BLK, LANES)))


def _dilated_fwd(qkv):
    s = qkv.shape[0]
    pairs = D_DIL // LANES
    base = (3 * D_SB) // LANES
    col = lambda off: pl.BlockSpec((s, LANES), lambda p: (0, off + p))
    own = pl.BlockSpec((s, LANES), lambda p: (0, p))

    def body(q_ref, k_ref, v_ref, acc_ref, m_ref, qf, kf, vf, l_scr):
        qf[...] = q_ref[...].astype(F32)
        kf[...] = k_ref[...].astype(F32)
        vf[...] = v_ref[...].astype(F32)
        for d in DILATIONS:
            nb = s // (d * BLK)

            def block(t, _):
                b, rows, prev, q2, kc, kp, vc, vp = _dil_tiles(qf, kf, vf, d, t, nb)
                in_cur, in_prev = _band_masks(b)
                zc = jnp.where(in_cur, _dot_nt(q2, kc) * SCALE, NEG)
                zp = jnp.where(in_prev, _dot_nt(q2, kp) * SCALE, NEG)
                m = jnp.maximum(jnp.max(zc, axis=1, keepdims=True), jnp.max(zp, axis=1, keepdims=True))
                pc, pp = jnp.exp(zc - m), jnp.exp(zp - m)
                den = jnp.sum(pc, axis=1, keepdims=True) + jnp.sum(pp, axis=1, keepdims=True)
                acc = _unstack_heads(_dot(pc.astype(BF16), vc) + _dot(pp.astype(BF16), vp))
                m_t, l_t = _lanes_of_heads(m), _lanes_of_heads(den)
                if d == DILATIONS[0]:
                    m_ref[rows, :] = m_t
                    l_scr[rows, :] = l_t
                    acc_ref[rows, :] = acc
                else:
                    m_old = m_ref[rows, :]
                    m_new = jnp.maximum(m_old, m_t)
                    keep, add = jnp.exp(m_old - m_new), jnp.exp(m_t - m_new)
                    m_ref[rows, :] = m_new
                    l_scr[rows, :] = l_scr[rows, :] * keep + l_t * add
                    acc_ref[rows, :] = acc_ref[rows, :] * keep + acc * add
                return 0

            lax.fori_loop(0, s // BLK, block, 0, unroll=DIL_UNROLL)

        def finish(i, _):
            l = l_scr[_blk(i), :]
            acc_ref[_blk(i), :] = acc_ref[_blk(i), :] / l
            m_ref[_blk(i), :] = m_ref[_blk(i), :] + jnp.log(l)
            return 0

        lax.fori_loop(0, s // BLK, finish, 0)

    return pl.pallas_call(
        body, name="dilated_fwd", grid=(pairs,),
        in_specs=[col(base), col(base + pairs), col(base + 2 * pairs)],
        out_specs=[own, own],
        out_shape=[jax.ShapeDtypeStruct((s, D_DIL), F32)] * 2,
        scratch_shapes=[pltpu.VMEM((s, LANES), F32)] * 4,
        compiler_params=_params(56),
    )(qkv, qkv, qkv)


def _stack_lanes(t):
    other = pltpu.roll(t, HEAD_DIM, 1)
    first = _head_masks()[0]
    return jnp.concatenate([jnp.where(first, t, other), jnp.where(first, other, t)], axis=0)


def _dilated_bwd(qkv, delta, lse, dout):
    s = qkv.shape[0]
    pairs = D_DIL // LANES
    base = (3 * D_SB) // LANES
    once = pl.Buffered(1)
    col = lambda off: pl.BlockSpec((s, LANES), lambda p: (0, off + p), pipeline_mode=once)
    own = pl.BlockSpec((s, LANES), lambda p: (0, p), pipeline_mode=once)
    res = pl.BlockSpec((s, LANES), lambda p: (0, p))

    def body(q_ref, k_ref, v_ref, dl_ref, l_ref, do_ref, dq_ref, dk_ref, dv_ref, qf, kf, vf):
        qf[...] = q_ref[...].astype(F32)
        kf[...] = k_ref[...].astype(F32)
        vf[...] = v_ref[...].astype(F32)
        dq_ref[...] = jnp.zeros_like(dq_ref)
        dk_ref[...] = jnp.zeros_like(dk_ref)
        dv_ref[...] = jnp.zeros_like(dv_ref)
        for d in DILATIONS:
            nb = s // (d * BLK)

            def block(t, _):
                b, rows, prev, q2, kc, kp, vc, vp = _dil_tiles(qf, kf, vf, d, t, nb)
                in_cur, in_prev = _band_masks(b)
                do2 = _stack_heads(do_ref[rows, :].astype(BF16))
                delta = _stack_lanes(dl_ref[rows, :])
                lse2 = _stack_lanes(l_ref[rows, :])
                wc = jnp.exp(jnp.where(in_cur, _dot_nt(q2, kc) * SCALE, NEG) - lse2)
                wp = jnp.exp(jnp.where(in_prev, _dot_nt(q2, kp) * SCALE, NEG) - lse2)
                dzc = (wc * (_dot_nt(do2, vc) - delta) * SCALE).astype(BF16)
                dzp = (wp * (_dot_nt(do2, vp) - delta) * SCALE).astype(BF16)
                dq_ref[rows, :] += _unstack_heads(_dot(dzc, kc) + _dot(dzp, kp))
                dk_ref[rows, :] += _dot_tn(dzc, q2)
                dk_ref[prev, :] += _dot_tn(dzp, q2)
                dv_ref[rows, :] += _dot_tn(wc.astype(BF16), do2)
                dv_ref[prev, :] += _dot_tn(wp.astype(BF16), do2)
                return 0

            lax.fori_loop(0, s // BLK, block, 0, unroll=DIL_UNROLL)

    return pl.pallas_call(
        body, name="dilated_bwd", grid=(pairs,),
        in_specs=[col(base), col(base + pairs), col(base + 2 * pairs), own, own, own],
        out_specs=[res, res, res],
        out_shape=[jax.ShapeDtypeStruct((s, D_DIL), F32)] * 3,
        scratch_shapes=[pltpu.VMEM((s, LANES), F32)] * 3,
        compiler_params=_params(60),
    )(qkv, qkv, qkv, delta, lse, dout)


def _dilated_finish(grads, cos, sin):
    s = grads[0].shape[0]
    spec = pl.BlockSpec((TM, D_DIL), lambda i: (i, 0))
    tab = pl.BlockSpec((TM, LANES), lambda i: (i, 0))

    def body(dq_ref, dk_ref, dv_ref, c_ref, s_ref, out_ref):
        for t, (src, rotated) in enumerate(((dq_ref, True), (dk_ref, True), (dv_ref, False))):
            for c in range(D_DIL // LANES):
                piece = src[:, c * LANES:(c + 1) * LANES]
                at = t * D_DIL + c * LANES
                out_ref[:, at:at + LANES] = (_rotate(piece, c_ref[...], -s_ref[...]) if rotated else piece).astype(BF16)

    return pl.pallas_call(
        body, name="dilated_finish", grid=(s // TM,),
        in_specs=[spec] * 3 + [tab, tab], out_specs=pl.BlockSpec((TM, 3 * D_DIL), lambda i: (i, 0)),
        out_shape=jax.ShapeDtypeStruct((s, 3 * D_DIL), BF16),
        compiler_params=_params(32),
    )(*grads, cos, sin)


def _place():
    x, y, c = lax.axis_index("x"), lax.axis_index("y"), lax.axis_index("c")
    return x, y, c, 2 * x + y


def _chip(k, c):
    return (k >> 1, k & 1, c)


def _half(ref, h):
    n = ref.shape[0] // 2
    return ref.at[pl.ds(h * n, n)]


class _BackgroundGather:
    def __init__(self, ins, outs, scratch):
        n = self.n = len(ins)
        self.ins, self.outs = ins, outs
        self.mine, self.landed, self.passed = scratch[0:3 * n:3], scratch[1:3 * n:3], scratch[2:3 * n:3]
        self.send_sem, self.recv_sem, self.local_sem = scratch[3 * n:3 * n + 3]
        x, y, self.c, self.k = _place()
        self.sibling = (x, y, 1 - self.c)

    @staticmethod
    def scratch_shapes(shards):
        shapes = []
        for a in shards:
            half = (N_CHIP - 1, a.shape[0] // 2, a.shape[1])
            shapes += [pltpu.VMEM(a.shape, a.dtype), pltpu.VMEM(half, a.dtype), pltpu.VMEM(half, a.dtype)]
        n = len(shards)
        return shapes + [pltpu.SemaphoreType.DMA((6 * n,)), pltpu.SemaphoreType.DMA((6 * n,)),
                         pltpu.SemaphoreType.DMA((8 * n,))]

    @staticmethod
    def out_shapes(shards):
        return [jax.ShapeDtypeStruct((N_CHIP,) + a.shape, a.dtype) for a in shards]

    def _remote(self, a, slot, src, dst, to):
        return pltpu.make_async_remote_copy(src_ref=src, dst_ref=dst, send_sem=self.send_sem.at[6 * a + slot],
                                            recv_sem=self.recv_sem.at[6 * a + slot], device_id=to, device_id_type=MESH)

    def _local(self, a, slot, src, dst):
        return pltpu.make_async_copy(src, dst, self.local_sem.at[8 * a + slot])

    def _ici(self, a, j):
        return self._remote(a, j - 1, _half(self.mine[a], self.c), self.landed[a].at[j - 1], _chip(self.k ^ j, self.c))

    def _to_sibling(self, a, j):
        return self._remote(a, 2 + j, self.landed[a].at[j - 1], self.passed[a].at[j - 1], self.sibling)

    def _own(self, a):
        return self._local(a, 0, self.ins[a], self.outs[a].at[self.k])

    def _load(self, a):
        return self._local(a, 1, self.ins[a], self.mine[a])

    def _store_landed(self, a, j):
        return self._local(a, 1 + j, self.landed[a].at[j - 1], _half(self.outs[a].at[self.k ^ j], self.c))

    def _store_passed(self, a, j):
        return self._local(a, 4 + j, self.passed[a].at[j - 1], _half(self.outs[a].at[self.k ^ j], 1 - self.c))

    def start(self):
        for a in range(self.n):
            self._own(a).start()
            self._load(a).start()
        for a in range(self.n):
            self._load(a).wait()
            for j in range(1, N_CHIP):
                self._ici(a, j).start()

    def forward(self):
        for j in range(1, N_CHIP):
            for a in range(self.n):
                self._ici(a, j).wait_recv()
                self._to_sibling(a, j).start()
                self._store_landed(a, j).start()

    def finish(self):
        for j in range(1, N_CHIP):
            for a in range(self.n):
                self._to_sibling(a, j).wait_recv()
                self._store_passed(a, j).start()
        for a in range(self.n):
            for j in range(1, N_CHIP):
                self._ici(a, j).wait_send()
                self._to_sibling(a, j).wait_send()
                self._store_landed(a, j).wait()
                self._store_passed(a, j).wait()
            self._own(a).wait()


def _all_gather(shards):
    n = len(shards)
    any_spec = pl.BlockSpec(memory_space=pl.ANY)

    def body(*refs):
        gather = _BackgroundGather(refs[:n], refs[n:2 * n], refs[2 * n:])
        gather.start()
        gather.forward()
        gather.finish()

    return pl.pallas_call(
        body, name="weights_all_gather",
        in_specs=[any_spec] * n, out_specs=[any_spec] * n,
        out_shape=_BackgroundGather.out_shapes(shards),
        scratch_shapes=_BackgroundGather.scratch_shapes(shards),
        compiler_params=_params(32),
    )(*shards)


def _reduce_scatter(g, core, name):
    n, r, c = g.shape
    hr = r // 2
    once = pl.Buffered(1)
    in_specs = [pl.BlockSpec((n, hr, c), lambda i, core_ref: (0, core_ref[0], 0), pipeline_mode=once),
                pl.BlockSpec((n, hr, c), lambda i, core_ref: (0, 1 - core_ref[0], 0), pipeline_mode=once)]

    def body(core_ref, mine_ref, other_ref, out_ref, from_core, sums_bf, from_chips, done, from_core2, send_sem, recv_sem):
        x, y, cc, k = _place()
        sibling = (x, y, 1 - cc)

        def copy(slot, src, dst, to):
            return pltpu.make_async_remote_copy(src_ref=src, dst_ref=dst, send_sem=send_sem.at[slot],
                                                recv_sem=recv_sem.at[slot], device_id=to, device_id_type=MESH)

        from_sibling = [copy(j, other_ref.at[k ^ j], from_core.at[k ^ j], sibling) for j in range(N_CHIP)]
        for j in (1, 2, 3, 0):
            from_sibling[j].start()
        sends = []
        for j in range(1, N_CHIP):
            from_sibling[j].wait()
            sums_bf[j - 1] = (mine_ref[k ^ j] + from_core[k ^ j]).astype(BF16)
            cp = copy(N_CHIP - 1 + j, sums_bf.at[j - 1], from_chips.at[j - 1], _chip(k ^ j, cc))
            cp.start()
            sends.append(cp)
        from_sibling[0].wait()
        red = mine_ref[k] + from_core[k]
        for j in range(1, N_CHIP):
            sends[j - 1].wait()
            red = red + from_chips[j - 1].astype(F32)
        done[...] = red
        last = copy(2 * N_CHIP - 1, done, from_core2, sibling)
        last.start()
        last.wait()
        row0 = pl.multiple_of(cc * hr, 8)
        row1 = pl.multiple_of((1 - cc) * hr, 8)
        out_ref[pl.ds(row0, hr), :] = red
        out_ref[pl.ds(row1, hr), :] = from_core2[...]

    grid_spec = pltpu.PrefetchScalarGridSpec(
        num_scalar_prefetch=1, grid=(1,), in_specs=in_specs,
        out_specs=pl.BlockSpec((r, c), lambda i, core_ref: (0, 0)),
        scratch_shapes=[pltpu.VMEM((n, hr, c), F32), pltpu.VMEM((N_CHIP - 1, hr, c), BF16),
                        pltpu.VMEM((N_CHIP - 1, hr, c), BF16), pltpu.VMEM((hr, c), F32), pltpu.VMEM((hr, c), F32),
                        pltpu.SemaphoreType.DMA((2 * N_CHIP,)), pltpu.SemaphoreType.DMA((2 * N_CHIP,))])
    return pl.pallas_call(
        body, name=name, grid_spec=grid_spec, out_shape=jax.ShapeDtypeStruct((r, c), F32),
        compiler_params=_params(56),
    )(core, g, g)


def _elementwise(fn, name, ins, n_out, rows):
    total, cols = ins[0].shape
    spec = pl.BlockSpec((rows, cols), lambda i: (i, 0))

    def body(*refs):
        res = fn(*[r[...] for r in refs[:len(ins)]])
        for o, v in zip(refs[len(ins):], res):
            o[...] = v

    return pl.pallas_call(
        body, name=name, grid=(total // rows,),
        in_specs=[spec] * len(ins), out_specs=[spec] * n_out,
        out_shape=[jax.ShapeDtypeStruct((total, cols), F32)] * n_out,
        compiler_params=_params(48),
    )(*ins)


def _adamw(w, g, m, v):
    m = ADAM_B1 * m + (1.0 - ADAM_B1) * g
    v = ADAM_B2 * v + (1.0 - ADAM_B2) * (g * g)
    m_hat = m / (1.0 - ADAM_B1 ** ADAM_STEP)
    v_hat = v / (1.0 - ADAM_B2 ** ADAM_STEP)
    delta = -ADAM_LR * (m_hat / (jnp.sqrt(v_hat) + ADAM_EPS) + ADAM_WD * w)
    return delta, m, v


def _reduce_and_update(grads, weights, moms, vels):
    core = lax.axis_index("c").astype(jnp.int32).reshape(1)
    full = [_reduce_scatter(g, core, f"grads_reduce_scatter_{a}") for a, g in enumerate(grads)]
    out = []
    for a, (g, w, m, v) in enumerate(zip(full, weights, moms, vels)):
        rows = g.shape[0] // 2
        out.append((g,) + tuple(_elementwise(lambda gg, ww, mm, vv: _adamw(ww, gg, mm, vv), f"adamw_{a}", [g, w, m, v], 3, rows)))
    return out


def _reduce_vectors(part, w, m, v):
    n_dev = 8

    def body(p_ref, w_ref, m_ref, v_ref, g_ref, d_ref, nm_ref, nv_ref, buf, send_sem, recv_sem):
        x, y, c, _ = _place()
        me = 4 * x + 2 * y + c
        buf[me] = p_ref[...]
        sends = []
        for off in range(1, n_dev):
            peer = me ^ off
            cp = pltpu.make_async_remote_copy(src_ref=p_ref, dst_ref=buf.at[me], send_sem=send_sem.at[off - 1],
                                              recv_sem=recv_sem.at[off - 1], device_id=(peer >> 2, (peer >> 1) & 1, peer & 1),
                                              device_id_type=MESH)
            cp.start()
            sends.append(cp)
        for off in range(1, n_dev):
            peer = me ^ off
            pltpu.make_async_remote_copy(src_ref=p_ref, dst_ref=buf.at[peer], send_sem=send_sem.at[off - 1],
                                         recv_sem=recv_sem.at[off - 1], device_id=(peer >> 2, (peer >> 1) & 1, peer & 1),
                                         device_id_type=MESH).wait_recv()
        for cp in sends:
            cp.wait_send()
        g = buf[0]
        for d in range(1, n_dev):
            g = g + buf[d]
        g_ref[...] = g
        delta, nm, nv = _adamw(w_ref[...], g, m_ref[...], v_ref[...])
        d_ref[...] = delta
        nm_ref[...] = nm
        nv_ref[...] = nv

    vm = pl.BlockSpec(memory_space=pltpu.VMEM)
    return pl.pallas_call(
        body, name="gains_all_reduce",
        in_specs=[vm] * 4, out_specs=[vm] * 4,
        out_shape=[jax.ShapeDtypeStruct(part.shape, F32)] * 4,
        scratch_shapes=[pltpu.VMEM((n_dev,) + part.shape, F32), pltpu.SemaphoreType.DMA((n_dev - 1,)),
                        pltpu.SemaphoreType.DMA((n_dev - 1,))],
    )(part, w, m, v)


def _pad_row(a):
    a = a.reshape(1, -1)
    return jnp.pad(a, ((0, 0), (0, D_MODEL - a.shape[1])))


def kernel(x, ffn1_norm, ffn1_w_gate, ffn1_w_up, ffn1_w_down, mix_norm, w_in, sb_out_norm, dil_out_norm, w_out, ffn2_norm, ffn2_w_gate, ffn2_w_up, ffn2_w_down, final_norm, loss_target, m_ffn1_norm, m_ffn1_w_gate, m_ffn1_w_up, m_ffn1_w_down, m_mix_norm, m_w_in, m_sb_out_norm, m_dil_out_norm, m_w_out, m_ffn2_norm, m_ffn2_w_gate, m_ffn2_w_up, m_ffn2_w_down, m_final_norm, v_ffn1_norm, v_ffn1_w_gate, v_ffn1_w_up, v_ffn1_w_down, v_mix_norm, v_w_in, v_sb_out_norm, v_dil_out_norm, v_w_out, v_ffn2_norm, v_ffn2_w_gate, v_ffn2_w_up, v_ffn2_w_down, v_final_norm):
    x = x[0]
    target = loss_target[0]
    s = x.shape[0]
    gf = final_norm.reshape(1, D_MODEL)
    cos, sin = _rope_tables(s)

    flip = lambda a: a[0].T
    shard = lambda w: w[0].astype(BF16)
    shard_t = lambda w: flip(w).astype(BF16)
    wg1, wu1, wd1 = _all_gather([shard_t(ffn1_w_gate), shard_t(ffn1_w_up), shard(ffn1_w_down)])

    x1, hm, saved1, (win, wout, wd2) = _ffn1_fwd(x, ffn1_norm, mix_norm, (wg1, wu1), wd1,
                                                 [shard(w_in), shard(w_out), shard(ffn2_w_down)])
    wout = wout.reshape(D_MODEL, D_MODEL)
    qkv, (wg2, wu2) = _proj_fwd(hm, win, cos, sin, [shard_t(ffn2_w_gate), shard_t(ffn2_w_up)])
    o_sb = _sb_fwd(qkv)
    o_dl, lse = _dilated_fwd(qkv)
    x2 = _outproj_fwd(o_sb, o_dl, sb_out_norm, dil_out_norm, x1, wout)
    dx3, st_final, saved2 = _ffn2_fwd_loss(x2, ffn2_norm, gf, target, (wg2, wu2), wd2)

    dx2, dwg2, dwu2, dwd2, st_ffn2 = _ffn_bwd(x2, ffn2_norm, dx3, saved2, (wg2, wu2), wd2, 1)
    do_sb, do_dl, delta_dl, dwout, st_out = _outproj_bwd(dx2, o_sb, o_dl, sb_out_norm, dil_out_norm, wout)
    dqkv_sb = _sb_bwd(qkv, o_sb, do_sb)
    dqkv_dl = _dilated_finish(_dilated_bwd(qkv, delta_dl, lse, do_dl), cos, sin)
    dx1, dwin, st_mix = _proj_bwd(x1, mix_norm, dqkv_sb, dqkv_dl, win, dx2)
    grad_x, dwg1, dwu1, dwd1, st_ffn1 = _ffn_bwd(x, ffn1_norm, dx1, saved1, (wg1, wu1), wd1, 0)

    names = ["ffn1_w_gate", "ffn1_w_up", "ffn1_w_down", "w_in", "w_out", "ffn2_w_gate", "ffn2_w_up", "ffn2_w_down"]
    grads = [dwg1, dwu1, dwd1, dwin, dwout.reshape(N_CHIP, OUTB, D_MODEL), dwg2, dwu2, dwd2]
    flipped = {"ffn1_w_gate", "ffn1_w_up", "ffn2_w_gate", "ffn2_w_up"}
    place = lambda n, a: flip(a) if n in flipped else a[0]
    weights = [place(n, a) for n, a in zip(names, [ffn1_w_gate, ffn1_w_up, ffn1_w_down, w_in, w_out, ffn2_w_gate, ffn2_w_up, ffn2_w_down])]
    moms = [place(n, a) for n, a in zip(names, [m_ffn1_w_gate, m_ffn1_w_up, m_ffn1_w_down, m_w_in, m_w_out, m_ffn2_w_gate, m_ffn2_w_up, m_ffn2_w_down])]
    vels = [place(n, a) for n, a in zip(names, [v_ffn1_w_gate, v_ffn1_w_up, v_ffn1_w_down, v_w_in, v_w_out, v_ffn2_w_gate, v_ffn2_w_up, v_ffn2_w_down])]
    mats = {n: tuple((t.T if n in flipped else t)[None] for t in r)
            for n, r in zip(names, _reduce_and_update(grads, weights, moms, vels))}

    vec_names = ["ffn1_norm", "mix_norm", "sb_out_norm", "dil_out_norm", "ffn2_norm", "final_norm"]
    part = jnp.concatenate([st_ffn1[0:1], st_mix[0:1], _pad_row(st_out[0]), _pad_row(st_out[1]), st_ffn2[0:1],
                            st_final[0:1], st_final[1:2], jnp.zeros((1, D_MODEL), F32)], axis=0)
    pack = lambda arrs: jnp.concatenate([_pad_row(a) for a in arrs] + [jnp.zeros((2, D_MODEL), F32)], axis=0)
    g_vec, d_vec, m_vec, v_vec = _reduce_vectors(
        part,
        pack([ffn1_norm, mix_norm, sb_out_norm, dil_out_norm, ffn2_norm, final_norm]),
        pack([m_ffn1_norm, m_mix_norm, m_sb_out_norm, m_dil_out_norm, m_ffn2_norm, m_final_norm]),
        pack([v_ffn1_norm, v_mix_norm, v_sb_out_norm, v_dil_out_norm, v_ffn2_norm, v_final_norm]))
    like = {"ffn1_norm": ffn1_norm, "mix_norm": mix_norm, "sb_out_norm": sb_out_norm, "dil_out_norm": dil_out_norm,
            "ffn2_norm": ffn2_norm, "final_norm": final_norm}
    vecs = {n: tuple(t[i, :like[n].size].reshape(like[n].shape) for t in (g_vec, d_vec, m_vec, v_vec))
            for i, n in enumerate(vec_names)}
    loss = 0.5 * jnp.sum(g_vec[6]) / D_MODEL

    order = ["ffn1_norm", "ffn1_w_gate", "ffn1_w_up", "ffn1_w_down", "mix_norm", "w_in", "sb_out_norm", "dil_out_norm",
             "w_out", "ffn2_norm", "ffn2_w_gate", "ffn2_w_up", "ffn2_w_down", "final_norm"]
    both = {**mats, **vecs}
    return (loss, grad_x[None], *[both[n][0] for n in order], *[both[n][1] for n in order],
            *[both[n][2] for n in order], *[both[n][3] for n in order])
```

```python
import functools

import jax
import jax.numpy as jnp
from jax import lax
from jax.experimental import pallas as pl
from jax.experimental.pallas import tpu as pltpu

D_MODEL = 1024
D_FF = 2816
HEAD_DIM = 64
D_SB = 512
D_DIL = 512
D_IN = 3072
N_CHIP = 4
FFB = D_FF // N_CHIP
INB = D_IN // N_CHIP
OUTB = D_MODEL // N_CHIP
BLK = 128
LANES = 128
DILATIONS = (1, 4, 16)
ROPE_THETA = 10000.0
RMS_EPS = 1e-6
SCALE = HEAD_DIM ** -0.5
NEG = -1e30
DEAD = -104.0
ADAM_LR = 0.001
ADAM_B1 = 0.9
ADAM_B2 = 0.999
ADAM_EPS = 1e-08
ADAM_WD = 0.01
ADAM_STEP = 10
MESH = pl.DeviceIdType.MESH
F32 = jnp.float32
BF16 = jnp.bfloat16
TM = 512


def _params(vmem_mb):
    return pltpu.CompilerParams(vmem_limit_bytes=vmem_mb << 20)


def _dot(a, b):
    return jnp.dot(a, b, preferred_element_type=F32)


def _dot_nt(a, b):
    return lax.dot_general(a, b, (((1,), (1,)), ((), ())), preferred_element_type=F32)


def _dot_tn(a, b):
    return lax.dot_general(a, b, (((0,), (0,)), ((), ())), preferred_element_type=F32)


def _rms_fwd(x, g):
    r = lax.rsqrt(jnp.mean(x * x, axis=-1, keepdims=True) + RMS_EPS)
    xh = x * r
    return xh * g, xh, r


def _rms_bwd(dy, xh, r, g):
    dyg = dy * g
    dx = r * (dyg - xh * jnp.mean(dyg * xh, axis=-1, keepdims=True))
    return dx, jnp.sum(dy * xh, axis=0, keepdims=True)


def _split_bf16(a):
    hi = a.astype(BF16)
    return hi, (a - hi.astype(F32)).astype(BF16)


def _dot_split(a, b2):
    hi, lo = _split_bf16(a)
    return _dot(jnp.concatenate([hi, lo], axis=1), b2)


def _ffn_weight_specs():
    return [pl.BlockSpec((None, FFB, D_MODEL), lambda i, j: (j, 0, 0))] * 3


def _ffn_saved(s):
    hidden = jax.ShapeDtypeStruct((N_CHIP, s, FFB), BF16)
    hid = pl.BlockSpec((None, TM, FFB), lambda i, j: (j, i, 0))
    row = pl.BlockSpec((TM, D_MODEL), lambda i, j: (i, 0))
    return [row, hid, hid, hid], [jax.ShapeDtypeStruct((s, D_MODEL), BF16), hidden, hidden, hidden]


def _ffn_accumulate(h_ref, acc_scr, wg_ref, wu_ref, wd_ref, a_ref, b_ref, act_ref):
    h = h_ref[...]
    a = _dot_nt(h, wg_ref[...])
    b = _dot_nt(h, wu_ref[...])
    act = ((a * jax.nn.sigmoid(a)) * b).astype(BF16)
    a_ref[...] = a.astype(BF16)
    b_ref[...] = b.astype(BF16)
    act_ref[...] = act
    acc_scr[...] += _dot(act, wd_ref[...])


def _host_gather_before(gather, i, j, steps):
    @pl.when((i == 0) & (j == 0))
    def _():
        gather.start()

    @pl.when((i == (3 * steps) // 4) & (j == 0))
    def _():
        gather.forward()


def _host_gather_after(gather, i, j, steps):
    @pl.when((i == steps - 1) & (j == N_CHIP - 1))
    def _():
        gather.finish()


def _ffn1_fwd(x, g1, gmix, gu, wd, later_shards):
    s = x.shape[0]
    row = pl.BlockSpec((TM, D_MODEL), lambda i, j: (i, 0))
    vec = pl.BlockSpec((1, D_MODEL), lambda i, j: (0, 0))
    saved_specs, saved_shapes = _ffn_saved(s)
    n = len(later_shards)
    any_spec = pl.BlockSpec(memory_space=pl.ANY)

    def body(*refs):
        x_ref, g_ref, gm_ref, wg_ref, wu_ref, wd_ref = refs[:6]
        shard_refs, refs = refs[6:6 + n], refs[6 + n:]
        x1_ref, hm_ref, h_ref, a_ref, b_ref, act_ref = refs[:6]
        gathered_refs, acc_scr, gather_scratch = refs[6:6 + n], refs[6 + n], refs[7 + n:]
        gather = _BackgroundGather(shard_refs, gathered_refs, gather_scratch)
        i, j = pl.program_id(0), pl.program_id(1)
        _host_gather_before(gather, i, j, s // TM)

        @pl.when(j == 0)
        def _():
            h, _, _ = _rms_fwd(x_ref[...], g_ref[...])
            h_ref[...] = h.astype(BF16)
            acc_scr[...] = jnp.zeros_like(acc_scr)

        _ffn_accumulate(h_ref, acc_scr, wg_ref, wu_ref, wd_ref, a_ref, b_ref, act_ref)

        @pl.when(j == N_CHIP - 1)
        def _():
            x1 = x_ref[...] + 0.5 * acc_scr[...]
            x1_ref[...] = x1
            hm, _, _ = _rms_fwd(x1, gm_ref[...])
            hm_ref[...] = hm.astype(BF16)

        _host_gather_after(gather, i, j, s // TM)

    x1, hm, h, a, b, act, *gathered = pl.pallas_call(
        body, name="ffn1_fwd", grid=(s // TM, N_CHIP),
        in_specs=[row, vec, vec] + _ffn_weight_specs() + [any_spec] * n,
        out_specs=[row, row] + saved_specs + [any_spec] * n,
        out_shape=([jax.ShapeDtypeStruct((s, D_MODEL), F32), jax.ShapeDtypeStruct((s, D_MODEL), BF16)] + saved_shapes
                   + _BackgroundGather.out_shapes(later_shards)),
        scratch_shapes=[pltpu.VMEM((TM, D_MODEL), F32)] + _BackgroundGather.scratch_shapes(later_shards),
        compiler_params=_params(58),
    )(x, g1, gmix, gu[0], gu[1], wd, *later_shards)
    return x1, hm, [h, a, b, act], gathered


def _ffn2_fwd_loss(x2, g2, gf, target, gu, wd):
    s = x2.shape[0]
    row = pl.BlockSpec((TM, D_MODEL), lambda i, j: (i, 0))
    vec = pl.BlockSpec((1, D_MODEL), lambda i, j: (0, 0))
    stat = pl.BlockSpec((8, D_MODEL), lambda i, j: (0, 0))
    saved_specs, saved_shapes = _ffn_saved(s)

    def body(x_ref, g_ref, gf_ref, t_ref, wg_ref, wu_ref, wd_ref, dx_ref, st_ref, h_ref, a_ref, b_ref, act_ref, acc_scr):
        i, j = pl.program_id(0), pl.program_id(1)

        @pl.when((i == 0) & (j == 0))
        def _():
            st_ref[...] = jnp.zeros_like(st_ref)

        @pl.when(j == 0)
        def _():
            h, _, _ = _rms_fwd(x_ref[...], g_ref[...])
            h_ref[...] = h.astype(BF16)
            acc_scr[...] = jnp.zeros_like(acc_scr)

        _ffn_accumulate(h_ref, acc_scr, wg_ref, wu_ref, wd_ref, a_ref, b_ref, act_ref)

        @pl.when(j == N_CHIP - 1)
        def _():
            x3 = x_ref[...] + 0.5 * acc_scr[...]
            y, xh, r = _rms_fwd(x3, gf_ref[...])
            err = y - t_ref[...]
            dx, dg = _rms_bwd(err * (1.0 / D_MODEL), xh, r, gf_ref[...])
            dx_ref[...] = dx
            st_ref[0:1, :] += dg
            st_ref[1:2, :] += jnp.sum(err * err, axis=0, keepdims=True)

    dx3, st, *saved = pl.pallas_call(
        body, name="ffn2_fwd_loss", grid=(s // TM, N_CHIP),
        in_specs=[row, vec, vec, row] + _ffn_weight_specs(),
        out_specs=[row, stat] + saved_specs,
        out_shape=[jax.ShapeDtypeStruct((s, D_MODEL), F32), jax.ShapeDtypeStruct((8, D_MODEL), F32)] + saved_shapes,
        scratch_shapes=[pltpu.VMEM((TM, D_MODEL), F32)],
        compiler_params=_params(56),
    )(x2, g2, gf, target, gu[0], gu[1], wd)
    return dx3, st, saved


def _ffn_bwd(xin, g, dy, saved, gu, wd, f):
    s = xin.shape[0]
    hb, gate, up, act = saved
    row = pl.BlockSpec((TM, D_MODEL), lambda i, j: (i, 0))
    vec = pl.BlockSpec((1, D_MODEL), lambda i, j: (0, 0))
    stat = pl.BlockSpec((8, D_MODEL), lambda i, j: (0, 0))
    hid = pl.BlockSpec((None, TM, FFB), lambda i, j: (j, i, 0))

    def body(x_ref, g_ref, dy_ref, a_ref, b_ref, wg_ref, wu_ref, wd_ref, out_ref, dyh_ref, da_ref, db_ref, st_ref, dh_scr):
        i, j = pl.program_id(0), pl.program_id(1)

        @pl.when((i == 0) & (j == 0))
        def _():
            st_ref[...] = jnp.zeros_like(st_ref)

        @pl.when(j == 0)
        def _():
            dyh_ref[...] = (0.5 * dy_ref[...]).astype(BF16)
            dh_scr[...] = jnp.zeros_like(dh_scr)

        a = a_ref[...].astype(F32)
        b = b_ref[...].astype(F32)
        sg = jax.nn.sigmoid(a)
        dact = _dot_nt(dyh_ref[...], wd_ref[...])
        dab = (dact * b * (sg * (1.0 + a * (1.0 - sg)))).astype(BF16)
        dbb = (dact * (a * sg)).astype(BF16)
        da_ref[...] = dab
        db_ref[...] = dbb
        dh_scr[...] += _dot(dab, wg_ref[...]) + _dot(dbb, wu_ref[...])

        @pl.when(j == N_CHIP - 1)
        def _():
            _, xh, r = _rms_fwd(x_ref[...], g_ref[...])
            dx, dg = _rms_bwd(dh_scr[...], xh, r, g_ref[...])
            out_ref[...] = dy_ref[...] + dx
            st_ref[0:1, :] += dg

    hidden = jax.ShapeDtypeStruct((N_CHIP, s, FFB), BF16)
    dx, dyh, da, db, st = pl.pallas_call(
        body, name=f"ffn{f + 1}_bwd_dx", grid=(s // TM, N_CHIP),
        in_specs=[row, vec, row, hid, hid] + _ffn_weight_specs(),
        out_specs=[row, row, hid, hid, stat],
        out_shape=[jax.ShapeDtypeStruct((s, D_MODEL), F32), jax.ShapeDtypeStruct((s, D_MODEL), BF16),
                   hidden, hidden, jax.ShapeDtypeStruct((8, D_MODEL), F32)],
        scratch_shapes=[pltpu.VMEM((TM, D_MODEL), F32)],
        compiler_params=_params(56),
    )(xin, g, dy, gate, up, gu[0], gu[1], wd)

    tk = 2 * TM
    tok = pl.BlockSpec((tk, D_MODEL), lambda j, i: (i, 0))
    hid2 = pl.BlockSpec((None, tk, FFB), lambda j, i: (j, i, 0))
    gspecs = [pl.BlockSpec((None, FFB, D_MODEL), lambda j, i: (j, 0, 0))] * 3

    def wbody(h_ref, dyh_ref, da_ref, db_ref, act_ref, dwg_ref, dwu_ref, dwd_ref):
        @pl.when(pl.program_id(1) == 0)
        def _():
            dwg_ref[...] = jnp.zeros_like(dwg_ref)
            dwu_ref[...] = jnp.zeros_like(dwu_ref)
            dwd_ref[...] = jnp.zeros_like(dwd_ref)

        hb = h_ref[...]
        dwg_ref[...] += _dot_tn(da_ref[...], hb)
        dwu_ref[...] += _dot_tn(db_ref[...], hb)
        dwd_ref[...] += _dot_tn(act_ref[...], dyh_ref[...])

    dwg, dwu, dwd = pl.pallas_call(
        wbody, name=f"ffn{f + 1}_bwd_dw", grid=(N_CHIP, s // tk),
        in_specs=[tok, tok, hid2, hid2, hid2], out_specs=gspecs,
        out_shape=[jax.ShapeDtypeStruct((N_CHIP, FFB, D_MODEL), F32)] * 3,
        compiler_params=_params(48),
    )(hb, dyh, da, db, act)
    return dx, dwg, dwu, dwd, st


def _rope_tables(s):
    half = HEAD_DIM // 2
    inv_freq = ROPE_THETA ** (-jnp.arange(half, dtype=F32) / half)
    ang = jnp.arange(s).astype(F32)[:, None] * inv_freq[None, :]
    cos, sin = jnp.cos(ang), jnp.sin(ang)
    cos2 = jnp.concatenate([cos, cos], axis=-1)
    sin2 = jnp.concatenate([-sin, sin], axis=-1)
    return jnp.tile(cos2, (1, LANES // HEAD_DIM)), jnp.tile(sin2, (1, LANES // HEAD_DIM))


def _rotate(t, cos, sin_signed):
    lane = lax.broadcasted_iota(jnp.int32, t.shape, 1)
    first = (lane % HEAD_DIM) < (HEAD_DIM // 2)
    partner = jnp.where(first, pltpu.roll(t, LANES - HEAD_DIM // 2, 1), pltpu.roll(t, HEAD_DIM // 2, 1))
    return t * cos + partner * sin_signed


def _proj_fwd(hm, win, cos, sin, later_shards):
    s = hm.shape[0]
    tm = 2 * TM
    n_sub = INB // LANES
    first_rot, last_rot = (3 * D_SB) // LANES, (3 * D_SB + 2 * D_DIL) // LANES
    n = len(later_shards)
    any_spec = pl.BlockSpec(memory_space=pl.ANY)

    def body(*refs):
        h_ref, w_ref, c_ref, s_ref = refs[:4]
        shard_refs, o_ref, gathered_refs, gather_scratch = refs[4:4 + n], refs[4 + n], refs[5 + n:5 + 2 * n], refs[5 + 2 * n:]
        gather = _BackgroundGather(shard_refs, gathered_refs, gather_scratch)
        i, j = pl.program_id(0), pl.program_id(1)
        _host_gather_before(gather, i, j, s // tm)
        r = _dot(h_ref[...], w_ref[...])
        for c in range(n_sub):
            t = r[:, c * LANES:(c + 1) * LANES]
            col = j * n_sub + c
            rot = (col >= first_rot) & (col < last_rot)
            lanes = slice(c * LANES, (c + 1) * LANES)

            @pl.when(rot)
            def _():
                o_ref[:, lanes] = _rotate(t, c_ref[...], s_ref[...]).astype(BF16)

            @pl.when(jnp.logical_not(rot))
            def _():
                o_ref[:, lanes] = t.astype(BF16)

        _host_gather_after(gather, i, j, s // tm)

    qkv, *gathered = pl.pallas_call(
        body, name="proj_fwd", grid=(s // tm, N_CHIP),
        in_specs=[pl.BlockSpec((tm, D_MODEL), lambda i, j: (i, 0)),
                  pl.BlockSpec((None, D_MODEL, INB), lambda i, j: (j, 0, 0)),
                  pl.BlockSpec((tm, LANES), lambda i, j: (i, 0)),
                  pl.BlockSpec((tm, LANES), lambda i, j: (i, 0))] + [any_spec] * n,
        out_specs=[pl.BlockSpec((tm, INB), lambda i, j: (i, j))] + [any_spec] * n,
        out_shape=[jax.ShapeDtypeStruct((s, D_IN), BF16)] + _BackgroundGather.out_shapes(later_shards),
        scratch_shapes=_BackgroundGather.scratch_shapes(later_shards),
        compiler_params=_params(48),
    )(hm, win, cos, sin, *later_shards)
    return qkv, gathered


def _proj_bwd(x1, gmix, dqkv_sb, dqkv_dl, win, dx2):
    s = x1.shape[0]
    row = pl.BlockSpec((TM, D_MODEL), lambda i, j: (i, 0))
    vec = pl.BlockSpec((1, D_MODEL), lambda i, j: (0, 0))
    per_group = N_CHIP // 2

    def body(x_ref, g_ref, dsb_ref, ddl_ref, w_ref, dx2_ref, out_ref, dw_ref, st_ref, h_scr, dh_scr, dq_ref):
        i, j = pl.program_id(0), pl.program_id(1)

        @pl.when(j < per_group)
        def _():
            dq_ref[...] = dsb_ref[...]

        @pl.when(j >= per_group)
        def _():
            dq_ref[...] = ddl_ref[...]

        @pl.when((i == 0) & (j == 0))
        def _():
            st_ref[...] = jnp.zeros_like(st_ref)
            dw_ref[...] = jnp.zeros_like(dw_ref)

        @pl.when(j == 0)
        def _():
            h, _, _ = _rms_fwd(x_ref[...], g_ref[...])
            h_scr[...] = h.astype(BF16)
            dh_scr[...] = jnp.zeros_like(dh_scr)

        dq = dq_ref[...]
        dw_ref[j] += _dot_tn(h_scr[...], dq)
        dh_scr[...] += _dot_nt(dq, w_ref[...])

        @pl.when(j == N_CHIP - 1)
        def _():
            _, xh, r = _rms_fwd(x_ref[...], g_ref[...])
            dx, dg = _rms_bwd(dh_scr[...], xh, r, g_ref[...])
            out_ref[...] = dx2_ref[...] + dx
            st_ref[0:1, :] += dg

    return pl.pallas_call(
        body, name="proj_bwd", grid=(s // TM, N_CHIP),
        in_specs=[row, vec,
                  pl.BlockSpec((TM, INB), lambda i, j: (i, jnp.minimum(j, per_group - 1))),
                  pl.BlockSpec((TM, INB), lambda i, j: (i, jnp.maximum(j - per_group, 0))),
                  pl.BlockSpec((None, D_MODEL, INB), lambda i, j: (j, 0, 0)), row],
        out_specs=[row, pl.BlockSpec((N_CHIP, D_MODEL, INB), lambda i, j: (0, 0, 0)),
                   pl.BlockSpec((8, D_MODEL), lambda i, j: (0, 0))],
        out_shape=[jax.ShapeDtypeStruct((s, D_MODEL), F32),
                   jax.ShapeDtypeStruct((N_CHIP, D_MODEL, INB), F32),
                   jax.ShapeDtypeStruct((8, D_MODEL), F32)],
        scratch_shapes=[pltpu.VMEM((TM, D_MODEL), BF16), pltpu.VMEM((TM, D_MODEL), F32), pltpu.VMEM((TM, INB), BF16)],
        compiler_params=_params(56),
    )(x1, gmix, dqkv_sb, dqkv_dl, win, dx2)


def _outproj_fwd(o_sb, o_dl, g_sb, g_dl, x1, wout):
    s = x1.shape[0]
    tm = 2 * TM
    half = pl.BlockSpec((tm, D_SB), lambda i: (i, 0))
    row = pl.BlockSpec((tm, D_MODEL), lambda i: (i, 0))
    vec = pl.BlockSpec((1, D_SB), lambda i: (0, 0))

    def body(a_ref, b_ref, ga_ref, gb_ref, x_ref, w_ref, o_ref):
        ma, _, _ = _rms_fwd(a_ref[...], ga_ref[...])
        mb, _, _ = _rms_fwd(b_ref[...], gb_ref[...])
        o_ref[...] = (x_ref[...] + _dot(ma.astype(BF16), w_ref[0:D_SB, :])
                      + _dot(mb.astype(BF16), w_ref[D_SB:D_MODEL, :]))

    return pl.pallas_call(
        body, name="outproj_fwd", grid=(s // tm,),
        in_specs=[half, half, vec, vec, row, pl.BlockSpec((D_MODEL, D_MODEL), lambda i: (0, 0))],
        out_specs=row, out_shape=jax.ShapeDtypeStruct((s, D_MODEL), F32),
        compiler_params=_params(32),
    )(o_sb, o_dl, g_sb, g_dl, x1, wout)


def _outproj_bwd(dx2, o_sb, o_dl, g_sb, g_dl, wout):
    s = dx2.shape[0]
    tm = 2 * TM
    half = pl.BlockSpec((tm, D_SB), lambda i: (i, 0))
    row = pl.BlockSpec((tm, D_MODEL), lambda i: (i, 0))
    vec = pl.BlockSpec((1, D_SB), lambda i: (0, 0))
    full = pl.BlockSpec((D_MODEL, D_MODEL), lambda i: (0, 0))

    def body(dy_ref, a_ref, b_ref, ga_ref, gb_ref, w_ref, da_ref, db_ref, dl_ref, dw_ref, st_ref):
        @pl.when(pl.program_id(0) == 0)
        def _():
            dw_ref[...] = jnp.zeros_like(dw_ref)
            st_ref[...] = jnp.zeros_like(st_ref)

        dy = dy_ref[...].astype(BF16)
        dm = _dot_nt(dy, w_ref[...])
        ma, xa, ra = _rms_fwd(a_ref[...], ga_ref[...])
        mb, xb, rb = _rms_fwd(b_ref[...], gb_ref[...])
        dw_ref[0:D_SB, :] += _dot_tn(ma.astype(BF16), dy)
        dw_ref[D_SB:D_MODEL, :] += _dot_tn(mb.astype(BF16), dy)
        da, dga = _rms_bwd(dm[:, 0:D_SB], xa, ra, ga_ref[...])
        db, dgb = _rms_bwd(dm[:, D_SB:D_MODEL], xb, rb, gb_ref[...])
        da_ref[...] = da
        db_ref[...] = db
        r = lax.broadcasted_iota(jnp.int32, (LANES, LANES), 0) >= HEAD_DIM
        c = lax.broadcasted_iota(jnp.int32, (LANES, LANES), 1) >= HEAD_DIM
        same_head = jnp.where(r == c, 1.0, 0.0).astype(BF16)
        same_head = jnp.concatenate([same_head, same_head], axis=0)
        prod = db * b_ref[...]
        for k in range(D_DIL // LANES):
            lanes = slice(k * LANES, (k + 1) * LANES)
            dl_ref[:, lanes] = _dot_split(prod[:, lanes], same_head)
        st_ref[0:1, :] += dga
        st_ref[1:2, :] += dgb

    return pl.pallas_call(
        body, name="outproj_bwd", grid=(s // tm,),
        in_specs=[row, half, half, vec, vec, full],
        out_specs=[half, half, half, full, pl.BlockSpec((8, D_SB), lambda i: (0, 0))],
        out_shape=[jax.ShapeDtypeStruct((s, D_SB), F32), jax.ShapeDtypeStruct((s, D_SB), F32),
                   jax.ShapeDtypeStruct((s, D_DIL), F32),
                   jax.ShapeDtypeStruct((D_MODEL, D_MODEL), F32), jax.ShapeDtypeStruct((8, D_SB), F32)],
        compiler_params=_params(48),
    )(dx2, o_sb, o_dl, g_sb, g_dl, wout)


def _head_masks():
    lane = lax.broadcasted_iota(jnp.int32, (BLK, LANES), 1)
    return [lane < HEAD_DIM, lane >= HEAD_DIM]


def _keep(mask, a):
    return a * jnp.where(mask, 1.0, 0.0).astype(a.dtype)


def _suffix_matrices():
    r = lax.broadcasted_iota(jnp.int32, (2 * BLK, BLK), 0) & (BLK - 1)
    c = lax.broadcasted_iota(jnp.int32, (2 * BLK, BLK), 1)
    ones = jnp.ones((2 * BLK, BLK), BF16)
    excl = jnp.concatenate([(r > c).astype(BF16), ones], axis=1)
    incl = jnp.concatenate([(r >= c).astype(BF16), ones], axis=1)
    return excl, incl


def _blk(i):
    return pl.ds(pl.multiple_of(i * BLK, BLK), BLK)


def _alive(carry_m):
    return (jnp.max(carry_m) > DEAD).astype(jnp.int32)


def _more_keys(last, carry):
    return (carry[0] * SB_KB <= last) & (carry[1] > 0)


def _stack_heads(a):
    masks = _head_masks()
    return jnp.concatenate([_keep(masks[0], a), _keep(masks[1], a)], axis=0)


def _unstack_heads(a2):
    return jnp.where(_head_masks()[0], a2[:BLK], a2[BLK:])


def _head_rowsum(a):
    masks = _head_masks()
    return jnp.concatenate([jnp.sum(jnp.where(m, a, 0.0), axis=1, keepdims=True) for m in masks], axis=0)


SB_QB = 2
SB_ROWS = SB_QB * 2 * BLK
SB_KB = 4
PAST_START = 1 << 30


def _sb_rows(ref, i0, cast=None):
    tiles = [ref[_blk(i0 + t), :] for t in range(SB_QB)]
    return jnp.concatenate([_stack_heads(t if cast is None else t.astype(cast)) for t in tiles], axis=0)


_SB_LATER_ROWS = (SB_QB - 1) * 2 * BLK


def _put_rows(full, rows, part):
    return part if rows.start == 0 else jnp.concatenate([full[:rows.start], part], axis=0)


def _sb_scores(q2, k, i, j, carry_m, u_excl):
    r = lax.broadcasted_iota(jnp.int32, (q2.shape[0], BLK), 0)
    row = (r & (BLK - 1)) + ((r >> 8) << 7)
    col = lax.broadcasted_iota(jnp.int32, (q2.shape[0], BLK), 1)
    valid = (jnp.where(j >= 0, j * BLK, PAST_START) + col) < (i * BLK + row)
    z = _dot_nt(q2, k) * SCALE
    sp = jnp.maximum(z, 0.0) + jnp.log(1.0 + jnp.exp(-jnp.abs(z)))
    log_stay = jnp.where(valid, -sp, 0.0)
    log_beta = z - sp
    sums = _dot_split(log_stay, u_excl)
    later = carry_m + sums[:, :BLK]
    w = jnp.where(valid, jnp.exp(log_beta + later), 0.0)
    return valid, log_beta, w, carry_m + sums[:, BLK:]


def _sb_fwd(qkv):
    s = qkv.shape[0]
    nq = s // BLK
    pairs = D_SB // LANES
    col = lambda off: pl.BlockSpec((s, LANES), lambda p: (0, off + p))

    def body(q_ref, k_ref, v_ref, o_ref):
        u_excl, _ = _suffix_matrices()
        zero = jnp.zeros((SB_ROWS, LANES), F32)

        def q_block(ib, _):
            i = ib * SB_QB
            last = i + SB_QB - 1
            q2 = _sb_rows(q_ref, i)

            def trip(jj, carry_m, acc, first):
                for t in range(SB_KB):
                    j = last - jj * SB_KB - t
                    at = _blk(jnp.maximum(j, 0))
                    rows = slice(_SB_LATER_ROWS, SB_ROWS) if first and t == 0 else slice(0, SB_ROWS)
                    base = i + rows.start // (2 * BLK)
                    _, _, w, part = _sb_scores(q2[rows], k_ref[at, :], base, j, carry_m[rows], u_excl)
                    carry_m = _put_rows(carry_m, rows, part)
                    acc = _put_rows(acc, rows, acc[rows] + _dot(w.astype(BF16), v_ref[at, :]))
                return carry_m, acc

            def k_block(carry):
                carry_m, acc = trip(carry[0], carry[2], carry[3], False)
                return carry[0] + 1, _alive(carry_m), carry_m, acc

            carry_m, acc = trip(0, zero, zero, True)
            _, _, _, acc = lax.while_loop(functools.partial(_more_keys, last), k_block,
                                          (jnp.int32(1), _alive(carry_m), carry_m, acc))
            for t in range(SB_QB):
                o_ref[_blk(i + t), :] = _unstack_heads(acc[2 * BLK * t:2 * BLK * (t + 1)])
            return 0

        lax.fori_loop(0, nq // SB_QB, q_block, 0)

    return pl.pallas_call(
        body, name="sb_fwd", grid=(pairs,),
        in_specs=[col(0), col(pairs), col(2 * pairs)],
        out_specs=pl.BlockSpec((s, LANES), lambda p: (0, p)),
        out_shape=jax.ShapeDtypeStruct((s, D_SB), F32),
        compiler_params=_params(48),
    )(qkv, qkv, qkv)


def _sb_bwd(qkv, o_sb, do_sb):
    s = qkv.shape[0]
    nq = s // BLK
    pairs = D_SB // LANES
    col = lambda off: pl.BlockSpec((s, LANES), lambda p, w: (0, off + p))
    own = pl.BlockSpec((s, LANES), lambda p, w: (0, p))

    def body(q_ref, k_ref, v_ref, o_ref, do_ref, out_ref, dq_acc, dk_acc, dv_acc):
        which = pl.program_id(1)

        @pl.when(which == 0)
        def _():
            walk(q_ref, k_ref, v_ref, o_ref, do_ref, dq_acc, dk_acc, dv_acc)
            out_ref[...] = dq_acc[...]

        @pl.when(which == 1)
        def _():
            out_ref[...] = dk_acc[...].astype(BF16)

        @pl.when(which == 2)
        def _():
            out_ref[...] = dv_acc[...].astype(BF16)

    def walk(q_ref, k_ref, v_ref, o_ref, do_ref, dq_ref, dk_acc, dv_acc):
        u_excl, u_incl = _suffix_matrices()
        zero = jnp.zeros((SB_ROWS, LANES), F32)
        dk_acc[...] = jnp.zeros_like(dk_acc)
        dv_acc[...] = jnp.zeros_like(dv_acc)

        def q_block(ib, _):
            i = ib * SB_QB
            last = i + SB_QB - 1
            q2 = _sb_rows(q_ref, i)
            do2 = _sb_rows(do_ref, i, BF16)
            totals = [_head_rowsum(do_ref[_blk(i + t), :].astype(BF16).astype(F32) * o_ref[_blk(i + t), :])
                      for t in range(SB_QB)]
            total = jnp.broadcast_to(jnp.concatenate(totals, axis=0), (SB_ROWS, BLK))

            def trip(jj, carry_m, carry_g, dq, first):
                for t in range(SB_KB):
                    j = last - jj * SB_KB - t
                    at = _blk(jnp.maximum(j, 0))
                    k = k_ref[at, :]
                    rows = slice(_SB_LATER_ROWS, SB_ROWS) if first and t == 0 else slice(0, SB_ROWS)
                    base = i + rows.start // (2 * BLK)
                    valid, log_beta, w, part_m = _sb_scores(q2[rows], k, base, j, carry_m[rows], u_excl)
                    wb = w.astype(BF16)
                    g = wb.astype(F32) * _dot_nt(do2[rows], v_ref[at, :])
                    sums = _dot_split(g, u_incl)
                    before = total[rows] - (carry_g[rows] + sums[:, :BLK])
                    dz = jnp.where(valid, g - jnp.exp(log_beta) * (g + before), 0.0)
                    dzb = (dz * SCALE).astype(BF16)
                    dk_acc[at, :] += _dot_tn(dzb, q2[rows])
                    dv_acc[at, :] += _dot_tn(wb, do2[rows])
                    carry_m = _put_rows(carry_m, rows, part_m)
                    carry_g = _put_rows(carry_g, rows, carry_g[rows] + sums[:, BLK:])
                    dq = _put_rows(dq, rows, dq[rows] + _dot(dzb, k))
                return carry_m, carry_g, dq

            def k_block(carry):
                carry_m, carry_g, dq = trip(carry[0], carry[2], carry[3], carry[4], False)
                return carry[0] + 1, _alive(carry_m), carry_m, carry_g, dq

            carry_m, carry_g, dq = trip(0, zero, zero, zero, True)
            _, _, _, _, dq = lax.while_loop(functools.partial(_more_keys, last), k_block,
                                            (jnp.int32(1), _alive(carry_m), carry_m, carry_g, dq))
            for t in range(SB_QB):
                dq_ref[_blk(i + t), :] = _unstack_heads(dq[2 * BLK * t:2 * BLK * (t + 1)]).astype(BF16)
            return 0

        lax.fori_loop(0, nq // SB_QB, q_block, 0)

    return pl.pallas_call(
        body, name="sb_bwd", grid=(pairs, 3),
        in_specs=[col(0), col(pairs), col(2 * pairs), own, own],
        out_specs=pl.BlockSpec((s, LANES), lambda p, w: (0, w * pairs + p)),
        out_shape=jax.ShapeDtypeStruct((s, 3 * D_SB), BF16),
        scratch_shapes=[pltpu.VMEM((s, LANES), BF16), pltpu.VMEM((s, LANES), F32), pltpu.VMEM((s, LANES), F32)],
        compiler_params=_params(58),
    )(qkv, qkv, qkv, o_sb, do_sb)


DIL_UNROLL = 8


def _band_masks(b):
    row = lax.broadcasted_iota(jnp.int32, (2 * BLK, BLK), 0) & (BLK - 1)
    col = lax.broadcasted_iota(jnp.int32, (2 * BLK, BLK), 1)
    return col <= row, (col - row) >= jnp.where(b > 0, 0, BLK)


def _dil_tiles(qf, kf, vf, d, t, nb):
    c, b = t // nb, t % nb
    start = c + d * BLK * b
    rows = pl.ds(start, BLK, stride=d)
    prev = pl.ds(jnp.where(b > 0, start - d * BLK, start), BLK, stride=d)
    bf = lambda ref, sl: ref[sl, :].astype(BF16)
    return b, rows, prev, _stack_heads(bf(qf, rows)), bf(kf, rows), bf(kf, prev), bf(vf, rows), bf(vf, prev)


def _lanes_of_heads(col2):
    return _unstack_heads(jnp.broadcast_to(col2, (2 * BLK, LANES)))


def _dilated_fwd(qkv):
    s = qkv.shape[0]
    pairs = D_DIL // LANES
    base = (3 * D_SB) // LANES
    col = lambda off: pl.BlockSpec((s, LANES), lambda p: (0, off + p))
    own = pl.BlockSpec((s, LANES), lambda p: (0, p))

    def body(q_ref, k_ref, v_ref, acc_ref, m_ref, qf, kf, vf, l_scr):
        qf[...] = q_ref[...].astype(F32)
        kf[...] = k_ref[...].astype(F32)
        vf[...] = v_ref[...].astype(F32)
        for d in DILATIONS:
            nb = s // (d * BLK)

            def block(t, _):
                b, rows, prev, q2, kc, kp, vc, vp = _dil_tiles(qf, kf, vf, d, t, nb)
                in_cur, in_prev = _band_masks(b)
                zc = jnp.where(in_cur, _dot_nt(q2, kc) * SCALE, NEG)
                zp = jnp.where(in_prev, _dot_nt(q2, kp) * SCALE, NEG)
                m = jnp.maximum(jnp.max(zc, axis=1, keepdims=True), jnp.max(zp, axis=1, keepdims=True))
                pc, pp = jnp.exp(zc - m), jnp.exp(zp - m)
                den = jnp.sum(pc, axis=1, keepdims=True) + jnp.sum(pp, axis=1, keepdims=True)
                acc = _unstack_heads(_dot(pc.astype(BF16), vc) + _dot(pp.astype(BF16), vp))
                m_t, l_t = _lanes_of_heads(m), _lanes_of_heads(den)
                if d == DILATIONS[0]:
                    m_ref[rows, :] = m_t
                    l_scr[rows, :] = l_t
                    acc_ref[rows, :] = acc
                else:
                    m_old = m_ref[rows, :]
                    m_new = jnp.maximum(m_old, m_t)
                    keep, add = jnp.exp(m_old - m_new), jnp.exp(m_t - m_new)
                    m_ref[rows, :] = m_new
                    l_scr[rows, :] = l_scr[rows, :] * keep + l_t * add
                    acc_ref[rows, :] = acc_ref[rows, :] * keep + acc * add
                return 0

            lax.fori_loop(0, s // BLK, block, 0, unroll=DIL_UNROLL)

        def finish(i, _):
            l = l_scr[_blk(i), :]
            acc_ref[_blk(i), :] = acc_ref[_blk(i), :] / l
            m_ref[_blk(i), :] = m_ref[_blk(i), :] + jnp.log(l)
            return 0

        lax.fori_loop(0, s // BLK, finish, 0)

    return pl.pallas_call(
        body, name="dilated_fwd", grid=(pairs,),
        in_specs=[col(base), col(base + pairs), col(base + 2 * pairs)],
        out_specs=[own, own],
        out_shape=[jax.ShapeDtypeStruct((s, D_DIL), F32)] * 2,
        scratch_shapes=[pltpu.VMEM((s, LANES), F32)] * 4,
        compiler_params=_params(56),
    )(qkv, qkv, qkv)


def _stack_lanes(t):
    other = pltpu.roll(t, HEAD_DIM, 1)
    first = _head_masks()[0]
    return jnp.concatenate([jnp.where(first, t, other), jnp.where(first, other, t)], axis=0)


def _dilated_bwd(qkv, delta, lse, dout):
    s = qkv.shape[0]
    pairs = D_DIL // LANES
    base = (3 * D_SB) // LANES
    once = pl.Buffered(1)
    col = lambda off: pl.BlockSpec((s, LANES), lambda p: (0, off + p), pipeline_mode=once)
    own = pl.BlockSpec((s, LANES), lambda p: (0, p), pipeline_mode=once)
    res = pl.BlockSpec((s, LANES), lambda p: (0, p))

    def body(q_ref, k_ref, v_ref, dl_ref, l_ref, do_ref, dq_ref, dk_ref, dv_ref, qf, kf, vf):
        qf[...] = q_ref[...].astype(F32)
        kf[...] = k_ref[...].astype(F32)
        vf[...] = v_ref[...].astype(F32)
        dq_ref[...] = jnp.zeros_like(dq_ref)
        dk_ref[...] = jnp.zeros_like(dk_ref)
        dv_ref[...] = jnp.zeros_like(dv_ref)
        for d in DILATIONS:
            nb = s // (d * BLK)

            def block(t, _):
                b, rows, prev, q2, kc, kp, vc, vp = _dil_tiles(qf, kf, vf, d, t, nb)
                in_cur, in_prev = _band_masks(b)
                do2 = _stack_heads(do_ref[rows, :].astype(BF16))
                delta = _stack_lanes(dl_ref[rows, :])
                lse2 = _stack_lanes(l_ref[rows, :])
                wc = jnp.exp(jnp.where(in_cur, _dot_nt(q2, kc) * SCALE, NEG) - lse2)
                wp = jnp.exp(jnp.where(in_prev, _dot_nt(q2, kp) * SCALE, NEG) - lse2)
                dzc = (wc * (_dot_nt(do2, vc) - delta) * SCALE).astype(BF16)
                dzp = (wp * (_dot_nt(do2, vp) - delta) * SCALE).astype(BF16)
                dq_ref[rows, :] += _unstack_heads(_dot(dzc, kc) + _dot(dzp, kp))
                dk_ref[rows, :] += _dot_tn(dzc, q2)
                dk_ref[prev, :] += _dot_tn(dzp, q2)
                dv_ref[rows, :] += _dot_tn(wc.astype(BF16), do2)
                dv_ref[prev, :] += _dot_tn(wp.astype(BF16), do2)
                return 0

            lax.fori_loop(0, s // BLK, block, 0, unroll=DIL_UNROLL)

    return pl.pallas_call(
        body, name="dilated_bwd", grid=(pairs,),
        in_specs=[col(base), col(base + pairs), col(base + 2 * pairs), own, own, own],
        out_specs=[res, res, res],
        out_shape=[jax.ShapeDtypeStruct((s, D_DIL), F32)] * 3,
        scratch_shapes=[pltpu.VMEM((s, LANES), F32)] * 3,
        compiler_params=_params(60),
    )(qkv, qkv, qkv, delta, lse, dout)


def _dilated_finish(grads, cos, sin):
    s = grads[0].shape[0]
    spec = pl.BlockSpec((TM, D_DIL), lambda i: (i, 0))
    tab = pl.BlockSpec((TM, LANES), lambda i: (i, 0))

    def body(dq_ref, dk_ref, dv_ref, c_ref, s_ref, out_ref):
        for t, (src, rotated) in enumerate(((dq_ref, True), (dk_ref, True), (dv_ref, False))):
            for c in range(D_DIL // LANES):
                piece = src[:, c * LANES:(c + 1) * LANES]
                at = t * D_DIL + c * LANES
                out_ref[:, at:at + LANES] = (_rotate(piece, c_ref[...], -s_ref[...]) if rotated else piece).astype(BF16)

    return pl.pallas_call(
        body, name="dilated_finish", grid=(s // TM,),
        in_specs=[spec] * 3 + [tab, tab], out_specs=pl.BlockSpec((TM, 3 * D_DIL), lambda i: (i, 0)),
        out_shape=jax.ShapeDtypeStruct((s, 3 * D_DIL), BF16),
        compiler_params=_params(32),
    )(*grads, cos, sin)


def _place():
    x, y, c = lax.axis_index("x"), lax.axis_index("y"), lax.axis_index("c")
    return x, y, c, 2 * x + y


def _chip(k, c):
    return (k >> 1, k & 1, c)


def _half(ref, h):
    n = ref.shape[0] // 2
    return ref.at[pl.ds(h * n, n)]


class _BackgroundGather:
    def __init__(self, ins, outs, scratch):
        n = self.n = len(ins)
        self.ins, self.outs = ins, outs
        self.mine, self.landed, self.passed = scratch[0:3 * n:3], scratch[1:3 * n:3], scratch[2:3 * n:3]
        self.send_sem, self.recv_sem, self.local_sem = scratch[3 * n:3 * n + 3]
        x, y, self.c, self.k = _place()
        self.sibling = (x, y, 1 - self.c)

    @staticmethod
    def scratch_shapes(shards):
        shapes = []
        for a in shards:
            half = (N_CHIP - 1, a.shape[0] // 2, a.shape[1])
            shapes += [pltpu.VMEM(a.shape, a.dtype), pltpu.VMEM(half, a.dtype), pltpu.VMEM(half, a.dtype)]
        n = len(shards)
        return shapes + [pltpu.SemaphoreType.DMA((6 * n,)), pltpu.SemaphoreType.DMA((6 * n,)),
                         pltpu.SemaphoreType.DMA((8 * n,))]

    @staticmethod
    def out_shapes(shards):
        return [jax.ShapeDtypeStruct((N_CHIP,) + a.shape, a.dtype) for a in shards]

    def _remote(self, a, slot, src, dst, to):
        return pltpu.make_async_remote_copy(src_ref=src, dst_ref=dst, send_sem=self.send_sem.at[6 * a + slot],
                                            recv_sem=self.recv_sem.at[6 * a + slot], device_id=to, device_id_type=MESH)

    def _local(self, a, slot, src, dst):
        return pltpu.make_async_copy(src, dst, self.local_sem.at[8 * a + slot])

    def _ici(self, a, j):
        return self._remote(a, j - 1, _half(self.mine[a], self.c), self.landed[a].at[j - 1], _chip(self.k ^ j, self.c))

    def _to_sibling(self, a, j):
        return self._remote(a, 2 + j, self.landed[a].at[j - 1], self.passed[a].at[j - 1], self.sibling)

    def _own(self, a):
        return self._local(a, 0, self.ins[a], self.outs[a].at[self.k])

    def _load(self, a):
        return self._local(a, 1, self.ins[a], self.mine[a])

    def _store_landed(self, a, j):
        return self._local(a, 1 + j, self.landed[a].at[j - 1], _half(self.outs[a].at[self.k ^ j], self.c))

    def _store_passed(self, a, j):
        return self._local(a, 4 + j, self.passed[a].at[j - 1], _half(self.outs[a].at[self.k ^ j], 1 - self.c))

    def start(self):
        for a in range(self.n):
            self._own(a).start()
            self._load(a).start()
        for a in range(self.n):
            self._load(a).wait()
            for j in range(1, N_CHIP):
                self._ici(a, j).start()

    def forward(self):
        for j in range(1, N_CHIP):
            for a in range(self.n):
                self._ici(a, j).wait_recv()
                self._to_sibling(a, j).start()
                self._store_landed(a, j).start()

    def finish(self):
        for j in range(1, N_CHIP):
            for a in range(self.n):
                self._to_sibling(a, j).wait_recv()
                self._store_passed(a, j).start()
        for a in range(self.n):
            for j in range(1, N_CHIP):
                self._ici(a, j).wait_send()
                self._to_sibling(a, j).wait_send()
                self._store_landed(a, j).wait()
                self._store_passed(a, j).wait()
            self._own(a).wait()


def _all_gather(shards):
    n = len(shards)
    any_spec = pl.BlockSpec(memory_space=pl.ANY)

    def body(*refs):
        gather = _BackgroundGather(refs[:n], refs[n:2 * n], refs[2 * n:])
        gather.start()
        gather.forward()
        gather.finish()

    return pl.pallas_call(
        body, name="weights_all_gather",
        in_specs=[any_spec] * n, out_specs=[any_spec] * n,
        out_shape=_BackgroundGather.out_shapes(shards),
        scratch_shapes=_BackgroundGather.scratch_shapes(shards),
        compiler_params=_params(32),
    )(*shards)


def _reduce_scatter(g, core, name):
    n, r, c = g.shape
    hr = r // 2
    once = pl.Buffered(1)
    in_specs = [pl.BlockSpec((n, hr, c), lambda i, core_ref: (0, core_ref[0], 0), pipeline_mode=once),
                pl.BlockSpec((n, hr, c), lambda i, core_ref: (0, 1 - core_ref[0], 0), pipeline_mode=once)]

    def body(core_ref, mine_ref, other_ref, out_ref, from_core, sums_bf, from_chips, done, from_core2, send_sem, recv_sem):
        x, y, cc, k = _place()
        sibling = (x, y, 1 - cc)

        def copy(slot, src, dst, to):
            return pltpu.make_async_remote_copy(src_ref=src, dst_ref=dst, send_sem=send_sem.at[slot],
                                                recv_sem=recv_sem.at[slot], device_id=to, device_id_type=MESH)

        from_sibling = [copy(j, other_ref.at[k ^ j], from_core.at[k ^ j], sibling) for j in range(N_CHIP)]
        for j in (1, 2, 3, 0):
            from_sibling[j].start()
        sends = []
        for j in range(1, N_CHIP):
            from_sibling[j].wait()
            sums_bf[j - 1] = (mine_ref[k ^ j] + from_core[k ^ j]).astype(BF16)
            cp = copy(N_CHIP - 1 + j, sums_bf.at[j - 1], from_chips.at[j - 1], _chip(k ^ j, cc))
            cp.start()
            sends.append(cp)
        from_sibling[0].wait()
        red = mine_ref[k] + from_core[k]
        for j in range(1, N_CHIP):
            sends[j - 1].wait()
            red = red + from_chips[j - 1].astype(F32)
        done[...] = red
        last = copy(2 * N_CHIP - 1, done, from_core2, sibling)
        last.start()
        last.wait()
        row0 = pl.multiple_of(cc * hr, 8)
        row1 = pl.multiple_of((1 - cc) * hr, 8)
        out_ref[pl.ds(row0, hr), :] = red
        out_ref[pl.ds(row1, hr), :] = from_core2[...]

    grid_spec = pltpu.PrefetchScalarGridSpec(
        num_scalar_prefetch=1, grid=(1,), in_specs=in_specs,
        out_specs=pl.BlockSpec((r, c), lambda i, core_ref: (0, 0)),
        scratch_shapes=[pltpu.VMEM((n, hr, c), F32), pltpu.VMEM((N_CHIP - 1, hr, c), BF16),
                        pltpu.VMEM((N_CHIP - 1, hr, c), BF16), pltpu.VMEM((hr, c), F32), pltpu.VMEM((hr, c), F32),
                        pltpu.SemaphoreType.DMA((2 * N_CHIP,)), pltpu.SemaphoreType.DMA((2 * N_CHIP,))])
    return pl.pallas_call(
        body, name=name, grid_spec=grid_spec, out_shape=jax.ShapeDtypeStruct((r, c), F32),
        compiler_params=_params(56),
    )(core, g, g)


def _elementwise(fn, name, ins, n_out, rows):
    total, cols = ins[0].shape
    spec = pl.BlockSpec((rows, cols), lambda i: (i, 0))

    def body(*refs):
        res = fn(*[r[...] for r in refs[:len(ins)]])
        for o, v in zip(refs[len(ins):], res):
            o[...] = v

    return pl.pallas_call(
        body, name=name, grid=(total // rows,),
        in_specs=[spec] * len(ins), out_specs=[spec] * n_out,
        out_shape=[jax.ShapeDtypeStruct((total, cols), F32)] * n_out,
        compiler_params=_params(48),
    )(*ins)


def _adamw(w, g, m, v):
    m = ADAM_B1 * m + (1.0 - ADAM_B1) * g
    v = ADAM_B2 * v + (1.0 - ADAM_B2) * (g * g)
    m_hat = m / (1.0 - ADAM_B1 ** ADAM_STEP)
    v_hat = v / (1.0 - ADAM_B2 ** ADAM_STEP)
    delta = -ADAM_LR * (m_hat / (jnp.sqrt(v_hat) + ADAM_EPS) + ADAM_WD * w)
    return delta, m, v


def _reduce_and_update(grads, weights, moms, vels):
    core = lax.axis_index("c").astype(jnp.int32).reshape(1)
    full = [_reduce_scatter(g, core, f"grads_reduce_scatter_{a}") for a, g in enumerate(grads)]
    out = []
    for a, (g, w, m, v) in enumerate(zip(full, weights, moms, vels)):
        rows = g.shape[0] // 2
        out.append((g,) + tuple(_elementwise(lambda gg, ww, mm, vv: _adamw(ww, gg, mm, vv), f"adamw_{a}", [g, w, m, v], 3, rows)))
    return out


def _reduce_vectors(part, w, m, v):
    n_dev = 8

    def body(p_ref, w_ref, m_ref, v_ref, g_ref, d_ref, nm_ref, nv_ref, buf, send_sem, recv_sem):
        x, y, c, _ = _place()
        me = 4 * x + 2 * y + c
        buf[me] = p_ref[...]
        sends = []
        for off in range(1, n_dev):
            peer = me ^ off
            cp = pltpu.make_async_remote_copy(src_ref=p_ref, dst_ref=buf.at[me], send_sem=send_sem.at[off - 1],
                                              recv_sem=recv_sem.at[off - 1], device_id=(peer >> 2, (peer >> 1) & 1, peer & 1),
                                              device_id_type=MESH)
            cp.start()
            sends.append(cp)
        for off in range(1, n_dev):
            peer = me ^ off
            pltpu.make_async_remote_copy(src_ref=p_ref, dst_ref=buf.at[peer], send_sem=send_sem.at[off - 1],
                                         recv_sem=recv_sem.at[off - 1], device_id=(peer >> 2, (peer >> 1) & 1, peer & 1),
                                         device_id_type=MESH).wait_recv()
        for cp in sends:
            cp.wait_send()
        g = buf[0]
        for d in range(1, n_dev):
            g = g + buf[d]
        g_ref[...] = g
        delta, nm, nv = _adamw(w_ref[...], g, m_ref[...], v_ref[...])
        d_ref[...] = delta
        nm_ref[...] = nm
        nv_ref[...] = nv

    vm = pl.BlockSpec(memory_space=pltpu.VMEM)
    return pl.pallas_call(
        body, name="gains_all_reduce",
        in_specs=[vm] * 4, out_specs=[vm] * 4,
        out_shape=[jax.ShapeDtypeStruct(part.shape, F32)] * 4,
        scratch_shapes=[pltpu.VMEM((n_dev,) + part.shape, F32), pltpu.SemaphoreType.DMA((n_dev - 1,)),
                        pltpu.SemaphoreType.DMA((n_dev - 1,))],
    )(part, w, m, v)


def _pad_row(a):
    a = a.reshape(1, -1)
    return jnp.pad(a, ((0, 0), (0, D_MODEL - a.shape[1])))


def kernel(x, ffn1_norm, ffn1_w_gate, ffn1_w_up, ffn1_w_down, mix_norm, w_in, sb_out_norm, dil_out_norm, w_out, ffn2_norm, ffn2_w_gate, ffn2_w_up, ffn2_w_down, final_norm, loss_target, m_ffn1_norm, m_ffn1_w_gate, m_ffn1_w_up, m_ffn1_w_down, m_mix_norm, m_w_in, m_sb_out_norm, m_dil_out_norm, m_w_out, m_ffn2_norm, m_ffn2_w_gate, m_ffn2_w_up, m_ffn2_w_down, m_final_norm, v_ffn1_norm, v_ffn1_w_gate, v_ffn1_w_up, v_ffn1_w_down, v_mix_norm, v_w_in, v_sb_out_norm, v_dil_out_norm, v_w_out, v_ffn2_norm, v_ffn2_w_gate, v_ffn2_w_up, v_ffn2_w_down, v_final_norm):
    x = x[0]
    target = loss_target[0]
    s = x.shape[0]
    gf = final_norm.reshape(1, D_MODEL)
    cos, sin = _rope_tables(s)

    flip = lambda a: a[0].T
    shard = lambda w: w[0].astype(BF16)
    shard_t = lambda w: flip(w).astype(BF16)
    wg1, wu1, wd1 = _all_gather([shard_t(ffn1_w_gate), shard_t(ffn1_w_up), shard(ffn1_w_down)])

    x1, hm, saved1, (win, wout, wd2) = _ffn1_fwd(x, ffn1_norm, mix_norm, (wg1, wu1), wd1,
                                                 [shard(w_in), shard(w_out), shard(ffn2_w_down)])
    wout = wout.reshape(D_MODEL, D_MODEL)
    qkv, (wg2, wu2) = _proj_fwd(hm, win, cos, sin, [shard_t(ffn2_w_gate), shard_t(ffn2_w_up)])
    o_sb = _sb_fwd(qkv)
    o_dl, lse = _dilated_fwd(qkv)
    x2 = _outproj_fwd(o_sb, o_dl, sb_out_norm, dil_out_norm, x1, wout)
    dx3, st_final, saved2 = _ffn2_fwd_loss(x2, ffn2_norm, gf, target, (wg2, wu2), wd2)

    dx2, dwg2, dwu2, dwd2, st_ffn2 = _ffn_bwd(x2, ffn2_norm, dx3, saved2, (wg2, wu2), wd2, 1)
    do_sb, do_dl, delta_dl, dwout, st_out = _outproj_bwd(dx2, o_sb, o_dl, sb_out_norm, dil_out_norm, wout)
    dqkv_sb = _sb_bwd(qkv, o_sb, do_sb)
    dqkv_dl = _dilated_finish(_dilated_bwd(qkv, delta_dl, lse, do_dl), cos, sin)
    dx1, dwin, st_mix = _proj_bwd(x1, mix_norm, dqkv_sb, dqkv_dl, win, dx2)
    grad_x, dwg1, dwu1, dwd1, st_ffn1 = _ffn_bwd(x, ffn1_norm, dx1, saved1, (wg1, wu1), wd1, 0)

    names = ["ffn1_w_gate", "ffn1_w_up", "ffn1_w_down", "w_in", "w_out", "ffn2_w_gate", "ffn2_w_up", "ffn2_w_down"]
    grads = [dwg1, dwu1, dwd1, dwin, dwout.reshape(N_CHIP, OUTB, D_MODEL), dwg2, dwu2, dwd2]
    flipped = {"ffn1_w_gate", "ffn1_w_up", "ffn2_w_gate", "ffn2_w_up"}
    place = lambda n, a: flip(a) if n in flipped else a[0]
    weights = [place(n, a) for n, a in zip(names, [ffn1_w_gate, ffn1_w_up, ffn1_w_down, w_in, w_out, ffn2_w_gate, ffn2_w_up, ffn2_w_down])]
    moms = [place(n, a) for n, a in zip(names, [m_ffn1_w_gate, m_ffn1_w_up, m_ffn1_w_down, m_w_in, m_w_out, m_ffn2_w_gate, m_ffn2_w_up, m_ffn2_w_down])]
    vels = [place(n, a) for n, a in zip(names, [v_ffn1_w_gate, v_ffn1_w_up, v_ffn1_w_down, v_w_in, v_w_out, v_ffn2_w_gate, v_ffn2_w_up, v_ffn2_w_down])]
    mats = {n: tuple((t.T if n in flipped else t)[None] for t in r)
            for n, r in zip(names, _reduce_and_update(grads, weights, moms, vels))}

    vec_names = ["ffn1_norm", "mix_norm", "sb_out_norm", "dil_out_norm", "ffn2_norm", "final_norm"]
    part = jnp.concatenate([st_ffn1[0:1], st_mix[0:1], _pad_row(st_out[0]), _pad_row(st_out[1]), st_ffn2[0:1],
                            st_final[0:1], st_final[1:2], jnp.zeros((1, D_MODEL), F32)], axis=0)
    pack = lambda arrs: jnp.concatenate([_pad_row(a) for a in arrs] + [jnp.zeros((2, D_MODEL), F32)], axis=0)
    g_vec, d_vec, m_vec, v_vec = _reduce_vectors(
        part,
        pack([ffn1_norm, mix_norm, sb_out_norm, dil_out_norm, ffn2_norm, final_norm]),
        pack([m_ffn1_norm, m_mix_norm, m_sb_out_norm, m_dil_out_norm, m_ffn2_norm, m_final_norm]),
        pack([v_ffn1_norm, v_mix_norm, v_sb_out_norm, v_dil_out_norm, v_ffn2_norm, v_final_norm]))
    like = {"ffn1_norm": ffn1_norm, "mix_norm": mix_norm, "sb_out_norm": sb_out_norm, "dil_out_norm": dil_out_norm,
            "ffn2_norm": ffn2_norm, "final_norm": final_norm}
    vecs = {n: tuple(t[i, :like[n].size].reshape(like[n].shape) for t in (g_vec, d_vec, m_vec, v_vec))
            for i, n in enumerate(vec_names)}
    loss = 0.5 * jnp.sum(g_vec[6]) / D_MODEL

    order = ["ffn1_norm", "ffn1_w_gate", "ffn1_w_up", "ffn1_w_down", "mix_norm", "w_in", "sb_out_norm", "dil_out_norm",
             "w_out", "ffn2_norm", "ffn2_w_gate", "ffn2_w_up", "ffn2_w_down", "final_norm"]
    both = {**mats, **vecs}
    return (loss, grad_x[None], *[both[n][0] for n in order], *[both[n][1] for n in order],
            *[both[n][2] for n in order], *[both[n][3] for n in order])
```

```python
import functools

import jax
import jax.numpy as jnp
from jax import lax
from jax.experimental import pallas as pl
from jax.experimental.pallas import tpu as pltpu

D_MODEL = 1024
D_FF = 2816
HEAD_DIM = 64
D_SB = 512
D_DIL = 512
D_IN = 3072
N_CHIP = 4
FFB = D_FF // N_CHIP
INB = D_IN // N_CHIP
OUTB = D_MODEL // N_CHIP
BLK = 128
LANES = 128
DILATIONS = (1, 4, 16)
ROPE_THETA = 10000.0
RMS_EPS = 1e-6
SCALE = HEAD_DIM ** -0.5
NEG = -1e30
DEAD = -104.0
ADAM_LR = 0.001
ADAM_B1 = 0.9
ADAM_B2 = 0.999
ADAM_EPS = 1e-08
ADAM_WD = 0.01
ADAM_STEP = 10
MESH = pl.DeviceIdType.MESH
F32 = jnp.float32
BF16 = jnp.bfloat16
TM = 512


def _params(vmem_mb):
    return pltpu.CompilerParams(vmem_limit_bytes=vmem_mb << 20)


def _dot(a, b):
    return jnp.dot(a, b, preferred_element_type=F32)


def _dot_nt(a, b):
    return lax.dot_general(a, b, (((1,), (1,)), ((), ())), preferred_element_type=F32)


def _dot_tn(a, b):
    return lax.dot_general(a, b, (((0,), (0,)), ((), ())), preferred_element_type=F32)


def _rms_fwd(x, g):
    r = lax.rsqrt(jnp.mean(x * x, axis=-1, keepdims=True) + RMS_EPS)
    xh = x * r
    return xh * g, xh, r


def _rms_bwd(dy, xh, r, g):
    dyg = dy * g
    dx = r * (dyg - xh * jnp.mean(dyg * xh, axis=-1, keepdims=True))
    return dx, jnp.sum(dy * xh, axis=0, keepdims=True)


def _split_bf16(a):
    hi = a.astype(BF16)
    return hi, (a - hi.astype(F32)).astype(BF16)


def _dot_split(a, b2):
    hi, lo = _split_bf16(a)
    return _dot(jnp.concatenate([hi, lo], axis=1), b2)


def _ffn_weight_specs():
    return [pl.BlockSpec((None, FFB, D_MODEL), lambda i, j: (j, 0, 0))] * 3


def _ffn_saved(s):
    hidden = jax.ShapeDtypeStruct((N_CHIP, s, FFB), BF16)
    hid = pl.BlockSpec((None, TM, FFB), lambda i, j: (j, i, 0))
    row = pl.BlockSpec((TM, D_MODEL), lambda i, j: (i, 0))
    return [row, hid, hid, hid], [jax.ShapeDtypeStruct((s, D_MODEL), BF16), hidden, hidden, hidden]


def _ffn_accumulate(h_ref, acc_scr, wg_ref, wu_ref, wd_ref, a_ref, b_ref, act_ref):
    h = h_ref[...]
    a = _dot_nt(h, wg_ref[...])
    b = _dot_nt(h, wu_ref[...])
    act = ((a * jax.nn.sigmoid(a)) * b).astype(BF16)
    a_ref[...] = a.astype(BF16)
    b_ref[...] = b.astype(BF16)
    act_ref[...] = act
    acc_scr[...] += _dot(act, wd_ref[...])


def _host_gather_before(gather, i, j, steps):
    @pl.when((i == 0) & (j == 0))
    def _():
        gather.start()

    @pl.when((i == (3 * steps) // 4) & (j == 0))
    def _():
        gather.forward()


def _host_gather_after(gather, i, j, steps):
    @pl.when((i == steps - 1) & (j == N_CHIP - 1))
    def _():
        gather.finish()


def _ffn1_fwd(x, g1, gmix, gu, wd, later_shards):
    s = x.shape[0]
    row = pl.BlockSpec((TM, D_MODEL), lambda i, j: (i, 0))
    vec = pl.BlockSpec((1, D_MODEL), lambda i, j: (0, 0))
    saved_specs, saved_shapes = _ffn_saved(s)
    n = len(later_shards)
    any_spec = pl.BlockSpec(memory_space=pl.ANY)

    def body(*refs):
        x_ref, g_ref, gm_ref, wg_ref, wu_ref, wd_ref = refs[:6]
        shard_refs, refs = refs[6:6 + n], refs[6 + n:]
        x1_ref, hm_ref, h_ref, a_ref, b_ref, act_ref = refs[:6]
        gathered_refs, acc_scr, gather_scratch = refs[6:6 + n], refs[6 + n], refs[7 + n:]
        gather = _BackgroundGather(shard_refs, gathered_refs, gather_scratch)
        i, j = pl.program_id(0), pl.program_id(1)
        _host_gather_before(gather, i, j, s // TM)

        @pl.when(j == 0)
        def _():
            h, _, _ = _rms_fwd(x_ref[...], g_ref[...])
            h_ref[...] = h.astype(BF16)
            acc_scr[...] = jnp.zeros_like(acc_scr)

        _ffn_accumulate(h_ref, acc_scr, wg_ref, wu_ref, wd_ref, a_ref, b_ref, act_ref)

        @pl.when(j == N_CHIP - 1)
        def _():
            x1 = x_ref[...] + 0.5 * acc_scr[...]
            x1_ref[...] = x1
            hm, _, _ = _rms_fwd(x1, gm_ref[...])
            hm_ref[...] = hm.astype(BF16)

        _host_gather_after(gather, i, j, s // TM)

    x1, hm, h, a, b, act, *gathered = pl.pallas_call(
        body, name="ffn1_fwd", grid=(s // TM, N_CHIP),
        in_specs=[row, vec, vec] + _ffn_weight_specs() + [any_spec] * n,
        out_specs=[row, row] + saved_specs + [any_spec] * n,
        out_shape=([jax.ShapeDtypeStruct((s, D_MODEL), F32), jax.ShapeDtypeStruct((s, D_MODEL), BF16)] + saved_shapes
                   + _BackgroundGather.out_shapes(later_shards)),
        scratch_shapes=[pltpu.VMEM((TM, D_MODEL), F32)] + _BackgroundGather.scratch_shapes(later_shards),
        compiler_params=_params(58),
    )(x, g1, gmix, gu[0], gu[1], wd, *later_shards)
    return x1, hm, [h, a, b, act], gathered


def _ffn2_fwd_loss(x2, g2, gf, target, gu, wd):
    s = x2.shape[0]
    row = pl.BlockSpec((TM, D_MODEL), lambda i, j: (i, 0))
    vec = pl.BlockSpec((1, D_MODEL), lambda i, j: (0, 0))
    stat = pl.BlockSpec((8, D_MODEL), lambda i, j: (0, 0))
    saved_specs, saved_shapes = _ffn_saved(s)

    def body(x_ref, g_ref, gf_ref, t_ref, wg_ref, wu_ref, wd_ref, dx_ref, st_ref, h_ref, a_ref, b_ref, act_ref, acc_scr):
        i, j = pl.program_id(0), pl.program_id(1)

        @pl.when((i == 0) & (j == 0))
        def _():
            st_ref[...] = jnp.zeros_like(st_ref)

        @pl.when(j == 0)
        def _():
            h, _, _ = _rms_fwd(x_ref[...], g_ref[...])
            h_ref[...] = h.astype(BF16)
            acc_scr[...] = jnp.zeros_like(acc_scr)

        _ffn_accumulate(h_ref, acc_scr, wg_ref, wu_ref, wd_ref, a_ref, b_ref, act_ref)

        @pl.when(j == N_CHIP - 1)
        def _():
            x3 = x_ref[...] + 0.5 * acc_scr[...]
            y, xh, r = _rms_fwd(x3, gf_ref[...])
            err = y - t_ref[...]
            dx, dg = _rms_bwd(err * (1.0 / D_MODEL), xh, r, gf_ref[...])
            dx_ref[...] = dx
            st_ref[0:1, :] += dg
            st_ref[1:2, :] += jnp.sum(err * err, axis=0, keepdims=True)

    dx3, st, *saved = pl.pallas_call(
        body, name="ffn2_fwd_loss", grid=(s // TM, N_CHIP),
        in_specs=[row, vec, vec, row] + _ffn_weight_specs(),
        out_specs=[row, stat] + saved_specs,
        out_shape=[jax.ShapeDtypeStruct((s, D_MODEL), F32), jax.ShapeDtypeStruct((8, D_MODEL), F32)] + saved_shapes,
        scratch_shapes=[pltpu.VMEM((TM, D_MODEL), F32)],
        compiler_params=_params(56),
    )(x2, g2, gf, target, gu[0], gu[1], wd)
    return dx3, st, saved


def _ffn_bwd(xin, g, dy, saved, gu, wd, f):
    s = xin.shape[0]
    hb, gate, up, act = saved
    row = pl.BlockSpec((TM, D_MODEL), lambda i, j: (i, 0))
    vec = pl.BlockSpec((1, D_MODEL), lambda i, j: (0, 0))
    stat = pl.BlockSpec((8, D_MODEL), lambda i, j: (0, 0))
    hid = pl.BlockSpec((None, TM, FFB), lambda i, j: (j, i, 0))

    def body(x_ref, g_ref, dy_ref, a_ref, b_ref, wg_ref, wu_ref, wd_ref, out_ref, dyh_ref, da_ref, db_ref, st_ref, dh_scr):
        i, j = pl.program_id(0), pl.program_id(1)

        @pl.when((i == 0) & (j == 0))
        def _():
            st_ref[...] = jnp.zeros_like(st_ref)

        @pl.when(j == 0)
        def _():
            dyh_ref[...] = (0.5 * dy_ref[...]).astype(BF16)
            dh_scr[...] = jnp.zeros_like(dh_scr)

        a = a_ref[...].astype(F32)
        b = b_ref[...].astype(F32)
        sg = jax.nn.sigmoid(a)
        dact = _dot_nt(dyh_ref[...], wd_ref[...])
        dab = (dact * b * (sg * (1.0 + a * (1.0 - sg)))).astype(BF16)
        dbb = (dact * (a * sg)).astype(BF16)
        da_ref[...] = dab
        db_ref[...] = dbb
        dh_scr[...] += _dot(dab, wg_ref[...]) + _dot(dbb, wu_ref[...])

        @pl.when(j == N_CHIP - 1)
        def _():
            _, xh, r = _rms_fwd(x_ref[...], g_ref[...])
            dx, dg = _rms_bwd(dh_scr[...], xh, r, g_ref[...])
            out_ref[...] = dy_ref[...] + dx
            st_ref[0:1, :] += dg

    hidden = jax.ShapeDtypeStruct((N_CHIP, s, FFB), BF16)
    dx, dyh, da, db, st = pl.pallas_call(
        body, name=f"ffn{f + 1}_bwd_dx", grid=(s // TM, N_CHIP),
        in_specs=[row, vec, row, hid, hid] + _ffn_weight_specs(),
        out_specs=[row, row, hid, hid, stat],
        out_shape=[jax.ShapeDtypeStruct((s, D_MODEL), F32), jax.ShapeDtypeStruct((s, D_MODEL), BF16),
                   hidden, hidden, jax.ShapeDtypeStruct((8, D_MODEL), F32)],
        scratch_shapes=[pltpu.VMEM((TM, D_MODEL), F32)],
        compiler_params=_params(56),
    )(xin, g, dy, gate, up, gu[0], gu[1], wd)

    tk = 4 * TM
    tok = pl.BlockSpec((tk, D_MODEL), lambda j, i: (i, 0))
    hid2 = pl.BlockSpec((None, tk, FFB), lambda j, i: (j, i, 0))
    gspecs = [pl.BlockSpec((None, FFB, D_MODEL), lambda j, i: (j, 0, 0))] * 3

    def wbody(h_ref, dyh_ref, da_ref, db_ref, act_ref, dwg_ref, dwu_ref, dwd_ref):
        @pl.when(pl.program_id(1) == 0)
        def _():
            dwg_ref[...] = jnp.zeros_like(dwg_ref)
            dwu_ref[...] = jnp.zeros_like(dwu_ref)
            dwd_ref[...] = jnp.zeros_like(dwd_ref)

        hb = h_ref[...]
        dwg_ref[...] += _dot_tn(da_ref[...], hb)
        dwu_ref[...] += _dot_tn(db_ref[...], hb)
        dwd_ref[...] += _dot_tn(act_ref[...], dyh_ref[...])

    dwg, dwu, dwd = pl.pallas_call(
        wbody, name=f"ffn{f + 1}_bwd_dw", grid=(N_CHIP, s // tk),
        in_specs=[tok, tok, hid2, hid2, hid2], out_specs=gspecs,
        out_shape=[jax.ShapeDtypeStruct((N_CHIP, FFB, D_MODEL), F32)] * 3,
        compiler_params=_params(60),
    )(hb, dyh, da, db, act)
    return dx, dwg, dwu, dwd, st


def _rope_tables(s):
    half = HEAD_DIM // 2
    inv_freq = ROPE_THETA ** (-jnp.arange(half, dtype=F32) / half)
    ang = jnp.arange(s).astype(F32)[:, None] * inv_freq[None, :]
    cos, sin = jnp.cos(ang), jnp.sin(ang)
    cos2 = jnp.concatenate([cos, cos], axis=-1)
    sin2 = jnp.concatenate([-sin, sin], axis=-1)
    return jnp.tile(cos2, (1, LANES // HEAD_DIM)), jnp.tile(sin2, (1, LANES // HEAD_DIM))


def _rotate(t, cos, sin_signed):
    lane = lax.broadcasted_iota(jnp.int32, t.shape, 1)
    first = (lane % HEAD_DIM) < (HEAD_DIM // 2)
    partner = jnp.where(first, pltpu.roll(t, LANES - HEAD_DIM // 2, 1), pltpu.roll(t, HEAD_DIM // 2, 1))
    return t * cos + partner * sin_signed


def _proj_fwd(hm, win, cos, sin, later_shards):
    s = hm.shape[0]
    tm = 2 * TM
    n_sub = INB // LANES
    first_rot, last_rot = (3 * D_SB) // LANES, (3 * D_SB + 2 * D_DIL) // LANES
    n = len(later_shards)
    any_spec = pl.BlockSpec(memory_space=pl.ANY)

    def body(*refs):
        h_ref, w_ref, c_ref, s_ref = refs[:4]
        shard_refs, o_ref, gathered_refs, gather_scratch = refs[4:4 + n], refs[4 + n], refs[5 + n:5 + 2 * n], refs[5 + 2 * n:]
        gather = _BackgroundGather(shard_refs, gathered_refs, gather_scratch)
        i, j = pl.program_id(0), pl.program_id(1)
        _host_gather_before(gather, i, j, s // tm)
        r = _dot(h_ref[...], w_ref[...])
        for c in range(n_sub):
            t = r[:, c * LANES:(c + 1) * LANES]
            col = j * n_sub + c
            rot = (col >= first_rot) & (col < last_rot)
            lanes = slice(c * LANES, (c + 1) * LANES)

            @pl.when(rot)
            def _():
                o_ref[:, lanes] = _rotate(t, c_ref[...], s_ref[...]).astype(BF16)

            @pl.when(jnp.logical_not(rot))
            def _():
                o_ref[:, lanes] = t.astype(BF16)

        _host_gather_after(gather, i, j, s // tm)

    qkv, *gathered = pl.pallas_call(
        body, name="proj_fwd", grid=(s // tm, N_CHIP),
        in_specs=[pl.BlockSpec((tm, D_MODEL), lambda i, j: (i, 0)),
                  pl.BlockSpec((None, D_MODEL, INB), lambda i, j: (j, 0, 0)),
                  pl.BlockSpec((tm, LANES), lambda i, j: (i, 0)),
                  pl.BlockSpec((tm, LANES), lambda i, j: (i, 0))] + [any_spec] * n,
        out_specs=[pl.BlockSpec((tm, INB), lambda i, j: (i, j))] + [any_spec] * n,
        out_shape=[jax.ShapeDtypeStruct((s, D_IN), BF16)] + _BackgroundGather.out_shapes(later_shards),
        scratch_shapes=_BackgroundGather.scratch_shapes(later_shards),
        compiler_params=_params(48),
    )(hm, win, cos, sin, *later_shards)
    return qkv, gathered


def _proj_bwd(x1, gmix, dqkv_sb, dqkv_dl, win, dx2):
    s = x1.shape[0]
    row = pl.BlockSpec((TM, D_MODEL), lambda i, j: (i, 0))
    vec = pl.BlockSpec((1, D_MODEL), lambda i, j: (0, 0))
    per_group = N_CHIP // 2

    def body(x_ref, g_ref, dsb_ref, ddl_ref, w_ref, dx2_ref, out_ref, dw_ref, st_ref, h_scr, dh_scr, dq_ref):
        i, j = pl.program_id(0), pl.program_id(1)

        @pl.when(j < per_group)
        def _():
            dq_ref[...] = dsb_ref[...]

        @pl.when(j >= per_group)
        def _():
            dq_ref[...] = ddl_ref[...]

        @pl.when((i == 0) & (j == 0))
        def _():
            st_ref[...] = jnp.zeros_like(st_ref)
            dw_ref[...] = jnp.zeros_like(dw_ref)

        @pl.when(j == 0)
        def _():
            h, _, _ = _rms_fwd(x_ref[...], g_ref[...])
            h_scr[...] = h.astype(BF16)
            dh_scr[...] = jnp.zeros_like(dh_scr)

        dq = dq_ref[...]
        dw_ref[j] += _dot_tn(h_scr[...], dq)
        dh_scr[...] += _dot_nt(dq, w_ref[...])

        @pl.when(j == N_CHIP - 1)
        def _():
            _, xh, r = _rms_fwd(x_ref[...], g_ref[...])
            dx, dg = _rms_bwd(dh_scr[...], xh, r, g_ref[...])
            out_ref[...] = dx2_ref[...] + dx
            st_ref[0:1, :] += dg

    return pl.pallas_call(
        body, name="proj_bwd", grid=(s // TM, N_CHIP),
        in_specs=[row, vec,
                  pl.BlockSpec((TM, INB), lambda i, j: (i, jnp.minimum(j, per_group - 1))),
                  pl.BlockSpec((TM, INB), lambda i, j: (i, jnp.maximum(j - per_group, 0))),
                  pl.BlockSpec((None, D_MODEL, INB), lambda i, j: (j, 0, 0)), row],
        out_specs=[row, pl.BlockSpec((N_CHIP, D_MODEL, INB), lambda i, j: (0, 0, 0)),
                   pl.BlockSpec((8, D_MODEL), lambda i, j: (0, 0))],
        out_shape=[jax.ShapeDtypeStruct((s, D_MODEL), F32),
                   jax.ShapeDtypeStruct((N_CHIP, D_MODEL, INB), F32),
                   jax.ShapeDtypeStruct((8, D_MODEL), F32)],
        scratch_shapes=[pltpu.VMEM((TM, D_MODEL), BF16), pltpu.VMEM((TM, D_MODEL), F32), pltpu.VMEM((TM, INB), BF16)],
        compiler_params=_params(56),
    )(x1, gmix, dqkv_sb, dqkv_dl, win, dx2)


def _outproj_fwd(o_sb, o_dl, g_sb, g_dl, x1, wout):
    s = x1.shape[0]
    tm = 2 * TM
    half = pl.BlockSpec((tm, D_SB), lambda i: (i, 0))
    row = pl.BlockSpec((tm, D_MODEL), lambda i: (i, 0))
    vec = pl.BlockSpec((1, D_SB), lambda i: (0, 0))

    def body(a_ref, b_ref, ga_ref, gb_ref, x_ref, w_ref, o_ref):
        ma, _, _ = _rms_fwd(a_ref[...], ga_ref[...])
        mb, _, _ = _rms_fwd(b_ref[...], gb_ref[...])
        o_ref[...] = (x_ref[...] + _dot(ma.astype(BF16), w_ref[0:D_SB, :])
                      + _dot(mb.astype(BF16), w_ref[D_SB:D_MODEL, :]))

    return pl.pallas_call(
        body, name="outproj_fwd", grid=(s // tm,),
        in_specs=[half, half, vec, vec, row, pl.BlockSpec((D_MODEL, D_MODEL), lambda i: (0, 0))],
        out_specs=row, out_shape=jax.ShapeDtypeStruct((s, D_MODEL), F32),
        compiler_params=_params(32),
    )(o_sb, o_dl, g_sb, g_dl, x1, wout)


def _outproj_bwd(dx2, o_sb, o_dl, g_sb, g_dl, wout):
    s = dx2.shape[0]
    tm = 2 * TM
    half = pl.BlockSpec((tm, D_SB), lambda i: (i, 0))
    row = pl.BlockSpec((tm, D_MODEL), lambda i: (i, 0))
    vec = pl.BlockSpec((1, D_SB), lambda i: (0, 0))
    full = pl.BlockSpec((D_MODEL, D_MODEL), lambda i: (0, 0))

    def body(dy_ref, a_ref, b_ref, ga_ref, gb_ref, w_ref, da_ref, db_ref, dl_ref, dw_ref, st_ref):
        @pl.when(pl.program_id(0) == 0)
        def _():
            dw_ref[...] = jnp.zeros_like(dw_ref)
            st_ref[...] = jnp.zeros_like(st_ref)

        dy = dy_ref[...].astype(BF16)
        dm = _dot_nt(dy, w_ref[...])
        ma, xa, ra = _rms_fwd(a_ref[...], ga_ref[...])
        mb, xb, rb = _rms_fwd(b_ref[...], gb_ref[...])
        dw_ref[0:D_SB, :] += _dot_tn(ma.astype(BF16), dy)
        dw_ref[D_SB:D_MODEL, :] += _dot_tn(mb.astype(BF16), dy)
        da, dga = _rms_bwd(dm[:, 0:D_SB], xa, ra, ga_ref[...])
        db, dgb = _rms_bwd(dm[:, D_SB:D_MODEL], xb, rb, gb_ref[...])
        da_ref[...] = da
        db_ref[...] = db
        r = lax.broadcasted_iota(jnp.int32, (LANES, LANES), 0) >= HEAD_DIM
        c = lax.broadcasted_iota(jnp.int32, (LANES, LANES), 1) >= HEAD_DIM
        same_head = jnp.where(r == c, 1.0, 0.0).astype(BF16)
        same_head = jnp.concatenate([same_head, same_head], axis=0)
        prod = db * b_ref[...]
        for k in range(D_DIL // LANES):
            lanes = slice(k * LANES, (k + 1) * LANES)
            dl_ref[:, lanes] = _dot_split(prod[:, lanes], same_head)
        st_ref[0:1, :] += dga
        st_ref[1:2, :] += dgb

    return pl.pallas_call(
        body, name="outproj_bwd", grid=(s // tm,),
        in_specs=[row, half, half, vec, vec, full],
        out_specs=[half, half, half, full, pl.BlockSpec((8, D_SB), lambda i: (0, 0))],
        out_shape=[jax.ShapeDtypeStruct((s, D_SB), F32), jax.ShapeDtypeStruct((s, D_SB), F32),
                   jax.ShapeDtypeStruct((s, D_DIL), F32),
                   jax.ShapeDtypeStruct((D_MODEL, D_MODEL), F32), jax.ShapeDtypeStruct((8, D_SB), F32)],
        compiler_params=_params(48),
    )(dx2, o_sb, o_dl, g_sb, g_dl, wout)


def _head_masks():
    lane = lax.broadcasted_iota(jnp.int32, (BLK, LANES), 1)
    return [lane < HEAD_DIM, lane >= HEAD_DIM]


def _keep(mask, a):
    return a * jnp.where(mask, 1.0, 0.0).astype(a.dtype)


def _suffix_matrices():
    r = lax.broadcasted_iota(jnp.int32, (2 * BLK, BLK), 0) & (BLK - 1)
    c = lax.broadcasted_iota(jnp.int32, (2 * BLK, BLK), 1)
    ones = jnp.ones((2 * BLK, BLK), BF16)
    excl = jnp.concatenate([(r > c).astype(BF16), ones], axis=1)
    incl = jnp.concatenate([(r >= c).astype(BF16), ones], axis=1)
    return excl, incl


def _blk(i):
    return pl.ds(pl.multiple_of(i * BLK, BLK), BLK)


def _alive(carry_m):
    return (jnp.max(carry_m) > DEAD).astype(jnp.int32)


def _more_keys(last, carry):
    return (carry[0] * SB_KB <= last) & (carry[1] > 0)


def _stack_heads(a):
    masks = _head_masks()
    return jnp.concatenate([_keep(masks[0], a), _keep(masks[1], a)], axis=0)


def _unstack_heads(a2):
    return jnp.where(_head_masks()[0], a2[:BLK], a2[BLK:])


def _head_rowsum(a):
    masks = _head_masks()
    return jnp.concatenate([jnp.sum(jnp.where(m, a, 0.0), axis=1, keepdims=True) for m in masks], axis=0)


SB_QB = 2
SB_ROWS = SB_QB * 2 * BLK
SB_KB = 4
PAST_START = 1 << 30


def _sb_rows(ref, i0, cast=None):
    tiles = [ref[_blk(i0 + t), :] for t in range(SB_QB)]
    return jnp.concatenate([_stack_heads(t if cast is None else t.astype(cast)) for t in tiles], axis=0)


_SB_LATER_ROWS = (SB_QB - 1) * 2 * BLK


def _put_rows(full, rows, part):
    return part if rows.start == 0 else jnp.concatenate([full[:rows.start], part], axis=0)


def _sb_scores(q2, k, i, j, carry_m, u_excl):
    r = lax.broadcasted_iota(jnp.int32, (q2.shape[0], BLK), 0)
    row = (r & (BLK - 1)) + ((r >> 8) << 7)
    col = lax.broadcasted_iota(jnp.int32, (q2.shape[0], BLK), 1)
    valid = (jnp.where(j >= 0, j * BLK, PAST_START) + col) < (i * BLK + row)
    z = _dot_nt(q2, k) * SCALE
    sp = jnp.maximum(z, 0.0) + jnp.log(1.0 + jnp.exp(-jnp.abs(z)))
    log_stay = jnp.where(valid, -sp, 0.0)
    log_beta = z - sp
    sums = _dot_split(log_stay, u_excl)
    later = carry_m + sums[:, :BLK]
    w = jnp.where(valid, jnp.exp(log_beta + later), 0.0)
    return valid, log_beta, w, carry_m + sums[:, BLK:]


def _sb_fwd(qkv):
    s = qkv.shape[0]
    nq = s // BLK
    pairs = D_SB // LANES
    col = lambda off: pl.BlockSpec((s, LANES), lambda p: (0, off + p))

    def body(q_ref, k_ref, v_ref, o_ref):
        u_excl, _ = _suffix_matrices()
        zero = jnp.zeros((SB_ROWS, LANES), F32)

        def q_block(ib, _):
            i = ib * SB_QB
            last = i + SB_QB - 1
            q2 = _sb_rows(q_ref, i)

            def trip(jj, carry_m, acc, first):
                for t in range(SB_KB):
                    j = last - jj * SB_KB - t
                    at = _blk(jnp.maximum(j, 0))
                    rows = slice(_SB_LATER_ROWS, SB_ROWS) if first and t == 0 else slice(0, SB_ROWS)
                    base = i + rows.start // (2 * BLK)
                    _, _, w, part = _sb_scores(q2[rows], k_ref[at, :], base, j, carry_m[rows], u_excl)
                    carry_m = _put_rows(carry_m, rows, part)
                    acc = _put_rows(acc, rows, acc[rows] + _dot(w.astype(BF16), v_ref[at, :]))
                return carry_m, acc

            def k_block(carry):
                carry_m, acc = trip(carry[0], carry[2], carry[3], False)
                return carry[0] + 1, _alive(carry_m), carry_m, acc

            carry_m, acc = trip(0, zero, zero, True)
            _, _, _, acc = lax.while_loop(functools.partial(_more_keys, last), k_block,
                                          (jnp.int32(1), _alive(carry_m), carry_m, acc))
            for t in range(SB_QB):
                o_ref[_blk(i + t), :] = _unstack_heads(acc[2 * BLK * t:2 * BLK * (t + 1)])
            return 0

        lax.fori_loop(0, nq // SB_QB, q_block, 0)

    return pl.pallas_call(
        body, name="sb_fwd", grid=(pairs,),
        in_specs=[col(0), col(pairs), col(2 * pairs)],
        out_specs=pl.BlockSpec((s, LANES), lambda p: (0, p)),
        out_shape=jax.ShapeDtypeStruct((s, D_SB), F32),
        compiler_params=_params(48),
    )(qkv, qkv, qkv)


def _sb_bwd(qkv, o_sb, do_sb):
    s = qkv.shape[0]
    nq = s // BLK
    pairs = D_SB // LANES
    col = lambda off: pl.BlockSpec((s, LANES), lambda p, w: (0, off + p))
    own = pl.BlockSpec((s, LANES), lambda p, w: (0, p))

    def body(q_ref, k_ref, v_ref, o_ref, do_ref, out_ref, dq_acc, dk_acc, dv_acc):
        which = pl.program_id(1)

        @pl.when(which == 0)
        def _():
            walk(q_ref, k_ref, v_ref, o_ref, do_ref, dq_acc, dk_acc, dv_acc)
            out_ref[...] = dq_acc[...]

        @pl.when(which == 1)
        def _():
            out_ref[...] = dk_acc[...].astype(BF16)

        @pl.when(which == 2)
        def _():
            out_ref[...] = dv_acc[...].astype(BF16)

    def walk(q_ref, k_ref, v_ref, o_ref, do_ref, dq_ref, dk_acc, dv_acc):
        u_excl, u_incl = _suffix_matrices()
        zero = jnp.zeros((SB_ROWS, LANES), F32)
        dk_acc[...] = jnp.zeros_like(dk_acc)
        dv_acc[...] = jnp.zeros_like(dv_acc)

        def q_block(ib, _):
            i = ib * SB_QB
            last = i + SB_QB - 1
            q2 = _sb_rows(q_ref, i)
            do2 = _sb_rows(do_ref, i, BF16)
            totals = [_head_rowsum(do_ref[_blk(i + t), :].astype(BF16).astype(F32) * o_ref[_blk(i + t), :])
                      for t in range(SB_QB)]
            total = jnp.broadcast_to(jnp.concatenate(totals, axis=0), (SB_ROWS, BLK))

            def trip(jj, carry_m, carry_g, dq, first):
                for t in range(SB_KB):
                    j = last - jj * SB_KB - t
                    at = _blk(jnp.maximum(j, 0))
                    k = k_ref[at, :]
                    rows = slice(_SB_LATER_ROWS, SB_ROWS) if first and t == 0 else slice(0, SB_ROWS)
                    base = i + rows.start // (2 * BLK)
                    valid, log_beta, w, part_m = _sb_scores(q2[rows], k, base, j, carry_m[rows], u_excl)
                    wb = w.astype(BF16)
                    g = wb.astype(F32) * _dot_nt(do2[rows], v_ref[at, :])
                    sums = _dot_split(g, u_incl)
                    before = total[rows] - (carry_g[rows] + sums[:, :BLK])
                    dz = jnp.where(valid, g - jnp.exp(log_beta) * (g + before), 0.0)
                    dzb = (dz * SCALE).astype(BF16)
                    dk_acc[at, :] += _dot_tn(dzb, q2[rows])
                    dv_acc[at, :] += _dot_tn(wb, do2[rows])
                    carry_m = _put_rows(carry_m, rows, part_m)
                    carry_g = _put_rows(carry_g, rows, carry_g[rows] + sums[:, BLK:])
                    dq = _put_rows(dq, rows, dq[rows] + _dot(dzb, k))
                return carry_m, carry_g, dq

            def k_block(carry):
                carry_m, carry_g, dq = trip(carry[0], carry[2], carry[3], carry[4], False)
                return carry[0] + 1, _alive(carry_m), carry_m, carry_g, dq

            carry_m, carry_g, dq = trip(0, zero, zero, zero, True)
            _, _, _, _, dq = lax.while_loop(functools.partial(_more_keys, last), k_block,
                                            (jnp.int32(1), _alive(carry_m), carry_m, carry_g, dq))
            for t in range(SB_QB):
                dq_ref[_blk(i + t), :] = _unstack_heads(dq[2 * BLK * t:2 * BLK * (t + 1)]).astype(BF16)
            return 0

        lax.fori_loop(0, nq // SB_QB, q_block, 0)

    return pl.pallas_call(
        body, name="sb_bwd", grid=(pairs, 3),
        in_specs=[col(0), col(pairs), col(2 * pairs), own, own],
        out_specs=pl.BlockSpec((s, LANES), lambda p, w: (0, w * pairs + p)),
        out_shape=jax.ShapeDtypeStruct((s, 3 * D_SB), BF16),
        scratch_shapes=[pltpu.VMEM((s, LANES), BF16), pltpu.VMEM((s, LANES), F32), pltpu.VMEM((s, LANES), F32)],
        compiler_params=_params(58),
    )(qkv, qkv, qkv, o_sb, do_sb)


DIL_UNROLL = 8


def _band_masks(b):
    row = lax.broadcasted_iota(jnp.int32, (2 * BLK, BLK), 0) & (BLK - 1)
    col = lax.broadcasted_iota(jnp.int32, (2 * BLK, BLK), 1)
    return col <= row, (col - row) >= jnp.where(b > 0, 0, BLK)


def _dil_tiles(qf, kf, vf, d, t, nb):
    c, b = t // nb, t % nb
    start = c + d * BLK * b
    rows = pl.ds(start, BLK, stride=d)
    prev = pl.ds(jnp.where(b > 0, start - d * BLK, start), BLK, stride=d)
    bf = lambda ref, sl: ref[sl, :].astype(BF16)
    return b, rows, prev, _stack_heads(bf(qf, rows)), bf(kf, rows), bf(kf, prev), bf(vf, rows), bf(vf, prev)


def _lanes_of_heads(col2):
    return _unstack_heads(jnp.broadcast_to(col2, (2 * BLK, LANES)))


def _dilated_fwd(qkv):
    s = qkv.shape[0]
    pairs = D_DIL // LANES
    base = (3 * D_SB) // LANES
    col = lambda off: pl.BlockSpec((s, LANES), lambda p: (0, off + p))
    own = pl.BlockSpec((s, LANES), lambda p: (0, p))

    def body(q_ref, k_ref, v_ref, acc_ref, m_ref, qf, kf, vf, l_scr):
        qf[...] = q_ref[...].astype(F32)
        kf[...] = k_ref[...].astype(F32)
        vf[...] = v_ref[...].astype(F32)
        for d in DILATIONS:
            nb = s // (d * BLK)

            def block(t, _):
                b, rows, prev, q2, kc, kp, vc, vp = _dil_tiles(qf, kf, vf, d, t, nb)
                in_cur, in_prev = _band_masks(b)
                zc = jnp.where(in_cur, _dot_nt(q2, kc) * SCALE, NEG)
                zp = jnp.where(in_prev, _dot_nt(q2, kp) * SCALE, NEG)
                m = jnp.maximum(jnp.max(zc, axis=1, keepdims=True), jnp.max(zp, axis=1, keepdims=True))
                pc, pp = jnp.exp(zc - m), jnp.exp(zp - m)
                den = jnp.sum(pc, axis=1, keepdims=True) + jnp.sum(pp, axis=1, keepdims=True)
                acc = _unstack_heads(_dot(pc.astype(BF16), vc) + _dot(pp.astype(BF16), vp))
                m_t, l_t = _lanes_of_heads(m), _lanes_of_heads(den)
                if d == DILATIONS[0]:
                    m_ref[rows, :] = m_t
                    l_scr[rows, :] = l_t
                    acc_ref[rows, :] = acc
                else:
                    m_old = m_ref[rows, :]
                    m_new = jnp.maximum(m_old, m_t)
                    keep, add = jnp.exp(m_old - m_new), jnp.exp(m_t - m_new)
                    m_ref[rows, :] = m_new
                    l_scr[rows, :] = l_scr[rows, :] * keep + l_t * add
                    acc_ref[rows, :] = acc_ref[rows, :] * keep + acc * add
                return 0

            lax.fori_loop(0, s // BLK, block, 0, unroll=DIL_UNROLL)

        def finish(i, _):
            l = l_scr[_blk(i), :]
            acc_ref[_blk(i), :] = acc_ref[_blk(i), :] / l
            m_ref[_blk(i), :] = m_ref[_blk(i), :] + jnp.log(l)
            return 0

        lax.fori_loop(0, s // BLK, finish, 0)

    return pl.pallas_call(
        body, name="dilated_fwd", grid=(pairs,),
        in_specs=[col(base), col(base + pairs), col(base + 2 * pairs)],
        out_specs=[own, own],
        out_shape=[jax.ShapeDtypeStruct((s, D_DIL), F32)] * 2,
        scratch_shapes=[pltpu.VMEM((s, LANES), F32)] * 4,
        compiler_params=_params(56),
    )(qkv, qkv, qkv)


def _stack_lanes(t):
    other = pltpu.roll(t, HEAD_DIM, 1)
    first = _head_masks()[0]
    return jnp.concatenate([jnp.where(first, t, other), jnp.where(first, other, t)], axis=0)


def _dilated_bwd(qkv, delta, lse, dout):
    s = qkv.shape[0]
    pairs = D_DIL // LANES
    base = (3 * D_SB) // LANES
    once = pl.Buffered(1)
    col = lambda off: pl.BlockSpec((s, LANES), lambda p: (0, off + p), pipeline_mode=once)
    own = pl.BlockSpec((s, LANES), lambda p: (0, p), pipeline_mode=once)
    res = pl.BlockSpec((s, LANES), lambda p: (0, p))

    def body(q_ref, k_ref, v_ref, dl_ref, l_ref, do_ref, dq_ref, dk_ref, dv_ref, qf, kf, vf):
        qf[...] = q_ref[...].astype(F32)
        kf[...] = k_ref[...].astype(F32)
        vf[...] = v_ref[...].astype(F32)
        dq_ref[...] = jnp.zeros_like(dq_ref)
        dk_ref[...] = jnp.zeros_like(dk_ref)
        dv_ref[...] = jnp.zeros_like(dv_ref)
        for d in DILATIONS:
            nb = s // (d * BLK)

            def block(t, _):
                b, rows, prev, q2, kc, kp, vc, vp = _dil_tiles(qf, kf, vf, d, t, nb)
                in_cur, in_prev = _band_masks(b)
                do2 = _stack_heads(do_ref[rows, :].astype(BF16))
                delta = _stack_lanes(dl_ref[rows, :])
                lse2 = _stack_lanes(l_ref[rows, :])
                wc = jnp.exp(jnp.where(in_cur, _dot_nt(q2, kc) * SCALE, NEG) - lse2)
                wp = jnp.exp(jnp.where(in_prev, _dot_nt(q2, kp) * SCALE, NEG) - lse2)
                dzc = (wc * (_dot_nt(do2, vc) - delta) * SCALE).astype(BF16)
                dzp = (wp * (_dot_nt(do2, vp) - delta) * SCALE).astype(BF16)
                dq_ref[rows, :] += _unstack_heads(_dot(dzc, kc) + _dot(dzp, kp))
                dk_ref[rows, :] += _dot_tn(dzc, q2)
                dk_ref[prev, :] += _dot_tn(dzp, q2)
                dv_ref[rows, :] += _dot_tn(wc.astype(BF16), do2)
                dv_ref[prev, :] += _dot_tn(wp.astype(BF16), do2)
                return 0

            lax.fori_loop(0, s // BLK, block, 0, unroll=DIL_UNROLL)

    return pl.pallas_call(
        body, name="dilated_bwd", grid=(pairs,),
        in_specs=[col(base), col(base + pairs), col(base + 2 * pairs), own, own, own],
        out_specs=[res, res, res],
        out_shape=[jax.ShapeDtypeStruct((s, D_DIL), F32)] * 3,
        scratch_shapes=[pltpu.VMEM((s, LANES), F32)] * 3,
        compiler_params=_params(60),
    )(qkv, qkv, qkv, delta, lse, dout)


def _dilated_finish(grads, cos, sin):
    s = grads[0].shape[0]
    spec = pl.BlockSpec((TM, D_DIL), lambda i: (i, 0))
    tab = pl.BlockSpec((TM, LANES), lambda i: (i, 0))

    def body(dq_ref, dk_ref, dv_ref, c_ref, s_ref, out_ref):
        for t, (src, rotated) in enumerate(((dq_ref, True), (dk_ref, True), (dv_ref, False))):
            for c in range(D_DIL // LANES):
                piece = src[:, c * LANES:(c + 1) * LANES]
                at = t * D_DIL + c * LANES
                out_ref[:, at:at + LANES] = (_rotate(piece, c_ref[...], -s_ref[...]) if rotated else piece).astype(BF16)

    return pl.pallas_call(
        body, name="dilated_finish", grid=(s // TM,),
        in_specs=[spec] * 3 + [tab, tab], out_specs=pl.BlockSpec((TM, 3 * D_DIL), lambda i: (i, 0)),
        out_shape=jax.ShapeDtypeStruct((s, 3 * D_DIL), BF16),
        compiler_params=_params(32),
    )(*grads, cos, sin)


def _place():
    x, y, c = lax.axis_index("x"), lax.axis_index("y"), lax.axis_index("c")
    return x, y, c, 2 * x + y


def _chip(k, c):
    return (k >> 1, k & 1, c)


def _half(ref, h):
    n = ref.shape[0] // 2
    return ref.at[pl.ds(h * n, n)]


class _BackgroundGather:
    def __init__(self, ins, outs, scratch):
        n = self.n = len(ins)
        self.ins, self.outs = ins, outs
        self.mine, self.landed, self.passed = scratch[0:3 * n:3], scratch[1:3 * n:3], scratch[2:3 * n:3]
        self.send_sem, self.recv_sem, self.local_sem = scratch[3 * n:3 * n + 3]
        x, y, self.c, self.k = _place()
        self.sibling = (x, y, 1 - self.c)

    @staticmethod
    def scratch_shapes(shards):
        shapes = []
        for a in shards:
            half = (N_CHIP - 1, a.shape[0] // 2, a.shape[1])
            shapes += [pltpu.VMEM(a.shape, a.dtype), pltpu.VMEM(half, a.dtype), pltpu.VMEM(half, a.dtype)]
        n = len(shards)
        return shapes + [pltpu.SemaphoreType.DMA((6 * n,)), pltpu.SemaphoreType.DMA((6 * n,)),
                         pltpu.SemaphoreType.DMA((8 * n,))]

    @staticmethod
    def out_shapes(shards):
        return [jax.ShapeDtypeStruct((N_CHIP,) + a.shape, a.dtype) for a in shards]

    def _remote(self, a, slot, src, dst, to):
        return pltpu.make_async_remote_copy(src_ref=src, dst_ref=dst, send_sem=self.send_sem.at[6 * a + slot],
                                            recv_sem=self.recv_sem.at[6 * a + slot], device_id=to, device_id_type=MESH)

    def _local(self, a, slot, src, dst):
        return pltpu.make_async_copy(src, dst, self.local_sem.at[8 * a + slot])

    def _ici(self, a, j):
        return self._remote(a, j - 1, _half(self.mine[a], self.c), self.landed[a].at[j - 1], _chip(self.k ^ j, self.c))

    def _to_sibling(self, a, j):
        return self._remote(a, 2 + j, self.landed[a].at[j - 1], self.passed[a].at[j - 1], self.sibling)

    def _own(self, a):
        return self._local(a, 0, self.ins[a], self.outs[a].at[self.k])

    def _load(self, a):
        return self._local(a, 1, self.ins[a], self.mine[a])

    def _store_landed(self, a, j):
        return self._local(a, 1 + j, self.landed[a].at[j - 1], _half(self.outs[a].at[self.k ^ j], self.c))

    def _store_passed(self, a, j):
        return self._local(a, 4 + j, self.passed[a].at[j - 1], _half(self.outs[a].at[self.k ^ j], 1 - self.c))

    def start(self):
        for a in range(self.n):
            self._own(a).start()
            self._load(a).start()
        for a in range(self.n):
            self._load(a).wait()
            for j in range(1, N_CHIP):
                self._ici(a, j).start()

    def forward(self):
        for j in range(1, N_CHIP):
            for a in range(self.n):
                self._ici(a, j).wait_recv()
                self._to_sibling(a, j).start()
                self._store_landed(a, j).start()

    def finish(self):
        for j in range(1, N_CHIP):
            for a in range(self.n):
                self._to_sibling(a, j).wait_recv()
                self._store_passed(a, j).start()
        for a in range(self.n):
            for j in range(1, N_CHIP):
                self._ici(a, j).wait_send()
                self._to_sibling(a, j).wait_send()
                self._store_landed(a, j).wait()
                self._store_passed(a, j).wait()
            self._own(a).wait()


def _all_gather(shards):
    n = len(shards)
    any_spec = pl.BlockSpec(memory_space=pl.ANY)

    def body(*refs):
        gather = _BackgroundGather(refs[:n], refs[n:2 * n], refs[2 * n:])
        gather.start()
        gather.forward()
        gather.finish()

    return pl.pallas_call(
        body, name="weights_all_gather",
        in_specs=[any_spec] * n, out_specs=[any_spec] * n,
        out_shape=_BackgroundGather.out_shapes(shards),
        scratch_shapes=_BackgroundGather.scratch_shapes(shards),
        compiler_params=_params(32),
    )(*shards)


def _reduce_scatter(g, core, name):
    n, r, c = g.shape
    hr = r // 2
    once = pl.Buffered(1)
    in_specs = [pl.BlockSpec((n, hr, c), lambda i, core_ref: (0, core_ref[0], 0), pipeline_mode=once),
                pl.BlockSpec((n, hr, c), lambda i, core_ref: (0, 1 - core_ref[0], 0), pipeline_mode=once)]

    def body(core_ref, mine_ref, other_ref, out_ref, from_core, sums_bf, from_chips, done, from_core2, send_sem, recv_sem):
        x, y, cc, k = _place()
        sibling = (x, y, 1 - cc)

        def copy(slot, src, dst, to):
            return pltpu.make_async_remote_copy(src_ref=src, dst_ref=dst, send_sem=send_sem.at[slot],
                                                recv_sem=recv_sem.at[slot], device_id=to, device_id_type=MESH)

        from_sibling = [copy(j, other_ref.at[k ^ j], from_core.at[k ^ j], sibling) for j in range(N_CHIP)]
        for j in (1, 2, 3, 0):
            from_sibling[j].start()
        sends = []
        for j in range(1, N_CHIP):
            from_sibling[j].wait()
            sums_bf[j - 1] = (mine_ref[k ^ j] + from_core[k ^ j]).astype(BF16)
            cp = copy(N_CHIP - 1 + j, sums_bf.at[j - 1], from_chips.at[j - 1], _chip(k ^ j, cc))
            cp.start()
            sends.append(cp)
        from_sibling[0].wait()
        red = mine_ref[k] + from_core[k]
        for j in range(1, N_CHIP):
            sends[j - 1].wait()
            red = red + from_chips[j - 1].astype(F32)
        done[...] = red
        last = copy(2 * N_CHIP - 1, done, from_core2, sibling)
        last.start()
        last.wait()
        row0 = pl.multiple_of(cc * hr, 8)
        row1 = pl.multiple_of((1 - cc) * hr, 8)
        out_ref[pl.ds(row0, hr), :] = red
        out_ref[pl.ds(row1, hr), :] = from_core2[...]

    grid_spec = pltpu.PrefetchScalarGridSpec(
        num_scalar_prefetch=1, grid=(1,), in_specs=in_specs,
        out_specs=pl.BlockSpec((r, c), lambda i, core_ref: (0, 0)),
        scratch_shapes=[pltpu.VMEM((n, hr, c), F32), pltpu.VMEM((N_CHIP - 1, hr, c), BF16),
                        pltpu.VMEM((N_CHIP - 1, hr, c), BF16), pltpu.VMEM((hr, c), F32), pltpu.VMEM((hr, c), F32),
                        pltpu.SemaphoreType.DMA((2 * N_CHIP,)), pltpu.SemaphoreType.DMA((2 * N_CHIP,))])
    return pl.pallas_call(
        body, name=name, grid_spec=grid_spec, out_shape=jax.ShapeDtypeStruct((r, c), F32),
        compiler_params=_params(56),
    )(core, g, g)


def _elementwise(fn, name, ins, n_out, rows):
    total, cols = ins[0].shape
    spec = pl.BlockSpec((rows, cols), lambda i: (i, 0))

    def body(*refs):
        res = fn(*[r[...] for r in refs[:len(ins)]])
        for o, v in zip(refs[len(ins):], res):
            o[...] = v

    return pl.pallas_call(
        body, name=name, grid=(total // rows,),
        in_specs=[spec] * len(ins), out_specs=[spec] * n_out,
        out_shape=[jax.ShapeDtypeStruct((total, cols), F32)] * n_out,
        compiler_params=_params(48),
    )(*ins)


def _adamw(w, g, m, v):
    m = ADAM_B1 * m + (1.0 - ADAM_B1) * g
    v = ADAM_B2 * v + (1.0 - ADAM_B2) * (g * g)
    m_hat = m / (1.0 - ADAM_B1 ** ADAM_STEP)
    v_hat = v / (1.0 - ADAM_B2 ** ADAM_STEP)
    delta = -ADAM_LR * (m_hat / (jnp.sqrt(v_hat) + ADAM_EPS) + ADAM_WD * w)
    return delta, m, v


def _reduce_and_update(grads, weights, moms, vels):
    core = lax.axis_index("c").astype(jnp.int32).reshape(1)
    full = [_reduce_scatter(g, core, f"grads_reduce_scatter_{a}") for a, g in enumerate(grads)]
    out = []
    for a, (g, w, m, v) in enumerate(zip(full, weights, moms, vels)):
        rows = g.shape[0] // 2
        out.append((g,) + tuple(_elementwise(lambda gg, ww, mm, vv: _adamw(ww, gg, mm, vv), f"adamw_{a}", [g, w, m, v], 3, rows)))
    return out


def _reduce_vectors(part, w, m, v):
    n_dev = 8

    def body(p_ref, w_ref, m_ref, v_ref, g_ref, d_ref, nm_ref, nv_ref, buf, send_sem, recv_sem):
        x, y, c, _ = _place()
        me = 4 * x + 2 * y + c
        buf[me] = p_ref[...]
        sends = []
        for off in range(1, n_dev):
            peer = me ^ off
            cp = pltpu.make_async_remote_copy(src_ref=p_ref, dst_ref=buf.at[me], send_sem=send_sem.at[off - 1],
                                              recv_sem=recv_sem.at[off - 1], device_id=(peer >> 2, (peer >> 1) & 1, peer & 1),
                                              device_id_type=MESH)
            cp.start()
            sends.append(cp)
        for off in range(1, n_dev):
            peer = me ^ off
            pltpu.make_async_remote_copy(src_ref=p_ref, dst_ref=buf.at[peer], send_sem=send_sem.at[off - 1],
                                         recv_sem=recv_sem.at[off - 1], device_id=(peer >> 2, (peer >> 1) & 1, peer & 1),
                                         device_id_type=MESH).wait_recv()
        for cp in sends:
            cp.wait_send()
        g = buf[0]
        for d in range(1, n_dev):
            g = g + buf[d]
        g_ref[...] = g
        delta, nm, nv = _adamw(w_ref[...], g, m_ref[...], v_ref[...])
        d_ref[...] = delta
        nm_ref[...] = nm
        nv_ref[...] = nv

    vm = pl.BlockSpec(memory_space=pltpu.VMEM)
    return pl.pallas_call(
        body, name="gains_all_reduce",
        in_specs=[vm] * 4, out_specs=[vm] * 4,
        out_shape=[jax.ShapeDtypeStruct(part.shape, F32)] * 4,
        scratch_shapes=[pltpu.VMEM((n_dev,) + part.shape, F32), pltpu.SemaphoreType.DMA((n_dev - 1,)),
                        pltpu.SemaphoreType.DMA((n_dev - 1,))],
    )(part, w, m, v)


def _pad_row(a):
    a = a.reshape(1, -1)
    return jnp.pad(a, ((0, 0), (0, D_MODEL - a.shape[1])))


def kernel(x, ffn1_norm, ffn1_w_gate, ffn1_w_up, ffn1_w_down, mix_norm, w_in, sb_out_norm, dil_out_norm, w_out, ffn2_norm, ffn2_w_gate, ffn2_w_up, ffn2_w_down, final_norm, loss_target, m_ffn1_norm, m_ffn1_w_gate, m_ffn1_w_up, m_ffn1_w_down, m_mix_norm, m_w_in, m_sb_out_norm, m_dil_out_norm, m_w_out, m_ffn2_norm, m_ffn2_w_gate, m_ffn2_w_up, m_ffn2_w_down, m_final_norm, v_ffn1_norm, v_ffn1_w_gate, v_ffn1_w_up, v_ffn1_w_down, v_mix_norm, v_w_in, v_sb_out_norm, v_dil_out_norm, v_w_out, v_ffn2_norm, v_ffn2_w_gate, v_ffn2_w_up, v_ffn2_w_down, v_final_norm):
    x = x[0]
    target = loss_target[0]
    s = x.shape[0]
    gf = final_norm.reshape(1, D_MODEL)
    cos, sin = _rope_tables(s)

    flip = lambda a: a[0].T
    shard = lambda w: w[0].astype(BF16)
    shard_t = lambda w: flip(w).astype(BF16)
    wg1, wu1, wd1 = _all_gather([shard_t(ffn1_w_gate), shard_t(ffn1_w_up), shard(ffn1_w_down)])

    x1, hm, saved1, (win, wout, wd2) = _ffn1_fwd(x, ffn1_norm, mix_norm, (wg1, wu1), wd1,
                                                 [shard(w_in), shard(w_out), shard(ffn2_w_down)])
    wout = wout.reshape(D_MODEL, D_MODEL)
    qkv, (wg2, wu2) = _proj_fwd(hm, win, cos, sin, [shard_t(ffn2_w_gate), shard_t(ffn2_w_up)])
    o_sb = _sb_fwd(qkv)
    o_dl, lse = _dilated_fwd(qkv)
    x2 = _outproj_fwd(o_sb, o_dl, sb_out_norm, dil_out_norm, x1, wout)
    dx3, st_final, saved2 = _ffn2_fwd_loss(x2, ffn2_norm, gf, target, (wg2, wu2), wd2)

    dx2, dwg2, dwu2, dwd2, st_ffn2 = _ffn_bwd(x2, ffn2_norm, dx3, saved2, (wg2, wu2), wd2, 1)
    do_sb, do_dl, delta_dl, dwout, st_out = _outproj_bwd(dx2, o_sb, o_dl, sb_out_norm, dil_out_norm, wout)
    dqkv_sb = _sb_bwd(qkv, o_sb, do_sb)
    dqkv_dl = _dilated_finish(_dilated_bwd(qkv, delta_dl, lse, do_dl), cos, sin)
    dx1, dwin, st_mix = _proj_bwd(x1, mix_norm, dqkv_sb, dqkv_dl, win, dx2)
    grad_x, dwg1, dwu1, dwd1, st_ffn1 = _ffn_bwd(x, ffn1_norm, dx1, saved1, (wg1, wu1), wd1, 0)

    names = ["ffn1_w_gate", "ffn1_w_up", "ffn1_w_down", "w_in", "w_out", "ffn2_w_gate", "ffn2_w_up", "ffn2_w_down"]
    grads = [dwg1, dwu1, dwd1, dwin, dwout.reshape(N_CHIP, OUTB, D_MODEL), dwg2, dwu2, dwd2]
    flipped = {"ffn1_w_gate", "ffn1_w_up", "ffn2_w_gate", "ffn2_w_up"}
    place = lambda n, a: flip(a) if n in flipped else a[0]
    weights = [place(n, a) for n, a in zip(names, [ffn1_w_gate, ffn1_w_up, ffn1_w_down, w_in, w_out, ffn2_w_gate, ffn2_w_up, ffn2_w_down])]
    moms = [place(n, a) for n, a in zip(names, [m_ffn1_w_gate, m_ffn1_w_up, m_ffn1_w_down, m_w_in, m_w_out, m_ffn2_w_gate, m_ffn2_w_up, m_ffn2_w_down])]
    vels = [place(n, a) for n, a in zip(names, [v_ffn1_w_gate, v_ffn1_w_up, v_ffn1_w_down, v_w_in, v_w_out, v_ffn2_w_gate, v_ffn2_w_up, v_ffn2_w_down])]
    mats = {n: tuple((t.T if n in flipped else t)[None] for t in r)
            for n, r in zip(names, _reduce_and_update(grads, weights, moms, vels))}

    vec_names = ["ffn1_norm", "mix_norm", "sb_out_norm", "dil_out_norm", "ffn2_norm", "final_norm"]
    part = jnp.concatenate([st_ffn1[0:1], st_mix[0:1], _pad_row(st_out[0]), _pad_row(st_out[1]), st_ffn2[0:1],
                            st_final[0:1], st_final[1:2], jnp.zeros((1, D_MODEL), F32)], axis=0)
    pack = lambda arrs: jnp.concatenate([_pad_row(a) for a in arrs] + [jnp.zeros((2, D_MODEL), F32)], axis=0)
    g_vec, d_vec, m_vec, v_vec = _reduce_vectors(
        part,
        pack([ffn1_norm, mix_norm, sb_out_norm, dil_out_norm, ffn2_norm, final_norm]),
        pack([m_ffn1_norm, m_mix_norm, m_sb_out_norm, m_dil_out_norm, m_ffn2_norm, m_final_norm]),
        pack([v_ffn1_norm, v_mix_norm, v_sb_out_norm, v_dil_out_norm, v_ffn2_norm, v_final_norm]))
    like = {"ffn1_norm": ffn1_norm, "mix_norm": mix_norm, "sb_out_norm": sb_out_norm, "dil_out_norm": dil_out_norm,
            "ffn2_norm": ffn2_norm, "final_norm": final_norm}
    vecs = {n: tuple(t[i, :like[n].size].reshape(like[n].shape) for t in (g_vec, d_vec, m_vec, v_vec))
            for i, n in enumerate(vec_names)}
    loss = 0.5 * jnp.sum(g_vec[6]) / D_MODEL

    order = ["ffn1_norm", "ffn1_w_gate", "ffn1_w_up", "ffn1_w_down", "mix_norm", "w_in", "sb_out_norm", "dil_out_norm",
             "w_out", "ffn2_norm", "ffn2_w_gate", "ffn2_w_up", "ffn2_w_down", "final_norm"]
    both = {**mats, **vecs}
    return (loss, grad_x[None], *[both[n][0] for n in order], *[both[n][1] for n in order],
            *[both[n][2] for n in order], *[both[n][3] for n in order])
```

```python
import functools

import jax
import jax.numpy as jnp
from jax import lax
from jax.experimental import pallas as pl
from jax.experimental.pallas import tpu as pltpu

D_MODEL = 1024
D_FF = 2816
HEAD_DIM = 64
D_SB = 512
D_DIL = 512
D_IN = 3072
N_CHIP = 4
FFB = D_FF // N_CHIP
INB = D_IN // N_CHIP
OUTB = D_MODEL // N_CHIP
BLK = 128
LANES = 128
DILATIONS = (1, 4, 16)
ROPE_THETA = 10000.0
RMS_EPS = 1e-6
SCALE = HEAD_DIM ** -0.5
NEG = -1e30
DEAD = -104.0
ADAM_LR = 0.001
ADAM_B1 = 0.9
ADAM_B2 = 0.999
ADAM_EPS = 1e-08
ADAM_WD = 0.01
ADAM_STEP = 10
MESH = pl.DeviceIdType.MESH
F32 = jnp.float32
BF16 = jnp.bfloat16
TM = 512


def _params(vmem_mb):
    return pltpu.CompilerParams(vmem_limit_bytes=vmem_mb << 20)


def _dot(a, b):
    return jnp.dot(a, b, preferred_element_type=F32)


def _dot_nt(a, b):
    return lax.dot_general(a, b, (((1,), (1,)), ((), ())), preferred_element_type=F32)


def _dot_tn(a, b):
    return lax.dot_general(a, b, (((0,), (0,)), ((), ())), preferred_element_type=F32)


def _rms_fwd(x, g):
    r = lax.rsqrt(jnp.mean(x * x, axis=-1, keepdims=True) + RMS_EPS)
    xh = x * r
    return xh * g, xh, r


def _rms_bwd(dy, xh, r, g):
    dyg = dy * g
    dx = r * (dyg - xh * jnp.mean(dyg * xh, axis=-1, keepdims=True))
    return dx, jnp.sum(dy * xh, axis=0, keepdims=True)


def _split_bf16(a):
    hi = a.astype(BF16)
    return hi, (a - hi.astype(F32)).astype(BF16)


def _dot_split(a, b2):
    hi, lo = _split_bf16(a)
    return _dot(jnp.concatenate([hi, lo], axis=1), b2)


def _ffn_weight_specs():
    return [pl.BlockSpec((None, FFB, D_MODEL), lambda i, j: (j, 0, 0))] * 3


def _ffn_saved(s):
    hidden = jax.ShapeDtypeStruct((N_CHIP, s, FFB), BF16)
    hid = pl.BlockSpec((None, TM, FFB), lambda i, j: (j, i, 0))
    row = pl.BlockSpec((TM, D_MODEL), lambda i, j: (i, 0))
    return [row, hid, hid, hid], [jax.ShapeDtypeStruct((s, D_MODEL), BF16), hidden, hidden, hidden]


def _ffn_accumulate(h_ref, acc_scr, wg_ref, wu_ref, wd_ref, a_ref, b_ref, act_ref):
    h = h_ref[...]
    a = _dot_nt(h, wg_ref[...])
    b = _dot_nt(h, wu_ref[...])
    act = ((a * jax.nn.sigmoid(a)) * b).astype(BF16)
    a_ref[...] = a.astype(BF16)
    b_ref[...] = b.astype(BF16)
    act_ref[...] = act
    acc_scr[...] += _dot(act, wd_ref[...])


def _host_gather_before(gather, i, j, steps):
    @pl.when((i == 0) & (j == 0))
    def _():
        gather.start()

    @pl.when((i == (3 * steps) // 4) & (j == 0))
    def _():
        gather.forward()


def _host_gather_after(gather, i, j, steps):
    @pl.when((i == steps - 1) & (j == N_CHIP - 1))
    def _():
        gather.finish()


def _ffn1_fwd(x, g1, gmix, gu, wd, later_shards):
    s = x.shape[0]
    row = pl.BlockSpec((TM, D_MODEL), lambda i, j: (i, 0))
    vec = pl.BlockSpec((1, D_MODEL), lambda i, j: (0, 0))
    saved_specs, saved_shapes = _ffn_saved(s)
    n = len(later_shards)
    any_spec = pl.BlockSpec(memory_space=pl.ANY)

    def body(*refs):
        x_ref, g_ref, gm_ref, wg_ref, wu_ref, wd_ref = refs[:6]
        shard_refs, refs = refs[6:6 + n], refs[6 + n:]
        x1_ref, hm_ref, h_ref, a_ref, b_ref, act_ref = refs[:6]
        gathered_refs, acc_scr, gather_scratch = refs[6:6 + n], refs[6 + n], refs[7 + n:]
        gather = _BackgroundGather(shard_refs, gathered_refs, gather_scratch)
        i, j = pl.program_id(0), pl.program_id(1)
        _host_gather_before(gather, i, j, s // TM)

        @pl.when(j == 0)
        def _():
            h, _, _ = _rms_fwd(x_ref[...], g_ref[...])
            h_ref[...] = h.astype(BF16)
            acc_scr[...] = jnp.zeros_like(acc_scr)

        _ffn_accumulate(h_ref, acc_scr, wg_ref, wu_ref, wd_ref, a_ref, b_ref, act_ref)

        @pl.when(j == N_CHIP - 1)
        def _():
            x1 = x_ref[...] + 0.5 * acc_scr[...]
            x1_ref[...] = x1
            hm, _, _ = _rms_fwd(x1, gm_ref[...])
            hm_ref[...] = hm.astype(BF16)

        _host_gather_after(gather, i, j, s // TM)

    x1, hm, h, a, b, act, *gathered = pl.pallas_call(
        body, name="ffn1_fwd", grid=(s // TM, N_CHIP),
        in_specs=[row, vec, vec] + _ffn_weight_specs() + [any_spec] * n,
        out_specs=[row, row] + saved_specs + [any_spec] * n,
        out_shape=([jax.ShapeDtypeStruct((s, D_MODEL), F32), jax.ShapeDtypeStruct((s, D_MODEL), BF16)] + saved_shapes
                   + _BackgroundGather.out_shapes(later_shards)),
        scratch_shapes=[pltpu.VMEM((TM, D_MODEL), F32)] + _BackgroundGather.scratch_shapes(later_shards),
        compiler_params=_params(58),
    )(x, g1, gmix, gu[0], gu[1], wd, *later_shards)
    return x1, hm, [h, a, b, act], gathered


def _ffn2_fwd_loss(x2, g2, gf, target, gu, wd):
    s = x2.shape[0]
    row = pl.BlockSpec((TM, D_MODEL), lambda i, j: (i, 0))
    vec = pl.BlockSpec((1, D_MODEL), lambda i, j: (0, 0))
    stat = pl.BlockSpec((8, D_MODEL), lambda i, j: (0, 0))
    saved_specs, saved_shapes = _ffn_saved(s)

    def body(x_ref, g_ref, gf_ref, t_ref, wg_ref, wu_ref, wd_ref, dx_ref, st_ref, h_ref, a_ref, b_ref, act_ref, acc_scr):
        i, j = pl.program_id(0), pl.program_id(1)

        @pl.when((i == 0) & (j == 0))
        def _():
            st_ref[...] = jnp.zeros_like(st_ref)

        @pl.when(j == 0)
        def _():
            h, _, _ = _rms_fwd(x_ref[...], g_ref[...])
            h_ref[...] = h.astype(BF16)
            acc_scr[...] = jnp.zeros_like(acc_scr)

        _ffn_accumulate(h_ref, acc_scr, wg_ref, wu_ref, wd_ref, a_ref, b_ref, act_ref)

        @pl.when(j == N_CHIP - 1)
        def _():
            x3 = x_ref[...] + 0.5 * acc_scr[...]
            y, xh, r = _rms_fwd(x3, gf_ref[...])
            err = y - t_ref[...]
            dx, dg = _rms_bwd(err * (1.0 / D_MODEL), xh, r, gf_ref[...])
            dx_ref[...] = dx
            st_ref[0:1, :] += dg
            st_ref[1:2, :] += jnp.sum(err * err, axis=0, keepdims=True)

    dx3, st, *saved = pl.pallas_call(
        body, name="ffn2_fwd_loss", grid=(s // TM, N_CHIP),
        in_specs=[row, vec, vec, row] + _ffn_weight_specs(),
        out_specs=[row, stat] + saved_specs,
        out_shape=[jax.ShapeDtypeStruct((s, D_MODEL), F32), jax.ShapeDtypeStruct((8, D_MODEL), F32)] + saved_shapes,
        scratch_shapes=[pltpu.VMEM((TM, D_MODEL), F32)],
        compiler_params=_params(56),
    )(x2, g2, gf, target, gu[0], gu[1], wd)
    return dx3, st, saved


def _ffn_bwd(xin, g, dy, saved, gu, wd, f):
    s = xin.shape[0]
    hb, gate, up, act = saved
    row = pl.BlockSpec((TM, D_MODEL), lambda i, j: (i, 0))
    vec = pl.BlockSpec((1, D_MODEL), lambda i, j: (0, 0))
    stat = pl.BlockSpec((8, D_MODEL), lambda i, j: (0, 0))
    hid = pl.BlockSpec((None, TM, FFB), lambda i, j: (j, i, 0))

    def body(x_ref, g_ref, dy_ref, a_ref, b_ref, wg_ref, wu_ref, wd_ref, out_ref, dyh_ref, da_ref, db_ref, st_ref, dh_scr):
        i, j = pl.program_id(0), pl.program_id(1)

        @pl.when((i == 0) & (j == 0))
        def _():
            st_ref[...] = jnp.zeros_like(st_ref)

        @pl.when(j == 0)
        def _():
            dyh_ref[...] = (0.5 * dy_ref[...]).astype(BF16)
            dh_scr[...] = jnp.zeros_like(dh_scr)

        a = a_ref[...].astype(F32)
        b = b_ref[...].astype(F32)
        sg = jax.nn.sigmoid(a)
        dact = _dot_nt(dyh_ref[...], wd_ref[...])
        dab = (dact * b * (sg * (1.0 + a * (1.0 - sg)))).astype(BF16)
        dbb = (dact * (a * sg)).astype(BF16)
        da_ref[...] = dab
        db_ref[...] = dbb
        dh_scr[...] += _dot(dab, wg_ref[...]) + _dot(dbb, wu_ref[...])

        @pl.when(j == N_CHIP - 1)
        def _():
            _, xh, r = _rms_fwd(x_ref[...], g_ref[...])
            dx, dg = _rms_bwd(dh_scr[...], xh, r, g_ref[...])
            out_ref[...] = dy_ref[...] + dx
            st_ref[0:1, :] += dg

    hidden = jax.ShapeDtypeStruct((N_CHIP, s, FFB), BF16)
    dx, dyh, da, db, st = pl.pallas_call(
        body, name=f"ffn{f + 1}_bwd_dx", grid=(s // TM, N_CHIP),
        in_specs=[row, vec, row, hid, hid] + _ffn_weight_specs(),
        out_specs=[row, row, hid, hid, stat],
        out_shape=[jax.ShapeDtypeStruct((s, D_MODEL), F32), jax.ShapeDtypeStruct((s, D_MODEL), BF16),
                   hidden, hidden, jax.ShapeDtypeStruct((8, D_MODEL), F32)],
        scratch_shapes=[pltpu.VMEM((TM, D_MODEL), F32)],
        compiler_params=_params(56),
    )(xin, g, dy, gate, up, gu[0], gu[1], wd)

    tk = 4 * TM
    tok = pl.BlockSpec((tk, D_MODEL), lambda j, i: (i, 0))
    hid2 = pl.BlockSpec((None, tk, FFB), lambda j, i: (j, i, 0))
    gspecs = [pl.BlockSpec((None, FFB, D_MODEL), lambda j, i: (j, 0, 0))] * 3

    def wbody(h_ref, dyh_ref, da_ref, db_ref, act_ref, dwg_ref, dwu_ref, dwd_ref):
        @pl.when(pl.program_id(1) == 0)
        def _():
            dwg_ref[...] = jnp.zeros_like(dwg_ref)
            dwu_ref[...] = jnp.zeros_like(dwu_ref)
            dwd_ref[...] = jnp.zeros_like(dwd_ref)

        hb = h_ref[...]
        dwg_ref[...] += _dot_tn(da_ref[...], hb)
        dwu_ref[...] += _dot_tn(db_ref[...], hb)
        dwd_ref[...] += _dot_tn(act_ref[...], dyh_ref[...])

    dwg, dwu, dwd = pl.pallas_call(
        wbody, name=f"ffn{f + 1}_bwd_dw", grid=(N_CHIP, s // tk),
        in_specs=[tok, tok, hid2, hid2, hid2], out_specs=gspecs,
        out_shape=[jax.ShapeDtypeStruct((N_CHIP, FFB, D_MODEL), F32)] * 3,
        compiler_params=_params(60),
    )(hb, dyh, da, db, act)
    return dx, dwg, dwu, dwd, st


def _rope_tables(s):
    half = HEAD_DIM // 2
    inv_freq = ROPE_THETA ** (-jnp.arange(half, dtype=F32) / half)
    ang = jnp.arange(s).astype(F32)[:, None] * inv_freq[None, :]
    cos, sin = jnp.cos(ang), jnp.sin(ang)
    cos2 = jnp.concatenate([cos, cos], axis=-1)
    sin2 = jnp.concatenate([-sin, sin], axis=-1)
    return jnp.tile(cos2, (1, LANES // HEAD_DIM)), jnp.tile(sin2, (1, LANES // HEAD_DIM))


def _rotate(t, cos, sin_signed):
    lane = lax.broadcasted_iota(jnp.int32, t.shape, 1)
    first = (lane % HEAD_DIM) < (HEAD_DIM // 2)
    partner = jnp.where(first, pltpu.roll(t, LANES - HEAD_DIM // 2, 1), pltpu.roll(t, HEAD_DIM // 2, 1))
    return t * cos + partner * sin_signed


def _proj_fwd(hm, win, cos, sin, later_shards):
    s = hm.shape[0]
    tm = 2 * TM
    n_sub = INB // LANES
    first_rot, last_rot = (3 * D_SB) // LANES, (3 * D_SB + 2 * D_DIL) // LANES
    n = len(later_shards)
    any_spec = pl.BlockSpec(memory_space=pl.ANY)

    def body(*refs):
        h_ref, w_ref, c_ref, s_ref = refs[:4]
        shard_refs, o_ref, gathered_refs, gather_scratch = refs[4:4 + n], refs[4 + n], refs[5 + n:5 + 2 * n], refs[5 + 2 * n:]
        gather = _BackgroundGather(shard_refs, gathered_refs, gather_scratch)
        i, j = pl.program_id(0), pl.program_id(1)
        _host_gather_before(gather, i, j, s // tm)
        r = _dot(h_ref[...], w_ref[...])
        for c in range(n_sub):
            t = r[:, c * LANES:(c + 1) * LANES]
            col = j * n_sub + c
            rot = (col >= first_rot) & (col < last_rot)
            lanes = slice(c * LANES, (c + 1) * LANES)

            @pl.when(rot)
            def _():
                o_ref[:, lanes] = _rotate(t, c_ref[...], s_ref[...]).astype(BF16)

            @pl.when(jnp.logical_not(rot))
            def _():
                o_ref[:, lanes] = t.astype(BF16)

        _host_gather_after(gather, i, j, s // tm)

    qkv, *gathered = pl.pallas_call(
        body, name="proj_fwd", grid=(s // tm, N_CHIP),
        in_specs=[pl.BlockSpec((tm, D_MODEL), lambda i, j: (i, 0)),
                  pl.BlockSpec((None, D_MODEL, INB), lambda i, j: (j, 0, 0)),
                  pl.BlockSpec((tm, LANES), lambda i, j: (i, 0)),
                  pl.BlockSpec((tm, LANES), lambda i, j: (i, 0))] + [any_spec] * n,
        out_specs=[pl.BlockSpec((tm, INB), lambda i, j: (i, j))] + [any_spec] * n,
        out_shape=[jax.ShapeDtypeStruct((s, D_IN), BF16)] + _BackgroundGather.out_shapes(later_shards),
        scratch_shapes=_BackgroundGather.scratch_shapes(later_shards),
        compiler_params=_params(48),
    )(hm, win, cos, sin, *later_shards)
    return qkv, gathered


def _proj_bwd(x1, gmix, dqkv_sb, dqkv_dl, win, dx2):
    s = x1.shape[0]
    row = pl.BlockSpec((TM, D_MODEL), lambda i, j: (i, 0))
    vec = pl.BlockSpec((1, D_MODEL), lambda i, j: (0, 0))
    per_group = N_CHIP // 2

    def body(x_ref, g_ref, dsb_ref, ddl_ref, w_ref, dx2_ref, out_ref, dw_ref, st_ref, h_scr, dh_scr, dq_ref):
        i, j = pl.program_id(0), pl.program_id(1)

        @pl.when(j < per_group)
        def _():
            dq_ref[...] = dsb_ref[...]

        @pl.when(j >= per_group)
        def _():
            dq_ref[...] = ddl_ref[...]

        @pl.when((i == 0) & (j == 0))
        def _():
            st_ref[...] = jnp.zeros_like(st_ref)
            dw_ref[...] = jnp.zeros_like(dw_ref)

        @pl.when(j == 0)
        def _():
            h, _, _ = _rms_fwd(x_ref[...], g_ref[...])
            h_scr[...] = h.astype(BF16)
            dh_scr[...] = jnp.zeros_like(dh_scr)

        dq = dq_ref[...]
        dw_ref[j] += _dot_tn(h_scr[...], dq)
        dh_scr[...] += _dot_nt(dq, w_ref[...])

        @pl.when(j == N_CHIP - 1)
        def _():
            _, xh, r = _rms_fwd(x_ref[...], g_ref[...])
            dx, dg = _rms_bwd(dh_scr[...], xh, r, g_ref[...])
            out_ref[...] = dx2_ref[...] + dx
            st_ref[0:1, :] += dg

    return pl.pallas_call(
        body, name="proj_bwd", grid=(s // TM, N_CHIP),
        in_specs=[row, vec,
                  pl.BlockSpec((TM, INB), lambda i, j: (i, jnp.minimum(j, per_group - 1))),
                  pl.BlockSpec((TM, INB), lambda i, j: (i, jnp.maximum(j - per_group, 0))),
                  pl.BlockSpec((None, D_MODEL, INB), lambda i, j: (j, 0, 0)), row],
        out_specs=[row, pl.BlockSpec((N_CHIP, D_MODEL, INB), lambda i, j: (0, 0, 0)),
                   pl.BlockSpec((8, D_MODEL), lambda i, j: (0, 0))],
        out_shape=[jax.ShapeDtypeStruct((s, D_MODEL), F32),
                   jax.ShapeDtypeStruct((N_CHIP, D_MODEL, INB), F32),
                   jax.ShapeDtypeStruct((8, D_MODEL), F32)],
        scratch_shapes=[pltpu.VMEM((TM, D_MODEL), BF16), pltpu.VMEM((TM, D_MODEL), F32), pltpu.VMEM((TM, INB), BF16)],
        compiler_params=_params(56),
    )(x1, gmix, dqkv_sb, dqkv_dl, win, dx2)


def _outproj_fwd(o_sb, o_dl, g_sb, g_dl, x1, wout):
    s = x1.shape[0]
    tm = 2 * TM
    half = pl.BlockSpec((tm, D_SB), lambda i: (i, 0))
    row = pl.BlockSpec((tm, D_MODEL), lambda i: (i, 0))
    vec = pl.BlockSpec((1, D_SB), lambda i: (0, 0))

    def body(a_ref, b_ref, ga_ref, gb_ref, x_ref, w_ref, o_ref):
        ma, _, _ = _rms_fwd(a_ref[...], ga_ref[...])
        mb, _, _ = _rms_fwd(b_ref[...], gb_ref[...])
        o_ref[...] = (x_ref[...] + _dot(ma.astype(BF16), w_ref[0:D_SB, :])
                      + _dot(mb.astype(BF16), w_ref[D_SB:D_MODEL, :]))

    return pl.pallas_call(
        body, name="outproj_fwd", grid=(s // tm,),
        in_specs=[half, half, vec, vec, row, pl.BlockSpec((D_MODEL, D_MODEL), lambda i: (0, 0))],
        out_specs=row, out_shape=jax.ShapeDtypeStruct((s, D_MODEL), F32),
        compiler_params=_params(32),
    )(o_sb, o_dl, g_sb, g_dl, x1, wout)


def _outproj_bwd(dx2, o_sb, o_dl, g_sb, g_dl, wout):
    s = dx2.shape[0]
    tm = 2 * TM
    half = pl.BlockSpec((tm, D_SB), lambda i: (i, 0))
    row = pl.BlockSpec((tm, D_MODEL), lambda i: (i, 0))
    vec = pl.BlockSpec((1, D_SB), lambda i: (0, 0))
    full = pl.BlockSpec((D_MODEL, D_MODEL), lambda i: (0, 0))

    def body(dy_ref, a_ref, b_ref, ga_ref, gb_ref, w_ref, da_ref, db_ref, dl_ref, dw_ref, st_ref):
        @pl.when(pl.program_id(0) == 0)
        def _():
            dw_ref[...] = jnp.zeros_like(dw_ref)
            st_ref[...] = jnp.zeros_like(st_ref)

        dy = dy_ref[...].astype(BF16)
        dm = _dot_nt(dy, w_ref[...])
        ma, xa, ra = _rms_fwd(a_ref[...], ga_ref[...])
        mb, xb, rb = _rms_fwd(b_ref[...], gb_ref[...])
        dw_ref[0:D_SB, :] += _dot_tn(ma.astype(BF16), dy)
        dw_ref[D_SB:D_MODEL, :] += _dot_tn(mb.astype(BF16), dy)
        da, dga = _rms_bwd(dm[:, 0:D_SB], xa, ra, ga_ref[...])
        db, dgb = _rms_bwd(dm[:, D_SB:D_MODEL], xb, rb, gb_ref[...])
        da_ref[...] = da
        db_ref[...] = db
        r = lax.broadcasted_iota(jnp.int32, (LANES, LANES), 0) >= HEAD_DIM
        c = lax.broadcasted_iota(jnp.int32, (LANES, LANES), 1) >= HEAD_DIM
        same_head = jnp.where(r == c, 1.0, 0.0).astype(BF16)
        same_head = jnp.concatenate([same_head, same_head], axis=0)
        prod = db * b_ref[...]
        for k in range(D_DIL // LANES):
            lanes = slice(k * LANES, (k + 1) * LANES)
            dl_ref[:, lanes] = _dot_split(prod[:, lanes], same_head)
        st_ref[0:1, :] += dga
        st_ref[1:2, :] += dgb

    return pl.pallas_call(
        body, name="outproj_bwd", grid=(s // tm,),
        in_specs=[row, half, half, vec, vec, full],
        out_specs=[half, half, half, full, pl.BlockSpec((8, D_SB), lambda i: (0, 0))],
        out_shape=[jax.ShapeDtypeStruct((s, D_SB), F32), jax.ShapeDtypeStruct((s, D_SB), F32),
                   jax.ShapeDtypeStruct((s, D_DIL), F32),
                   jax.ShapeDtypeStruct((D_MODEL, D_MODEL), F32), jax.ShapeDtypeStruct((8, D_SB), F32)],
        compiler_params=_params(48),
    )(dx2, o_sb, o_dl, g_sb, g_dl, wout)


def _head_masks():
    lane = lax.broadcasted_iota(jnp.int32, (BLK, LANES), 1)
    return [lane < HEAD_DIM, lane >= HEAD_DIM]


def _keep(mask, a):
    return a * jnp.where(mask, 1.0, 0.0).astype(a.dtype)


def _suffix_matrices():
    r = lax.broadcasted_iota(jnp.int32, (2 * BLK, BLK), 0) & (BLK - 1)
    c = lax.broadcasted_iota(jnp.int32, (2 * BLK, BLK), 1)
    ones = jnp.ones((2 * BLK, BLK), BF16)
    excl = jnp.concatenate([(r > c).astype(BF16), ones], axis=1)
    incl = jnp.concatenate([(r >= c).astype(BF16), ones], axis=1)
    return excl, incl


def _blk(i):
    return pl.ds(pl.multiple_of(i * BLK, BLK), BLK)


def _alive(carry_m):
    return (jnp.max(carry_m) > DEAD).astype(jnp.int32)


def _more_keys(last, carry):
    return (carry[0] * SB_KB <= last) & (carry[1] > 0)


def _stack_heads(a):
    masks = _head_masks()
    return jnp.concatenate([_keep(masks[0], a), _keep(masks[1], a)], axis=0)


def _unstack_heads(a2):
    return jnp.where(_head_masks()[0], a2[:BLK], a2[BLK:])


def _head_rowsum(a):
    masks = _head_masks()
    return jnp.concatenate([jnp.sum(jnp.where(m, a, 0.0), axis=1, keepdims=True) for m in masks], axis=0)


SB_QB = 2
SB_ROWS = SB_QB * 2 * BLK
SB_KB = 4
PAST_START = 1 << 30


def _sb_rows(ref, i0, cast=None):
    tiles = [ref[_blk(i0 + t), :] for t in range(SB_QB)]
    return jnp.concatenate([_stack_heads(t if cast is None else t.astype(cast)) for t in tiles], axis=0)


_SB_LATER_ROWS = (SB_QB - 1) * 2 * BLK


def _put_rows(full, rows, part):
    return part if rows.start == 0 else jnp.concatenate([full[:rows.start], part], axis=0)


def _sb_scores(q2, k, i, j, carry_m, u_excl):
    r = lax.broadcasted_iota(jnp.int32, (q2.shape[0], BLK), 0)
    row = (r & (BLK - 1)) + ((r >> 8) << 7)
    col = lax.broadcasted_iota(jnp.int32, (q2.shape[0], BLK), 1)
    valid = (jnp.where(j >= 0, j * BLK, PAST_START) + col) < (i * BLK + row)
    z = _dot_nt(q2, k) * SCALE
    sp = jnp.maximum(z, 0.0) + jnp.log(1.0 + jnp.exp(-jnp.abs(z)))
    log_stay = jnp.where(valid, -sp, 0.0)
    log_beta = z - sp
    sums = _dot_split(log_stay, u_excl)
    later = carry_m + sums[:, :BLK]
    w = jnp.where(valid, jnp.exp(log_beta + later), 0.0)
    return valid, log_beta, w, carry_m + sums[:, BLK:]


def _sb_fwd(qkv):
    s = qkv.shape[0]
    nq = s // BLK
    pairs = D_SB // LANES
    col = lambda off: pl.BlockSpec((s, LANES), lambda p: (0, off + p))

    def body(q_ref, k_ref, v_ref, o_ref):
        u_excl, _ = _suffix_matrices()
        zero = jnp.zeros((SB_ROWS, LANES), F32)

        def q_block(ib, _):
            i = ib * SB_QB
            last = i + SB_QB - 1
            q2 = _sb_rows(q_ref, i)

            def trip(jj, carry_m, acc, first):
                for t in range(SB_KB):
                    j = last - jj * SB_KB - t
                    at = _blk(jnp.maximum(j, 0))
                    rows = slice(_SB_LATER_ROWS, SB_ROWS) if first and t == 0 else slice(0, SB_ROWS)
                    base = i + rows.start // (2 * BLK)
                    _, _, w, part = _sb_scores(q2[rows], k_ref[at, :], base, j, carry_m[rows], u_excl)
                    carry_m = _put_rows(carry_m, rows, part)
                    acc = _put_rows(acc, rows, acc[rows] + _dot(w.astype(BF16), v_ref[at, :]))
                return carry_m, acc

            def k_block(carry):
                carry_m, acc = trip(carry[0], carry[2], carry[3], False)
                return carry[0] + 1, _alive(carry_m), carry_m, acc

            carry_m, acc = trip(0, zero, zero, True)
            _, _, _, acc = lax.while_loop(functools.partial(_more_keys, last), k_block,
                                          (jnp.int32(1), _alive(carry_m), carry_m, acc))
            for t in range(SB_QB):
                o_ref[_blk(i + t), :] = _unstack_heads(acc[2 * BLK * t:2 * BLK * (t + 1)])
            return 0

        lax.fori_loop(0, nq // SB_QB, q_block, 0)

    return pl.pallas_call(
        body, name="sb_fwd", grid=(pairs,),
        in_specs=[col(0), col(pairs), col(2 * pairs)],
        out_specs=pl.BlockSpec((s, LANES), lambda p: (0, p)),
        out_shape=jax.ShapeDtypeStruct((s, D_SB), F32),
        compiler_params=_params(48),
    )(qkv, qkv, qkv)


def _sb_bwd(qkv, o_sb, do_sb):
    s = qkv.shape[0]
    nq = s // BLK
    pairs = D_SB // LANES
    col = lambda off: pl.BlockSpec((s, LANES), lambda p, w: (0, off + p))
    own = pl.BlockSpec((s, LANES), lambda p, w: (0, p))

    def body(q_ref, k_ref, v_ref, o_ref, do_ref, out_ref, dq_acc, dk_acc, dv_acc):
        which = pl.program_id(1)

        @pl.when(which == 0)
        def _():
            walk(q_ref, k_ref, v_ref, o_ref, do_ref, dq_acc, dk_acc, dv_acc)
            out_ref[...] = dq_acc[...]

        @pl.when(which == 1)
        def _():
            out_ref[...] = dk_acc[...].astype(BF16)

        @pl.when(which == 2)
        def _():
            out_ref[...] = dv_acc[...].astype(BF16)

    def walk(q_ref, k_ref, v_ref, o_ref, do_ref, dq_ref, dk_acc, dv_acc):
        u_excl, u_incl = _suffix_matrices()
        zero = jnp.zeros((SB_ROWS, LANES), F32)
        dk_acc[...] = jnp.zeros_like(dk_acc)
        dv_acc[...] = jnp.zeros_like(dv_acc)

        def q_block(ib, _):
            i = ib * SB_QB
            last = i + SB_QB - 1
            q2 = _sb_rows(q_ref, i)
            do2 = _sb_rows(do_ref, i, BF16)
            totals = [_head_rowsum(do_ref[_blk(i + t), :].astype(BF16).astype(F32) * o_ref[_blk(i + t), :])
                      for t in range(SB_QB)]
            total = jnp.broadcast_to(jnp.concatenate(totals, axis=0), (SB_ROWS, BLK))

            def trip(jj, carry_m, carry_g, dq, first):
                for t in range(SB_KB):
                    j = last - jj * SB_KB - t
                    at = _blk(jnp.maximum(j, 0))
                    k = k_ref[at, :]
                    rows = slice(_SB_LATER_ROWS, SB_ROWS) if first and t == 0 else slice(0, SB_ROWS)
                    base = i + rows.start // (2 * BLK)
                    valid, log_beta, w, part_m = _sb_scores(q2[rows], k, base, j, carry_m[rows], u_excl)
                    wb = w.astype(BF16)
                    g = wb.astype(F32) * _dot_nt(do2[rows], v_ref[at, :])
                    sums = _dot_split(g, u_incl)
                    before = total[rows] - (carry_g[rows] + sums[:, :BLK])
                    dz = jnp.where(valid, g - jnp.exp(log_beta) * (g + before), 0.0)
                    dzb = (dz * SCALE).astype(BF16)
                    dk_acc[at, :] += _dot_tn(dzb, q2[rows])
                    dv_acc[at, :] += _dot_tn(wb, do2[rows])
                    carry_m = _put_rows(carry_m, rows, part_m)
                    carry_g = _put_rows(carry_g, rows, carry_g[rows] + sums[:, BLK:])
                    dq = _put_rows(dq, rows, dq[rows] + _dot(dzb, k))
                return carry_m, carry_g, dq

            def k_block(carry):
                carry_m, carry_g, dq = trip(carry[0], carry[2], carry[3], carry[4], False)
                return carry[0] + 1, _alive(carry_m), carry_m, carry_g, dq

            carry_m, carry_g, dq = trip(0, zero, zero, zero, True)
            _, _, _, _, dq = lax.while_loop(functools.partial(_more_keys, last), k_block,
                                            (jnp.int32(1), _alive(carry_m), carry_m, carry_g, dq))
            for t in range(SB_QB):
                dq_ref[_blk(i + t), :] = _unstack_heads(dq[2 * BLK * t:2 * BLK * (t + 1)]).astype(BF16)
            return 0

        lax.fori_loop(0, nq // SB_QB, q_block, 0)

    return pl.pallas_call(
        body, name="sb_bwd", grid=(pairs, 3),
        in_specs=[col(0), col(pairs), col(2 * pairs), own, own],
        out_specs=pl.BlockSpec((s, LANES), lambda p, w: (0, w * pairs + p)),
        out_shape=jax.ShapeDtypeStruct((s, 3 * D_SB), BF16),
        scratch_shapes=[pltpu.VMEM((s, LANES), BF16), pltpu.VMEM((s, LANES), F32), pltpu.VMEM((s, LANES), F32)],
        compiler_params=_params(58),
    )(qkv, qkv, qkv, o_sb, do_sb)


DIL_UNROLL = 16


def _band_masks(b):
    row = lax.broadcasted_iota(jnp.int32, (2 * BLK, BLK), 0) & (BLK - 1)
    col = lax.broadcasted_iota(jnp.int32, (2 * BLK, BLK), 1)
    return col <= row, (col - row) >= jnp.where(b > 0, 0, BLK)


def _dil_tiles(qf, kf, vf, d, t, nb):
    c, b = t // nb, t % nb
    start = c + d * BLK * b
    rows = pl.ds(start, BLK, stride=d)
    prev = pl.ds(jnp.where(b > 0, start - d * BLK, start), BLK, stride=d)
    bf = lambda ref, sl: ref[sl, :].astype(BF16)
    return b, rows, prev, _stack_heads(bf(qf, rows)), bf(kf, rows), bf(kf, prev), bf(vf, rows), bf(vf, prev)


def _lanes_of_heads(col2):
    return _unstack_heads(jnp.broadcast_to(col2, (2 * BLK, LANES)))


def _dilated_fwd(qkv):
    s = qkv.shape[0]
    pairs = D_DIL // LANES
    base = (3 * D_SB) // LANES
    col = lambda off: pl.BlockSpec((s, LANES), lambda p: (0, off + p))
    own = pl.BlockSpec((s, LANES), lambda p: (0, p))

    def body(q_ref, k_ref, v_ref, acc_ref, m_ref, qf, kf, vf, l_scr):
        qf[...] = q_ref[...].astype(F32)
        kf[...] = k_ref[...].astype(F32)
        vf[...] = v_ref[...].astype(F32)
        for d in DILATIONS:
            nb = s // (d * BLK)

            def block(t, _):
                b, rows, prev, q2, kc, kp, vc, vp = _dil_tiles(qf, kf, vf, d, t, nb)
                in_cur, in_prev = _band_masks(b)
                zc = jnp.where(in_cur, _dot_nt(q2, kc) * SCALE, NEG)
                zp = jnp.where(in_prev, _dot_nt(q2, kp) * SCALE, NEG)
                m = jnp.maximum(jnp.max(zc, axis=1, keepdims=True), jnp.max(zp, axis=1, keepdims=True))
                pc, pp = jnp.exp(zc - m), jnp.exp(zp - m)
                den = jnp.sum(pc, axis=1, keepdims=True) + jnp.sum(pp, axis=1, keepdims=True)
                acc = _unstack_heads(_dot(pc.astype(BF16), vc) + _dot(pp.astype(BF16), vp))
                m_t, l_t = _lanes_of_heads(m), _lanes_of_heads(den)
                if d == DILATIONS[0]:
                    m_ref[rows, :] = m_t
                    l_scr[rows, :] = l_t
                    acc_ref[rows, :] = acc
                else:
                    m_old = m_ref[rows, :]
                    m_new = jnp.maximum(m_old, m_t)
                    keep, add = jnp.exp(m_old - m_new), jnp.exp(m_t - m_new)
                    m_ref[rows, :] = m_new
                    l_scr[rows, :] = l_scr[rows, :] * keep + l_t * add
                    acc_ref[rows, :] = acc_ref[rows, :] * keep + acc * add
                return 0

            lax.fori_loop(0, s // BLK, block, 0, unroll=DIL_UNROLL)

        def finish(i, _):
            l = l_scr[_blk(i), :]
            acc_ref[_blk(i), :] = acc_ref[_blk(i), :] / l
            m_ref[_blk(i), :] = m_ref[_blk(i), :] + jnp.log(l)
            return 0

        lax.fori_loop(0, s // BLK, finish, 0)

    return pl.pallas_call(
        body, name="dilated_fwd", grid=(pairs,),
        in_specs=[col(base), col(base + pairs), col(base + 2 * pairs)],
        out_specs=[own, own],
        out_shape=[jax.ShapeDtypeStruct((s, D_DIL), F32)] * 2,
        scratch_shapes=[pltpu.VMEM((s, LANES), F32)] * 4,
        compiler_params=_params(56),
    )(qkv, qkv, qkv)


def _stack_lanes(t):
    other = pltpu.roll(t, HEAD_DIM, 1)
    first = _head_masks()[0]
    return jnp.concatenate([jnp.where(first, t, other), jnp.where(first, other, t)], axis=0)


def _dilated_bwd(qkv, delta, lse, dout):
    s = qkv.shape[0]
    pairs = D_DIL // LANES
    base = (3 * D_SB) // LANES
    once = pl.Buffered(1)
    col = lambda off: pl.BlockSpec((s, LANES), lambda p: (0, off + p), pipeline_mode=once)
    own = pl.BlockSpec((s, LANES), lambda p: (0, p), pipeline_mode=once)
    res = pl.BlockSpec((s, LANES), lambda p: (0, p))

    def body(q_ref, k_ref, v_ref, dl_ref, l_ref, do_ref, dq_ref, dk_ref, dv_ref, qf, kf, vf):
        qf[...] = q_ref[...].astype(F32)
        kf[...] = k_ref[...].astype(F32)
        vf[...] = v_ref[...].astype(F32)
        dq_ref[...] = jnp.zeros_like(dq_ref)
        dk_ref[...] = jnp.zeros_like(dk_ref)
        dv_ref[...] = jnp.zeros_like(dv_ref)
        for d in DILATIONS:
            nb = s // (d * BLK)

            def block(t, _):
                b, rows, prev, q2, kc, kp, vc, vp = _dil_tiles(qf, kf, vf, d, t, nb)
                in_cur, in_prev = _band_masks(b)
                do2 = _stack_heads(do_ref[rows, :].astype(BF16))
                delta = _stack_lanes(dl_ref[rows, :])
                lse2 = _stack_lanes(l_ref[rows, :])
                wc = jnp.exp(jnp.where(in_cur, _dot_nt(q2, kc) * SCALE, NEG) - lse2)
                wp = jnp.exp(jnp.where(in_prev, _dot_nt(q2, kp) * SCALE, NEG) - lse2)
                dzc = (wc * (_dot_nt(do2, vc) - delta) * SCALE).astype(BF16)
                dzp = (wp * (_dot_nt(do2, vp) - delta) * SCALE).astype(BF16)
                dq_ref[rows, :] += _unstack_heads(_dot(dzc, kc) + _dot(dzp, kp))
                dk_ref[rows, :] += _dot_tn(dzc, q2)
                dk_ref[prev, :] += _dot_tn(dzp, q2)
                dv_ref[rows, :] += _dot_tn(wc.astype(BF16), do2)
                dv_ref[prev, :] += _dot_tn(wp.astype(BF16), do2)
                return 0

            lax.fori_loop(0, s // BLK, block, 0, unroll=DIL_UNROLL)

    return pl.pallas_call(
        body, name="dilated_bwd", grid=(pairs,),
        in_specs=[col(base), col(base + pairs), col(base + 2 * pairs), own, own, own],
        out_specs=[res, res, res],
        out_shape=[jax.ShapeDtypeStruct((s, D_DIL), F32)] * 3,
        scratch_shapes=[pltpu.VMEM((s, LANES), F32)] * 3,
        compiler_params=_params(60),
    )(qkv, qkv, qkv, delta, lse, dout)


def _dilated_finish(grads, cos, sin):
    s = grads[0].shape[0]
    spec = pl.BlockSpec((TM, D_DIL), lambda i: (i, 0))
    tab = pl.BlockSpec((TM, LANES), lambda i: (i, 0))

    def body(dq_ref, dk_ref, dv_ref, c_ref, s_ref, out_ref):
        for t, (src, rotated) in enumerate(((dq_ref, True), (dk_ref, True), (dv_ref, False))):
            for c in range(D_DIL // LANES):
                piece = src[:, c * LANES:(c + 1) * LANES]
                at = t * D_DIL + c * LANES
                out_ref[:, at:at + LANES] = (_rotate(piece, c_ref[...], -s_ref[...]) if rotated else piece).astype(BF16)

    return pl.pallas_call(
        body, name="dilated_finish", grid=(s // TM,),
        in_specs=[spec] * 3 + [tab, tab], out_specs=pl.BlockSpec((TM, 3 * D_DIL), lambda i: (i, 0)),
        out_shape=jax.ShapeDtypeStruct((s, 3 * D_DIL), BF16),
        compiler_params=_params(32),
    )(*grads, cos, sin)


def _place():
    x, y, c = lax.axis_index("x"), lax.axis_index("y"), lax.axis_index("c")
    return x, y, c, 2 * x + y


def _chip(k, c):
    return (k >> 1, k & 1, c)


def _half(ref, h):
    n = ref.shape[0] // 2
    return ref.at[pl.ds(h * n, n)]


class _BackgroundGather:
    def __init__(self, ins, outs, scratch):
        n = self.n = len(ins)
        self.ins, self.outs = ins, outs
        self.mine, self.landed, self.passed = scratch[0:3 * n:3], scratch[1:3 * n:3], scratch[2:3 * n:3]
        self.send_sem, self.recv_sem, self.local_sem = scratch[3 * n:3 * n + 3]
        x, y, self.c, self.k = _place()
        self.sibling = (x, y, 1 - self.c)

    @staticmethod
    def scratch_shapes(shards):
        shapes = []
        for a in shards:
            half = (N_CHIP - 1, a.shape[0] // 2, a.shape[1])
            shapes += [pltpu.VMEM(a.shape, a.dtype), pltpu.VMEM(half, a.dtype), pltpu.VMEM(half, a.dtype)]
        n = len(shards)
        return shapes + [pltpu.SemaphoreType.DMA((6 * n,)), pltpu.SemaphoreType.DMA((6 * n,)),
                         pltpu.SemaphoreType.DMA((8 * n,))]

    @staticmethod
    def out_shapes(shards):
        return [jax.ShapeDtypeStruct((N_CHIP,) + a.shape, a.dtype) for a in shards]

    def _remote(self, a, slot, src, dst, to):
        return pltpu.make_async_remote_copy(src_ref=src, dst_ref=dst, send_sem=self.send_sem.at[6 * a + slot],
                                            recv_sem=self.recv_sem.at[6 * a + slot], device_id=to, device_id_type=MESH)

    def _local(self, a, slot, src, dst):
        return pltpu.make_async_copy(src, dst, self.local_sem.at[8 * a + slot])

    def _ici(self, a, j):
        return self._remote(a, j - 1, _half(self.mine[a], self.c), self.landed[a].at[j - 1], _chip(self.k ^ j, self.c))

    def _to_sibling(self, a, j):
        return self._remote(a, 2 + j, self.landed[a].at[j - 1], self.passed[a].at[j - 1], self.sibling)

    def _own(self, a):
        return self._local(a, 0, self.ins[a], self.outs[a].at[self.k])

    def _load(self, a):
        return self._local(a, 1, self.ins[a], self.mine[a])

    def _store_landed(self, a, j):
        return self._local(a, 1 + j, self.landed[a].at[j - 1], _half(self.outs[a].at[self.k ^ j], self.c))

    def _store_passed(self, a, j):
        return self._local(a, 4 + j, self.passed[a].at[j - 1], _half(self.outs[a].at[self.k ^ j], 1 - self.c))

    def start(self):
        for a in range(self.n):
            self._own(a).start()
            self._load(a).start()
        for a in range(self.n):
            self._load(a).wait()
            for j in range(1, N_CHIP):
                self._ici(a, j).start()

    def forward(self):
        for j in range(1, N_CHIP):
            for a in range(self.n):
                self._ici(a, j).wait_recv()
                self._to_sibling(a, j).start()
                self._store_landed(a, j).start()

    def finish(self):
        for j in range(1, N_CHIP):
            for a in range(self.n):
                self._to_sibling(a, j).wait_recv()
                self._store_passed(a, j).start()
        for a in range(self.n):
            for j in range(1, N_CHIP):
                self._ici(a, j).wait_send()
                self._to_sibling(a, j).wait_send()
                self._store_landed(a, j).wait()
                self._store_passed(a, j).wait()
            self._own(a).wait()


def _all_gather(shards):
    n = len(shards)
    any_spec = pl.BlockSpec(memory_space=pl.ANY)

    def body(*refs):
        gather = _BackgroundGather(refs[:n], refs[n:2 * n], refs[2 * n:])
        gather.start()
        gather.forward()
        gather.finish()

    return pl.pallas_call(
        body, name="weights_all_gather",
        in_specs=[any_spec] * n, out_specs=[any_spec] * n,
        out_shape=_BackgroundGather.out_shapes(shards),
        scratch_shapes=_BackgroundGather.scratch_shapes(shards),
        compiler_params=_params(32),
    )(*shards)


def _reduce_scatter(g, core, name):
    n, r, c = g.shape
    hr = r // 2
    once = pl.Buffered(1)
    in_specs = [pl.BlockSpec((n, hr, c), lambda i, core_ref: (0, core_ref[0], 0), pipeline_mode=once),
                pl.BlockSpec((n, hr, c), lambda i, core_ref: (0, 1 - core_ref[0], 0), pipeline_mode=once)]

    def body(core_ref, mine_ref, other_ref, out_ref, from_core, sums_bf, from_chips, done, from_core2, send_sem, recv_sem):
        x, y, cc, k = _place()
        sibling = (x, y, 1 - cc)

        def copy(slot, src, dst, to):
            return pltpu.make_async_remote_copy(src_ref=src, dst_ref=dst, send_sem=send_sem.at[slot],
                                                recv_sem=recv_sem.at[slot], device_id=to, device_id_type=MESH)

        from_sibling = [copy(j, other_ref.at[k ^ j], from_core.at[k ^ j], sibling) for j in range(N_CHIP)]
        for j in (1, 2, 3, 0):
            from_sibling[j].start()
        sends = []
        for j in range(1, N_CHIP):
            from_sibling[j].wait()
            sums_bf[j - 1] = (mine_ref[k ^ j] + from_core[k ^ j]).astype(BF16)
            cp = copy(N_CHIP - 1 + j, sums_bf.at[j - 1], from_chips.at[j - 1], _chip(k ^ j, cc))
            cp.start()
            sends.append(cp)
        from_sibling[0].wait()
        red = mine_ref[k] + from_core[k]
        for j in range(1, N_CHIP):
            sends[j - 1].wait()
            red = red + from_chips[j - 1].astype(F32)
        done[...] = red
        last = copy(2 * N_CHIP - 1, done, from_core2, sibling)
        last.start()
        last.wait()
        row0 = pl.multiple_of(cc * hr, 8)
        row1 = pl.multiple_of((1 - cc) * hr, 8)
        out_ref[pl.ds(row0, hr), :] = red
        out_ref[pl.ds(row1, hr), :] = from_core2[...]

    grid_spec = pltpu.PrefetchScalarGridSpec(
        num_scalar_prefetch=1, grid=(1,), in_specs=in_specs,
        out_specs=pl.BlockSpec((r, c), lambda i, core_ref: (0, 0)),
        scratch_shapes=[pltpu.VMEM((n, hr, c), F32), pltpu.VMEM((N_CHIP - 1, hr, c), BF16),
                        pltpu.VMEM((N_CHIP - 1, hr, c), BF16), pltpu.VMEM((hr, c), F32), pltpu.VMEM((hr, c), F32),
                        pltpu.SemaphoreType.DMA((2 * N_CHIP,)), pltpu.SemaphoreType.DMA((2 * N_CHIP,))])
    return pl.pallas_call(
        body, name=name, grid_spec=grid_spec, out_shape=jax.ShapeDtypeStruct((r, c), F32),
        compiler_params=_params(56),
    )(core, g, g)


def _elementwise(fn, name, ins, n_out, rows):
    total, cols = ins[0].shape
    spec = pl.BlockSpec((rows, cols), lambda i: (i, 0))

    def body(*refs):
        res = fn(*[r[...] for r in refs[:len(ins)]])
        for o, v in zip(refs[len(ins):], res):
            o[...] = v

    return pl.pallas_call(
        body, name=name, grid=(total // rows,),
        in_specs=[spec] * len(ins), out_specs=[spec] * n_out,
        out_shape=[jax.ShapeDtypeStruct((total, cols), F32)] * n_out,
        compiler_params=_params(48),
    )(*ins)


def _adamw(w, g, m, v):
    m = ADAM_B1 * m + (1.0 - ADAM_B1) * g
    v = ADAM_B2 * v + (1.0 - ADAM_B2) * (g * g)
    m_hat = m / (1.0 - ADAM_B1 ** ADAM_STEP)
    v_hat = v / (1.0 - ADAM_B2 ** ADAM_STEP)
    delta = -ADAM_LR * (m_hat / (jnp.sqrt(v_hat) + ADAM_EPS) + ADAM_WD * w)
    return delta, m, v


def _reduce_and_update(grads, weights, moms, vels):
    core = lax.axis_index("c").astype(jnp.int32).reshape(1)
    full = [_reduce_scatter(g, core, f"grads_reduce_scatter_{a}") for a, g in enumerate(grads)]
    out = []
    for a, (g, w, m, v) in enumerate(zip(full, weights, moms, vels)):
        rows = g.shape[0] // 2
        out.append((g,) + tuple(_elementwise(lambda gg, ww, mm, vv: _adamw(ww, gg, mm, vv), f"adamw_{a}", [g, w, m, v], 3, rows)))
    return out


def _reduce_vectors(part, w, m, v):
    n_dev = 8

    def body(p_ref, w_ref, m_ref, v_ref, g_ref, d_ref, nm_ref, nv_ref, buf, send_sem, recv_sem):
        x, y, c, _ = _place()
        me = 4 * x + 2 * y + c
        buf[me] = p_ref[...]
        sends = []
        for off in range(1, n_dev):
            peer = me ^ off
            cp = pltpu.make_async_remote_copy(src_ref=p_ref, dst_ref=buf.at[me], send_sem=send_sem.at[off - 1],
                                              recv_sem=recv_sem.at[off - 1], device_id=(peer >> 2, (peer >> 1) & 1, peer & 1),
                                              device_id_type=MESH)
            cp.start()
            sends.append(cp)
        for off in range(1, n_dev):
            peer = me ^ off
            pltpu.make_async_remote_copy(src_ref=p_ref, dst_ref=buf.at[peer], send_sem=send_sem.at[off - 1],
                                         recv_sem=recv_sem.at[off - 1], device_id=(peer >> 2, (peer >> 1) & 1, peer & 1),
                                         device_id_type=MESH).wait_recv()
        for cp in sends:
            cp.wait_send()
        g = buf[0]
        for d in range(1, n_dev):
            g = g + buf[d]
        g_ref[...] = g
        delta, nm, nv = _adamw(w_ref[...], g, m_ref[...], v_ref[...])
        d_ref[...] = delta
        nm_ref[...] = nm
        nv_ref[...] = nv

    vm = pl.BlockSpec(memory_space=pltpu.VMEM)
    return pl.pallas_call(
        body, name="gains_all_reduce",
        in_specs=[vm] * 4, out_specs=[vm] * 4,
        out_shape=[jax.ShapeDtypeStruct(part.shape, F32)] * 4,
        scratch_shapes=[pltpu.VMEM((n_dev,) + part.shape, F32), pltpu.SemaphoreType.DMA((n_dev - 1,)),
                        pltpu.SemaphoreType.DMA((n_dev - 1,))],
    )(part, w, m, v)


def _pad_row(a):
    a = a.reshape(1, -1)
    return jnp.pad(a, ((0, 0), (0, D_MODEL - a.shape[1])))


def kernel(x, ffn1_norm, ffn1_w_gate, ffn1_w_up, ffn1_w_down, mix_norm, w_in, sb_out_norm, dil_out_norm, w_out, ffn2_norm, ffn2_w_gate, ffn2_w_up, ffn2_w_down, final_norm, loss_target, m_ffn1_norm, m_ffn1_w_gate, m_ffn1_w_up, m_ffn1_w_down, m_mix_norm, m_w_in, m_sb_out_norm, m_dil_out_norm, m_w_out, m_ffn2_norm, m_ffn2_w_gate, m_ffn2_w_up, m_ffn2_w_down, m_final_norm, v_ffn1_norm, v_ffn1_w_gate, v_ffn1_w_up, v_ffn1_w_down, v_mix_norm, v_w_in, v_sb_out_norm, v_dil_out_norm, v_w_out, v_ffn2_norm, v_ffn2_w_gate, v_ffn2_w_up, v_ffn2_w_down, v_final_norm):
    x = x[0]
    target = loss_target[0]
    s = x.shape[0]
    gf = final_norm.reshape(1, D_MODEL)
    cos, sin = _rope_tables(s)

    flip = lambda a: a[0].T
    shard = lambda w: w[0].astype(BF16)
    shard_t = lambda w: flip(w).astype(BF16)
    wg1, wu1, wd1 = _all_gather([shard_t(ffn1_w_gate), shard_t(ffn1_w_up), shard(ffn1_w_down)])

    x1, hm, saved1, (win, wout, wd2) = _ffn1_fwd(x, ffn1_norm, mix_norm, (wg1, wu1), wd1,
                                                 [shard(w_in), shard(w_out), shard(ffn2_w_down)])
    wout = wout.reshape(D_MODEL, D_MODEL)
    qkv, (wg2, wu2) = _proj_fwd(hm, win, cos, sin, [shard_t(ffn2_w_gate), shard_t(ffn2_w_up)])
    o_sb = _sb_fwd(qkv)
    o_dl, lse = _dilated_fwd(qkv)
    x2 = _outproj_fwd(o_sb, o_dl, sb_out_norm, dil_out_norm, x1, wout)
    dx3, st_final, saved2 = _ffn2_fwd_loss(x2, ffn2_norm, gf, target, (wg2, wu2), wd2)

    dx2, dwg2, dwu2, dwd2, st_ffn2 = _ffn_bwd(x2, ffn2_norm, dx3, saved2, (wg2, wu2), wd2, 1)
    do_sb, do_dl, delta_dl, dwout, st_out = _outproj_bwd(dx2, o_sb, o_dl, sb_out_norm, dil_out_norm, wout)
    dqkv_sb = _sb_bwd(qkv, o_sb, do_sb)
    dqkv_dl = _dilated_finish(_dilated_bwd(qkv, delta_dl, lse, do_dl), cos, sin)
    dx1, dwin, st_mix = _proj_bwd(x1, mix_norm, dqkv_sb, dqkv_dl, win, dx2)
    grad_x, dwg1, dwu1, dwd1, st_ffn1 = _ffn_bwd(x, ffn1_norm, dx1, saved1, (wg1, wu1), wd1, 0)

    names = ["ffn1_w_gate", "ffn1_w_up", "ffn1_w_down", "w_in", "w_out", "ffn2_w_gate", "ffn2_w_up", "ffn2_w_down"]
    grads = [dwg1, dwu1, dwd1, dwin, dwout.reshape(N_CHIP, OUTB, D_MODEL), dwg2, dwu2, dwd2]
    flipped = {"ffn1_w_gate", "ffn1_w_up", "ffn2_w_gate", "ffn2_w_up"}
    place = lambda n, a: flip(a) if n in flipped else a[0]
    weights = [place(n, a) for n, a in zip(names, [ffn1_w_gate, ffn1_w_up, ffn1_w_down, w_in, w_out, ffn2_w_gate, ffn2_w_up, ffn2_w_down])]
    moms = [place(n, a) for n, a in zip(names, [m_ffn1_w_gate, m_ffn1_w_up, m_ffn1_w_down, m_w_in, m_w_out, m_ffn2_w_gate, m_ffn2_w_up, m_ffn2_w_down])]
    vels = [place(n, a) for n, a in zip(names, [v_ffn1_w_gate, v_ffn1_w_up, v_ffn1_w_down, v_w_in, v_w_out, v_ffn2_w_gate, v_ffn2_w_up, v_ffn2_w_down])]
    mats = {n: tuple((t.T if n in flipped else t)[None] for t in r)
            for n, r in zip(names, _reduce_and_update(grads, weights, moms, vels))}

    vec_names = ["ffn1_norm", "mix_norm", "sb_out_norm", "dil_out_norm", "ffn2_norm", "final_norm"]
    part = jnp.concatenate([st_ffn1[0:1], st_mix[0:1], _pad_row(st_out[0]), _pad_row(st_out[1]), st_ffn2[0:1],
                            st_final[0:1], st_final[1:2], jnp.zeros((1, D_MODEL), F32)], axis=0)
    pack = lambda arrs: jnp.concatenate([_pad_row(a) for a in arrs] + [jnp.zeros((2, D_MODEL), F32)], axis=0)
    g_vec, d_vec, m_vec, v_vec = _reduce_vectors(
        part,
        pack([ffn1_norm, mix_norm, sb_out_norm, dil_out_norm, ffn2_norm, final_norm]),
        pack([m_ffn1_norm, m_mix_norm, m_sb_out_norm, m_dil_out_norm, m_ffn2_norm, m_final_norm]),
        pack([v_ffn1_norm, v_mix_norm, v_sb_out_norm, v_dil_out_norm, v_ffn2_norm, v_final_norm]))
    like = {"ffn1_norm": ffn1_norm, "mix_norm": mix_norm, "sb_out_norm": sb_out_norm, "dil_out_norm": dil_out_norm,
            "ffn2_norm": ffn2_norm, "final_norm": final_norm}
    vecs = {n: tuple(t[i, :like[n].size].reshape(like[n].shape) for t in (g_vec, d_vec, m_vec, v_vec))
            for i, n in enumerate(vec_names)}
    loss = 0.5 * jnp.sum(g_vec[6]) / D_MODEL

    order = ["ffn1_norm", "ffn1_w_gate", "ffn1_w_up", "ffn1_w_down", "mix_norm", "w_in", "sb_out_norm", "dil_out_norm",
             "w_out", "ffn2_norm", "ffn2_w_gate", "ffn2_w_up", "ffn2_w_down", "final_norm"]
    both = {**mats, **vecs}
    return (loss, grad_x[None], *[both[n][0] for n in order], *[both[n][1] for n in order],
            *[both[n][2] for n in order], *[both[n][3] for n in order])
```

```python
import functools

import jax
import jax.numpy as jnp
from jax import lax
from jax.experimental import pallas as pl
from jax.experimental.pallas import tpu as pltpu

D_MODEL = 1024
D_FF = 2816
HEAD_DIM = 64
D_SB = 512
D_DIL = 512
D_IN = 3072
N_CHIP = 4
FFB = D_FF // N_CHIP
INB = D_IN // N_CHIP
OUTB = D_MODEL // N_CHIP
BLK = 128
LANES = 128
DILATIONS = (1, 4, 16)
ROPE_THETA = 10000.0
RMS_EPS = 1e-6
SCALE = HEAD_DIM ** -0.5
NEG = -1e30
DEAD = -104.0
ADAM_LR = 0.001
ADAM_B1 = 0.9
ADAM_B2 = 0.999
ADAM_EPS = 1e-08
ADAM_WD = 0.01
ADAM_STEP = 10
MESH = pl.DeviceIdType.MESH
F32 = jnp.float32
BF16 = jnp.bfloat16
TM = 512


def _params(vmem_mb):
    return pltpu.CompilerParams(vmem_limit_bytes=vmem_mb << 20)


def _dot(a, b):
    return jnp.dot(a, b, preferred_element_type=F32)


def _dot_nt(a, b):
    return lax.dot_general(a, b, (((1,), (1,)), ((), ())), preferred_element_type=F32)


def _dot_tn(a, b):
    return lax.dot_general(a, b, (((0,), (0,)), ((), ())), preferred_element_type=F32)


def _rms_fwd(x, g):
    r = lax.rsqrt(jnp.mean(x * x, axis=-1, keepdims=True) + RMS_EPS)
    xh = x * r
    return xh * g, xh, r


def _rms_bwd(dy, xh, r, g):
    dyg = dy * g
    dx = r * (dyg - xh * jnp.mean(dyg * xh, axis=-1, keepdims=True))
    return dx, jnp.sum(dy * xh, axis=0, keepdims=True)


def _split_bf16(a):
    hi = a.astype(BF16)
    return hi, (a - hi.astype(F32)).astype(BF16)


def _dot_split(a, b2):
    hi, lo = _split_bf16(a)
    return _dot(jnp.concatenate([hi, lo], axis=1), b2)


def _ffn_weight_specs():
    return [pl.BlockSpec((None, FFB, D_MODEL), lambda i, j: (j, 0, 0))] * 3


def _ffn_saved(s):
    hidden = jax.ShapeDtypeStruct((N_CHIP, s, FFB), BF16)
    hid = pl.BlockSpec((None, TM, FFB), lambda i, j: (j, i, 0))
    row = pl.BlockSpec((TM, D_MODEL), lambda i, j: (i, 0))
    return [row, hid, hid, hid], [jax.ShapeDtypeStruct((s, D_MODEL), BF16), hidden, hidden, hidden]


def _ffn_accumulate(h_ref, acc_scr, wg_ref, wu_ref, wd_ref, a_ref, b_ref, act_ref):
    h = h_ref[...]
    a = _dot_nt(h, wg_ref[...])
    b = _dot_nt(h, wu_ref[...])
    act = ((a * jax.nn.sigmoid(a)) * b).astype(BF16)
    a_ref[...] = a.astype(BF16)
    b_ref[...] = b.astype(BF16)
    act_ref[...] = act
    acc_scr[...] += _dot(act, wd_ref[...])


def _host_gather_before(gather, i, j, steps):
    @pl.when((i == 0) & (j == 0))
    def _():
        gather.start()

    @pl.when((i == (3 * steps) // 4) & (j == 0))
    def _():
        gather.forward()


def _host_gather_after(gather, i, j, steps):
    @pl.when((i == steps - 1) & (j == N_CHIP - 1))
    def _():
        gather.finish()


def _ffn1_fwd(x, g1, gmix, gu, wd, later_shards):
    s = x.shape[0]
    row = pl.BlockSpec((TM, D_MODEL), lambda i, j: (i, 0))
    vec = pl.BlockSpec((1, D_MODEL), lambda i, j: (0, 0))
    saved_specs, saved_shapes = _ffn_saved(s)
    n = len(later_shards)
    any_spec = pl.BlockSpec(memory_space=pl.ANY)

    def body(*refs):
        x_ref, g_ref, gm_ref, wg_ref, wu_ref, wd_ref = refs[:6]
        shard_refs, refs = refs[6:6 + n], refs[6 + n:]
        x1_ref, hm_ref, h_ref, a_ref, b_ref, act_ref = refs[:6]
        gathered_refs, acc_scr, gather_scratch = refs[6:6 + n], refs[6 + n], refs[7 + n:]
        gather = _BackgroundGather(shard_refs, gathered_refs, gather_scratch)
        i, j = pl.program_id(0), pl.program_id(1)
        _host_gather_before(gather, i, j, s // TM)

        @pl.when(j == 0)
        def _():
            h, _, _ = _rms_fwd(x_ref[...], g_ref[...])
            h_ref[...] = h.astype(BF16)
            acc_scr[...] = jnp.zeros_like(acc_scr)

        _ffn_accumulate(h_ref, acc_scr, wg_ref, wu_ref, wd_ref, a_ref, b_ref, act_ref)

        @pl.when(j == N_CHIP - 1)
        def _():
            x1 = x_ref[...] + 0.5 * acc_scr[...]
            x1_ref[...] = x1
            hm, _, _ = _rms_fwd(x1, gm_ref[...])
            hm_ref[...] = hm.astype(BF16)

        _host_gather_after(gather, i, j, s // TM)

    x1, hm, h, a, b, act, *gathered = pl.pallas_call(
        body, name="ffn1_fwd", grid=(s // TM, N_CHIP),
        in_specs=[row, vec, vec] + _ffn_weight_specs() + [any_spec] * n,
        out_specs=[row, row] + saved_specs + [any_spec] * n,
        out_shape=([jax.ShapeDtypeStruct((s, D_MODEL), F32), jax.ShapeDtypeStruct((s, D_MODEL), BF16)] + saved_shapes
                   + _BackgroundGather.out_shapes(later_shards)),
        scratch_shapes=[pltpu.VMEM((TM, D_MODEL), F32)] + _BackgroundGather.scratch_shapes(later_shards),
        compiler_params=_params(58),
    )(x, g1, gmix, gu[0], gu[1], wd, *later_shards)
    return x1, hm, [h, a, b, act], gathered


def _ffn2_fwd_loss(x2, g2, gf, target, gu, wd):
    s = x2.shape[0]
    row = pl.BlockSpec((TM, D_MODEL), lambda i, j: (i, 0))
    vec = pl.BlockSpec((1, D_MODEL), lambda i, j: (0, 0))
    stat = pl.BlockSpec((8, D_MODEL), lambda i, j: (0, 0))
    saved_specs, saved_shapes = _ffn_saved(s)

    def body(x_ref, g_ref, gf_ref, t_ref, wg_ref, wu_ref, wd_ref, dx_ref, st_ref, h_ref, a_ref, b_ref, act_ref, acc_scr):
        i, j = pl.program_id(0), pl.program_id(1)

        @pl.when((i == 0) & (j == 0))
        def _():
            st_ref[...] = jnp.zeros_like(st_ref)

        @pl.when(j == 0)
        def _():
            h, _, _ = _rms_fwd(x_ref[...], g_ref[...])
            h_ref[...] = h.astype(BF16)
            acc_scr[...] = jnp.zeros_like(acc_scr)

        _ffn_accumulate(h_ref, acc_scr, wg_ref, wu_ref, wd_ref, a_ref, b_ref, act_ref)

        @pl.when(j == N_CHIP - 1)
        def _():
            x3 = x_ref[...] + 0.5 * acc_scr[...]
            y, xh, r = _rms_fwd(x3, gf_ref[...])
            err = y - t_ref[...]
            dx, dg = _rms_bwd(err * (1.0 / D_MODEL), xh, r, gf_ref[...])
            dx_ref[...] = dx
            st_ref[0:1, :] += dg
            st_ref[1:2, :] += jnp.sum(err * err, axis=0, keepdims=True)

    dx3, st, *saved = pl.pallas_call(
        body, name="ffn2_fwd_loss", grid=(s // TM, N_CHIP),
        in_specs=[row, vec, vec, row] + _ffn_weight_specs(),
        out_specs=[row, stat] + saved_specs,
        out_shape=[jax.ShapeDtypeStruct((s, D_MODEL), F32), jax.ShapeDtypeStruct((8, D_MODEL), F32)] + saved_shapes,
        scratch_shapes=[pltpu.VMEM((TM, D_MODEL), F32)],
        compiler_params=_params(56),
    )(x2, g2, gf, target, gu[0], gu[1], wd)
    return dx3, st, saved


def _ffn_bwd(xin, g, dy, saved, gu, wd, f):
    s = xin.shape[0]
    hb, gate, up, act = saved
    row = pl.BlockSpec((TM, D_MODEL), lambda i, j: (i, 0))
    vec = pl.BlockSpec((1, D_MODEL), lambda i, j: (0, 0))
    stat = pl.BlockSpec((8, D_MODEL), lambda i, j: (0, 0))
    hid = pl.BlockSpec((None, TM, FFB), lambda i, j: (j, i, 0))

    def body(x_ref, g_ref, dy_ref, a_ref, b_ref, wg_ref, wu_ref, wd_ref, wd_next_ref,
             out_ref, dyh_ref, da_ref, db_ref, st_ref, dh_scr, dact_scr):
        i, j = pl.program_id(0), pl.program_id(1)

        @pl.when((i == 0) & (j == 0))
        def _():
            st_ref[...] = jnp.zeros_like(st_ref)

        @pl.when(j == 0)
        def _():
            dyh_ref[...] = (0.5 * dy_ref[...]).astype(BF16)
            dh_scr[...] = jnp.zeros_like(dh_scr)
            dact_scr[...] = _dot_nt(dyh_ref[...], wd_ref[...])

        a = a_ref[...].astype(F32)
        b = b_ref[...].astype(F32)
        sg = jax.nn.sigmoid(a)
        dact = dact_scr[...]
        dab = (dact * b * (sg * (1.0 + a * (1.0 - sg)))).astype(BF16)
        dbb = (dact * (a * sg)).astype(BF16)
        da_ref[...] = dab
        db_ref[...] = dbb
        dact_scr[...] = _dot_nt(dyh_ref[...], wd_next_ref[...])
        dh_scr[...] += _dot(dab, wg_ref[...]) + _dot(dbb, wu_ref[...])

        @pl.when(j == N_CHIP - 1)
        def _():
            _, xh, r = _rms_fwd(x_ref[...], g_ref[...])
            dx, dg = _rms_bwd(dh_scr[...], xh, r, g_ref[...])
            out_ref[...] = dy_ref[...] + dx
            st_ref[0:1, :] += dg

    hidden = jax.ShapeDtypeStruct((N_CHIP, s, FFB), BF16)
    dx, dyh, da, db, st = pl.pallas_call(
        body, name=f"ffn{f + 1}_bwd_dx", grid=(s // TM, N_CHIP),
        in_specs=([row, vec, row, hid, hid] + _ffn_weight_specs()
                  + [pl.BlockSpec((None, FFB, D_MODEL), lambda i, j: ((j + 1) % N_CHIP, 0, 0))]),
        out_specs=[row, row, hid, hid, stat],
        out_shape=[jax.ShapeDtypeStruct((s, D_MODEL), F32), jax.ShapeDtypeStruct((s, D_MODEL), BF16),
                   hidden, hidden, jax.ShapeDtypeStruct((8, D_MODEL), F32)],
        scratch_shapes=[pltpu.VMEM((TM, D_MODEL), F32), pltpu.VMEM((TM, FFB), F32)],
        compiler_params=_params(58),
    )(xin, g, dy, gate, up, gu[0], gu[1], wd, wd)

    tk = 4 * TM
    tok = pl.BlockSpec((tk, D_MODEL), lambda j, i: (i, 0))
    hid2 = pl.BlockSpec((None, tk, FFB), lambda j, i: (j, i, 0))
    gspecs = [pl.BlockSpec((None, FFB, D_MODEL), lambda j, i: (j, 0, 0))] * 3

    def wbody(h_ref, dyh_ref, da_ref, db_ref, act_ref, dwg_ref, dwu_ref, dwd_ref):
        @pl.when(pl.program_id(1) == 0)
        def _():
            dwg_ref[...] = jnp.zeros_like(dwg_ref)
            dwu_ref[...] = jnp.zeros_like(dwu_ref)
            dwd_ref[...] = jnp.zeros_like(dwd_ref)

        hb = h_ref[...]
        dwg_ref[...] += _dot_tn(da_ref[...], hb)
        dwu_ref[...] += _dot_tn(db_ref[...], hb)
        dwd_ref[...] += _dot_tn(act_ref[...], dyh_ref[...])

    dwg, dwu, dwd = pl.pallas_call(
        wbody, name=f"ffn{f + 1}_bwd_dw", grid=(N_CHIP, s // tk),
        in_specs=[tok, tok, hid2, hid2, hid2], out_specs=gspecs,
        out_shape=[jax.ShapeDtypeStruct((N_CHIP, FFB, D_MODEL), F32)] * 3,
        compiler_params=_params(60),
    )(hb, dyh, da, db, act)
    return dx, dwg, dwu, dwd, st


def _rope_tables(s):
    half = HEAD_DIM // 2
    inv_freq = ROPE_THETA ** (-jnp.arange(half, dtype=F32) / half)
    ang = jnp.arange(s).astype(F32)[:, None] * inv_freq[None, :]
    cos, sin = jnp.cos(ang), jnp.sin(ang)
    cos2 = jnp.concatenate([cos, cos], axis=-1)
    sin2 = jnp.concatenate([-sin, sin], axis=-1)
    return jnp.tile(cos2, (1, LANES // HEAD_DIM)), jnp.tile(sin2, (1, LANES // HEAD_DIM))


def _rotate(t, cos, sin_signed):
    lane = lax.broadcasted_iota(jnp.int32, t.shape, 1)
    first = (lane % HEAD_DIM) < (HEAD_DIM // 2)
    partner = jnp.where(first, pltpu.roll(t, LANES - HEAD_DIM // 2, 1), pltpu.roll(t, HEAD_DIM // 2, 1))
    return t * cos + partner * sin_signed


def _proj_fwd(hm, win, cos, sin, later_shards):
    s = hm.shape[0]
    tm = 2 * TM
    n_sub = INB // LANES
    first_rot, last_rot = (3 * D_SB) // LANES, (3 * D_SB + 2 * D_DIL) // LANES
    n = len(later_shards)
    any_spec = pl.BlockSpec(memory_space=pl.ANY)

    def body(*refs):
        h_ref, w_ref, c_ref, s_ref = refs[:4]
        shard_refs, o_ref, gathered_refs, gather_scratch = refs[4:4 + n], refs[4 + n], refs[5 + n:5 + 2 * n], refs[5 + 2 * n:]
        gather = _BackgroundGather(shard_refs, gathered_refs, gather_scratch)
        i, j = pl.program_id(0), pl.program_id(1)
        _host_gather_before(gather, i, j, s // tm)
        r = _dot(h_ref[...], w_ref[...])
        for c in range(n_sub):
            t = r[:, c * LANES:(c + 1) * LANES]
            col = j * n_sub + c
            rot = (col >= first_rot) & (col < last_rot)
            lanes = slice(c * LANES, (c + 1) * LANES)

            @pl.when(rot)
            def _():
                o_ref[:, lanes] = _rotate(t, c_ref[...], s_ref[...]).astype(BF16)

            @pl.when(jnp.logical_not(rot))
            def _():
                o_ref[:, lanes] = t.astype(BF16)

        _host_gather_after(gather, i, j, s // tm)

    qkv, *gathered = pl.pallas_call(
        body, name="proj_fwd", grid=(s // tm, N_CHIP),
        in_specs=[pl.BlockSpec((tm, D_MODEL), lambda i, j: (i, 0)),
                  pl.BlockSpec((None, D_MODEL, INB), lambda i, j: (j, 0, 0)),
                  pl.BlockSpec((tm, LANES), lambda i, j: (i, 0)),
                  pl.BlockSpec((tm, LANES), lambda i, j: (i, 0))] + [any_spec] * n,
        out_specs=[pl.BlockSpec((tm, INB), lambda i, j: (i, j))] + [any_spec] * n,
        out_shape=[jax.ShapeDtypeStruct((s, D_IN), BF16)] + _BackgroundGather.out_shapes(later_shards),
        scratch_shapes=_BackgroundGather.scratch_shapes(later_shards),
        compiler_params=_params(48),
    )(hm, win, cos, sin, *later_shards)
    return qkv, gathered


def _proj_bwd(x1, gmix, dqkv_sb, dqkv_dl, win, dx2):
    s = x1.shape[0]
    row = pl.BlockSpec((TM, D_MODEL), lambda i, j: (i, 0))
    vec = pl.BlockSpec((1, D_MODEL), lambda i, j: (0, 0))
    per_group = N_CHIP // 2

    def body(x_ref, g_ref, dsb_ref, ddl_ref, w_ref, dx2_ref, out_ref, dw_ref, st_ref, h_scr, dh_scr, dq_ref):
        i, j = pl.program_id(0), pl.program_id(1)

        @pl.when(j < per_group)
        def _():
            dq_ref[...] = dsb_ref[...]

        @pl.when(j >= per_group)
        def _():
            dq_ref[...] = ddl_ref[...]

        @pl.when((i == 0) & (j == 0))
        def _():
            st_ref[...] = jnp.zeros_like(st_ref)
            dw_ref[...] = jnp.zeros_like(dw_ref)

        @pl.when(j == 0)
        def _():
            h, _, _ = _rms_fwd(x_ref[...], g_ref[...])
            h_scr[...] = h.astype(BF16)
            dh_scr[...] = jnp.zeros_like(dh_scr)

        dq = dq_ref[...]
        dw_ref[j] += _dot_tn(h_scr[...], dq)
        dh_scr[...] += _dot_nt(dq, w_ref[...])

        @pl.when(j == N_CHIP - 1)
        def _():
            _, xh, r = _rms_fwd(x_ref[...], g_ref[...])
            dx, dg = _rms_bwd(dh_scr[...], xh, r, g_ref[...])
            out_ref[...] = dx2_ref[...] + dx
            st_ref[0:1, :] += dg

    return pl.pallas_call(
        body, name="proj_bwd", grid=(s // TM, N_CHIP),
        in_specs=[row, vec,
                  pl.BlockSpec((TM, INB), lambda i, j: (i, jnp.minimum(j, per_group - 1))),
                  pl.BlockSpec((TM, INB), lambda i, j: (i, jnp.maximum(j - per_group, 0))),
                  pl.BlockSpec((None, D_MODEL, INB), lambda i, j: (j, 0, 0)), row],
        out_specs=[row, pl.BlockSpec((N_CHIP, D_MODEL, INB), lambda i, j: (0, 0, 0)),
                   pl.BlockSpec((8, D_MODEL), lambda i, j: (0, 0))],
        out_shape=[jax.ShapeDtypeStruct((s, D_MODEL), F32),
                   jax.ShapeDtypeStruct((N_CHIP, D_MODEL, INB), F32),
                   jax.ShapeDtypeStruct((8, D_MODEL), F32)],
        scratch_shapes=[pltpu.VMEM((TM, D_MODEL), BF16), pltpu.VMEM((TM, D_MODEL), F32), pltpu.VMEM((TM, INB), BF16)],
        compiler_params=_params(56),
    )(x1, gmix, dqkv_sb, dqkv_dl, win, dx2)


def _outproj_fwd(o_sb, o_dl, g_sb, g_dl, x1, wout):
    s = x1.shape[0]
    tm = 2 * TM
    half = pl.BlockSpec((tm, D_SB), lambda i: (i, 0))
    row = pl.BlockSpec((tm, D_MODEL), lambda i: (i, 0))
    vec = pl.BlockSpec((1, D_SB), lambda i: (0, 0))

    def body(a_ref, b_ref, ga_ref, gb_ref, x_ref, w_ref, o_ref):
        ma, _, _ = _rms_fwd(a_ref[...], ga_ref[...])
        mb, _, _ = _rms_fwd(b_ref[...], gb_ref[...])
        o_ref[...] = (x_ref[...] + _dot(ma.astype(BF16), w_ref[0:D_SB, :])
                      + _dot(mb.astype(BF16), w_ref[D_SB:D_MODEL, :]))

    return pl.pallas_call(
        body, name="outproj_fwd", grid=(s // tm,),
        in_specs=[half, half, vec, vec, row, pl.BlockSpec((D_MODEL, D_MODEL), lambda i: (0, 0))],
        out_specs=row, out_shape=jax.ShapeDtypeStruct((s, D_MODEL), F32),
        compiler_params=_params(32),
    )(o_sb, o_dl, g_sb, g_dl, x1, wout)


def _outproj_bwd(dx2, o_sb, o_dl, g_sb, g_dl, wout):
    s = dx2.shape[0]
    tm = 2 * TM
    half = pl.BlockSpec((tm, D_SB), lambda i: (i, 0))
    row = pl.BlockSpec((tm, D_MODEL), lambda i: (i, 0))
    vec = pl.BlockSpec((1, D_SB), lambda i: (0, 0))
    full = pl.BlockSpec((D_MODEL, D_MODEL), lambda i: (0, 0))

    def body(dy_ref, a_ref, b_ref, ga_ref, gb_ref, w_ref, da_ref, db_ref, dl_ref, dw_ref, st_ref):
        @pl.when(pl.program_id(0) == 0)
        def _():
            dw_ref[...] = jnp.zeros_like(dw_ref)
            st_ref[...] = jnp.zeros_like(st_ref)

        dy = dy_ref[...].astype(BF16)
        dm = _dot_nt(dy, w_ref[...])
        ma, xa, ra = _rms_fwd(a_ref[...], ga_ref[...])
        mb, xb, rb = _rms_fwd(b_ref[...], gb_ref[...])
        dw_ref[0:D_SB, :] += _dot_tn(ma.astype(BF16), dy)
        dw_ref[D_SB:D_MODEL, :] += _dot_tn(mb.astype(BF16), dy)
        da, dga = _rms_bwd(dm[:, 0:D_SB], xa, ra, ga_ref[...])
        db, dgb = _rms_bwd(dm[:, D_SB:D_MODEL], xb, rb, gb_ref[...])
        da_ref[...] = da
        db_ref[...] = db
        r = lax.broadcasted_iota(jnp.int32, (LANES, LANES), 0) >= HEAD_DIM
        c = lax.broadcasted_iota(jnp.int32, (LANES, LANES), 1) >= HEAD_DIM
        same_head = jnp.where(r == c, 1.0, 0.0).astype(BF16)
        same_head = jnp.concatenate([same_head, same_head], axis=0)
        prod = db * b_ref[...]
        for k in range(D_DIL // LANES):
            lanes = slice(k * LANES, (k + 1) * LANES)
            dl_ref[:, lanes] = _dot_split(prod[:, lanes], same_head)
        st_ref[0:1, :] += dga
        st_ref[1:2, :] += dgb

    return pl.pallas_call(
        body, name="outproj_bwd", grid=(s // tm,),
        in_specs=[row, half, half, vec, vec, full],
        out_specs=[half, half, half, full, pl.BlockSpec((8, D_SB), lambda i: (0, 0))],
        out_shape=[jax.ShapeDtypeStruct((s, D_SB), F32), jax.ShapeDtypeStruct((s, D_SB), F32),
                   jax.ShapeDtypeStruct((s, D_DIL), F32),
                   jax.ShapeDtypeStruct((D_MODEL, D_MODEL), F32), jax.ShapeDtypeStruct((8, D_SB), F32)],
        compiler_params=_params(48),
    )(dx2, o_sb, o_dl, g_sb, g_dl, wout)


def _head_masks():
    lane = lax.broadcasted_iota(jnp.int32, (BLK, LANES), 1)
    return [lane < HEAD_DIM, lane >= HEAD_DIM]


def _keep(mask, a):
    return a * jnp.where(mask, 1.0, 0.0).astype(a.dtype)


def _suffix_matrices():
    r = lax.broadcasted_iota(jnp.int32, (2 * BLK, BLK), 0) & (BLK - 1)
    c = lax.broadcasted_iota(jnp.int32, (2 * BLK, BLK), 1)
    ones = jnp.ones((2 * BLK, BLK), BF16)
    excl = jnp.concatenate([(r > c).astype(BF16), ones], axis=1)
    incl = jnp.concatenate([(r >= c).astype(BF16), ones], axis=1)
    return excl, incl


def _blk(i):
    return pl.ds(pl.multiple_of(i * BLK, BLK), BLK)


def _alive(carry_m):
    return (jnp.max(carry_m) > DEAD).astype(jnp.int32)


def _more_keys(last, carry):
    return (carry[0] * SB_KB <= last) & (carry[1] > 0)


def _stack_heads(a):
    masks = _head_masks()
    return jnp.concatenate([_keep(masks[0], a), _keep(masks[1], a)], axis=0)


def _unstack_heads(a2):
    return jnp.where(_head_masks()[0], a2[:BLK], a2[BLK:])


def _head_rowsum(a):
    masks = _head_masks()
    return jnp.concatenate([jnp.sum(jnp.where(m, a, 0.0), axis=1, keepdims=True) for m in masks], axis=0)


SB_QB = 2
SB_ROWS = SB_QB * 2 * BLK
SB_KB = 4
PAST_START = 1 << 30


def _sb_rows(ref, i0, cast=None):
    tiles = [ref[_blk(i0 + t), :] for t in range(SB_QB)]
    return jnp.concatenate([_stack_heads(t if cast is None else t.astype(cast)) for t in tiles], axis=0)


_SB_LATER_ROWS = (SB_QB - 1) * 2 * BLK


def _put_rows(full, rows, part):
    return part if rows.start == 0 else jnp.concatenate([full[:rows.start], part], axis=0)


def _sb_scores(q2, k, i, j, carry_m, u_excl):
    r = lax.broadcasted_iota(jnp.int32, (q2.shape[0], BLK), 0)
    row = (r & (BLK - 1)) + ((r >> 8) << 7)
    col = lax.broadcasted_iota(jnp.int32, (q2.shape[0], BLK), 1)
    valid = (jnp.where(j >= 0, j * BLK, PAST_START) + col) < (i * BLK + row)
    z = _dot_nt(q2, k) * SCALE
    sp = jnp.maximum(z, 0.0) + jnp.log(1.0 + jnp.exp(-jnp.abs(z)))
    log_stay = jnp.where(valid, -sp, 0.0)
    log_beta = z - sp
    sums = _dot_split(log_stay, u_excl)
    later = carry_m + sums[:, :BLK]
    w = jnp.where(valid, jnp.exp(log_beta + later), 0.0)
    return valid, log_beta, w, carry_m + sums[:, BLK:]


def _sb_fwd(qkv):
    s = qkv.shape[0]
    nq = s // BLK
    pairs = D_SB // LANES
    col = lambda off: pl.BlockSpec((s, LANES), lambda p: (0, off + p))

    def body(q_ref, k_ref, v_ref, o_ref):
        u_excl, _ = _suffix_matrices()
        zero = jnp.zeros((SB_ROWS, LANES), F32)

        def q_block(ib, _):
            i = ib * SB_QB
            last = i + SB_QB - 1
            q2 = _sb_rows(q_ref, i)

            def trip(jj, carry_m, acc, first):
                for t in range(SB_KB):
                    j = last - jj * SB_KB - t
                    at = _blk(jnp.maximum(j, 0))
                    rows = slice(_SB_LATER_ROWS, SB_ROWS) if first and t == 0 else slice(0, SB_ROWS)
                    base = i + rows.start // (2 * BLK)
                    _, _, w, part = _sb_scores(q2[rows], k_ref[at, :], base, j, carry_m[rows], u_excl)
                    carry_m = _put_rows(carry_m, rows, part)
                    acc = _put_rows(acc, rows, acc[rows] + _dot(w.astype(BF16), v_ref[at, :]))
                return carry_m, acc

            def k_block(carry):
                carry_m, acc = trip(carry[0], carry[2], carry[3], False)
                return carry[0] + 1, _alive(carry_m), carry_m, acc

            carry_m, acc = trip(0, zero, zero, True)
            _, _, _, acc = lax.while_loop(functools.partial(_more_keys, last), k_block,
                                          (jnp.int32(1), _alive(carry_m), carry_m, acc))
            for t in range(SB_QB):
                o_ref[_blk(i + t), :] = _unstack_heads(acc[2 * BLK * t:2 * BLK * (t + 1)])
            return 0

        lax.fori_loop(0, nq // SB_QB, q_block, 0)

    return pl.pallas_call(
        body, name="sb_fwd", grid=(pairs,),
        in_specs=[col(0), col(pairs), col(2 * pairs)],
        out_specs=pl.BlockSpec((s, LANES), lambda p: (0, p)),
        out_shape=jax.ShapeDtypeStruct((s, D_SB), F32),
        compiler_params=_params(48),
    )(qkv, qkv, qkv)


def _sb_bwd(qkv, o_sb, do_sb):
    s = qkv.shape[0]
    nq = s // BLK
    pairs = D_SB // LANES
    col = lambda off: pl.BlockSpec((s, LANES), lambda p, w: (0, off + p))
    own = pl.BlockSpec((s, LANES), lambda p, w: (0, p))

    def body(q_ref, k_ref, v_ref, o_ref, do_ref, out_ref, dq_acc, dk_acc, dv_acc):
        which = pl.program_id(1)

        @pl.when(which == 0)
        def _():
            walk(q_ref, k_ref, v_ref, o_ref, do_ref, dq_acc, dk_acc, dv_acc)
            out_ref[...] = dq_acc[...]

        @pl.when(which == 1)
        def _():
            out_ref[...] = dk_acc[...].astype(BF16)

        @pl.when(which == 2)
        def _():
            out_ref[...] = dv_acc[...].astype(BF16)

    def walk(q_ref, k_ref, v_ref, o_ref, do_ref, dq_ref, dk_acc, dv_acc):
        u_excl, u_incl = _suffix_matrices()
        zero = jnp.zeros((SB_ROWS, LANES), F32)
        dk_acc[...] = jnp.zeros_like(dk_acc)
        dv_acc[...] = jnp.zeros_like(dv_acc)

        def q_block(ib, _):
            i = ib * SB_QB
            last = i + SB_QB - 1
            q2 = _sb_rows(q_ref, i)
            do2 = _sb_rows(do_ref, i, BF16)
            totals = [_head_rowsum(do_ref[_blk(i + t), :].astype(BF16).astype(F32) * o_ref[_blk(i + t), :])
                      for t in range(SB_QB)]
            total = jnp.broadcast_to(jnp.concatenate(totals, axis=0), (SB_ROWS, BLK))

            def trip(jj, carry_m, carry_g, dq, first):
                for t in range(SB_KB):
                    j = last - jj * SB_KB - t
                    at = _blk(jnp.maximum(j, 0))
                    k = k_ref[at, :]
                    rows = slice(_SB_LATER_ROWS, SB_ROWS) if first and t == 0 else slice(0, SB_ROWS)
                    base = i + rows.start // (2 * BLK)
                    valid, log_beta, w, part_m = _sb_scores(q2[rows], k, base, j, carry_m[rows], u_excl)
                    wb = w.astype(BF16)
                    g = wb.astype(F32) * _dot_nt(do2[rows], v_ref[at, :])
                    sums = _dot_split(g, u_incl)
                    before = total[rows] - (carry_g[rows] + sums[:, :BLK])
                    dz = jnp.where(valid, g - jnp.exp(log_beta) * (g + before), 0.0)
                    dzb = (dz * SCALE).astype(BF16)
                    dk_acc[at, :] += _dot_tn(dzb, q2[rows])
                    dv_acc[at, :] += _dot_tn(wb, do2[rows])
                    carry_m = _put_rows(carry_m, rows, part_m)
                    carry_g = _put_rows(carry_g, rows, carry_g[rows] + sums[:, BLK:])
                    dq = _put_rows(dq, rows, dq[rows] + _dot(dzb, k))
                return carry_m, carry_g, dq

            def k_block(carry):
                carry_m, carry_g, dq = trip(carry[0], carry[2], carry[3], carry[4], False)
                return carry[0] + 1, _alive(carry_m), carry_m, carry_g, dq

            carry_m, carry_g, dq = trip(0, zero, zero, zero, True)
            _, _, _, _, dq = lax.while_loop(functools.partial(_more_keys, last), k_block,
                                            (jnp.int32(1), _alive(carry_m), carry_m, carry_g, dq))
            for t in range(SB_QB):
                dq_ref[_blk(i + t), :] = _unstack_heads(dq[2 * BLK * t:2 * BLK * (t + 1)]).astype(BF16)
            return 0

        lax.fori_loop(0, nq // SB_QB, q_block, 0)

    return pl.pallas_call(
        body, name="sb_bwd", grid=(pairs, 3),
        in_specs=[col(0), col(pairs), col(2 * pairs), own, own],
        out_specs=pl.BlockSpec((s, LANES), lambda p, w: (0, w * pairs + p)),
        out_shape=jax.ShapeDtypeStruct((s, 3 * D_SB), BF16),
        scratch_shapes=[pltpu.VMEM((s, LANES), BF16), pltpu.VMEM((s, LANES), F32), pltpu.VMEM((s, LANES), F32)],
        compiler_params=_params(58),
    )(qkv, qkv, qkv, o_sb, do_sb)


DIL_UNROLL = 16


def _band_masks(b):
    row = lax.broadcasted_iota(jnp.int32, (2 * BLK, BLK), 0) & (BLK - 1)
    col = lax.broadcasted_iota(jnp.int32, (2 * BLK, BLK), 1)
    return col <= row, (col - row) >= jnp.where(b > 0, 0, BLK)


def _dil_tiles(qf, kf, vf, d, t, nb):
    c, b = t // nb, t % nb
    start = c + d * BLK * b
    rows = pl.ds(start, BLK, stride=d)
    prev = pl.ds(jnp.where(b > 0, start - d * BLK, start), BLK, stride=d)
    bf = lambda ref, sl: ref[sl, :].astype(BF16)
    return b, rows, prev, _stack_heads(bf(qf, rows)), bf(kf, rows), bf(kf, prev), bf(vf, rows), bf(vf, prev)


def _lanes_of_heads(col2):
    return _unstack_heads(jnp.broadcast_to(col2, (2 * BLK, LANES)))


def _dilated_fwd(qkv):
    s = qkv.shape[0]
    pairs = D_DIL // LANES
    base = (3 * D_SB) // LANES
    col = lambda off: pl.BlockSpec((s, LANES), lambda p: (0, off + p))
    own = pl.BlockSpec((s, LANES), lambda p: (0, p))

    def body(q_ref, k_ref, v_ref, acc_ref, m_ref, qf, kf, vf, l_scr):
        qf[...] = q_ref[...].astype(F32)
        kf[...] = k_ref[...].astype(F32)
        vf[...] = v_ref[...].astype(F32)
        for d in DILATIONS:
            nb = s // (d * BLK)

            def block(t, _):
                b, rows, prev, q2, kc, kp, vc, vp = _dil_tiles(qf, kf, vf, d, t, nb)
                in_cur, in_prev = _band_masks(b)
                zc = jnp.where(in_cur, _dot_nt(q2, kc) * SCALE, NEG)
                zp = jnp.where(in_prev, _dot_nt(q2, kp) * SCALE, NEG)
                m = jnp.maximum(jnp.max(zc, axis=1, keepdims=True), jnp.max(zp, axis=1, keepdims=True))
                pc, pp = jnp.exp(zc - m), jnp.exp(zp - m)
                den = jnp.sum(pc, axis=1, keepdims=True) + jnp.sum(pp, axis=1, keepdims=True)
                acc = _unstack_heads(_dot(pc.astype(BF16), vc) + _dot(pp.astype(BF16), vp))
                m_t, l_t = _lanes_of_heads(m), _lanes_of_heads(den)
                if d == DILATIONS[0]:
                    m_ref[rows, :] = m_t
                    l_scr[rows, :] = l_t
                    acc_ref[rows, :] = acc
                else:
                    m_old = m_ref[rows, :]
                    m_new = jnp.maximum(m_old, m_t)
                    keep, add = jnp.exp(m_old - m_new), jnp.exp(m_t - m_new)
                    m_ref[rows, :] = m_new
                    l_scr[rows, :] = l_scr[rows, :] * keep + l_t * add
                    acc_ref[rows, :] = acc_ref[rows, :] * keep + acc * add
                return 0

            lax.fori_loop(0, s // BLK, block, 0, unroll=DIL_UNROLL)

        def finish(i, _):
            l = l_scr[_blk(i), :]
            acc_ref[_blk(i), :] = acc_ref[_blk(i), :] / l
            m_ref[_blk(i), :] = m_ref[_blk(i), :] + jnp.log(l)
            return 0

        lax.fori_loop(0, s // BLK, finish, 0)

    return pl.pallas_call(
        body, name="dilated_fwd", grid=(pairs,),
        in_specs=[col(base), col(base + pairs), col(base + 2 * pairs)],
        out_specs=[own, own],
        out_shape=[jax.ShapeDtypeStruct((s, D_DIL), F32)] * 2,
        scratch_shapes=[pltpu.VMEM((s, LANES), F32)] * 4,
        compiler_params=_params(56),
    )(qkv, qkv, qkv)


def _stack_lanes(t):
    other = pltpu.roll(t, HEAD_DIM, 1)
    first = _head_masks()[0]
    return jnp.concatenate([jnp.where(first, t, other), jnp.where(first, other, t)], axis=0)


def _dilated_bwd(qkv, delta, lse, dout):
    s = qkv.shape[0]
    pairs = D_DIL // LANES
    base = (3 * D_SB) // LANES
    once = pl.Buffered(1)
    col = lambda off: pl.BlockSpec((s, LANES), lambda p: (0, off + p), pipeline_mode=once)
    own = pl.BlockSpec((s, LANES), lambda p: (0, p), pipeline_mode=once)
    res = pl.BlockSpec((s, LANES), lambda p: (0, p))

    def body(q_ref, k_ref, v_ref, dl_ref, l_ref, do_ref, dq_ref, dk_ref, dv_ref, qf, kf, vf):
        qf[...] = q_ref[...].astype(F32)
        kf[...] = k_ref[...].astype(F32)
        vf[...] = v_ref[...].astype(F32)
        dq_ref[...] = jnp.zeros_like(dq_ref)
        dk_ref[...] = jnp.zeros_like(dk_ref)
        dv_ref[...] = jnp.zeros_like(dv_ref)
        for d in DILATIONS:
            nb = s // (d * BLK)

            def block(t, _):
                b, rows, prev, q2, kc, kp, vc, vp = _dil_tiles(qf, kf, vf, d, t, nb)
                in_cur, in_prev = _band_masks(b)
                do2 = _stack_heads(do_ref[rows, :].astype(BF16))
                delta = _stack_lanes(dl_ref[rows, :])
                lse2 = _stack_lanes(l_ref[rows, :])
                wc = jnp.exp(jnp.where(in_cur, _dot_nt(q2, kc) * SCALE, NEG) - lse2)
                wp = jnp.exp(jnp.where(in_prev, _dot_nt(q2, kp) * SCALE, NEG) - lse2)
                dzc = (wc * (_dot_nt(do2, vc) - delta) * SCALE).astype(BF16)
                dzp = (wp * (_dot_nt(do2, vp) - delta) * SCALE).astype(BF16)
                dq_ref[rows, :] += _unstack_heads(_dot(dzc, kc) + _dot(dzp, kp))
                dk_ref[rows, :] += _dot_tn(dzc, q2)
                dk_ref[prev, :] += _dot_tn(dzp, q2)
                dv_ref[rows, :] += _dot_tn(wc.astype(BF16), do2)
                dv_ref[prev, :] += _dot_tn(wp.astype(BF16), do2)
                return 0

            lax.fori_loop(0, s // BLK, block, 0, unroll=DIL_UNROLL)

    return pl.pallas_call(
        body, name="dilated_bwd", grid=(pairs,),
        in_specs=[col(base), col(base + pairs), col(base + 2 * pairs), own, own, own],
        out_specs=[res, res, res],
        out_shape=[jax.ShapeDtypeStruct((s, D_DIL), F32)] * 3,
        scratch_shapes=[pltpu.VMEM((s, LANES), F32)] * 3,
        compiler_params=_params(60),
    )(qkv, qkv, qkv, delta, lse, dout)


def _dilated_finish(grads, cos, sin):
    s = grads[0].shape[0]
    spec = pl.BlockSpec((TM, D_DIL), lambda i: (i, 0))
    tab = pl.BlockSpec((TM, LANES), lambda i: (i, 0))

    def body(dq_ref, dk_ref, dv_ref, c_ref, s_ref, out_ref):
        for t, (src, rotated) in enumerate(((dq_ref, True), (dk_ref, True), (dv_ref, False))):
            for c in range(D_DIL // LANES):
                piece = src[:, c * LANES:(c + 1) * LANES]
                at = t * D_DIL + c * LANES
                out_ref[:, at:at + LANES] = (_rotate(piece, c_ref[...], -s_ref[...]) if rotated else piece).astype(BF16)

    return pl.pallas_call(
        body, name="dilated_finish", grid=(s // TM,),
        in_specs=[spec] * 3 + [tab, tab], out_specs=pl.BlockSpec((TM, 3 * D_DIL), lambda i: (i, 0)),
        out_shape=jax.ShapeDtypeStruct((s, 3 * D_DIL), BF16),
        compiler_params=_params(32),
    )(*grads, cos, sin)


def _place():
    x, y, c = lax.axis_index("x"), lax.axis_index("y"), lax.axis_index("c")
    return x, y, c, 2 * x + y


def _chip(k, c):
    return (k >> 1, k & 1, c)


def _half(ref, h):
    n = ref.shape[0] // 2
    return ref.at[pl.ds(h * n, n)]


class _BackgroundGather:
    def __init__(self, ins, outs, scratch):
        n = self.n = len(ins)
        self.ins, self.outs = ins, outs
        self.mine, self.landed, self.passed = scratch[0:3 * n:3], scratch[1:3 * n:3], scratch[2:3 * n:3]
        self.send_sem, self.recv_sem, self.local_sem = scratch[3 * n:3 * n + 3]
        x, y, self.c, self.k = _place()
        self.sibling = (x, y, 1 - self.c)

    @staticmethod
    def scratch_shapes(shards):
        shapes = []
        for a in shards:
            half = (N_CHIP - 1, a.shape[0] // 2, a.shape[1])
            shapes += [pltpu.VMEM(a.shape, a.dtype), pltpu.VMEM(half, a.dtype), pltpu.VMEM(half, a.dtype)]
        n = len(shards)
        return shapes + [pltpu.SemaphoreType.DMA((6 * n,)), pltpu.SemaphoreType.DMA((6 * n,)),
                         pltpu.SemaphoreType.DMA((8 * n,))]

    @staticmethod
    def out_shapes(shards):
        return [jax.ShapeDtypeStruct((N_CHIP,) + a.shape, a.dtype) for a in shards]

    def _remote(self, a, slot, src, dst, to):
        return pltpu.make_async_remote_copy(src_ref=src, dst_ref=dst, send_sem=self.send_sem.at[6 * a + slot],
                                            recv_sem=self.recv_sem.at[6 * a + slot], device_id=to, device_id_type=MESH)

    def _local(self, a, slot, src, dst):
        return pltpu.make_async_copy(src, dst, self.local_sem.at[8 * a + slot])

    def _ici(self, a, j):
        return self._remote(a, j - 1, _half(self.mine[a], self.c), self.landed[a].at[j - 1], _chip(self.k ^ j, self.c))

    def _to_sibling(self, a, j):
        return self._remote(a, 2 + j, self.landed[a].at[j - 1], self.passed[a].at[j - 1], self.sibling)

    def _own(self, a):
        return self._local(a, 0, self.ins[a], self.outs[a].at[self.k])

    def _load(self, a):
        return self._local(a, 1, self.ins[a], self.mine[a])

    def _store_landed(self, a, j):
        return self._local(a, 1 + j, self.landed[a].at[j - 1], _half(self.outs[a].at[self.k ^ j], self.c))

    def _store_passed(self, a, j):
        return self._local(a, 4 + j, self.passed[a].at[j - 1], _half(self.outs[a].at[self.k ^ j], 1 - self.c))

    def start(self):
        for a in range(self.n):
            self._own(a).start()
            self._load(a).start()
        for a in range(self.n):
            self._load(a).wait()
            for j in range(1, N_CHIP):
                self._ici(a, j).start()

    def forward(self):
        for j in range(1, N_CHIP):
            for a in range(self.n):
                self._ici(a, j).wait_recv()
                self._to_sibling(a, j).start()
                self._store_landed(a, j).start()

    def finish(self):
        for j in range(1, N_CHIP):
            for a in range(self.n):
                self._to_sibling(a, j).wait_recv()
                self._store_passed(a, j).start()
        for a in range(self.n):
            for j in range(1, N_CHIP):
                self._ici(a, j).wait_send()
                self._to_sibling(a, j).wait_send()
                self._store_landed(a, j).wait()
                self._store_passed(a, j).wait()
            self._own(a).wait()


def _all_gather(shards):
    n = len(shards)
    any_spec = pl.BlockSpec(memory_space=pl.ANY)

    def body(*refs):
        gather = _BackgroundGather(refs[:n], refs[n:2 * n], refs[2 * n:])
        gather.start()
        gather.forward()
        gather.finish()

    return pl.pallas_call(
        body, name="weights_all_gather",
        in_specs=[any_spec] * n, out_specs=[any_spec] * n,
        out_shape=_BackgroundGather.out_shapes(shards),
        scratch_shapes=_BackgroundGather.scratch_shapes(shards),
        compiler_params=_params(32),
    )(*shards)


def _reduce_scatter(g, core, name):
    n, r, c = g.shape
    hr = r // 2
    once = pl.Buffered(1)
    in_specs = [pl.BlockSpec((n, hr, c), lambda i, core_ref: (0, core_ref[0], 0), pipeline_mode=once),
                pl.BlockSpec((n, hr, c), lambda i, core_ref: (0, 1 - core_ref[0], 0), pipeline_mode=once)]

    def body(core_ref, mine_ref, other_ref, out_ref, from_core, sums_bf, from_chips, done, from_core2, send_sem, recv_sem):
        x, y, cc, k = _place()
        sibling = (x, y, 1 - cc)

        def copy(slot, src, dst, to):
            return pltpu.make_async_remote_copy(src_ref=src, dst_ref=dst, send_sem=send_sem.at[slot],
                                                recv_sem=recv_sem.at[slot], device_id=to, device_id_type=MESH)

        from_sibling = [copy(j, other_ref.at[k ^ j], from_core.at[k ^ j], sibling) for j in range(N_CHIP)]
        for j in (1, 2, 3, 0):
            from_sibling[j].start()
        sends = []
        for j in range(1, N_CHIP):
            from_sibling[j].wait()
            sums_bf[j - 1] = (mine_ref[k ^ j] + from_core[k ^ j]).astype(BF16)
            cp = copy(N_CHIP - 1 + j, sums_bf.at[j - 1], from_chips.at[j - 1], _chip(k ^ j, cc))
            cp.start()
            sends.append(cp)
        from_sibling[0].wait()
        red = mine_ref[k] + from_core[k]
        for j in range(1, N_CHIP):
            sends[j - 1].wait()
            red = red + from_chips[j - 1].astype(F32)
        done[...] = red
        last = copy(2 * N_CHIP - 1, done, from_core2, sibling)
        last.start()
        last.wait()
        row0 = pl.multiple_of(cc * hr, 8)
        row1 = pl.multiple_of((1 - cc) * hr, 8)
        out_ref[pl.ds(row0, hr), :] = red
        out_ref[pl.ds(row1, hr), :] = from_core2[...]

    grid_spec = pltpu.PrefetchScalarGridSpec(
        num_scalar_prefetch=1, grid=(1,), in_specs=in_specs,
        out_specs=pl.BlockSpec((r, c), lambda i, core_ref: (0, 0)),
        scratch_shapes=[pltpu.VMEM((n, hr, c), F32), pltpu.VMEM((N_CHIP - 1, hr, c), BF16),
                        pltpu.VMEM((N_CHIP - 1, hr, c), BF16), pltpu.VMEM((hr, c), F32), pltpu.VMEM((hr, c), F32),
                        pltpu.SemaphoreType.DMA((2 * N_CHIP,)), pltpu.SemaphoreType.DMA((2 * N_CHIP,))])
    return pl.pallas_call(
        body, name=name, grid_spec=grid_spec, out_shape=jax.ShapeDtypeStruct((r, c), F32),
        compiler_params=_params(56),
    )(core, g, g)


def _elementwise(fn, name, ins, n_out, rows):
    total, cols = ins[0].shape
    spec = pl.BlockSpec((rows, cols), lambda i: (i, 0))

    def body(*refs):
        res = fn(*[r[...] for r in refs[:len(ins)]])
        for o, v in zip(refs[len(ins):], res):
            o[...] = v

    return pl.pallas_call(
        body, name=name, grid=(total // rows,),
        in_specs=[spec] * len(ins), out_specs=[spec] * n_out,
        out_shape=[jax.ShapeDtypeStruct((total, cols), F32)] * n_out,
        compiler_params=_params(48),
    )(*ins)


def _adamw(w, g, m, v):
    m = ADAM_B1 * m + (1.0 - ADAM_B1) * g
    v = ADAM_B2 * v + (1.0 - ADAM_B2) * (g * g)
    m_hat = m / (1.0 - ADAM_B1 ** ADAM_STEP)
    v_hat = v / (1.0 - ADAM_B2 ** ADAM_STEP)
    delta = -ADAM_LR * (m_hat / (jnp.sqrt(v_hat) + ADAM_EPS) + ADAM_WD * w)
    return delta, m, v


def _reduce_and_update(grads, weights, moms, vels):
    core = lax.axis_index("c").astype(jnp.int32).reshape(1)
    full = [_reduce_scatter(g, core, f"grads_reduce_scatter_{a}") for a, g in enumerate(grads)]
    out = []
    for a, (g, w, m, v) in enumerate(zip(full, weights, moms, vels)):
        rows = g.shape[0] // 2
        out.append((g,) + tuple(_elementwise(lambda gg, ww, mm, vv: _adamw(ww, gg, mm, vv), f"adamw_{a}", [g, w, m, v], 3, rows)))
    return out


def _reduce_vectors(part, w, m, v):
    n_dev = 8

    def body(p_ref, w_ref, m_ref, v_ref, g_ref, d_ref, nm_ref, nv_ref, buf, send_sem, recv_sem):
        x, y, c, _ = _place()
        me = 4 * x + 2 * y + c
        buf[me] = p_ref[...]
        sends = []
        for off in range(1, n_dev):
            peer = me ^ off
            cp = pltpu.make_async_remote_copy(src_ref=p_ref, dst_ref=buf.at[me], send_sem=send_sem.at[off - 1],
                                              recv_sem=recv_sem.at[off - 1], device_id=(peer >> 2, (peer >> 1) & 1, peer & 1),
                                              device_id_type=MESH)
            cp.start()
            sends.append(cp)
        for off in range(1, n_dev):
            peer = me ^ off
            pltpu.make_async_remote_copy(src_ref=p_ref, dst_ref=buf.at[peer], send_sem=send_sem.at[off - 1],
                                         recv_sem=recv_sem.at[off - 1], device_id=(peer >> 2, (peer >> 1) & 1, peer & 1),
                                         device_id_type=MESH).wait_recv()
        for cp in sends:
            cp.wait_send()
        g = buf[0]
        for d in range(1, n_dev):
            g = g + buf[d]
        g_ref[...] = g
        delta, nm, nv = _adamw(w_ref[...], g, m_ref[...], v_ref[...])
        d_ref[...] = delta
        nm_ref[...] = nm
        nv_ref[...] = nv

    vm = pl.BlockSpec(memory_space=pltpu.VMEM)
    return pl.pallas_call(
        body, name="gains_all_reduce",
        in_specs=[vm] * 4, out_specs=[vm] * 4,
        out_shape=[jax.ShapeDtypeStruct(part.shape, F32)] * 4,
        scratch_shapes=[pltpu.VMEM((n_dev,) + part.shape, F32), pltpu.SemaphoreType.DMA((n_dev - 1,)),
                        pltpu.SemaphoreType.DMA((n_dev - 1,))],
    )(part, w, m, v)


def _pad_row(a):
    a = a.reshape(1, -1)
    return jnp.pad(a, ((0, 0), (0, D_MODEL - a.shape[1])))


def kernel(x, ffn1_norm, ffn1_w_gate, ffn1_w_up, ffn1_w_down, mix_norm, w_in, sb_out_norm, dil_out_norm, w_out, ffn2_norm, ffn2_w_gate, ffn2_w_up, ffn2_w_down, final_norm, loss_target, m_ffn1_norm, m_ffn1_w_gate, m_ffn1_w_up, m_ffn1_w_down, m_mix_norm, m_w_in, m_sb_out_norm, m_dil_out_norm, m_w_out, m_ffn2_norm, m_ffn2_w_gate, m_ffn2_w_up, m_ffn2_w_down, m_final_norm, v_ffn1_norm, v_ffn1_w_gate, v_ffn1_w_up, v_ffn1_w_down, v_mix_norm, v_w_in, v_sb_out_norm, v_dil_out_norm, v_w_out, v_ffn2_norm, v_ffn2_w_gate, v_ffn2_w_up, v_ffn2_w_down, v_final_norm):
    x = x[0]
    target = loss_target[0]
    s = x.shape[0]
    gf = final_norm.reshape(1, D_MODEL)
    cos, sin = _rope_tables(s)

    flip = lambda a: a[0].T
    shard = lambda w: w[0].astype(BF16)
    shard_t = lambda w: flip(w).astype(BF16)
    wg1, wu1, wd1 = _all_gather([shard_t(ffn1_w_gate), shard_t(ffn1_w_up), shard(ffn1_w_down)])

    x1, hm, saved1, (win, wout, wd2) = _ffn1_fwd(x, ffn1_norm, mix_norm, (wg1, wu1), wd1,
                                                 [shard(w_in), shard(w_out), shard(ffn2_w_down)])
    wout = wout.reshape(D_MODEL, D_MODEL)
    qkv, (wg2, wu2) = _proj_fwd(hm, win, cos, sin, [shard_t(ffn2_w_gate), shard_t(ffn2_w_up)])
    o_sb = _sb_fwd(qkv)
    o_dl, lse = _dilated_fwd(qkv)
    x2 = _outproj_fwd(o_sb, o_dl, sb_out_norm, dil_out_norm, x1, wout)
    dx3, st_final, saved2 = _ffn2_fwd_loss(x2, ffn2_norm, gf, target, (wg2, wu2), wd2)

    dx2, dwg2, dwu2, dwd2, st_ffn2 = _ffn_bwd(x2, ffn2_norm, dx3, saved2, (wg2, wu2), wd2, 1)
    do_sb, do_dl, delta_dl, dwout, st_out = _outproj_bwd(dx2, o_sb, o_dl, sb_out_norm, dil_out_norm, wout)
    dqkv_sb = _sb_bwd(qkv, o_sb, do_sb)
    dqkv_dl = _dilated_finish(_dilated_bwd(qkv, delta_dl, lse, do_dl), cos, sin)
    dx1, dwin, st_mix = _proj_bwd(x1, mix_norm, dqkv_sb, dqkv_dl, win, dx2)
    grad_x, dwg1, dwu1, dwd1, st_ffn1 = _ffn_bwd(x, ffn1_norm, dx1, saved1, (wg1, wu1), wd1, 0)

    names = ["ffn1_w_gate", "ffn1_w_up", "ffn1_w_down", "w_in", "w_out", "ffn2_w_gate", "ffn2_w_up", "ffn2_w_down"]
    grads = [dwg1, dwu1, dwd1, dwin, dwout.reshape(N_CHIP, OUTB, D_MODEL), dwg2, dwu2, dwd2]
    flipped = {"ffn1_w_gate", "ffn1_w_up", "ffn2_w_gate", "ffn2_w_up"}
    place = lambda n, a: flip(a) if n in flipped else a[0]
    weights = [place(n, a) for n, a in zip(names, [ffn1_w_gate, ffn1_w_up, ffn1_w_down, w_in, w_out, ffn2_w_gate, ffn2_w_up, ffn2_w_down])]
    moms = [place(n, a) for n, a in zip(names, [m_ffn1_w_gate, m_ffn1_w_up, m_ffn1_w_down, m_w_in, m_w_out, m_ffn2_w_gate, m_ffn2_w_up, m_ffn2_w_down])]
    vels = [place(n, a) for n, a in zip(names, [v_ffn1_w_gate, v_ffn1_w_up, v_ffn1_w_down, v_w_in, v_w_out, v_ffn2_w_gate, v_ffn2_w_up, v_ffn2_w_down])]
    mats = {n: tuple((t.T if n in flipped else t)[None] for t in r)
            for n, r in zip(names, _reduce_and_update(grads, weights, moms, vels))}

    vec_names = ["ffn1_norm", "mix_norm", "sb_out_norm", "dil_out_norm", "ffn2_norm", "final_norm"]
    part = jnp.concatenate([st_ffn1[0:1], st_mix[0:1], _pad_row(st_out[0]), _pad_row(st_out[1]), st_ffn2[0:1],
                            st_final[0:1], st_final[1:2], jnp.zeros((1, D_MODEL), F32)], axis=0)
    pack = lambda arrs: jnp.concatenate([_pad_row(a) for a in arrs] + [jnp.zeros((2, D_MODEL), F32)], axis=0)
    g_vec, d_vec, m_vec, v_vec = _reduce_vectors(
        part,
        pack([ffn1_norm, mix_norm, sb_out_norm, dil_out_norm, ffn2_norm, final_norm]),
        pack([m_ffn1_norm, m_mix_norm, m_sb_out_norm, m_dil_out_norm, m_ffn2_norm, m_final_norm]),
        pack([v_ffn1_norm, v_mix_norm, v_sb_out_norm, v_dil_out_norm, v_ffn2_norm, v_final_norm]))
    like = {"ffn1_norm": ffn1_norm, "mix_norm": mix_norm, "sb_out_norm": sb_out_norm, "dil_out_norm": dil_out_norm,
            "ffn2_norm": ffn2_norm, "final_norm": final_norm}
    vecs = {n: tuple(t[i, :like[n].size].reshape(like[n].shape) for t in (g_vec, d_vec, m_vec, v_vec))
            for i, n in enumerate(vec_names)}
    loss = 0.5 * jnp.sum(g_vec[6]) / D_MODEL

    order = ["ffn1_norm", "ffn1_w_gate", "ffn1_w_up", "ffn1_w_down", "mix_norm", "w_in", "sb_out_norm", "dil_out_norm",
             "w_out", "ffn2_norm", "ffn2_w_gate", "ffn2_w_up", "ffn2_w_down", "final_norm"]
    both = {**mats, **vecs}
    return (loss, grad_x[None], *[both[n][0] for n in order], *[both[n][1] for n in order],
            *[both[n][2] for n in order], *[both[n][3] for n in order])
```

```python
import functools

import jax
import jax.numpy as jnp
from jax import lax
from jax.experimental import pallas as pl
from jax.experimental.pallas import tpu as pltpu

D_MODEL = 1024
D_FF = 2816
HEAD_DIM = 64
D_SB = 512
D_DIL = 512
D_IN = 3072
N_CHIP = 4
FFB = D_FF // N_CHIP
INB = D_IN // N_CHIP
OUTB = D_MODEL // N_CHIP
BLK = 128
LANES = 128
DILATIONS = (1, 4, 16)
ROPE_THETA = 10000.0
RMS_EPS = 1e-6
SCALE = HEAD_DIM ** -0.5
NEG = -1e30
DEAD = -104.0
ADAM_LR = 0.001
ADAM_B1 = 0.9
ADAM_B2 = 0.999
ADAM_EPS = 1e-08
ADAM_WD = 0.01
ADAM_STEP = 10
MESH = pl.DeviceIdType.MESH
F32 = jnp.float32
BF16 = jnp.bfloat16
TM = 512


def _params(vmem_mb):
    return pltpu.CompilerParams(vmem_limit_bytes=vmem_mb << 20)


def _dot(a, b):
    return jnp.dot(a, b, preferred_element_type=F32)


def _dot_nt(a, b):
    return lax.dot_general(a, b, (((1,), (1,)), ((), ())), preferred_element_type=F32)


def _dot_tn(a, b):
    return lax.dot_general(a, b, (((0,), (0,)), ((), ())), preferred_element_type=F32)


def _rms_fwd(x, g):
    r = lax.rsqrt(jnp.mean(x * x, axis=-1, keepdims=True) + RMS_EPS)
    xh = x * r
    return xh * g, xh, r


def _rms_bwd(dy, xh, r, g):
    dyg = dy * g
    dx = r * (dyg - xh * jnp.mean(dyg * xh, axis=-1, keepdims=True))
    return dx, jnp.sum(dy * xh, axis=0, keepdims=True)


def _split_bf16(a):
    hi = a.astype(BF16)
    return hi, (a - hi.astype(F32)).astype(BF16)


def _dot_split(a, b2):
    hi, lo = _split_bf16(a)
    return _dot(jnp.concatenate([hi, lo], axis=1), b2)


def _ffn_weight_specs():
    return [pl.BlockSpec((None, FFB, D_MODEL), lambda i, j: (j, 0, 0))] * 3


def _ffn_saved(s):
    hidden = jax.ShapeDtypeStruct((N_CHIP, s, FFB), BF16)
    hid = pl.BlockSpec((None, TM, FFB), lambda i, j: (j, i, 0))
    row = pl.BlockSpec((TM, D_MODEL), lambda i, j: (i, 0))
    return [row, hid, hid, hid], [jax.ShapeDtypeStruct((s, D_MODEL), BF16), hidden, hidden, hidden]


def _ffn_accumulate(h_ref, acc_scr, wg_ref, wu_ref, wd_ref, a_ref, b_ref, act_ref):
    h = h_ref[...]
    a = _dot_nt(h, wg_ref[...])
    b = _dot_nt(h, wu_ref[...])
    act = ((a * jax.nn.sigmoid(a)) * b).astype(BF16)
    a_ref[...] = a.astype(BF16)
    b_ref[...] = b.astype(BF16)
    act_ref[...] = act
    acc_scr[...] += _dot(act, wd_ref[...])


def _host_gather_before(gather, i, j, steps):
    @pl.when((i == 0) & (j == 0))
    def _():
        gather.start()

    @pl.when((i == (3 * steps) // 4) & (j == 0))
    def _():
        gather.forward()


def _host_gather_after(gather, i, j, steps):
    @pl.when((i == steps - 1) & (j == N_CHIP - 1))
    def _():
        gather.finish()


def _ffn1_fwd(x, g1, gmix, gu, wd, later_shards):
    s = x.shape[0]
    row = pl.BlockSpec((TM, D_MODEL), lambda i, j: (i, 0))
    vec = pl.BlockSpec((1, D_MODEL), lambda i, j: (0, 0))
    saved_specs, saved_shapes = _ffn_saved(s)
    n = len(later_shards)
    any_spec = pl.BlockSpec(memory_space=pl.ANY)

    def body(*refs):
        x_ref, g_ref, gm_ref, wg_ref, wu_ref, wd_ref = refs[:6]
        shard_refs, refs = refs[6:6 + n], refs[6 + n:]
        x1_ref, hm_ref, h_ref, a_ref, b_ref, act_ref = refs[:6]
        gathered_refs, acc_scr, gather_scratch = refs[6:6 + n], refs[6 + n], refs[7 + n:]
        gather = _BackgroundGather(shard_refs, gathered_refs, gather_scratch)
        i, j = pl.program_id(0), pl.program_id(1)
        _host_gather_before(gather, i, j, s // TM)

        @pl.when(j == 0)
        def _():
            h, _, _ = _rms_fwd(x_ref[...], g_ref[...])
            h_ref[...] = h.astype(BF16)
            acc_scr[...] = jnp.zeros_like(acc_scr)

        _ffn_accumulate(h_ref, acc_scr, wg_ref, wu_ref, wd_ref, a_ref, b_ref, act_ref)

        @pl.when(j == N_CHIP - 1)
        def _():
            x1 = x_ref[...] + 0.5 * acc_scr[...]
            x1_ref[...] = x1
            hm, _, _ = _rms_fwd(x1, gm_ref[...])
            hm_ref[...] = hm.astype(BF16)

        _host_gather_after(gather, i, j, s // TM)

    x1, hm, h, a, b, act, *gathered = pl.pallas_call(
        body, name="ffn1_fwd", grid=(s // TM, N_CHIP),
        in_specs=[row, vec, vec] + _ffn_weight_specs() + [any_spec] * n,
        out_specs=[row, row] + saved_specs + [any_spec] * n,
        out_shape=([jax.ShapeDtypeStruct((s, D_MODEL), F32), jax.ShapeDtypeStruct((s, D_MODEL), BF16)] + saved_shapes
                   + _BackgroundGather.out_shapes(later_shards)),
        scratch_shapes=[pltpu.VMEM((TM, D_MODEL), F32)] + _BackgroundGather.scratch_shapes(later_shards),
        compiler_params=_params(58),
    )(x, g1, gmix, gu[0], gu[1], wd, *later_shards)
    return x1, hm, [h, a, b, act], gathered


def _ffn2_fwd_loss(x2, g2, gf, target, gu, wd):
    s = x2.shape[0]
    row = pl.BlockSpec((TM, D_MODEL), lambda i, j: (i, 0))
    vec = pl.BlockSpec((1, D_MODEL), lambda i, j: (0, 0))
    stat = pl.BlockSpec((8, D_MODEL), lambda i, j: (0, 0))
    next_shard = pl.BlockSpec((None, FFB, D_MODEL), lambda i, j: ((j + 1) % N_CHIP, 0, 0))
    saved_specs, saved_shapes = _ffn_saved(s)

    def body(x_ref, g_ref, gf_ref, t_ref, wg_ref, wu_ref, wd_ref, wg_next_ref, wu_next_ref,
             dx_ref, st_ref, h_ref, a_ref, b_ref, act_ref, acc_scr, gate_scr, up_scr):
        i, j = pl.program_id(0), pl.program_id(1)

        @pl.when((i == 0) & (j == 0))
        def _():
            st_ref[...] = jnp.zeros_like(st_ref)

        @pl.when(j == 0)
        def _():
            h, _, _ = _rms_fwd(x_ref[...], g_ref[...])
            h_ref[...] = h.astype(BF16)
            acc_scr[...] = jnp.zeros_like(acc_scr)
            gate_scr[...] = _dot_nt(h_ref[...], wg_ref[...])
            up_scr[...] = _dot_nt(h_ref[...], wu_ref[...])

        a, b = gate_scr[...], up_scr[...]
        act = ((a * jax.nn.sigmoid(a)) * b).astype(BF16)
        a_ref[...] = a.astype(BF16)
        b_ref[...] = b.astype(BF16)
        act_ref[...] = act
        gate_scr[...] = _dot_nt(h_ref[...], wg_next_ref[...])
        up_scr[...] = _dot_nt(h_ref[...], wu_next_ref[...])
        acc_scr[...] += _dot(act, wd_ref[...])

        @pl.when(j == N_CHIP - 1)
        def _():
            x3 = x_ref[...] + 0.5 * acc_scr[...]
            y, xh, r = _rms_fwd(x3, gf_ref[...])
            err = y - t_ref[...]
            dx, dg = _rms_bwd(err * (1.0 / D_MODEL), xh, r, gf_ref[...])
            dx_ref[...] = dx
            st_ref[0:1, :] += dg
            st_ref[1:2, :] += jnp.sum(err * err, axis=0, keepdims=True)

    dx3, st, *saved = pl.pallas_call(
        body, name="ffn2_fwd_loss", grid=(s // TM, N_CHIP),
        in_specs=[row, vec, vec, row] + _ffn_weight_specs() + [next_shard, next_shard],
        out_specs=[row, stat] + saved_specs,
        out_shape=[jax.ShapeDtypeStruct((s, D_MODEL), F32), jax.ShapeDtypeStruct((8, D_MODEL), F32)] + saved_shapes,
        scratch_shapes=[pltpu.VMEM((TM, D_MODEL), F32), pltpu.VMEM((TM, FFB), F32), pltpu.VMEM((TM, FFB), F32)],
        compiler_params=_params(58),
    )(x2, g2, gf, target, gu[0], gu[1], wd, gu[0], gu[1])
    return dx3, st, saved


def _ffn_bwd(xin, g, dy, saved, gu, wd, f):
    s = xin.shape[0]
    hb, gate, up, act = saved
    row = pl.BlockSpec((TM, D_MODEL), lambda i, j: (i, 0))
    vec = pl.BlockSpec((1, D_MODEL), lambda i, j: (0, 0))
    stat = pl.BlockSpec((8, D_MODEL), lambda i, j: (0, 0))
    hid = pl.BlockSpec((None, TM, FFB), lambda i, j: (j, i, 0))

    def body(x_ref, g_ref, dy_ref, a_ref, b_ref, wg_ref, wu_ref, wd_ref, wd_next_ref,
             out_ref, dyh_ref, da_ref, db_ref, st_ref, dh_scr, dact_scr):
        i, j = pl.program_id(0), pl.program_id(1)

        @pl.when((i == 0) & (j == 0))
        def _():
            st_ref[...] = jnp.zeros_like(st_ref)

        @pl.when(j == 0)
        def _():
            dyh_ref[...] = (0.5 * dy_ref[...]).astype(BF16)
            dh_scr[...] = jnp.zeros_like(dh_scr)
            dact_scr[...] = _dot_nt(dyh_ref[...], wd_ref[...])

        a = a_ref[...].astype(F32)
        b = b_ref[...].astype(F32)
        sg = jax.nn.sigmoid(a)
        dact = dact_scr[...]
        dab = (dact * b * (sg * (1.0 + a * (1.0 - sg)))).astype(BF16)
        dbb = (dact * (a * sg)).astype(BF16)
        da_ref[...] = dab
        db_ref[...] = dbb
        dact_scr[...] = _dot_nt(dyh_ref[...], wd_next_ref[...])
        dh_scr[...] += _dot(dab, wg_ref[...]) + _dot(dbb, wu_ref[...])

        @pl.when(j == N_CHIP - 1)
        def _():
            _, xh, r = _rms_fwd(x_ref[...], g_ref[...])
            dx, dg = _rms_bwd(dh_scr[...], xh, r, g_ref[...])
            out_ref[...] = dy_ref[...] + dx
            st_ref[0:1, :] += dg

    hidden = jax.ShapeDtypeStruct((N_CHIP, s, FFB), BF16)
    dx, dyh, da, db, st = pl.pallas_call(
        body, name=f"ffn{f + 1}_bwd_dx", grid=(s // TM, N_CHIP),
        in_specs=([row, vec, row, hid, hid] + _ffn_weight_specs()
                  + [pl.BlockSpec((None, FFB, D_MODEL), lambda i, j: ((j + 1) % N_CHIP, 0, 0))]),
        out_specs=[row, row, hid, hid, stat],
        out_shape=[jax.ShapeDtypeStruct((s, D_MODEL), F32), jax.ShapeDtypeStruct((s, D_MODEL), BF16),
                   hidden, hidden, jax.ShapeDtypeStruct((8, D_MODEL), F32)],
        scratch_shapes=[pltpu.VMEM((TM, D_MODEL), F32), pltpu.VMEM((TM, FFB), F32)],
        compiler_params=_params(58),
    )(xin, g, dy, gate, up, gu[0], gu[1], wd, wd)

    tk = 4 * TM
    tok = pl.BlockSpec((tk, D_MODEL), lambda j, i: (i, 0))
    hid2 = pl.BlockSpec((None, tk, FFB), lambda j, i: (j, i, 0))
    gspecs = [pl.BlockSpec((None, FFB, D_MODEL), lambda j, i: (j, 0, 0))] * 3

    def wbody(h_ref, dyh_ref, da_ref, db_ref, act_ref, dwg_ref, dwu_ref, dwd_ref):
        @pl.when(pl.program_id(1) == 0)
        def _():
            dwg_ref[...] = jnp.zeros_like(dwg_ref)
            dwu_ref[...] = jnp.zeros_like(dwu_ref)
            dwd_ref[...] = jnp.zeros_like(dwd_ref)

        hb = h_ref[...]
        dwg_ref[...] += _dot_tn(da_ref[...], hb)
        dwu_ref[...] += _dot_tn(db_ref[...], hb)
        dwd_ref[...] += _dot_tn(act_ref[...], dyh_ref[...])

    dwg, dwu, dwd = pl.pallas_call(
        wbody, name=f"ffn{f + 1}_bwd_dw", grid=(N_CHIP, s // tk),
        in_specs=[tok, tok, hid2, hid2, hid2], out_specs=gspecs,
        out_shape=[jax.ShapeDtypeStruct((N_CHIP, FFB, D_MODEL), F32)] * 3,
        compiler_params=_params(60),
    )(hb, dyh, da, db, act)
    return dx, dwg, dwu, dwd, st


def _rope_tables(s):
    half = HEAD_DIM // 2
    inv_freq = ROPE_THETA ** (-jnp.arange(half, dtype=F32) / half)
    ang = jnp.arange(s).astype(F32)[:, None] * inv_freq[None, :]
    cos, sin = jnp.cos(ang), jnp.sin(ang)
    cos2 = jnp.concatenate([cos, cos], axis=-1)
    sin2 = jnp.concatenate([-sin, sin], axis=-1)
    return jnp.tile(cos2, (1, LANES // HEAD_DIM)), jnp.tile(sin2, (1, LANES // HEAD_DIM))


def _rotate(t, cos, sin_signed):
    lane = lax.broadcasted_iota(jnp.int32, t.shape, 1)
    first = (lane % HEAD_DIM) < (HEAD_DIM // 2)
    partner = jnp.where(first, pltpu.roll(t, LANES - HEAD_DIM // 2, 1), pltpu.roll(t, HEAD_DIM // 2, 1))
    return t * cos + partner * sin_signed


def _proj_fwd(hm, win, cos, sin, later_shards):
    s = hm.shape[0]
    tm = 2 * TM
    n_sub = INB // LANES
    first_rot, last_rot = (3 * D_SB) // LANES, (3 * D_SB + 2 * D_DIL) // LANES
    n = len(later_shards)
    any_spec = pl.BlockSpec(memory_space=pl.ANY)

    def body(*refs):
        h_ref, w_ref, c_ref, s_ref = refs[:4]
        shard_refs, o_ref, gathered_refs, gather_scratch = refs[4:4 + n], refs[4 + n], refs[5 + n:5 + 2 * n], refs[5 + 2 * n:]
        gather = _BackgroundGather(shard_refs, gathered_refs, gather_scratch)
        i, j = pl.program_id(0), pl.program_id(1)
        _host_gather_before(gather, i, j, s // tm)
        r = _dot(h_ref[...], w_ref[...])
        for c in range(n_sub):
            t = r[:, c * LANES:(c + 1) * LANES]
            col = j * n_sub + c
            rot = (col >= first_rot) & (col < last_rot)
            lanes = slice(c * LANES, (c + 1) * LANES)

            @pl.when(rot)
            def _():
                o_ref[:, lanes] = _rotate(t, c_ref[...], s_ref[...]).astype(BF16)

            @pl.when(jnp.logical_not(rot))
            def _():
                o_ref[:, lanes] = t.astype(BF16)

        _host_gather_after(gather, i, j, s // tm)

    qkv, *gathered = pl.pallas_call(
        body, name="proj_fwd", grid=(s // tm, N_CHIP),
        in_specs=[pl.BlockSpec((tm, D_MODEL), lambda i, j: (i, 0)),
                  pl.BlockSpec((None, D_MODEL, INB), lambda i, j: (j, 0, 0)),
                  pl.BlockSpec((tm, LANES), lambda i, j: (i, 0)),
                  pl.BlockSpec((tm, LANES), lambda i, j: (i, 0))] + [any_spec] * n,
        out_specs=[pl.BlockSpec((tm, INB), lambda i, j: (i, j))] + [any_spec] * n,
        out_shape=[jax.ShapeDtypeStruct((s, D_IN), BF16)] + _BackgroundGather.out_shapes(later_shards),
        scratch_shapes=_BackgroundGather.scratch_shapes(later_shards),
        compiler_params=_params(48),
    )(hm, win, cos, sin, *later_shards)
    return qkv, gathered


def _proj_bwd(x1, gmix, dqkv_sb, dqkv_dl, win, dx2):
    s = x1.shape[0]
    row = pl.BlockSpec((TM, D_MODEL), lambda i, j: (i, 0))
    vec = pl.BlockSpec((1, D_MODEL), lambda i, j: (0, 0))
    per_group = N_CHIP // 2

    def body(x_ref, g_ref, dsb_ref, ddl_ref, w_ref, dx2_ref, out_ref, dw_ref, st_ref, h_scr, dh_scr, dq_ref):
        i, j = pl.program_id(0), pl.program_id(1)

        @pl.when(j < per_group)
        def _():
            dq_ref[...] = dsb_ref[...]

        @pl.when(j >= per_group)
        def _():
            dq_ref[...] = ddl_ref[...]

        @pl.when((i == 0) & (j == 0))
        def _():
            st_ref[...] = jnp.zeros_like(st_ref)
            dw_ref[...] = jnp.zeros_like(dw_ref)

        @pl.when(j == 0)
        def _():
            h, _, _ = _rms_fwd(x_ref[...], g_ref[...])
            h_scr[...] = h.astype(BF16)
            dh_scr[...] = jnp.zeros_like(dh_scr)

        dq = dq_ref[...]
        dw_ref[j] += _dot_tn(h_scr[...], dq)
        dh_scr[...] += _dot_nt(dq, w_ref[...])

        @pl.when(j == N_CHIP - 1)
        def _():
            _, xh, r = _rms_fwd(x_ref[...], g_ref[...])
            dx, dg = _rms_bwd(dh_scr[...], xh, r, g_ref[...])
            out_ref[...] = dx2_ref[...] + dx
            st_ref[0:1, :] += dg

    return pl.pallas_call(
        body, name="proj_bwd", grid=(s // TM, N_CHIP),
        in_specs=[row, vec,
                  pl.BlockSpec((TM, INB), lambda i, j: (i, jnp.minimum(j, per_group - 1))),
                  pl.BlockSpec((TM, INB), lambda i, j: (i, jnp.maximum(j - per_group, 0))),
                  pl.BlockSpec((None, D_MODEL, INB), lambda i, j: (j, 0, 0)), row],
        out_specs=[row, pl.BlockSpec((N_CHIP, D_MODEL, INB), lambda i, j: (0, 0, 0)),
                   pl.BlockSpec((8, D_MODEL), lambda i, j: (0, 0))],
        out_shape=[jax.ShapeDtypeStruct((s, D_MODEL), F32),
                   jax.ShapeDtypeStruct((N_CHIP, D_MODEL, INB), F32),
                   jax.ShapeDtypeStruct((8, D_MODEL), F32)],
        scratch_shapes=[pltpu.VMEM((TM, D_MODEL), BF16), pltpu.VMEM((TM, D_MODEL), F32), pltpu.VMEM((TM, INB), BF16)],
        compiler_params=_params(56),
    )(x1, gmix, dqkv_sb, dqkv_dl, win, dx2)


def _outproj_fwd(o_sb, o_dl, g_sb, g_dl, x1, wout):
    s = x1.shape[0]
    tm = 2 * TM
    half = pl.BlockSpec((tm, D_SB), lambda i: (i, 0))
    row = pl.BlockSpec((tm, D_MODEL), lambda i: (i, 0))
    vec = pl.BlockSpec((1, D_SB), lambda i: (0, 0))

    def body(a_ref, b_ref, ga_ref, gb_ref, x_ref, w_ref, o_ref):
        ma, _, _ = _rms_fwd(a_ref[...], ga_ref[...])
        mb, _, _ = _rms_fwd(b_ref[...], gb_ref[...])
        o_ref[...] = (x_ref[...] + _dot(ma.astype(BF16), w_ref[0:D_SB, :])
                      + _dot(mb.astype(BF16), w_ref[D_SB:D_MODEL, :]))

    return pl.pallas_call(
        body, name="outproj_fwd", grid=(s // tm,),
        in_specs=[half, half, vec, vec, row, pl.BlockSpec((D_MODEL, D_MODEL), lambda i: (0, 0))],
        out_specs=row, out_shape=jax.ShapeDtypeStruct((s, D_MODEL), F32),
        compiler_params=_params(32),
    )(o_sb, o_dl, g_sb, g_dl, x1, wout)


def _outproj_bwd(dx2, o_sb, o_dl, g_sb, g_dl, wout):
    s = dx2.shape[0]
    tm = 2 * TM
    half = pl.BlockSpec((tm, D_SB), lambda i: (i, 0))
    row = pl.BlockSpec((tm, D_MODEL), lambda i: (i, 0))
    vec = pl.BlockSpec((1, D_SB), lambda i: (0, 0))
    full = pl.BlockSpec((D_MODEL, D_MODEL), lambda i: (0, 0))

    def body(dy_ref, a_ref, b_ref, ga_ref, gb_ref, w_ref, da_ref, db_ref, dl_ref, dw_ref, st_ref):
        @pl.when(pl.program_id(0) == 0)
        def _():
            dw_ref[...] = jnp.zeros_like(dw_ref)
            st_ref[...] = jnp.zeros_like(st_ref)

        dy = dy_ref[...].astype(BF16)
        dm = _dot_nt(dy, w_ref[...])
        ma, xa, ra = _rms_fwd(a_ref[...], ga_ref[...])
        mb, xb, rb = _rms_fwd(b_ref[...], gb_ref[...])
        dw_ref[0:D_SB, :] += _dot_tn(ma.astype(BF16), dy)
        dw_ref[D_SB:D_MODEL, :] += _dot_tn(mb.astype(BF16), dy)
        da, dga = _rms_bwd(dm[:, 0:D_SB], xa, ra, ga_ref[...])
        db, dgb = _rms_bwd(dm[:, D_SB:D_MODEL], xb, rb, gb_ref[...])
        da_ref[...] = da
        db_ref[...] = db
        r = lax.broadcasted_iota(jnp.int32, (LANES, LANES), 0) >= HEAD_DIM
        c = lax.broadcasted_iota(jnp.int32, (LANES, LANES), 1) >= HEAD_DIM
        same_head = jnp.where(r == c, 1.0, 0.0).astype(BF16)
        same_head = jnp.concatenate([same_head, same_head], axis=0)
        prod = db * b_ref[...]
        for k in range(D_DIL // LANES):
            lanes = slice(k * LANES, (k + 1) * LANES)
            dl_ref[:, lanes] = _dot_split(prod[:, lanes], same_head)
        st_ref[0:1, :] += dga
        st_ref[1:2, :] += dgb

    return pl.pallas_call(
        body, name="outproj_bwd", grid=(s // tm,),
        in_specs=[row, half, half, vec, vec, full],
        out_specs=[half, half, half, full, pl.BlockSpec((8, D_SB), lambda i: (0, 0))],
        out_shape=[jax.ShapeDtypeStruct((s, D_SB), F32), jax.ShapeDtypeStruct((s, D_SB), F32),
                   jax.ShapeDtypeStruct((s, D_DIL), F32),
                   jax.ShapeDtypeStruct((D_MODEL, D_MODEL), F32), jax.ShapeDtypeStruct((8, D_SB), F32)],
        compiler_params=_params(48),
    )(dx2, o_sb, o_dl, g_sb, g_dl, wout)


def _head_masks():
    lane = lax.broadcasted_iota(jnp.int32, (BLK, LANES), 1)
    return [lane < HEAD_DIM, lane >= HEAD_DIM]


def _keep(mask, a):
    return a * jnp.where(mask, 1.0, 0.0).astype(a.dtype)


def _suffix_matrices():
    r = lax.broadcasted_iota(jnp.int32, (2 * BLK, BLK), 0) & (BLK - 1)
    c = lax.broadcasted_iota(jnp.int32, (2 * BLK, BLK), 1)
    ones = jnp.ones((2 * BLK, BLK), BF16)
    excl = jnp.concatenate([(r > c).astype(BF16), ones], axis=1)
    incl = jnp.concatenate([(r >= c).astype(BF16), ones], axis=1)
    return excl, incl


def _blk(i):
    return pl.ds(pl.multiple_of(i * BLK, BLK), BLK)


def _alive(carry_m):
    return (jnp.max(carry_m) > DEAD).astype(jnp.int32)


def _more_keys(last, carry):
    return (carry[0] * SB_KB <= last) & (carry[1] > 0)


def _stack_heads(a):
    masks = _head_masks()
    return jnp.concatenate([_keep(masks[0], a), _keep(masks[1], a)], axis=0)


def _unstack_heads(a2):
    return jnp.where(_head_masks()[0], a2[:BLK], a2[BLK:])


def _head_rowsum(a):
    masks = _head_masks()
    return jnp.concatenate([jnp.sum(jnp.where(m, a, 0.0), axis=1, keepdims=True) for m in masks], axis=0)


SB_QB = 2
SB_ROWS = SB_QB * 2 * BLK
SB_KB = 4
PAST_START = 1 << 30


def _sb_rows(ref, i0, cast=None):
    tiles = [ref[_blk(i0 + t), :] for t in range(SB_QB)]
    return jnp.concatenate([_stack_heads(t if cast is None else t.astype(cast)) for t in tiles], axis=0)


_SB_LATER_ROWS = (SB_QB - 1) * 2 * BLK


def _put_rows(full, rows, part):
    return part if rows.start == 0 else jnp.concatenate([full[:rows.start], part], axis=0)


def _sb_scores(q2, k, i, j, carry_m, u_excl):
    r = lax.broadcasted_iota(jnp.int32, (q2.shape[0], BLK), 0)
    row = (r & (BLK - 1)) + ((r >> 8) << 7)
    col = lax.broadcasted_iota(jnp.int32, (q2.shape[0], BLK), 1)
    valid = (jnp.where(j >= 0, j * BLK, PAST_START) + col) < (i * BLK + row)
    z = _dot_nt(q2, k) * SCALE
    sp = jnp.maximum(z, 0.0) + jnp.log(1.0 + jnp.exp(-jnp.abs(z)))
    log_stay = jnp.where(valid, -sp, 0.0)
    log_beta = z - sp
    sums = _dot_split(log_stay, u_excl)
    later = carry_m + sums[:, :BLK]
    w = jnp.where(valid, jnp.exp(log_beta + later), 0.0)
    return valid, log_beta, w, carry_m + sums[:, BLK:]


def _sb_fwd(qkv):
    s = qkv.shape[0]
    nq = s // BLK
    pairs = D_SB // LANES
    col = lambda off: pl.BlockSpec((s, LANES), lambda p: (0, off + p))

    def body(q_ref, k_ref, v_ref, o_ref):
        u_excl, _ = _suffix_matrices()
        zero = jnp.zeros((SB_ROWS, LANES), F32)

        def q_block(ib, _):
            i = ib * SB_QB
            last = i + SB_QB - 1
            q2 = _sb_rows(q_ref, i)

            def trip(jj, carry_m, acc, first):
                for t in range(SB_KB):
                    j = last - jj * SB_KB - t
                    at = _blk(jnp.maximum(j, 0))
                    rows = slice(_SB_LATER_ROWS, SB_ROWS) if first and t == 0 else slice(0, SB_ROWS)
                    base = i + rows.start // (2 * BLK)
                    _, _, w, part = _sb_scores(q2[rows], k_ref[at, :], base, j, carry_m[rows], u_excl)
                    carry_m = _put_rows(carry_m, rows, part)
                    acc = _put_rows(acc, rows, acc[rows] + _dot(w.astype(BF16), v_ref[at, :]))
                return carry_m, acc

            def k_block(carry):
                carry_m, acc = trip(carry[0], carry[2], carry[3], False)
                return carry[0] + 1, _alive(carry_m), carry_m, acc

            carry_m, acc = trip(0, zero, zero, True)
            _, _, _, acc = lax.while_loop(functools.partial(_more_keys, last), k_block,
                                          (jnp.int32(1), _alive(carry_m), carry_m, acc))
            for t in range(SB_QB):
                o_ref[_blk(i + t), :] = _unstack_heads(acc[2 * BLK * t:2 * BLK * (t + 1)])
            return 0

        lax.fori_loop(0, nq // SB_QB, q_block, 0)

    return pl.pallas_call(
        body, name="sb_fwd", grid=(pairs,),
        in_specs=[col(0), col(pairs), col(2 * pairs)],
        out_specs=pl.BlockSpec((s, LANES), lambda p: (0, p)),
        out_shape=jax.ShapeDtypeStruct((s, D_SB), F32),
        compiler_params=_params(48),
    )(qkv, qkv, qkv)


def _sb_bwd(qkv, o_sb, do_sb):
    s = qkv.shape[0]
    nq = s // BLK
    pairs = D_SB // LANES
    col = lambda off: pl.BlockSpec((s, LANES), lambda p, w: (0, off + p))
    own = pl.BlockSpec((s, LANES), lambda p, w: (0, p))

    def body(q_ref, k_ref, v_ref, o_ref, do_ref, out_ref, dq_acc, dk_acc, dv_acc):
        which = pl.program_id(1)

        @pl.when(which == 0)
        def _():
            walk(q_ref, k_ref, v_ref, o_ref, do_ref, dq_acc, dk_acc, dv_acc)
            out_ref[...] = dq_acc[...]

        @pl.when(which == 1)
        def _():
            out_ref[...] = dk_acc[...].astype(BF16)

        @pl.when(which == 2)
        def _():
            out_ref[...] = dv_acc[...].astype(BF16)

    def walk(q_ref, k_ref, v_ref, o_ref, do_ref, dq_ref, dk_acc, dv_acc):
        u_excl, u_incl = _suffix_matrices()
        zero = jnp.zeros((SB_ROWS, LANES), F32)
        dk_acc[...] = jnp.zeros_like(dk_acc)
        dv_acc[...] = jnp.zeros_like(dv_acc)

        def q_block(ib, _):
            i = ib * SB_QB
            last = i + SB_QB - 1
            q2 = _sb_rows(q_ref, i)
            do2 = _sb_rows(do_ref, i, BF16)
            totals = [_head_rowsum(do_ref[_blk(i + t), :].astype(BF16).astype(F32) * o_ref[_blk(i + t), :])
                      for t in range(SB_QB)]
            total = jnp.broadcast_to(jnp.concatenate(totals, axis=0), (SB_ROWS, BLK))

            def trip(jj, carry_m, carry_g, dq, first):
                for t in range(SB_KB):
                    j = last - jj * SB_KB - t
                    at = _blk(jnp.maximum(j, 0))
                    k = k_ref[at, :]
                    rows = slice(_SB_LATER_ROWS, SB_ROWS) if first and t == 0 else slice(0, SB_ROWS)
                    base = i + rows.start // (2 * BLK)
                    valid, log_beta, w, part_m = _sb_scores(q2[rows], k, base, j, carry_m[rows], u_excl)
                    wb = w.astype(BF16)
                    g = wb.astype(F32) * _dot_nt(do2[rows], v_ref[at, :])
                    sums = _dot_split(g, u_incl)
                    before = total[rows] - (carry_g[rows] + sums[:, :BLK])
                    dz = jnp.where(valid, g - jnp.exp(log_beta) * (g + before), 0.0)
                    dzb = (dz * SCALE).astype(BF16)
                    dk_acc[at, :] += _dot_tn(dzb, q2[rows])
                    dv_acc[at, :] += _dot_tn(wb, do2[rows])
                    carry_m = _put_rows(carry_m, rows, part_m)
                    carry_g = _put_rows(carry_g, rows, carry_g[rows] + sums[:, BLK:])
                    dq = _put_rows(dq, rows, dq[rows] + _dot(dzb, k))
                return carry_m, carry_g, dq

            def k_block(carry):
                carry_m, carry_g, dq = trip(carry[0], carry[2], carry[3], carry[4], False)
                return carry[0] + 1, _alive(carry_m), carry_m, carry_g, dq

            carry_m, carry_g, dq = trip(0, zero, zero, zero, True)
            _, _, _, _, dq = lax.while_loop(functools.partial(_more_keys, last), k_block,
                                            (jnp.int32(1), _alive(carry_m), carry_m, carry_g, dq))
            for t in range(SB_QB):
                dq_ref[_blk(i + t), :] = _unstack_heads(dq[2 * BLK * t:2 * BLK * (t + 1)]).astype(BF16)
            return 0

        lax.fori_loop(0, nq // SB_QB, q_block, 0)

    return pl.pallas_call(
        body, name="sb_bwd", grid=(pairs, 3),
        in_specs=[col(0), col(pairs), col(2 * pairs), own, own],
        out_specs=pl.BlockSpec((s, LANES), lambda p, w: (0, w * pairs + p)),
        out_shape=jax.ShapeDtypeStruct((s, 3 * D_SB), BF16),
        scratch_shapes=[pltpu.VMEM((s, LANES), BF16), pltpu.VMEM((s, LANES), F32), pltpu.VMEM((s, LANES), F32)],
        compiler_params=_params(58),
    )(qkv, qkv, qkv, o_sb, do_sb)


DIL_UNROLL = 16


def _band_masks(b):
    row = lax.broadcasted_iota(jnp.int32, (2 * BLK, BLK), 0) & (BLK - 1)
    col = lax.broadcasted_iota(jnp.int32, (2 * BLK, BLK), 1)
    return col <= row, (col - row) >= jnp.where(b > 0, 0, BLK)


def _dil_tiles(qf, kf, vf, d, t, nb):
    c, b = t // nb, t % nb
    start = c + d * BLK * b
    rows = pl.ds(start, BLK, stride=d)
    prev = pl.ds(jnp.where(b > 0, start - d * BLK, start), BLK, stride=d)
    bf = lambda ref, sl: ref[sl, :].astype(BF16)
    return b, rows, prev, _stack_heads(bf(qf, rows)), bf(kf, rows), bf(kf, prev), bf(vf, rows), bf(vf, prev)


def _lanes_of_heads(col2):
    return _unstack_heads(jnp.broadcast_to(col2, (2 * BLK, LANES)))


def _dilated_fwd(qkv):
    s = qkv.shape[0]
    pairs = D_DIL // LANES
    base = (3 * D_SB) // LANES
    col = lambda off: pl.BlockSpec((s, LANES), lambda p: (0, off + p))
    own = pl.BlockSpec((s, LANES), lambda p: (0, p))

    def body(q_ref, k_ref, v_ref, acc_ref, m_ref, qf, kf, vf, l_scr):
        qf[...] = q_ref[...].astype(F32)
        kf[...] = k_ref[...].astype(F32)
        vf[...] = v_ref[...].astype(F32)
        for d in DILATIONS:
            nb = s // (d * BLK)

            def block(t, _):
                b, rows, prev, q2, kc, kp, vc, vp = _dil_tiles(qf, kf, vf, d, t, nb)
                in_cur, in_prev = _band_masks(b)
                zc = jnp.where(in_cur, _dot_nt(q2, kc) * SCALE, NEG)
                zp = jnp.where(in_prev, _dot_nt(q2, kp) * SCALE, NEG)
                m = jnp.maximum(jnp.max(zc, axis=1, keepdims=True), jnp.max(zp, axis=1, keepdims=True))
                pc, pp = jnp.exp(zc - m), jnp.exp(zp - m)
                den = jnp.sum(pc, axis=1, keepdims=True) + jnp.sum(pp, axis=1, keepdims=True)
                acc = _unstack_heads(_dot(pc.astype(BF16), vc) + _dot(pp.astype(BF16), vp))
                m_t, l_t = _lanes_of_heads(m), _lanes_of_heads(den)
                if d == DILATIONS[0]:
                    m_ref[rows, :] = m_t
                    l_scr[rows, :] = l_t
                    acc_ref[rows, :] = acc
                else:
                    m_old = m_ref[rows, :]
                    m_new = jnp.maximum(m_old, m_t)
                    keep, add = jnp.exp(m_old - m_new), jnp.exp(m_t - m_new)
                    m_ref[rows, :] = m_new
                    l_scr[rows, :] = l_scr[rows, :] * keep + l_t * add
                    acc_ref[rows, :] = acc_ref[rows, :] * keep + acc * add
                return 0

            lax.fori_loop(0, s // BLK, block, 0, unroll=DIL_UNROLL)

        def finish(i, _):
            l = l_scr[_blk(i), :]
            acc_ref[_blk(i), :] = acc_ref[_blk(i), :] / l
            m_ref[_blk(i), :] = m_ref[_blk(i), :] + jnp.log(l)
            return 0

        lax.fori_loop(0, s // BLK, finish, 0)

    return pl.pallas_call(
        body, name="dilated_fwd", grid=(pairs,),
        in_specs=[col(base), col(base + pairs), col(base + 2 * pairs)],
        out_specs=[own, own],
        out_shape=[jax.ShapeDtypeStruct((s, D_DIL), F32)] * 2,
        scratch_shapes=[pltpu.VMEM((s, LANES), F32)] * 4,
        compiler_params=_params(56),
    )(qkv, qkv, qkv)


def _stack_lanes(t):
    other = pltpu.roll(t, HEAD_DIM, 1)
    first = _head_masks()[0]
    return jnp.concatenate([jnp.where(first, t, other), jnp.where(first, other, t)], axis=0)


def _dilated_bwd(qkv, delta, lse, dout):
    s = qkv.shape[0]
    pairs = D_DIL // LANES
    base = (3 * D_SB) // LANES
    once = pl.Buffered(1)
    col = lambda off: pl.BlockSpec((s, LANES), lambda p: (0, off + p), pipeline_mode=once)
    own = pl.BlockSpec((s, LANES), lambda p: (0, p), pipeline_mode=once)
    res = pl.BlockSpec((s, LANES), lambda p: (0, p))

    def body(q_ref, k_ref, v_ref, dl_ref, l_ref, do_ref, dq_ref, dk_ref, dv_ref, qf, kf, vf):
        qf[...] = q_ref[...].astype(F32)
        kf[...] = k_ref[...].astype(F32)
        vf[...] = v_ref[...].astype(F32)
        dq_ref[...] = jnp.zeros_like(dq_ref)
        dk_ref[...] = jnp.zeros_like(dk_ref)
        dv_ref[...] = jnp.zeros_like(dv_ref)
        for d in DILATIONS:
            nb = s // (d * BLK)

            def block(t, _):
                b, rows, prev, q2, kc, kp, vc, vp = _dil_tiles(qf, kf, vf, d, t, nb)
                in_cur, in_prev = _band_masks(b)
                do2 = _stack_heads(do_ref[rows, :].astype(BF16))
                delta = _stack_lanes(dl_ref[rows, :])
                lse2 = _stack_lanes(l_ref[rows, :])
                wc = jnp.exp(jnp.where(in_cur, _dot_nt(q2, kc) * SCALE, NEG) - lse2)
                wp = jnp.exp(jnp.where(in_prev, _dot_nt(q2, kp) * SCALE, NEG) - lse2)
                dzc = (wc * (_dot_nt(do2, vc) - delta) * SCALE).astype(BF16)
                dzp = (wp * (_dot_nt(do2, vp) - delta) * SCALE).astype(BF16)
                dq_ref[rows, :] += _unstack_heads(_dot(dzc, kc) + _dot(dzp, kp))
                dk_ref[rows, :] += _dot_tn(dzc, q2)
                dk_ref[prev, :] += _dot_tn(dzp, q2)
                dv_ref[rows, :] += _dot_tn(wc.astype(BF16), do2)
                dv_ref[prev, :] += _dot_tn(wp.astype(BF16), do2)
                return 0

            lax.fori_loop(0, s // BLK, block, 0, unroll=DIL_UNROLL)

    return pl.pallas_call(
        body, name="dilated_bwd", grid=(pairs,),
        in_specs=[col(base), col(base + pairs), col(base + 2 * pairs), own, own, own],
        out_specs=[res, res, res],
        out_shape=[jax.ShapeDtypeStruct((s, D_DIL), F32)] * 3,
        scratch_shapes=[pltpu.VMEM((s, LANES), F32)] * 3,
        compiler_params=_params(60),
    )(qkv, qkv, qkv, delta, lse, dout)


def _dilated_finish(grads, cos, sin):
    s = grads[0].shape[0]
    spec = pl.BlockSpec((TM, D_DIL), lambda i: (i, 0))
    tab = pl.BlockSpec((TM, LANES), lambda i: (i, 0))

    def body(dq_ref, dk_ref, dv_ref, c_ref, s_ref, out_ref):
        for t, (src, rotated) in enumerate(((dq_ref, True), (dk_ref, True), (dv_ref, False))):
            for c in range(D_DIL // LANES):
                piece = src[:, c * LANES:(c + 1) * LANES]
                at = t * D_DIL + c * LANES
                out_ref[:, at:at + LANES] = (_rotate(piece, c_ref[...], -s_ref[...]) if rotated else piece).astype(BF16)

    return pl.pallas_call(
        body, name="dilated_finish", grid=(s // TM,),
        in_specs=[spec] * 3 + [tab, tab], out_specs=pl.BlockSpec((TM, 3 * D_DIL), lambda i: (i, 0)),
        out_shape=jax.ShapeDtypeStruct((s, 3 * D_DIL), BF16),
        compiler_params=_params(32),
    )(*grads, cos, sin)


def _place():
    x, y, c = lax.axis_index("x"), lax.axis_index("y"), lax.axis_index("c")
    return x, y, c, 2 * x + y


def _chip(k, c):
    return (k >> 1, k & 1, c)


def _half(ref, h):
    n = ref.shape[0] // 2
    return ref.at[pl.ds(h * n, n)]


class _BackgroundGather:
    def __init__(self, ins, outs, scratch):
        n = self.n = len(ins)
        self.ins, self.outs = ins, outs
        self.mine, self.landed, self.passed = scratch[0:3 * n:3], scratch[1:3 * n:3], scratch[2:3 * n:3]
        self.send_sem, self.recv_sem, self.local_sem = scratch[3 * n:3 * n + 3]
        x, y, self.c, self.k = _place()
        self.sibling = (x, y, 1 - self.c)

    @staticmethod
    def scratch_shapes(shards):
        shapes = []
        for a in shards:
            half = (N_CHIP - 1, a.shape[0] // 2, a.shape[1])
            shapes += [pltpu.VMEM(a.shape, a.dtype), pltpu.VMEM(half, a.dtype), pltpu.VMEM(half, a.dtype)]
        n = len(shards)
        return shapes + [pltpu.SemaphoreType.DMA((6 * n,)), pltpu.SemaphoreType.DMA((6 * n,)),
                         pltpu.SemaphoreType.DMA((8 * n,))]

    @staticmethod
    def out_shapes(shards):
        return [jax.ShapeDtypeStruct((N_CHIP,) + a.shape, a.dtype) for a in shards]

    def _remote(self, a, slot, src, dst, to):
        return pltpu.make_async_remote_copy(src_ref=src, dst_ref=dst, send_sem=self.send_sem.at[6 * a + slot],
                                            recv_sem=self.recv_sem.at[6 * a + slot], device_id=to, device_id_type=MESH)

    def _local(self, a, slot, src, dst):
        return pltpu.make_async_copy(src, dst, self.local_sem.at[8 * a + slot])

    def _ici(self, a, j):
        return self._remote(a, j - 1, _half(self.mine[a], self.c), self.landed[a].at[j - 1], _chip(self.k ^ j, self.c))

    def _to_sibling(self, a, j):
        return self._remote(a, 2 + j, self.landed[a].at[j - 1], self.passed[a].at[j - 1], self.sibling)

    def _own(self, a):
        return self._local(a, 0, self.ins[a], self.outs[a].at[self.k])

    def _load(self, a):
        return self._local(a, 1, self.ins[a], self.mine[a])

    def _store_landed(self, a, j):
        return self._local(a, 1 + j, self.landed[a].at[j - 1], _half(self.outs[a].at[self.k ^ j], self.c))

    def _store_passed(self, a, j):
        return self._local(a, 4 + j, self.passed[a].at[j - 1], _half(self.outs[a].at[self.k ^ j], 1 - self.c))

    def start(self):
        for a in range(self.n):
            self._own(a).start()
            self._load(a).start()
        for a in range(self.n):
            self._load(a).wait()
            for j in range(1, N_CHIP):
                self._ici(a, j).start()

    def forward(self):
        for j in range(1, N_CHIP):
            for a in range(self.n):
                self._ici(a, j).wait_recv()
                self._to_sibling(a, j).start()
                self._store_landed(a, j).start()

    def finish(self):
        for j in range(1, N_CHIP):
            for a in range(self.n):
                self._to_sibling(a, j).wait_recv()
                self._store_passed(a, j).start()
        for a in range(self.n):
            for j in range(1, N_CHIP):
                self._ici(a, j).wait_send()
                self._to_sibling(a, j).wait_send()
                self._store_landed(a, j).wait()
                self._store_passed(a, j).wait()
            self._own(a).wait()


def _all_gather(shards):
    n = len(shards)
    any_spec = pl.BlockSpec(memory_space=pl.ANY)

    def body(*refs):
        gather = _BackgroundGather(refs[:n], refs[n:2 * n], refs[2 * n:])
        gather.start()
        gather.forward()
        gather.finish()

    return pl.pallas_call(
        body, name="weights_all_gather",
        in_specs=[any_spec] * n, out_specs=[any_spec] * n,
        out_shape=_BackgroundGather.out_shapes(shards),
        scratch_shapes=_BackgroundGather.scratch_shapes(shards),
        compiler_params=_params(32),
    )(*shards)


def _reduce_scatter(g, core, name):
    n, r, c = g.shape
    hr = r // 2
    once = pl.Buffered(1)
    in_specs = [pl.BlockSpec((n, hr, c), lambda i, core_ref: (0, core_ref[0], 0), pipeline_mode=once),
                pl.BlockSpec((n, hr, c), lambda i, core_ref: (0, 1 - core_ref[0], 0), pipeline_mode=once)]

    def body(core_ref, mine_ref, other_ref, out_ref, from_core, sums_bf, from_chips, done, from_core2, send_sem, recv_sem):
        x, y, cc, k = _place()
        sibling = (x, y, 1 - cc)

        def copy(slot, src, dst, to):
            return pltpu.make_async_remote_copy(src_ref=src, dst_ref=dst, send_sem=send_sem.at[slot],
                                                recv_sem=recv_sem.at[slot], device_id=to, device_id_type=MESH)

        from_sibling = [copy(j, other_ref.at[k ^ j], from_core.at[k ^ j], sibling) for j in range(N_CHIP)]
        for j in (1, 2, 3, 0):
            from_sibling[j].start()
        sends = []
        for j in range(1, N_CHIP):
            from_sibling[j].wait()
            sums_bf[j - 1] = (mine_ref[k ^ j] + from_core[k ^ j]).astype(BF16)
            cp = copy(N_CHIP - 1 + j, sums_bf.at[j - 1], from_chips.at[j - 1], _chip(k ^ j, cc))
            cp.start()
            sends.append(cp)
        from_sibling[0].wait()
        red = mine_ref[k] + from_core[k]
        for j in range(1, N_CHIP):
            sends[j - 1].wait()
            red = red + from_chips[j - 1].astype(F32)
        done[...] = red
        last = copy(2 * N_CHIP - 1, done, from_core2, sibling)
        last.start()
        last.wait()
        row0 = pl.multiple_of(cc * hr, 8)
        row1 = pl.multiple_of((1 - cc) * hr, 8)
        out_ref[pl.ds(row0, hr), :] = red
        out_ref[pl.ds(row1, hr), :] = from_core2[...]

    grid_spec = pltpu.PrefetchScalarGridSpec(
        num_scalar_prefetch=1, grid=(1,), in_specs=in_specs,
        out_specs=pl.BlockSpec((r, c), lambda i, core_ref: (0, 0)),
        scratch_shapes=[pltpu.VMEM((n, hr, c), F32), pltpu.VMEM((N_CHIP - 1, hr, c), BF16),
                        pltpu.VMEM((N_CHIP - 1, hr, c), BF16), pltpu.VMEM((hr, c), F32), pltpu.VMEM((hr, c), F32),
                        pltpu.SemaphoreType.DMA((2 * N_CHIP,)), pltpu.SemaphoreType.DMA((2 * N_CHIP,))])
    return pl.pallas_call(
        body, name=name, grid_spec=grid_spec, out_shape=jax.ShapeDtypeStruct((r, c), F32),
        compiler_params=_params(56),
    )(core, g, g)


def _elementwise(fn, name, ins, n_out, rows):
    total, cols = ins[0].shape
    spec = pl.BlockSpec((rows, cols), lambda i: (i, 0))

    def body(*refs):
        res = fn(*[r[...] for r in refs[:len(ins)]])
        for o, v in zip(refs[len(ins):], res):
            o[...] = v

    return pl.pallas_call(
        body, name=name, grid=(total // rows,),
        in_specs=[spec] * len(ins), out_specs=[spec] * n_out,
        out_shape=[jax.ShapeDtypeStruct((total, cols), F32)] * n_out,
        compiler_params=_params(48),
    )(*ins)


def _adamw(w, g, m, v):
    m = ADAM_B1 * m + (1.0 - ADAM_B1) * g
    v = ADAM_B2 * v + (1.0 - ADAM_B2) * (g * g)
    m_hat = m / (1.0 - ADAM_B1 ** ADAM_STEP)
    v_hat = v / (1.0 - ADAM_B2 ** ADAM_STEP)
    delta = -ADAM_LR * (m_hat / (jnp.sqrt(v_hat) + ADAM_EPS) + ADAM_WD * w)
    return delta, m, v


def _reduce_and_update(grads, weights, moms, vels):
    core = lax.axis_index("c").astype(jnp.int32).reshape(1)
    full = [_reduce_scatter(g, core, f"grads_reduce_scatter_{a}") for a, g in enumerate(grads)]
    out = []
    for a, (g, w, m, v) in enumerate(zip(full, weights, moms, vels)):
        rows = g.shape[0] // 2
        out.append((g,) + tuple(_elementwise(lambda gg, ww, mm, vv: _adamw(ww, gg, mm, vv), f"adamw_{a}", [g, w, m, v], 3, rows)))
    return out


def _reduce_vectors(part, w, m, v):
    n_dev = 8

    def body(p_ref, w_ref, m_ref, v_ref, g_ref, d_ref, nm_ref, nv_ref, buf, send_sem, recv_sem):
        x, y, c, _ = _place()
        me = 4 * x + 2 * y + c
        buf[me] = p_ref[...]
        sends = []
        for off in range(1, n_dev):
            peer = me ^ off
            cp = pltpu.make_async_remote_copy(src_ref=p_ref, dst_ref=buf.at[me], send_sem=send_sem.at[off - 1],
                                              recv_sem=recv_sem.at[off - 1], device_id=(peer >> 2, (peer >> 1) & 1, peer & 1),
                                              device_id_type=MESH)
            cp.start()
            sends.append(cp)
        for off in range(1, n_dev):
            peer = me ^ off
            pltpu.make_async_remote_copy(src_ref=p_ref, dst_ref=buf.at[peer], send_sem=send_sem.at[off - 1],
                                         recv_sem=recv_sem.at[off - 1], device_id=(peer >> 2, (peer >> 1) & 1, peer & 1),
                                         device_id_type=MESH).wait_recv()
        for cp in sends:
            cp.wait_send()
        g = buf[0]
        for d in range(1, n_dev):
            g = g + buf[d]
        g_ref[...] = g
        delta, nm, nv = _adamw(w_ref[...], g, m_ref[...], v_ref[...])
        d_ref[...] = delta
        nm_ref[...] = nm
        nv_ref[...] = nv

    vm = pl.BlockSpec(memory_space=pltpu.VMEM)
    return pl.pallas_call(
        body, name="gains_all_reduce",
        in_specs=[vm] * 4, out_specs=[vm] * 4,
        out_shape=[jax.ShapeDtypeStruct(part.shape, F32)] * 4,
        scratch_shapes=[pltpu.VMEM((n_dev,) + part.shape, F32), pltpu.SemaphoreType.DMA((n_dev - 1,)),
                        pltpu.SemaphoreType.DMA((n_dev - 1,))],
    )(part, w, m, v)


def _pad_row(a):
    a = a.reshape(1, -1)
    return jnp.pad(a, ((0, 0), (0, D_MODEL - a.shape[1])))


def kernel(x, ffn1_norm, ffn1_w_gate, ffn1_w_up, ffn1_w_down, mix_norm, w_in, sb_out_norm, dil_out_norm, w_out, ffn2_norm, ffn2_w_gate, ffn2_w_up, ffn2_w_down, final_norm, loss_target, m_ffn1_norm, m_ffn1_w_gate, m_ffn1_w_up, m_ffn1_w_down, m_mix_norm, m_w_in, m_sb_out_norm, m_dil_out_norm, m_w_out, m_ffn2_norm, m_ffn2_w_gate, m_ffn2_w_up, m_ffn2_w_down, m_final_norm, v_ffn1_norm, v_ffn1_w_gate, v_ffn1_w_up, v_ffn1_w_down, v_mix_norm, v_w_in, v_sb_out_norm, v_dil_out_norm, v_w_out, v_ffn2_norm, v_ffn2_w_gate, v_ffn2_w_up, v_ffn2_w_down, v_final_norm):
    x = x[0]
    target = loss_target[0]
    s = x.shape[0]
    gf = final_norm.reshape(1, D_MODEL)
    cos, sin = _rope_tables(s)

    flip = lambda a: a[0].T
    shard = lambda w: w[0].astype(BF16)
    shard_t = lambda w: flip(w).astype(BF16)
    wg1, wu1, wd1 = _all_gather([shard_t(ffn1_w_gate), shard_t(ffn1_w_up), shard(ffn1_w_down)])

    x1, hm, saved1, (win, wout, wd2) = _ffn1_fwd(x, ffn1_norm, mix_norm, (wg1, wu1), wd1,
                                                 [shard(w_in), shard(w_out), shard(ffn2_w_down)])
    wout = wout.reshape(D_MODEL, D_MODEL)
    qkv, (wg2, wu2) = _proj_fwd(hm, win, cos, sin, [shard_t(ffn2_w_gate), shard_t(ffn2_w_up)])
    o_sb = _sb_fwd(qkv)
    o_dl, lse = _dilated_fwd(qkv)
    x2 = _outproj_fwd(o_sb, o_dl, sb_out_norm, dil_out_norm, x1, wout)
    dx3, st_final, saved2 = _ffn2_fwd_loss(x2, ffn2_norm, gf, target, (wg2, wu2), wd2)

    dx2, dwg2, dwu2, dwd2, st_ffn2 = _ffn_bwd(x2, ffn2_norm, dx3, saved2, (wg2, wu2), wd2, 1)
    do_sb, do_dl, delta_dl, dwout, st_out = _outproj_bwd(dx2, o_sb, o_dl, sb_out_norm, dil_out_norm, wout)
    dqkv_sb = _sb_bwd(qkv, o_sb, do_sb)
    dqkv_dl = _dilated_finish(_dilated_bwd(qkv, delta_dl, lse, do_dl), cos, sin)
    dx1, dwin, st_mix = _proj_bwd(x1, mix_norm, dqkv_sb, dqkv_dl, win, dx2)
    grad_x, dwg1, dwu1, dwd1, st_ffn1 = _ffn_bwd(x, ffn1_norm, dx1, saved1, (wg1, wu1), wd1, 0)

    names = ["ffn1_w_gate", "ffn1_w_up", "ffn1_w_down", "w_in", "w_out", "ffn2_w_gate", "ffn2_w_up", "ffn2_w_down"]
    grads = [dwg1, dwu1, dwd1, dwin, dwout.reshape(N_CHIP, OUTB, D_MODEL), dwg2, dwu2, dwd2]
    flipped = {"ffn1_w_gate", "ffn1_w_up", "ffn2_w_gate", "ffn2_w_up"}
    place = lambda n, a: flip(a) if n in flipped else a[0]
    weights = [place(n, a) for n, a in zip(names, [ffn1_w_gate, ffn1_w_up, ffn1_w_down, w_in, w_out, ffn2_w_gate, ffn2_w_up, ffn2_w_down])]
    moms = [place(n, a) for n, a in zip(names, [m_ffn1_w_gate, m_ffn1_w_up, m_ffn1_w_down, m_w_in, m_w_out, m_ffn2_w_gate, m_ffn2_w_up, m_ffn2_w_down])]
    vels = [place(n, a) for n, a in zip(names, [v_ffn1_w_gate, v_ffn1_w_up, v_ffn1_w_down, v_w_in, v_w_out, v_ffn2_w_gate, v_ffn2_w_up, v_ffn2_w_down])]
    mats = {n: tuple((t.T if n in flipped else t)[None] for t in r)
            for n, r in zip(names, _reduce_and_update(grads, weights, moms, vels))}

    vec_names = ["ffn1_norm", "mix_norm", "sb_out_norm", "dil_out_norm", "ffn2_norm", "final_norm"]
    part = jnp.concatenate([st_ffn1[0:1], st_mix[0:1], _pad_row(st_out[0]), _pad_row(st_out[1]), st_ffn2[0:1],
                            st_final[0:1], st_final[1:2], jnp.zeros((1, D_MODEL), F32)], axis=0)
    pack = lambda arrs: jnp.concatenate([_pad_row(a) for a in arrs] + [jnp.zeros((2, D_MODEL), F32)], axis=0)
    g_vec, d_vec, m_vec, v_vec = _reduce_vectors(
        part,
        pack([ffn1_norm, mix_norm, sb_out_norm, dil_out_norm, ffn2_norm, final_norm]),
        pack([m_ffn1_norm, m_mix_norm, m_sb_out_norm, m_dil_out_norm, m_ffn2_norm, m_final_norm]),
        pack([v_ffn1_norm, v_mix_norm, v_sb_out_norm, v_dil_out_norm, v_ffn2_norm, v_final_norm]))
    like = {"ffn1_norm": ffn1_norm, "mix_norm": mix_norm, "sb_out_norm": sb_out_norm, "dil_out_norm": dil_out_norm,
            "ffn2_norm": ffn2_norm, "final_norm": final_norm}
    vecs = {n: tuple(t[i, :like[n].size].reshape(like[n].shape) for t in (g_vec, d_vec, m_vec, v_vec))
            for i, n in enumerate(vec_names)}
    loss = 0.5 * jnp.sum(g_vec[6]) / D_MODEL

    order = ["ffn1_norm", "ffn1_w_gate", "ffn1_w_up", "ffn1_w_down", "mix_norm", "w_in", "sb_out_norm", "dil_out_norm",
             "w_out", "ffn2_norm", "ffn2_w_gate", "ffn2_w_up", "ffn2_w_down", "final_norm"]
    both = {**mats, **vecs}
    return (loss, grad_x[None], *[both[n][0] for n in order], *[both[n][1] for n in order],
            *[both[n][2] for n in order], *[both[n][3] for n in order])
```

```python
import functools

import jax
import jax.numpy as jnp
from jax import lax
from jax.experimental import pallas as pl
from jax.experimental.pallas import tpu as pltpu

D_MODEL = 1024
D_FF = 2816
HEAD_DIM = 64
D_SB = 512
D_DIL = 512
D_IN = 3072
N_CHIP = 4
FFB = D_FF // N_CHIP
INB = D_IN // N_CHIP
OUTB = D_MODEL // N_CHIP
BLK = 128
LANES = 128
DILATIONS = (1, 4, 16)
ROPE_THETA = 10000.0
RMS_EPS = 1e-6
SCALE = HEAD_DIM ** -0.5
NEG = -1e30
DEAD = -104.0
ADAM_LR = 0.001
ADAM_B1 = 0.9
ADAM_B2 = 0.999
ADAM_EPS = 1e-08
ADAM_WD = 0.01
ADAM_STEP = 10
MESH = pl.DeviceIdType.MESH
F32 = jnp.float32
BF16 = jnp.bfloat16
TM = 512


def _params(vmem_mb):
    return pltpu.CompilerParams(vmem_limit_bytes=vmem_mb << 20)


def _dot(a, b):
    return jnp.dot(a, b, preferred_element_type=F32)


def _dot_nt(a, b):
    return lax.dot_general(a, b, (((1,), (1,)), ((), ())), preferred_element_type=F32)


def _dot_tn(a, b):
    return lax.dot_general(a, b, (((0,), (0,)), ((), ())), preferred_element_type=F32)


def _rms_fwd(x, g):
    r = lax.rsqrt(jnp.mean(x * x, axis=-1, keepdims=True) + RMS_EPS)
    xh = x * r
    return xh * g, xh, r


def _rms_bwd(dy, xh, r, g):
    dyg = dy * g
    dx = r * (dyg - xh * jnp.mean(dyg * xh, axis=-1, keepdims=True))
    return dx, jnp.sum(dy * xh, axis=0, keepdims=True)


def _split_bf16(a):
    hi = a.astype(BF16)
    return hi, (a - hi.astype(F32)).astype(BF16)


def _dot_split(a, b2):
    hi, lo = _split_bf16(a)
    return _dot(jnp.concatenate([hi, lo], axis=1), b2)


def _ffn_weight_specs():
    return [pl.BlockSpec((None, FFB, D_MODEL), lambda i, j: (j, 0, 0))] * 3


def _ffn_saved(s):
    hidden = jax.ShapeDtypeStruct((N_CHIP, s, FFB), BF16)
    hid = pl.BlockSpec((None, TM, FFB), lambda i, j: (j, i, 0))
    row = pl.BlockSpec((TM, D_MODEL), lambda i, j: (i, 0))
    return [row, hid, hid, hid], [jax.ShapeDtypeStruct((s, D_MODEL), BF16), hidden, hidden, hidden]


def _ffn_accumulate(h_ref, acc_scr, wg_ref, wu_ref, wd_ref, a_ref, b_ref, act_ref):
    h = h_ref[...]
    a = _dot_nt(h, wg_ref[...])
    b = _dot_nt(h, wu_ref[...])
    act = ((a * jax.nn.sigmoid(a)) * b).astype(BF16)
    a_ref[...] = a.astype(BF16)
    b_ref[...] = b.astype(BF16)
    act_ref[...] = act
    acc_scr[...] += _dot(act, wd_ref[...])


def _host_gather_before(gather, i, j, steps):
    @pl.when((i == 0) & (j == 0))
    def _():
        gather.start()

    @pl.when((i == (3 * steps) // 4) & (j == 0))
    def _():
        gather.forward()


def _host_gather_after(gather, i, j, steps):
    @pl.when((i == steps - 1) & (j == N_CHIP - 1))
    def _():
        gather.finish()


def _ffn1_fwd(x, g1, gmix, gu, wd, later_shards):
    s = x.shape[0]
    row = pl.BlockSpec((TM, D_MODEL), lambda i, j: (i, 0))
    vec = pl.BlockSpec((1, D_MODEL), lambda i, j: (0, 0))
    saved_specs, saved_shapes = _ffn_saved(s)
    n = len(later_shards)
    any_spec = pl.BlockSpec(memory_space=pl.ANY)

    def body(*refs):
        x_ref, g_ref, gm_ref, wg_ref, wu_ref, wd_ref = refs[:6]
        shard_refs, refs = refs[6:6 + n], refs[6 + n:]
        x1_ref, hm_ref, h_ref, a_ref, b_ref, act_ref = refs[:6]
        gathered_refs, acc_scr, gather_scratch = refs[6:6 + n], refs[6 + n], refs[7 + n:]
        gather = _BackgroundGather(shard_refs, gathered_refs, gather_scratch)
        i, j = pl.program_id(0), pl.program_id(1)
        _host_gather_before(gather, i, j, s // TM)

        @pl.when(j == 0)
        def _():
            h, _, _ = _rms_fwd(x_ref[...], g_ref[...])
            h_ref[...] = h.astype(BF16)
            acc_scr[...] = jnp.zeros_like(acc_scr)

        _ffn_accumulate(h_ref, acc_scr, wg_ref, wu_ref, wd_ref, a_ref, b_ref, act_ref)

        @pl.when(j == N_CHIP - 1)
        def _():
            x1 = x_ref[...] + 0.5 * acc_scr[...]
            x1_ref[...] = x1
            hm, _, _ = _rms_fwd(x1, gm_ref[...])
            hm_ref[...] = hm.astype(BF16)

        _host_gather_after(gather, i, j, s // TM)

    x1, hm, h, a, b, act, *gathered = pl.pallas_call(
        body, name="ffn1_fwd", grid=(s // TM, N_CHIP),
        in_specs=[row, vec, vec] + _ffn_weight_specs() + [any_spec] * n,
        out_specs=[row, row] + saved_specs + [any_spec] * n,
        out_shape=([jax.ShapeDtypeStruct((s, D_MODEL), F32), jax.ShapeDtypeStruct((s, D_MODEL), BF16)] + saved_shapes
                   + _BackgroundGather.out_shapes(later_shards)),
        scratch_shapes=[pltpu.VMEM((TM, D_MODEL), F32)] + _BackgroundGather.scratch_shapes(later_shards),
        compiler_params=_params(58),
    )(x, g1, gmix, gu[0], gu[1], wd, *later_shards)
    return x1, hm, [h, a, b, act], gathered


def _ffn2_fwd_loss(x2, g2, gf, target, gu, wd):
    s = x2.shape[0]
    row = pl.BlockSpec((TM, D_MODEL), lambda i, j: (i, 0))
    vec = pl.BlockSpec((1, D_MODEL), lambda i, j: (0, 0))
    stat = pl.BlockSpec((8, D_MODEL), lambda i, j: (0, 0))
    saved_specs, saved_shapes = _ffn_saved(s)

    def body(x_ref, g_ref, gf_ref, t_ref, wg_ref, wu_ref, wd_ref, wd_prev_ref,
             dx_ref, st_ref, h_ref, a_ref, b_ref, act_ref, acc_scr, late_scr):
        i, j = pl.program_id(0), pl.program_id(1)

        @pl.when((i == 0) & (j == 0))
        def _():
            st_ref[...] = jnp.zeros_like(st_ref)

        @pl.when(j == 0)
        def _():
            h, _, _ = _rms_fwd(x_ref[...], g_ref[...])
            h_ref[...] = h.astype(BF16)
            acc_scr[...] = jnp.zeros_like(acc_scr)
            late_scr[...] = jnp.zeros_like(late_scr)

        h = h_ref[...]
        a = _dot_nt(h, wg_ref[...])
        b = _dot_nt(h, wu_ref[...])
        act = ((a * jax.nn.sigmoid(a)) * b).astype(BF16)
        a_ref[...] = a.astype(BF16)
        b_ref[...] = b.astype(BF16)
        act_ref[...] = act
        acc_scr[...] += _dot(late_scr[...], wd_prev_ref[...])
        late_scr[...] = act

        @pl.when(j == N_CHIP - 1)
        def _():
            x3 = x_ref[...] + 0.5 * (acc_scr[...] + _dot(late_scr[...], wd_ref[...]))
            y, xh, r = _rms_fwd(x3, gf_ref[...])
            err = y - t_ref[...]
            dx, dg = _rms_bwd(err * (1.0 / D_MODEL), xh, r, gf_ref[...])
            dx_ref[...] = dx
            st_ref[0:1, :] += dg
            st_ref[1:2, :] += jnp.sum(err * err, axis=0, keepdims=True)

    dx3, st, *saved = pl.pallas_call(
        body, name="ffn2_fwd_loss", grid=(s // TM, N_CHIP),
        in_specs=([row, vec, vec, row] + _ffn_weight_specs()
                  + [pl.BlockSpec((None, FFB, D_MODEL), lambda i, j: ((j + N_CHIP - 1) % N_CHIP, 0, 0))]),
        out_specs=[row, stat] + saved_specs,
        out_shape=[jax.ShapeDtypeStruct((s, D_MODEL), F32), jax.ShapeDtypeStruct((8, D_MODEL), F32)] + saved_shapes,
        scratch_shapes=[pltpu.VMEM((TM, D_MODEL), F32), pltpu.VMEM((TM, FFB), BF16)],
        compiler_params=_params(56),
    )(x2, g2, gf, target, gu[0], gu[1], wd, wd)
    return dx3, st, saved


def _ffn_bwd(xin, g, dy, saved, gu, wd, f):
    s = xin.shape[0]
    hb, gate, up, act = saved
    row = pl.BlockSpec((TM, D_MODEL), lambda i, j: (i, 0))
    vec = pl.BlockSpec((1, D_MODEL), lambda i, j: (0, 0))
    stat = pl.BlockSpec((8, D_MODEL), lambda i, j: (0, 0))
    hid = pl.BlockSpec((None, TM, FFB), lambda i, j: (j, i, 0))

    def body(x_ref, g_ref, dy_ref, a_ref, b_ref, wg_ref, wu_ref, wd_ref, wd_next_ref,
             out_ref, dyh_ref, da_ref, db_ref, st_ref, dh_scr, dact_scr):
        i, j = pl.program_id(0), pl.program_id(1)

        @pl.when((i == 0) & (j == 0))
        def _():
            st_ref[...] = jnp.zeros_like(st_ref)

        @pl.when(j == 0)
        def _():
            dyh_ref[...] = (0.5 * dy_ref[...]).astype(BF16)
            dh_scr[...] = jnp.zeros_like(dh_scr)
            dact_scr[...] = _dot_nt(dyh_ref[...], wd_ref[...])

        a = a_ref[...].astype(F32)
        b = b_ref[...].astype(F32)
        sg = jax.nn.sigmoid(a)
        dact = dact_scr[...]
        dab = (dact * b * (sg * (1.0 + a * (1.0 - sg)))).astype(BF16)
        dbb = (dact * (a * sg)).astype(BF16)
        da_ref[...] = dab
        db_ref[...] = dbb
        dact_scr[...] = _dot_nt(dyh_ref[...], wd_next_ref[...])
        dh_scr[...] += _dot(dab, wg_ref[...]) + _dot(dbb, wu_ref[...])

        @pl.when(j == N_CHIP - 1)
        def _():
            _, xh, r = _rms_fwd(x_ref[...], g_ref[...])
            dx, dg = _rms_bwd(dh_scr[...], xh, r, g_ref[...])
            out_ref[...] = dy_ref[...] + dx
            st_ref[0:1, :] += dg

    hidden = jax.ShapeDtypeStruct((N_CHIP, s, FFB), BF16)
    dx, dyh, da, db, st = pl.pallas_call(
        body, name=f"ffn{f + 1}_bwd_dx", grid=(s // TM, N_CHIP),
        in_specs=([row, vec, row, hid, hid] + _ffn_weight_specs()
                  + [pl.BlockSpec((None, FFB, D_MODEL), lambda i, j: ((j + 1) % N_CHIP, 0, 0))]),
        out_specs=[row, row, hid, hid, stat],
        out_shape=[jax.ShapeDtypeStruct((s, D_MODEL), F32), jax.ShapeDtypeStruct((s, D_MODEL), BF16),
                   hidden, hidden, jax.ShapeDtypeStruct((8, D_MODEL), F32)],
        scratch_shapes=[pltpu.VMEM((TM, D_MODEL), F32), pltpu.VMEM((TM, FFB), F32)],
        compiler_params=_params(58),
    )(xin, g, dy, gate, up, gu[0], gu[1], wd, wd)

    tk = 4 * TM
    tok = pl.BlockSpec((tk, D_MODEL), lambda j, i: (i, 0))
    hid2 = pl.BlockSpec((None, tk, FFB), lambda j, i: (j, i, 0))
    gspecs = [pl.BlockSpec((None, FFB, D_MODEL), lambda j, i: (j, 0, 0))] * 3

    def wbody(h_ref, dyh_ref, da_ref, db_ref, act_ref, dwg_ref, dwu_ref, dwd_ref):
        @pl.when(pl.program_id(1) == 0)
        def _():
            dwg_ref[...] = jnp.zeros_like(dwg_ref)
            dwu_ref[...] = jnp.zeros_like(dwu_ref)
            dwd_ref[...] = jnp.zeros_like(dwd_ref)

        hb = h_ref[...]
        dwg_ref[...] += _dot_tn(da_ref[...], hb)
        dwu_ref[...] += _dot_tn(db_ref[...], hb)
        dwd_ref[...] += _dot_tn(act_ref[...], dyh_ref[...])

    dwg, dwu, dwd = pl.pallas_call(
        wbody, name=f"ffn{f + 1}_bwd_dw", grid=(N_CHIP, s // tk),
        in_specs=[tok, tok, hid2, hid2, hid2], out_specs=gspecs,
        out_shape=[jax.ShapeDtypeStruct((N_CHIP, FFB, D_MODEL), F32)] * 3,
        compiler_params=_params(60),
    )(hb, dyh, da, db, act)
    return dx, dwg, dwu, dwd, st


def _rope_tables(s):
    half = HEAD_DIM // 2
    inv_freq = ROPE_THETA ** (-jnp.arange(half, dtype=F32) / half)
    ang = jnp.arange(s).astype(F32)[:, None] * inv_freq[None, :]
    cos, sin = jnp.cos(ang), jnp.sin(ang)
    cos2 = jnp.concatenate([cos, cos], axis=-1)
    sin2 = jnp.concatenate([-sin, sin], axis=-1)
    return jnp.tile(cos2, (1, LANES // HEAD_DIM)), jnp.tile(sin2, (1, LANES // HEAD_DIM))


def _rotate(t, cos, sin_signed):
    lane = lax.broadcasted_iota(jnp.int32, t.shape, 1)
    first = (lane % HEAD_DIM) < (HEAD_DIM // 2)
    partner = jnp.where(first, pltpu.roll(t, LANES - HEAD_DIM // 2, 1), pltpu.roll(t, HEAD_DIM // 2, 1))
    return t * cos + partner * sin_signed


def _proj_fwd(hm, win, cos, sin, later_shards):
    s = hm.shape[0]
    tm = 2 * TM
    n_sub = INB // LANES
    first_rot, last_rot = (3 * D_SB) // LANES, (3 * D_SB + 2 * D_DIL) // LANES
    n = len(later_shards)
    any_spec = pl.BlockSpec(memory_space=pl.ANY)

    def body(*refs):
        h_ref, w_ref, c_ref, s_ref = refs[:4]
        shard_refs, o_ref, gathered_refs, gather_scratch = refs[4:4 + n], refs[4 + n], refs[5 + n:5 + 2 * n], refs[5 + 2 * n:]
        gather = _BackgroundGather(shard_refs, gathered_refs, gather_scratch)
        i, j = pl.program_id(0), pl.program_id(1)
        _host_gather_before(gather, i, j, s // tm)
        r = _dot(h_ref[...], w_ref[...])
        for c in range(n_sub):
            t = r[:, c * LANES:(c + 1) * LANES]
            col = j * n_sub + c
            rot = (col >= first_rot) & (col < last_rot)
            lanes = slice(c * LANES, (c + 1) * LANES)

            @pl.when(rot)
            def _():
                o_ref[:, lanes] = _rotate(t, c_ref[...], s_ref[...]).astype(BF16)

            @pl.when(jnp.logical_not(rot))
            def _():
                o_ref[:, lanes] = t.astype(BF16)

        _host_gather_after(gather, i, j, s // tm)

    qkv, *gathered = pl.pallas_call(
        body, name="proj_fwd", grid=(s // tm, N_CHIP),
        in_specs=[pl.BlockSpec((tm, D_MODEL), lambda i, j: (i, 0)),
                  pl.BlockSpec((None, D_MODEL, INB), lambda i, j: (j, 0, 0)),
                  pl.BlockSpec((tm, LANES), lambda i, j: (i, 0)),
                  pl.BlockSpec((tm, LANES), lambda i, j: (i, 0))] + [any_spec] * n,
        out_specs=[pl.BlockSpec((tm, INB), lambda i, j: (i, j))] + [any_spec] * n,
        out_shape=[jax.ShapeDtypeStruct((s, D_IN), BF16)] + _BackgroundGather.out_shapes(later_shards),
        scratch_shapes=_BackgroundGather.scratch_shapes(later_shards),
        compiler_params=_params(48),
    )(hm, win, cos, sin, *later_shards)
    return qkv, gathered


def _proj_bwd(x1, gmix, dqkv_sb, dqkv_dl, win, dx2):
    s = x1.shape[0]
    row = pl.BlockSpec((TM, D_MODEL), lambda i, j: (i, 0))
    vec = pl.BlockSpec((1, D_MODEL), lambda i, j: (0, 0))
    per_group = N_CHIP // 2

    def body(x_ref, g_ref, dsb_ref, ddl_ref, w_ref, dx2_ref, out_ref, dw_ref, st_ref, h_scr, dh_scr, dq_ref):
        i, j = pl.program_id(0), pl.program_id(1)

        @pl.when(j < per_group)
        def _():
            dq_ref[...] = dsb_ref[...]

        @pl.when(j >= per_group)
        def _():
            dq_ref[...] = ddl_ref[...]

        @pl.when((i == 0) & (j == 0))
        def _():
            st_ref[...] = jnp.zeros_like(st_ref)
            dw_ref[...] = jnp.zeros_like(dw_ref)

        @pl.when(j == 0)
        def _():
            h, _, _ = _rms_fwd(x_ref[...], g_ref[...])
            h_scr[...] = h.astype(BF16)
            dh_scr[...] = jnp.zeros_like(dh_scr)

        dq = dq_ref[...]
        dw_ref[j] += _dot_tn(h_scr[...], dq)
        dh_scr[...] += _dot_nt(dq, w_ref[...])

        @pl.when(j == N_CHIP - 1)
        def _():
            _, xh, r = _rms_fwd(x_ref[...], g_ref[...])
            dx, dg = _rms_bwd(dh_scr[...], xh, r, g_ref[...])
            out_ref[...] = dx2_ref[...] + dx
            st_ref[0:1, :] += dg

    return pl.pallas_call(
        body, name="proj_bwd", grid=(s // TM, N_CHIP),
        in_specs=[row, vec,
                  pl.BlockSpec((TM, INB), lambda i, j: (i, jnp.minimum(j, per_group - 1))),
                  pl.BlockSpec((TM, INB), lambda i, j: (i, jnp.maximum(j - per_group, 0))),
                  pl.BlockSpec((None, D_MODEL, INB), lambda i, j: (j, 0, 0)), row],
        out_specs=[row, pl.BlockSpec((N_CHIP, D_MODEL, INB), lambda i, j: (0, 0, 0)),
                   pl.BlockSpec((8, D_MODEL), lambda i, j: (0, 0))],
        out_shape=[jax.ShapeDtypeStruct((s, D_MODEL), F32),
                   jax.ShapeDtypeStruct((N_CHIP, D_MODEL, INB), F32),
                   jax.ShapeDtypeStruct((8, D_MODEL), F32)],
        scratch_shapes=[pltpu.VMEM((TM, D_MODEL), BF16), pltpu.VMEM((TM, D_MODEL), F32), pltpu.VMEM((TM, INB), BF16)],
        compiler_params=_params(56),
    )(x1, gmix, dqkv_sb, dqkv_dl, win, dx2)


def _outproj_fwd(o_sb, o_dl, g_sb, g_dl, x1, wout):
    s = x1.shape[0]
    tm = 2 * TM
    half = pl.BlockSpec((tm, D_SB), lambda i: (i, 0))
    row = pl.BlockSpec((tm, D_MODEL), lambda i: (i, 0))
    vec = pl.BlockSpec((1, D_SB), lambda i: (0, 0))

    def body(a_ref, b_ref, ga_ref, gb_ref, x_ref, w_ref, o_ref):
        ma, _, _ = _rms_fwd(a_ref[...], ga_ref[...])
        mb, _, _ = _rms_fwd(b_ref[...], gb_ref[...])
        o_ref[...] = (x_ref[...] + _dot(ma.astype(BF16), w_ref[0:D_SB, :])
                      + _dot(mb.astype(BF16), w_ref[D_SB:D_MODEL, :]))

    return pl.pallas_call(
        body, name="outproj_fwd", grid=(s // tm,),
        in_specs=[half, half, vec, vec, row, pl.BlockSpec((D_MODEL, D_MODEL), lambda i: (0, 0))],
        out_specs=row, out_shape=jax.ShapeDtypeStruct((s, D_MODEL), F32),
        compiler_params=_params(32),
    )(o_sb, o_dl, g_sb, g_dl, x1, wout)


def _outproj_bwd(dx2, o_sb, o_dl, g_sb, g_dl, wout):
    s = dx2.shape[0]
    tm = 2 * TM
    half = pl.BlockSpec((tm, D_SB), lambda i: (i, 0))
    row = pl.BlockSpec((tm, D_MODEL), lambda i: (i, 0))
    vec = pl.BlockSpec((1, D_SB), lambda i: (0, 0))
    full = pl.BlockSpec((D_MODEL, D_MODEL), lambda i: (0, 0))

    def body(dy_ref, a_ref, b_ref, ga_ref, gb_ref, w_ref, da_ref, db_ref, dl_ref, dw_ref, st_ref):
        @pl.when(pl.program_id(0) == 0)
        def _():
            dw_ref[...] = jnp.zeros_like(dw_ref)
            st_ref[...] = jnp.zeros_like(st_ref)

        dy = dy_ref[...].astype(BF16)
        dm = _dot_nt(dy, w_ref[...])
        ma, xa, ra = _rms_fwd(a_ref[...], ga_ref[...])
        mb, xb, rb = _rms_fwd(b_ref[...], gb_ref[...])
        dw_ref[0:D_SB, :] += _dot_tn(ma.astype(BF16), dy)
        dw_ref[D_SB:D_MODEL, :] += _dot_tn(mb.astype(BF16), dy)
        da, dga = _rms_bwd(dm[:, 0:D_SB], xa, ra, ga_ref[...])
        db, dgb = _rms_bwd(dm[:, D_SB:D_MODEL], xb, rb, gb_ref[...])
        da_ref[...] = da
        db_ref[...] = db
        r = lax.broadcasted_iota(jnp.int32, (LANES, LANES), 0) >= HEAD_DIM
        c = lax.broadcasted_iota(jnp.int32, (LANES, LANES), 1) >= HEAD_DIM
        same_head = jnp.where(r == c, 1.0, 0.0).astype(BF16)
        same_head = jnp.concatenate([same_head, same_head], axis=0)
        prod = db * b_ref[...]
        for k in range(D_DIL // LANES):
            lanes = slice(k * LANES, (k + 1) * LANES)
            dl_ref[:, lanes] = _dot_split(prod[:, lanes], same_head)
        st_ref[0:1, :] += dga
        st_ref[1:2, :] += dgb

    return pl.pallas_call(
        body, name="outproj_bwd", grid=(s // tm,),
        in_specs=[row, half, half, vec, vec, full],
        out_specs=[half, half, half, full, pl.BlockSpec((8, D_SB), lambda i: (0, 0))],
        out_shape=[jax.ShapeDtypeStruct((s, D_SB), F32), jax.ShapeDtypeStruct((s, D_SB), F32),
                   jax.ShapeDtypeStruct((s, D_DIL), F32),
                   jax.ShapeDtypeStruct((D_MODEL, D_MODEL), F32), jax.ShapeDtypeStruct((8, D_SB), F32)],
        compiler_params=_params(48),
    )(dx2, o_sb, o_dl, g_sb, g_dl, wout)


def _head_masks():
    lane = lax.broadcasted_iota(jnp.int32, (BLK, LANES), 1)
    return [lane < HEAD_DIM, lane >= HEAD_DIM]


def _keep(mask, a):
    return a * jnp.where(mask, 1.0, 0.0).astype(a.dtype)


def _suffix_matrices():
    r = lax.broadcasted_iota(jnp.int32, (2 * BLK, BLK), 0) & (BLK - 1)
    c = lax.broadcasted_iota(jnp.int32, (2 * BLK, BLK), 1)
    ones = jnp.ones((2 * BLK, BLK), BF16)
    excl = jnp.concatenate([(r > c).astype(BF16), ones], axis=1)
    incl = jnp.concatenate([(r >= c).astype(BF16), ones], axis=1)
    return excl, incl


def _blk(i):
    return pl.ds(pl.multiple_of(i * BLK, BLK), BLK)


def _alive(carry_m):
    return (jnp.max(carry_m) > DEAD).astype(jnp.int32)


def _more_keys(last, carry):
    return (carry[0] * SB_KB <= last) & (carry[1] > 0)


def _stack_heads(a):
    masks = _head_masks()
    return jnp.concatenate([_keep(masks[0], a), _keep(masks[1], a)], axis=0)


def _unstack_heads(a2):
    return jnp.where(_head_masks()[0], a2[:BLK], a2[BLK:])


def _head_rowsum(a):
    masks = _head_masks()
    return jnp.concatenate([jnp.sum(jnp.where(m, a, 0.0), axis=1, keepdims=True) for m in masks], axis=0)


SB_QB = 2
SB_ROWS = SB_QB * 2 * BLK
SB_KB = 4
PAST_START = 1 << 30


def _sb_rows(ref, i0, cast=None):
    tiles = [ref[_blk(i0 + t), :] for t in range(SB_QB)]
    return jnp.concatenate([_stack_heads(t if cast is None else t.astype(cast)) for t in tiles], axis=0)


_SB_LATER_ROWS = (SB_QB - 1) * 2 * BLK


def _put_rows(full, rows, part):
    return part if rows.start == 0 else jnp.concatenate([full[:rows.start], part], axis=0)


def _sb_scores(q2, k, i, j, carry_m, u_excl):
    r = lax.broadcasted_iota(jnp.int32, (q2.shape[0], BLK), 0)
    row = (r & (BLK - 1)) + ((r >> 8) << 7)
    col = lax.broadcasted_iota(jnp.int32, (q2.shape[0], BLK), 1)
    valid = (jnp.where(j >= 0, j * BLK, PAST_START) + col) < (i * BLK + row)
    z = _dot_nt(q2, k) * SCALE
    sp = jnp.maximum(z, 0.0) + jnp.log(1.0 + jnp.exp(-jnp.abs(z)))
    log_stay = jnp.where(valid, -sp, 0.0)
    log_beta = z - sp
    sums = _dot_split(log_stay, u_excl)
    later = carry_m + sums[:, :BLK]
    w = jnp.where(valid, jnp.exp(log_beta + later), 0.0)
    return valid, log_beta, w, carry_m + sums[:, BLK:]


def _sb_fwd(qkv):
    s = qkv.shape[0]
    nq = s // BLK
    pairs = D_SB // LANES
    col = lambda off: pl.BlockSpec((s, LANES), lambda p: (0, off + p))

    def body(q_ref, k_ref, v_ref, o_ref):
        u_excl, _ = _suffix_matrices()
        zero = jnp.zeros((SB_ROWS, LANES), F32)

        def q_block(ib, _):
            i = ib * SB_QB
            last = i + SB_QB - 1
            q2 = _sb_rows(q_ref, i)

            def trip(jj, carry_m, acc, first):
                for t in range(SB_KB):
                    j = last - jj * SB_KB - t
                    at = _blk(jnp.maximum(j, 0))
                    rows = slice(_SB_LATER_ROWS, SB_ROWS) if first and t == 0 else slice(0, SB_ROWS)
                    base = i + rows.start // (2 * BLK)
                    _, _, w, part = _sb_scores(q2[rows], k_ref[at, :], base, j, carry_m[rows], u_excl)
                    carry_m = _put_rows(carry_m, rows, part)
                    acc = _put_rows(acc, rows, acc[rows] + _dot(w.astype(BF16), v_ref[at, :]))
                return carry_m, acc

            def k_block(carry):
                carry_m, acc = trip(carry[0], carry[2], carry[3], False)
                return carry[0] + 1, _alive(carry_m), carry_m, acc

            carry_m, acc = trip(0, zero, zero, True)
            _, _, _, acc = lax.while_loop(functools.partial(_more_keys, last), k_block,
                                          (jnp.int32(1), _alive(carry_m), carry_m, acc))
            for t in range(SB_QB):
                o_ref[_blk(i + t), :] = _unstack_heads(acc[2 * BLK * t:2 * BLK * (t + 1)])
            return 0

        lax.fori_loop(0, nq // SB_QB, q_block, 0)

    return pl.pallas_call(
        body, name="sb_fwd", grid=(pairs,),
        in_specs=[col(0), col(pairs), col(2 * pairs)],
        out_specs=pl.BlockSpec((s, LANES), lambda p: (0, p)),
        out_shape=jax.ShapeDtypeStruct((s, D_SB), F32),
        compiler_params=_params(48),
    )(qkv, qkv, qkv)


def _sb_bwd(qkv, o_sb, do_sb):
    s = qkv.shape[0]
    nq = s // BLK
    pairs = D_SB // LANES
    col = lambda off: pl.BlockSpec((s, LANES), lambda p, w: (0, off + p))
    own = pl.BlockSpec((s, LANES), lambda p, w: (0, p))

    def body(q_ref, k_ref, v_ref, o_ref, do_ref, out_ref, dq_acc, dk_acc, dv_acc):
        which = pl.program_id(1)

        @pl.when(which == 0)
        def _():
            walk(q_ref, k_ref, v_ref, o_ref, do_ref, dq_acc, dk_acc, dv_acc)
            out_ref[...] = dq_acc[...]

        @pl.when(which == 1)
        def _():
            out_ref[...] = dk_acc[...].astype(BF16)

        @pl.when(which == 2)
        def _():
            out_ref[...] = dv_acc[...].astype(BF16)

    def walk(q_ref, k_ref, v_ref, o_ref, do_ref, dq_ref, dk_acc, dv_acc):
        u_excl, u_incl = _suffix_matrices()
        zero = jnp.zeros((SB_ROWS, LANES), F32)
        dk_acc[...] = jnp.zeros_like(dk_acc)
        dv_acc[...] = jnp.zeros_like(dv_acc)

        def q_block(ib, _):
            i = ib * SB_QB
            last = i + SB_QB - 1
            q2 = _sb_rows(q_ref, i)
            do2 = _sb_rows(do_ref, i, BF16)
            totals = [_head_rowsum(do_ref[_blk(i + t), :].astype(BF16).astype(F32) * o_ref[_blk(i + t), :])
                      for t in range(SB_QB)]
            total = jnp.broadcast_to(jnp.concatenate(totals, axis=0), (SB_ROWS, BLK))

            def trip(jj, carry_m, carry_g, dq, first):
                for t in range(SB_KB):
                    j = last - jj * SB_KB - t
                    at = _blk(jnp.maximum(j, 0))
                    k = k_ref[at, :]
                    rows = slice(_SB_LATER_ROWS, SB_ROWS) if first and t == 0 else slice(0, SB_ROWS)
                    base = i + rows.start // (2 * BLK)
                    valid, log_beta, w, part_m = _sb_scores(q2[rows], k, base, j, carry_m[rows], u_excl)
                    wb = w.astype(BF16)
                    g = wb.astype(F32) * _dot_nt(do2[rows], v_ref[at, :])
                    sums = _dot_split(g, u_incl)
                    before = total[rows] - (carry_g[rows] + sums[:, :BLK])
                    dz = jnp.where(valid, g - jnp.exp(log_beta) * (g + before), 0.0)
                    dzb = (dz * SCALE).astype(BF16)
                    dk_acc[at, :] += _dot_tn(dzb, q2[rows])
                    dv_acc[at, :] += _dot_tn(wb, do2[rows])
                    carry_m = _put_rows(carry_m, rows, part_m)
                    carry_g = _put_rows(carry_g, rows, carry_g[rows] + sums[:, BLK:])
                    dq = _put_rows(dq, rows, dq[rows] + _dot(dzb, k))
                return carry_m, carry_g, dq

            def k_block(carry):
                carry_m, carry_g, dq = trip(carry[0], carry[2], carry[3], carry[4], False)
                return carry[0] + 1, _alive(carry_m), carry_m, carry_g, dq

            carry_m, carry_g, dq = trip(0, zero, zero, zero, True)
            _, _, _, _, dq = lax.while_loop(functools.partial(_more_keys, last), k_block,
                                            (jnp.int32(1), _alive(carry_m), carry_m, carry_g, dq))
            for t in range(SB_QB):
                dq_ref[_blk(i + t), :] = _unstack_heads(dq[2 * BLK * t:2 * BLK * (t + 1)]).astype(BF16)
            return 0

        lax.fori_loop(0, nq // SB_QB, q_block, 0)

    return pl.pallas_call(
        body, name="sb_bwd", grid=(pairs, 3),
        in_specs=[col(0), col(pairs), col(2 * pairs), own, own],
        out_specs=pl.BlockSpec((s, LANES), lambda p, w: (0, w * pairs + p)),
        out_shape=jax.ShapeDtypeStruct((s, 3 * D_SB), BF16),
        scratch_shapes=[pltpu.VMEM((s, LANES), BF16), pltpu.VMEM((s, LANES), F32), pltpu.VMEM((s, LANES), F32)],
        compiler_params=_params(58),
    )(qkv, qkv, qkv, o_sb, do_sb)


DIL_UNROLL = 16


def _band_masks(b):
    row = lax.broadcasted_iota(jnp.int32, (2 * BLK, BLK), 0) & (BLK - 1)
    col = lax.broadcasted_iota(jnp.int32, (2 * BLK, BLK), 1)
    return col <= row, (col - row) >= jnp.where(b > 0, 0, BLK)


def _dil_tiles(qf, kf, vf, d, t, nb):
    c, b = t // nb, t % nb
    start = c + d * BLK * b
    rows = pl.ds(start, BLK, stride=d)
    prev = pl.ds(jnp.where(b > 0, start - d * BLK, start), BLK, stride=d)
    bf = lambda ref, sl: ref[sl, :].astype(BF16)
    return b, rows, prev, _stack_heads(bf(qf, rows)), bf(kf, rows), bf(kf, prev), bf(vf, rows), bf(vf, prev)


def _lanes_of_heads(col2):
    return _unstack_heads(jnp.broadcast_to(col2, (2 * BLK, LANES)))


def _dilated_fwd(qkv):
    s = qkv.shape[0]
    pairs = D_DIL // LANES
    base = (3 * D_SB) // LANES
    col = lambda off: pl.BlockSpec((s, LANES), lambda p: (0, off + p))
    own = pl.BlockSpec((s, LANES), lambda p: (0, p))

    def body(q_ref, k_ref, v_ref, acc_ref, m_ref, qf, kf, vf, l_scr):
        qf[...] = q_ref[...].astype(F32)
        kf[...] = k_ref[...].astype(F32)
        vf[...] = v_ref[...].astype(F32)
        for d in DILATIONS:
            nb = s // (d * BLK)

            def block(t, _):
                b, rows, prev, q2, kc, kp, vc, vp = _dil_tiles(qf, kf, vf, d, t, nb)
                in_cur, in_prev = _band_masks(b)
                zc = jnp.where(in_cur, _dot_nt(q2, kc) * SCALE, NEG)
                zp = jnp.where(in_prev, _dot_nt(q2, kp) * SCALE, NEG)
                m = jnp.maximum(jnp.max(zc, axis=1, keepdims=True), jnp.max(zp, axis=1, keepdims=True))
                pc, pp = jnp.exp(zc - m), jnp.exp(zp - m)
                den = jnp.sum(pc, axis=1, keepdims=True) + jnp.sum(pp, axis=1, keepdims=True)
                acc = _unstack_heads(_dot(pc.astype(BF16), vc) + _dot(pp.astype(BF16), vp))
                m_t, l_t = _lanes_of_heads(m), _lanes_of_heads(den)
                if d == DILATIONS[0]:
                    m_ref[rows, :] = m_t
                    l_scr[rows, :] = l_t
                    acc_ref[rows, :] = acc
                else:
                    m_old = m_ref[rows, :]
                    m_new = jnp.maximum(m_old, m_t)
                    keep, add = jnp.exp(m_old - m_new), jnp.exp(m_t - m_new)
                    m_ref[rows, :] = m_new
                    l_scr[rows, :] = l_scr[rows, :] * keep + l_t * add
                    acc_ref[rows, :] = acc_ref[rows, :] * keep + acc * add
                return 0

            lax.fori_loop(0, s // BLK, block, 0, unroll=DIL_UNROLL)

        def finish(i, _):
            l = l_scr[_blk(i), :]
            acc_ref[_blk(i), :] = acc_ref[_blk(i), :] / l
            m_ref[_blk(i), :] = m_ref[_blk(i), :] + jnp.log(l)
            return 0

        lax.fori_loop(0, s // BLK, finish, 0)

    return pl.pallas_call(
        body, name="dilated_fwd", grid=(pairs,),
        in_specs=[col(base), col(base + pairs), col(base + 2 * pairs)],
        out_specs=[own, own],
        out_shape=[jax.ShapeDtypeStruct((s, D_DIL), F32)] * 2,
        scratch_shapes=[pltpu.VMEM((s, LANES), F32)] * 4,
        compiler_params=_params(56),
    )(qkv, qkv, qkv)


def _stack_lanes(t):
    other = pltpu.roll(t, HEAD_DIM, 1)
    first = _head_masks()[0]
    return jnp.concatenate([jnp.where(first, t, other), jnp.where(first, other, t)], axis=0)


def _dilated_bwd(qkv, delta, lse, dout):
    s = qkv.shape[0]
    pairs = D_DIL // LANES
    base = (3 * D_SB) // LANES
    once = pl.Buffered(1)
    col = lambda off: pl.BlockSpec((s, LANES), lambda p: (0, off + p), pipeline_mode=once)
    own = pl.BlockSpec((s, LANES), lambda p: (0, p), pipeline_mode=once)
    res = pl.BlockSpec((s, LANES), lambda p: (0, p))

    def body(q_ref, k_ref, v_ref, dl_ref, l_ref, do_ref, dq_ref, dk_ref, dv_ref, qf, kf, vf):
        qf[...] = q_ref[...].astype(F32)
        kf[...] = k_ref[...].astype(F32)
        vf[...] = v_ref[...].astype(F32)
        dq_ref[...] = jnp.zeros_like(dq_ref)
        dk_ref[...] = jnp.zeros_like(dk_ref)
        dv_ref[...] = jnp.zeros_like(dv_ref)
        for d in DILATIONS:
            nb = s // (d * BLK)

            def block(t, _):
                b, rows, prev, q2, kc, kp, vc, vp = _dil_tiles(qf, kf, vf, d, t, nb)
                in_cur, in_prev = _band_masks(b)
                do2 = _stack_heads(do_ref[rows, :].astype(BF16))
                delta = _stack_lanes(dl_ref[rows, :])
                lse2 = _stack_lanes(l_ref[rows, :])
                wc = jnp.exp(jnp.where(in_cur, _dot_nt(q2, kc) * SCALE, NEG) - lse2)
                wp = jnp.exp(jnp.where(in_prev, _dot_nt(q2, kp) * SCALE, NEG) - lse2)
                dzc = (wc * (_dot_nt(do2, vc) - delta) * SCALE).astype(BF16)
                dzp = (wp * (_dot_nt(do2, vp) - delta) * SCALE).astype(BF16)
                dq_ref[rows, :] += _unstack_heads(_dot(dzc, kc) + _dot(dzp, kp))
                dk_ref[rows, :] += _dot_tn(dzc, q2)
                dk_ref[prev, :] += _dot_tn(dzp, q2)
                dv_ref[rows, :] += _dot_tn(wc.astype(BF16), do2)
                dv_ref[prev, :] += _dot_tn(wp.astype(BF16), do2)
                return 0

            lax.fori_loop(0, s // BLK, block, 0, unroll=DIL_UNROLL)

    return pl.pallas_call(
        body, name="dilated_bwd", grid=(pairs,),
        in_specs=[col(base), col(base + pairs), col(base + 2 * pairs), own, own, own],
        out_specs=[res, res, res],
        out_shape=[jax.ShapeDtypeStruct((s, D_DIL), F32)] * 3,
        scratch_shapes=[pltpu.VMEM((s, LANES), F32)] * 3,
        compiler_params=_params(60),
    )(qkv, qkv, qkv, delta, lse, dout)


def _dilated_finish(grads, cos, sin):
    s = grads[0].shape[0]
    spec = pl.BlockSpec((TM, D_DIL), lambda i: (i, 0))
    tab = pl.BlockSpec((TM, LANES), lambda i: (i, 0))

    def body(dq_ref, dk_ref, dv_ref, c_ref, s_ref, out_ref):
        for t, (src, rotated) in enumerate(((dq_ref, True), (dk_ref, True), (dv_ref, False))):
            for c in range(D_DIL // LANES):
                piece = src[:, c * LANES:(c + 1) * LANES]
                at = t * D_DIL + c * LANES
                out_ref[:, at:at + LANES] = (_rotate(piece, c_ref[...], -s_ref[...]) if rotated else piece).astype(BF16)

    return pl.pallas_call(
        body, name="dilated_finish", grid=(s // TM,),
        in_specs=[spec] * 3 + [tab, tab], out_specs=pl.BlockSpec((TM, 3 * D_DIL), lambda i: (i, 0)),
        out_shape=jax.ShapeDtypeStruct((s, 3 * D_DIL), BF16),
        compiler_params=_params(32),
    )(*grads, cos, sin)


def _place():
    x, y, c = lax.axis_index("x"), lax.axis_index("y"), lax.axis_index("c")
    return x, y, c, 2 * x + y


def _chip(k, c):
    return (k >> 1, k & 1, c)


def _half(ref, h):
    n = ref.shape[0] // 2
    return ref.at[pl.ds(h * n, n)]


class _BackgroundGather:
    def __init__(self, ins, outs, scratch):
        n = self.n = len(ins)
        self.ins, self.outs = ins, outs
        self.mine, self.landed, self.passed = scratch[0:3 * n:3], scratch[1:3 * n:3], scratch[2:3 * n:3]
        self.send_sem, self.recv_sem, self.local_sem = scratch[3 * n:3 * n + 3]
        x, y, self.c, self.k = _place()
        self.sibling = (x, y, 1 - self.c)

    @staticmethod
    def scratch_shapes(shards):
        shapes = []
        for a in shards:
            half = (N_CHIP - 1, a.shape[0] // 2, a.shape[1])
            shapes += [pltpu.VMEM(a.shape, a.dtype), pltpu.VMEM(half, a.dtype), pltpu.VMEM(half, a.dtype)]
        n = len(shards)
        return shapes + [pltpu.SemaphoreType.DMA((6 * n,)), pltpu.SemaphoreType.DMA((6 * n,)),
                         pltpu.SemaphoreType.DMA((8 * n,))]

    @staticmethod
    def out_shapes(shards):
        return [jax.ShapeDtypeStruct((N_CHIP,) + a.shape, a.dtype) for a in shards]

    def _remote(self, a, slot, src, dst, to):
        return pltpu.make_async_remote_copy(src_ref=src, dst_ref=dst, send_sem=self.send_sem.at[6 * a + slot],
                                            recv_sem=self.recv_sem.at[6 * a + slot], device_id=to, device_id_type=MESH)

    def _local(self, a, slot, src, dst):
        return pltpu.make_async_copy(src, dst, self.local_sem.at[8 * a + slot])

    def _ici(self, a, j):
        return self._remote(a, j - 1, _half(self.mine[a], self.c), self.landed[a].at[j - 1], _chip(self.k ^ j, self.c))

    def _to_sibling(self, a, j):
        return self._remote(a, 2 + j, self.landed[a].at[j - 1], self.passed[a].at[j - 1], self.sibling)

    def _own(self, a):
        return self._local(a, 0, self.ins[a], self.outs[a].at[self.k])

    def _load(self, a):
        return self._local(a, 1, self.ins[a], self.mine[a])

    def _store_landed(self, a, j):
        return self._local(a, 1 + j, self.landed[a].at[j - 1], _half(self.outs[a].at[self.k ^ j], self.c))

    def _store_passed(self, a, j):
        return self._local(a, 4 + j, self.passed[a].at[j - 1], _half(self.outs[a].at[self.k ^ j], 1 - self.c))

    def start(self):
        for a in range(self.n):
            self._own(a).start()
            self._load(a).start()
        for a in range(self.n):
            self._load(a).wait()
            for j in range(1, N_CHIP):
                self._ici(a, j).start()

    def forward(self):
        for j in range(1, N_CHIP):
            for a in range(self.n):
                self._ici(a, j).wait_recv()
                self._to_sibling(a, j).start()
                self._store_landed(a, j).start()

    def finish(self):
        for j in range(1, N_CHIP):
            for a in range(self.n):
                self._to_sibling(a, j).wait_recv()
                self._store_passed(a, j).start()
        for a in range(self.n):
            for j in range(1, N_CHIP):
                self._ici(a, j).wait_send()
                self._to_sibling(a, j).wait_send()
                self._store_landed(a, j).wait()
                self._store_passed(a, j).wait()
            self._own(a).wait()


def _all_gather(shards):
    n = len(shards)
    any_spec = pl.BlockSpec(memory_space=pl.ANY)

    def body(*refs):
        gather = _BackgroundGather(refs[:n], refs[n:2 * n], refs[2 * n:])
        gather.start()
        gather.forward()
        gather.finish()

    return pl.pallas_call(
        body, name="weights_all_gather",
        in_specs=[any_spec] * n, out_specs=[any_spec] * n,
        out_shape=_BackgroundGather.out_shapes(shards),
        scratch_shapes=_BackgroundGather.scratch_shapes(shards),
        compiler_params=_params(32),
    )(*shards)


def _reduce_scatter(g, core, name):
    n, r, c = g.shape
    hr = r // 2
    once = pl.Buffered(1)
    in_specs = [pl.BlockSpec((n, hr, c), lambda i, core_ref: (0, core_ref[0], 0), pipeline_mode=once),
                pl.BlockSpec((n, hr, c), lambda i, core_ref: (0, 1 - core_ref[0], 0), pipeline_mode=once)]

    def body(core_ref, mine_ref, other_ref, out_ref, from_core, sums_bf, from_chips, done, from_core2, send_sem, recv_sem):
        x, y, cc, k = _place()
        sibling = (x, y, 1 - cc)

        def copy(slot, src, dst, to):
            return pltpu.make_async_remote_copy(src_ref=src, dst_ref=dst, send_sem=send_sem.at[slot],
                                                recv_sem=recv_sem.at[slot], device_id=to, device_id_type=MESH)

        from_sibling = [copy(j, other_ref.at[k ^ j], from_core.at[k ^ j], sibling) for j in range(N_CHIP)]
        for j in (1, 2, 3, 0):
            from_sibling[j].start()
        sends = []
        for j in range(1, N_CHIP):
            from_sibling[j].wait()
            sums_bf[j - 1] = (mine_ref[k ^ j] + from_core[k ^ j]).astype(BF16)
            cp = copy(N_CHIP - 1 + j, sums_bf.at[j - 1], from_chips.at[j - 1], _chip(k ^ j, cc))
            cp.start()
            sends.append(cp)
        from_sibling[0].wait()
        red = mine_ref[k] + from_core[k]
        for j in range(1, N_CHIP):
            sends[j - 1].wait()
            red = red + from_chips[j - 1].astype(F32)
        done[...] = red
        last = copy(2 * N_CHIP - 1, done, from_core2, sibling)
        last.start()
        last.wait()
        row0 = pl.multiple_of(cc * hr, 8)
        row1 = pl.multiple_of((1 - cc) * hr, 8)
        out_ref[pl.ds(row0, hr), :] = red
        out_ref[pl.ds(row1, hr), :] = from_core2[...]

    grid_spec = pltpu.PrefetchScalarGridSpec(
        num_scalar_prefetch=1, grid=(1,), in_specs=in_specs,
        out_specs=pl.BlockSpec((r, c), lambda i, core_ref: (0, 0)),
        scratch_shapes=[pltpu.VMEM((n, hr, c), F32), pltpu.VMEM((N_CHIP - 1, hr, c), BF16),
                        pltpu.VMEM((N_CHIP - 1, hr, c), BF16), pltpu.VMEM((hr, c), F32), pltpu.VMEM((hr, c), F32),
                        pltpu.SemaphoreType.DMA((2 * N_CHIP,)), pltpu.SemaphoreType.DMA((2 * N_CHIP,))])
    return pl.pallas_call(
        body, name=name, grid_spec=grid_spec, out_shape=jax.ShapeDtypeStruct((r, c), F32),
        compiler_params=_params(56),
    )(core, g, g)


def _elementwise(fn, name, ins, n_out, rows):
    total, cols = ins[0].shape
    spec = pl.BlockSpec((rows, cols), lambda i: (i, 0))

    def body(*refs):
        res = fn(*[r[...] for r in refs[:len(ins)]])
        for o, v in zip(refs[len(ins):], res):
            o[...] = v

    return pl.pallas_call(
        body, name=name, grid=(total // rows,),
        in_specs=[spec] * len(ins), out_specs=[spec] * n_out,
        out_shape=[jax.ShapeDtypeStruct((total, cols), F32)] * n_out,
        compiler_params=_params(48),
    )(*ins)


def _adamw(w, g, m, v):
    m = ADAM_B1 * m + (1.0 - ADAM_B1) * g
    v = ADAM_B2 * v + (1.0 - ADAM_B2) * (g * g)
    m_hat = m / (1.0 - ADAM_B1 ** ADAM_STEP)
    v_hat = v / (1.0 - ADAM_B2 ** ADAM_STEP)
    delta = -ADAM_LR * (m_hat / (jnp.sqrt(v_hat) + ADAM_EPS) + ADAM_WD * w)
    return delta, m, v


def _reduce_and_update(grads, weights, moms, vels):
    core = lax.axis_index("c").astype(jnp.int32).reshape(1)
    full = [_reduce_scatter(g, core, f"grads_reduce_scatter_{a}") for a, g in enumerate(grads)]
    out = []
    for a, (g, w, m, v) in enumerate(zip(full, weights, moms, vels)):
        rows = g.shape[0] // 2
        out.append((g,) + tuple(_elementwise(lambda gg, ww, mm, vv: _adamw(ww, gg, mm, vv), f"adamw_{a}", [g, w, m, v], 3, rows)))
    return out


def _reduce_vectors(part, w, m, v):
    n_dev = 8

    def body(p_ref, w_ref, m_ref, v_ref, g_ref, d_ref, nm_ref, nv_ref, buf, send_sem, recv_sem):
        x, y, c, _ = _place()
        me = 4 * x + 2 * y + c
        buf[me] = p_ref[...]
        sends = []
        for off in range(1, n_dev):
            peer = me ^ off
            cp = pltpu.make_async_remote_copy(src_ref=p_ref, dst_ref=buf.at[me], send_sem=send_sem.at[off - 1],
                                              recv_sem=recv_sem.at[off - 1], device_id=(peer >> 2, (peer >> 1) & 1, peer & 1),
                                              device_id_type=MESH)
            cp.start()
            sends.append(cp)
        for off in range(1, n_dev):
            peer = me ^ off
            pltpu.make_async_remote_copy(src_ref=p_ref, dst_ref=buf.at[peer], send_sem=send_sem.at[off - 1],
                                         recv_sem=recv_sem.at[off - 1], device_id=(peer >> 2, (peer >> 1) & 1, peer & 1),
                                         device_id_type=MESH).wait_recv()
        for cp in sends:
            cp.wait_send()
        g = buf[0]
        for d in range(1, n_dev):
            g = g + buf[d]
        g_ref[...] = g
        delta, nm, nv = _adamw(w_ref[...], g, m_ref[...], v_ref[...])
        d_ref[...] = delta
        nm_ref[...] = nm
        nv_ref[...] = nv

    vm = pl.BlockSpec(memory_space=pltpu.VMEM)
    return pl.pallas_call(
        body, name="gains_all_reduce",
        in_specs=[vm] * 4, out_specs=[vm] * 4,
        out_shape=[jax.ShapeDtypeStruct(part.shape, F32)] * 4,
        scratch_shapes=[pltpu.VMEM((n_dev,) + part.shape, F32), pltpu.SemaphoreType.DMA((n_dev - 1,)),
                        pltpu.SemaphoreType.DMA((n_dev - 1,))],
    )(part, w, m, v)


def _pad_row(a):
    a = a.reshape(1, -1)
    return jnp.pad(a, ((0, 0), (0, D_MODEL - a.shape[1])))


def kernel(x, ffn1_norm, ffn1_w_gate, ffn1_w_up, ffn1_w_down, mix_norm, w_in, sb_out_norm, dil_out_norm, w_out, ffn2_norm, ffn2_w_gate, ffn2_w_up, ffn2_w_down, final_norm, loss_target, m_ffn1_norm, m_ffn1_w_gate, m_ffn1_w_up, m_ffn1_w_down, m_mix_norm, m_w_in, m_sb_out_norm, m_dil_out_norm, m_w_out, m_ffn2_norm, m_ffn2_w_gate, m_ffn2_w_up, m_ffn2_w_down, m_final_norm, v_ffn1_norm, v_ffn1_w_gate, v_ffn1_w_up, v_ffn1_w_down, v_mix_norm, v_w_in, v_sb_out_norm, v_dil_out_norm, v_w_out, v_ffn2_norm, v_ffn2_w_gate, v_ffn2_w_up, v_ffn2_w_down, v_final_norm):
    x = x[0]
    target = loss_target[0]
    s = x.shape[0]
    gf = final_norm.reshape(1, D_MODEL)
    cos, sin = _rope_tables(s)

    flip = lambda a: a[0].T
    shard = lambda w: w[0].astype(BF16)
    shard_t = lambda w: flip(w).astype(BF16)
    wg1, wu1, wd1 = _all_gather([shard_t(ffn1_w_gate), shard_t(ffn1_w_up), shard(ffn1_w_down)])

    x1, hm, saved1, (win, wout, wd2) = _ffn1_fwd(x, ffn1_norm, mix_norm, (wg1, wu1), wd1,
                                                 [shard(w_in), shard(w_out), shard(ffn2_w_down)])
    wout = wout.reshape(D_MODEL, D_MODEL)
    qkv, (wg2, wu2) = _proj_fwd(hm, win, cos, sin, [shard_t(ffn2_w_gate), shard_t(ffn2_w_up)])
    o_sb = _sb_fwd(qkv)
    o_dl, lse = _dilated_fwd(qkv)
    x2 = _outproj_fwd(o_sb, o_dl, sb_out_norm, dil_out_norm, x1, wout)
    dx3, st_final, saved2 = _ffn2_fwd_loss(x2, ffn2_norm, gf, target, (wg2, wu2), wd2)

    dx2, dwg2, dwu2, dwd2, st_ffn2 = _ffn_bwd(x2, ffn2_norm, dx3, saved2, (wg2, wu2), wd2, 1)
    do_sb, do_dl, delta_dl, dwout, st_out = _outproj_bwd(dx2, o_sb, o_dl, sb_out_norm, dil_out_norm, wout)
    dqkv_sb = _sb_bwd(qkv, o_sb, do_sb)
    dqkv_dl = _dilated_finish(_dilated_bwd(qkv, delta_dl, lse, do_dl), cos, sin)
    dx1, dwin, st_mix = _proj_bwd(x1, mix_norm, dqkv_sb, dqkv_dl, win, dx2)
    grad_x, dwg1, dwu1, dwd1, st_ffn1 = _ffn_bwd(x, ffn1_norm, dx1, saved1, (wg1, wu1), wd1, 0)

    names = ["ffn1_w_gate", "ffn1_w_up", "ffn1_w_down", "w_in", "w_out", "ffn2_w_gate", "ffn2_w_up", "ffn2_w_down"]
    grads = [dwg1, dwu1, dwd1, dwin, dwout.reshape(N_CHIP, OUTB, D_MODEL), dwg2, dwu2, dwd2]
    flipped = {"ffn1_w_gate", "ffn1_w_up", "ffn2_w_gate", "ffn2_w_up"}
    place = lambda n, a: flip(a) if n in flipped else a[0]
    weights = [place(n, a) for n, a in zip(names, [ffn1_w_gate, ffn1_w_up, ffn1_w_down, w_in, w_out, ffn2_w_gate, ffn2_w_up, ffn2_w_down])]
    moms = [place(n, a) for n, a in zip(names, [m_ffn1_w_gate, m_ffn1_w_up, m_ffn1_w_down, m_w_in, m_w_out, m_ffn2_w_gate, m_ffn2_w_up, m_ffn2_w_down])]
    vels = [place(n, a) for n, a in zip(names, [v_ffn1_w_gate, v_ffn1_w_up, v_ffn1_w_down, v_w_in, v_w_out, v_ffn2_w_gate, v_ffn2_w_up, v_ffn2_w_down])]
    mats = {n: tuple((t.T if n in flipped else t)[None] for t in r)
            for n, r in zip(names, _reduce_and_update(grads, weights, moms, vels))}

    vec_names = ["ffn1_norm", "mix_norm", "sb_out_norm", "dil_out_norm", "ffn2_norm", "final_norm"]
    part = jnp.concatenate([st_ffn1[0:1], st_mix[0:1], _pad_row(st_out[0]), _pad_row(st_out[1]), st_ffn2[0:1],
                            st_final[0:1], st_final[1:2], jnp.zeros((1, D_MODEL), F32)], axis=0)
    pack = lambda arrs: jnp.concatenate([_pad_row(a) for a in arrs] + [jnp.zeros((2, D_MODEL), F32)], axis=0)
    g_vec, d_vec, m_vec, v_vec = _reduce_vectors(
        part,
        pack([ffn1_norm, mix_norm, sb_out_norm, dil_out_norm, ffn2_norm, final_norm]),
        pack([m_ffn1_norm, m_mix_norm, m_sb_out_norm, m_dil_out_norm, m_ffn2_norm, m_final_norm]),
        pack([v_ffn1_norm, v_mix_norm, v_sb_out_norm, v_dil_out_norm, v_ffn2_norm, v_final_norm]))
    like = {"ffn1_norm": ffn1_norm, "mix_norm": mix_norm, "sb_out_norm": sb_out_norm, "dil_out_norm": dil_out_norm,
            "ffn2_norm": ffn2_norm, "final_norm": final_norm}
    vecs = {n: tuple(t[i, :like[n].size].reshape(like[n].shape) for t in (g_vec, d_vec, m_vec, v_vec))
            for i, n in enumerate(vec_names)}
    loss = 0.5 * jnp.sum(g_vec[6]) / D_MODEL

    order = ["ffn1_norm", "ffn1_w_gate", "ffn1_w_up", "ffn1_w_down", "mix_norm", "w_in", "sb_out_norm", "dil_out_norm",
             "w_out", "ffn2_norm", "ffn2_w_gate", "ffn2_w_up", "ffn2_w_down", "final_norm"]
    both = {**mats, **vecs}
    return (loss, grad_x[None], *[both[n][0] for n in order], *[both[n][1] for n in order],
            *[both[n][2] for n in order], *[both[n][3] for n in order])
```
